```python
import jax, jax.numpy as jnp
from jax import lax
import numpy as np

D_MODEL = 1024
BATCH = 8
SEQ = 4096
DEPTH = 4

N_META = 16
EPS = 1e-6
SSD_D_INNER = 2 * D_MODEL
SSD_HEAD_DIM = 64
SSD_HEADS = SSD_D_INNER // SSD_HEAD_DIM
SSD_GROUPS = 8
SSD_HPG = SSD_HEADS // SSD_GROUPS
SSD_STATE = 128
SSD_CONV = 4
SSD_CHUNK = 128
SSD_CONV_DIM = SSD_D_INNER + 2 * SSD_GROUPS * SSD_STATE
SSD_IN_DIM = SSD_D_INNER + SSD_CONV_DIM + SSD_HEADS
MLA_HEADS = 16
MLA_NOPE = 64
MLA_ROPE = 32
MLA_V = 64
MLA_QK = MLA_NOPE + MLA_ROPE
MLA_Q_RANK = 384
MLA_KV_RANK = 256
MLA_IN_DIM = MLA_Q_RANK + MLA_KV_RANK + MLA_ROPE
ROPE_THETA = 10000.0
ATTN_BLOCK = 128
D_FF = 4 * D_MODEL
N_SSD_LAYERS = (DEPTH + 1) // 2
N_MLA_LAYERS = DEPTH // 2

kernel_name = "hybrid_ssd_mla_meta_trunk"


def rms_norm(x, gain):
    xf = x.astype(jnp.float32)
    y = xf * lax.rsqrt(jnp.mean(xf * xf, axis=-1, keepdims=True) + EPS)
    return (y * gain.astype(jnp.float32)).astype(x.dtype)


def causal_depthwise_conv(u, w, b):
    out = lax.conv_general_dilated(
        u, w[:, None, :].astype(u.dtype), window_strides=(1,),
        padding=[(SSD_CONV - 1, 0)], dimension_numbers=('NWC', 'WIO', 'NWC'),
        feature_group_count=u.shape[-1])
    return out + b.astype(u.dtype)


def ssd_mixer(h, w_in, conv_w, conv_b, dt_bias, a_log, d_skip, norm_g, w_out):
    f32 = jnp.float32
    bsz, L, _ = h.shape
    zxbcdt = h @ w_in
    z, xbc, dt = jnp.split(zxbcdt, [SSD_D_INNER, SSD_D_INNER + SSD_CONV_DIM], axis=-1)
    xbc = jax.nn.silu(causal_depthwise_conv(xbc, conv_w, conv_b))
    xs, b_in, c_in = jnp.split(xbc, [SSD_D_INNER, SSD_D_INNER + SSD_GROUPS * SSD_STATE], axis=-1)
    dt = jax.nn.softplus(dt.astype(f32) + dt_bias.astype(f32))
    a = -jnp.exp(a_log.astype(f32))

    pad = (-L) % SSD_CHUNK
    n_chunks = (L + pad) // SSD_CHUNK

    def front_pad(t):
        return jnp.pad(t.astype(f32), [(0, 0), (pad, 0)] + [(0, 0)] * (t.ndim - 2))

    x_c = front_pad(xs).reshape(bsz, n_chunks, SSD_CHUNK, SSD_GROUPS, SSD_HPG, SSD_HEAD_DIM)
    b_c = front_pad(b_in).reshape(bsz, n_chunks, SSD_CHUNK, SSD_GROUPS, SSD_STATE)
    c_c = front_pad(c_in).reshape(bsz, n_chunks, SSD_CHUNK, SSD_GROUPS, SSD_STATE)
    dt_c = front_pad(dt).reshape(bsz, n_chunks, SSD_CHUNK, SSD_GROUPS, SSD_HPG)
    xdt = x_c * dt_c[..., None]
    a_dt = (dt_c * a.reshape(SSD_GROUPS, SSD_HPG)).transpose(0, 1, 3, 4, 2)
    a_cs = jnp.cumsum(a_dt, axis=-1)

    idx = jnp.arange(SSD_CHUNK)
    causal = idx[:, None] >= idx[None, :]
    decay = jnp.exp(jnp.where(causal, a_cs[..., :, None] - a_cs[..., None, :], -jnp.inf))
    cb = jnp.einsum('bclgn,bcsgn->bcgls', c_c, b_c)
    y_diag = jnp.einsum('bcgjls,bcsgjp->bclgjp', cb[:, :, :, None] * decay, xdt)

    decay_to_end = jnp.exp(a_cs[..., -1:] - a_cs).transpose(0, 1, 4, 2, 3)
    states = jnp.einsum('bclgn,bclgjp->bcgjpn', b_c, xdt * decay_to_end[..., None])
    chunk_decay = jnp.exp(a_cs[..., -1])

    def step(carry, inp):
        st, dec = inp
        return carry * dec[..., None, None] + st, carry

    init = jnp.zeros((bsz, SSD_GROUPS, SSD_HPG, SSD_HEAD_DIM, SSD_STATE), f32)
    _, prev = lax.scan(step, init, (jnp.moveaxis(states, 1, 0), jnp.moveaxis(chunk_decay, 1, 0)))
    prev = jnp.moveaxis(prev, 0, 1)
    decay_from_start = jnp.exp(a_cs).transpose(0, 1, 4, 2, 3)
    y_off = jnp.einsum('bclgn,bcgjpn->bclgjp', c_c, prev) * decay_from_start[..., None]

    y = (y_diag + y_off).reshape(bsz, n_chunks * SSD_CHUNK, SSD_D_INNER)[:, pad:]
    y = y + xs.astype(f32) * jnp.repeat(d_skip.astype(f32), SSD_HEAD_DIM)
    g = (y * jax.nn.silu(z.astype(f32))).reshape(bsz, L, SSD_GROUPS, SSD_D_INNER // SSD_GROUPS)
    g = g * lax.rsqrt(jnp.mean(g * g, axis=-1, keepdims=True) + EPS)
    g = g.reshape(bsz, L, SSD_D_INNER) * norm_g.astype(f32)
    return g.astype(h.dtype) @ w_out


def rope_tables(L):
    inv = 1.0 / (ROPE_THETA ** (jnp.arange(0, MLA_ROPE, 2, dtype=jnp.float32) / MLA_ROPE))
    ang = jnp.arange(L, dtype=jnp.float32)[:, None] * inv[None, :]
    return jnp.cos(ang)[None, :, None, :], jnp.sin(ang)[None, :, None, :]


def apply_rope(t, cos, sin):
    t1, t2 = jnp.split(t, 2, axis=-1)
    cos = cos.astype(t.dtype)
    sin = sin.astype(t.dtype)
    return jnp.concatenate([t1 * cos - t2 * sin, t1 * sin + t2 * cos], axis=-1)


def mla_mixer(h, w_in, q_a_g, w_q_b, kv_a_g, w_kv_b, q_norm_g, k_norm_g, w_out):
    bsz, L, _ = h.shape
    q_lat, kv_lat, k_pe = jnp.split(h @ w_in, [MLA_Q_RANK, MLA_Q_RANK + MLA_KV_RANK], axis=-1)
    q = (rms_norm(q_lat, q_a_g) @ w_q_b).reshape(bsz, L, MLA_HEADS, MLA_QK)
    kv = (rms_norm(kv_lat, kv_a_g) @ w_kv_b).reshape(bsz, L, MLA_HEADS, MLA_NOPE + MLA_V)
    k_nope, v = jnp.split(kv, [MLA_NOPE], axis=-1)
    k = jnp.concatenate(
        [k_nope, jnp.broadcast_to(k_pe[:, :, None, :], (bsz, L, MLA_HEADS, MLA_ROPE))], axis=-1)
    q = rms_norm(q, q_norm_g)
    k = rms_norm(k, k_norm_g)
    cos, sin = rope_tables(L)
    q = jnp.concatenate([q[..., :MLA_NOPE], apply_rope(q[..., MLA_NOPE:], cos, sin)], axis=-1)
    k = jnp.concatenate([k[..., :MLA_NOPE], apply_rope(k[..., MLA_NOPE:], cos, sin)], axis=-1)
    scale = MLA_QK ** -0.5

    blocks = [(0, N_META)] + [(s, min(s + ATTN_BLOCK, L)) for s in range(N_META, L, ATTN_BLOCK)]
    outs = []
    for s, e in blocks:
        sc = jnp.einsum('bqhd,bkhd->bhqk', q[:, s:e], k[:, :e]).astype(jnp.float32) * scale
        mask = jnp.arange(e)[None, :] <= jnp.arange(s, e)[:, None]
        p = jax.nn.softmax(jnp.where(mask, sc, -jnp.inf), axis=-1).astype(v.dtype)
        outs.append(jnp.einsum('bhqk,bkhd->bqhd', p, v[:, :e]))
    o = jnp.concatenate(outs, axis=1).reshape(bsz, L, MLA_HEADS * MLA_V)
    return o @ w_out


def sqrelu_mlp(h, w_up, w_down):
    return jnp.square(jax.nn.relu(h @ w_up)) @ w_down


def _fwd_setup_inputs(seed: int = 0) -> dict:
    key = jax.random.key(seed)
    ks = jax.random.split(key, 24)
    f32 = jnp.float32

    def nrm(k, shape, fan_in):
        return jax.random.normal(k, shape, f32) * (fan_in ** -0.5)

    def gain(k, shape):
        return 1.0 + 0.02 * jax.random.normal(k, shape, f32)

    ns, nm = N_SSD_LAYERS, N_MLA_LAYERS
    dt0 = jnp.exp(jax.random.uniform(ks[7], (ns, SSD_HEADS), f32, np.log(1e-3), np.log(1e-1)))
    return {
        "x": jax.random.normal(ks[0], (BATCH, SEQ, D_MODEL), f32),
        "meta_tokens": jax.random.normal(ks[1], (N_META, D_MODEL), f32),
        "ln_mix": gain(ks[2], (DEPTH, D_MODEL)),
        "ln_mlp": gain(ks[3], (DEPTH, D_MODEL)),
        "ssd_w_in": nrm(ks[4], (ns, D_MODEL, SSD_IN_DIM), D_MODEL),
        "ssd_conv_w": nrm(ks[5], (ns, SSD_CONV, SSD_CONV_DIM), SSD_CONV),
        "ssd_conv_b": 0.02 * jax.random.normal(ks[6], (ns, SSD_CONV_DIM), f32),
        "ssd_dt_bias": dt0 + jnp.log(-jnp.expm1(-dt0)),
        "ssd_a_log": jnp.log(jax.random.uniform(ks[8], (ns, SSD_HEADS), f32, 1.0, 16.0)),
        "ssd_d": 1.0 + 0.1 * jax.random.normal(ks[9], (ns, SSD_HEADS), f32),
        "ssd_norm": gain(ks[10], (ns, SSD_D_INNER)),
        "ssd_w_out": nrm(ks[11], (ns, SSD_D_INNER, D_MODEL), SSD_D_INNER),
        "mla_w_in": nrm(ks[12], (nm, D_MODEL, MLA_IN_DIM), D_MODEL),
        "mla_q_a_norm": gain(ks[13], (nm, MLA_Q_RANK)),
        "mla_w_q_b": nrm(ks[14], (nm, MLA_Q_RANK, MLA_HEADS * MLA_QK), MLA_Q_RANK),
        "mla_kv_a_norm": gain(ks[15], (nm, MLA_KV_RANK)),
        "mla_w_kv_b": nrm(ks[16], (nm, MLA_KV_RANK, MLA_HEADS * (MLA_NOPE + MLA_V)), MLA_KV_RANK),
        "mla_q_norm": gain(ks[17], (nm, MLA_QK)),
        "mla_k_norm": gain(ks[18], (nm, MLA_QK)),
        "mla_w_out": nrm(ks[19], (nm, MLA_HEADS * MLA_V, D_MODEL), MLA_HEADS * MLA_V),
        "mlp_w_up": nrm(ks[20], (DEPTH, D_MODEL, D_FF), D_MODEL),
        "mlp_w_down": nrm(ks[21], (DEPTH, D_FF, D_MODEL), D_FF),
    }


def _fwd_reference(x, meta_tokens, ln_mix, ln_mlp, ssd_w_in, ssd_conv_w, ssd_conv_b, ssd_dt_bias,
              ssd_a_log, ssd_d, ssd_norm, ssd_w_out, mla_w_in, mla_q_a_norm, mla_w_q_b,
              mla_kv_a_norm, mla_w_kv_b, mla_q_norm, mla_k_norm, mla_w_out, mlp_w_up, mlp_w_down):
    bsz = x.shape[0]
    meta = jnp.broadcast_to(meta_tokens[None].astype(x.dtype), (bsz, N_META, D_MODEL))
    h = jnp.concatenate([meta, x], axis=1)
    for i in range(DEPTH):
        j = i // 2
        hn = rms_norm(h, ln_mix[i])
        if i % 2 == 0:
            h = h + ssd_mixer(hn, ssd_w_in[j], ssd_conv_w[j], ssd_conv_b[j], ssd_dt_bias[j],
                              ssd_a_log[j], ssd_d[j], ssd_norm[j], ssd_w_out[j])
        else:
            h = h + mla_mixer(hn, mla_w_in[j], mla_q_a_norm[j], mla_w_q_b[j], mla_kv_a_norm[j],
                              mla_w_kv_b[j], mla_q_norm[j], mla_k_norm[j], mla_w_out[j])
        h = h + sqrelu_mlp(rms_norm(h, ln_mlp[i]), mlp_w_up[i], mlp_w_down[i])
    return h[:, N_META:]


import jax as _jax
import jax.numpy as _jnp

TWIN_FORMAT = 'train_step'
FWD_PARAMS = ['x', 'meta_tokens', 'ln_mix', 'ln_mlp', 'ssd_w_in', 'ssd_conv_w', 'ssd_conv_b', 'ssd_dt_bias', 'ssd_a_log', 'ssd_d', 'ssd_norm', 'ssd_w_out', 'mla_w_in', 'mla_q_a_norm', 'mla_w_q_b', 'mla_kv_a_norm', 'mla_w_kv_b', 'mla_q_norm', 'mla_k_norm', 'mla_w_out', 'mlp_w_up', 'mlp_w_down']
TWIN_WEIGHTS = ['meta_tokens', 'ln_mix', 'ln_mlp', 'ssd_w_in', 'ssd_conv_w', 'ssd_conv_b', 'ssd_dt_bias', 'ssd_a_log', 'ssd_d', 'ssd_norm', 'ssd_w_out', 'mla_w_in', 'mla_q_a_norm', 'mla_w_q_b', 'mla_kv_a_norm', 'mla_w_kv_b', 'mla_q_norm', 'mla_k_norm', 'mla_w_out', 'mlp_w_up', 'mlp_w_down']
TWIN_DIFF_INPUT = 'x'
TWIN_INPUTS = ['x', 'meta_tokens', 'ln_mix', 'ln_mlp', 'ssd_w_in', 'ssd_conv_w', 'ssd_conv_b', 'ssd_dt_bias', 'ssd_a_log', 'ssd_d', 'ssd_norm', 'ssd_w_out', 'mla_w_in', 'mla_q_a_norm', 'mla_w_q_b', 'mla_kv_a_norm', 'mla_w_kv_b', 'mla_q_norm', 'mla_k_norm', 'mla_w_out', 'mlp_w_up', 'mlp_w_down', 'loss_target', 'm_meta_tokens', 'm_ln_mix', 'm_ln_mlp', 'm_ssd_w_in', 'm_ssd_conv_w', 'm_ssd_conv_b', 'm_ssd_dt_bias', 'm_ssd_a_log', 'm_ssd_d', 'm_ssd_norm', 'm_ssd_w_out', 'm_mla_w_in', 'm_mla_q_a_norm', 'm_mla_w_q_b', 'm_mla_kv_a_norm', 'm_mla_w_kv_b', 'm_mla_q_norm', 'm_mla_k_norm', 'm_mla_w_out', 'm_mlp_w_up', 'm_mlp_w_down', 'v_meta_tokens', 'v_ln_mix', 'v_ln_mlp', 'v_ssd_w_in', 'v_ssd_conv_w', 'v_ssd_conv_b', 'v_ssd_dt_bias', 'v_ssd_a_log', 'v_ssd_d', 'v_ssd_norm', 'v_ssd_w_out', 'v_mla_w_in', 'v_mla_q_a_norm', 'v_mla_w_q_b', 'v_mla_kv_a_norm', 'v_mla_w_kv_b', 'v_mla_q_norm', 'v_mla_k_norm', 'v_mla_w_out', 'v_mlp_w_up', 'v_mlp_w_down']
TWIN_OUTPUTS = ['loss', 'grad_x', 'grad_meta_tokens', 'grad_ln_mix', 'grad_ln_mlp', 'grad_ssd_w_in', 'grad_ssd_conv_w', 'grad_ssd_conv_b', 'grad_ssd_dt_bias', 'grad_ssd_a_log', 'grad_ssd_d', 'grad_ssd_norm', 'grad_ssd_w_out', 'grad_mla_w_in', 'grad_mla_q_a_norm', 'grad_mla_w_q_b', 'grad_mla_kv_a_norm', 'grad_mla_w_kv_b', 'grad_mla_q_norm', 'grad_mla_k_norm', 'grad_mla_w_out', 'grad_mlp_w_up', 'grad_mlp_w_down', 'delta_meta_tokens', 'delta_ln_mix', 'delta_ln_mlp', 'delta_ssd_w_in', 'delta_ssd_conv_w', 'delta_ssd_conv_b', 'delta_ssd_dt_bias', 'delta_ssd_a_log', 'delta_ssd_d', 'delta_ssd_norm', 'delta_ssd_w_out', 'delta_mla_w_in', 'delta_mla_q_a_norm', 'delta_mla_w_q_b', 'delta_mla_kv_a_norm', 'delta_mla_w_kv_b', 'delta_mla_q_norm', 'delta_mla_k_norm', 'delta_mla_w_out', 'delta_mlp_w_up', 'delta_mlp_w_down', 'new_m_meta_tokens', 'new_m_ln_mix', 'new_m_ln_mlp', 'new_m_ssd_w_in', 'new_m_ssd_conv_w', 'new_m_ssd_conv_b', 'new_m_ssd_dt_bias', 'new_m_ssd_a_log', 'new_m_ssd_d', 'new_m_ssd_norm', 'new_m_ssd_w_out', 'new_m_mla_w_in', 'new_m_mla_q_a_norm', 'new_m_mla_w_q_b', 'new_m_mla_kv_a_norm', 'new_m_mla_w_kv_b', 'new_m_mla_q_norm', 'new_m_mla_k_norm', 'new_m_mla_w_out', 'new_m_mlp_w_up', 'new_m_mlp_w_down', 'new_v_meta_tokens', 'new_v_ln_mix', 'new_v_ln_mlp', 'new_v_ssd_w_in', 'new_v_ssd_conv_w', 'new_v_ssd_conv_b', 'new_v_ssd_dt_bias', 'new_v_ssd_a_log', 'new_v_ssd_d', 'new_v_ssd_norm', 'new_v_ssd_w_out', 'new_v_mla_w_in', 'new_v_mla_q_a_norm', 'new_v_mla_w_q_b', 'new_v_mla_kv_a_norm', 'new_v_mla_w_kv_b', 'new_v_mla_q_norm', 'new_v_mla_k_norm', 'new_v_mla_w_out', 'new_v_mlp_w_up', 'new_v_mlp_w_down']
TWIN_LEAF_KINDS = {'loss': 'loss', 'grad_x': 'grad_x', 'grad_meta_tokens': 'grad_w', 'grad_ln_mix': 'grad_w', 'grad_ln_mlp': 'grad_w', 'grad_ssd_w_in': 'grad_w', 'grad_ssd_conv_w': 'grad_w', 'grad_ssd_conv_b': 'grad_w', 'grad_ssd_dt_bias': 'grad_w', 'grad_ssd_a_log': 'grad_w', 'grad_ssd_d': 'grad_w', 'grad_ssd_norm': 'grad_w', 'grad_ssd_w_out': 'grad_w', 'grad_mla_w_in': 'grad_w', 'grad_mla_q_a_norm': 'grad_w', 'grad_mla_w_q_b': 'grad_w', 'grad_mla_kv_a_norm': 'grad_w', 'grad_mla_w_kv_b': 'grad_w', 'grad_mla_q_norm': 'grad_w', 'grad_mla_k_norm': 'grad_w', 'grad_mla_w_out': 'grad_w', 'grad_mlp_w_up': 'grad_w', 'grad_mlp_w_down': 'grad_w', 'delta_meta_tokens': 'delta_w', 'delta_ln_mix': 'delta_w', 'delta_ln_mlp': 'delta_w', 'delta_ssd_w_in': 'delta_w', 'delta_ssd_conv_w': 'delta_w', 'delta_ssd_conv_b': 'delta_w', 'delta_ssd_dt_bias': 'delta_w', 'delta_ssd_a_log': 'delta_w', 'delta_ssd_d': 'delta_w', 'delta_ssd_norm': 'delta_w', 'delta_ssd_w_out': 'delta_w', 'delta_mla_w_in': 'delta_w', 'delta_mla_q_a_norm': 'delta_w', 'delta_mla_w_q_b': 'delta_w', 'delta_mla_kv_a_norm': 'delta_w', 'delta_mla_w_kv_b': 'delta_w', 'delta_mla_q_norm': 'delta_w', 'delta_mla_k_norm': 'delta_w', 'delta_mla_w_out': 'delta_w', 'delta_mlp_w_up': 'delta_w', 'delta_mlp_w_down': 'delta_w', 'new_m_meta_tokens': 'new_m', 'new_m_ln_mix': 'new_m', 'new_m_ln_mlp': 'new_m', 'new_m_ssd_w_in': 'new_m', 'new_m_ssd_conv_w': 'new_m', 'new_m_ssd_conv_b': 'new_m', 'new_m_ssd_dt_bias': 'new_m', 'new_m_ssd_a_log': 'new_m', 'new_m_ssd_d': 'new_m', 'new_m_ssd_norm': 'new_m', 'new_m_ssd_w_out': 'new_m', 'new_m_mla_w_in': 'new_m', 'new_m_mla_q_a_norm': 'new_m', 'new_m_mla_w_q_b': 'new_m', 'new_m_mla_kv_a_norm': 'new_m', 'new_m_mla_w_kv_b': 'new_m', 'new_m_mla_q_norm': 'new_m', 'new_m_mla_k_norm': 'new_m', 'new_m_mla_w_out': 'new_m', 'new_m_mlp_w_up': 'new_m', 'new_m_mlp_w_down': 'new_m', 'new_v_meta_tokens': 'new_v', 'new_v_ln_mix': 'new_v', 'new_v_ln_mlp': 'new_v', 'new_v_ssd_w_in': 'new_v', 'new_v_ssd_conv_w': 'new_v', 'new_v_ssd_conv_b': 'new_v', 'new_v_ssd_dt_bias': 'new_v', 'new_v_ssd_a_log': 'new_v', 'new_v_ssd_d': 'new_v', 'new_v_ssd_norm': 'new_v', 'new_v_ssd_w_out': 'new_v', 'new_v_mla_w_in': 'new_v', 'new_v_mla_q_a_norm': 'new_v', 'new_v_mla_w_q_b': 'new_v', 'new_v_mla_kv_a_norm': 'new_v', 'new_v_mla_w_kv_b': 'new_v', 'new_v_mla_q_norm': 'new_v', 'new_v_mla_k_norm': 'new_v', 'new_v_mla_w_out': 'new_v', 'new_v_mlp_w_up': 'new_v', 'new_v_mlp_w_down': 'new_v'}


def _forward(args):
    return _fwd_reference(*[args[k] for k in FWD_PARAMS])


def _output_shape():
    out = _jax.eval_shape(lambda: _forward(_fwd_setup_inputs(0)))
    return out.shape, out.dtype

N_MICROBATCH = 1
ADAM_LR = 0.001
ADAM_B1 = 0.9
ADAM_B2 = 0.999
ADAM_EPS = 1e-08
ADAM_WD = 0.01
ADAM_STEP = 10
PER_EXAMPLE_BATCH_AXIS = {'x': 0, 'loss_target': 0}
SHARED_INPUTS = []
_WEIGHT_DTYPES = {'meta_tokens': _jnp.float32, 'ln_mix': _jnp.float32, 'ln_mlp': _jnp.float32, 'ssd_w_in': _jnp.float32, 'ssd_conv_w': _jnp.float32, 'ssd_conv_b': _jnp.float32, 'ssd_dt_bias': _jnp.float32, 'ssd_a_log': _jnp.float32, 'ssd_d': _jnp.float32, 'ssd_norm': _jnp.float32, 'ssd_w_out': _jnp.float32, 'mla_w_in': _jnp.float32, 'mla_q_a_norm': _jnp.float32, 'mla_w_q_b': _jnp.float32, 'mla_kv_a_norm': _jnp.float32, 'mla_w_kv_b': _jnp.float32, 'mla_q_norm': _jnp.float32, 'mla_k_norm': _jnp.float32, 'mla_w_out': _jnp.float32, 'mlp_w_up': _jnp.float32, 'mlp_w_down': _jnp.float32}
MOMENT_SCALE = {'meta_tokens': 3.812929e-01, 'ln_mix': 1.715022e+01, 'ln_mlp': 1.013851e+02, 'ssd_w_in': 4.131868e+00, 'ssd_conv_w': 6.095019e+00, 'ssd_conv_b': 1.917448e+01, 'ssd_dt_bias': 4.512455e+00, 'ssd_a_log': 3.647498e+01, 'ssd_d': 3.405964e+01, 'ssd_norm': 3.275421e+01, 'ssd_w_out': 1.844813e+01, 'mla_w_in': 2.791319e+01, 'mla_q_a_norm': 1.252520e+00, 'mla_w_q_b': 6.365291e-01, 'mla_kv_a_norm': 4.935368e+01, 'mla_w_kv_b': 1.429545e+01, 'mla_q_norm': 2.502386e+00, 'mla_k_norm': 2.483770e+00, 'mla_w_out': 1.923650e+01, 'mlp_w_up': 9.344404e+00, 'mlp_w_down': 3.479704e+01}


def _to_microbatches(a, axis):
    t = _jnp.moveaxis(a, axis, 0)
    t = t.reshape((N_MICROBATCH, t.shape[0] // N_MICROBATCH) + t.shape[1:])
    return _jnp.moveaxis(t, 1, axis + 1)


def setup_inputs(seed: int = 0) -> dict:
    inp = _fwd_setup_inputs(seed)
    key = _jax.random.fold_in(_jax.random.key(seed), 7919)
    shape, _ = _output_shape()
    out = dict(inp)
    out["loss_target"] = _jax.random.normal(_jax.random.fold_in(key, 0), shape, _jnp.float32)
    for i, name in enumerate(TWIN_WEIGHTS):
        w = inp[name].astype(_jnp.float32)
        if MOMENT_SCALE is None:
            s = _jnp.sqrt(_jnp.mean(_jnp.square(w)) + 1e-30)
        else:
            s = MOMENT_SCALE[name]
        km, kv = _jax.random.split(_jax.random.fold_in(key, i + 1))
        out[name] = w
        out["m_" + name] = s * _jax.random.normal(km, w.shape, _jnp.float32)
        out["v_" + name] = (s * s) * _jax.random.uniform(kv, w.shape, _jnp.float32, 0.5, 1.5)
    if N_MICROBATCH > 1:
        for name, axis in PER_EXAMPLE_BATCH_AXIS.items():
            out[name] = _to_microbatches(out[name], axis)
    return {'x': out['x'], 'meta_tokens': out['meta_tokens'], 'ln_mix': out['ln_mix'], 'ln_mlp': out['ln_mlp'], 'ssd_w_in': out['ssd_w_in'], 'ssd_conv_w': out['ssd_conv_w'], 'ssd_conv_b': out['ssd_conv_b'], 'ssd_dt_bias': out['ssd_dt_bias'], 'ssd_a_log': out['ssd_a_log'], 'ssd_d': out['ssd_d'], 'ssd_norm': out['ssd_norm'], 'ssd_w_out': out['ssd_w_out'], 'mla_w_in': out['mla_w_in'], 'mla_q_a_norm': out['mla_q_a_norm'], 'mla_w_q_b': out['mla_w_q_b'], 'mla_kv_a_norm': out['mla_kv_a_norm'], 'mla_w_kv_b': out['mla_w_kv_b'], 'mla_q_norm': out['mla_q_norm'], 'mla_k_norm': out['mla_k_norm'], 'mla_w_out': out['mla_w_out'], 'mlp_w_up': out['mlp_w_up'], 'mlp_w_down': out['mlp_w_down'], 'loss_target': out['loss_target'], 'm_meta_tokens': out['m_meta_tokens'], 'm_ln_mix': out['m_ln_mix'], 'm_ln_mlp': out['m_ln_mlp'], 'm_ssd_w_in': out['m_ssd_w_in'], 'm_ssd_conv_w': out['m_ssd_conv_w'], 'm_ssd_conv_b': out['m_ssd_conv_b'], 'm_ssd_dt_bias': out['m_ssd_dt_bias'], 'm_ssd_a_log': out['m_ssd_a_log'], 'm_ssd_d': out['m_ssd_d'], 'm_ssd_norm': out['m_ssd_norm'], 'm_ssd_w_out': out['m_ssd_w_out'], 'm_mla_w_in': out['m_mla_w_in'], 'm_mla_q_a_norm': out['m_mla_q_a_norm'], 'm_mla_w_q_b': out['m_mla_w_q_b'], 'm_mla_kv_a_norm': out['m_mla_kv_a_norm'], 'm_mla_w_kv_b': out['m_mla_w_kv_b'], 'm_mla_q_norm': out['m_mla_q_norm'], 'm_mla_k_norm': out['m_mla_k_norm'], 'm_mla_w_out': out['m_mla_w_out'], 'm_mlp_w_up': out['m_mlp_w_up'], 'm_mlp_w_down': out['m_mlp_w_down'], 'v_meta_tokens': out['v_meta_tokens'], 'v_ln_mix': out['v_ln_mix'], 'v_ln_mlp': out['v_ln_mlp'], 'v_ssd_w_in': out['v_ssd_w_in'], 'v_ssd_conv_w': out['v_ssd_conv_w'], 'v_ssd_conv_b': out['v_ssd_conv_b'], 'v_ssd_dt_bias': out['v_ssd_dt_bias'], 'v_ssd_a_log': out['v_ssd_a_log'], 'v_ssd_d': out['v_ssd_d'], 'v_ssd_norm': out['v_ssd_norm'], 'v_ssd_w_out': out['v_ssd_w_out'], 'v_mla_w_in': out['v_mla_w_in'], 'v_mla_q_a_norm': out['v_mla_q_a_norm'], 'v_mla_w_q_b': out['v_mla_w_q_b'], 'v_mla_kv_a_norm': out['v_mla_kv_a_norm'], 'v_mla_w_kv_b': out['v_mla_w_kv_b'], 'v_mla_q_norm': out['v_mla_q_norm'], 'v_mla_k_norm': out['v_mla_k_norm'], 'v_mla_w_out': out['v_mla_w_out'], 'v_mlp_w_up': out['v_mlp_w_up'], 'v_mlp_w_down': out['v_mlp_w_down']}


def _loss(weights, diff, rest, loss_target):
    with _jax.named_scope("forward"):
        args = {**rest, TWIN_DIFF_INPUT: diff, **{k: w.astype(_WEIGHT_DTYPES[k]) for k, w in weights.items()}}
        y = _forward(args)
    with _jax.named_scope("loss_head"):
        err = _jnp.square(y.astype(_jnp.float32) - loss_target)
        return 0.5 * _jnp.sum(_jnp.mean(err, axis=-1)) if err.ndim else 0.5 * err


def _adamw(w, g, m, v):
    m = ADAM_B1 * m + (1.0 - ADAM_B1) * g
    v = ADAM_B2 * v + (1.0 - ADAM_B2) * _jnp.square(g)
    m_hat = m / (1.0 - ADAM_B1 ** ADAM_STEP)
    v_hat = v / (1.0 - ADAM_B2 ** ADAM_STEP)
    delta = -ADAM_LR * (m_hat / (_jnp.sqrt(v_hat) + ADAM_EPS) + ADAM_WD * w)
    return delta, m, v


def reference(x, meta_tokens, ln_mix, ln_mlp, ssd_w_in, ssd_conv_w, ssd_conv_b, ssd_dt_bias, ssd_a_log, ssd_d, ssd_norm, ssd_w_out, mla_w_in, mla_q_a_norm, mla_w_q_b, mla_kv_a_norm, mla_w_kv_b, mla_q_norm, mla_k_norm, mla_w_out, mlp_w_up, mlp_w_down, loss_target, m_meta_tokens, m_ln_mix, m_ln_mlp, m_ssd_w_in, m_ssd_conv_w, m_ssd_conv_b, m_ssd_dt_bias, m_ssd_a_log, m_ssd_d, m_ssd_norm, m_ssd_w_out, m_mla_w_in, m_mla_q_a_norm, m_mla_w_q_b, m_mla_kv_a_norm, m_mla_w_kv_b, m_mla_q_norm, m_mla_k_norm, m_mla_w_out, m_mlp_w_up, m_mlp_w_down, v_meta_tokens, v_ln_mix, v_ln_mlp, v_ssd_w_in, v_ssd_conv_w, v_ssd_conv_b, v_ssd_dt_bias, v_ssd_a_log, v_ssd_d, v_ssd_norm, v_ssd_w_out, v_mla_w_in, v_mla_q_a_norm, v_mla_w_q_b, v_mla_kv_a_norm, v_mla_w_kv_b, v_mla_q_norm, v_mla_k_norm, v_mla_w_out, v_mlp_w_up, v_mlp_w_down):
    given = dict(x=x, meta_tokens=meta_tokens, ln_mix=ln_mix, ln_mlp=ln_mlp, ssd_w_in=ssd_w_in, ssd_conv_w=ssd_conv_w, ssd_conv_b=ssd_conv_b, ssd_dt_bias=ssd_dt_bias, ssd_a_log=ssd_a_log, ssd_d=ssd_d, ssd_norm=ssd_norm, ssd_w_out=ssd_w_out, mla_w_in=mla_w_in, mla_q_a_norm=mla_q_a_norm, mla_w_q_b=mla_w_q_b, mla_kv_a_norm=mla_kv_a_norm, mla_w_kv_b=mla_w_kv_b, mla_q_norm=mla_q_norm, mla_k_norm=mla_k_norm, mla_w_out=mla_w_out, mlp_w_up=mlp_w_up, mlp_w_down=mlp_w_down, loss_target=loss_target, m_meta_tokens=m_meta_tokens, m_ln_mix=m_ln_mix, m_ln_mlp=m_ln_mlp, m_ssd_w_in=m_ssd_w_in, m_ssd_conv_w=m_ssd_conv_w, m_ssd_conv_b=m_ssd_conv_b, m_ssd_dt_bias=m_ssd_dt_bias, m_ssd_a_log=m_ssd_a_log, m_ssd_d=m_ssd_d, m_ssd_norm=m_ssd_norm, m_ssd_w_out=m_ssd_w_out, m_mla_w_in=m_mla_w_in, m_mla_q_a_norm=m_mla_q_a_norm, m_mla_w_q_b=m_mla_w_q_b, m_mla_kv_a_norm=m_mla_kv_a_norm, m_mla_w_kv_b=m_mla_w_kv_b, m_mla_q_norm=m_mla_q_norm, m_mla_k_norm=m_mla_k_norm, m_mla_w_out=m_mla_w_out, m_mlp_w_up=m_mlp_w_up, m_mlp_w_down=m_mlp_w_down, v_meta_tokens=v_meta_tokens, v_ln_mix=v_ln_mix, v_ln_mlp=v_ln_mlp, v_ssd_w_in=v_ssd_w_in, v_ssd_conv_w=v_ssd_conv_w, v_ssd_conv_b=v_ssd_conv_b, v_ssd_dt_bias=v_ssd_dt_bias, v_ssd_a_log=v_ssd_a_log, v_ssd_d=v_ssd_d, v_ssd_norm=v_ssd_norm, v_ssd_w_out=v_ssd_w_out, v_mla_w_in=v_mla_w_in, v_mla_q_a_norm=v_mla_q_a_norm, v_mla_w_q_b=v_mla_w_q_b, v_mla_kv_a_norm=v_mla_kv_a_norm, v_mla_w_kv_b=v_mla_w_kv_b, v_mla_q_norm=v_mla_q_norm, v_mla_k_norm=v_mla_k_norm, v_mla_w_out=v_mla_w_out, v_mlp_w_up=v_mlp_w_up, v_mlp_w_down=v_mlp_w_down)
    weights = {n: given[n] for n in TWIN_WEIGHTS}
    shared = {n: given[n] for n in SHARED_INPUTS}
    per_example = {n: given[n] for n in ['x']}
    grad_fn = _jax.value_and_grad(_loss, argnums=(0, 1))

    def one_microbatch(ex, loss_target):
        ex = dict(ex)
        diff = ex.pop(TWIN_DIFF_INPUT)
        return grad_fn(weights, diff, {**shared, **ex}, loss_target)

    if N_MICROBATCH == 1:
        loss, (grad_w, grad_x) = one_microbatch(per_example, given["loss_target"])
    else:
        def body(carry, xs):
            loss_sum, grad_sum = carry
            l_k, (gw_k, gx_k) = one_microbatch(xs[0], xs[1])
            with _jax.named_scope("update"):
                return (loss_sum + l_k, _jax.tree.map(_jnp.add, grad_sum, gw_k)), gx_k

        init = (_jnp.zeros((), _jnp.float32), _jax.tree.map(_jnp.zeros_like, weights))
        (loss, grad_w), grad_x = _jax.lax.scan(body, init, (per_example, given["loss_target"]))
    with _jax.named_scope("update"):
        delta_w, new_m, new_v = {}, {}, {}
        for n in TWIN_WEIGHTS:
            delta_w[n], new_m[n], new_v[n] = _adamw(weights[n], grad_w[n], given["m_" + n], given["v_" + n])
    return (loss, grad_x, *[grad_w[n] for n in TWIN_WEIGHTS], *[delta_w[n] for n in TWIN_WEIGHTS],
            *[new_m[n] for n in TWIN_WEIGHTS], *[new_v[n] for n in TWIN_WEIGHTS])
```

```python
import functools
import math

import jax
import jax.numpy as jnp
from jax import lax
from jax.experimental import pallas as pl
from jax.experimental.pallas import tpu as pltpu

F32 = jnp.float32
BF16 = jnp.bfloat16
HI = lax.Precision.HIGHEST
MESH = pl.DeviceIdType.MESH

D_MODEL = 1024
N_META = 16
EPS = 1e-6
SSD_D_INNER = 2048
SSD_HEADS = 32
SSD_HEAD_DIM = 64
SSD_GROUPS = 8
SSD_HPG = 4
SSD_STATE = 128
SSD_CONV = 4
CHUNK = 128
SSD_IN_DIM = 6176
SSD_IN_PAD = 6272
MLA_HEADS = 16
MLA_NOPE = 64
MLA_ROPE = 32
MLA_V = 64
MLA_QK = 96
MLA_Q_RANK = 384
MLA_KV_RANK = 256
HEAD_SLOT = 128
MLA_WIDE = MLA_HEADS * HEAD_SLOT
LAT_PAD = 768
ROPE_THETA = 10000.0
D_FF = 4096
NPAD = CHUNK - N_META
ADAM_LR, ADAM_B1, ADAM_B2, ADAM_EPS, ADAM_WD, ADAM_STEP = 0.001, 0.9, 0.999, 1e-08, 0.01, 10
LANES = 1024
VMEM_LIMIT = 56 * 1024 * 1024


def _pick(n, cands):
    for c in cands:
        if n % c == 0:
            return c
    return n


def _cparams(**kw):
    return pltpu.CompilerParams(vmem_limit_bytes=VMEM_LIMIT, **kw)


def _mm(a, b, dims, *, name, out_dtype=F32, a_fn=None, epi=None, extras=()):
    if dims == 'nn':
        (M, K), (K2, N) = a.shape, b.shape
    elif dims == 'nt':
        (M, K), (N, K2) = a.shape, b.shape
    else:
        (K, M), (K2, N) = a.shape, b.shape
    assert K == K2, (a.shape, b.shape, dims)
    if dims == 'tn':
        tm = _pick(M, (1024, 768, 512, 384, 256, 128))
        tn = _pick(N, (1024, 896, 768, 512, 384, 256, 128))
        tk = _pick(K, (1408, 1024, 512, 384, 256, 128))
    else:
        tm = _pick(M, (1408, 1024, 512, 384, 256, 128))
        tn = _pick(N, (512, 896, 768, 384, 256, 128))
        tk = _pick(K, (1024, 896, 768, 512, 384, 256, 128))
    nk = K // tk
    if dims == 'nn':
        a_spec = pl.BlockSpec((tm, tk), lambda i, j, k: (i, k))
        b_spec = pl.BlockSpec((tk, tn), lambda i, j, k: (k, j))
        dn = (((1,), (0,)), ((), ()))
    elif dims == 'nt':
        a_spec = pl.BlockSpec((tm, tk), lambda i, j, k: (i, k))
        b_spec = pl.BlockSpec((tn, tk), lambda i, j, k: (j, k))
        dn = (((1,), (1,)), ((), ()))
    else:
        a_spec = pl.BlockSpec((tk, tm), lambda i, j, k: (k, i))
        b_spec = pl.BlockSpec((tk, tn), lambda i, j, k: (k, j))
        dn = (((0,), (0,)), ((), ()))
    o_spec = pl.BlockSpec((tm, tn), lambda i, j, k: (i, j))
    n_ex = len(extras)

    def body(a_ref, b_ref, *rest):
        ex_refs, o_ref, acc = rest[:n_ex], rest[n_ex], rest[n_ex + 1]
        k = pl.program_id(2)

        @pl.when(k == 0)
        def _():
            acc[...] = jnp.zeros_like(acc)

        av = a_ref[...]
        if a_fn is not None:
            av = a_fn(av)
        acc[...] += lax.dot_general(av.astype(BF16), b_ref[...].astype(BF16), dn,
                                    preferred_element_type=F32)

        @pl.when(k == nk - 1)
        def _():
            r = acc[...]
            if epi is not None:
                r = epi(r, *[e[...] for e in ex_refs])
            o_ref[...] = r.astype(out_dtype)

    return pl.pallas_call(
        body, name=name,
        out_shape=jax.ShapeDtypeStruct((M, N), out_dtype),
        grid=(M // tm, N // tn, nk),
        in_specs=[a_spec, b_spec] + [o_spec] * n_ex,
        out_specs=o_spec,
        scratch_shapes=[pltpu.VMEM((tm, tn), F32)],
        compiler_params=_cparams(dimension_semantics=("parallel", "parallel", "arbitrary")),
    )(a, b, *extras)


def _row_call(fn, rows, consts, out_rows, out_accs=(), *, n_rows, tile, name):
    n_r, n_c, n_o, n_a = len(rows), len(consts), len(out_rows), len(out_accs)
    steps = n_rows // tile

    def body(*refs):
        r_refs = refs[:n_r]
        c_refs = refs[n_r:n_r + n_c]
        o_refs = refs[n_r + n_c:n_r + n_c + n_o]
        a_refs = refs[n_r + n_c + n_o:]
        i = pl.program_id(0)
        res = fn(i, *[r[...] for r in r_refs], *[c[...] for c in c_refs])
        for o_ref, val in zip(o_refs, res[:n_o]):
            o_ref[...] = val.astype(o_ref.dtype)

        @pl.when(i == 0)
        def _():
            for a_ref in a_refs:
                a_ref[...] = jnp.zeros_like(a_ref)

        for a_ref, val in zip(a_refs, res[n_o:]):
            a_ref[...] += val

    in_specs = [pl.BlockSpec((tile, w), functools.partial(lambda i, cb: (i, cb), cb=cb))
                for (_, w, cb) in rows]
    in_specs += [pl.BlockSpec(c.shape, lambda i: (0, 0)) for c in consts]
    out_specs = [pl.BlockSpec((tile, c), lambda i: (i, 0)) for (c, _) in out_rows]
    out_specs += [pl.BlockSpec(s, lambda i: (0, 0)) for s in out_accs]
    out_shape = [jax.ShapeDtypeStruct((n_rows, c), dt) for (c, dt) in out_rows]
    out_shape += [jax.ShapeDtypeStruct(s, F32) for s in out_accs]
    return pl.pallas_call(
        body, name=name, out_shape=out_shape, grid=(steps,),
        in_specs=in_specs, out_specs=out_specs,
        compiler_params=_cparams(dimension_semantics=("arbitrary",)),
    )(*[r[0] for r in rows], *consts)


def _row_mask(i, tile):
    r = i * tile + lax.broadcasted_iota(jnp.int32, (tile, 1), 0)
    return (r >= NPAD).astype(F32)


def _rms(x, g):
    return x * lax.rsqrt(jnp.mean(x * x, axis=-1, keepdims=True) + EPS) * g


def _silu(x):
    return x * (1.0 / (1.0 + jnp.exp(-x)))


def _softplus(x):
    return jnp.maximum(x, 0.0) + jnp.log(1.0 + jnp.exp(-jnp.abs(x)))


def _rms_fwd(h, g, name):
    lp = h.shape[0]
    return _row_call(lambda i, hv, gv: (_rms(hv, gv),), [(h, D_MODEL, 0)], [g],
                     [(D_MODEL, BF16)], n_rows=lp, tile=_pick(lp, (384, 256, 128)), name=name)[0]


def _rms_bwd(h, g, d_hn, d_res, name):
    lp = h.shape[0]
    tile = _pick(lp, (384, 256, 128))

    def fn(i, hv, dv, rv, gv):
        _, vjp = jax.vjp(_rms, hv, gv)
        dh, dg = vjp(dv)
        return (rv + dh) * _row_mask(i, tile), dg

    return _row_call(fn, [(h, D_MODEL, 0), (d_hn, D_MODEL, 0), (d_res, D_MODEL, 0)], [g],
                     [(D_MODEL, F32)], [(1, D_MODEL)], n_rows=lp, tile=tile, name=name)


@functools.partial(jax.custom_vjp, nondiff_argnums=(1,))
def _roll_rows(x, s):
    return pltpu.roll(x, s, 0)


def _roll_rows_fwd(x, s):
    return pltpu.roll(x, s, 0), None


def _roll_rows_bwd(s, _, ct):
    return (pltpu.roll(ct, (ct.shape[0] - s) % ct.shape[0], 0),)


_roll_rows.defvjp(_roll_rows_fwd, _roll_rows_bwd)


def _conv_silu(cur, halo, w_rows, b):
    full = jnp.concatenate([halo, cur], axis=0)
    acc = cur * w_rows[SSD_CONV - 1] + b
    for k in range(SSD_CONV - 1):
        acc = acc + _roll_rows(full, SSD_CONV - 1 - k)[8:] * w_rows[k]
    return _silu(acc)


def _expand_heads(v, e_mat):
    return lax.dot_general(v, e_mat, (((1,), (0,)), ((), ())), precision=HI,
                           preferred_element_type=F32)


def _ssd_chunk(mask, z, xs_pre, bc_pre, halo_x, halo_bc, dt_pre, st, cwx0, cwx1, cwx2, cwx3,
               cwb0, cwb1, cwb2, cwb3, cb_x, cb_bc, dtb, alog, dsk, ng):
    L = CHUNK
    lane_h = lax.broadcasted_iota(jnp.int32, (1, 128), 1)
    head_ok = (lane_h < SSD_HEADS).astype(F32)
    e_mat = (lax.broadcasted_iota(jnp.int32, (128, SSD_D_INNER), 1) // SSD_HEAD_DIM
             == lax.broadcasted_iota(jnp.int32, (128, SSD_D_INNER), 0)).astype(F32)
    ri = lax.broadcasted_iota(jnp.int32, (L, L), 0)
    ci = lax.broadcasted_iota(jnp.int32, (L, L), 1)
    causal = ri >= ci
    tri = causal.astype(F32)

    xs = _conv_silu(xs_pre, halo_x, (cwx0, cwx1, cwx2, cwx3), cb_x) * mask
    bc = _conv_silu(bc_pre, halo_bc, (cwb0, cwb1, cwb2, cwb3), cb_bc) * mask
    dt = _softplus(dt_pre + dtb) * mask * head_ok
    a_dt = dt * (-jnp.exp(alog))
    a_cs = lax.dot_general(tri, a_dt, (((1,), (0,)), ((), ())), precision=HI,
                           preferred_element_type=F32)
    a_cs_t = a_cs.T
    dt_e = _expand_heads(dt, e_mat)
    acs_e = _expand_heads(a_cs, e_mat)
    last_e = jnp.sum(_expand_heads(a_dt, e_mat), axis=0, keepdims=True)
    row8 = lax.broadcasted_iota(jnp.int32, (8, 128), 0)
    d_e = jnp.sum(_expand_heads(jnp.where(row8 == 0, jnp.broadcast_to(dsk, (8, 128)), 0.0), e_mat),
                  axis=0, keepdims=True)
    xdt = xs * dt_e
    dte_e = jnp.exp(last_e - acs_e)
    dfs_e = jnp.exp(acs_e)
    cd_e = jnp.exp(last_e)
    sub_h = lax.broadcasted_iota(jnp.int32, (128, L), 0)
    lane_hl = lax.broadcasted_iota(jnp.int32, (L, 128), 1)
    lane_g = lax.broadcasted_iota(jnp.int32, (1, SSD_HPG * SSD_HEAD_DIM), 1) // SSD_HEAD_DIM

    ys, new_st = [], []
    for g in range(SSD_GROUPS):
        b_g = bc[:, g * 128:(g + 1) * 128].astype(BF16)
        c_g = bc[:, 1024 + g * 128:1024 + (g + 1) * 128].astype(BF16)
        gs = slice(g * 256, (g + 1) * 256)
        xdt_g = xdt[:, gs]
        cb = lax.dot_general(c_g, b_g, (((1,), (1,)), ((), ())), preferred_element_type=F32)
        st_g = st[g * 128:(g + 1) * 128, :]
        y_g = lax.dot_general(c_g, st_g.astype(BF16), (((1,), (0,)), ((), ())),
                              preferred_element_type=F32) * dfs_e[:, gs]
        for j in range(SSD_HPG):
            h = g * SSD_HPG + j
            col = jnp.sum(jnp.where(lane_hl == h, a_cs, 0.0), axis=1, keepdims=True)
            row = jnp.sum(jnp.where(sub_h == h, a_cs_t, 0.0), axis=0, keepdims=True)
            dec = jnp.where(causal, jnp.exp(jnp.where(causal, col - row, 0.0)), 0.0)
            m_h = (cb * dec).astype(BF16)
            x_h = jnp.where(lane_g == j, xdt_g, 0.0).astype(BF16)
            y_g = y_g + lax.dot_general(m_h, x_h, (((1,), (0,)), ((), ())),
                                        preferred_element_type=F32)
        s_new = lax.dot_general(b_g, (xdt_g * dte_e[:, gs]).astype(BF16), (((0,), (0,)), ((), ())),
                                preferred_element_type=F32)
        new_st.append(st_g * cd_e[:, gs] + s_new)
        ys.append(y_g)
    y = jnp.concatenate(ys, axis=1) + xs * d_e
    gg = y * _silu(z)
    outs = []
    for g in range(SSD_GROUPS):
        sl = gg[:, g * 256:(g + 1) * 256]
        outs.append(sl * lax.rsqrt(jnp.mean(sl * sl, axis=-1, keepdims=True) + EPS))
    out = jnp.concatenate(outs, axis=1) * ng
    return out, jnp.concatenate(new_st, axis=0)


def _ssd_consts(conv_w, conv_b, dtb, alog, dsk, ng):
    return [conv_w, conv_b, dtb, alog, dsk, ng]


def _ssd_param_vals(cw_ref, cb_ref, dtb_ref, alog_ref, dsk_ref, ng_ref):
    cwx = [cw_ref[k:k + 1, 0:SSD_D_INNER] for k in range(SSD_CONV)]
    cwb = [cw_ref[k:k + 1, SSD_D_INNER:2 * SSD_D_INNER] for k in range(SSD_CONV)]
    return (*cwx, *cwb, cb_ref[:, 0:SSD_D_INNER], cb_ref[:, SSD_D_INNER:2 * SSD_D_INNER],
            dtb_ref[...], alog_ref[...], dsk_ref[...], ng_ref[...])


def _ssd_in_specs(rev, nc):
    def cidx(i):
        return (nc - 1 - i) if rev else i

    def halo(cb):
        return pl.BlockSpec((8, SSD_D_INNER), lambda i: (jnp.maximum(16 * cidx(i) - 1, 0), cb))

    return [
        pl.BlockSpec((CHUNK, SSD_D_INNER), lambda i: (cidx(i), 0)),
        pl.BlockSpec((CHUNK, SSD_D_INNER), lambda i: (cidx(i), 1)),
        pl.BlockSpec((CHUNK, SSD_D_INNER), lambda i: (cidx(i), 2)),
        halo(1), halo(2),
        pl.BlockSpec((CHUNK, 128), lambda i: (cidx(i), 48)),
    ]


def _ssd_fwd(zxd, consts, name):
    lp = zxd.shape[0]
    nc = lp // CHUNK

    def body(z_ref, xs_ref, bc_ref, hx_ref, hb_ref, dt_ref, cw_ref, cb_ref, dtb_ref, alog_ref,
             dsk_ref, ng_ref, y_ref, st_ref, state):
        c = pl.program_id(0)

        @pl.when(c == 0)
        def _():
            state[...] = jnp.zeros_like(state)

        live = (c > 0).astype(F32)
        st_ref[0] = state[...]
        out, st_new = _ssd_chunk(
            _row_mask(c, CHUNK), z_ref[...], xs_ref[...], bc_ref[...], hx_ref[...] * live,
            hb_ref[...] * live, dt_ref[...], state[...],
            *_ssd_param_vals(cw_ref, cb_ref, dtb_ref, alog_ref, dsk_ref, ng_ref))
        y_ref[...] = out.astype(y_ref.dtype)
        state[...] = st_new

    return pl.pallas_call(
        body, name=name,
        out_shape=[jax.ShapeDtypeStruct((lp, SSD_D_INNER), BF16),
                   jax.ShapeDtypeStruct((nc, SSD_GROUPS * SSD_STATE, 256), F32)],
        grid=(nc,),
        in_specs=_ssd_in_specs(False, nc) + [pl.BlockSpec(c.shape, lambda i: (0, 0)) for c in consts],
        out_specs=[pl.BlockSpec((CHUNK, SSD_D_INNER), lambda i: (i, 0)),
                   pl.BlockSpec((1, SSD_GROUPS * SSD_STATE, 256), lambda i: (i, 0, 0))],
        scratch_shapes=[pltpu.VMEM((SSD_GROUPS * SSD_STATE, 256), F32)],
        compiler_params=_cparams(dimension_semantics=("arbitrary",)),
    )(zxd, zxd, zxd, zxd, zxd, zxd, *consts)


def _ssd_bwd(zxd, states, d_y, consts, name):
    lp = zxd.shape[0]
    nc = lp // CHUNK

    def body(z_ref, xs_ref, bc_ref, hx_ref, hb_ref, dt_ref, st_ref, dy_ref, cw_ref, cb_ref, dtb_ref,
             alog_ref, dsk_ref, ng_ref, dz_ref, dcw_ref, dcb_ref, ddtb_ref, dalog_ref, ddsk_ref,
             dng_ref, d_state, d_hx, d_hb):
        i = pl.program_id(0)
        c = nc - 1 - i

        @pl.when(i == 0)
        def _():
            d_state[...] = jnp.zeros_like(d_state)
            d_hx[...] = jnp.zeros_like(d_hx)
            d_hb[...] = jnp.zeros_like(d_hb)
            for r in (dcw_ref, dcb_ref, ddtb_ref, dalog_ref, ddsk_ref, dng_ref):
                r[...] = jnp.zeros_like(r)

        live = (c > 0).astype(F32)
        fn = functools.partial(_ssd_chunk, _row_mask(c, CHUNK))
        prim = (z_ref[...], xs_ref[...], bc_ref[...], hx_ref[...] * live, hb_ref[...] * live,
                dt_ref[...], st_ref[0],
                *_ssd_param_vals(cw_ref, cb_ref, dtb_ref, alog_ref, dsk_ref, ng_ref))
        _, vjp = jax.vjp(fn, *prim)
        (d_z, d_xs, d_bc, g_hx, g_hb, d_dt, g_st, *d_par) = vjp((dy_ref[...], d_state[...]))
        zeros = jnp.zeros((CHUNK - 8, SSD_D_INNER), F32)
        d_xs = d_xs + jnp.concatenate([zeros, d_hx[...]], axis=0)
        d_bc = d_bc + jnp.concatenate([zeros, d_hb[...]], axis=0)
        dz_ref[:, 0:SSD_D_INNER] = d_z
        dz_ref[:, SSD_D_INNER:2 * SSD_D_INNER] = d_xs
        dz_ref[:, 2 * SSD_D_INNER:3 * SSD_D_INNER] = d_bc
        dz_ref[:, 3 * SSD_D_INNER:] = d_dt
        d_state[...] = g_st
        d_hx[...] = g_hx * live
        d_hb[...] = g_hb * live
        for k in range(SSD_CONV):
            dcw_ref[k:k + 1, 0:SSD_D_INNER] += d_par[k]
            dcw_ref[k:k + 1, SSD_D_INNER:2 * SSD_D_INNER] += d_par[SSD_CONV + k]
        dcb_ref[:, 0:SSD_D_INNER] += d_par[8]
        dcb_ref[:, SSD_D_INNER:2 * SSD_D_INNER] += d_par[9]
        ddtb_ref[...] += d_par[10]
        dalog_ref[...] += d_par[11]
        ddsk_ref[...] += d_par[12]
        dng_ref[...] += d_par[13]

    const_specs = [pl.BlockSpec(c.shape, lambda i: (0, 0)) for c in consts]
    return pl.pallas_call(
        body, name=name,
        out_shape=[jax.ShapeDtypeStruct((lp, SSD_IN_PAD), F32)]
        + [jax.ShapeDtypeStruct(c.shape, F32) for c in consts],
        grid=(nc,),
        in_specs=_ssd_in_specs(True, nc)
        + [pl.BlockSpec((1, SSD_GROUPS * SSD_STATE, 256), lambda i: (nc - 1 - i, 0, 0)),
           pl.BlockSpec((CHUNK, SSD_D_INNER), lambda i: (nc - 1 - i, 0))] + const_specs,
        out_specs=[pl.BlockSpec((CHUNK, SSD_IN_PAD), lambda i: (nc - 1 - i, 0))] + const_specs,
        scratch_shapes=[pltpu.VMEM((SSD_GROUPS * SSD_STATE, 256), F32),
                        pltpu.VMEM((8, SSD_D_INNER), F32), pltpu.VMEM((8, SSD_D_INNER), F32)],
        compiler_params=_cparams(dimension_semantics=("arbitrary",)),
    )(zxd, zxd, zxd, zxd, zxd, zxd, states, d_y, *consts)


@jax.custom_vjp
def _rot_half(x):
    lane = lax.broadcasted_iota(jnp.int32, x.shape, 1)
    lo = (lane >= MLA_NOPE) & (lane < MLA_NOPE + MLA_ROPE // 2)
    hi = (lane >= MLA_NOPE + MLA_ROPE // 2) & (lane < MLA_QK)
    down = pltpu.roll(x, HEAD_SLOT - MLA_ROPE // 2, 1)
    up = pltpu.roll(x, MLA_ROPE // 2, 1)
    return jnp.where(lo, -down, jnp.where(hi, up, 0.0))


def _rot_half_fwd(x):
    return _rot_half(x), None


def _rot_half_bwd(_, ct):
    return (-_rot_half(ct),)


_rot_half.defvjp(_rot_half_fwd, _rot_half_bwd)


def _head_norm_rope(t, gain, cos, sin):
    n = t * lax.rsqrt(jnp.sum(t * t, axis=-1, keepdims=True) * (1.0 / MLA_QK) + EPS) * gain
    return n * cos + _rot_half(n) * sin


def _qk_prep(q_raw, kn_raw, kpe, cos, sin, qg, kg):
    qs, ks = [], []
    for h in range(MLA_HEADS):
        sl = slice(h * HEAD_SLOT, (h + 1) * HEAD_SLOT)
        qs.append(_head_norm_rope(q_raw[:, sl], qg, cos, sin))
        ks.append(_head_norm_rope(kn_raw[:, sl] + kpe, kg, cos, sin))
    return jnp.concatenate(qs, axis=1), jnp.concatenate(ks, axis=1)


def _lat_norm(kv_lat, q_lat, kvg, qg):
    return _rms(kv_lat, kvg), _rms(q_lat, qg)


def _attn_mask(qi, ki, tq, tk):
    qpos = qi * tq + lax.broadcasted_iota(jnp.int32, (tq, tk), 0)
    kpos = ki * tk + lax.broadcasted_iota(jnp.int32, (tq, tk), 1)
    return (kpos <= qpos) & ((kpos >= NPAD) | (kpos == qpos))


_NEG = -1e30
_SCALE = MLA_QK ** -0.5


def _attn_fwd(q, k, v, name):
    lp = q.shape[0]
    t = _pick(lp, (384, 256, 128))
    nb = lp // t

    def body(q_ref, k_ref, v_ref, o_ref, lse_ref, m_s, l_s, acc):
        qi, ki = pl.program_id(1), pl.program_id(2)

        @pl.when(ki == 0)
        def _():
            m_s[...] = jnp.full_like(m_s, _NEG)
            l_s[...] = jnp.zeros_like(l_s)
            acc[...] = jnp.zeros_like(acc)

        @pl.when(ki <= qi)
        def _():
            s = lax.dot_general(q_ref[...], k_ref[...], (((1,), (1,)), ((), ())),
                                preferred_element_type=F32) * _SCALE
            s = jnp.where(_attn_mask(qi, ki, t, t), s, _NEG)
            m_new = jnp.maximum(m_s[...], jnp.max(s, axis=-1, keepdims=True))
            alpha = jnp.exp(m_s[...] - m_new)
            p = jnp.exp(s - m_new)
            l_s[...] = alpha * l_s[...] + jnp.sum(p, axis=-1, keepdims=True)
            acc[...] = alpha * acc[...] + lax.dot_general(
                p.astype(BF16), v_ref[...], (((1,), (0,)), ((), ())), preferred_element_type=F32)
            m_s[...] = m_new

        @pl.when(ki == qi)
        def _():
            o_ref[...] = acc[...] / l_s[...] * _row_mask(qi, t)
            lse_ref[0] = m_s[...] + jnp.log(l_s[...])

    qspec = pl.BlockSpec((t, HEAD_SLOT), lambda h, i, j: (i, h))
    kspec = pl.BlockSpec((t, HEAD_SLOT), lambda h, i, j: (jnp.minimum(j, i), h))
    return pl.pallas_call(
        body, name=name,
        out_shape=[jax.ShapeDtypeStruct((lp, MLA_WIDE), F32),
                   jax.ShapeDtypeStruct((MLA_HEADS, lp, 1), F32)],
        grid=(MLA_HEADS, nb, nb),
        in_specs=[qspec, kspec, kspec],
        out_specs=[qspec, pl.BlockSpec((1, t, 1), lambda h, i, j: (h, i, 0))],
        scratch_shapes=[pltpu.VMEM((t, 1), F32), pltpu.VMEM((t, 1), F32), pltpu.VMEM((t, HEAD_SLOT), F32)],
        compiler_params=_cparams(dimension_semantics=("parallel", "parallel", "arbitrary")),
    )(q, k, v)


def _attn_p_ds(q, k, v, do, o, lse, qi, ki, t):
    s = lax.dot_general(q, k, (((1,), (1,)), ((), ())), preferred_element_type=F32) * _SCALE
    p = jnp.where(_attn_mask(qi, ki, t, t), jnp.exp(s - lse), 0.0)
    dp = lax.dot_general(do.astype(BF16), v, (((1,), (1,)), ((), ())), preferred_element_type=F32)
    delta = jnp.sum(do * o, axis=-1, keepdims=True)
    ds = p * (dp - delta)
    return p.astype(BF16), ds.astype(BF16)


def _attn_bwd_dq(q, k, v, do, o, lse, name):
    lp = q.shape[0]
    t = _pick(lp, (384, 256, 128))
    nb = lp // t

    def body(q_ref, k_ref, v_ref, do_ref, o_ref, lse_ref, dq_ref, acc):
        qi, ki = pl.program_id(1), pl.program_id(2)

        @pl.when(ki == 0)
        def _():
            acc[...] = jnp.zeros_like(acc)

        @pl.when(ki <= qi)
        def _():
            _, ds = _attn_p_ds(q_ref[...], k_ref[...], v_ref[...], do_ref[...], o_ref[...],
                               lse_ref[0], qi, ki, t)
            acc[...] += lax.dot_general(ds, k_ref[...], (((1,), (0,)), ((), ())),
                                        preferred_element_type=F32)

        @pl.when(ki == qi)
        def _():
            dq_ref[...] = acc[...] * _SCALE

    qspec = pl.BlockSpec((t, HEAD_SLOT), lambda h, i, j: (i, h))
    kspec = pl.BlockSpec((t, HEAD_SLOT), lambda h, i, j: (jnp.minimum(j, i), h))
    return pl.pallas_call(
        body, name=name,
        out_shape=jax.ShapeDtypeStruct((lp, MLA_WIDE), F32),
        grid=(MLA_HEADS, nb, nb),
        in_specs=[qspec, kspec, kspec, qspec, qspec, pl.BlockSpec((1, t, 1), lambda h, i, j: (h, i, 0))],
        out_specs=qspec,
        scratch_shapes=[pltpu.VMEM((t, HEAD_SLOT), F32)],
        compiler_params=_cparams(dimension_semantics=("parallel", "parallel", "arbitrary")),
    )(q, k, v, do, o, lse)


def _attn_bwd_dkv(q, k, v, do, o, lse, name):
    lp = q.shape[0]
    t = _pick(lp, (384, 256, 128))
    nb = lp // t

    def body(q_ref, k_ref, v_ref, do_ref, o_ref, lse_ref, dk_ref, dv_ref, dk_acc, dv_acc):
        ki, qi = pl.program_id(1), pl.program_id(2)

        @pl.when(qi == 0)
        def _():
            dk_acc[...] = jnp.zeros_like(dk_acc)
            dv_acc[...] = jnp.zeros_like(dv_acc)

        @pl.when(qi >= ki)
        def _():
            p, ds = _attn_p_ds(q_ref[...], k_ref[...], v_ref[...], do_ref[...], o_ref[...],
                               lse_ref[0], qi, ki, t)
            dv_acc[...] += lax.dot_general(p, do_ref[...].astype(BF16), (((0,), (0,)), ((), ())),
                                           preferred_element_type=F32)
            dk_acc[...] += lax.dot_general(ds, q_ref[...], (((0,), (0,)), ((), ())),
                                           preferred_element_type=F32)

        @pl.when(qi == nb - 1)
        def _():
            dk_ref[...] = dk_acc[...] * _SCALE
            dv_ref[...] = dv_acc[...]

    qspec = pl.BlockSpec((t, HEAD_SLOT), lambda h, j, i: (jnp.maximum(i, j), h))
    kspec = pl.BlockSpec((t, HEAD_SLOT), lambda h, j, i: (j, h))
    return pl.pallas_call(
        body, name=name,
        out_shape=[jax.ShapeDtypeStruct((lp, MLA_WIDE), F32)] * 2,
        grid=(MLA_HEADS, nb, nb),
        in_specs=[qspec, kspec, kspec, qspec, qspec,
                  pl.BlockSpec((1, t, 1), lambda h, j, i: (h, jnp.maximum(i, j), 0))],
        out_specs=[kspec, kspec],
        scratch_shapes=[pltpu.VMEM((t, HEAD_SLOT), F32)] * 2,
        compiler_params=_cparams(dimension_semantics=("parallel", "parallel", "arbitrary")),
    )(q, k, v, do, o, lse)


def _rope_tables(lp):
    inv = 1.0 / (ROPE_THETA ** (jnp.arange(0, MLA_ROPE, 2, dtype=F32) / MLA_ROPE))
    pos = jnp.maximum(jnp.arange(lp, dtype=jnp.int32) - NPAD, 0).astype(F32)
    ang = pos[:, None] * inv[None, :]
    cos, sin = jnp.cos(ang), jnp.sin(ang)
    z32 = jnp.zeros((lp, HEAD_SLOT - MLA_QK), F32)
    cos_t = jnp.concatenate([jnp.ones((lp, MLA_NOPE), F32), cos, cos, z32], axis=1)
    sin_t = jnp.concatenate([jnp.zeros((lp, MLA_NOPE), F32), sin, sin, z32], axis=1)
    return cos_t, sin_t


def _loss_head(h, target, name):
    lp = h.shape[0]

    def body(h_ref, t_ref, d_ref, loss_ref):
        i = pl.program_id(0)

        @pl.when(i == 0)
        def _():
            d_ref[...] = jnp.zeros_like(d_ref)
            loss_ref[...] = jnp.zeros_like(loss_ref)

        @pl.when(i > 0)
        def _():
            err = h_ref[...] - t_ref[...]
            d_ref[...] = err * (1.0 / D_MODEL)
            loss_ref[...] += jnp.sum(err * err, axis=0, keepdims=True) * (0.5 / D_MODEL)

    return pl.pallas_call(
        body, name=name,
        out_shape=[jax.ShapeDtypeStruct((lp, D_MODEL), F32), jax.ShapeDtypeStruct((1, D_MODEL), F32)],
        grid=(lp // CHUNK,),
        in_specs=[pl.BlockSpec((CHUNK, D_MODEL), lambda i: (i, 0)),
                  pl.BlockSpec((CHUNK, D_MODEL), lambda i: (jnp.maximum(i - 1, 0), 0))],
        out_specs=[pl.BlockSpec((CHUNK, D_MODEL), lambda i: (i, 0)),
                   pl.BlockSpec((1, D_MODEL), lambda i: (0, 0))],
        compiler_params=_cparams(dimension_semantics=("arbitrary",)),
    )(h, target)


def _pad_cols(w, n):
    return jnp.pad(w, [(0, 0)] * (w.ndim - 1) + [(0, n - w.shape[-1])])


def _prep_weights(w):
    p = {}
    p['ssd_in'] = [_pad_cols(w['ssd_w_in'][j], SSD_IN_PAD).astype(BF16) for j in range(2)]
    p['ssd_out'] = [w['ssd_w_out'][j].astype(BF16) for j in range(2)]
    p['mla_in'], p['mla_qb'], p['mla_kvb'], p['mla_out'] = [], [], [], []
    for j in range(2):
        wi = w['mla_w_in'][j]
        kpe = jnp.pad(wi[:, MLA_Q_RANK + MLA_KV_RANK:], ((0, 0), (MLA_NOPE, HEAD_SLOT - MLA_QK)))
        p['mla_in'].append(jnp.concatenate(
            [wi[:, MLA_Q_RANK:MLA_Q_RANK + MLA_KV_RANK], kpe, wi[:, :MLA_Q_RANK]], axis=1).astype(BF16))
        qb = w['mla_w_q_b'][j].reshape(MLA_Q_RANK, MLA_HEADS, MLA_QK)
        p['mla_qb'].append(_pad_cols(qb, HEAD_SLOT).reshape(MLA_Q_RANK, MLA_WIDE).astype(BF16))
        kvb = w['mla_w_kv_b'][j].reshape(MLA_KV_RANK, MLA_HEADS, MLA_NOPE + MLA_V)
        kn = _pad_cols(kvb[:, :, :MLA_NOPE], HEAD_SLOT).reshape(MLA_KV_RANK, MLA_WIDE)
        vv = _pad_cols(kvb[:, :, MLA_NOPE:], HEAD_SLOT).reshape(MLA_KV_RANK, MLA_WIDE)
        p['mla_kvb'].append(jnp.concatenate([kn, vv], axis=1).astype(BF16))
        wo = w['mla_w_out'][j].reshape(MLA_HEADS, MLA_V, D_MODEL)
        p['mla_out'].append(jnp.pad(wo, ((0, 0), (0, HEAD_SLOT - MLA_V), (0, 0)))
                            .reshape(MLA_WIDE, D_MODEL).astype(BF16))
    p['up'] = [w['mlp_w_up'][i].astype(BF16) for i in range(4)]
    p['down'] = [w['mlp_w_down'][i].astype(BF16) for i in range(4)]
    return p


def _pad128(v):
    return _pad_cols(v.reshape(1, -1), 128)


def _sqrelu(u):
    r = jnp.maximum(u, 0.0)
    return r * r


def _local_step(x, target, w):
    seq = x.shape[0]
    lp = NPAD + N_META + seq
    p = _prep_weights(w)
    h = jnp.concatenate([jnp.zeros((NPAD, D_MODEL), F32), w['meta_tokens'], x], axis=0)
    cos_t, sin_t = _rope_tables(lp)
    rt = _pick(lp, (384, 256, 128))
    saved = []
    for i in range(4):
        j = i // 2
        s = {'h0': h}
        g_mix = w['ln_mix'][i].reshape(1, -1)
        hn = _rms_fwd(h, g_mix, f"rms_mix_f{i}")
        s['hn'] = hn
        if i % 2 == 0:
            zxd = _mm(hn, p['ssd_in'][j], 'nn', name=f"ssd_in_f{i}")
            consts = _ssd_consts(w['ssd_conv_w'][j], w['ssd_conv_b'][j].reshape(1, -1),
                                 _pad128(w['ssd_dt_bias'][j]), _pad128(w['ssd_a_log'][j]),
                                 _pad128(w['ssd_d'][j]), w['ssd_norm'][j].reshape(1, -1))
            yg, states = _ssd_fwd(zxd, consts, f"ssd_core_f{i}")
            s.update(zxd=zxd, consts=consts, yg=yg, states=states)
            h = _mm(yg, p['ssd_out'][j], 'nn', name=f"ssd_out_f{i}", epi=lambda r, hv: hv + r, extras=(h,))
        else:
            lat = _mm(hn, p['mla_in'][j], 'nn', name=f"mla_in_f{i}")
            kvg = w['mla_kv_a_norm'][j].reshape(1, -1)
            qag = w['mla_q_a_norm'][j].reshape(1, -1)
            kvn, qn = _row_call(lambda _, a, b, c, d: _lat_norm(a, b, c, d),
                                [(lat, MLA_KV_RANK, 0), (lat, MLA_Q_RANK, 1)], [kvg, qag],
                                [(MLA_KV_RANK, BF16), (MLA_Q_RANK, BF16)], n_rows=lp, tile=rt,
                                name=f"mla_latnorm_f{i}")
            q_raw = _mm(qn, p['mla_qb'][j], 'nn', name=f"mla_qb_f{i}")
            kv_raw = _mm(kvn, p['mla_kvb'][j], 'nn', name=f"mla_kvb_f{i}")
            qg = _pad_cols(w['mla_q_norm'][j].reshape(1, -1), HEAD_SLOT)
            kg = _pad_cols(w['mla_k_norm'][j].reshape(1, -1), HEAD_SLOT)

            def prep_fwd(_, qr, kn, kpe, vv, cs, sn, qgv, kgv):
                qq, kk = _qk_prep(qr, kn, kpe, cs, sn, qgv, kgv)
                return qq, kk, vv

            q, k, v = _row_call(prep_fwd,
                                [(q_raw, MLA_WIDE, 0), (kv_raw, MLA_WIDE, 0), (lat, HEAD_SLOT, 2),
                                 (kv_raw, MLA_WIDE, 1), (cos_t, HEAD_SLOT, 0), (sin_t, HEAD_SLOT, 0)],
                                [qg, kg], [(MLA_WIDE, BF16)] * 3, n_rows=lp, tile=rt,
                                name=f"mla_qkprep_f{i}")
            o, lse = _attn_fwd(q, k, v, f"mla_attn_f{i}")
            s.update(lat=lat, kvg=kvg, qag=qag, kvn=kvn, qn=qn, q_raw=q_raw, kv_raw=kv_raw, qg=qg, kg=kg,
                     q=q, k=k, v=v, o=o, lse=lse)
            h = _mm(o, p['mla_out'][j], 'nn', name=f"mla_out_f{i}", epi=lambda r, hv: hv + r, extras=(h,))
        s['h1'] = h
        g_mlp = w['ln_mlp'][i].reshape(1, -1)
        hn2 = _rms_fwd(h, g_mlp, f"rms_mlp_f{i}")
        u = _mm(hn2, p['up'][i], 'nn', name=f"mlp_up_f{i}")
        h = _mm(u, p['down'][i], 'nn', name=f"mlp_down_f{i}", a_fn=_sqrelu,
                epi=lambda r, hv: hv + r, extras=(h,))
        s.update(hn2=hn2, u=u, g_mix=g_mix, g_mlp=g_mlp)
        saved.append(s)

    dh, loss_row = _loss_head(h, target, "loss_head")

    g = {k_: [None] * (4 if k_ in ('ln_mix', 'ln_mlp', 'mlp_w_up', 'mlp_w_down') else 2)
         for k_ in w if k_ != 'meta_tokens'}
    for i in reversed(range(4)):
        j = i // 2
        s = saved[i]
        g['mlp_w_down'][i] = _mm(s['u'], dh, 'tn', name=f"mlp_down_dw{i}", a_fn=_sqrelu)
        du = _mm(dh, p['down'][i], 'nt', name=f"mlp_down_dx{i}",
                 epi=lambda r, uv: r * (2.0 * jnp.maximum(uv, 0.0)), extras=(s['u'],))
        g['mlp_w_up'][i] = _mm(s['hn2'], du, 'tn', name=f"mlp_up_dw{i}")
        d_hn2 = _mm(du, p['up'][i], 'nt', name=f"mlp_up_dx{i}")
        dh, dg = _rms_bwd(s['h1'], s['g_mlp'], d_hn2, dh, f"rms_mlp_b{i}")
        g['ln_mlp'][i] = dg[0]
        if i % 2 == 0:
            g['ssd_w_out'][j] = _mm(s['yg'], dh, 'tn', name=f"ssd_out_dw{i}")
            d_yg = _mm(dh, p['ssd_out'][j], 'nt', name=f"ssd_out_dx{i}")
            d_zxd, dcw, dcb, ddtb, dalog, ddsk, dng = _ssd_bwd(s['zxd'], s['states'], d_yg, s['consts'],
                                                              f"ssd_core_b{i}")
            g['ssd_conv_w'][j], g['ssd_conv_b'][j], g['ssd_norm'][j] = dcw, dcb[0], dng[0]
            g['ssd_dt_bias'][j], g['ssd_a_log'][j], g['ssd_d'][j] = (
                ddtb[0, :SSD_HEADS], dalog[0, :SSD_HEADS], ddsk[0, :SSD_HEADS])
            g['ssd_w_in'][j] = _mm(s['hn'], d_zxd, 'tn', name=f"ssd_in_dw{i}")[:, :SSD_IN_DIM]
            d_hn = _mm(d_zxd, p['ssd_in'][j], 'nt', name=f"ssd_in_dx{i}")
        else:
            wo = _mm(s['o'], dh, 'tn', name=f"mla_out_dw{i}")
            g['mla_w_out'][j] = wo.reshape(MLA_HEADS, HEAD_SLOT, D_MODEL)[:, :MLA_V].reshape(-1, D_MODEL)
            do = _mm(dh, p['mla_out'][j], 'nt', name=f"mla_out_dx{i}")
            dq = _attn_bwd_dq(s['q'], s['k'], s['v'], do, s['o'], s['lse'], f"mla_attn_dq{i}")
            dk, dv = _attn_bwd_dkv(s['q'], s['k'], s['v'], do, s['o'], s['lse'], f"mla_attn_dkv{i}")

            def prep_bwd(_, qr, kn, kpe, cs, sn, dqv, dkv, dvv, qgv, kgv):
                _, vjp = jax.vjp(lambda a, b, c, d, e: _qk_prep(a, b, c, cs, sn, d, e), qr, kn, kpe, qgv, kgv)
                d_qr, d_kn, d_kpe, d_qg, d_kg = vjp((dqv, dkv))
                return d_qr, jnp.concatenate([d_kn, dvv], axis=1), d_kpe, d_qg, d_kg

            d_qraw, d_kvraw, d_kpe, d_qg, d_kg = _row_call(
                prep_bwd,
                [(s['q_raw'], MLA_WIDE, 0), (s['kv_raw'], MLA_WIDE, 0), (s['lat'], HEAD_SLOT, 2),
                 (cos_t, HEAD_SLOT, 0), (sin_t, HEAD_SLOT, 0), (dq, MLA_WIDE, 0), (dk, MLA_WIDE, 0),
                 (dv, MLA_WIDE, 0)],
                [s['qg'], s['kg']], [(MLA_WIDE, F32), (2 * MLA_WIDE, F32), (HEAD_SLOT, F32)],
                [(1, HEAD_SLOT), (1, HEAD_SLOT)], n_rows=lp, tile=_pick(lp, (128,)), name=f"mla_qkprep_b{i}")
            g['mla_q_norm'][j], g['mla_k_norm'][j] = d_qg[0, :MLA_QK], d_kg[0, :MLA_QK]
            wqb = _mm(s['qn'], d_qraw, 'tn', name=f"mla_qb_dw{i}")
            g['mla_w_q_b'][j] = wqb.reshape(MLA_Q_RANK, MLA_HEADS, HEAD_SLOT)[:, :, :MLA_QK].reshape(MLA_Q_RANK, -1)
            d_qn = _mm(d_qraw, p['mla_qb'][j], 'nt', name=f"mla_qb_dx{i}")
            wkvb = _mm(s['kvn'], d_kvraw, 'tn', name=f"mla_kvb_dw{i}").reshape(MLA_KV_RANK, 2, MLA_HEADS, HEAD_SLOT)
            g['mla_w_kv_b'][j] = jnp.concatenate([wkvb[:, 0, :, :MLA_NOPE], wkvb[:, 1, :, :MLA_V]],
                                                 axis=-1).reshape(MLA_KV_RANK, -1)
            d_kvn = _mm(d_kvraw, p['mla_kvb'][j], 'nt', name=f"mla_kvb_dx{i}")

            def lat_bwd(_, kvl, ql, dkvn, dqn, dkpe, kvgv, qagv):
                _, vjp = jax.vjp(_lat_norm, kvl, ql, kvgv, qagv)
                d_kvl, d_ql, d_kvg, d_qag = vjp((dkvn, dqn))
                return jnp.concatenate([d_kvl, dkpe, d_ql], axis=1), d_kvg, d_qag

            d_lat, d_kvg, d_qag = _row_call(
                lat_bwd, [(s['lat'], MLA_KV_RANK, 0), (s['lat'], MLA_Q_RANK, 1), (d_kvn, MLA_KV_RANK, 0),
                          (d_qn, MLA_Q_RANK, 0), (d_kpe, HEAD_SLOT, 0)],
                [s['kvg'], s['qag']], [(LAT_PAD, F32)], [(1, MLA_KV_RANK), (1, MLA_Q_RANK)],
                n_rows=lp, tile=rt, name=f"mla_latnorm_b{i}")
            g['mla_kv_a_norm'][j], g['mla_q_a_norm'][j] = d_kvg[0], d_qag[0]
            win = _mm(s['hn'], d_lat, 'tn', name=f"mla_in_dw{i}")
            g['mla_w_in'][j] = jnp.concatenate(
                [win[:, MLA_KV_RANK + HEAD_SLOT:], win[:, :MLA_KV_RANK],
                 win[:, MLA_KV_RANK + MLA_NOPE:MLA_KV_RANK + MLA_QK]], axis=1)
            d_hn = _mm(d_lat, p['mla_in'][j], 'nt', name=f"mla_in_dx{i}")
        dh, dg = _rms_bwd(s['h0'], s['g_mix'], d_hn, dh, f"rms_mix_b{i}")
        g['ln_mix'][i] = dg[0]

    grads = {k_: jnp.stack(v_) for k_, v_ in g.items()}
    grads['meta_tokens'] = dh[NPAD:NPAD + N_META]
    return loss_row, dh[NPAD + N_META:], grads


def _all_gather8(shard, name):
    m_per, n = shard.shape

    def body(x_ref, out_ref, send_sems, recv_sems, local_sem):
        x, y, c = lax.axis_index("x"), lax.axis_index("y"), lax.axis_index("c")
        me, sibling = (x, y, c), (x, y, 1 - c)
        chips = [(1 - x, y), (x, 1 - y), (1 - x, 1 - y)]

        def rows(px, py, pc):
            return out_ref.at[pl.ds((4 * px + 2 * py + pc) * m_per, m_per), :]

        def copy(k, block, to, src=None):
            return pltpu.make_async_remote_copy(
                src_ref=rows(*block) if src is None else src, dst_ref=rows(*block),
                send_sem=send_sems.at[k], recv_sem=recv_sems.at[k], device_id=to, device_id_type=MESH)

        mine = pltpu.make_async_copy(x_ref, rows(*me), local_sem)
        mine.start()
        first = [copy(0, me, sibling, src=x_ref)]
        first += [copy(1 + j, me, (*chip, c), src=x_ref) for j, chip in enumerate(chips)]
        for cp in first:
            cp.start()
        passed = [copy(4 + j, (*chip, c), sibling) for j, chip in enumerate(chips)]
        for j, chip in enumerate(chips):
            copy(1 + j, (*chip, c), me).wait_recv()
            passed[j].start()
        copy(0, sibling, me).wait_recv()
        for j, chip in enumerate(chips):
            copy(4 + j, (*chip, 1 - c), me).wait_recv()
        for cp in first + passed:
            cp.wait_send()
        mine.wait()

    return pl.pallas_call(
        body, name=name,
        out_shape=jax.ShapeDtypeStruct((8 * m_per, n), shard.dtype),
        in_specs=[pl.BlockSpec(memory_space=pl.ANY)],
        out_specs=pl.BlockSpec(memory_space=pl.ANY),
        scratch_shapes=[pltpu.SemaphoreType.DMA((7,)), pltpu.SemaphoreType.DMA((7,)), pltpu.SemaphoreType.DMA],
    )(shard)


def _sibling_swap(src, name):
    _, m, n = src.shape

    def body(src_ref, out_ref, send_sem, recv_sem):
        x, y, c = lax.axis_index("x"), lax.axis_index("y"), lax.axis_index("c")
        cp = pltpu.make_async_remote_copy(src_ref=src_ref.at[1 - c], dst_ref=out_ref, send_sem=send_sem,
                                          recv_sem=recv_sem, device_id=(x, y, 1 - c), device_id_type=MESH)
        cp.start()
        cp.wait()

    return pl.pallas_call(
        body, name=name, out_shape=jax.ShapeDtypeStruct((m, n), src.dtype),
        in_specs=[pl.BlockSpec(memory_space=pl.ANY)], out_specs=pl.BlockSpec(memory_space=pl.ANY),
        scratch_shapes=[pltpu.SemaphoreType.DMA, pltpu.SemaphoreType.DMA],
    )(src)


def _chip_exchange(part, name):
    _, m, n = part.shape

    def body(p_ref, out_ref, send_sems, recv_sems, local_sem):
        x, y, c = lax.axis_index("x"), lax.axis_index("y"), lax.axis_index("c")
        kme = 2 * x + y
        chips = [(1 - x, y), (x, 1 - y), (1 - x, 1 - y)]
        mine = pltpu.make_async_copy(p_ref.at[kme], out_ref.at[kme], local_sem)
        mine.start()
        cps = []
        for j, (px, py) in enumerate(chips):
            cps.append(pltpu.make_async_remote_copy(
                src_ref=p_ref.at[2 * px + py], dst_ref=out_ref.at[kme], send_sem=send_sems.at[j],
                recv_sem=recv_sems.at[j], device_id=(px, py, c), device_id_type=MESH))
        for cp in cps:
            cp.start()
        for cp in cps:
            cp.wait()
        mine.wait()

    return pl.pallas_call(
        body, name=name, out_shape=jax.ShapeDtypeStruct(part.shape, part.dtype),
        in_specs=[pl.BlockSpec(memory_space=pl.ANY)], out_specs=pl.BlockSpec(memory_space=pl.ANY),
        scratch_shapes=[pltpu.SemaphoreType.DMA((3,)), pltpu.SemaphoreType.DMA((3,)), pltpu.SemaphoreType.DMA],
    )(part)


def _sibling_share(mine, name):
    m, n = mine.shape

    def body(src_ref, out_ref, send_sem, recv_sem, local_sem):
        x, y, c = lax.axis_index("x"), lax.axis_index("y"), lax.axis_index("c")
        loc = pltpu.make_async_copy(src_ref, out_ref.at[c], local_sem)
        loc.start()
        cp = pltpu.make_async_remote_copy(src_ref=src_ref, dst_ref=out_ref.at[c], send_sem=send_sem,
                                          recv_sem=recv_sem, device_id=(x, y, 1 - c), device_id_type=MESH)
        cp.start()
        cp.wait()
        loc.wait()

    return pl.pallas_call(
        body, name=name, out_shape=jax.ShapeDtypeStruct((2, m, n), mine.dtype),
        in_specs=[pl.BlockSpec(memory_space=pl.ANY)], out_specs=pl.BlockSpec(memory_space=pl.ANY),
        scratch_shapes=[pltpu.SemaphoreType.DMA, pltpu.SemaphoreType.DMA, pltpu.SemaphoreType.DMA],
    )(mine)


def _add_own_half(g4, recv, c_idx, name):
    _, _, m, n = g4.shape
    t = _pick(m, (856, 512, 256, 128, 64, 32, 16, 8))

    def body(c_ref, g_ref, r_ref, o_ref):
        o_ref[...] = (g_ref[...][:, 0] + r_ref[...]).astype(o_ref.dtype)

    return pl.pallas_call(
        body, name=name, out_shape=jax.ShapeDtypeStruct((4, m, n), BF16),
        grid_spec=pltpu.PrefetchScalarGridSpec(
            num_scalar_prefetch=1, grid=(4, m // t),
            in_specs=[pl.BlockSpec((1, 1, t, n), lambda k, i, c: (k, c[0], i, 0)),
                      pl.BlockSpec((1, t, n), lambda k, i, c: (k, i, 0))],
            out_specs=pl.BlockSpec((1, t, n), lambda k, i, c: (k, i, 0))),
        compiler_params=_cparams(dimension_semantics=("parallel", "parallel")),
    )(c_idx, g4, recv)


def _sum4(parts, name):
    _, m, n = parts.shape
    t = _pick(m, (856, 512, 256, 128, 64, 32, 16, 8))

    def body(p_ref, o_ref):
        pv = p_ref[...].astype(F32)
        o_ref[...] = ((pv[0] + pv[1]) + pv[2]) + pv[3]

    return pl.pallas_call(
        body, name=name, out_shape=jax.ShapeDtypeStruct((m, n), F32), grid=(m // t,),
        in_specs=[pl.BlockSpec((4, t, n), lambda i: (0, i, 0))],
        out_specs=pl.BlockSpec((t, n), lambda i: (i, 0)),
        compiler_params=_cparams(dimension_semantics=("parallel",)),
    )(parts)


def _sum8(parts, name):
    _, m, n = parts.shape

    def body(p_ref, o_ref):
        acc = p_ref[0]
        for d in range(1, 8):
            acc = acc + p_ref[d]
        o_ref[...] = acc

    return pl.pallas_call(body, name=name, out_shape=jax.ShapeDtypeStruct((m, n), F32))(parts)


def _adamw(wp, gp, mp, vp, name):
    r, n = wp.shape
    t = _pick(r, (256, 128, 64, 40, 32, 16, 8))

    def body(w_ref, g_ref, m_ref, v_ref, d_ref, mo_ref, vo_ref):
        gv = g_ref[...]
        m2 = ADAM_B1 * m_ref[...] + (1.0 - ADAM_B1) * gv
        v2 = ADAM_B2 * v_ref[...] + (1.0 - ADAM_B2) * (gv * gv)
        m_hat = m2 / (1.0 - ADAM_B1 ** ADAM_STEP)
        v_hat = v2 / (1.0 - ADAM_B2 ** ADAM_STEP)
        d_ref[...] = -ADAM_LR * (m_hat / (jnp.sqrt(v_hat) + ADAM_EPS) + ADAM_WD * w_ref[...])
        mo_ref[...] = m2
        vo_ref[...] = v2

    spec = pl.BlockSpec((t, n), lambda i: (i, 0))
    return pl.pallas_call(
        body, name=name, out_shape=[jax.ShapeDtypeStruct((r, n), F32)] * 3, grid=(r // t,),
        in_specs=[spec] * 4, out_specs=[spec] * 3,
        compiler_params=_cparams(dimension_semantics=("parallel",)),
    )(wp, gp, mp, vp)


BIG = (('ssd_w_in', 2), ('ssd_w_out', 1), ('mla_w_in', 1), ('mla_w_q_b', 2), ('mla_w_kv_b', 2),
       ('mla_w_out', 1), ('mlp_w_up', 2), ('mlp_w_down', 1))
SMALL_SHARDED = (('meta_tokens', 1), ('ssd_conv_w', 2), ('mla_q_a_norm', 1), ('mla_kv_a_norm', 1))
SMALL_REPL = ('ln_mix', 'ln_mlp', 'ssd_conv_b', 'ssd_dt_bias', 'ssd_a_log', 'ssd_d', 'ssd_norm',
              'mla_q_norm', 'mla_k_norm')
ALL_NAMES = ('meta_tokens', 'ln_mix', 'ln_mlp', 'ssd_w_in', 'ssd_conv_w', 'ssd_conv_b', 'ssd_dt_bias',
             'ssd_a_log', 'ssd_d', 'ssd_norm', 'ssd_w_out', 'mla_w_in', 'mla_q_a_norm', 'mla_w_q_b',
             'mla_kv_a_norm', 'mla_w_kv_b', 'mla_q_norm', 'mla_k_norm', 'mla_w_out', 'mlp_w_up', 'mlp_w_down')


def _pack(arrs, rows_mult):
    flat = jnp.concatenate([a.reshape(-1) for a in arrs])
    per = LANES * rows_mult
    pad = (-flat.shape[0]) % per
    if pad:
        flat = jnp.concatenate([flat, jnp.zeros((pad,), flat.dtype)])
    return flat.reshape(-1, LANES)


def _unpack(pack, shapes):
    flat = pack.reshape(-1)
    out, off = [], 0
    for shp in shapes:
        n = math.prod(shp)
        out.append(flat[off:off + n].reshape(shp))
        off += n
    return out


def _split4(full, axis):
    shp = full.shape
    r = full.reshape(shp[:axis] + (4, shp[axis] // 4) + shp[axis + 1:])
    return jnp.moveaxis(r, axis, 0)


def _join4(parts, axis):
    r = jnp.moveaxis(parts, 0, axis)
    shp = r.shape
    return r.reshape(shp[:axis] + (shp[axis] * shp[axis + 1],) + shp[axis + 2:])


def _gather_params(shards, table, dtype, c, name):
    pack = _pack([shards[n].astype(dtype) for n, _ in table], 16)
    half = pack.shape[0] // 2
    mine = lax.dynamic_slice_in_dim(pack, c * half, half, axis=0)
    full = _all_gather8(mine, name).reshape(4, -1)
    out, off = {}, 0
    for n, ax in table:
        cnt = math.prod(shards[n].shape)
        out[n] = _join4(full[:, off:off + cnt].reshape((4,) + shards[n].shape), ax)
        off += cnt
    return out


def kernel(x, meta_tokens, ln_mix, ln_mlp, ssd_w_in, ssd_conv_w, ssd_conv_b, ssd_dt_bias, ssd_a_log, ssd_d, ssd_norm, ssd_w_out, mla_w_in, mla_q_a_norm, mla_w_q_b, mla_kv_a_norm, mla_w_kv_b, mla_q_norm, mla_k_norm, mla_w_out, mlp_w_up, mlp_w_down, loss_target, m_meta_tokens, m_ln_mix, m_ln_mlp, m_ssd_w_in, m_ssd_conv_w, m_ssd_conv_b, m_ssd_dt_bias, m_ssd_a_log, m_ssd_d, m_ssd_norm, m_ssd_w_out, m_mla_w_in, m_mla_q_a_norm, m_mla_w_q_b, m_mla_kv_a_norm, m_mla_w_kv_b, m_mla_q_norm, m_mla_k_norm, m_mla_w_out, m_mlp_w_up, m_mlp_w_down, v_meta_tokens, v_ln_mix, v_ln_mlp, v_ssd_w_in, v_ssd_conv_w, v_ssd_conv_b, v_ssd_dt_bias, v_ssd_a_log, v_ssd_d, v_ssd_norm, v_ssd_w_out, v_mla_w_in, v_mla_q_a_norm, v_mla_w_q_b, v_mla_kv_a_norm, v_mla_w_kv_b, v_mla_q_norm, v_mla_k_norm, v_mla_w_out, v_mlp_w_up, v_mlp_w_down):
    w_sh = dict(meta_tokens=meta_tokens, ln_mix=ln_mix, ln_mlp=ln_mlp, ssd_w_in=ssd_w_in, ssd_conv_w=ssd_conv_w, ssd_conv_b=ssd_conv_b, ssd_dt_bias=ssd_dt_bias, ssd_a_log=ssd_a_log, ssd_d=ssd_d, ssd_norm=ssd_norm, ssd_w_out=ssd_w_out, mla_w_in=mla_w_in, mla_q_a_norm=mla_q_a_norm, mla_w_q_b=mla_w_q_b, mla_kv_a_norm=mla_kv_a_norm, mla_w_kv_b=mla_w_kv_b, mla_q_norm=mla_q_norm, mla_k_norm=mla_k_norm, mla_w_out=mla_w_out, mlp_w_up=mlp_w_up, mlp_w_down=mlp_w_down)
    m_sh = dict(meta_tokens=m_meta_tokens, ln_mix=m_ln_mix, ln_mlp=m_ln_mlp, ssd_w_in=m_ssd_w_in, ssd_conv_w=m_ssd_conv_w, ssd_conv_b=m_ssd_conv_b, ssd_dt_bias=m_ssd_dt_bias, ssd_a_log=m_ssd_a_log, ssd_d=m_ssd_d, ssd_norm=m_ssd_norm, ssd_w_out=m_ssd_w_out, mla_w_in=m_mla_w_in, mla_q_a_norm=m_mla_q_a_norm, mla_w_q_b=m_mla_w_q_b, mla_kv_a_norm=m_mla_kv_a_norm, mla_w_kv_b=m_mla_w_kv_b, mla_q_norm=m_mla_q_norm, mla_k_norm=m_mla_k_norm, mla_w_out=m_mla_w_out, mlp_w_up=m_mlp_w_up, mlp_w_down=m_mlp_w_down)
    v_sh = dict(meta_tokens=v_meta_tokens, ln_mix=v_ln_mix, ln_mlp=v_ln_mlp, ssd_w_in=v_ssd_w_in, ssd_conv_w=v_ssd_conv_w, ssd_conv_b=v_ssd_conv_b, ssd_dt_bias=v_ssd_dt_bias, ssd_a_log=v_ssd_a_log, ssd_d=v_ssd_d, ssd_norm=v_ssd_norm, ssd_w_out=v_ssd_w_out, mla_w_in=v_mla_w_in, mla_q_a_norm=v_mla_q_a_norm, mla_w_q_b=v_mla_w_q_b, mla_kv_a_norm=v_mla_kv_a_norm, mla_w_kv_b=v_mla_w_kv_b, mla_q_norm=v_mla_q_norm, mla_k_norm=v_mla_k_norm, mla_w_out=v_mla_w_out, mlp_w_up=v_mlp_w_up, mlp_w_down=v_mlp_w_down)

    cx, cy, cc = lax.axis_index("x"), lax.axis_index("y"), lax.axis_index("c")
    chip = 2 * cx + cy

    w = {n: w_sh[n] for n in SMALL_REPL}
    w.update(_gather_params(w_sh, BIG, BF16, cc, "gather_big"))
    w.update(_gather_params(w_sh, SMALL_SHARDED, F32, cc, "gather_small"))

    loss_row, grad_x, grads = _local_step(x[0], loss_target[0], w)
    loss = lax.psum(jnp.sum(loss_row), ("x", "y", "c"))

    g4 = jnp.concatenate([_split4(grads[n], ax).reshape(4, -1) for n, ax in BIG], axis=1)
    rows = g4.shape[1] // LANES
    g4 = g4.reshape(4, 2, rows // 2, LANES)
    other = _sibling_swap(jnp.moveaxis(g4, 1, 0).reshape(2, 4 * (rows // 2), LANES), "rs_sibling_swap")
    part = _add_own_half(g4, other.reshape(4, rows // 2, LANES), cc.reshape(1).astype(jnp.int32),
                         "rs_add_own")
    recv = _chip_exchange(part, "rs_chip_exchange")
    mine = _sum4(recv, "rs_sum4")
    big_pack = _sibling_share(mine, "rs_sibling_share").reshape(rows, LANES)

    small_names = tuple(n for n, _ in SMALL_SHARDED) + SMALL_REPL
    sp = _pack([grads[n] for n in small_names], 8)
    srows = sp.shape[0]
    s_all = _sum8(_all_gather8(sp, "ar_small_gather").reshape(8, srows, LANES), "ar_small_sum")
    s_full = dict(zip(small_names, _unpack(s_all, [grads[n].shape for n in small_names])))

    g_sh = dict(zip([n for n, _ in BIG], _unpack(big_pack, [w_sh[n].shape for n, _ in BIG])))
    for n, ax in SMALL_SHARDED:
        g_sh[n] = lax.dynamic_index_in_dim(_split4(s_full[n], ax), chip, axis=0, keepdims=False)
    for n in SMALL_REPL:
        g_sh[n] = s_full[n]

    big_names = [n for n, _ in BIG]
    d_b, m_b, v_b = _adamw(_pack([w_sh[n] for n in big_names], 16), big_pack,
                           _pack([m_sh[n] for n in big_names], 16),
                           _pack([v_sh[n] for n in big_names], 16), "adamw_big")
    d_s, m_s, v_s = _adamw(*[_pack([t[n] for n in small_names], 8) for t in (w_sh, g_sh, m_sh, v_sh)],
                           "adamw_small")
    delta, new_m, new_v = {}, {}, {}
    for dst, pb, ps in ((delta, d_b, d_s), (new_m, m_b, m_s), (new_v, v_b, v_s)):
        dst.update(zip(big_names, _unpack(pb, [w_sh[n].shape for n in big_names])))
        dst.update(zip(small_names, _unpack(ps, [w_sh[n].shape for n in small_names])))

    return (loss, grad_x[None], *[g_sh[n] for n in ALL_NAMES], *[delta[n] for n in ALL_NAMES],
            *[new_m[n] for n in ALL_NAMES], *[new_v[n] for n in ALL_NAMES])
```

```python
import functools
import math

import jax
import jax.numpy as jnp
from jax import lax
from jax.experimental import pallas as pl
from jax.experimental.pallas import tpu as pltpu

F32 = jnp.float32
BF16 = jnp.bfloat16
HI = lax.Precision.HIGHEST
MESH = pl.DeviceIdType.MESH

D_MODEL = 1024
N_META = 16
EPS = 1e-6
SSD_D_INNER = 2048
SSD_HEADS = 32
SSD_HEAD_DIM = 64
SSD_GROUPS = 8
SSD_HPG = 4
SSD_STATE = 128
SSD_CONV = 4
CHUNK = 128
SSD_IN_DIM = 6176
SSD_IN_PAD = 6272
MLA_HEADS = 16
MLA_NOPE = 64
MLA_ROPE = 32
MLA_V = 64
MLA_QK = 96
MLA_Q_RANK = 384
MLA_KV_RANK = 256
HEAD_SLOT = 128
MLA_WIDE = MLA_HEADS * HEAD_SLOT
LAT_PAD = 768
ROPE_THETA = 10000.0
D_FF = 4096
NPAD = CHUNK - N_META
ADAM_LR, ADAM_B1, ADAM_B2, ADAM_EPS, ADAM_WD, ADAM_STEP = 0.001, 0.9, 0.999, 1e-08, 0.01, 10
LANES = 1024
VMEM_LIMIT = 56 * 1024 * 1024


def _pick(n, cands):
    for c in cands:
        if n % c == 0:
            return c
    return n


def _cparams(**kw):
    return pltpu.CompilerParams(vmem_limit_bytes=VMEM_LIMIT, **kw)


def _mm(a, b, dims, *, name, out_dtype=F32, a_fn=None, epi=None, extras=()):
    if dims == 'nn':
        (M, K), (K2, N) = a.shape, b.shape
    elif dims == 'nt':
        (M, K), (N, K2) = a.shape, b.shape
    else:
        (K, M), (K2, N) = a.shape, b.shape
    assert K == K2, (a.shape, b.shape, dims)
    if dims == 'tn':
        tm = _pick(M, (1024, 768, 512, 384, 256, 128))
        tn = _pick(N, (1024, 896, 768, 512, 384, 256, 128))
        tk = _pick(K, (1408, 1024, 512, 384, 256, 128))
    else:
        tm = _pick(M, (1408, 1024, 512, 384, 256, 128))
        tn = _pick(N, (512, 896, 768, 384, 256, 128))
        tk = _pick(K, (1024, 896, 768, 512, 384, 256, 128))
    nk = K // tk
    if dims == 'nn':
        a_spec = pl.BlockSpec((tm, tk), lambda i, j, k: (i, k))
        b_spec = pl.BlockSpec((tk, tn), lambda i, j, k: (k, j))
        dn = (((1,), (0,)), ((), ()))
    elif dims == 'nt':
        a_spec = pl.BlockSpec((tm, tk), lambda i, j, k: (i, k))
        b_spec = pl.BlockSpec((tn, tk), lambda i, j, k: (j, k))
        dn = (((1,), (1,)), ((), ()))
    else:
        a_spec = pl.BlockSpec((tk, tm), lambda i, j, k: (k, i))
        b_spec = pl.BlockSpec((tk, tn), lambda i, j, k: (k, j))
        dn = (((0,), (0,)), ((), ()))
    o_spec = pl.BlockSpec((tm, tn), lambda i, j, k: (i, j))
    n_ex = len(extras)

    def body(a_ref, b_ref, *rest):
        ex_refs, o_ref, acc = rest[:n_ex], rest[n_ex], rest[n_ex + 1]
        k = pl.program_id(2)

        @pl.when(k == 0)
        def _():
            acc[...] = jnp.zeros_like(acc)

        av = a_ref[...]
        if a_fn is not None:
            av = a_fn(av)
        acc[...] += lax.dot_general(av.astype(BF16), b_ref[...].astype(BF16), dn,
                                    preferred_element_type=F32)

        @pl.when(k == nk - 1)
        def _():
            r = acc[...]
            if epi is not None:
                r = epi(r, *[e[...] for e in ex_refs])
            o_ref[...] = r.astype(out_dtype)

    return pl.pallas_call(
        body, name=name,
        out_shape=jax.ShapeDtypeStruct((M, N), out_dtype),
        grid=(M // tm, N // tn, nk),
        in_specs=[a_spec, b_spec] + [o_spec] * n_ex,
        out_specs=o_spec,
        scratch_shapes=[pltpu.VMEM((tm, tn), F32)],
        compiler_params=_cparams(dimension_semantics=("parallel", "parallel", "arbitrary")),
    )(a, b, *extras)


def _row_call(fn, rows, consts, out_rows, out_accs=(), *, n_rows, tile, name):
    n_r, n_c, n_o, n_a = len(rows), len(consts), len(out_rows), len(out_accs)
    steps = n_rows // tile

    def body(*refs):
        r_refs = refs[:n_r]
        c_refs = refs[n_r:n_r + n_c]
        o_refs = refs[n_r + n_c:n_r + n_c + n_o]
        a_refs = refs[n_r + n_c + n_o:]
        i = pl.program_id(0)
        res = fn(i, *[r[...] for r in r_refs], *[c[...] for c in c_refs])
        for o_ref, val in zip(o_refs, res[:n_o]):
            o_ref[...] = val.astype(o_ref.dtype)

        @pl.when(i == 0)
        def _():
            for a_ref in a_refs:
                a_ref[...] = jnp.zeros_like(a_ref)

        for a_ref, val in zip(a_refs, res[n_o:]):
            a_ref[...] += val

    in_specs = [pl.BlockSpec((tile, w), functools.partial(lambda i, cb: (i, cb), cb=cb))
                for (_, w, cb) in rows]
    in_specs += [pl.BlockSpec(c.shape, lambda i: (0, 0)) for c in consts]
    out_specs = [pl.BlockSpec((tile, c), lambda i: (i, 0)) for (c, _) in out_rows]
    out_specs += [pl.BlockSpec(s, lambda i: (0, 0)) for s in out_accs]
    out_shape = [jax.ShapeDtypeStruct((n_rows, c), dt) for (c, dt) in out_rows]
    out_shape += [jax.ShapeDtypeStruct(s, F32) for s in out_accs]
    return pl.pallas_call(
        body, name=name, out_shape=out_shape, grid=(steps,),
        in_specs=in_specs, out_specs=out_specs,
        compiler_params=_cparams(dimension_semantics=("arbitrary",)),
    )(*[r[0] for r in rows], *consts)


def _row_mask(i, tile):
    r = i * tile + lax.broadcasted_iota(jnp.int32, (tile, 1), 0)
    return (r >= NPAD).astype(F32)


def _rms(x, g):
    return x * lax.rsqrt(jnp.mean(x * x, axis=-1, keepdims=True) + EPS) * g


def _silu(x):
    return x * (1.0 / (1.0 + jnp.exp(-x)))


def _softplus(x):
    return jnp.maximum(x, 0.0) + jnp.log(1.0 + jnp.exp(-jnp.abs(x)))


def _rms_fwd(h, g, name):
    lp = h.shape[0]
    return _row_call(lambda i, hv, gv: (_rms(hv, gv),), [(h, D_MODEL, 0)], [g],
                     [(D_MODEL, BF16)], n_rows=lp, tile=_pick(lp, (384, 256, 128)), name=name)[0]


def _rms_bwd(h, g, d_hn, d_res, name):
    lp = h.shape[0]
    tile = _pick(lp, (384, 256, 128))

    def fn(i, hv, dv, rv, gv):
        _, vjp = jax.vjp(_rms, hv, gv)
        dh, dg = vjp(dv)
        return (rv + dh) * _row_mask(i, tile), dg

    return _row_call(fn, [(h, D_MODEL, 0), (d_hn, D_MODEL, 0), (d_res, D_MODEL, 0)], [g],
                     [(D_MODEL, F32)], [(1, D_MODEL)], n_rows=lp, tile=tile, name=name)


@functools.partial(jax.custom_vjp, nondiff_argnums=(1,))
def _roll_rows(x, s):
    return pltpu.roll(x, s, 0)


def _roll_rows_fwd(x, s):
    return pltpu.roll(x, s, 0), None


def _roll_rows_bwd(s, _, ct):
    return (pltpu.roll(ct, (ct.shape[0] - s) % ct.shape[0], 0),)


_roll_rows.defvjp(_roll_rows_fwd, _roll_rows_bwd)


def _conv_silu(cur, halo, w_rows, b):
    full = jnp.concatenate([halo, cur], axis=0)
    acc = cur * w_rows[SSD_CONV - 1] + b
    for k in range(SSD_CONV - 1):
        acc = acc + _roll_rows(full, SSD_CONV - 1 - k)[8:] * w_rows[k]
    return _silu(acc)


def _expand_heads(v, e_mat):
    return lax.dot_general(v, e_mat, (((1,), (0,)), ((), ())), precision=HI,
                           preferred_element_type=F32)


def _ssd_chunk(mask, z, xs_pre, bc_pre, halo_x, halo_bc, dt_pre, st, cwx0, cwx1, cwx2, cwx3,
               cwb0, cwb1, cwb2, cwb3, cb_x, cb_bc, dtb, alog, dsk, ng):
    L = CHUNK
    lane_h = lax.broadcasted_iota(jnp.int32, (1, 128), 1)
    head_ok = (lane_h < SSD_HEADS).astype(F32)
    e_mat = (lax.broadcasted_iota(jnp.int32, (128, SSD_D_INNER), 1) // SSD_HEAD_DIM
             == lax.broadcasted_iota(jnp.int32, (128, SSD_D_INNER), 0)).astype(F32)
    ri = lax.broadcasted_iota(jnp.int32, (L, L), 0)
    ci = lax.broadcasted_iota(jnp.int32, (L, L), 1)
    causal = ri >= ci
    tri = causal.astype(F32)

    xs = _conv_silu(xs_pre, halo_x, (cwx0, cwx1, cwx2, cwx3), cb_x) * mask
    bc = _conv_silu(bc_pre, halo_bc, (cwb0, cwb1, cwb2, cwb3), cb_bc) * mask
    dt = _softplus(dt_pre + dtb) * mask * head_ok
    a_dt = dt * (-jnp.exp(alog))
    a_cs = lax.dot_general(tri, a_dt, (((1,), (0,)), ((), ())), precision=HI,
                           preferred_element_type=F32)
    a_cs_t = a_cs.T
    dt_e = _expand_heads(dt, e_mat)
    acs_e = _expand_heads(a_cs, e_mat)
    last_e = jnp.sum(_expand_heads(a_dt, e_mat), axis=0, keepdims=True)
    row8 = lax.broadcasted_iota(jnp.int32, (8, 128), 0)
    d_e = jnp.sum(_expand_heads(jnp.where(row8 == 0, jnp.broadcast_to(dsk, (8, 128)), 0.0), e_mat),
                  axis=0, keepdims=True)
    xdt = xs * dt_e
    dte_e = jnp.exp(last_e - acs_e)
    dfs_e = jnp.exp(acs_e)
    cd_e = jnp.exp(last_e)
    sub_h = lax.broadcasted_iota(jnp.int32, (128, L), 0)
    lane_hl = lax.broadcasted_iota(jnp.int32, (L, 128), 1)
    lane_g = lax.broadcasted_iota(jnp.int32, (1, SSD_HPG * SSD_HEAD_DIM), 1) // SSD_HEAD_DIM

    ys, new_st = [], []
    for g in range(SSD_GROUPS):
        b_g = bc[:, g * 128:(g + 1) * 128].astype(BF16)
        c_g = bc[:, 1024 + g * 128:1024 + (g + 1) * 128].astype(BF16)
        gs = slice(g * 256, (g + 1) * 256)
        xdt_g = xdt[:, gs]
        cb = lax.dot_general(c_g, b_g, (((1,), (1,)), ((), ())), preferred_element_type=F32)
        st_g = st[g * 128:(g + 1) * 128, :]
        y_g = lax.dot_general(c_g, st_g.astype(BF16), (((1,), (0,)), ((), ())),
                              preferred_element_type=F32) * dfs_e[:, gs]
        for j in range(SSD_HPG):
            h = g * SSD_HPG + j
            col = jnp.sum(jnp.where(lane_hl == h, a_cs, 0.0), axis=1, keepdims=True)
            row = jnp.sum(jnp.where(sub_h == h, a_cs_t, 0.0), axis=0, keepdims=True)
            dec = jnp.where(causal, jnp.exp(jnp.where(causal, col - row, 0.0)), 0.0)
            m_h = (cb * dec).astype(BF16)
            x_h = jnp.where(lane_g == j, xdt_g, 0.0).astype(BF16)
            y_g = y_g + lax.dot_general(m_h, x_h, (((1,), (0,)), ((), ())),
                                        preferred_element_type=F32)
        s_new = lax.dot_general(b_g, (xdt_g * dte_e[:, gs]).astype(BF16), (((0,), (0,)), ((), ())),
                                preferred_element_type=F32)
        new_st.append(st_g * cd_e[:, gs] + s_new)
        ys.append(y_g)
    y = jnp.concatenate(ys, axis=1) + xs * d_e
    gg = y * _silu(z)
    outs = []
    for g in range(SSD_GROUPS):
        sl = gg[:, g * 256:(g + 1) * 256]
        outs.append(sl * lax.rsqrt(jnp.mean(sl * sl, axis=-1, keepdims=True) + EPS))
    out = jnp.concatenate(outs, axis=1) * ng
    return out, jnp.concatenate(new_st, axis=0)


def _ssd_consts(conv_w, conv_b, dtb, alog, dsk, ng):
    return [conv_w, conv_b, dtb, alog, dsk, ng]


def _ssd_param_vals(cw_ref, cb_ref, dtb_ref, alog_ref, dsk_ref, ng_ref):
    cwx = [cw_ref[k:k + 1, 0:SSD_D_INNER] for k in range(SSD_CONV)]
    cwb = [cw_ref[k:k + 1, SSD_D_INNER:2 * SSD_D_INNER] for k in range(SSD_CONV)]
    return (*cwx, *cwb, cb_ref[:, 0:SSD_D_INNER], cb_ref[:, SSD_D_INNER:2 * SSD_D_INNER],
            dtb_ref[...], alog_ref[...], dsk_ref[...], ng_ref[...])


def _ssd_in_specs(rev, nc):
    def cidx(i):
        return (nc - 1 - i) if rev else i

    def halo(cb):
        return pl.BlockSpec((8, SSD_D_INNER), lambda i: (jnp.maximum(16 * cidx(i) - 1, 0), cb))

    return [
        pl.BlockSpec((CHUNK, SSD_D_INNER), lambda i: (cidx(i), 0)),
        pl.BlockSpec((CHUNK, SSD_D_INNER), lambda i: (cidx(i), 1)),
        pl.BlockSpec((CHUNK, SSD_D_INNER), lambda i: (cidx(i), 2)),
        halo(1), halo(2),
        pl.BlockSpec((CHUNK, 128), lambda i: (cidx(i), 48)),
    ]


def _ssd_fwd(zxd, consts, name):
    lp = zxd.shape[0]
    nc = lp // CHUNK

    def body(z_ref, xs_ref, bc_ref, hx_ref, hb_ref, dt_ref, cw_ref, cb_ref, dtb_ref, alog_ref,
             dsk_ref, ng_ref, y_ref, st_ref, state):
        c = pl.program_id(0)

        @pl.when(c == 0)
        def _():
            state[...] = jnp.zeros_like(state)

        live = (c > 0).astype(F32)
        st_ref[0] = state[...]
        out, st_new = _ssd_chunk(
            _row_mask(c, CHUNK), z_ref[...], xs_ref[...], bc_ref[...], hx_ref[...] * live,
            hb_ref[...] * live, dt_ref[...], state[...],
            *_ssd_param_vals(cw_ref, cb_ref, dtb_ref, alog_ref, dsk_ref, ng_ref))
        y_ref[...] = out.astype(y_ref.dtype)
        state[...] = st_new

    return pl.pallas_call(
        body, name=name,
        out_shape=[jax.ShapeDtypeStruct((lp, SSD_D_INNER), BF16),
                   jax.ShapeDtypeStruct((nc, SSD_GROUPS * SSD_STATE, 256), F32)],
        grid=(nc,),
        in_specs=_ssd_in_specs(False, nc) + [pl.BlockSpec(c.shape, lambda i: (0, 0)) for c in consts],
        out_specs=[pl.BlockSpec((CHUNK, SSD_D_INNER), lambda i: (i, 0)),
                   pl.BlockSpec((1, SSD_GROUPS * SSD_STATE, 256), lambda i: (i, 0, 0))],
        scratch_shapes=[pltpu.VMEM((SSD_GROUPS * SSD_STATE, 256), F32)],
        compiler_params=_cparams(dimension_semantics=("arbitrary",)),
    )(zxd, zxd, zxd, zxd, zxd, zxd, *consts)


def _ssd_bwd(zxd, states, d_y, consts, name):
    lp = zxd.shape[0]
    nc = lp // CHUNK

    def body(z_ref, xs_ref, bc_ref, hx_ref, hb_ref, dt_ref, st_ref, dy_ref, cw_ref, cb_ref, dtb_ref,
             alog_ref, dsk_ref, ng_ref, dz_ref, dcw_ref, dcb_ref, ddtb_ref, dalog_ref, ddsk_ref,
             dng_ref, d_state, d_hx, d_hb):
        i = pl.program_id(0)
        c = nc - 1 - i

        @pl.when(i == 0)
        def _():
            d_state[...] = jnp.zeros_like(d_state)
            d_hx[...] = jnp.zeros_like(d_hx)
            d_hb[...] = jnp.zeros_like(d_hb)
            for r in (dcw_ref, dcb_ref, ddtb_ref, dalog_ref, ddsk_ref, dng_ref):
                r[...] = jnp.zeros_like(r)

        live = (c > 0).astype(F32)
        fn = functools.partial(_ssd_chunk, _row_mask(c, CHUNK))
        prim = (z_ref[...], xs_ref[...], bc_ref[...], hx_ref[...] * live, hb_ref[...] * live,
                dt_ref[...], st_ref[0],
                *_ssd_param_vals(cw_ref, cb_ref, dtb_ref, alog_ref, dsk_ref, ng_ref))
        _, vjp = jax.vjp(fn, *prim)
        (d_z, d_xs, d_bc, g_hx, g_hb, d_dt, g_st, *d_par) = vjp((dy_ref[...], d_state[...]))
        zeros = jnp.zeros((CHUNK - 8, SSD_D_INNER), F32)
        d_xs = d_xs + jnp.concatenate([zeros, d_hx[...]], axis=0)
        d_bc = d_bc + jnp.concatenate([zeros, d_hb[...]], axis=0)
        dz_ref[:, 0:SSD_D_INNER] = d_z
        dz_ref[:, SSD_D_INNER:2 * SSD_D_INNER] = d_xs
        dz_ref[:, 2 * SSD_D_INNER:3 * SSD_D_INNER] = d_bc
        dz_ref[:, 3 * SSD_D_INNER:] = d_dt
        d_state[...] = g_st
        d_hx[...] = g_hx * live
        d_hb[...] = g_hb * live
        for k in range(SSD_CONV):
            dcw_ref[k:k + 1, 0:SSD_D_INNER] += d_par[k]
            dcw_ref[k:k + 1, SSD_D_INNER:2 * SSD_D_INNER] += d_par[SSD_CONV + k]
        dcb_ref[:, 0:SSD_D_INNER] += d_par[8]
        dcb_ref[:, SSD_D_INNER:2 * SSD_D_INNER] += d_par[9]
        ddtb_ref[...] += d_par[10]
        dalog_ref[...] += d_par[11]
        ddsk_ref[...] += d_par[12]
        dng_ref[...] += d_par[13]

    const_specs = [pl.BlockSpec(c.shape, lambda i: (0, 0)) for c in consts]
    return pl.pallas_call(
        body, name=name,
        out_shape=[jax.ShapeDtypeStruct((lp, SSD_IN_PAD), F32)]
        + [jax.ShapeDtypeStruct(c.shape, F32) for c in consts],
        grid=(nc,),
        in_specs=_ssd_in_specs(True, nc)
        + [pl.BlockSpec((1, SSD_GROUPS * SSD_STATE, 256), lambda i: (nc - 1 - i, 0, 0)),
           pl.BlockSpec((CHUNK, SSD_D_INNER), lambda i: (nc - 1 - i, 0))] + const_specs,
        out_specs=[pl.BlockSpec((CHUNK, SSD_IN_PAD), lambda i: (nc - 1 - i, 0))] + const_specs,
        scratch_shapes=[pltpu.VMEM((SSD_GROUPS * SSD_STATE, 256), F32),
                        pltpu.VMEM((8, SSD_D_INNER), F32), pltpu.VMEM((8, SSD_D_INNER), F32)],
        compiler_params=_cparams(dimension_semantics=("arbitrary",)),
    )(zxd, zxd, zxd, zxd, zxd, zxd, states, d_y, *consts)


@jax.custom_vjp
def _rot_half(x):
    lane = lax.broadcasted_iota(jnp.int32, x.shape, 1)
    lo = (lane >= MLA_NOPE) & (lane < MLA_NOPE + MLA_ROPE // 2)
    hi = (lane >= MLA_NOPE + MLA_ROPE // 2) & (lane < MLA_QK)
    down = pltpu.roll(x, HEAD_SLOT - MLA_ROPE // 2, 1)
    up = pltpu.roll(x, MLA_ROPE // 2, 1)
    return jnp.where(lo, -down, jnp.where(hi, up, 0.0))


def _rot_half_fwd(x):
    return _rot_half(x), None


def _rot_half_bwd(_, ct):
    return (-_rot_half(ct),)


_rot_half.defvjp(_rot_half_fwd, _rot_half_bwd)


def _head_norm_rope(t, gain, cos, sin):
    n = t * lax.rsqrt(jnp.sum(t * t, axis=-1, keepdims=True) * (1.0 / MLA_QK) + EPS) * gain
    return n * cos + _rot_half(n) * sin


def _qk_prep(q_raw, kn_raw, kpe, cos, sin, qg, kg):
    qs, ks = [], []
    for h in range(MLA_HEADS):
        sl = slice(h * HEAD_SLOT, (h + 1) * HEAD_SLOT)
        qs.append(_head_norm_rope(q_raw[:, sl], qg, cos, sin))
        ks.append(_head_norm_rope(kn_raw[:, sl] + kpe, kg, cos, sin))
    return jnp.concatenate(qs, axis=1), jnp.concatenate(ks, axis=1)


def _lat_norm(kv_lat, q_lat, kvg, qg):
    return _rms(kv_lat, kvg), _rms(q_lat, qg)


_NEG = -1e30
_SCALE = MLA_QK ** -0.5
_NT = (((1,), (1,)), ((), ()))
_NN = (((1,), (0,)), ((), ()))
_TN = (((0,), (0,)), ((), ()))


def _diag_mask(blk, t):
    qpos = blk * t + lax.broadcasted_iota(jnp.int32, (t, t), 0)
    kpos = blk * t + lax.broadcasted_iota(jnp.int32, (t, t), 1)
    return (kpos <= qpos) & ((kpos >= NPAD) | (kpos == qpos))


def _key_ok(blk, t):
    return blk * t + lax.broadcasted_iota(jnp.int32, (1, t), 1) >= NPAD


def _attn_fwd(q, k, v, name):
    lp = q.shape[0]
    t = _pick(lp, (384, 256, 128))
    nb = lp // t

    def body(q_ref, k_ref, v_ref, o_ref, lse_ref):
        qi = pl.program_id(1)
        qv = q_ref[...]

        def scores(ki):
            start = pl.multiple_of(ki * t, t)
            s = lax.dot_general(qv, k_ref[pl.ds(start, t), :], _NT, preferred_element_type=F32) * _SCALE
            return s, v_ref[pl.ds(start, t), :]

        def update(carry, s, vb):
            m, l, acc = carry
            m_new = jnp.maximum(m, jnp.max(s, axis=-1, keepdims=True))
            alpha = jnp.exp(m - m_new)
            p = jnp.exp(s - m_new)
            l = alpha * l + jnp.sum(p, axis=-1, keepdims=True)
            acc = alpha * acc + lax.dot_general(p.astype(BF16), vb, _NN, preferred_element_type=F32)
            return m_new, l, acc

        def off_diag(ki, carry):
            s, vb = scores(ki)
            return update(carry, jnp.where(_key_ok(ki, t), s, _NEG), vb)

        init = (jnp.full((t, 1), _NEG, F32), jnp.zeros((t, 1), F32), jnp.zeros((t, HEAD_SLOT), F32))
        carry = lax.fori_loop(0, qi, off_diag, init)
        s, vb = scores(qi)
        m, l, acc = update(carry, jnp.where(_diag_mask(qi, t), s, _NEG), vb)
        o_ref[...] = acc / l * _row_mask(qi, t)
        lse_ref[0] = m + jnp.log(l)

    qspec = pl.BlockSpec((t, HEAD_SLOT), lambda h, i: (i, h))
    kspec = pl.BlockSpec((lp, HEAD_SLOT), lambda h, i: (0, h))
    return pl.pallas_call(
        body, name=name,
        out_shape=[jax.ShapeDtypeStruct((lp, MLA_WIDE), F32),
                   jax.ShapeDtypeStruct((MLA_HEADS, lp, 1), F32)],
        grid=(MLA_HEADS, nb),
        in_specs=[qspec, kspec, kspec],
        out_specs=[qspec, pl.BlockSpec((1, t, 1), lambda h, i: (h, i, 0))],
        compiler_params=_cparams(dimension_semantics=("parallel", "arbitrary")),
    )(q, k, v)


def _attn_bwd(q, k, v, do, o, lse, name):
    lp = q.shape[0]
    t = _pick(lp, (384, 256, 128))
    nb = lp // t

    def body(q_ref, k_ref, v_ref, do_ref, o_ref, lse_ref, dq_ref, dk_ref, dv_ref):
        kj = pl.program_id(1)

        @pl.when(kj == 0)
        def _():
            dq_ref[...] = jnp.zeros_like(dq_ref)

        kb, vb = k_ref[...], v_ref[...]

        def tile(qi, mask):
            rows = pl.ds(pl.multiple_of(qi * t, t), t)
            qb, dob = q_ref[rows, :], do_ref[rows, :]
            s = lax.dot_general(qb, kb, _NT, preferred_element_type=F32) * _SCALE
            p = jnp.where(mask, jnp.exp(s - lse_ref[0, rows, :]), 0.0)
            dp = lax.dot_general(dob.astype(BF16), vb, _NT, preferred_element_type=F32)
            delta = jnp.sum(dob * o_ref[rows, :], axis=-1, keepdims=True)
            ds = (p * (dp - delta)).astype(BF16)
            dq_ref[rows, :] += lax.dot_general(ds, kb, _NN, preferred_element_type=F32) * _SCALE
            return (lax.dot_general(p.astype(BF16), dob.astype(BF16), _TN, preferred_element_type=F32),
                    lax.dot_general(ds, qb, _TN, preferred_element_type=F32))

        def below(qi, carry):
            dv_t, dk_t = tile(qi, _key_ok(kj, t))
            return carry[0] + dv_t, carry[1] + dk_t

        dv, dk = lax.fori_loop(kj + 1, nb, below, tile(kj, _diag_mask(kj, t)))
        dk_ref[...] = dk * _SCALE
        dv_ref[...] = dv

    whole = pl.BlockSpec((lp, HEAD_SLOT), lambda h, j: (0, h))
    kspec = pl.BlockSpec((t, HEAD_SLOT), lambda h, j: (j, h))
    return pl.pallas_call(
        body, name=name,
        out_shape=[jax.ShapeDtypeStruct((lp, MLA_WIDE), F32)] * 3,
        grid=(MLA_HEADS, nb),
        in_specs=[whole, kspec, kspec, whole, whole, pl.BlockSpec((1, lp, 1), lambda h, j: (h, 0, 0))],
        out_specs=[whole, kspec, kspec],
        compiler_params=_cparams(dimension_semantics=("parallel", "arbitrary")),
    )(q, k, v, do, o, lse)


def _rope_tables(lp):
    inv = 1.0 / (ROPE_THETA ** (jnp.arange(0, MLA_ROPE, 2, dtype=F32) / MLA_ROPE))
    pos = jnp.maximum(jnp.arange(lp, dtype=jnp.int32) - NPAD, 0).astype(F32)
    ang = pos[:, None] * inv[None, :]
    cos, sin = jnp.cos(ang), jnp.sin(ang)
    z32 = jnp.zeros((lp, HEAD_SLOT - MLA_QK), F32)
    cos_t = jnp.concatenate([jnp.ones((lp, MLA_NOPE), F32), cos, cos, z32], axis=1)
    sin_t = jnp.concatenate([jnp.zeros((lp, MLA_NOPE), F32), sin, sin, z32], axis=1)
    return cos_t, sin_t


def _loss_head(h, target, name):
    lp = h.shape[0]

    def body(h_ref, t_ref, d_ref, loss_ref):
        i = pl.program_id(0)

        @pl.when(i == 0)
        def _():
            d_ref[...] = jnp.zeros_like(d_ref)
            loss_ref[...] = jnp.zeros_like(loss_ref)

        @pl.when(i > 0)
        def _():
            err = h_ref[...] - t_ref[...]
            d_ref[...] = err * (1.0 / D_MODEL)
            loss_ref[...] += jnp.sum(err * err, axis=0, keepdims=True) * (0.5 / D_MODEL)

    return pl.pallas_call(
        body, name=name,
        out_shape=[jax.ShapeDtypeStruct((lp, D_MODEL), F32), jax.ShapeDtypeStruct((1, D_MODEL), F32)],
        grid=(lp // CHUNK,),
        in_specs=[pl.BlockSpec((CHUNK, D_MODEL), lambda i: (i, 0)),
                  pl.BlockSpec((CHUNK, D_MODEL), lambda i: (jnp.maximum(i - 1, 0), 0))],
        out_specs=[pl.BlockSpec((CHUNK, D_MODEL), lambda i: (i, 0)),
                   pl.BlockSpec((1, D_MODEL), lambda i: (0, 0))],
        compiler_params=_cparams(dimension_semantics=("arbitrary",)),
    )(h, target)


def _pad_cols(w, n):
    return jnp.pad(w, [(0, 0)] * (w.ndim - 1) + [(0, n - w.shape[-1])])


def _prep_weights(w):
    p = {}
    p['ssd_in'] = [_pad_cols(w['ssd_w_in'][j], SSD_IN_PAD).astype(BF16) for j in range(2)]
    p['ssd_out'] = [w['ssd_w_out'][j].astype(BF16) for j in range(2)]
    p['mla_in'], p['mla_qb'], p['mla_kvb'], p['mla_out'] = [], [], [], []
    for j in range(2):
        wi = w['mla_w_in'][j]
        kpe = jnp.pad(wi[:, MLA_Q_RANK + MLA_KV_RANK:], ((0, 0), (MLA_NOPE, HEAD_SLOT - MLA_QK)))
        p['mla_in'].append(jnp.concatenate(
            [wi[:, MLA_Q_RANK:MLA_Q_RANK + MLA_KV_RANK], kpe, wi[:, :MLA_Q_RANK]], axis=1).astype(BF16))
        qb = w['mla_w_q_b'][j].reshape(MLA_Q_RANK, MLA_HEADS, MLA_QK)
        p['mla_qb'].append(_pad_cols(qb, HEAD_SLOT).reshape(MLA_Q_RANK, MLA_WIDE).astype(BF16))
        kvb = w['mla_w_kv_b'][j].reshape(MLA_KV_RANK, MLA_HEADS, MLA_NOPE + MLA_V)
        kn = _pad_cols(kvb[:, :, :MLA_NOPE], HEAD_SLOT).reshape(MLA_KV_RANK, MLA_WIDE)
        vv = _pad_cols(kvb[:, :, MLA_NOPE:], HEAD_SLOT).reshape(MLA_KV_RANK, MLA_WIDE)
        p['mla_kvb'].append(jnp.concatenate([kn, vv], axis=1).astype(BF16))
        wo = w['mla_w_out'][j].reshape(MLA_HEADS, MLA_V, D_MODEL)
        p['mla_out'].append(jnp.pad(wo, ((0, 0), (0, HEAD_SLOT - MLA_V), (0, 0)))
                            .reshape(MLA_WIDE, D_MODEL).astype(BF16))
    p['up'] = [w['mlp_w_up'][i].astype(BF16) for i in range(4)]
    p['down'] = [w['mlp_w_down'][i].astype(BF16) for i in range(4)]
    return p


def _pad128(v):
    return _pad_cols(v.reshape(1, -1), 128)


def _sqrelu(u):
    r = jnp.maximum(u, 0.0)
    return r * r


def _local_step(x, target, w):
    seq = x.shape[0]
    lp = NPAD + N_META + seq
    p = _prep_weights(w)
    h = jnp.concatenate([jnp.zeros((NPAD, D_MODEL), F32), w['meta_tokens'], x], axis=0)
    cos_t, sin_t = _rope_tables(lp)
    rt = _pick(lp, (384, 256, 128))
    saved = []
    for i in range(4):
        j = i // 2
        s = {'h0': h}
        g_mix = w['ln_mix'][i].reshape(1, -1)
        hn = _rms_fwd(h, g_mix, f"rms_mix_f{i}")
        s['hn'] = hn
        if i % 2 == 0:
            zxd = _mm(hn, p['ssd_in'][j], 'nn', name=f"ssd_in_f{i}")
            consts = _ssd_consts(w['ssd_conv_w'][j], w['ssd_conv_b'][j].reshape(1, -1),
                                 _pad128(w['ssd_dt_bias'][j]), _pad128(w['ssd_a_log'][j]),
                                 _pad128(w['ssd_d'][j]), w['ssd_norm'][j].reshape(1, -1))
            yg, states = _ssd_fwd(zxd, consts, f"ssd_core_f{i}")
            s.update(zxd=zxd, consts=consts, yg=yg, states=states)
            h = _mm(yg, p['ssd_out'][j], 'nn', name=f"ssd_out_f{i}", epi=lambda r, hv: hv + r, extras=(h,))
        else:
            lat = _mm(hn, p['mla_in'][j], 'nn', name=f"mla_in_f{i}")
            kvg = w['mla_kv_a_norm'][j].reshape(1, -1)
            qag = w['mla_q_a_norm'][j].reshape(1, -1)
            kvn, qn = _row_call(lambda _, a, b, c, d: _lat_norm(a, b, c, d),
                                [(lat, MLA_KV_RANK, 0), (lat, MLA_Q_RANK, 1)], [kvg, qag],
                                [(MLA_KV_RANK, BF16), (MLA_Q_RANK, BF16)], n_rows=lp, tile=rt,
                                name=f"mla_latnorm_f{i}")
            q_raw = _mm(qn, p['mla_qb'][j], 'nn', name=f"mla_qb_f{i}")
            kv_raw = _mm(kvn, p['mla_kvb'][j], 'nn', name=f"mla_kvb_f{i}")
            qg = _pad_cols(w['mla_q_norm'][j].reshape(1, -1), HEAD_SLOT)
            kg = _pad_cols(w['mla_k_norm'][j].reshape(1, -1), HEAD_SLOT)

            def prep_fwd(_, qr, kn, kpe, vv, cs, sn, qgv, kgv):
                qq, kk = _qk_prep(qr, kn, kpe, cs, sn, qgv, kgv)
                return qq, kk, vv

            q, k, v = _row_call(prep_fwd,
                                [(q_raw, MLA_WIDE, 0), (kv_raw, MLA_WIDE, 0), (lat, HEAD_SLOT, 2),
                                 (kv_raw, MLA_WIDE, 1), (cos_t, HEAD_SLOT, 0), (sin_t, HEAD_SLOT, 0)],
                                [qg, kg], [(MLA_WIDE, BF16)] * 3, n_rows=lp, tile=rt,
                                name=f"mla_qkprep_f{i}")
            o, lse = _attn_fwd(q, k, v, f"mla_attn_f{i}")
            s.update(lat=lat, kvg=kvg, qag=qag, kvn=kvn, qn=qn, q_raw=q_raw, kv_raw=kv_raw, qg=qg, kg=kg,
                     q=q, k=k, v=v, o=o, lse=lse)
            h = _mm(o, p['mla_out'][j], 'nn', name=f"mla_out_f{i}", epi=lambda r, hv: hv + r, extras=(h,))
        s['h1'] = h
        g_mlp = w['ln_mlp'][i].reshape(1, -1)
        hn2 = _rms_fwd(h, g_mlp, f"rms_mlp_f{i}")
        u = _mm(hn2, p['up'][i], 'nn', name=f"mlp_up_f{i}")
        h = _mm(u, p['down'][i], 'nn', name=f"mlp_down_f{i}", a_fn=_sqrelu,
                epi=lambda r, hv: hv + r, extras=(h,))
        s.update(hn2=hn2, u=u, g_mix=g_mix, g_mlp=g_mlp)
        saved.append(s)

    dh, loss_row = _loss_head(h, target, "loss_head")

    g = {k_: [None] * (4 if k_ in ('ln_mix', 'ln_mlp', 'mlp_w_up', 'mlp_w_down') else 2)
         for k_ in w if k_ != 'meta_tokens'}
    for i in reversed(range(4)):
        j = i // 2
        s = saved[i]
        g['mlp_w_down'][i] = _mm(s['u'], dh, 'tn', name=f"mlp_down_dw{i}", a_fn=_sqrelu)
        du = _mm(dh, p['down'][i], 'nt', name=f"mlp_down_dx{i}",
                 epi=lambda r, uv: r * (2.0 * jnp.maximum(uv, 0.0)), extras=(s['u'],))
        g['mlp_w_up'][i] = _mm(s['hn2'], du, 'tn', name=f"mlp_up_dw{i}")
        d_hn2 = _mm(du, p['up'][i], 'nt', name=f"mlp_up_dx{i}")
        dh, dg = _rms_bwd(s['h1'], s['g_mlp'], d_hn2, dh, f"rms_mlp_b{i}")
        g['ln_mlp'][i] = dg[0]
        if i % 2 == 0:
            g['ssd_w_out'][j] = _mm(s['yg'], dh, 'tn', name=f"ssd_out_dw{i}")
            d_yg = _mm(dh, p['ssd_out'][j], 'nt', name=f"ssd_out_dx{i}")
            d_zxd, dcw, dcb, ddtb, dalog, ddsk, dng = _ssd_bwd(s['zxd'], s['states'], d_yg, s['consts'],
                                                              f"ssd_core_b{i}")
            g['ssd_conv_w'][j], g['ssd_conv_b'][j], g['ssd_norm'][j] = dcw, dcb[0], dng[0]
            g['ssd_dt_bias'][j], g['ssd_a_log'][j], g['ssd_d'][j] = (
                ddtb[0, :SSD_HEADS], dalog[0, :SSD_HEADS], ddsk[0, :SSD_HEADS])
            g['ssd_w_in'][j] = _mm(s['hn'], d_zxd, 'tn', name=f"ssd_in_dw{i}")
            d_hn = _mm(d_zxd, p['ssd_in'][j], 'nt', name=f"ssd_in_dx{i}")
        else:
            wo = _mm(s['o'], dh, 'tn', name=f"mla_out_dw{i}")
            g['mla_w_out'][j] = wo.reshape(MLA_HEADS, HEAD_SLOT, D_MODEL)[:, :MLA_V].reshape(-1, D_MODEL)
            do = _mm(dh, p['mla_out'][j], 'nt', name=f"mla_out_dx{i}")
            dq, dk, dv = _attn_bwd(s['q'], s['k'], s['v'], do, s['o'], s['lse'], f"mla_attn_b{i}")

            def prep_bwd(_, qr, kn, kpe, cs, sn, dqv, dkv, dvv, qgv, kgv):
                _, vjp = jax.vjp(lambda a, b, c, d, e: _qk_prep(a, b, c, cs, sn, d, e), qr, kn, kpe, qgv, kgv)
                d_qr, d_kn, d_kpe, d_qg, d_kg = vjp((dqv, dkv))
                return d_qr, jnp.concatenate([d_kn, dvv], axis=1), d_kpe, d_qg, d_kg

            d_qraw, d_kvraw, d_kpe, d_qg, d_kg = _row_call(
                prep_bwd,
                [(s['q_raw'], MLA_WIDE, 0), (s['kv_raw'], MLA_WIDE, 0), (s['lat'], HEAD_SLOT, 2),
                 (cos_t, HEAD_SLOT, 0), (sin_t, HEAD_SLOT, 0), (dq, MLA_WIDE, 0), (dk, MLA_WIDE, 0),
                 (dv, MLA_WIDE, 0)],
                [s['qg'], s['kg']], [(MLA_WIDE, F32), (2 * MLA_WIDE, F32), (HEAD_SLOT, F32)],
                [(1, HEAD_SLOT), (1, HEAD_SLOT)], n_rows=lp, tile=_pick(lp, (128,)), name=f"mla_qkprep_b{i}")
            g['mla_q_norm'][j], g['mla_k_norm'][j] = d_qg[0, :MLA_QK], d_kg[0, :MLA_QK]
            wqb = _mm(s['qn'], d_qraw, 'tn', name=f"mla_qb_dw{i}")
            g['mla_w_q_b'][j] = wqb.reshape(MLA_Q_RANK, MLA_HEADS, HEAD_SLOT)[:, :, :MLA_QK].reshape(MLA_Q_RANK, -1)
            d_qn = _mm(d_qraw, p['mla_qb'][j], 'nt', name=f"mla_qb_dx{i}")
            wkvb = _mm(s['kvn'], d_kvraw, 'tn', name=f"mla_kvb_dw{i}").reshape(MLA_KV_RANK, 2, MLA_HEADS, HEAD_SLOT)
            g['mla_w_kv_b'][j] = jnp.concatenate([wkvb[:, 0, :, :MLA_NOPE], wkvb[:, 1, :, :MLA_V]],
                                                 axis=-1).reshape(MLA_KV_RANK, -1)
            d_kvn = _mm(d_kvraw, p['mla_kvb'][j], 'nt', name=f"mla_kvb_dx{i}")

            def lat_bwd(_, kvl, ql, dkvn, dqn, dkpe, kvgv, qagv):
                _, vjp = jax.vjp(_lat_norm, kvl, ql, kvgv, qagv)
                d_kvl, d_ql, d_kvg, d_qag = vjp((dkvn, dqn))
                return jnp.concatenate([d_kvl, dkpe, d_ql], axis=1), d_kvg, d_qag

            d_lat, d_kvg, d_qag = _row_call(
                lat_bwd, [(s['lat'], MLA_KV_RANK, 0), (s['lat'], MLA_Q_RANK, 1), (d_kvn, MLA_KV_RANK, 0),
                          (d_qn, MLA_Q_RANK, 0), (d_kpe, HEAD_SLOT, 0)],
                [s['kvg'], s['qag']], [(LAT_PAD, F32)], [(1, MLA_KV_RANK), (1, MLA_Q_RANK)],
                n_rows=lp, tile=rt, name=f"mla_latnorm_b{i}")
            g['mla_kv_a_norm'][j], g['mla_q_a_norm'][j] = d_kvg[0], d_qag[0]
            win = _mm(s['hn'], d_lat, 'tn', name=f"mla_in_dw{i}")
            g['mla_w_in'][j] = jnp.concatenate(
                [win[:, MLA_KV_RANK + HEAD_SLOT:], win[:, :MLA_KV_RANK],
                 win[:, MLA_KV_RANK + MLA_NOPE:MLA_KV_RANK + MLA_QK]], axis=1)
            d_hn = _mm(d_lat, p['mla_in'][j], 'nt', name=f"mla_in_dx{i}")
        dh, dg = _rms_bwd(s['h0'], s['g_mix'], d_hn, dh, f"rms_mix_b{i}")
        g['ln_mix'][i] = dg[0]

    grads = {k_: jnp.stack(v_) for k_, v_ in g.items()}
    grads['meta_tokens'] = dh[NPAD:NPAD + N_META]
    return loss_row, dh[NPAD + N_META:], grads


def _all_gather8(shard, name):
    m_per, n = shard.shape

    def body(x_ref, out_ref, send_sems, recv_sems, local_sem):
        x, y, c = lax.axis_index("x"), lax.axis_index("y"), lax.axis_index("c")
        me, sibling = (x, y, c), (x, y, 1 - c)
        chips = [(1 - x, y), (x, 1 - y), (1 - x, 1 - y)]

        def rows(px, py, pc):
            return out_ref.at[pl.ds((4 * px + 2 * py + pc) * m_per, m_per), :]

        def copy(k, block, to, src=None):
            return pltpu.make_async_remote_copy(
                src_ref=rows(*block) if src is None else src, dst_ref=rows(*block),
                send_sem=send_sems.at[k], recv_sem=recv_sems.at[k], device_id=to, device_id_type=MESH)

        mine = pltpu.make_async_copy(x_ref, rows(*me), local_sem)
        mine.start()
        first = [copy(0, me, sibling, src=x_ref)]
        first += [copy(1 + j, me, (*chip, c), src=x_ref) for j, chip in enumerate(chips)]
        for cp in first:
            cp.start()
        passed = [copy(4 + j, (*chip, c), sibling) for j, chip in enumerate(chips)]
        for j, chip in enumerate(chips):
            copy(1 + j, (*chip, c), me).wait_recv()
            passed[j].start()
        copy(0, sibling, me).wait_recv()
        for j, chip in enumerate(chips):
            copy(4 + j, (*chip, 1 - c), me).wait_recv()
        for cp in first + passed:
            cp.wait_send()
        mine.wait()

    return pl.pallas_call(
        body, name=name,
        out_shape=jax.ShapeDtypeStruct((8 * m_per, n), shard.dtype),
        in_specs=[pl.BlockSpec(memory_space=pl.ANY)],
        out_specs=pl.BlockSpec(memory_space=pl.ANY),
        scratch_shapes=[pltpu.SemaphoreType.DMA((7,)), pltpu.SemaphoreType.DMA((7,)), pltpu.SemaphoreType.DMA],
    )(shard)


def _mesh_pos():
    return lax.axis_index("x"), lax.axis_index("y"), lax.axis_index("c")


def _half_rows(pc, h):
    return pl.ds(pl.multiple_of(pc * h, 16), h)


def _whole_view(ref, kind, shard_shape, k, pc):
    _, r, c = shard_shape
    rows = _half_rows(pc, r // 2)
    if kind == 'row':
        return ref.at[:, k, rows, :]
    if kind == 'col':
        return ref.at[:, rows, pl.ds(pl.multiple_of(k * c, 128), c)]
    return ref.at[k, :, rows, :]


def _whole_shape(kind, shard_shape, rows=None):
    l, r, c = shard_shape
    r = r if rows is None else rows
    return {'row': (l, 4, r, c), 'col': (l, r, 4 * c), 'colx': (4, l, r, c)}[kind]


def _gather_big(shards, kinds, name):
    n = len(shards)
    shapes = [s.shape for s in shards]

    def body(*refs):
        ins, outs = refs[:n], refs[n:2 * n]
        send_sems, recv_sems, local_sems = refs[2 * n:]
        x, y, c = _mesh_pos()
        me, sibling = (x, y, c), (x, y, 1 - c)
        chips = [(1 - x, y), (x, 1 - y), (1 - x, 1 - y)]

        def place(a, px, py, pc):
            return _whole_view(outs[a], kinds[a], shapes[a], 2 * px + py, pc)

        def own(a):
            return ins[a].at[:, _half_rows(c, shapes[a][1] // 2), :]

        def copy(a, k, block, to, src=None):
            return pltpu.make_async_remote_copy(
                src_ref=place(a, *block) if src is None else src, dst_ref=place(a, *block),
                send_sem=send_sems.at[7 * a + k], recv_sem=recv_sems.at[7 * a + k],
                device_id=to, device_id_type=MESH)

        mine = [pltpu.make_async_copy(own(a), place(a, *me), local_sems.at[a]) for a in range(n)]
        first = [copy(a, 1 + j, me, (*chip, c), src=own(a)) for j, chip in enumerate(chips) for a in range(n)]
        first += [copy(a, 0, me, sibling, src=own(a)) for a in range(n)]
        for cp in first + mine:
            cp.start()
        passed = []
        for j, chip in enumerate(chips):
            for a in range(n):
                copy(a, 1 + j, (*chip, c), me).wait_recv()
                passed.append(copy(a, 4 + j, (*chip, c), sibling))
                passed[-1].start()
        for a in range(n):
            copy(a, 0, sibling, me).wait_recv()
        for j, chip in enumerate(chips):
            for a in range(n):
                copy(a, 4 + j, (*chip, 1 - c), me).wait_recv()
        for cp in first + passed:
            cp.wait_send()
        for cp in mine:
            cp.wait()

    return pl.pallas_call(
        body, name=name,
        out_shape=[jax.ShapeDtypeStruct(_whole_shape(k, s.shape), s.dtype) for k, s in zip(kinds, shards)],
        in_specs=[pl.BlockSpec(memory_space=pl.ANY)] * n,
        out_specs=[pl.BlockSpec(memory_space=pl.ANY)] * n,
        scratch_shapes=[pltpu.SemaphoreType.DMA((7 * n,)), pltpu.SemaphoreType.DMA((7 * n,)),
                        pltpu.SemaphoreType.DMA((n,))],
    )(*shards)


def _rs_swap(wholes, kinds, shapes, name):
    n = len(wholes)

    def body(*refs):
        ins, outs = refs[:n], refs[n:2 * n]
        send_sems, recv_sems = refs[2 * n:]
        x, y, c = _mesh_pos()
        cps = []
        for a in range(n):
            rows = _half_rows(1 - c, shapes[a][1] // 2)
            src = ins[a].at[:, rows, :] if kinds[a] == 'col' else ins[a].at[:, :, rows, :]
            cps.append(pltpu.make_async_remote_copy(
                src_ref=src, dst_ref=outs[a], send_sem=send_sems.at[a], recv_sem=recv_sems.at[a],
                device_id=(x, y, 1 - c), device_id_type=MESH))
        for cp in cps:
            cp.start()
        for cp in cps:
            cp.wait()

    return pl.pallas_call(
        body, name=name,
        out_shape=[jax.ShapeDtypeStruct(_whole_shape(k, s, s[1] // 2), w.dtype)
                   for k, s, w in zip(kinds, shapes, wholes)],
        in_specs=[pl.BlockSpec(memory_space=pl.ANY)] * n,
        out_specs=[pl.BlockSpec(memory_space=pl.ANY)] * n,
        scratch_shapes=[pltpu.SemaphoreType.DMA((n,)), pltpu.SemaphoreType.DMA((n,))],
    )(*wholes)


def _rs_exchange(parts, kinds, shapes, name):
    n = len(parts)

    def body(*refs):
        ins, outs = refs[:n], refs[n:2 * n]
        send_sems, recv_sems, local_sems = refs[2 * n:]
        x, y, c = _mesh_pos()
        kme = 2 * x + y
        chips = [(1 - x, y), (x, 1 - y), (1 - x, 1 - y)]

        def slab(a, k):
            if kinds[a] == 'row':
                return ins[a].at[:, k]
            if kinds[a] == 'col':
                cw = shapes[a][2]
                return ins[a].at[:, :, pl.ds(pl.multiple_of(k * cw, 128), cw)]
            return ins[a].at[k]

        cps = [pltpu.make_async_remote_copy(
            src_ref=slab(a, 2 * px + py), dst_ref=outs[a].at[kme], send_sem=send_sems.at[3 * a + j],
            recv_sem=recv_sems.at[3 * a + j], device_id=(px, py, c), device_id_type=MESH)
            for j, (px, py) in enumerate(chips) for a in range(n)]
        cps_local = [pltpu.make_async_copy(slab(a, kme), outs[a].at[kme], local_sems.at[a]) for a in range(n)]
        for cp in cps + cps_local:
            cp.start()
        for cp in cps + cps_local:
            cp.wait()

    return pl.pallas_call(
        body, name=name,
        out_shape=[jax.ShapeDtypeStruct((4, s[0], s[1] // 2, s[2]), p.dtype) for s, p in zip(shapes, parts)],
        in_specs=[pl.BlockSpec(memory_space=pl.ANY)] * n,
        out_specs=[pl.BlockSpec(memory_space=pl.ANY)] * n,
        scratch_shapes=[pltpu.SemaphoreType.DMA((3 * n,)), pltpu.SemaphoreType.DMA((3 * n,)),
                        pltpu.SemaphoreType.DMA((n,))],
    )(*parts)


def _rs_share(halves, name):
    n = len(halves)

    def body(*refs):
        ins, outs = refs[:n], refs[n:2 * n]
        send_sems, recv_sems, local_sems = refs[2 * n:]
        x, y, c = _mesh_pos()
        cps = []
        for a in range(n):
            dst = outs[a].at[:, _half_rows(c, halves[a].shape[1]), :]
            cps.append(pltpu.make_async_remote_copy(
                src_ref=ins[a], dst_ref=dst, send_sem=send_sems.at[a], recv_sem=recv_sems.at[a],
                device_id=(x, y, 1 - c), device_id_type=MESH))
            cps.append(pltpu.make_async_copy(ins[a], dst, local_sems.at[a]))
        for cp in cps:
            cp.start()
        for cp in cps:
            cp.wait()

    return pl.pallas_call(
        body, name=name,
        out_shape=[jax.ShapeDtypeStruct((h.shape[0], 2 * h.shape[1], h.shape[2]), h.dtype) for h in halves],
        in_specs=[pl.BlockSpec(memory_space=pl.ANY)] * n,
        out_specs=[pl.BlockSpec(memory_space=pl.ANY)] * n,
        scratch_shapes=[pltpu.SemaphoreType.DMA((n,)), pltpu.SemaphoreType.DMA((n,)),
                        pltpu.SemaphoreType.DMA((n,))],
    )(*halves)


def _tile_rows(rows, cols, budget=2 * 1024 * 1024):
    for t in (1024, 512, 256, 128, 64, 32, 16, 8):
        if rows % t == 0 and t * cols * 4 <= budget:
            return t
    return rows


def _add_half(g3, r3, c_idx, name):
    a, h, n = r3.shape
    t = _tile_rows(h, n)
    nt = h // t

    def body(c_ref, g_ref, r_ref, o_ref):
        o_ref[...] = (g_ref[...] + r_ref[...]).astype(o_ref.dtype)

    return pl.pallas_call(
        body, name=name, out_shape=jax.ShapeDtypeStruct((a, h, n), BF16),
        grid_spec=pltpu.PrefetchScalarGridSpec(
            num_scalar_prefetch=1, grid=(a, nt),
            in_specs=[pl.BlockSpec((1, t, n), lambda k, i, c: (k, c[0] * nt + i, 0)),
                      pl.BlockSpec((1, t, n), lambda k, i, c: (k, i, 0))],
            out_specs=pl.BlockSpec((1, t, n), lambda k, i, c: (k, i, 0))),
        compiler_params=_cparams(dimension_semantics=("parallel", "parallel")),
    )(c_idx, g3, r3)


def _sum4(parts, name):
    _, m, n = parts.shape
    t = _tile_rows(m, n, 1024 * 1024)

    def body(p_ref, o_ref):
        pv = p_ref[...].astype(F32)
        o_ref[...] = ((pv[0] + pv[1]) + pv[2]) + pv[3]

    return pl.pallas_call(
        body, name=name, out_shape=jax.ShapeDtypeStruct((m, n), F32), grid=(m // t,),
        in_specs=[pl.BlockSpec((4, t, n), lambda i: (0, i, 0))],
        out_specs=pl.BlockSpec((t, n), lambda i: (i, 0)),
        compiler_params=_cparams(dimension_semantics=("parallel",)),
    )(parts)


def _sum8(parts, name):
    _, m, n = parts.shape

    def body(p_ref, o_ref):
        acc = p_ref[0]
        for d in range(1, 8):
            acc = acc + p_ref[d]
        o_ref[...] = acc

    return pl.pallas_call(body, name=name, out_shape=jax.ShapeDtypeStruct((m, n), F32))(parts)


def _adamw(wp, gp, mp, vp, name):
    r, n = wp.shape
    t = _tile_rows(r, n, 1024 * 1024)

    def body(w_ref, g_ref, m_ref, v_ref, d_ref, mo_ref, vo_ref):
        gv = g_ref[...]
        m2 = ADAM_B1 * m_ref[...] + (1.0 - ADAM_B1) * gv
        v2 = ADAM_B2 * v_ref[...] + (1.0 - ADAM_B2) * (gv * gv)
        m_hat = m2 / (1.0 - ADAM_B1 ** ADAM_STEP)
        v_hat = v2 / (1.0 - ADAM_B2 ** ADAM_STEP)
        d_ref[...] = -ADAM_LR * (m_hat / (jnp.sqrt(v_hat) + ADAM_EPS) + ADAM_WD * w_ref[...])
        mo_ref[...] = m2
        vo_ref[...] = v2

    spec = pl.BlockSpec((t, n), lambda i: (i, 0))
    return pl.pallas_call(
        body, name=name, out_shape=[jax.ShapeDtypeStruct((r, n), F32)] * 3, grid=(r // t,),
        in_specs=[spec] * 4, out_specs=[spec] * 3,
        compiler_params=_cparams(dimension_semantics=("parallel",)),
    )(wp, gp, mp, vp)


BIG = (('ssd_w_in', 'colx'), ('ssd_w_out', 'row'), ('mla_w_in', 'row'), ('mla_w_q_b', 'col'),
       ('mla_w_kv_b', 'col'), ('mla_w_out', 'row'), ('mlp_w_up', 'col'), ('mlp_w_down', 'row'))
SMALL_SHARDED = (('meta_tokens', 1), ('ssd_conv_w', 2), ('mla_q_a_norm', 1), ('mla_kv_a_norm', 1))
SMALL_REPL = ('ln_mix', 'ln_mlp', 'ssd_conv_b', 'ssd_dt_bias', 'ssd_a_log', 'ssd_d', 'ssd_norm',
              'mla_q_norm', 'mla_k_norm')
ALL_NAMES = ('meta_tokens', 'ln_mix', 'ln_mlp', 'ssd_w_in', 'ssd_conv_w', 'ssd_conv_b', 'ssd_dt_bias',
             'ssd_a_log', 'ssd_d', 'ssd_norm', 'ssd_w_out', 'mla_w_in', 'mla_q_a_norm', 'mla_w_q_b',
             'mla_kv_a_norm', 'mla_w_kv_b', 'mla_q_norm', 'mla_k_norm', 'mla_w_out', 'mlp_w_up', 'mlp_w_down')


def _pack(arrs, rows_mult):
    flat = jnp.concatenate([a.reshape(-1) for a in arrs])
    per = LANES * rows_mult
    pad = (-flat.shape[0]) % per
    if pad:
        flat = jnp.concatenate([flat, jnp.zeros((pad,), flat.dtype)])
    return flat.reshape(-1, LANES)


def _unpack(pack, shapes):
    flat = pack.reshape(-1)
    out, off = [], 0
    for shp in shapes:
        n = math.prod(shp)
        out.append(flat[off:off + n].reshape(shp))
        off += n
    return out


def _split4(full, axis):
    shp = full.shape
    r = full.reshape(shp[:axis] + (4, shp[axis] // 4) + shp[axis + 1:])
    return jnp.moveaxis(r, axis, 0)


def _join4(parts, axis):
    r = jnp.moveaxis(parts, 0, axis)
    shp = r.shape
    return r.reshape(shp[:axis] + (shp[axis] * shp[axis + 1],) + shp[axis + 2:])


def _gather_params(shards, table, dtype, c, name):
    pack = _pack([shards[n].astype(dtype) for n, _ in table], 16)
    half = pack.shape[0] // 2
    mine = lax.dynamic_slice_in_dim(pack, c * half, half, axis=0)
    full = _all_gather8(mine, name).reshape(4, -1)
    out, off = {}, 0
    for n, ax in table:
        cnt = math.prod(shards[n].shape)
        out[n] = _join4(full[:, off:off + cnt].reshape((4,) + shards[n].shape), ax)
        off += cnt
    return out


def kernel(x, meta_tokens, ln_mix, ln_mlp, ssd_w_in, ssd_conv_w, ssd_conv_b, ssd_dt_bias, ssd_a_log, ssd_d, ssd_norm, ssd_w_out, mla_w_in, mla_q_a_norm, mla_w_q_b, mla_kv_a_norm, mla_w_kv_b, mla_q_norm, mla_k_norm, mla_w_out, mlp_w_up, mlp_w_down, loss_target, m_meta_tokens, m_ln_mix, m_ln_mlp, m_ssd_w_in, m_ssd_conv_w, m_ssd_conv_b, m_ssd_dt_bias, m_ssd_a_log, m_ssd_d, m_ssd_norm, m_ssd_w_out, m_mla_w_in, m_mla_q_a_norm, m_mla_w_q_b, m_mla_kv_a_norm, m_mla_w_kv_b, m_mla_q_norm, m_mla_k_norm, m_mla_w_out, m_mlp_w_up, m_mlp_w_down, v_meta_tokens, v_ln_mix, v_ln_mlp, v_ssd_w_in, v_ssd_conv_w, v_ssd_conv_b, v_ssd_dt_bias, v_ssd_a_log, v_ssd_d, v_ssd_norm, v_ssd_w_out, v_mla_w_in, v_mla_q_a_norm, v_mla_w_q_b, v_mla_kv_a_norm, v_mla_w_kv_b, v_mla_q_norm, v_mla_k_norm, v_mla_w_out, v_mlp_w_up, v_mlp_w_down):
    w_sh = dict(meta_tokens=meta_tokens, ln_mix=ln_mix, ln_mlp=ln_mlp, ssd_w_in=ssd_w_in, ssd_conv_w=ssd_conv_w, ssd_conv_b=ssd_conv_b, ssd_dt_bias=ssd_dt_bias, ssd_a_log=ssd_a_log, ssd_d=ssd_d, ssd_norm=ssd_norm, ssd_w_out=ssd_w_out, mla_w_in=mla_w_in, mla_q_a_norm=mla_q_a_norm, mla_w_q_b=mla_w_q_b, mla_kv_a_norm=mla_kv_a_norm, mla_w_kv_b=mla_w_kv_b, mla_q_norm=mla_q_norm, mla_k_norm=mla_k_norm, mla_w_out=mla_w_out, mlp_w_up=mlp_w_up, mlp_w_down=mlp_w_down)
    m_sh = dict(meta_tokens=m_meta_tokens, ln_mix=m_ln_mix, ln_mlp=m_ln_mlp, ssd_w_in=m_ssd_w_in, ssd_conv_w=m_ssd_conv_w, ssd_conv_b=m_ssd_conv_b, ssd_dt_bias=m_ssd_dt_bias, ssd_a_log=m_ssd_a_log, ssd_d=m_ssd_d, ssd_norm=m_ssd_norm, ssd_w_out=m_ssd_w_out, mla_w_in=m_mla_w_in, mla_q_a_norm=m_mla_q_a_norm, mla_w_q_b=m_mla_w_q_b, mla_kv_a_norm=m_mla_kv_a_norm, mla_w_kv_b=m_mla_w_kv_b, mla_q_norm=m_mla_q_norm, mla_k_norm=m_mla_k_norm, mla_w_out=m_mla_w_out, mlp_w_up=m_mlp_w_up, mlp_w_down=m_mlp_w_down)
    v_sh = dict(meta_tokens=v_meta_tokens, ln_mix=v_ln_mix, ln_mlp=v_ln_mlp, ssd_w_in=v_ssd_w_in, ssd_conv_w=v_ssd_conv_w, ssd_conv_b=v_ssd_conv_b, ssd_dt_bias=v_ssd_dt_bias, ssd_a_log=v_ssd_a_log, ssd_d=v_ssd_d, ssd_norm=v_ssd_norm, ssd_w_out=v_ssd_w_out, mla_w_in=v_mla_w_in, mla_q_a_norm=v_mla_q_a_norm, mla_w_q_b=v_mla_w_q_b, mla_kv_a_norm=v_mla_kv_a_norm, mla_w_kv_b=v_mla_w_kv_b, mla_q_norm=v_mla_q_norm, mla_k_norm=v_mla_k_norm, mla_w_out=v_mla_w_out, mlp_w_up=v_mlp_w_up, mlp_w_down=v_mlp_w_down)

    cx, cy, cc = lax.axis_index("x"), lax.axis_index("y"), lax.axis_index("c")
    chip = 2 * cx + cy

    c_idx = cc.reshape(1).astype(jnp.int32)
    big_names = [n for n, _ in BIG]
    kinds = [k for _, k in BIG]
    shapes = [w_sh[n].shape for n in big_names]

    w = {n: w_sh[n] for n in SMALL_REPL}
    wholes = _gather_big([w_sh[n].astype(BF16) for n in big_names], kinds, "gather_big")
    for n, kind, s, f in zip(big_names, kinds, shapes, wholes):
        if kind == 'row':
            w[n] = f.reshape(s[0], 4 * s[1], s[2])
        elif kind == 'col':
            w[n] = f
        else:
            w[n] = jnp.concatenate([f[k] for k in range(4)], axis=-1)
    w.update(_gather_params(w_sh, SMALL_SHARDED, F32, cc, "gather_small"))

    loss_row, grad_x, grads = _local_step(x[0], loss_target[0], w)
    loss = lax.psum(jnp.sum(loss_row), ("x", "y", "c"))

    g_whole = []
    for n, kind, s in zip(big_names, kinds, shapes):
        if kind == 'row':
            g_whole.append(grads[n].reshape(s[0], 4, s[1], s[2]))
        elif kind == 'col':
            g_whole.append(grads[n])
        else:
            g_whole.append(jnp.stack([grads[n][..., k * s[2]:(k + 1) * s[2]] for k in range(4)]))
    recv = _rs_swap(g_whole, kinds, shapes, "rs_swap")
    parts = []
    for n, kind, s, g, r in zip(big_names, kinds, shapes, g_whole, recv):
        if kind == 'col':
            g3, r3 = g, r
        else:
            g3, r3 = g.reshape(-1, s[1], s[2]), r.reshape(-1, s[1] // 2, s[2])
        parts.append(_add_half(g3, r3, c_idx, f"rs_add_{n}").reshape(r.shape))
    got = _rs_exchange(parts, kinds, shapes, "rs_exchange")
    halves = [_sum4(p.reshape(4, -1, s[2]), f"rs_sum_{n}").reshape(s[0], s[1] // 2, s[2])
              for n, s, p in zip(big_names, shapes, got)]
    g_sh = dict(zip(big_names, _rs_share(halves, "rs_share")))

    small_names = tuple(n for n, _ in SMALL_SHARDED) + SMALL_REPL
    sp = _pack([grads[n] for n in small_names], 8)
    srows = sp.shape[0]
    s_all = _sum8(_all_gather8(sp, "ar_small_gather").reshape(8, srows, LANES), "ar_small_sum")
    s_full = dict(zip(small_names, _unpack(s_all, [grads[n].shape for n in small_names])))
    for n, ax in SMALL_SHARDED:
        g_sh[n] = lax.dynamic_index_in_dim(_split4(s_full[n], ax), chip, axis=0, keepdims=False)
    for n in SMALL_REPL:
        g_sh[n] = s_full[n]

    delta, new_m, new_v = {}, {}, {}
    for n, s in zip(big_names, shapes):
        res = _adamw(*[t[n].reshape(-1, s[2]) for t in (w_sh, g_sh, m_sh, v_sh)], f"adamw_{n}")
        delta[n], new_m[n], new_v[n] = [r.reshape(s) for r in res]
    d_s, m_s, v_s = _adamw(*[_pack([t[n] for n in small_names], 8) for t in (w_sh, g_sh, m_sh, v_sh)],
                           "adamw_small")
    for dst, ps in ((delta, d_s), (new_m, m_s), (new_v, v_s)):
        dst.update(zip(small_names, _unpack(ps, [w_sh[n].shape for n in small_names])))

    return (loss, grad_x[None], *[g_sh[n] for n in ALL_NAMES], *[delta[n] for n in ALL_NAMES],
            *[new_m[n] for n in ALL_NAMES], *[new_v[n] for n in ALL_NAMES])
```

```python
import functools
import math

import jax
import jax.numpy as jnp
from jax import lax
from jax.experimental import pallas as pl
from jax.experimental.pallas import tpu as pltpu

F32 = jnp.float32
BF16 = jnp.bfloat16
MESH = pl.DeviceIdType.MESH
_NN = (((1,), (0,)), ((), ()))
_NT = (((1,), (1,)), ((), ()))
_TN = (((0,), (0,)), ((), ()))

D_MODEL = 1024
N_META = 16
EPS = 1e-6
SSD_D_INNER = 2048
SSD_HEADS = 32
SSD_HEAD_DIM = 64
SSD_GROUPS = 8
SSD_HPG = 4
SSD_STATE = 128
SSD_CONV = 4
CHUNK = 128
SSD_IN_DIM = 6176
SSD_IN_PAD = 6272
MLA_HEADS = 16
MLA_NOPE = 64
MLA_ROPE = 32
MLA_V = 64
MLA_QK = 96
MLA_Q_RANK = 384
MLA_KV_RANK = 256
HEAD_SLOT = 128
MLA_WIDE = MLA_HEADS * HEAD_SLOT
HEADS_PER_STEP = 2
LAT_PAD = 768
ROPE_THETA = 10000.0
D_FF = 4096
NPAD = CHUNK - N_META
ADAM_LR, ADAM_B1, ADAM_B2, ADAM_EPS, ADAM_WD, ADAM_STEP = 0.001, 0.9, 0.999, 1e-08, 0.01, 10
LANES = 1024
VMEM_LIMIT = 56 * 1024 * 1024


def _pick(n, cands):
    for c in cands:
        if n % c == 0:
            return c
    return n


def _cparams(**kw):
    return pltpu.CompilerParams(vmem_limit_bytes=VMEM_LIMIT, **kw)


def _mm(a, b, dims, *, name, out_dtype=F32, a_fn=None, epi=None, extras=()):
    if dims == 'nn':
        (M, K), (K2, N) = a.shape, b.shape
    elif dims == 'nt':
        (M, K), (N, K2) = a.shape, b.shape
    else:
        (K, M), (K2, N) = a.shape, b.shape
    assert K == K2, (a.shape, b.shape, dims)
    if dims == 'tn':
        tm = _pick(M, (1024, 768, 512, 384, 256, 128))
        tn = _pick(N, (1024, 896, 768, 512, 384, 256, 128))
        tk = _pick(K, (1408, 1024, 512, 384, 256, 128))
    else:
        tm = _pick(M, (1408, 1024, 512, 384, 256, 128))
        tn = _pick(N, (512, 896, 768, 384, 256, 128))
        tk = _pick(K, (1024, 896, 768, 512, 384, 256, 128))
    nk = K // tk
    if dims == 'nn':
        a_spec = pl.BlockSpec((tm, tk), lambda i, j, k: (i, k))
        b_spec = pl.BlockSpec((tk, tn), lambda i, j, k: (k, j))
        dn = (((1,), (0,)), ((), ()))
    elif dims == 'nt':
        a_spec = pl.BlockSpec((tm, tk), lambda i, j, k: (i, k))
        b_spec = pl.BlockSpec((tn, tk), lambda i, j, k: (j, k))
        dn = (((1,), (1,)), ((), ()))
    else:
        a_spec = pl.BlockSpec((tk, tm), lambda i, j, k: (k, i))
        b_spec = pl.BlockSpec((tk, tn), lambda i, j, k: (k, j))
        dn = (((0,), (0,)), ((), ()))
    o_spec = pl.BlockSpec((tm, tn), lambda i, j, k: (i, j))
    n_ex = len(extras)

    def body(a_ref, b_ref, *rest):
        ex_refs, o_ref, acc = rest[:n_ex], rest[n_ex], rest[n_ex + 1]
        k = pl.program_id(2)

        @pl.when(k == 0)
        def _():
            acc[...] = jnp.zeros_like(acc)

        av = a_ref[...]
        if a_fn is not None:
            av = a_fn(av)
        acc[...] += lax.dot_general(av.astype(BF16), b_ref[...].astype(BF16), dn,
                                    preferred_element_type=F32)

        @pl.when(k == nk - 1)
        def _():
            r = acc[...]
            if epi is not None:
                r = epi(r, *[e[...] for e in ex_refs])
            o_ref[...] = r.astype(out_dtype)

    return pl.pallas_call(
        body, name=name,
        out_shape=jax.ShapeDtypeStruct((M, N), out_dtype),
        grid=(M // tm, N // tn, nk),
        in_specs=[a_spec, b_spec] + [o_spec] * n_ex,
        out_specs=o_spec,
        scratch_shapes=[pltpu.VMEM((tm, tn), F32)],
        compiler_params=_cparams(dimension_semantics=("parallel", "parallel", "arbitrary")),
    )(a, b, *extras)


def _row_call(fn, rows, consts, out_rows, out_accs=(), *, n_rows, tile, name):
    n_r, n_c, n_o, n_a = len(rows), len(consts), len(out_rows), len(out_accs)
    steps = n_rows // tile

    def body(*refs):
        r_refs = refs[:n_r]
        c_refs = refs[n_r:n_r + n_c]
        o_refs = refs[n_r + n_c:n_r + n_c + n_o]
        a_refs = refs[n_r + n_c + n_o:]
        i = pl.program_id(0)
        res = fn(i, *[r[...] for r in r_refs], *[c[...] for c in c_refs])
        for o_ref, val in zip(o_refs, res[:n_o]):
            o_ref[...] = val.astype(o_ref.dtype)

        @pl.when(i == 0)
        def _():
            for a_ref in a_refs:
                a_ref[...] = jnp.zeros_like(a_ref)

        for a_ref, val in zip(a_refs, res[n_o:]):
            a_ref[...] += val

    in_specs = [pl.BlockSpec((tile, w), functools.partial(lambda i, cb: (i, cb), cb=cb))
                for (_, w, cb) in rows]
    in_specs += [pl.BlockSpec(c.shape, lambda i: (0, 0)) for c in consts]
    out_specs = [pl.BlockSpec((tile, c), lambda i: (i, 0)) for (c, _) in out_rows]
    out_specs += [pl.BlockSpec(s, lambda i: (0, 0)) for s in out_accs]
    out_shape = [jax.ShapeDtypeStruct((n_rows, c), dt) for (c, dt) in out_rows]
    out_shape += [jax.ShapeDtypeStruct(s, F32) for s in out_accs]
    return pl.pallas_call(
        body, name=name, out_shape=out_shape, grid=(steps,),
        in_specs=in_specs, out_specs=out_specs,
        compiler_params=_cparams(dimension_semantics=("arbitrary",)),
    )(*[r[0] for r in rows], *consts)


def _row_mask(i, tile):
    r = i * tile + lax.broadcasted_iota(jnp.int32, (tile, 1), 0)
    return (r >= NPAD).astype(F32)


def _rms(x, g):
    return x * lax.rsqrt(jnp.mean(x * x, axis=-1, keepdims=True) + EPS) * g


def _silu(x):
    return x * (0.5 * jnp.tanh(0.5 * x) + 0.5)


def _softplus(x):
    return jnp.maximum(x, 0.0) + jnp.log(1.0 + jnp.exp(-jnp.abs(x)))


def _rms_fwd(h, g, name):
    lp = h.shape[0]
    return _row_call(lambda i, hv, gv: (_rms(hv, gv),), [(h, D_MODEL, 0)], [g],
                     [(D_MODEL, BF16)], n_rows=lp, tile=_pick(lp, (384, 256, 128)), name=name)[0]


def _rms_bwd(h, g, d_hn, d_res, name):
    lp = h.shape[0]
    tile = _pick(lp, (384, 256, 128))

    def fn(i, hv, dv, rv, gv):
        _, vjp = jax.vjp(_rms, hv, gv)
        dh, dg = vjp(dv)
        return (rv + dh) * _row_mask(i, tile), dg

    return _row_call(fn, [(h, D_MODEL, 0), (d_hn, D_MODEL, 0), (d_res, D_MODEL, 0)], [g],
                     [(D_MODEL, F32)], [(1, D_MODEL)], n_rows=lp, tile=tile, name=name)


@functools.partial(jax.custom_vjp, nondiff_argnums=(1,))
def _roll_rows(x, s):
    return pltpu.roll(x, s, 0)


def _roll_rows_fwd(x, s):
    return pltpu.roll(x, s, 0), None


def _roll_rows_bwd(s, _, ct):
    return (pltpu.roll(ct, (ct.shape[0] - s) % ct.shape[0], 0),)


_roll_rows.defvjp(_roll_rows_fwd, _roll_rows_bwd)


def _conv_silu(cur, halo, w_rows, b):
    full = jnp.concatenate([halo, cur], axis=0)
    acc = cur * w_rows[SSD_CONV - 1] + b
    for k in range(SSD_CONV - 1):
        acc = acc + _roll_rows(full, SSD_CONV - 1 - k)[8:] * w_rows[k]
    return _silu(acc)


def _split3(v):
    hi = v.astype(BF16)
    r1 = v - hi.astype(F32)
    mid = r1.astype(BF16)
    lo = (r1 - mid.astype(F32)).astype(BF16)
    return hi, mid, lo


def _select_right(v, sel, dn):
    return sum(lax.dot_general(p, sel, dn, preferred_element_type=F32) for p in _split3(v))


@jax.custom_vjp
def _expand_heads(v, e_mat):
    return _select_right(v, e_mat, _NN)


def _expand_heads_fwd(v, e_mat):
    return _select_right(v, e_mat, _NN), e_mat


def _expand_heads_bwd(e_mat, ct):
    return _select_right(ct, e_mat, _NT), jnp.zeros_like(e_mat)


_expand_heads.defvjp(_expand_heads_fwd, _expand_heads_bwd)


@jax.custom_vjp
def _cumsum_rows(a, tri):
    return sum(lax.dot_general(tri, p, _NN, preferred_element_type=F32) for p in _split3(a))


def _cumsum_rows_fwd(a, tri):
    return _cumsum_rows(a, tri), tri


def _cumsum_rows_bwd(tri, ct):
    return (sum(lax.dot_general(tri, p, _TN, preferred_element_type=F32) for p in _split3(ct)),
            jnp.zeros_like(tri))


_cumsum_rows.defvjp(_cumsum_rows_fwd, _cumsum_rows_bwd)


def _ssd_chunk(mask, z, xs_pre, bc_pre, halo_x, halo_bc, dt_pre, st, cwx0, cwx1, cwx2, cwx3,
               cwb0, cwb1, cwb2, cwb3, cb_x, cb_bc, dtb, alog, dsk, ng):
    L = CHUNK
    lane_h = lax.broadcasted_iota(jnp.int32, (1, 128), 1)
    head_ok = (lane_h < SSD_HEADS).astype(F32)
    e_mat = (lax.broadcasted_iota(jnp.int32, (128, SSD_D_INNER), 1) // SSD_HEAD_DIM
             == lax.broadcasted_iota(jnp.int32, (128, SSD_D_INNER), 0)).astype(BF16)
    ri = lax.broadcasted_iota(jnp.int32, (L, L), 0)
    ci = lax.broadcasted_iota(jnp.int32, (L, L), 1)
    causal = ri >= ci

    xs = _conv_silu(xs_pre, halo_x, (cwx0, cwx1, cwx2, cwx3), cb_x) * mask
    bc = _conv_silu(bc_pre, halo_bc, (cwb0, cwb1, cwb2, cwb3), cb_bc) * mask
    dt = _softplus(dt_pre + dtb) * mask * head_ok
    a_dt = dt * (-jnp.exp(alog))
    a_cs = _cumsum_rows(a_dt, causal.astype(BF16))
    a_cs_t = a_cs.T
    row8 = lax.broadcasted_iota(jnp.int32, (8, 128), 0)
    last8 = jnp.where(row8 == 0, jnp.sum(a_dt, axis=0, keepdims=True), 0.0)
    dsk8 = jnp.where(row8 == 0, dsk, 0.0)
    wide = _expand_heads(jnp.concatenate([dt, a_cs, last8, dsk8], axis=0), e_mat)
    dt_e, acs_e = wide[0:L], wide[L:2 * L]
    last_e = jnp.sum(wide[2 * L:2 * L + 8], axis=0, keepdims=True)
    d_e = jnp.sum(wide[2 * L + 8:2 * L + 16], axis=0, keepdims=True)
    xdt = xs * dt_e
    dte_e = jnp.exp(last_e - acs_e)
    dfs_e = jnp.exp(acs_e)
    cd_e = jnp.exp(last_e)
    sub_h = lax.broadcasted_iota(jnp.int32, (128, L), 0)
    lane_hl = lax.broadcasted_iota(jnp.int32, (L, 128), 1)
    lane_g = lax.broadcasted_iota(jnp.int32, (1, SSD_HPG * SSD_HEAD_DIM), 1) // SSD_HEAD_DIM

    ys, new_st = [], []
    for g in range(SSD_GROUPS):
        b_g = bc[:, g * 128:(g + 1) * 128].astype(BF16)
        c_g = bc[:, 1024 + g * 128:1024 + (g + 1) * 128].astype(BF16)
        gs = slice(g * 256, (g + 1) * 256)
        xdt_g = xdt[:, gs]
        cb = lax.dot_general(c_g, b_g, (((1,), (1,)), ((), ())), preferred_element_type=F32)
        st_g = st[g * 128:(g + 1) * 128, :]
        y_g = lax.dot_general(c_g, st_g.astype(BF16), (((1,), (0,)), ((), ())),
                              preferred_element_type=F32) * dfs_e[:, gs]
        for j in range(SSD_HPG):
            h = g * SSD_HPG + j
            col = jnp.sum(jnp.where(lane_hl == h, a_cs, 0.0), axis=1, keepdims=True)
            row = jnp.sum(jnp.where(sub_h == h, a_cs_t, 0.0), axis=0, keepdims=True)
            dec = jnp.where(causal, jnp.exp(jnp.where(causal, col - row, 0.0)), 0.0)
            m_h = (cb * dec).astype(BF16)
            x_h = jnp.where(lane_g == j, xdt_g, 0.0).astype(BF16)
            y_g = y_g + lax.dot_general(m_h, x_h, (((1,), (0,)), ((), ())),
                                        preferred_element_type=F32)
        s_new = lax.dot_general(b_g, (xdt_g * dte_e[:, gs]).astype(BF16), (((0,), (0,)), ((), ())),
                                preferred_element_type=F32)
        new_st.append(st_g * cd_e[:, gs] + s_new)
        ys.append(y_g)
    y = jnp.concatenate(ys, axis=1) + xs * d_e
    gg = y * _silu(z)
    outs = []
    for g in range(SSD_GROUPS):
        sl = gg[:, g * 256:(g + 1) * 256]
        outs.append(sl * lax.rsqrt(jnp.mean(sl * sl, axis=-1, keepdims=True) + EPS))
    out = jnp.concatenate(outs, axis=1) * ng
    return out, jnp.concatenate(new_st, axis=0)


def _ssd_consts(conv_w, conv_b, dtb, alog, dsk, ng):
    return [conv_w, conv_b, dtb, alog, dsk, ng]


def _ssd_param_vals(cw_ref, cb_ref, dtb_ref, alog_ref, dsk_ref, ng_ref):
    cwx = [cw_ref[k:k + 1, 0:SSD_D_INNER] for k in range(SSD_CONV)]
    cwb = [cw_ref[k:k + 1, SSD_D_INNER:2 * SSD_D_INNER] for k in range(SSD_CONV)]
    return (*cwx, *cwb, cb_ref[:, 0:SSD_D_INNER], cb_ref[:, SSD_D_INNER:2 * SSD_D_INNER],
            dtb_ref[...], alog_ref[...], dsk_ref[...], ng_ref[...])


def _ssd_in_specs(rev, nc):
    def cidx(i):
        return (nc - 1 - i) if rev else i

    def halo(cb):
        return pl.BlockSpec((8, SSD_D_INNER), lambda i: (jnp.maximum(16 * cidx(i) - 1, 0), cb))

    return [
        pl.BlockSpec((CHUNK, SSD_D_INNER), lambda i: (cidx(i), 0)),
        pl.BlockSpec((CHUNK, SSD_D_INNER), lambda i: (cidx(i), 1)),
        pl.BlockSpec((CHUNK, SSD_D_INNER), lambda i: (cidx(i), 2)),
        halo(1), halo(2),
        pl.BlockSpec((CHUNK, 128), lambda i: (cidx(i), 48)),
    ]


def _ssd_fwd(zxd, consts, name):
    lp = zxd.shape[0]
    nc = lp // CHUNK

    def body(z_ref, xs_ref, bc_ref, hx_ref, hb_ref, dt_ref, cw_ref, cb_ref, dtb_ref, alog_ref,
             dsk_ref, ng_ref, y_ref, st_ref, state):
        c = pl.program_id(0)

        @pl.when(c == 0)
        def _():
            state[...] = jnp.zeros_like(state)

        live = (c > 0).astype(F32)
        st_ref[0] = state[...]
        out, st_new = _ssd_chunk(
            _row_mask(c, CHUNK), z_ref[...], xs_ref[...], bc_ref[...], hx_ref[...] * live,
            hb_ref[...] * live, dt_ref[...], state[...],
            *_ssd_param_vals(cw_ref, cb_ref, dtb_ref, alog_ref, dsk_ref, ng_ref))
        y_ref[...] = out.astype(y_ref.dtype)
        state[...] = st_new

    return pl.pallas_call(
        body, name=name,
        out_shape=[jax.ShapeDtypeStruct((lp, SSD_D_INNER), BF16),
                   jax.ShapeDtypeStruct((nc, SSD_GROUPS * SSD_STATE, 256), F32)],
        grid=(nc,),
        in_specs=_ssd_in_specs(False, nc) + [pl.BlockSpec(c.shape, lambda i: (0, 0)) for c in consts],
        out_specs=[pl.BlockSpec((CHUNK, SSD_D_INNER), lambda i: (i, 0)),
                   pl.BlockSpec((1, SSD_GROUPS * SSD_STATE, 256), lambda i: (i, 0, 0))],
        scratch_shapes=[pltpu.VMEM((SSD_GROUPS * SSD_STATE, 256), F32)],
        compiler_params=_cparams(dimension_semantics=("arbitrary",)),
    )(zxd, zxd, zxd, zxd, zxd, zxd, *consts)


def _ssd_bwd(zxd, states, d_y, consts, name):
    lp = zxd.shape[0]
    nc = lp // CHUNK

    def body(z_ref, xs_ref, bc_ref, hx_ref, hb_ref, dt_ref, st_ref, dy_ref, cw_ref, cb_ref, dtb_ref,
             alog_ref, dsk_ref, ng_ref, dz_ref, dcw_ref, dcb_ref, ddtb_ref, dalog_ref, ddsk_ref,
             dng_ref, d_state, d_hx, d_hb):
        i = pl.program_id(0)
        c = nc - 1 - i

        @pl.when(i == 0)
        def _():
            d_state[...] = jnp.zeros_like(d_state)
            d_hx[...] = jnp.zeros_like(d_hx)
            d_hb[...] = jnp.zeros_like(d_hb)
            for r in (dcw_ref, dcb_ref, ddtb_ref, dalog_ref, ddsk_ref, dng_ref):
                r[...] = jnp.zeros_like(r)

        live = (c > 0).astype(F32)
        fn = functools.partial(_ssd_chunk, _row_mask(c, CHUNK))
        prim = (z_ref[...], xs_ref[...], bc_ref[...], hx_ref[...] * live, hb_ref[...] * live,
                dt_ref[...], st_ref[0],
                *_ssd_param_vals(cw_ref, cb_ref, dtb_ref, alog_ref, dsk_ref, ng_ref))
        _, vjp = jax.vjp(fn, *prim)
        (d_z, d_xs, d_bc, g_hx, g_hb, d_dt, g_st, *d_par) = vjp((dy_ref[...], d_state[...]))
        zeros = jnp.zeros((CHUNK - 8, SSD_D_INNER), F32)
        d_xs = d_xs + jnp.concatenate([zeros, d_hx[...]], axis=0)
        d_bc = d_bc + jnp.concatenate([zeros, d_hb[...]], axis=0)
        dz_ref[:, 0:SSD_D_INNER] = d_z
        dz_ref[:, SSD_D_INNER:2 * SSD_D_INNER] = d_xs
        dz_ref[:, 2 * SSD_D_INNER:3 * SSD_D_INNER] = d_bc
        dz_ref[:, 3 * SSD_D_INNER:] = d_dt
        d_state[...] = g_st
        d_hx[...] = g_hx * live
        d_hb[...] = g_hb * live
        for k in range(SSD_CONV):
            dcw_ref[k:k + 1, 0:SSD_D_INNER] += d_par[k]
            dcw_ref[k:k + 1, SSD_D_INNER:2 * SSD_D_INNER] += d_par[SSD_CONV + k]
        dcb_ref[:, 0:SSD_D_INNER] += d_par[8]
        dcb_ref[:, SSD_D_INNER:2 * SSD_D_INNER] += d_par[9]
        ddtb_ref[...] += d_par[10]
        dalog_ref[...] += d_par[11]
        ddsk_ref[...] += d_par[12]
        dng_ref[...] += d_par[13]

    const_specs = [pl.BlockSpec(c.shape, lambda i: (0, 0)) for c in consts]
    return pl.pallas_call(
        body, name=name,
        out_shape=[jax.ShapeDtypeStruct((lp, SSD_IN_PAD), F32)]
        + [jax.ShapeDtypeStruct(c.shape, F32) for c in consts],
        grid=(nc,),
        in_specs=_ssd_in_specs(True, nc)
        + [pl.BlockSpec((1, SSD_GROUPS * SSD_STATE, 256), lambda i: (nc - 1 - i, 0, 0)),
           pl.BlockSpec((CHUNK, SSD_D_INNER), lambda i: (nc - 1 - i, 0))] + const_specs,
        out_specs=[pl.BlockSpec((CHUNK, SSD_IN_PAD), lambda i: (nc - 1 - i, 0))] + const_specs,
        scratch_shapes=[pltpu.VMEM((SSD_GROUPS * SSD_STATE, 256), F32),
                        pltpu.VMEM((8, SSD_D_INNER), F32), pltpu.VMEM((8, SSD_D_INNER), F32)],
        compiler_params=_cparams(dimension_semantics=("arbitrary",)),
    )(zxd, zxd, zxd, zxd, zxd, zxd, states, d_y, *consts)


@jax.custom_vjp
def _rot_half(x):
    lane = lax.broadcasted_iota(jnp.int32, x.shape, 1)
    lo = (lane >= MLA_NOPE) & (lane < MLA_NOPE + MLA_ROPE // 2)
    hi = (lane >= MLA_NOPE + MLA_ROPE // 2) & (lane < MLA_QK)
    down = pltpu.roll(x, HEAD_SLOT - MLA_ROPE // 2, 1)
    up = pltpu.roll(x, MLA_ROPE // 2, 1)
    return jnp.where(lo, -down, jnp.where(hi, up, 0.0))


def _rot_half_fwd(x):
    return _rot_half(x), None


def _rot_half_bwd(_, ct):
    return (-_rot_half(ct),)


_rot_half.defvjp(_rot_half_fwd, _rot_half_bwd)


def _head_norm_rope(t, gain, cos, sin):
    n = t * lax.rsqrt(jnp.sum(t * t, axis=-1, keepdims=True) * (1.0 / MLA_QK) + EPS) * gain
    return n * cos + _rot_half(n) * sin


def _qk_prep(q_raw, kn_raw, kpe, cos, sin, qg, kg):
    qs, ks = [], []
    for h in range(MLA_HEADS):
        sl = slice(h * HEAD_SLOT, (h + 1) * HEAD_SLOT)
        qs.append(_head_norm_rope(q_raw[:, sl], qg, cos, sin))
        ks.append(_head_norm_rope(kn_raw[:, sl] + kpe, kg, cos, sin))
    return jnp.concatenate(qs, axis=1), jnp.concatenate(ks, axis=1)


def _lat_norm(kv_lat, q_lat, kvg, qg):
    return _rms(kv_lat, kvg), _rms(q_lat, qg)


_NEG = -1e30
_SCALE = MLA_QK ** -0.5


def _diag_mask(blk, t):
    qpos = blk * t + lax.broadcasted_iota(jnp.int32, (t, t), 0)
    kpos = blk * t + lax.broadcasted_iota(jnp.int32, (t, t), 1)
    return (kpos <= qpos) & ((kpos >= NPAD) | (kpos == qpos))


def _key_ok(blk, t):
    return blk * t + lax.broadcasted_iota(jnp.int32, (1, t), 1) >= NPAD


def _attn_fwd(q, k, v, name):
    lp = q.shape[0]
    t = _pick(lp, (384, 256, 128))
    nb = lp // t
    wide = HEADS_PER_STEP * HEAD_SLOT
    heads = [slice(a * HEAD_SLOT, (a + 1) * HEAD_SLOT) for a in range(HEADS_PER_STEP)]

    def body(q_ref, k_ref, v_ref, o_ref, lse_ref):
        qi = pl.program_id(1)

        def update(a, ki, carry, mask):
            rows = pl.ds(pl.multiple_of(ki * t, t), t)
            m, l, acc = carry
            s = lax.dot_general(q_ref[:, heads[a]], k_ref[rows, heads[a]], _NT,
                                preferred_element_type=F32) * _SCALE
            s = jnp.where(mask, s, _NEG)
            m_new = jnp.maximum(m, jnp.max(s, axis=-1, keepdims=True))
            alpha = jnp.exp(m - m_new)
            p = jnp.exp(s - m_new)
            l = alpha * l + jnp.sum(p, axis=-1, keepdims=True)
            acc = alpha * acc + lax.dot_general(p.astype(BF16), v_ref[rows, heads[a]], _NN,
                                                preferred_element_type=F32)
            return m_new, l, acc

        def off_diag(ki, carry):
            return tuple(update(a, ki, carry[a], _key_ok(ki, t)) for a in range(HEADS_PER_STEP))

        init = (jnp.full((t, 1), _NEG, F32), jnp.zeros((t, 1), F32), jnp.zeros((t, HEAD_SLOT), F32))
        carry = lax.fori_loop(0, qi, off_diag, (init,) * HEADS_PER_STEP)
        for a in range(HEADS_PER_STEP):
            m, l, acc = update(a, qi, carry[a], _diag_mask(qi, t))
            o_ref[:, heads[a]] = acc / l * _row_mask(qi, t)
            lse_ref[a] = m + jnp.log(l)

    qspec = pl.BlockSpec((t, wide), lambda g, i: (i, g))
    kspec = pl.BlockSpec((lp, wide), lambda g, i: (0, g))
    return pl.pallas_call(
        body, name=name,
        out_shape=[jax.ShapeDtypeStruct((lp, MLA_WIDE), F32),
                   jax.ShapeDtypeStruct((MLA_HEADS, lp, 1), F32)],
        grid=(MLA_HEADS // HEADS_PER_STEP, nb),
        in_specs=[qspec, kspec, kspec],
        out_specs=[qspec, pl.BlockSpec((HEADS_PER_STEP, t, 1), lambda g, i: (g, i, 0))],
        compiler_params=_cparams(dimension_semantics=("parallel", "arbitrary")),
    )(q, k, v)


def _attn_delta(do, o, name):
    lp = do.shape[0]
    t = _pick(lp, (384, 256, 128))

    def body(do_ref, o_ref, dob_ref, delta_ref):
        dob_ref[...] = do_ref[...].astype(BF16)
        for h in range(MLA_HEADS):
            sl = slice(h * HEAD_SLOT, (h + 1) * HEAD_SLOT)
            delta_ref[h] = jnp.sum(do_ref[:, sl] * o_ref[:, sl], axis=-1, keepdims=True)

    spec = pl.BlockSpec((t, MLA_WIDE), lambda i: (i, 0))
    return pl.pallas_call(
        body, name=name,
        out_shape=[jax.ShapeDtypeStruct((lp, MLA_WIDE), BF16), jax.ShapeDtypeStruct((MLA_HEADS, lp, 1), F32)],
        grid=(lp // t,), in_specs=[spec, spec],
        out_specs=[spec, pl.BlockSpec((MLA_HEADS, t, 1), lambda i: (0, i, 0))],
        compiler_params=_cparams(dimension_semantics=("parallel",)),
    )(do, o)


def _attn_bwd(q, k, v, do, lse, delta, name):
    lp = q.shape[0]
    t = _pick(lp, (384, 256, 128))
    nb = lp // t
    wide = HEADS_PER_STEP * HEAD_SLOT
    heads = [slice(a * HEAD_SLOT, (a + 1) * HEAD_SLOT) for a in range(HEADS_PER_STEP)]

    def body(q_ref, k_ref, v_ref, do_ref, lse_ref, delta_ref, dq_ref, dk_ref, dv_ref):
        kj = pl.program_id(1)

        @pl.when(kj == 0)
        def _():
            dq_ref[...] = jnp.zeros_like(dq_ref)

        def tile(a, qi, mask):
            rows = pl.ds(pl.multiple_of(qi * t, t), t)
            qb, dob = q_ref[rows, heads[a]], do_ref[rows, heads[a]]
            kb, vb = k_ref[:, heads[a]], v_ref[:, heads[a]]
            s = lax.dot_general(qb, kb, _NT, preferred_element_type=F32) * _SCALE
            p = jnp.where(mask, jnp.exp(s - lse_ref[a, rows, :]), 0.0)
            dp = lax.dot_general(dob, vb, _NT, preferred_element_type=F32)
            ds = (p * (dp - delta_ref[a, rows, :])).astype(BF16)
            dq_ref[rows, heads[a]] += lax.dot_general(ds, kb, _NN, preferred_element_type=F32) * _SCALE
            return (lax.dot_general(p.astype(BF16), dob, _TN, preferred_element_type=F32),
                    lax.dot_general(ds, qb, _TN, preferred_element_type=F32))

        def below(qi, carry):
            out = []
            for a in range(HEADS_PER_STEP):
                dv_t, dk_t = tile(a, qi, _key_ok(kj, t))
                out.append((carry[a][0] + dv_t, carry[a][1] + dk_t))
            return tuple(out)

        first = tuple(tile(a, kj, _diag_mask(kj, t)) for a in range(HEADS_PER_STEP))
        res = lax.fori_loop(kj + 1, nb, below, first)
        for a in range(HEADS_PER_STEP):
            dv_ref[:, heads[a]] = res[a][0]
            dk_ref[:, heads[a]] = res[a][1] * _SCALE

    whole = pl.BlockSpec((lp, wide), lambda g, j: (0, g))
    kspec = pl.BlockSpec((t, wide), lambda g, j: (j, g))
    stat = pl.BlockSpec((HEADS_PER_STEP, lp, 1), lambda g, j: (g, 0, 0))
    return pl.pallas_call(
        body, name=name,
        out_shape=[jax.ShapeDtypeStruct((lp, MLA_WIDE), F32)] * 3,
        grid=(MLA_HEADS // HEADS_PER_STEP, nb),
        in_specs=[whole, kspec, kspec, whole, stat, stat],
        out_specs=[whole, kspec, kspec],
        compiler_params=_cparams(dimension_semantics=("parallel", "arbitrary")),
    )(q, k, v, do, lse, delta)


def _rope_tables(lp):
    inv = 1.0 / (ROPE_THETA ** (jnp.arange(0, MLA_ROPE, 2, dtype=F32) / MLA_ROPE))
    pos = jnp.maximum(jnp.arange(lp, dtype=jnp.int32) - NPAD, 0).astype(F32)
    ang = pos[:, None] * inv[None, :]
    cos, sin = jnp.cos(ang), jnp.sin(ang)
    z32 = jnp.zeros((lp, HEAD_SLOT - MLA_QK), F32)
    cos_t = jnp.concatenate([jnp.ones((lp, MLA_NOPE), F32), cos, cos, z32], axis=1)
    sin_t = jnp.concatenate([jnp.zeros((lp, MLA_NOPE), F32), sin, sin, z32], axis=1)
    return cos_t, sin_t


def _loss_head(h, target, name):
    lp = h.shape[0]

    def body(h_ref, t_ref, d_ref, loss_ref):
        i = pl.program_id(0)

        @pl.when(i == 0)
        def _():
            d_ref[...] = jnp.zeros_like(d_ref)
            loss_ref[...] = jnp.zeros_like(loss_ref)

        @pl.when(i > 0)
        def _():
            err = h_ref[...] - t_ref[...]
            d_ref[...] = err * (1.0 / D_MODEL)
            loss_ref[...] += jnp.sum(err * err, axis=0, keepdims=True) * (0.5 / D_MODEL)

    return pl.pallas_call(
        body, name=name,
        out_shape=[jax.ShapeDtypeStruct((lp, D_MODEL), F32), jax.ShapeDtypeStruct((1, D_MODEL), F32)],
        grid=(lp // CHUNK,),
        in_specs=[pl.BlockSpec((CHUNK, D_MODEL), lambda i: (i, 0)),
                  pl.BlockSpec((CHUNK, D_MODEL), lambda i: (jnp.maximum(i - 1, 0), 0))],
        out_specs=[pl.BlockSpec((CHUNK, D_MODEL), lambda i: (i, 0)),
                   pl.BlockSpec((1, D_MODEL), lambda i: (0, 0))],
        compiler_params=_cparams(dimension_semantics=("arbitrary",)),
    )(h, target)


def _pad_cols(w, n):
    return jnp.pad(w, [(0, 0)] * (w.ndim - 1) + [(0, n - w.shape[-1])])


def _prep_weights(w):
    p = {}
    p['ssd_in'] = [_pad_cols(w['ssd_w_in'][j], SSD_IN_PAD).astype(BF16) for j in range(2)]
    p['ssd_out'] = [w['ssd_w_out'][j].astype(BF16) for j in range(2)]
    p['mla_in'], p['mla_qb'], p['mla_kvb'], p['mla_out'] = [], [], [], []
    for j in range(2):
        wi = w['mla_w_in'][j]
        kpe = jnp.pad(wi[:, MLA_Q_RANK + MLA_KV_RANK:], ((0, 0), (MLA_NOPE, HEAD_SLOT - MLA_QK)))
        p['mla_in'].append(jnp.concatenate(
            [wi[:, MLA_Q_RANK:MLA_Q_RANK + MLA_KV_RANK], kpe, wi[:, :MLA_Q_RANK]], axis=1).astype(BF16))
        qb = w['mla_w_q_b'][j].reshape(MLA_Q_RANK, MLA_HEADS, MLA_QK)
        p['mla_qb'].append(_pad_cols(qb, HEAD_SLOT).reshape(MLA_Q_RANK, MLA_WIDE).astype(BF16))
        kvb = w['mla_w_kv_b'][j].reshape(MLA_KV_RANK, MLA_HEADS, MLA_NOPE + MLA_V)
        kn = _pad_cols(kvb[:, :, :MLA_NOPE], HEAD_SLOT).reshape(MLA_KV_RANK, MLA_WIDE)
        vv = _pad_cols(kvb[:, :, MLA_NOPE:], HEAD_SLOT).reshape(MLA_KV_RANK, MLA_WIDE)
        p['mla_kvb'].append(jnp.concatenate([kn, vv], axis=1).astype(BF16))
        wo = w['mla_w_out'][j].reshape(MLA_HEADS, MLA_V, D_MODEL)
        p['mla_out'].append(jnp.pad(wo, ((0, 0), (0, HEAD_SLOT - MLA_V), (0, 0)))
                            .reshape(MLA_WIDE, D_MODEL).astype(BF16))
    p['up'] = [w['mlp_w_up'][i].astype(BF16) for i in range(4)]
    p['down'] = [w['mlp_w_down'][i].astype(BF16) for i in range(4)]
    return p


def _pad128(v):
    return _pad_cols(v.reshape(1, -1), 128)


def _sqrelu(u):
    r = jnp.maximum(u, 0.0)
    return r * r


def _local_step(x, target, w):
    seq = x.shape[0]
    lp = NPAD + N_META + seq
    p = _prep_weights(w)
    h = jnp.concatenate([jnp.zeros((NPAD, D_MODEL), F32), w['meta_tokens'], x], axis=0)
    cos_t, sin_t = _rope_tables(lp)
    rt = _pick(lp, (384, 256, 128))
    saved = []
    for i in range(4):
        j = i // 2
        s = {'h0': h}
        g_mix = w['ln_mix'][i].reshape(1, -1)
        hn = _rms_fwd(h, g_mix, f"rms_mix_f{i}")
        s['hn'] = hn
        if i % 2 == 0:
            zxd = _mm(hn, p['ssd_in'][j], 'nn', name=f"ssd_in_f{i}")
            consts = _ssd_consts(w['ssd_conv_w'][j], w['ssd_conv_b'][j].reshape(1, -1),
                                 _pad128(w['ssd_dt_bias'][j]), _pad128(w['ssd_a_log'][j]),
                                 _pad128(w['ssd_d'][j]), w['ssd_norm'][j].reshape(1, -1))
            yg, states = _ssd_fwd(zxd, consts, f"ssd_core_f{i}")
            s.update(zxd=zxd, consts=consts, yg=yg, states=states)
            h = _mm(yg, p['ssd_out'][j], 'nn', name=f"ssd_out_f{i}", epi=lambda r, hv: hv + r, extras=(h,))
        else:
            lat = _mm(hn, p['mla_in'][j], 'nn', name=f"mla_in_f{i}")
            kvg = w['mla_kv_a_norm'][j].reshape(1, -1)
            qag = w['mla_q_a_norm'][j].reshape(1, -1)
            kvn, qn = _row_call(lambda _, a, b, c, d: _lat_norm(a, b, c, d),
                                [(lat, MLA_KV_RANK, 0), (lat, MLA_Q_RANK, 1)], [kvg, qag],
                                [(MLA_KV_RANK, BF16), (MLA_Q_RANK, BF16)], n_rows=lp, tile=rt,
                                name=f"mla_latnorm_f{i}")
            q_raw = _mm(qn, p['mla_qb'][j], 'nn', name=f"mla_qb_f{i}")
            kv_raw = _mm(kvn, p['mla_kvb'][j], 'nn', name=f"mla_kvb_f{i}")
            qg = _pad_cols(w['mla_q_norm'][j].reshape(1, -1), HEAD_SLOT)
            kg = _pad_cols(w['mla_k_norm'][j].reshape(1, -1), HEAD_SLOT)

            def prep_fwd(_, qr, kn, kpe, vv, cs, sn, qgv, kgv):
                qq, kk = _qk_prep(qr, kn, kpe, cs, sn, qgv, kgv)
                return qq, kk, vv

            q, k, v = _row_call(prep_fwd,
                                [(q_raw, MLA_WIDE, 0), (kv_raw, MLA_WIDE, 0), (lat, HEAD_SLOT, 2),
                                 (kv_raw, MLA_WIDE, 1), (cos_t, HEAD_SLOT, 0), (sin_t, HEAD_SLOT, 0)],
                                [qg, kg], [(MLA_WIDE, BF16)] * 3, n_rows=lp, tile=rt,
                                name=f"mla_qkprep_f{i}")
            o, lse = _attn_fwd(q, k, v, f"mla_attn_f{i}")
            s.update(lat=lat, kvg=kvg, qag=qag, kvn=kvn, qn=qn, q_raw=q_raw, kv_raw=kv_raw, qg=qg, kg=kg,
                     q=q, k=k, v=v, o=o, lse=lse)
            h = _mm(o, p['mla_out'][j], 'nn', name=f"mla_out_f{i}", epi=lambda r, hv: hv + r, extras=(h,))
        s['h1'] = h
        g_mlp = w['ln_mlp'][i].reshape(1, -1)
        hn2 = _rms_fwd(h, g_mlp, f"rms_mlp_f{i}")
        u = _mm(hn2, p['up'][i], 'nn', name=f"mlp_up_f{i}")
        h = _mm(u, p['down'][i], 'nn', name=f"mlp_down_f{i}", a_fn=_sqrelu,
                epi=lambda r, hv: hv + r, extras=(h,))
        s.update(hn2=hn2, u=u, g_mix=g_mix, g_mlp=g_mlp)
        saved.append(s)

    dh, loss_row = _loss_head(h, target, "loss_head")

    g = {k_: [None] * (4 if k_ in ('ln_mix', 'ln_mlp', 'mlp_w_up', 'mlp_w_down') else 2)
         for k_ in w if k_ != 'meta_tokens'}
    for i in reversed(range(4)):
        j = i // 2
        s = saved[i]
        g['mlp_w_down'][i] = _mm(s['u'], dh, 'tn', name=f"mlp_down_dw{i}", a_fn=_sqrelu)
        du = _mm(dh, p['down'][i], 'nt', name=f"mlp_down_dx{i}",
                 epi=lambda r, uv: r * (2.0 * jnp.maximum(uv, 0.0)), extras=(s['u'],))
        g['mlp_w_up'][i] = _mm(s['hn2'], du, 'tn', name=f"mlp_up_dw{i}")
        d_hn2 = _mm(du, p['up'][i], 'nt', name=f"mlp_up_dx{i}")
        dh, dg = _rms_bwd(s['h1'], s['g_mlp'], d_hn2, dh, f"rms_mlp_b{i}")
        g['ln_mlp'][i] = dg[0]
        if i % 2 == 0:
            g['ssd_w_out'][j] = _mm(s['yg'], dh, 'tn', name=f"ssd_out_dw{i}")
            d_yg = _mm(dh, p['ssd_out'][j], 'nt', name=f"ssd_out_dx{i}")
            d_zxd, dcw, dcb, ddtb, dalog, ddsk, dng = _ssd_bwd(s['zxd'], s['states'], d_yg, s['consts'],
                                                              f"ssd_core_b{i}")
            g['ssd_conv_w'][j], g['ssd_conv_b'][j], g['ssd_norm'][j] = dcw, dcb[0], dng[0]
            g['ssd_dt_bias'][j], g['ssd_a_log'][j], g['ssd_d'][j] = (
                ddtb[0, :SSD_HEADS], dalog[0, :SSD_HEADS], ddsk[0, :SSD_HEADS])
            g['ssd_w_in'][j] = _mm(s['hn'], d_zxd, 'tn', name=f"ssd_in_dw{i}")
            d_hn = _mm(d_zxd, p['ssd_in'][j], 'nt', name=f"ssd_in_dx{i}")
        else:
            wo = _mm(s['o'], dh, 'tn', name=f"mla_out_dw{i}")
            g['mla_w_out'][j] = wo.reshape(MLA_HEADS, HEAD_SLOT, D_MODEL)[:, :MLA_V].reshape(-1, D_MODEL)
            do = _mm(dh, p['mla_out'][j], 'nt', name=f"mla_out_dx{i}")
            dob, delta = _attn_delta(do, s['o'], f"mla_attn_delta{i}")
            dq, dk, dv = _attn_bwd(s['q'], s['k'], s['v'], dob, s['lse'], delta, f"mla_attn_b{i}")

            def prep_bwd(_, qr, kn, kpe, cs, sn, dqv, dkv, dvv, qgv, kgv):
                _, vjp = jax.vjp(lambda a, b, c, d, e: _qk_prep(a, b, c, cs, sn, d, e), qr, kn, kpe, qgv, kgv)
                d_qr, d_kn, d_kpe, d_qg, d_kg = vjp((dqv, dkv))
                return d_qr, jnp.concatenate([d_kn, dvv], axis=1), d_kpe, d_qg, d_kg

            d_qraw, d_kvraw, d_kpe, d_qg, d_kg = _row_call(
                prep_bwd,
                [(s['q_raw'], MLA_WIDE, 0), (s['kv_raw'], MLA_WIDE, 0), (s['lat'], HEAD_SLOT, 2),
                 (cos_t, HEAD_SLOT, 0), (sin_t, HEAD_SLOT, 0), (dq, MLA_WIDE, 0), (dk, MLA_WIDE, 0),
                 (dv, MLA_WIDE, 0)],
                [s['qg'], s['kg']], [(MLA_WIDE, F32), (2 * MLA_WIDE, F32), (HEAD_SLOT, F32)],
                [(1, HEAD_SLOT), (1, HEAD_SLOT)], n_rows=lp, tile=_pick(lp, (128,)), name=f"mla_qkprep_b{i}")
            g['mla_q_norm'][j], g['mla_k_norm'][j] = d_qg[0, :MLA_QK], d_kg[0, :MLA_QK]
            wqb = _mm(s['qn'], d_qraw, 'tn', name=f"mla_qb_dw{i}")
            g['mla_w_q_b'][j] = wqb.reshape(MLA_Q_RANK, MLA_HEADS, HEAD_SLOT)[:, :, :MLA_QK].reshape(MLA_Q_RANK, -1)
            d_qn = _mm(d_qraw, p['mla_qb'][j], 'nt', name=f"mla_qb_dx{i}")
            wkvb = _mm(s['kvn'], d_kvraw, 'tn', name=f"mla_kvb_dw{i}").reshape(MLA_KV_RANK, 2, MLA_HEADS, HEAD_SLOT)
            g['mla_w_kv_b'][j] = jnp.concatenate([wkvb[:, 0, :, :MLA_NOPE], wkvb[:, 1, :, :MLA_V]],
                                                 axis=-1).reshape(MLA_KV_RANK, -1)
            d_kvn = _mm(d_kvraw, p['mla_kvb'][j], 'nt', name=f"mla_kvb_dx{i}")

            def lat_bwd(_, kvl, ql, dkvn, dqn, dkpe, kvgv, qagv):
                _, vjp = jax.vjp(_lat_norm, kvl, ql, kvgv, qagv)
                d_kvl, d_ql, d_kvg, d_qag = vjp((dkvn, dqn))
                return jnp.concatenate([d_kvl, dkpe, d_ql], axis=1), d_kvg, d_qag

            d_lat, d_kvg, d_qag = _row_call(
                lat_bwd, [(s['lat'], MLA_KV_RANK, 0), (s['lat'], MLA_Q_RANK, 1), (d_kvn, MLA_KV_RANK, 0),
                          (d_qn, MLA_Q_RANK, 0), (d_kpe, HEAD_SLOT, 0)],
                [s['kvg'], s['qag']], [(LAT_PAD, F32)], [(1, MLA_KV_RANK), (1, MLA_Q_RANK)],
                n_rows=lp, tile=rt, name=f"mla_latnorm_b{i}")
            g['mla_kv_a_norm'][j], g['mla_q_a_norm'][j] = d_kvg[0], d_qag[0]
            win = _mm(s['hn'], d_lat, 'tn', name=f"mla_in_dw{i}")
            g['mla_w_in'][j] = jnp.concatenate(
                [win[:, MLA_KV_RANK + HEAD_SLOT:], win[:, :MLA_KV_RANK],
                 win[:, MLA_KV_RANK + MLA_NOPE:MLA_KV_RANK + MLA_QK]], axis=1)
            d_hn = _mm(d_lat, p['mla_in'][j], 'nt', name=f"mla_in_dx{i}")
        dh, dg = _rms_bwd(s['h0'], s['g_mix'], d_hn, dh, f"rms_mix_b{i}")
        g['ln_mix'][i] = dg[0]

    grads = {k_: jnp.stack(v_) for k_, v_ in g.items()}
    grads['meta_tokens'] = dh[NPAD:NPAD + N_META]
    return loss_row, dh[NPAD + N_META:], grads


def _all_gather8(shard, name):
    m_per, n = shard.shape

    def body(x_ref, out_ref, send_sems, recv_sems, local_sem):
        x, y, c = lax.axis_index("x"), lax.axis_index("y"), lax.axis_index("c")
        me, sibling = (x, y, c), (x, y, 1 - c)
        chips = [(1 - x, y), (x, 1 - y), (1 - x, 1 - y)]

        def rows(px, py, pc):
            return out_ref.at[pl.ds((4 * px + 2 * py + pc) * m_per, m_per), :]

        def copy(k, block, to, src=None):
            return pltpu.make_async_remote_copy(
                src_ref=rows(*block) if src is None else src, dst_ref=rows(*block),
                send_sem=send_sems.at[k], recv_sem=recv_sems.at[k], device_id=to, device_id_type=MESH)

        mine = pltpu.make_async_copy(x_ref, rows(*me), local_sem)
        mine.start()
        first = [copy(0, me, sibling, src=x_ref)]
        first += [copy(1 + j, me, (*chip, c), src=x_ref) for j, chip in enumerate(chips)]
        for cp in first:
            cp.start()
        passed = [copy(4 + j, (*chip, c), sibling) for j, chip in enumerate(chips)]
        for j, chip in enumerate(chips):
            copy(1 + j, (*chip, c), me).wait_recv()
            passed[j].start()
        copy(0, sibling, me).wait_recv()
        for j, chip in enumerate(chips):
            copy(4 + j, (*chip, 1 - c), me).wait_recv()
        for cp in first + passed:
            cp.wait_send()
        mine.wait()

    return pl.pallas_call(
        body, name=name,
        out_shape=jax.ShapeDtypeStruct((8 * m_per, n), shard.dtype),
        in_specs=[pl.BlockSpec(memory_space=pl.ANY)],
        out_specs=pl.BlockSpec(memory_space=pl.ANY),
        scratch_shapes=[pltpu.SemaphoreType.DMA((7,)), pltpu.SemaphoreType.DMA((7,)), pltpu.SemaphoreType.DMA],
    )(shard)


def _mesh_pos():
    return lax.axis_index("x"), lax.axis_index("y"), lax.axis_index("c")


def _half_rows(pc, h):
    return pl.ds(pl.multiple_of(pc * h, 16), h)


def _whole_view(ref, kind, shard_shape, k, pc):
    _, r, c = shard_shape
    rows = _half_rows(pc, r // 2)
    if kind == 'row':
        return ref.at[:, k, rows, :]
    if kind == 'col':
        return ref.at[:, rows, pl.ds(pl.multiple_of(k * c, 128), c)]
    return ref.at[k, :, rows, :]


def _whole_shape(kind, shard_shape, rows=None):
    l, r, c = shard_shape
    r = r if rows is None else rows
    return {'row': (l, 4, r, c), 'col': (l, r, 4 * c), 'colx': (4, l, r, c)}[kind]


def _gather_big(shards, kinds, name):
    n = len(shards)
    shapes = [s.shape for s in shards]

    def body(*refs):
        ins, outs = refs[:n], refs[n:2 * n]
        send_sems, recv_sems, local_sems = refs[2 * n:]
        x, y, c = _mesh_pos()
        me, sibling = (x, y, c), (x, y, 1 - c)
        chips = [(1 - x, y), (x, 1 - y), (1 - x, 1 - y)]

        def place(a, px, py, pc):
            return _whole_view(outs[a], kinds[a], shapes[a], 2 * px + py, pc)

        def own(a):
            return ins[a].at[:, _half_rows(c, shapes[a][1] // 2), :]

        def copy(a, k, block, to, src=None):
            return pltpu.make_async_remote_copy(
                src_ref=place(a, *block) if src is None else src, dst_ref=place(a, *block),
                send_sem=send_sems.at[7 * a + k], recv_sem=recv_sems.at[7 * a + k],
                device_id=to, device_id_type=MESH)

        mine = [pltpu.make_async_copy(own(a), place(a, *me), local_sems.at[a]) for a in range(n)]
        first = [copy(a, 1 + j, me, (*chip, c), src=own(a)) for j, chip in enumerate(chips) for a in range(n)]
        first += [copy(a, 0, me, sibling, src=own(a)) for a in range(n)]
        for cp in first + mine:
            cp.start()
        passed = []
        for j, chip in enumerate(chips):
            for a in range(n):
                copy(a, 1 + j, (*chip, c), me).wait_recv()
                passed.append(copy(a, 4 + j, (*chip, c), sibling))
                passed[-1].start()
        for a in range(n):
            copy(a, 0, sibling, me).wait_recv()
        for j, chip in enumerate(chips):
            for a in range(n):
                copy(a, 4 + j, (*chip, 1 - c), me).wait_recv()
        for cp in first + passed:
            cp.wait_send()
        for cp in mine:
            cp.wait()

    return pl.pallas_call(
        body, name=name,
        out_shape=[jax.ShapeDtypeStruct(_whole_shape(k, s.shape), s.dtype) for k, s in zip(kinds, shards)],
        in_specs=[pl.BlockSpec(memory_space=pl.ANY)] * n,
        out_specs=[pl.BlockSpec(memory_space=pl.ANY)] * n,
        scratch_shapes=[pltpu.SemaphoreType.DMA((7 * n,)), pltpu.SemaphoreType.DMA((7 * n,)),
                        pltpu.SemaphoreType.DMA((n,))],
    )(*shards)


def _rs_swap(wholes, kinds, shapes, name):
    n = len(wholes)

    def body(*refs):
        ins, outs = refs[:n], refs[n:2 * n]
        send_sems, recv_sems = refs[2 * n:]
        x, y, c = _mesh_pos()
        cps = []
        for a in range(n):
            rows = _half_rows(1 - c, shapes[a][1] // 2)
            src = ins[a].at[:, rows, :] if kinds[a] == 'col' else ins[a].at[:, :, rows, :]
            cps.append(pltpu.make_async_remote_copy(
                src_ref=src, dst_ref=outs[a], send_sem=send_sems.at[a], recv_sem=recv_sems.at[a],
                device_id=(x, y, 1 - c), device_id_type=MESH))
        for cp in cps:
            cp.start()
        for cp in cps:
            cp.wait()

    return pl.pallas_call(
        body, name=name,
        out_shape=[jax.ShapeDtypeStruct(_whole_shape(k, s, s[1] // 2), w.dtype)
                   for k, s, w in zip(kinds, shapes, wholes)],
        in_specs=[pl.BlockSpec(memory_space=pl.ANY)] * n,
        out_specs=[pl.BlockSpec(memory_space=pl.ANY)] * n,
        scratch_shapes=[pltpu.SemaphoreType.DMA((n,)), pltpu.SemaphoreType.DMA((n,))],
    )(*wholes)


def _rs_exchange(parts, kinds, shapes, name):
    n = len(parts)

    def body(*refs):
        ins, outs = refs[:n], refs[n:2 * n]
        send_sems, recv_sems, local_sems = refs[2 * n:]
        x, y, c = _mesh_pos()
        kme = 2 * x + y
        chips = [(1 - x, y), (x, 1 - y), (1 - x, 1 - y)]

        def slab(a, k):
            if kinds[a] == 'row':
                return ins[a].at[:, k]
            if kinds[a] == 'col':
                cw = shapes[a][2]
                return ins[a].at[:, :, pl.ds(pl.multiple_of(k * cw, 128), cw)]
            return ins[a].at[k]

        cps = [pltpu.make_async_remote_copy(
            src_ref=slab(a, 2 * px + py), dst_ref=outs[a].at[kme], send_sem=send_sems.at[3 * a + j],
            recv_sem=recv_sems.at[3 * a + j], device_id=(px, py, c), device_id_type=MESH)
            for j, (px, py) in enumerate(chips) for a in range(n)]
        cps_local = [pltpu.make_async_copy(slab(a, kme), outs[a].at[kme], local_sems.at[a]) for a in range(n)]
        for cp in cps + cps_local:
            cp.start()
        for cp in cps + cps_local:
            cp.wait()

    return pl.pallas_call(
        body, name=name,
        out_shape=[jax.ShapeDtypeStruct((4, s[0], s[1] // 2, s[2]), p.dtype) for s, p in zip(shapes, parts)],
        in_specs=[pl.BlockSpec(memory_space=pl.ANY)] * n,
        out_specs=[pl.BlockSpec(memory_space=pl.ANY)] * n,
        scratch_shapes=[pltpu.SemaphoreType.DMA((3 * n,)), pltpu.SemaphoreType.DMA((3 * n,)),
                        pltpu.SemaphoreType.DMA((n,))],
    )(*parts)


def _rs_share(shards, name):
    n = len(shards)

    def body(*refs):
        outs = refs[n:2 * n]
        send_sems, recv_sems = refs[2 * n:]
        x, y, c = _mesh_pos()
        cps = []
        for a in range(n):
            rows = outs[a].at[:, _half_rows(c, shards[a].shape[1] // 2), :]
            cps.append(pltpu.make_async_remote_copy(
                src_ref=rows, dst_ref=rows, send_sem=send_sems.at[a], recv_sem=recv_sems.at[a],
                device_id=(x, y, 1 - c), device_id_type=MESH))
        for cp in cps:
            cp.start()
        for cp in cps:
            cp.wait()

    return pl.pallas_call(
        body, name=name,
        out_shape=[jax.ShapeDtypeStruct(s.shape, s.dtype) for s in shards],
        in_specs=[pl.BlockSpec(memory_space=pl.ANY)] * n,
        out_specs=[pl.BlockSpec(memory_space=pl.ANY)] * n,
        input_output_aliases={a: a for a in range(n)},
        scratch_shapes=[pltpu.SemaphoreType.DMA((n,)), pltpu.SemaphoreType.DMA((n,))],
    )(*shards)


def _tile_rows(rows, cols, budget=2 * 1024 * 1024):
    for t in (1024, 512, 256, 128, 64, 32, 16, 8):
        if rows % t == 0 and t * cols * 4 <= budget:
            return t
    return rows


def _add_half(g3, r3, c_idx, name):
    a, h, n = r3.shape
    t = _tile_rows(h, n)
    nt = h // t

    def body(c_ref, g_ref, r_ref, o_ref):
        o_ref[...] = (g_ref[...] + r_ref[...]).astype(o_ref.dtype)

    return pl.pallas_call(
        body, name=name, out_shape=jax.ShapeDtypeStruct((a, h, n), BF16),
        grid_spec=pltpu.PrefetchScalarGridSpec(
            num_scalar_prefetch=1, grid=(a, nt),
            in_specs=[pl.BlockSpec((1, t, n), lambda k, i, c: (k, c[0] * nt + i, 0)),
                      pl.BlockSpec((1, t, n), lambda k, i, c: (k, i, 0))],
            out_specs=pl.BlockSpec((1, t, n), lambda k, i, c: (k, i, 0))),
        compiler_params=_cparams(dimension_semantics=("parallel", "parallel")),
    )(c_idx, g3, r3)


def _sum4(parts, c_idx, name):
    _, l, h, n = parts.shape
    t = _tile_rows(h, n, 1024 * 1024)
    nt = h // t

    def body(c_ref, p_ref, o_ref):
        pv = p_ref[...].astype(F32)
        o_ref[...] = ((pv[0] + pv[1]) + pv[2]) + pv[3]

    return pl.pallas_call(
        body, name=name, out_shape=jax.ShapeDtypeStruct((l, 2 * h, n), F32),
        grid_spec=pltpu.PrefetchScalarGridSpec(
            num_scalar_prefetch=1, grid=(l, nt),
            in_specs=[pl.BlockSpec((4, 1, t, n), lambda k, i, c: (0, k, i, 0))],
            out_specs=pl.BlockSpec((1, t, n), lambda k, i, c: (k, c[0] * nt + i, 0))),
        compiler_params=_cparams(dimension_semantics=("parallel", "parallel")),
    )(c_idx, parts)


def _sum8(parts, name):
    _, m, n = parts.shape

    def body(p_ref, o_ref):
        acc = p_ref[0]
        for d in range(1, 8):
            acc = acc + p_ref[d]
        o_ref[...] = acc

    return pl.pallas_call(body, name=name, out_shape=jax.ShapeDtypeStruct((m, n), F32))(parts)


def _adamw(wp, gp, mp, vp, name):
    r, n = wp.shape
    t = _tile_rows(r, n, 1024 * 1024)

    def body(w_ref, g_ref, m_ref, v_ref, d_ref, mo_ref, vo_ref):
        gv = g_ref[...]
        m2 = ADAM_B1 * m_ref[...] + (1.0 - ADAM_B1) * gv
        v2 = ADAM_B2 * v_ref[...] + (1.0 - ADAM_B2) * (gv * gv)
        m_hat = m2 / (1.0 - ADAM_B1 ** ADAM_STEP)
        v_hat = v2 / (1.0 - ADAM_B2 ** ADAM_STEP)
        d_ref[...] = -ADAM_LR * (m_hat / (jnp.sqrt(v_hat) + ADAM_EPS) + ADAM_WD * w_ref[...])
        mo_ref[...] = m2
        vo_ref[...] = v2

    spec = pl.BlockSpec((t, n), lambda i: (i, 0))
    return pl.pallas_call(
        body, name=name, out_shape=[jax.ShapeDtypeStruct((r, n), F32)] * 3, grid=(r // t,),
        in_specs=[spec] * 4, out_specs=[spec] * 3,
        compiler_params=_cparams(dimension_semantics=("parallel",)),
    )(wp, gp, mp, vp)


BIG = (('ssd_w_in', 'colx'), ('ssd_w_out', 'row'), ('mla_w_in', 'row'), ('mla_w_q_b', 'col'),
       ('mla_w_kv_b', 'col'), ('mla_w_out', 'row'), ('mlp_w_up', 'col'), ('mlp_w_down', 'row'))
SMALL_SHARDED = (('meta_tokens', 1), ('ssd_conv_w', 2), ('mla_q_a_norm', 1), ('mla_kv_a_norm', 1))
SMALL_REPL = ('ln_mix', 'ln_mlp', 'ssd_conv_b', 'ssd_dt_bias', 'ssd_a_log', 'ssd_d', 'ssd_norm',
              'mla_q_norm', 'mla_k_norm')
ALL_NAMES = ('meta_tokens', 'ln_mix', 'ln_mlp', 'ssd_w_in', 'ssd_conv_w', 'ssd_conv_b', 'ssd_dt_bias',
             'ssd_a_log', 'ssd_d', 'ssd_norm', 'ssd_w_out', 'mla_w_in', 'mla_q_a_norm', 'mla_w_q_b',
             'mla_kv_a_norm', 'mla_w_kv_b', 'mla_q_norm', 'mla_k_norm', 'mla_w_out', 'mlp_w_up', 'mlp_w_down')


def _pack(arrs, rows_mult):
    flat = jnp.concatenate([a.reshape(-1) for a in arrs])
    per = LANES * rows_mult
    pad = (-flat.shape[0]) % per
    if pad:
        flat = jnp.concatenate([flat, jnp.zeros((pad,), flat.dtype)])
    return flat.reshape(-1, LANES)


def _unpack(pack, shapes):
    flat = pack.reshape(-1)
    out, off = [], 0
    for shp in shapes:
        n = math.prod(shp)
        out.append(flat[off:off + n].reshape(shp))
        off += n
    return out


def _split4(full, axis):
    shp = full.shape
    r = full.reshape(shp[:axis] + (4, shp[axis] // 4) + shp[axis + 1:])
    return jnp.moveaxis(r, axis, 0)


def _join4(parts, axis):
    r = jnp.moveaxis(parts, 0, axis)
    shp = r.shape
    return r.reshape(shp[:axis] + (shp[axis] * shp[axis + 1],) + shp[axis + 2:])


def _gather_params(shards, table, dtype, c, name):
    pack = _pack([shards[n].astype(dtype) for n, _ in table], 16)
    half = pack.shape[0] // 2
    mine = lax.dynamic_slice_in_dim(pack, c * half, half, axis=0)
    full = _all_gather8(mine, name).reshape(4, -1)
    out, off = {}, 0
    for n, ax in table:
        cnt = math.prod(shards[n].shape)
        out[n] = _join4(full[:, off:off + cnt].reshape((4,) + shards[n].shape), ax)
        off += cnt
    return out


def kernel(x, meta_tokens, ln_mix, ln_mlp, ssd_w_in, ssd_conv_w, ssd_conv_b, ssd_dt_bias, ssd_a_log, ssd_d, ssd_norm, ssd_w_out, mla_w_in, mla_q_a_norm, mla_w_q_b, mla_kv_a_norm, mla_w_kv_b, mla_q_norm, mla_k_norm, mla_w_out, mlp_w_up, mlp_w_down, loss_target, m_meta_tokens, m_ln_mix, m_ln_mlp, m_ssd_w_in, m_ssd_conv_w, m_ssd_conv_b, m_ssd_dt_bias, m_ssd_a_log, m_ssd_d, m_ssd_norm, m_ssd_w_out, m_mla_w_in, m_mla_q_a_norm, m_mla_w_q_b, m_mla_kv_a_norm, m_mla_w_kv_b, m_mla_q_norm, m_mla_k_norm, m_mla_w_out, m_mlp_w_up, m_mlp_w_down, v_meta_tokens, v_ln_mix, v_ln_mlp, v_ssd_w_in, v_ssd_conv_w, v_ssd_conv_b, v_ssd_dt_bias, v_ssd_a_log, v_ssd_d, v_ssd_norm, v_ssd_w_out, v_mla_w_in, v_mla_q_a_norm, v_mla_w_q_b, v_mla_kv_a_norm, v_mla_w_kv_b, v_mla_q_norm, v_mla_k_norm, v_mla_w_out, v_mlp_w_up, v_mlp_w_down):
    w_sh = dict(meta_tokens=meta_tokens, ln_mix=ln_mix, ln_mlp=ln_mlp, ssd_w_in=ssd_w_in, ssd_conv_w=ssd_conv_w, ssd_conv_b=ssd_conv_b, ssd_dt_bias=ssd_dt_bias, ssd_a_log=ssd_a_log, ssd_d=ssd_d, ssd_norm=ssd_norm, ssd_w_out=ssd_w_out, mla_w_in=mla_w_in, mla_q_a_norm=mla_q_a_norm, mla_w_q_b=mla_w_q_b, mla_kv_a_norm=mla_kv_a_norm, mla_w_kv_b=mla_w_kv_b, mla_q_norm=mla_q_norm, mla_k_norm=mla_k_norm, mla_w_out=mla_w_out, mlp_w_up=mlp_w_up, mlp_w_down=mlp_w_down)
    m_sh = dict(meta_tokens=m_meta_tokens, ln_mix=m_ln_mix, ln_mlp=m_ln_mlp, ssd_w_in=m_ssd_w_in, ssd_conv_w=m_ssd_conv_w, ssd_conv_b=m_ssd_conv_b, ssd_dt_bias=m_ssd_dt_bias, ssd_a_log=m_ssd_a_log, ssd_d=m_ssd_d, ssd_norm=m_ssd_norm, ssd_w_out=m_ssd_w_out, mla_w_in=m_mla_w_in, mla_q_a_norm=m_mla_q_a_norm, mla_w_q_b=m_mla_w_q_b, mla_kv_a_norm=m_mla_kv_a_norm, mla_w_kv_b=m_mla_w_kv_b, mla_q_norm=m_mla_q_norm, mla_k_norm=m_mla_k_norm, mla_w_out=m_mla_w_out, mlp_w_up=m_mlp_w_up, mlp_w_down=m_mlp_w_down)
    v_sh = dict(meta_tokens=v_meta_tokens, ln_mix=v_ln_mix, ln_mlp=v_ln_mlp, ssd_w_in=v_ssd_w_in, ssd_conv_w=v_ssd_conv_w, ssd_conv_b=v_ssd_conv_b, ssd_dt_bias=v_ssd_dt_bias, ssd_a_log=v_ssd_a_log, ssd_d=v_ssd_d, ssd_norm=v_ssd_norm, ssd_w_out=v_ssd_w_out, mla_w_in=v_mla_w_in, mla_q_a_norm=v_mla_q_a_norm, mla_w_q_b=v_mla_w_q_b, mla_kv_a_norm=v_mla_kv_a_norm, mla_w_kv_b=v_mla_w_kv_b, mla_q_norm=v_mla_q_norm, mla_k_norm=v_mla_k_norm, mla_w_out=v_mla_w_out, mlp_w_up=v_mlp_w_up, mlp_w_down=v_mlp_w_down)

    cx, cy, cc = lax.axis_index("x"), lax.axis_index("y"), lax.axis_index("c")
    chip = 2 * cx + cy

    c_idx = cc.reshape(1).astype(jnp.int32)
    big_names = [n for n, _ in BIG]
    kinds = [k for _, k in BIG]
    shapes = [w_sh[n].shape for n in big_names]

    w = {n: w_sh[n] for n in SMALL_REPL}
    wholes = _gather_big([w_sh[n].astype(BF16) for n in big_names], kinds, "gather_big")
    for n, kind, s, f in zip(big_names, kinds, shapes, wholes):
        if kind == 'row':
            w[n] = f.reshape(s[0], 4 * s[1], s[2])
        elif kind == 'col':
            w[n] = f
        else:
            w[n] = jnp.concatenate([f[k] for k in range(4)], axis=-1)
    w.update(_gather_params(w_sh, SMALL_SHARDED, F32, cc, "gather_small"))

    loss_row, grad_x, grads = _local_step(x[0], loss_target[0], w)
    loss = lax.psum(jnp.sum(loss_row), ("x", "y", "c"))

    g_whole = []
    for n, kind, s in zip(big_names, kinds, shapes):
        if kind == 'row':
            g_whole.append(grads[n].reshape(s[0], 4, s[1], s[2]))
        elif kind == 'col':
            g_whole.append(grads[n])
        else:
            g_whole.append(jnp.stack([grads[n][..., k * s[2]:(k + 1) * s[2]] for k in range(4)]))
    recv = _rs_swap(g_whole, kinds, shapes, "rs_swap")
    parts = []
    for n, kind, s, g, r in zip(big_names, kinds, shapes, g_whole, recv):
        if kind == 'col':
            g3, r3 = g, r
        else:
            g3, r3 = g.reshape(-1, s[1], s[2]), r.reshape(-1, s[1] // 2, s[2])
        parts.append(_add_half(g3, r3, c_idx, f"rs_add_{n}").reshape(r.shape))
    got = _rs_exchange(parts, kinds, shapes, "rs_exchange")
    reduced = [_sum4(p, c_idx, f"rs_sum_{n}") for n, p in zip(big_names, got)]
    g_sh = dict(zip(big_names, _rs_share(reduced, "rs_share")))

    small_names = tuple(n for n, _ in SMALL_SHARDED) + SMALL_REPL
    sp = _pack([grads[n] for n in small_names], 8)
    srows = sp.shape[0]
    s_all = _sum8(_all_gather8(sp, "ar_small_gather").reshape(8, srows, LANES), "ar_small_sum")
    s_full = dict(zip(small_names, _unpack(s_all, [grads[n].shape for n in small_names])))
    for n, ax in SMALL_SHARDED:
        g_sh[n] = lax.dynamic_index_in_dim(_split4(s_full[n], ax), chip, axis=0, keepdims=False)
    for n in SMALL_REPL:
        g_sh[n] = s_full[n]

    delta, new_m, new_v = {}, {}, {}
    for n, s in zip(big_names, shapes):
        res = _adamw(*[t[n].reshape(-1, s[2]) for t in (w_sh, g_sh, m_sh, v_sh)], f"adamw_{n}")
        delta[n], new_m[n], new_v[n] = [r.reshape(s) for r in res]
    d_s, m_s, v_s = _adamw(*[_pack([t[n] for n in small_names], 8) for t in (w_sh, g_sh, m_sh, v_sh)],
                           "adamw_small")
    for dst, ps in ((delta, d_s), (new_m, m_s), (new_v, v_s)):
        dst.update(zip(small_names, _unpack(ps, [w_sh[n].shape for n in small_names])))

    return (loss, grad_x[None], *[g_sh[n] for n in ALL_NAMES], *[delta[n] for n in ALL_NAMES],
            *[new_m[n] for n in ALL_NAMES], *[new_v[n] for n in ALL_NAMES])
```

```python
import functools
import math

import jax
import jax.numpy as jnp
from jax import lax
from jax.experimental import pallas as pl
from jax.experimental.pallas import tpu as pltpu

F32 = jnp.float32
BF16 = jnp.bfloat16
MESH = pl.DeviceIdType.MESH
_NN = (((1,), (0,)), ((), ()))
_NT = (((1,), (1,)), ((), ()))
_TN = (((0,), (0,)), ((), ()))

D_MODEL = 1024
N_META = 16
EPS = 1e-6
SSD_D_INNER = 2048
SSD_HEADS = 32
SSD_HEAD_DIM = 64
SSD_GROUPS = 8
SSD_HPG = 4
SSD_STATE = 128
SSD_CONV = 4
CHUNK = 128
SSD_IN_DIM = 6176
SSD_IN_PAD = 6272
MLA_HEADS = 16
MLA_NOPE = 64
MLA_ROPE = 32
MLA_V = 64
MLA_QK = 96
MLA_Q_RANK = 384
MLA_KV_RANK = 256
HEAD_SLOT = 128
MLA_WIDE = MLA_HEADS * HEAD_SLOT
HEADS_PER_STEP = 2
LAT_PAD = 768
ROPE_THETA = 10000.0
D_FF = 4096
NPAD = CHUNK - N_META
ADAM_LR, ADAM_B1, ADAM_B2, ADAM_EPS, ADAM_WD, ADAM_STEP = 0.001, 0.9, 0.999, 1e-08, 0.01, 10
LANES = 1024
VMEM_LIMIT = 56 * 1024 * 1024


def _pick(n, cands):
    for c in cands:
        if n % c == 0:
            return c
    return n


def _cparams(**kw):
    return pltpu.CompilerParams(vmem_limit_bytes=VMEM_LIMIT, **kw)


def _mm(a, b, dims, *, name, out_dtype=F32, a_fn=None, epi=None, extras=()):
    if dims == 'nn':
        (M, K), (K2, N) = a.shape, b.shape
    elif dims == 'nt':
        (M, K), (N, K2) = a.shape, b.shape
    else:
        (K, M), (K2, N) = a.shape, b.shape
    assert K == K2, (a.shape, b.shape, dims)
    if dims == 'tn':
        tm = _pick(M, (1024, 768, 512, 384, 256, 128))
        tn = _pick(N, (1024, 896, 768, 512, 384, 256, 128))
        tk = _pick(K, (1408, 1024, 512, 384, 256, 128))
    else:
        tm = _pick(M, (1408, 1024, 512, 384, 256, 128))
        tn = _pick(N, (512, 896, 768, 384, 256, 128))
        tk = _pick(K, (1024, 896, 768, 512, 384, 256, 128))
    nk = K // tk
    if dims == 'nn':
        a_spec = pl.BlockSpec((tm, tk), lambda i, j, k: (i, k))
        b_spec = pl.BlockSpec((tk, tn), lambda i, j, k: (k, j))
        dn = (((1,), (0,)), ((), ()))
    elif dims == 'nt':
        a_spec = pl.BlockSpec((tm, tk), lambda i, j, k: (i, k))
        b_spec = pl.BlockSpec((tn, tk), lambda i, j, k: (j, k))
        dn = (((1,), (1,)), ((), ()))
    else:
        a_spec = pl.BlockSpec((tk, tm), lambda i, j, k: (k, i))
        b_spec = pl.BlockSpec((tk, tn), lambda i, j, k: (k, j))
        dn = (((0,), (0,)), ((), ()))
    o_spec = pl.BlockSpec((tm, tn), lambda i, j, k: (i, j))
    n_ex = len(extras)

    def body(a_ref, b_ref, *rest):
        ex_refs, o_ref, acc = rest[:n_ex], rest[n_ex], rest[n_ex + 1]
        k = pl.program_id(2)

        @pl.when(k == 0)
        def _():
            acc[...] = jnp.zeros_like(acc)

        av = a_ref[...]
        if a_fn is not None:
            av = a_fn(av)
        acc[...] += lax.dot_general(av.astype(BF16), b_ref[...].astype(BF16), dn,
                                    preferred_element_type=F32)

        @pl.when(k == nk - 1)
        def _():
            r = acc[...]
            if epi is not None:
                r = epi(r, *[e[...] for e in ex_refs])
            o_ref[...] = r.astype(out_dtype)

    return pl.pallas_call(
        body, name=name,
        out_shape=jax.ShapeDtypeStruct((M, N), out_dtype),
        grid=(M // tm, N // tn, nk),
        in_specs=[a_spec, b_spec] + [o_spec] * n_ex,
        out_specs=o_spec,
        scratch_shapes=[pltpu.VMEM((tm, tn), F32)],
        compiler_params=_cparams(dimension_semantics=("parallel", "parallel", "arbitrary")),
    )(a, b, *extras)


def _row_call(fn, rows, consts, out_rows, out_accs=(), *, n_rows, tile, name):
    n_r, n_c, n_o, n_a = len(rows), len(consts), len(out_rows), len(out_accs)
    steps = n_rows // tile

    def body(*refs):
        r_refs = refs[:n_r]
        c_refs = refs[n_r:n_r + n_c]
        o_refs = refs[n_r + n_c:n_r + n_c + n_o]
        a_refs = refs[n_r + n_c + n_o:]
        i = pl.program_id(0)
        res = fn(i, *[r[...] for r in r_refs], *[c[...] for c in c_refs])
        for o_ref, val in zip(o_refs, res[:n_o]):
            o_ref[...] = val.astype(o_ref.dtype)

        @pl.when(i == 0)
        def _():
            for a_ref in a_refs:
                a_ref[...] = jnp.zeros_like(a_ref)

        for a_ref, val in zip(a_refs, res[n_o:]):
            a_ref[...] += val

    in_specs = [pl.BlockSpec((tile, w), functools.partial(lambda i, cb: (i, cb), cb=cb))
                for (_, w, cb) in rows]
    in_specs += [pl.BlockSpec(c.shape, lambda i: (0, 0)) for c in consts]
    out_specs = [pl.BlockSpec((tile, c), lambda i: (i, 0)) for (c, _) in out_rows]
    out_specs += [pl.BlockSpec(s, lambda i: (0, 0)) for s in out_accs]
    out_shape = [jax.ShapeDtypeStruct((n_rows, c), dt) for (c, dt) in out_rows]
    out_shape += [jax.ShapeDtypeStruct(s, F32) for s in out_accs]
    return pl.pallas_call(
        body, name=name, out_shape=out_shape, grid=(steps,),
        in_specs=in_specs, out_specs=out_specs,
        compiler_params=_cparams(dimension_semantics=("arbitrary",)),
    )(*[r[0] for r in rows], *consts)


def _row_mask(i, tile):
    r = i * tile + lax.broadcasted_iota(jnp.int32, (tile, 1), 0)
    return (r >= NPAD).astype(F32)


def _rms(x, g):
    return x * lax.rsqrt(jnp.mean(x * x, axis=-1, keepdims=True) + EPS) * g


def _silu(x):
    return x * (0.5 * jnp.tanh(0.5 * x) + 0.5)


def _softplus(x):
    return jnp.maximum(x, 0.0) + jnp.log(1.0 + jnp.exp(-jnp.abs(x)))


def _rms_fwd(h, g, name):
    lp = h.shape[0]
    return _row_call(lambda i, hv, gv: (_rms(hv, gv),), [(h, D_MODEL, 0)], [g],
                     [(D_MODEL, BF16)], n_rows=lp, tile=_pick(lp, (384, 256, 128)), name=name)[0]


def _rms_bwd(h, g, d_hn, d_res, name):
    lp = h.shape[0]
    tile = _pick(lp, (384, 256, 128))

    def fn(i, hv, dv, rv, gv):
        _, vjp = jax.vjp(_rms, hv, gv)
        dh, dg = vjp(dv)
        return (rv + dh) * _row_mask(i, tile), dg

    return _row_call(fn, [(h, D_MODEL, 0), (d_hn, D_MODEL, 0), (d_res, D_MODEL, 0)], [g],
                     [(D_MODEL, F32)], [(1, D_MODEL)], n_rows=lp, tile=tile, name=name)


@functools.partial(jax.custom_vjp, nondiff_argnums=(1,))
def _roll_rows(x, s):
    return pltpu.roll(x, s, 0)


def _roll_rows_fwd(x, s):
    return pltpu.roll(x, s, 0), None


def _roll_rows_bwd(s, _, ct):
    return (pltpu.roll(ct, (ct.shape[0] - s) % ct.shape[0], 0),)


_roll_rows.defvjp(_roll_rows_fwd, _roll_rows_bwd)


def _conv_silu(cur, halo, w_rows, b):
    full = jnp.concatenate([halo, cur], axis=0)
    acc = cur * w_rows[SSD_CONV - 1] + b
    for k in range(SSD_CONV - 1):
        acc = acc + _roll_rows(full, SSD_CONV - 1 - k)[8:] * w_rows[k]
    return _silu(acc)


def _split3(v):
    hi = v.astype(BF16)
    r1 = v - hi.astype(F32)
    mid = r1.astype(BF16)
    lo = (r1 - mid.astype(F32)).astype(BF16)
    return hi, mid, lo


def _select_right(v, sel, dn):
    return sum(lax.dot_general(p, sel, dn, preferred_element_type=F32) for p in _split3(v))


@jax.custom_vjp
def _expand_heads(v, e_mat):
    return _select_right(v, e_mat, _NN)


def _expand_heads_fwd(v, e_mat):
    return _select_right(v, e_mat, _NN), e_mat


def _expand_heads_bwd(e_mat, ct):
    return _select_right(ct, e_mat, _NT), jnp.zeros_like(e_mat)


_expand_heads.defvjp(_expand_heads_fwd, _expand_heads_bwd)


@jax.custom_vjp
def _cumsum_rows(a, tri):
    return sum(lax.dot_general(tri, p, _NN, preferred_element_type=F32) for p in _split3(a))


def _cumsum_rows_fwd(a, tri):
    return _cumsum_rows(a, tri), tri


def _cumsum_rows_bwd(tri, ct):
    return (sum(lax.dot_general(tri, p, _TN, preferred_element_type=F32) for p in _split3(ct)),
            jnp.zeros_like(tri))


_cumsum_rows.defvjp(_cumsum_rows_fwd, _cumsum_rows_bwd)


def _ssd_chunk(mask, z, xs_pre, bc_pre, halo_x, halo_bc, dt_pre, st, cwx0, cwx1, cwx2, cwx3,
               cwb0, cwb1, cwb2, cwb3, cb_x, cb_bc, dtb, alog, dsk, ng):
    L = CHUNK
    lane_h = lax.broadcasted_iota(jnp.int32, (1, 128), 1)
    head_ok = (lane_h < SSD_HEADS).astype(F32)
    e_mat = (lax.broadcasted_iota(jnp.int32, (128, SSD_D_INNER), 1) // SSD_HEAD_DIM
             == lax.broadcasted_iota(jnp.int32, (128, SSD_D_INNER), 0)).astype(BF16)
    ri = lax.broadcasted_iota(jnp.int32, (L, L), 0)
    ci = lax.broadcasted_iota(jnp.int32, (L, L), 1)
    causal = ri >= ci

    xs = _conv_silu(xs_pre, halo_x, (cwx0, cwx1, cwx2, cwx3), cb_x) * mask
    bc = _conv_silu(bc_pre, halo_bc, (cwb0, cwb1, cwb2, cwb3), cb_bc) * mask
    dt = _softplus(dt_pre + dtb) * mask * head_ok
    a_dt = dt * (-jnp.exp(alog))
    a_cs = _cumsum_rows(a_dt, causal.astype(BF16))
    a_cs_t = a_cs.T
    row8 = lax.broadcasted_iota(jnp.int32, (8, 128), 0)
    last8 = jnp.where(row8 == 0, jnp.sum(a_dt, axis=0, keepdims=True), 0.0)
    dsk8 = jnp.where(row8 == 0, dsk, 0.0)
    wide = _expand_heads(jnp.concatenate([dt, a_cs, last8, dsk8], axis=0), e_mat)
    dt_e, acs_e = wide[0:L], wide[L:2 * L]
    last_e = jnp.sum(wide[2 * L:2 * L + 8], axis=0, keepdims=True)
    d_e = jnp.sum(wide[2 * L + 8:2 * L + 16], axis=0, keepdims=True)
    xdt = xs * dt_e
    dte_e = jnp.exp(last_e - acs_e)
    dfs_e = jnp.exp(acs_e)
    cd_e = jnp.exp(last_e)
    sub_h = lax.broadcasted_iota(jnp.int32, (128, L), 0)
    lane_hl = lax.broadcasted_iota(jnp.int32, (L, 128), 1)
    lane_g = lax.broadcasted_iota(jnp.int32, (1, SSD_HPG * SSD_HEAD_DIM), 1) // SSD_HEAD_DIM

    ys, new_st = [], []
    for g in range(SSD_GROUPS):
        b_g = bc[:, g * 128:(g + 1) * 128].astype(BF16)
        c_g = bc[:, 1024 + g * 128:1024 + (g + 1) * 128].astype(BF16)
        gs = slice(g * 256, (g + 1) * 256)
        xdt_g = xdt[:, gs]
        cb = lax.dot_general(c_g, b_g, (((1,), (1,)), ((), ())), preferred_element_type=F32)
        st_g = st[g * 128:(g + 1) * 128, :]
        y_g = lax.dot_general(c_g, st_g.astype(BF16), (((1,), (0,)), ((), ())),
                              preferred_element_type=F32) * dfs_e[:, gs]
        for j in range(SSD_HPG):
            h = g * SSD_HPG + j
            col = jnp.sum(jnp.where(lane_hl == h, a_cs, 0.0), axis=1, keepdims=True)
            row = jnp.sum(jnp.where(sub_h == h, a_cs_t, 0.0), axis=0, keepdims=True)
            dec = jnp.where(causal, jnp.exp(jnp.where(causal, col - row, 0.0)), 0.0)
            m_h = (cb * dec).astype(BF16)
            x_h = jnp.where(lane_g == j, xdt_g, 0.0).astype(BF16)
            y_g = y_g + lax.dot_general(m_h, x_h, (((1,), (0,)), ((), ())),
                                        preferred_element_type=F32)
        s_new = lax.dot_general(b_g, (xdt_g * dte_e[:, gs]).astype(BF16), (((0,), (0,)), ((), ())),
                                preferred_element_type=F32)
        new_st.append(st_g * cd_e[:, gs] + s_new)
        ys.append(y_g)
    y = jnp.concatenate(ys, axis=1) + xs * d_e
    gg = y * _silu(z)
    outs = []
    for g in range(SSD_GROUPS):
        sl = gg[:, g * 256:(g + 1) * 256]
        outs.append(sl * lax.rsqrt(jnp.mean(sl * sl, axis=-1, keepdims=True) + EPS))
    out = jnp.concatenate(outs, axis=1) * ng
    return out, jnp.concatenate(new_st, axis=0)


def _ssd_consts(conv_w, conv_b, dtb, alog, dsk, ng):
    return [conv_w, conv_b, dtb, alog, dsk, ng]


def _ssd_param_vals(cw_ref, cb_ref, dtb_ref, alog_ref, dsk_ref, ng_ref):
    cwx = [cw_ref[k:k + 1, 0:SSD_D_INNER] for k in range(SSD_CONV)]
    cwb = [cw_ref[k:k + 1, SSD_D_INNER:2 * SSD_D_INNER] for k in range(SSD_CONV)]
    return (*cwx, *cwb, cb_ref[:, 0:SSD_D_INNER], cb_ref[:, SSD_D_INNER:2 * SSD_D_INNER],
            dtb_ref[...], alog_ref[...], dsk_ref[...], ng_ref[...])


def _ssd_in_specs(rev, nc):
    def cidx(i):
        return (nc - 1 - i) if rev else i

    def halo(cb):
        return pl.BlockSpec((8, SSD_D_INNER), lambda i: (jnp.maximum(16 * cidx(i) - 1, 0), cb))

    return [
        pl.BlockSpec((CHUNK, SSD_D_INNER), lambda i: (cidx(i), 0)),
        pl.BlockSpec((CHUNK, SSD_D_INNER), lambda i: (cidx(i), 1)),
        pl.BlockSpec((CHUNK, SSD_D_INNER), lambda i: (cidx(i), 2)),
        halo(1), halo(2),
        pl.BlockSpec((CHUNK, 128), lambda i: (cidx(i), 48)),
    ]


def _ssd_fwd(zxd, consts, name):
    lp = zxd.shape[0]
    nc = lp // CHUNK

    def body(z_ref, xs_ref, bc_ref, hx_ref, hb_ref, dt_ref, cw_ref, cb_ref, dtb_ref, alog_ref,
             dsk_ref, ng_ref, y_ref, st_ref, state):
        c = pl.program_id(0)

        @pl.when(c == 0)
        def _():
            state[...] = jnp.zeros_like(state)

        live = (c > 0).astype(F32)
        st_ref[0] = state[...]
        out, st_new = _ssd_chunk(
            _row_mask(c, CHUNK), z_ref[...], xs_ref[...], bc_ref[...], hx_ref[...] * live,
            hb_ref[...] * live, dt_ref[...], state[...],
            *_ssd_param_vals(cw_ref, cb_ref, dtb_ref, alog_ref, dsk_ref, ng_ref))
        y_ref[...] = out.astype(y_ref.dtype)
        state[...] = st_new

    return pl.pallas_call(
        body, name=name,
        out_shape=[jax.ShapeDtypeStruct((lp, SSD_D_INNER), BF16),
                   jax.ShapeDtypeStruct((nc, SSD_GROUPS * SSD_STATE, 256), F32)],
        grid=(nc,),
        in_specs=_ssd_in_specs(False, nc) + [pl.BlockSpec(c.shape, lambda i: (0, 0)) for c in consts],
        out_specs=[pl.BlockSpec((CHUNK, SSD_D_INNER), lambda i: (i, 0)),
                   pl.BlockSpec((1, SSD_GROUPS * SSD_STATE, 256), lambda i: (i, 0, 0))],
        scratch_shapes=[pltpu.VMEM((SSD_GROUPS * SSD_STATE, 256), F32)],
        compiler_params=_cparams(dimension_semantics=("arbitrary",)),
    )(zxd, zxd, zxd, zxd, zxd, zxd, *consts)


def _ssd_bwd(zxd, states, d_y, consts, name):
    lp = zxd.shape[0]
    nc = lp // CHUNK

    def body(z_ref, xs_ref, bc_ref, hx_ref, hb_ref, dt_ref, st_ref, dy_ref, cw_ref, cb_ref, dtb_ref,
             alog_ref, dsk_ref, ng_ref, dz_ref, dcw_ref, dcb_ref, ddtb_ref, dalog_ref, ddsk_ref,
             dng_ref, d_state, d_hx, d_hb):
        i = pl.program_id(0)
        c = nc - 1 - i

        @pl.when(i == 0)
        def _():
            d_state[...] = jnp.zeros_like(d_state)
            d_hx[...] = jnp.zeros_like(d_hx)
            d_hb[...] = jnp.zeros_like(d_hb)
            for r in (dcw_ref, dcb_ref, ddtb_ref, dalog_ref, ddsk_ref, dng_ref):
                r[...] = jnp.zeros_like(r)

        live = (c > 0).astype(F32)
        fn = functools.partial(_ssd_chunk, _row_mask(c, CHUNK))
        prim = (z_ref[...], xs_ref[...], bc_ref[...], hx_ref[...] * live, hb_ref[...] * live,
                dt_ref[...], st_ref[0],
                *_ssd_param_vals(cw_ref, cb_ref, dtb_ref, alog_ref, dsk_ref, ng_ref))
        _, vjp = jax.vjp(fn, *prim)
        (d_z, d_xs, d_bc, g_hx, g_hb, d_dt, g_st, *d_par) = vjp((dy_ref[...], d_state[...]))
        zeros = jnp.zeros((CHUNK - 8, SSD_D_INNER), F32)
        d_xs = d_xs + jnp.concatenate([zeros, d_hx[...]], axis=0)
        d_bc = d_bc + jnp.concatenate([zeros, d_hb[...]], axis=0)
        dz_ref[:, 0:SSD_D_INNER] = d_z.astype(dz_ref.dtype)
        dz_ref[:, SSD_D_INNER:2 * SSD_D_INNER] = d_xs.astype(dz_ref.dtype)
        dz_ref[:, 2 * SSD_D_INNER:3 * SSD_D_INNER] = d_bc.astype(dz_ref.dtype)
        dz_ref[:, 3 * SSD_D_INNER:] = d_dt.astype(dz_ref.dtype)
        d_state[...] = g_st
        d_hx[...] = g_hx * live
        d_hb[...] = g_hb * live
        for k in range(SSD_CONV):
            dcw_ref[k:k + 1, 0:SSD_D_INNER] += d_par[k]
            dcw_ref[k:k + 1, SSD_D_INNER:2 * SSD_D_INNER] += d_par[SSD_CONV + k]
        dcb_ref[:, 0:SSD_D_INNER] += d_par[8]
        dcb_ref[:, SSD_D_INNER:2 * SSD_D_INNER] += d_par[9]
        ddtb_ref[...] += d_par[10]
        dalog_ref[...] += d_par[11]
        ddsk_ref[...] += d_par[12]
        dng_ref[...] += d_par[13]

    const_specs = [pl.BlockSpec(c.shape, lambda i: (0, 0)) for c in consts]
    return pl.pallas_call(
        body, name=name,
        out_shape=[jax.ShapeDtypeStruct((lp, SSD_IN_PAD), BF16)]
        + [jax.ShapeDtypeStruct(c.shape, F32) for c in consts],
        grid=(nc,),
        in_specs=_ssd_in_specs(True, nc)
        + [pl.BlockSpec((1, SSD_GROUPS * SSD_STATE, 256), lambda i: (nc - 1 - i, 0, 0)),
           pl.BlockSpec((CHUNK, SSD_D_INNER), lambda i: (nc - 1 - i, 0))] + const_specs,
        out_specs=[pl.BlockSpec((CHUNK, SSD_IN_PAD), lambda i: (nc - 1 - i, 0))] + const_specs,
        scratch_shapes=[pltpu.VMEM((SSD_GROUPS * SSD_STATE, 256), F32),
                        pltpu.VMEM((8, SSD_D_INNER), F32), pltpu.VMEM((8, SSD_D_INNER), F32)],
        compiler_params=_cparams(dimension_semantics=("arbitrary",)),
    )(zxd, zxd, zxd, zxd, zxd, zxd, states, d_y, *consts)


@jax.custom_vjp
def _rot_half(x):
    lane = lax.broadcasted_iota(jnp.int32, x.shape, 1)
    lo = (lane >= MLA_NOPE) & (lane < MLA_NOPE + MLA_ROPE // 2)
    hi = (lane >= MLA_NOPE + MLA_ROPE // 2) & (lane < MLA_QK)
    down = pltpu.roll(x, HEAD_SLOT - MLA_ROPE // 2, 1)
    up = pltpu.roll(x, MLA_ROPE // 2, 1)
    return jnp.where(lo, -down, jnp.where(hi, up, 0.0))


def _rot_half_fwd(x):
    return _rot_half(x), None


def _rot_half_bwd(_, ct):
    return (-_rot_half(ct),)


_rot_half.defvjp(_rot_half_fwd, _rot_half_bwd)


def _head_norm_rope(t, gain, cos, sin):
    n = t * lax.rsqrt(jnp.sum(t * t, axis=-1, keepdims=True) * (1.0 / MLA_QK) + EPS) * gain
    return n * cos + _rot_half(n) * sin


def _qk_prep(q_raw, kn_raw, kpe, cos, sin, qg, kg):
    qs, ks = [], []
    for h in range(MLA_HEADS):
        sl = slice(h * HEAD_SLOT, (h + 1) * HEAD_SLOT)
        qs.append(_head_norm_rope(q_raw[:, sl], qg, cos, sin))
        ks.append(_head_norm_rope(kn_raw[:, sl] + kpe, kg, cos, sin))
    return jnp.concatenate(qs, axis=1), jnp.concatenate(ks, axis=1)


def _lat_norm(kv_lat, q_lat, kvg, qg):
    return _rms(kv_lat, kvg), _rms(q_lat, qg)


_NEG = -1e30
_SCALE = MLA_QK ** -0.5


STRIP = 128
_EXP2_SCALE = _SCALE * math.log2(math.e)


def _strip_mask(kind, blk, c, t):
    if kind is None:
        return None
    kpos = blk * t + c * STRIP + lax.broadcasted_iota(jnp.int32, (1, STRIP), 1)
    if kind == 'keys':
        return kpos >= NPAD
    qpos = blk * t + lax.broadcasted_iota(jnp.int32, (t, 1), 0)
    return (kpos <= qpos) & ((kpos >= NPAD) | (kpos == qpos))


def _attn_fwd(q, k, v, name):
    lp = q.shape[0]
    t = _pick(lp, (384, 256, 128))
    nb = lp // t
    ns = t // STRIP
    hp = HEADS_PER_STEP
    wide = hp * HEAD_SLOT
    heads = [slice(a * HEAD_SLOT, (a + 1) * HEAD_SLOT) for a in range(hp)]

    def body(q_ref, k_ref, v_ref, o_ref, lse_ref, s_scr, p_scr, m_scr, l_scr, acc_scr):
        qi = pl.program_id(1)
        m_scr[...] = jnp.full_like(m_scr, _NEG)
        l_scr[...] = jnp.zeros_like(l_scr)
        acc_scr[...] = jnp.zeros_like(acc_scr)

        def tile(ki, kind):
            rows = pl.ds(pl.multiple_of(ki * t, t), t)
            for a in range(hp):
                s_scr[a] = lax.dot_general(q_ref[:, heads[a]], k_ref[rows, heads[a]], _NT,
                                           preferred_element_type=F32)

                def strip(c):
                    sc = s_scr[a, :, c * STRIP:(c + 1) * STRIP]
                    mask = _strip_mask(kind, ki, c, t)
                    return sc if mask is None else jnp.where(mask, sc, _NEG)

                ml = strip(0)
                for c in range(1, ns):
                    ml = jnp.maximum(ml, strip(c))
                m_old = m_scr[a]
                m_new = jnp.maximum(m_old, jnp.max(ml, axis=-1, keepdims=True))
                alpha = jnp.exp2((m_old - m_new) * _EXP2_SCALE)
                psum = None
                for c in range(ns):
                    pc = jnp.exp2((strip(c) - m_new) * _EXP2_SCALE)
                    p_scr[a, :, c * STRIP:(c + 1) * STRIP] = pc.astype(BF16)
                    psum = pc if psum is None else psum + pc
                l_scr[a] = alpha * l_scr[a] + psum
                acc_scr[a] = alpha * acc_scr[a] + lax.dot_general(
                    p_scr[a], v_ref[rows, heads[a]], _NN, preferred_element_type=F32)
                m_scr[a] = m_new

        @pl.when(qi > 0)
        def _():
            tile(0, 'keys')

        def inner(ki, carry):
            tile(ki, None)
            return carry

        lax.fori_loop(1, qi, inner, 0)
        tile(qi, 'diag')
        for a in range(hp):
            l = jnp.sum(l_scr[a], axis=-1, keepdims=True)
            o_ref[:, heads[a]] = acc_scr[a] / l * _row_mask(qi, t)
            lse_ref[a] = m_scr[a] * _SCALE + jnp.log(l)

    qspec = pl.BlockSpec((t, wide), lambda g, i: (i, g))
    kspec = pl.BlockSpec((lp, wide), lambda g, i: (0, g))
    return pl.pallas_call(
        body, name=name,
        out_shape=[jax.ShapeDtypeStruct((lp, MLA_WIDE), F32),
                   jax.ShapeDtypeStruct((MLA_HEADS, lp, 1), F32)],
        grid=(MLA_HEADS // hp, nb),
        in_specs=[qspec, kspec, kspec],
        out_specs=[qspec, pl.BlockSpec((hp, t, 1), lambda g, i: (g, i, 0))],
        scratch_shapes=[pltpu.VMEM((hp, t, t), F32), pltpu.VMEM((hp, t, t), BF16),
                        pltpu.VMEM((hp, t, 1), F32), pltpu.VMEM((hp, t, STRIP), F32),
                        pltpu.VMEM((hp, t, HEAD_SLOT), F32)],
        compiler_params=_cparams(dimension_semantics=("parallel", "arbitrary")),
    )(q, k, v)


def _attn_delta(do, o, name):
    lp = do.shape[0]
    t = _pick(lp, (384, 256, 128))

    def body(do_ref, o_ref, dob_ref, delta_ref):
        dob_ref[...] = do_ref[...].astype(BF16)
        for h in range(MLA_HEADS):
            sl = slice(h * HEAD_SLOT, (h + 1) * HEAD_SLOT)
            delta_ref[h] = jnp.sum(do_ref[:, sl] * o_ref[:, sl], axis=-1, keepdims=True)

    spec = pl.BlockSpec((t, MLA_WIDE), lambda i: (i, 0))
    return pl.pallas_call(
        body, name=name,
        out_shape=[jax.ShapeDtypeStruct((lp, MLA_WIDE), BF16), jax.ShapeDtypeStruct((MLA_HEADS, lp, 1), F32)],
        grid=(lp // t,), in_specs=[spec, spec],
        out_specs=[spec, pl.BlockSpec((MLA_HEADS, t, 1), lambda i: (0, i, 0))],
        compiler_params=_cparams(dimension_semantics=("parallel",)),
    )(do, o)


def _attn_bwd(q, k, v, do, lse, delta, name):
    lp = q.shape[0]
    t = _pick(lp, (384, 256, 128))
    nb = lp // t
    ns = t // STRIP
    hp = HEADS_PER_STEP
    wide = hp * HEAD_SLOT
    heads = [slice(a * HEAD_SLOT, (a + 1) * HEAD_SLOT) for a in range(hp)]
    log2e = math.log2(math.e)

    def body(q_ref, k_ref, v_ref, do_ref, lse_ref, delta_ref, dq_ref, dk_ref, dv_ref,
             s_scr, dp_scr, p_scr, ds_scr):
        kj = pl.program_id(1)

        @pl.when(kj == 0)
        def _():
            dq_ref[...] = jnp.zeros_like(dq_ref)

        dk_ref[...] = jnp.zeros_like(dk_ref)
        dv_ref[...] = jnp.zeros_like(dv_ref)

        def tile(qi, kind):
            rows = pl.ds(pl.multiple_of(qi * t, t), t)
            for a in range(hp):
                qb, dob = q_ref[rows, heads[a]], do_ref[rows, heads[a]]
                kb, vb = k_ref[:, heads[a]], v_ref[:, heads[a]]
                s_scr[a] = lax.dot_general(qb, kb, _NT, preferred_element_type=F32)
                dp_scr[a] = lax.dot_general(dob, vb, _NT, preferred_element_type=F32)
                lse2 = lse_ref[a, rows, :] * log2e
                delta = delta_ref[a, rows, :]
                for c in range(ns):
                    cs = slice(c * STRIP, (c + 1) * STRIP)
                    pc = jnp.exp2(s_scr[a, :, cs] * _EXP2_SCALE - lse2)
                    pc = jnp.where(_strip_mask(kind, kj, c, t), pc, 0.0)
                    p_scr[a, :, cs] = pc.astype(BF16)
                    ds_scr[a, :, cs] = (pc * (dp_scr[a, :, cs] - delta)).astype(BF16)
                dq_ref[rows, heads[a]] += lax.dot_general(ds_scr[a], kb, _NN,
                                                          preferred_element_type=F32) * _SCALE
                dv_ref[:, heads[a]] += lax.dot_general(p_scr[a], dob, _TN, preferred_element_type=F32)
                dk_ref[:, heads[a]] += lax.dot_general(ds_scr[a], qb, _TN, preferred_element_type=F32)

        tile(kj, 'diag')

        def below(qi, carry):
            tile(qi, 'keys')
            return carry

        lax.fori_loop(kj + 1, nb, below, 0)
        dk_ref[...] = dk_ref[...] * _SCALE

    whole = pl.BlockSpec((lp, wide), lambda g, j: (0, g))
    kspec = pl.BlockSpec((t, wide), lambda g, j: (j, g))
    stat = pl.BlockSpec((hp, lp, 1), lambda g, j: (g, 0, 0))
    return pl.pallas_call(
        body, name=name,
        out_shape=[jax.ShapeDtypeStruct((lp, MLA_WIDE), F32)] * 3,
        grid=(MLA_HEADS // hp, nb),
        in_specs=[whole, kspec, kspec, whole, stat, stat],
        out_specs=[whole, kspec, kspec],
        scratch_shapes=[pltpu.VMEM((hp, t, t), F32), pltpu.VMEM((hp, t, t), F32),
                        pltpu.VMEM((hp, t, t), BF16), pltpu.VMEM((hp, t, t), BF16)],
        compiler_params=_cparams(dimension_semantics=("parallel", "arbitrary")),
    )(q, k, v, do, lse, delta)


def _rope_tables(lp):
    inv = 1.0 / (ROPE_THETA ** (jnp.arange(0, MLA_ROPE, 2, dtype=F32) / MLA_ROPE))
    pos = jnp.maximum(jnp.arange(lp, dtype=jnp.int32) - NPAD, 0).astype(F32)
    ang = pos[:, None] * inv[None, :]
    cos, sin = jnp.cos(ang), jnp.sin(ang)
    z32 = jnp.zeros((lp, HEAD_SLOT - MLA_QK), F32)
    cos_t = jnp.concatenate([jnp.ones((lp, MLA_NOPE), F32), cos, cos, z32], axis=1)
    sin_t = jnp.concatenate([jnp.zeros((lp, MLA_NOPE), F32), sin, sin, z32], axis=1)
    return cos_t, sin_t


def _loss_head(h, target, name):
    lp = h.shape[0]

    def body(h_ref, t_ref, d_ref, loss_ref):
        i = pl.program_id(0)

        @pl.when(i == 0)
        def _():
            d_ref[...] = jnp.zeros_like(d_ref)
            loss_ref[...] = jnp.zeros_like(loss_ref)

        @pl.when(i > 0)
        def _():
            err = h_ref[...] - t_ref[...]
            d_ref[...] = err * (1.0 / D_MODEL)
            loss_ref[...] += jnp.sum(err * err, axis=0, keepdims=True) * (0.5 / D_MODEL)

    return pl.pallas_call(
        body, name=name,
        out_shape=[jax.ShapeDtypeStruct((lp, D_MODEL), F32), jax.ShapeDtypeStruct((1, D_MODEL), F32)],
        grid=(lp // CHUNK,),
        in_specs=[pl.BlockSpec((CHUNK, D_MODEL), lambda i: (i, 0)),
                  pl.BlockSpec((CHUNK, D_MODEL), lambda i: (jnp.maximum(i - 1, 0), 0))],
        out_specs=[pl.BlockSpec((CHUNK, D_MODEL), lambda i: (i, 0)),
                   pl.BlockSpec((1, D_MODEL), lambda i: (0, 0))],
        compiler_params=_cparams(dimension_semantics=("arbitrary",)),
    )(h, target)


def _pad_cols(w, n):
    return jnp.pad(w, [(0, 0)] * (w.ndim - 1) + [(0, n - w.shape[-1])])


def _prep_weights(w):
    p = {}
    p['ssd_in'] = [_pad_cols(w['ssd_w_in'][j], SSD_IN_PAD).astype(BF16) for j in range(2)]
    p['ssd_out'] = [w['ssd_w_out'][j].astype(BF16) for j in range(2)]
    p['mla_in'], p['mla_qb'], p['mla_kvb'], p['mla_out'] = [], [], [], []
    for j in range(2):
        wi = w['mla_w_in'][j]
        kpe = jnp.pad(wi[:, MLA_Q_RANK + MLA_KV_RANK:], ((0, 0), (MLA_NOPE, HEAD_SLOT - MLA_QK)))
        p['mla_in'].append(jnp.concatenate(
            [wi[:, MLA_Q_RANK:MLA_Q_RANK + MLA_KV_RANK], kpe, wi[:, :MLA_Q_RANK]], axis=1).astype(BF16))
        qb = w['mla_w_q_b'][j].reshape(MLA_Q_RANK, MLA_HEADS, MLA_QK)
        p['mla_qb'].append(_pad_cols(qb, HEAD_SLOT).reshape(MLA_Q_RANK, MLA_WIDE).astype(BF16))
        kvb = w['mla_w_kv_b'][j].reshape(MLA_KV_RANK, MLA_HEADS, MLA_NOPE + MLA_V)
        kn = _pad_cols(kvb[:, :, :MLA_NOPE], HEAD_SLOT).reshape(MLA_KV_RANK, MLA_WIDE)
        vv = _pad_cols(kvb[:, :, MLA_NOPE:], HEAD_SLOT).reshape(MLA_KV_RANK, MLA_WIDE)
        p['mla_kvb'].append(jnp.concatenate([kn, vv], axis=1).astype(BF16))
        wo = w['mla_w_out'][j].reshape(MLA_HEADS, MLA_V, D_MODEL)
        p['mla_out'].append(jnp.pad(wo, ((0, 0), (0, HEAD_SLOT - MLA_V), (0, 0)))
                            .reshape(MLA_WIDE, D_MODEL).astype(BF16))
    p['up'] = [w['mlp_w_up'][i].astype(BF16) for i in range(4)]
    p['down'] = [w['mlp_w_down'][i].astype(BF16) for i in range(4)]
    return p


def _pad128(v):
    return _pad_cols(v.reshape(1, -1), 128)


def _sqrelu(u):
    r = jnp.maximum(u, 0.0)
    return r * r


def _local_step(x, target, w):
    seq = x.shape[0]
    lp = NPAD + N_META + seq
    p = _prep_weights(w)
    h = jnp.concatenate([jnp.zeros((NPAD, D_MODEL), F32), w['meta_tokens'], x], axis=0)
    cos_t, sin_t = _rope_tables(lp)
    rt = _pick(lp, (384, 256, 128))
    saved = []
    for i in range(4):
        j = i // 2
        s = {'h0': h}
        g_mix = w['ln_mix'][i].reshape(1, -1)
        hn = _rms_fwd(h, g_mix, f"rms_mix_f{i}")
        s['hn'] = hn
        if i % 2 == 0:
            zxd = _mm(hn, p['ssd_in'][j], 'nn', name=f"ssd_in_f{i}")
            consts = _ssd_consts(w['ssd_conv_w'][j], w['ssd_conv_b'][j].reshape(1, -1),
                                 _pad128(w['ssd_dt_bias'][j]), _pad128(w['ssd_a_log'][j]),
                                 _pad128(w['ssd_d'][j]), w['ssd_norm'][j].reshape(1, -1))
            yg, states = _ssd_fwd(zxd, consts, f"ssd_core_f{i}")
            s.update(zxd=zxd, consts=consts, yg=yg, states=states)
            h = _mm(yg, p['ssd_out'][j], 'nn', name=f"ssd_out_f{i}", epi=lambda r, hv: hv + r, extras=(h,))
        else:
            lat = _mm(hn, p['mla_in'][j], 'nn', name=f"mla_in_f{i}")
            kvg = w['mla_kv_a_norm'][j].reshape(1, -1)
            qag = w['mla_q_a_norm'][j].reshape(1, -1)
            kvn, qn = _row_call(lambda _, a, b, c, d: _lat_norm(a, b, c, d),
                                [(lat, MLA_KV_RANK, 0), (lat, MLA_Q_RANK, 1)], [kvg, qag],
                                [(MLA_KV_RANK, BF16), (MLA_Q_RANK, BF16)], n_rows=lp, tile=rt,
                                name=f"mla_latnorm_f{i}")
            q_raw = _mm(qn, p['mla_qb'][j], 'nn', name=f"mla_qb_f{i}")
            kv_raw = _mm(kvn, p['mla_kvb'][j], 'nn', name=f"mla_kvb_f{i}")
            qg = _pad_cols(w['mla_q_norm'][j].reshape(1, -1), HEAD_SLOT)
            kg = _pad_cols(w['mla_k_norm'][j].reshape(1, -1), HEAD_SLOT)

            def prep_fwd(_, qr, kn, kpe, vv, cs, sn, qgv, kgv):
                qq, kk = _qk_prep(qr, kn, kpe, cs, sn, qgv, kgv)
                return qq, kk, vv

            q, k, v = _row_call(prep_fwd,
                                [(q_raw, MLA_WIDE, 0), (kv_raw, MLA_WIDE, 0), (lat, HEAD_SLOT, 2),
                                 (kv_raw, MLA_WIDE, 1), (cos_t, HEAD_SLOT, 0), (sin_t, HEAD_SLOT, 0)],
                                [qg, kg], [(MLA_WIDE, BF16)] * 3, n_rows=lp, tile=rt,
                                name=f"mla_qkprep_f{i}")
            o, lse = _attn_fwd(q, k, v, f"mla_attn_f{i}")
            s.update(lat=lat, kvg=kvg, qag=qag, kvn=kvn, qn=qn, q_raw=q_raw, kv_raw=kv_raw, qg=qg, kg=kg,
                     q=q, k=k, v=v, o=o, lse=lse)
            h = _mm(o, p['mla_out'][j], 'nn', name=f"mla_out_f{i}", epi=lambda r, hv: hv + r, extras=(h,))
        s['h1'] = h
        g_mlp = w['ln_mlp'][i].reshape(1, -1)
        hn2 = _rms_fwd(h, g_mlp, f"rms_mlp_f{i}")
        u = _mm(hn2, p['up'][i], 'nn', name=f"mlp_up_f{i}")
        h = _mm(u, p['down'][i], 'nn', name=f"mlp_down_f{i}", a_fn=_sqrelu,
                epi=lambda r, hv: hv + r, extras=(h,))
        s.update(hn2=hn2, u=u, g_mix=g_mix, g_mlp=g_mlp)
        saved.append(s)

    dh, loss_row = _loss_head(h, target, "loss_head")

    g = {k_: [None] * (4 if k_ in ('ln_mix', 'ln_mlp', 'mlp_w_up', 'mlp_w_down') else 2)
         for k_ in w if k_ != 'meta_tokens'}
    for i in reversed(range(4)):
        j = i // 2
        s = saved[i]
        g['mlp_w_down'][i] = _mm(s['u'], dh, 'tn', name=f"mlp_down_dw{i}", a_fn=_sqrelu)
        du = _mm(dh, p['down'][i], 'nt', name=f"mlp_down_dx{i}", out_dtype=BF16,
                 epi=lambda r, uv: r * (2.0 * jnp.maximum(uv, 0.0)), extras=(s['u'],))
        g['mlp_w_up'][i] = _mm(s['hn2'], du, 'tn', name=f"mlp_up_dw{i}")
        d_hn2 = _mm(du, p['up'][i], 'nt', name=f"mlp_up_dx{i}")
        dh, dg = _rms_bwd(s['h1'], s['g_mlp'], d_hn2, dh, f"rms_mlp_b{i}")
        g['ln_mlp'][i] = dg[0]
        if i % 2 == 0:
            g['ssd_w_out'][j] = _mm(s['yg'], dh, 'tn', name=f"ssd_out_dw{i}")
            d_yg = _mm(dh, p['ssd_out'][j], 'nt', name=f"ssd_out_dx{i}")
            d_zxd, dcw, dcb, ddtb, dalog, ddsk, dng = _ssd_bwd(s['zxd'], s['states'], d_yg, s['consts'],
                                                              f"ssd_core_b{i}")
            g['ssd_conv_w'][j], g['ssd_conv_b'][j], g['ssd_norm'][j] = dcw, dcb[0], dng[0]
            g['ssd_dt_bias'][j], g['ssd_a_log'][j], g['ssd_d'][j] = (
                ddtb[0, :SSD_HEADS], dalog[0, :SSD_HEADS], ddsk[0, :SSD_HEADS])
            g['ssd_w_in'][j] = _mm(s['hn'], d_zxd, 'tn', name=f"ssd_in_dw{i}")
            d_hn = _mm(d_zxd, p['ssd_in'][j], 'nt', name=f"ssd_in_dx{i}")
        else:
            wo = _mm(s['o'], dh, 'tn', name=f"mla_out_dw{i}")
            g['mla_w_out'][j] = wo.reshape(MLA_HEADS, HEAD_SLOT, D_MODEL)[:, :MLA_V].reshape(-1, D_MODEL)
            do = _mm(dh, p['mla_out'][j], 'nt', name=f"mla_out_dx{i}")
            dob, delta = _attn_delta(do, s['o'], f"mla_attn_delta{i}")
            dq, dk, dv = _attn_bwd(s['q'], s['k'], s['v'], dob, s['lse'], delta, f"mla_attn_b{i}")

            def prep_bwd(_, qr, kn, kpe, cs, sn, dqv, dkv, dvv, qgv, kgv):
                _, vjp = jax.vjp(lambda a, b, c, d, e: _qk_prep(a, b, c, cs, sn, d, e), qr, kn, kpe, qgv, kgv)
                d_qr, d_kn, d_kpe, d_qg, d_kg = vjp((dqv, dkv))
                return d_qr, jnp.concatenate([d_kn, dvv], axis=1), d_kpe, d_qg, d_kg

            d_qraw, d_kvraw, d_kpe, d_qg, d_kg = _row_call(
                prep_bwd,
                [(s['q_raw'], MLA_WIDE, 0), (s['kv_raw'], MLA_WIDE, 0), (s['lat'], HEAD_SLOT, 2),
                 (cos_t, HEAD_SLOT, 0), (sin_t, HEAD_SLOT, 0), (dq, MLA_WIDE, 0), (dk, MLA_WIDE, 0),
                 (dv, MLA_WIDE, 0)],
                [s['qg'], s['kg']], [(MLA_WIDE, BF16), (2 * MLA_WIDE, BF16), (HEAD_SLOT, F32)],
                [(1, HEAD_SLOT), (1, HEAD_SLOT)], n_rows=lp, tile=_pick(lp, (128,)), name=f"mla_qkprep_b{i}")
            g['mla_q_norm'][j], g['mla_k_norm'][j] = d_qg[0, :MLA_QK], d_kg[0, :MLA_QK]
            wqb = _mm(s['qn'], d_qraw, 'tn', name=f"mla_qb_dw{i}")
            g['mla_w_q_b'][j] = wqb.reshape(MLA_Q_RANK, MLA_HEADS, HEAD_SLOT)[:, :, :MLA_QK].reshape(MLA_Q_RANK, -1)
            d_qn = _mm(d_qraw, p['mla_qb'][j], 'nt', name=f"mla_qb_dx{i}")
            wkvb = _mm(s['kvn'], d_kvraw, 'tn', name=f"mla_kvb_dw{i}").reshape(MLA_KV_RANK, 2, MLA_HEADS, HEAD_SLOT)
            g['mla_w_kv_b'][j] = jnp.concatenate([wkvb[:, 0, :, :MLA_NOPE], wkvb[:, 1, :, :MLA_V]],
                                                 axis=-1).reshape(MLA_KV_RANK, -1)
            d_kvn = _mm(d_kvraw, p['mla_kvb'][j], 'nt', name=f"mla_kvb_dx{i}")

            def lat_bwd(_, kvl, ql, dkvn, dqn, dkpe, kvgv, qagv):
                _, vjp = jax.vjp(_lat_norm, kvl, ql, kvgv, qagv)
                d_kvl, d_ql, d_kvg, d_qag = vjp((dkvn, dqn))
                return jnp.concatenate([d_kvl, dkpe, d_ql], axis=1), d_kvg, d_qag

            d_lat, d_kvg, d_qag = _row_call(
                lat_bwd, [(s['lat'], MLA_KV_RANK, 0), (s['lat'], MLA_Q_RANK, 1), (d_kvn, MLA_KV_RANK, 0),
                          (d_qn, MLA_Q_RANK, 0), (d_kpe, HEAD_SLOT, 0)],
                [s['kvg'], s['qag']], [(LAT_PAD, BF16)], [(1, MLA_KV_RANK), (1, MLA_Q_RANK)],
                n_rows=lp, tile=rt, name=f"mla_latnorm_b{i}")
            g['mla_kv_a_norm'][j], g['mla_q_a_norm'][j] = d_kvg[0], d_qag[0]
            win = _mm(s['hn'], d_lat, 'tn', name=f"mla_in_dw{i}")
            g['mla_w_in'][j] = jnp.concatenate(
                [win[:, MLA_KV_RANK + HEAD_SLOT:], win[:, :MLA_KV_RANK],
                 win[:, MLA_KV_RANK + MLA_NOPE:MLA_KV_RANK + MLA_QK]], axis=1)
            d_hn = _mm(d_lat, p['mla_in'][j], 'nt', name=f"mla_in_dx{i}")
        dh, dg = _rms_bwd(s['h0'], s['g_mix'], d_hn, dh, f"rms_mix_b{i}")
        g['ln_mix'][i] = dg[0]

    grads = {k_: jnp.stack(v_) for k_, v_ in g.items()}
    grads['meta_tokens'] = dh[NPAD:NPAD + N_META]
    return loss_row, dh[NPAD + N_META:], grads


def _all_gather8(shard, name):
    m_per, n = shard.shape

    def body(x_ref, out_ref, send_sems, recv_sems, local_sem):
        x, y, c = lax.axis_index("x"), lax.axis_index("y"), lax.axis_index("c")
        me, sibling = (x, y, c), (x, y, 1 - c)
        chips = [(1 - x, y), (x, 1 - y), (1 - x, 1 - y)]

        def rows(px, py, pc):
            return out_ref.at[pl.ds((4 * px + 2 * py + pc) * m_per, m_per), :]

        def copy(k, block, to, src=None):
            return pltpu.make_async_remote_copy(
                src_ref=rows(*block) if src is None else src, dst_ref=rows(*block),
                send_sem=send_sems.at[k], recv_sem=recv_sems.at[k], device_id=to, device_id_type=MESH)

        mine = pltpu.make_async_copy(x_ref, rows(*me), local_sem)
        mine.start()
        first = [copy(0, me, sibling, src=x_ref)]
        first += [copy(1 + j, me, (*chip, c), src=x_ref) for j, chip in enumerate(chips)]
        for cp in first:
            cp.start()
        passed = [copy(4 + j, (*chip, c), sibling) for j, chip in enumerate(chips)]
        for j, chip in enumerate(chips):
            copy(1 + j, (*chip, c), me).wait_recv()
            passed[j].start()
        copy(0, sibling, me).wait_recv()
        for j, chip in enumerate(chips):
            copy(4 + j, (*chip, 1 - c), me).wait_recv()
        for cp in first + passed:
            cp.wait_send()
        mine.wait()

    return pl.pallas_call(
        body, name=name,
        out_shape=jax.ShapeDtypeStruct((8 * m_per, n), shard.dtype),
        in_specs=[pl.BlockSpec(memory_space=pl.ANY)],
        out_specs=pl.BlockSpec(memory_space=pl.ANY),
        scratch_shapes=[pltpu.SemaphoreType.DMA((7,)), pltpu.SemaphoreType.DMA((7,)), pltpu.SemaphoreType.DMA],
    )(shard)


def _mesh_pos():
    return lax.axis_index("x"), lax.axis_index("y"), lax.axis_index("c")


def _half_rows(pc, h):
    return pl.ds(pl.multiple_of(pc * h, 16), h)


def _whole_view(ref, kind, shard_shape, k, pc):
    _, r, c = shard_shape
    rows = _half_rows(pc, r // 2)
    if kind == 'row':
        return ref.at[:, k, rows, :]
    if kind == 'col':
        return ref.at[:, rows, pl.ds(pl.multiple_of(k * c, 128), c)]
    return ref.at[k, :, rows, :]


def _whole_shape(kind, shard_shape, rows=None):
    l, r, c = shard_shape
    r = r if rows is None else rows
    return {'row': (l, 4, r, c), 'col': (l, r, 4 * c), 'colx': (4, l, r, c)}[kind]


def _gather_big(shards, kinds, name):
    n = len(shards)
    shapes = [s.shape for s in shards]

    def body(*refs):
        ins, outs = refs[:n], refs[n:2 * n]
        send_sems, recv_sems, local_sems = refs[2 * n:]
        x, y, c = _mesh_pos()
        me, sibling = (x, y, c), (x, y, 1 - c)
        chips = [(1 - x, y), (x, 1 - y), (1 - x, 1 - y)]

        def place(a, px, py, pc):
            return _whole_view(outs[a], kinds[a], shapes[a], 2 * px + py, pc)

        def own(a):
            return ins[a].at[:, _half_rows(c, shapes[a][1] // 2), :]

        def copy(a, k, block, to, src=None):
            return pltpu.make_async_remote_copy(
                src_ref=place(a, *block) if src is None else src, dst_ref=place(a, *block),
                send_sem=send_sems.at[7 * a + k], recv_sem=recv_sems.at[7 * a + k],
                device_id=to, device_id_type=MESH)

        mine = [pltpu.make_async_copy(own(a), place(a, *me), local_sems.at[a]) for a in range(n)]
        first = [copy(a, 1 + j, me, (*chip, c), src=own(a)) for j, chip in enumerate(chips) for a in range(n)]
        first += [copy(a, 0, me, sibling, src=own(a)) for a in range(n)]
        for cp in first + mine:
            cp.start()
        passed = []
        for j, chip in enumerate(chips):
            for a in range(n):
                copy(a, 1 + j, (*chip, c), me).wait_recv()
                passed.append(copy(a, 4 + j, (*chip, c), sibling))
                passed[-1].start()
        for a in range(n):
            copy(a, 0, sibling, me).wait_recv()
        for j, chip in enumerate(chips):
            for a in range(n):
                copy(a, 4 + j, (*chip, 1 - c), me).wait_recv()
        for cp in first + passed:
            cp.wait_send()
        for cp in mine:
            cp.wait()

    return pl.pallas_call(
        body, name=name,
        out_shape=[jax.ShapeDtypeStruct(_whole_shape(k, s.shape), s.dtype) for k, s in zip(kinds, shards)],
        in_specs=[pl.BlockSpec(memory_space=pl.ANY)] * n,
        out_specs=[pl.BlockSpec(memory_space=pl.ANY)] * n,
        scratch_shapes=[pltpu.SemaphoreType.DMA((7 * n,)), pltpu.SemaphoreType.DMA((7 * n,)),
                        pltpu.SemaphoreType.DMA((n,))],
    )(*shards)


def _rs_swap(wholes, kinds, shapes, name):
    n = len(wholes)

    def body(*refs):
        ins, outs = refs[:n], refs[n:2 * n]
        send_sems, recv_sems = refs[2 * n:]
        x, y, c = _mesh_pos()
        cps = []
        for a in range(n):
            rows = _half_rows(1 - c, shapes[a][1] // 2)
            src = ins[a].at[:, rows, :] if kinds[a] == 'col' else ins[a].at[:, :, rows, :]
            cps.append(pltpu.make_async_remote_copy(
                src_ref=src, dst_ref=outs[a], send_sem=send_sems.at[a], recv_sem=recv_sems.at[a],
                device_id=(x, y, 1 - c), device_id_type=MESH))
        for cp in cps:
            cp.start()
        for cp in cps:
            cp.wait()

    return pl.pallas_call(
        body, name=name,
        out_shape=[jax.ShapeDtypeStruct(_whole_shape(k, s, s[1] // 2), w.dtype)
                   for k, s, w in zip(kinds, shapes, wholes)],
        in_specs=[pl.BlockSpec(memory_space=pl.ANY)] * n,
        out_specs=[pl.BlockSpec(memory_space=pl.ANY)] * n,
        scratch_shapes=[pltpu.SemaphoreType.DMA((n,)), pltpu.SemaphoreType.DMA((n,))],
    )(*wholes)


def _rs_exchange(parts, kinds, shapes, name):
    n = len(parts)

    def body(*refs):
        ins, outs = refs[:n], refs[n:2 * n]
        send_sems, recv_sems, local_sems = refs[2 * n:]
        x, y, c = _mesh_pos()
        kme = 2 * x + y
        chips = [(1 - x, y), (x, 1 - y), (1 - x, 1 - y)]

        def slab(a, k):
            if kinds[a] == 'row':
                return ins[a].at[:, k]
            if kinds[a] == 'col':
                cw = shapes[a][2]
                return ins[a].at[:, :, pl.ds(pl.multiple_of(k * cw, 128), cw)]
            return ins[a].at[k]

        cps = [pltpu.make_async_remote_copy(
            src_ref=slab(a, 2 * px + py), dst_ref=outs[a].at[kme], send_sem=send_sems.at[3 * a + j],
            recv_sem=recv_sems.at[3 * a + j], device_id=(px, py, c), device_id_type=MESH)
            for j, (px, py) in enumerate(chips) for a in range(n)]
        cps_local = [pltpu.make_async_copy(slab(a, kme), outs[a].at[kme], local_sems.at[a]) for a in range(n)]
        for cp in cps + cps_local:
            cp.start()
        for cp in cps + cps_local:
            cp.wait()

    return pl.pallas_call(
        body, name=name,
        out_shape=[jax.ShapeDtypeStruct((4, s[0], s[1] // 2, s[2]), p.dtype) for s, p in zip(shapes, parts)],
        in_specs=[pl.BlockSpec(memory_space=pl.ANY)] * n,
        out_specs=[pl.BlockSpec(memory_space=pl.ANY)] * n,
        scratch_shapes=[pltpu.SemaphoreType.DMA((3 * n,)), pltpu.SemaphoreType.DMA((3 * n,)),
                        pltpu.SemaphoreType.DMA((n,))],
    )(*parts)


def _rs_share(shards, name):
    n = len(shards)

    def body(*refs):
        outs = refs[n:2 * n]
        send_sems, recv_sems = refs[2 * n:]
        x, y, c = _mesh_pos()
        cps = []
        for a in range(n):
            rows = outs[a].at[:, _half_rows(c, shards[a].shape[1] // 2), :]
            cps.append(pltpu.make_async_remote_copy(
                src_ref=rows, dst_ref=rows, send_sem=send_sems.at[a], recv_sem=recv_sems.at[a],
                device_id=(x, y, 1 - c), device_id_type=MESH))
        for cp in cps:
            cp.start()
        for cp in cps:
            cp.wait()

    return pl.pallas_call(
        body, name=name,
        out_shape=[jax.ShapeDtypeStruct(s.shape, s.dtype) for s in shards],
        in_specs=[pl.BlockSpec(memory_space=pl.ANY)] * n,
        out_specs=[pl.BlockSpec(memory_space=pl.ANY)] * n,
        input_output_aliases={a: a for a in range(n)},
        scratch_shapes=[pltpu.SemaphoreType.DMA((n,)), pltpu.SemaphoreType.DMA((n,))],
    )(*shards)


def _tile_rows(rows, cols, budget=2 * 1024 * 1024):
    for t in (1024, 512, 256, 128, 64, 32, 16, 8):
        if rows % t == 0 and t * cols * 4 <= budget:
            return t
    return rows


def _add_half(g3, r3, c_idx, name):
    a, h, n = r3.shape
    t = _tile_rows(h, n)
    nt = h // t

    def body(c_ref, g_ref, r_ref, o_ref):
        o_ref[...] = (g_ref[...] + r_ref[...]).astype(o_ref.dtype)

    return pl.pallas_call(
        body, name=name, out_shape=jax.ShapeDtypeStruct((a, h, n), BF16),
        grid_spec=pltpu.PrefetchScalarGridSpec(
            num_scalar_prefetch=1, grid=(a, nt),
            in_specs=[pl.BlockSpec((1, t, n), lambda k, i, c: (k, c[0] * nt + i, 0)),
                      pl.BlockSpec((1, t, n), lambda k, i, c: (k, i, 0))],
            out_specs=pl.BlockSpec((1, t, n), lambda k, i, c: (k, i, 0))),
        compiler_params=_cparams(dimension_semantics=("parallel", "parallel")),
    )(c_idx, g3, r3)


def _sum4(parts, c_idx, name):
    _, l, h, n = parts.shape
    t = _tile_rows(h, n, 1024 * 1024)
    nt = h // t

    def body(c_ref, p_ref, o_ref):
        pv = p_ref[...].astype(F32)
        o_ref[...] = ((pv[0] + pv[1]) + pv[2]) + pv[3]

    return pl.pallas_call(
        body, name=name, out_shape=jax.ShapeDtypeStruct((l, 2 * h, n), F32),
        grid_spec=pltpu.PrefetchScalarGridSpec(
            num_scalar_prefetch=1, grid=(l, nt),
            in_specs=[pl.BlockSpec((4, 1, t, n), lambda k, i, c: (0, k, i, 0))],
            out_specs=pl.BlockSpec((1, t, n), lambda k, i, c: (k, c[0] * nt + i, 0))),
        compiler_params=_cparams(dimension_semantics=("parallel", "parallel")),
    )(c_idx, parts)


def _sum8(parts, name):
    _, m, n = parts.shape

    def body(p_ref, o_ref):
        acc = p_ref[0]
        for d in range(1, 8):
            acc = acc + p_ref[d]
        o_ref[...] = acc

    return pl.pallas_call(body, name=name, out_shape=jax.ShapeDtypeStruct((m, n), F32))(parts)


def _adamw(wp, gp, mp, vp, name):
    r, n = wp.shape
    t = _tile_rows(r, n, 1024 * 1024)

    def body(w_ref, g_ref, m_ref, v_ref, d_ref, mo_ref, vo_ref):
        gv = g_ref[...]
        m2 = ADAM_B1 * m_ref[...] + (1.0 - ADAM_B1) * gv
        v2 = ADAM_B2 * v_ref[...] + (1.0 - ADAM_B2) * (gv * gv)
        m_hat = m2 / (1.0 - ADAM_B1 ** ADAM_STEP)
        v_hat = v2 / (1.0 - ADAM_B2 ** ADAM_STEP)
        d_ref[...] = -ADAM_LR * (m_hat / (jnp.sqrt(v_hat) + ADAM_EPS) + ADAM_WD * w_ref[...])
        mo_ref[...] = m2
        vo_ref[...] = v2

    spec = pl.BlockSpec((t, n), lambda i: (i, 0))
    return pl.pallas_call(
        body, name=name, out_shape=[jax.ShapeDtypeStruct((r, n), F32)] * 3, grid=(r // t,),
        in_specs=[spec] * 4, out_specs=[spec] * 3,
        compiler_params=_cparams(dimension_semantics=("parallel",)),
    )(wp, gp, mp, vp)


BIG = (('ssd_w_in', 'colx'), ('ssd_w_out', 'row'), ('mla_w_in', 'row'), ('mla_w_q_b', 'col'),
       ('mla_w_kv_b', 'col'), ('mla_w_out', 'row'), ('mlp_w_up', 'col'), ('mlp_w_down', 'row'))
SMALL_SHARDED = (('meta_tokens', 1), ('ssd_conv_w', 2), ('mla_q_a_norm', 1), ('mla_kv_a_norm', 1))
SMALL_REPL = ('ln_mix', 'ln_mlp', 'ssd_conv_b', 'ssd_dt_bias', 'ssd_a_log', 'ssd_d', 'ssd_norm',
              'mla_q_norm', 'mla_k_norm')
ALL_NAMES = ('meta_tokens', 'ln_mix', 'ln_mlp', 'ssd_w_in', 'ssd_conv_w', 'ssd_conv_b', 'ssd_dt_bias',
             'ssd_a_log', 'ssd_d', 'ssd_norm', 'ssd_w_out', 'mla_w_in', 'mla_q_a_norm', 'mla_w_q_b',
             'mla_kv_a_norm', 'mla_w_kv_b', 'mla_q_norm', 'mla_k_norm', 'mla_w_out', 'mlp_w_up', 'mlp_w_down')


def _pack(arrs, rows_mult):
    flat = jnp.concatenate([a.reshape(-1) for a in arrs])
    per = LANES * rows_mult
    pad = (-flat.shape[0]) % per
    if pad:
        flat = jnp.concatenate([flat, jnp.zeros((pad,), flat.dtype)])
    return flat.reshape(-1, LANES)


def _unpack(pack, shapes):
    flat = pack.reshape(-1)
    out, off = [], 0
    for shp in shapes:
        n = math.prod(shp)
        out.append(flat[off:off + n].reshape(shp))
        off += n
    return out


def _split4(full, axis):
    shp = full.shape
    r = full.reshape(shp[:axis] + (4, shp[axis] // 4) + shp[axis + 1:])
    return jnp.moveaxis(r, axis, 0)


def _join4(parts, axis):
    r = jnp.moveaxis(parts, 0, axis)
    shp = r.shape
    return r.reshape(shp[:axis] + (shp[axis] * shp[axis + 1],) + shp[axis + 2:])


def _gather_params(shards, table, dtype, c, name):
    pack = _pack([shards[n].astype(dtype) for n, _ in table], 16)
    half = pack.shape[0] // 2
    mine = lax.dynamic_slice_in_dim(pack, c * half, half, axis=0)
    full = _all_gather8(mine, name).reshape(4, -1)
    out, off = {}, 0
    for n, ax in table:
        cnt = math.prod(shards[n].shape)
        out[n] = _join4(full[:, off:off + cnt].reshape((4,) + shards[n].shape), ax)
        off += cnt
    return out


def kernel(x, meta_tokens, ln_mix, ln_mlp, ssd_w_in, ssd_conv_w, ssd_conv_b, ssd_dt_bias, ssd_a_log, ssd_d, ssd_norm, ssd_w_out, mla_w_in, mla_q_a_norm, mla_w_q_b, mla_kv_a_norm, mla_w_kv_b, mla_q_norm, mla_k_norm, mla_w_out, mlp_w_up, mlp_w_down, loss_target, m_meta_tokens, m_ln_mix, m_ln_mlp, m_ssd_w_in, m_ssd_conv_w, m_ssd_conv_b, m_ssd_dt_bias, m_ssd_a_log, m_ssd_d, m_ssd_norm, m_ssd_w_out, m_mla_w_in, m_mla_q_a_norm, m_mla_w_q_b, m_mla_kv_a_norm, m_mla_w_kv_b, m_mla_q_norm, m_mla_k_norm, m_mla_w_out, m_mlp_w_up, m_mlp_w_down, v_meta_tokens, v_ln_mix, v_ln_mlp, v_ssd_w_in, v_ssd_conv_w, v_ssd_conv_b, v_ssd_dt_bias, v_ssd_a_log, v_ssd_d, v_ssd_norm, v_ssd_w_out, v_mla_w_in, v_mla_q_a_norm, v_mla_w_q_b, v_mla_kv_a_norm, v_mla_w_kv_b, v_mla_q_norm, v_mla_k_norm, v_mla_w_out, v_mlp_w_up, v_mlp_w_down):
    w_sh = dict(meta_tokens=meta_tokens, ln_mix=ln_mix, ln_mlp=ln_mlp, ssd_w_in=ssd_w_in, ssd_conv_w=ssd_conv_w, ssd_conv_b=ssd_conv_b, ssd_dt_bias=ssd_dt_bias, ssd_a_log=ssd_a_log, ssd_d=ssd_d, ssd_norm=ssd_norm, ssd_w_out=ssd_w_out, mla_w_in=mla_w_in, mla_q_a_norm=mla_q_a_norm, mla_w_q_b=mla_w_q_b, mla_kv_a_norm=mla_kv_a_norm, mla_w_kv_b=mla_w_kv_b, mla_q_norm=mla_q_norm, mla_k_norm=mla_k_norm, mla_w_out=mla_w_out, mlp_w_up=mlp_w_up, mlp_w_down=mlp_w_down)
    m_sh = dict(meta_tokens=m_meta_tokens, ln_mix=m_ln_mix, ln_mlp=m_ln_mlp, ssd_w_in=m_ssd_w_in, ssd_conv_w=m_ssd_conv_w, ssd_conv_b=m_ssd_conv_b, ssd_dt_bias=m_ssd_dt_bias, ssd_a_log=m_ssd_a_log, ssd_d=m_ssd_d, ssd_norm=m_ssd_norm, ssd_w_out=m_ssd_w_out, mla_w_in=m_mla_w_in, mla_q_a_norm=m_mla_q_a_norm, mla_w_q_b=m_mla_w_q_b, mla_kv_a_norm=m_mla_kv_a_norm, mla_w_kv_b=m_mla_w_kv_b, mla_q_norm=m_mla_q_norm, mla_k_norm=m_mla_k_norm, mla_w_out=m_mla_w_out, mlp_w_up=m_mlp_w_up, mlp_w_down=m_mlp_w_down)
    v_sh = dict(meta_tokens=v_meta_tokens, ln_mix=v_ln_mix, ln_mlp=v_ln_mlp, ssd_w_in=v_ssd_w_in, ssd_conv_w=v_ssd_conv_w, ssd_conv_b=v_ssd_conv_b, ssd_dt_bias=v_ssd_dt_bias, ssd_a_log=v_ssd_a_log, ssd_d=v_ssd_d, ssd_norm=v_ssd_norm, ssd_w_out=v_ssd_w_out, mla_w_in=v_mla_w_in, mla_q_a_norm=v_mla_q_a_norm, mla_w_q_b=v_mla_w_q_b, mla_kv_a_norm=v_mla_kv_a_norm, mla_w_kv_b=v_mla_w_kv_b, mla_q_norm=v_mla_q_norm, mla_k_norm=v_mla_k_norm, mla_w_out=v_mla_w_out, mlp_w_up=v_mlp_w_up, mlp_w_down=v_mlp_w_down)

    cx, cy, cc = lax.axis_index("x"), lax.axis_index("y"), lax.axis_index("c")
    chip = 2 * cx + cy

    c_idx = cc.reshape(1).astype(jnp.int32)
    big_names = [n for n, _ in BIG]
    kinds = [k for _, k in BIG]
    shapes = [w_sh[n].shape for n in big_names]

    w = {n: w_sh[n] for n in SMALL_REPL}
    wholes = _gather_big([w_sh[n].astype(BF16) for n in big_names], kinds, "gather_big")
    for n, kind, s, f in zip(big_names, kinds, shapes, wholes):
        if kind == 'row':
            w[n] = f.reshape(s[0], 4 * s[1], s[2])
        elif kind == 'col':
            w[n] = f
        else:
            w[n] = jnp.concatenate([f[k] for k in range(4)], axis=-1)
    w.update(_gather_params(w_sh, SMALL_SHARDED, F32, cc, "gather_small"))

    loss_row, grad_x, grads = _local_step(x[0], loss_target[0], w)
    loss = lax.psum(jnp.sum(loss_row), ("x", "y", "c"))

    g_whole = []
    for n, kind, s in zip(big_names, kinds, shapes):
        if kind == 'row':
            g_whole.append(grads[n].reshape(s[0], 4, s[1], s[2]))
        elif kind == 'col':
            g_whole.append(grads[n])
        else:
            g_whole.append(jnp.stack([grads[n][..., k * s[2]:(k + 1) * s[2]] for k in range(4)]))
    recv = _rs_swap(g_whole, kinds, shapes, "rs_swap")
    parts = []
    for n, kind, s, g, r in zip(big_names, kinds, shapes, g_whole, recv):
        if kind == 'col':
            g3, r3 = g, r
        else:
            g3, r3 = g.reshape(-1, s[1], s[2]), r.reshape(-1, s[1] // 2, s[2])
        parts.append(_add_half(g3, r3, c_idx, f"rs_add_{n}").reshape(r.shape))
    got = _rs_exchange(parts, kinds, shapes, "rs_exchange")
    reduced = [_sum4(p, c_idx, f"rs_sum_{n}") for n, p in zip(big_names, got)]
    g_sh = dict(zip(big_names, _rs_share(reduced, "rs_share")))

    small_names = tuple(n for n, _ in SMALL_SHARDED) + SMALL_REPL
    sp = _pack([grads[n] for n in small_names], 8)
    srows = sp.shape[0]
    s_all = _sum8(_all_gather8(sp, "ar_small_gather").reshape(8, srows, LANES), "ar_small_sum")
    s_full = dict(zip(small_names, _unpack(s_all, [grads[n].shape for n in small_names])))
    for n, ax in SMALL_SHARDED:
        g_sh[n] = lax.dynamic_index_in_dim(_split4(s_full[n], ax), chip, axis=0, keepdims=False)
    for n in SMALL_REPL:
        g_sh[n] = s_full[n]

    delta, new_m, new_v = {}, {}, {}
    for n, s in zip(big_names, shapes):
        res = _adamw(*[t[n].reshape(-1, s[2]) for t in (w_sh, g_sh, m_sh, v_sh)], f"adamw_{n}")
        delta[n], new_m[n], new_v[n] = [r.reshape(s) for r in res]
    d_s, m_s, v_s = _adamw(*[_pack([t[n] for n in small_names], 8) for t in (w_sh, g_sh, m_sh, v_sh)],
                           "adamw_small")
    for dst, ps in ((delta, d_s), (new_m, m_s), (new_v, v_s)):
        dst.update(zip(small_names, _unpack(ps, [w_sh[n].shape for n in small_names])))

    return (loss, grad_x[None], *[g_sh[n] for n in ALL_NAMES], *[delta[n] for n in ALL_NAMES],
            *[new_m[n] for n in ALL_NAMES], *[new_v[n] for n in ALL_NAMES])
```

```python
import functools
import math

import jax
import jax.numpy as jnp
from jax import lax
from jax.experimental import pallas as pl
from jax.experimental.pallas import tpu as pltpu

F32 = jnp.float32
BF16 = jnp.bfloat16
MESH = pl.DeviceIdType.MESH
_NN = (((1,), (0,)), ((), ()))
_NT = (((1,), (1,)), ((), ()))
_TN = (((0,), (0,)), ((), ()))

D_MODEL = 1024
N_META = 16
EPS = 1e-6
SSD_D_INNER = 2048
SSD_HEADS = 32
SSD_HEAD_DIM = 64
SSD_GROUPS = 8
SSD_HPG = 4
SSD_STATE = 128
SSD_CONV = 4
CHUNK = 128
SSD_IN_DIM = 6176
SSD_IN_PAD = 6272
MLA_HEADS = 16
MLA_NOPE = 64
MLA_ROPE = 32
MLA_V = 64
MLA_QK = 96
MLA_Q_RANK = 384
MLA_KV_RANK = 256
HEAD_SLOT = 128
MLA_WIDE = MLA_HEADS * HEAD_SLOT
HEADS_PER_STEP = 2
LAT_PAD = 768
ROPE_THETA = 10000.0
D_FF = 4096
NPAD = CHUNK - N_META
ADAM_LR, ADAM_B1, ADAM_B2, ADAM_EPS, ADAM_WD, ADAM_STEP = 0.001, 0.9, 0.999, 1e-08, 0.01, 10
LANES = 1024
VMEM_LIMIT = 56 * 1024 * 1024


def _pick(n, cands):
    for c in cands:
        if n % c == 0:
            return c
    return n


def _cparams(**kw):
    return pltpu.CompilerParams(vmem_limit_bytes=VMEM_LIMIT, **kw)


def _mm(a, b, dims, *, name, out_dtype=F32, a_fn=None, epi=None, extras=(), stack=None):
    if dims == 'nn':
        (M, K), (K2, N) = a.shape, b.shape
    elif dims == 'nt':
        (M, K), (N, K2) = a.shape, b.shape
    else:
        (K, M), (K2, N) = a.shape, b.shape
    assert K == K2, (a.shape, b.shape, dims)
    if dims == 'tn':
        tm = _pick(M, (1024, 768, 512, 384, 256, 128))
        tn = _pick(N, (1024, 896, 768, 512, 384, 256, 128))
        tk = _pick(K, (1408, 1024, 512, 384, 256, 128))
    else:
        tm = _pick(M, (1408, 1024, 512, 384, 256, 128))
        tn = _pick(N, (512, 896, 768, 384, 256, 128))
        tk = _pick(K, (1024, 896, 768, 512, 384, 256, 128))
    nk = K // tk
    if dims == 'nn':
        a_spec = pl.BlockSpec((tm, tk), lambda i, j, k: (i, k))
        b_spec = pl.BlockSpec((tk, tn), lambda i, j, k: (k, j))
        dn = (((1,), (0,)), ((), ()))
    elif dims == 'nt':
        a_spec = pl.BlockSpec((tm, tk), lambda i, j, k: (i, k))
        b_spec = pl.BlockSpec((tn, tk), lambda i, j, k: (j, k))
        dn = (((1,), (1,)), ((), ()))
    else:
        a_spec = pl.BlockSpec((tk, tm), lambda i, j, k: (k, i))
        b_spec = pl.BlockSpec((tk, tn), lambda i, j, k: (k, j))
        dn = (((0,), (0,)), ((), ()))
    o_spec = pl.BlockSpec((tm, tn), lambda i, j, k: (i, j))
    n_ex = len(extras)
    out_shape = jax.ShapeDtypeStruct((M, N), out_dtype)
    out_spec, held, aliases = o_spec, (), {}
    if stack is not None:
        n_slabs, slab, buf = stack
        out_shape = jax.ShapeDtypeStruct((n_slabs, M, N), out_dtype)
        out_spec = pl.BlockSpec((None, tm, tn), lambda i, j, k: (slab, i, j))
        if buf is not None:
            held, aliases = (buf,), {2 + n_ex: 0}

    def body(a_ref, b_ref, *rest):
        ex_refs, o_ref, acc = rest[:n_ex], rest[n_ex + len(held)], rest[n_ex + len(held) + 1]
        k = pl.program_id(2)

        @pl.when(k == 0)
        def _():
            acc[...] = jnp.zeros_like(acc)

        av = a_ref[...]
        if a_fn is not None:
            av = a_fn(av)
        acc[...] += lax.dot_general(av.astype(BF16), b_ref[...].astype(BF16), dn,
                                    preferred_element_type=F32)

        @pl.when(k == nk - 1)
        def _():
            r = acc[...]
            if epi is not None:
                r = epi(r, *[e[...] for e in ex_refs])
            o_ref[...] = r.astype(out_dtype)

    return pl.pallas_call(
        body, name=name,
        out_shape=out_shape,
        grid=(M // tm, N // tn, nk),
        in_specs=[a_spec, b_spec] + [o_spec] * n_ex + [pl.BlockSpec(memory_space=pl.ANY)] * len(held),
        out_specs=out_spec,
        input_output_aliases=aliases,
        scratch_shapes=[pltpu.VMEM((tm, tn), F32)],
        compiler_params=_cparams(dimension_semantics=("parallel", "parallel", "arbitrary")),
    )(a, b, *extras, *held)


def _row_call(fn, rows, consts, out_rows, out_accs=(), *, n_rows, tile, name):
    n_r, n_c, n_o, n_a = len(rows), len(consts), len(out_rows), len(out_accs)
    steps = n_rows // tile

    def body(*refs):
        r_refs = refs[:n_r]
        c_refs = refs[n_r:n_r + n_c]
        o_refs = refs[n_r + n_c:n_r + n_c + n_o]
        a_refs = refs[n_r + n_c + n_o:]
        i = pl.program_id(0)
        res = fn(i, *[r[...] for r in r_refs], *[c[...] for c in c_refs])
        for o_ref, val in zip(o_refs, res[:n_o]):
            o_ref[...] = val.astype(o_ref.dtype)

        @pl.when(i == 0)
        def _():
            for a_ref in a_refs:
                a_ref[...] = jnp.zeros_like(a_ref)

        for a_ref, val in zip(a_refs, res[n_o:]):
            a_ref[...] += val

    in_specs = [pl.BlockSpec((tile, w), functools.partial(lambda i, cb: (i, cb), cb=cb))
                for (_, w, cb) in rows]
    in_specs += [pl.BlockSpec(c.shape, lambda i: (0, 0)) for c in consts]
    out_specs = [pl.BlockSpec((tile, c), lambda i: (i, 0)) for (c, _) in out_rows]
    out_specs += [pl.BlockSpec(s, lambda i: (0, 0)) for s in out_accs]
    out_shape = [jax.ShapeDtypeStruct((n_rows, c), dt) for (c, dt) in out_rows]
    out_shape += [jax.ShapeDtypeStruct(s, F32) for s in out_accs]
    return pl.pallas_call(
        body, name=name, out_shape=out_shape, grid=(steps,),
        in_specs=in_specs, out_specs=out_specs,
        compiler_params=_cparams(dimension_semantics=("arbitrary",)),
    )(*[r[0] for r in rows], *consts)


def _row_mask(i, tile):
    r = i * tile + lax.broadcasted_iota(jnp.int32, (tile, 1), 0)
    return (r >= NPAD).astype(F32)


def _rms(x, g):
    return x * lax.rsqrt(jnp.mean(x * x, axis=-1, keepdims=True) + EPS) * g


def _silu(x):
    return x * (0.5 * jnp.tanh(0.5 * x) + 0.5)


def _softplus(x):
    return jnp.maximum(x, 0.0) + jnp.log(1.0 + jnp.exp(-jnp.abs(x)))


def _rms_fwd(h, g, name):
    lp = h.shape[0]
    return _row_call(lambda i, hv, gv: (_rms(hv, gv),), [(h, D_MODEL, 0)], [g],
                     [(D_MODEL, BF16)], n_rows=lp, tile=_pick(lp, (384, 256, 128)), name=name)[0]


def _rms_bwd(h, g, d_hn, d_res, name):
    lp = h.shape[0]
    tile = _pick(lp, (384, 256, 128))

    def fn(i, hv, dv, rv, gv):
        _, vjp = jax.vjp(_rms, hv, gv)
        dh, dg = vjp(dv)
        return (rv + dh) * _row_mask(i, tile), dg

    return _row_call(fn, [(h, D_MODEL, 0), (d_hn, D_MODEL, 0), (d_res, D_MODEL, 0)], [g],
                     [(D_MODEL, F32)], [(1, D_MODEL)], n_rows=lp, tile=tile, name=name)


@functools.partial(jax.custom_vjp, nondiff_argnums=(1,))
def _roll_rows(x, s):
    return pltpu.roll(x, s, 0)


def _roll_rows_fwd(x, s):
    return pltpu.roll(x, s, 0), None


def _roll_rows_bwd(s, _, ct):
    return (pltpu.roll(ct, (ct.shape[0] - s) % ct.shape[0], 0),)


_roll_rows.defvjp(_roll_rows_fwd, _roll_rows_bwd)


def _conv_silu(cur, halo, w_rows, b):
    full = jnp.concatenate([halo, cur], axis=0)
    acc = cur * w_rows[SSD_CONV - 1] + b
    for k in range(SSD_CONV - 1):
        acc = acc + _roll_rows(full, SSD_CONV - 1 - k)[8:] * w_rows[k]
    return _silu(acc)


def _split3(v):
    hi = v.astype(BF16)
    r1 = v - hi.astype(F32)
    mid = r1.astype(BF16)
    lo = (r1 - mid.astype(F32)).astype(BF16)
    return hi, mid, lo


def _select_right(v, sel, dn):
    return sum(lax.dot_general(p, sel, dn, preferred_element_type=F32) for p in _split3(v))


@jax.custom_vjp
def _expand_heads(v, e_mat):
    return _select_right(v, e_mat, _NN)


def _expand_heads_fwd(v, e_mat):
    return _select_right(v, e_mat, _NN), e_mat


def _expand_heads_bwd(e_mat, ct):
    return _select_right(ct, e_mat, _NT), jnp.zeros_like(e_mat)


_expand_heads.defvjp(_expand_heads_fwd, _expand_heads_bwd)


@jax.custom_vjp
def _cumsum_rows(a, tri):
    return sum(lax.dot_general(tri, p, _NN, preferred_element_type=F32) for p in _split3(a))


def _cumsum_rows_fwd(a, tri):
    return _cumsum_rows(a, tri), tri


def _cumsum_rows_bwd(tri, ct):
    return (sum(lax.dot_general(tri, p, _TN, preferred_element_type=F32) for p in _split3(ct)),
            jnp.zeros_like(tri))


_cumsum_rows.defvjp(_cumsum_rows_fwd, _cumsum_rows_bwd)


def _ssd_chunk(mask, z, xs_pre, bc_pre, halo_x, halo_bc, dt_pre, st, cwx0, cwx1, cwx2, cwx3,
               cwb0, cwb1, cwb2, cwb3, cb_x, cb_bc, dtb, alog, dsk, ng):
    L = CHUNK
    lane_h = lax.broadcasted_iota(jnp.int32, (1, 128), 1)
    head_ok = (lane_h < SSD_HEADS).astype(F32)
    e_mat = (lax.broadcasted_iota(jnp.int32, (128, SSD_D_INNER), 1) // SSD_HEAD_DIM
             == lax.broadcasted_iota(jnp.int32, (128, SSD_D_INNER), 0)).astype(BF16)
    ri = lax.broadcasted_iota(jnp.int32, (L, L), 0)
    ci = lax.broadcasted_iota(jnp.int32, (L, L), 1)
    causal = ri >= ci

    xs = _conv_silu(xs_pre, halo_x, (cwx0, cwx1, cwx2, cwx3), cb_x) * mask
    bc = _conv_silu(bc_pre, halo_bc, (cwb0, cwb1, cwb2, cwb3), cb_bc) * mask
    dt = _softplus(dt_pre + dtb) * mask * head_ok
    a_dt = dt * (-jnp.exp(alog))
    a_cs = _cumsum_rows(a_dt, causal.astype(BF16))
    a_cs_t = a_cs.T
    row8 = lax.broadcasted_iota(jnp.int32, (8, 128), 0)
    last8 = jnp.where(row8 == 0, jnp.sum(a_dt, axis=0, keepdims=True), 0.0)
    dsk8 = jnp.where(row8 == 0, dsk, 0.0)
    wide = _expand_heads(jnp.concatenate([dt, a_cs, last8, dsk8], axis=0), e_mat)
    dt_e, acs_e = wide[0:L], wide[L:2 * L]
    last_e = jnp.sum(wide[2 * L:2 * L + 8], axis=0, keepdims=True)
    d_e = jnp.sum(wide[2 * L + 8:2 * L + 16], axis=0, keepdims=True)
    xdt = xs * dt_e
    dte_e = jnp.exp(last_e - acs_e)
    dfs_e = jnp.exp(acs_e)
    cd_e = jnp.exp(last_e)
    sub_h = lax.broadcasted_iota(jnp.int32, (128, L), 0)
    lane_hl = lax.broadcasted_iota(jnp.int32, (L, 128), 1)
    lane_g = lax.broadcasted_iota(jnp.int32, (1, SSD_HPG * SSD_HEAD_DIM), 1) // SSD_HEAD_DIM

    ys, new_st = [], []
    for g in range(SSD_GROUPS):
        b_g = bc[:, g * 128:(g + 1) * 128].astype(BF16)
        c_g = bc[:, 1024 + g * 128:1024 + (g + 1) * 128].astype(BF16)
        gs = slice(g * 256, (g + 1) * 256)
        xdt_g = xdt[:, gs]
        cb = lax.dot_general(c_g, b_g, (((1,), (1,)), ((), ())), preferred_element_type=F32)
        st_g = st[g * 128:(g + 1) * 128, :]
        y_g = lax.dot_general(c_g, st_g.astype(BF16), (((1,), (0,)), ((), ())),
                              preferred_element_type=F32) * dfs_e[:, gs]
        for j in range(SSD_HPG):
            h = g * SSD_HPG + j
            col = jnp.sum(jnp.where(lane_hl == h, a_cs, 0.0), axis=1, keepdims=True)
            row = jnp.sum(jnp.where(sub_h == h, a_cs_t, 0.0), axis=0, keepdims=True)
            dec = jnp.where(causal, jnp.exp(jnp.where(causal, col - row, 0.0)), 0.0)
            m_h = (cb * dec).astype(BF16)
            x_h = jnp.where(lane_g == j, xdt_g, 0.0).astype(BF16)
            y_g = y_g + lax.dot_general(m_h, x_h, (((1,), (0,)), ((), ())),
                                        preferred_element_type=F32)
        s_new = lax.dot_general(b_g, (xdt_g * dte_e[:, gs]).astype(BF16), (((0,), (0,)), ((), ())),
                                preferred_element_type=F32)
        new_st.append(st_g * cd_e[:, gs] + s_new)
        ys.append(y_g)
    y = jnp.concatenate(ys, axis=1) + xs * d_e
    gg = y * _silu(z)
    outs = []
    for g in range(SSD_GROUPS):
        sl = gg[:, g * 256:(g + 1) * 256]
        outs.append(sl * lax.rsqrt(jnp.mean(sl * sl, axis=-1, keepdims=True) + EPS))
    out = jnp.concatenate(outs, axis=1) * ng
    return out, jnp.concatenate(new_st, axis=0)


def _ssd_consts(conv_w, conv_b, dtb, alog, dsk, ng):
    return [conv_w, conv_b, dtb, alog, dsk, ng]


def _ssd_param_vals(cw_ref, cb_ref, dtb_ref, alog_ref, dsk_ref, ng_ref):
    cwx = [cw_ref[k:k + 1, 0:SSD_D_INNER] for k in range(SSD_CONV)]
    cwb = [cw_ref[k:k + 1, SSD_D_INNER:2 * SSD_D_INNER] for k in range(SSD_CONV)]
    return (*cwx, *cwb, cb_ref[:, 0:SSD_D_INNER], cb_ref[:, SSD_D_INNER:2 * SSD_D_INNER],
            dtb_ref[...], alog_ref[...], dsk_ref[...], ng_ref[...])


def _ssd_in_specs(rev, nc):
    def cidx(i):
        return (nc - 1 - i) if rev else i

    def halo(cb):
        return pl.BlockSpec((8, SSD_D_INNER), lambda i: (jnp.maximum(16 * cidx(i) - 1, 0), cb))

    return [
        pl.BlockSpec((CHUNK, SSD_D_INNER), lambda i: (cidx(i), 0)),
        pl.BlockSpec((CHUNK, SSD_D_INNER), lambda i: (cidx(i), 1)),
        pl.BlockSpec((CHUNK, SSD_D_INNER), lambda i: (cidx(i), 2)),
        halo(1), halo(2),
        pl.BlockSpec((CHUNK, 128), lambda i: (cidx(i), 48)),
    ]


def _ssd_fwd(zxd, consts, name):
    lp = zxd.shape[0]
    nc = lp // CHUNK

    def body(z_ref, xs_ref, bc_ref, hx_ref, hb_ref, dt_ref, cw_ref, cb_ref, dtb_ref, alog_ref,
             dsk_ref, ng_ref, y_ref, st_ref, state):
        c = pl.program_id(0)

        @pl.when(c == 0)
        def _():
            state[...] = jnp.zeros_like(state)

        live = (c > 0).astype(F32)
        st_ref[0] = state[...]
        out, st_new = _ssd_chunk(
            _row_mask(c, CHUNK), z_ref[...], xs_ref[...], bc_ref[...], hx_ref[...] * live,
            hb_ref[...] * live, dt_ref[...], state[...],
            *_ssd_param_vals(cw_ref, cb_ref, dtb_ref, alog_ref, dsk_ref, ng_ref))
        y_ref[...] = out.astype(y_ref.dtype)
        state[...] = st_new

    return pl.pallas_call(
        body, name=name,
        out_shape=[jax.ShapeDtypeStruct((lp, SSD_D_INNER), BF16),
                   jax.ShapeDtypeStruct((nc, SSD_GROUPS * SSD_STATE, 256), F32)],
        grid=(nc,),
        in_specs=_ssd_in_specs(False, nc) + [pl.BlockSpec(c.shape, lambda i: (0, 0)) for c in consts],
        out_specs=[pl.BlockSpec((CHUNK, SSD_D_INNER), lambda i: (i, 0)),
                   pl.BlockSpec((1, SSD_GROUPS * SSD_STATE, 256), lambda i: (i, 0, 0))],
        scratch_shapes=[pltpu.VMEM((SSD_GROUPS * SSD_STATE, 256), F32)],
        compiler_params=_cparams(dimension_semantics=("arbitrary",)),
    )(zxd, zxd, zxd, zxd, zxd, zxd, *consts)


def _ssd_bwd(zxd, states, d_y, consts, name):
    lp = zxd.shape[0]
    nc = lp // CHUNK

    def body(z_ref, xs_ref, bc_ref, hx_ref, hb_ref, dt_ref, st_ref, dy_ref, cw_ref, cb_ref, dtb_ref,
             alog_ref, dsk_ref, ng_ref, dz_ref, dcw_ref, dcb_ref, ddtb_ref, dalog_ref, ddsk_ref,
             dng_ref, d_state, d_hx, d_hb):
        i = pl.program_id(0)
        c = nc - 1 - i

        @pl.when(i == 0)
        def _():
            d_state[...] = jnp.zeros_like(d_state)
            d_hx[...] = jnp.zeros_like(d_hx)
            d_hb[...] = jnp.zeros_like(d_hb)
            for r in (dcw_ref, dcb_ref, ddtb_ref, dalog_ref, ddsk_ref, dng_ref):
                r[...] = jnp.zeros_like(r)

        live = (c > 0).astype(F32)
        fn = functools.partial(_ssd_chunk, _row_mask(c, CHUNK))
        prim = (z_ref[...], xs_ref[...], bc_ref[...], hx_ref[...] * live, hb_ref[...] * live,
                dt_ref[...], st_ref[0],
                *_ssd_param_vals(cw_ref, cb_ref, dtb_ref, alog_ref, dsk_ref, ng_ref))
        _, vjp = jax.vjp(fn, *prim)
        (d_z, d_xs, d_bc, g_hx, g_hb, d_dt, g_st, *d_par) = vjp((dy_ref[...], d_state[...]))
        zeros = jnp.zeros((CHUNK - 8, SSD_D_INNER), F32)
        d_xs = d_xs + jnp.concatenate([zeros, d_hx[...]], axis=0)
        d_bc = d_bc + jnp.concatenate([zeros, d_hb[...]], axis=0)
        dz_ref[:, 0:SSD_D_INNER] = d_z.astype(dz_ref.dtype)
        dz_ref[:, SSD_D_INNER:2 * SSD_D_INNER] = d_xs.astype(dz_ref.dtype)
        dz_ref[:, 2 * SSD_D_INNER:3 * SSD_D_INNER] = d_bc.astype(dz_ref.dtype)
        dz_ref[:, 3 * SSD_D_INNER:] = d_dt.astype(dz_ref.dtype)
        d_state[...] = g_st
        d_hx[...] = g_hx * live
        d_hb[...] = g_hb * live
        for k in range(SSD_CONV):
            dcw_ref[k:k + 1, 0:SSD_D_INNER] += d_par[k]
            dcw_ref[k:k + 1, SSD_D_INNER:2 * SSD_D_INNER] += d_par[SSD_CONV + k]
        dcb_ref[:, 0:SSD_D_INNER] += d_par[8]
        dcb_ref[:, SSD_D_INNER:2 * SSD_D_INNER] += d_par[9]
        ddtb_ref[...] += d_par[10]
        dalog_ref[...] += d_par[11]
        ddsk_ref[...] += d_par[12]
        dng_ref[...] += d_par[13]

    const_specs = [pl.BlockSpec(c.shape, lambda i: (0, 0)) for c in consts]
    return pl.pallas_call(
        body, name=name,
        out_shape=[jax.ShapeDtypeStruct((lp, SSD_IN_PAD), BF16)]
        + [jax.ShapeDtypeStruct(c.shape, F32) for c in consts],
        grid=(nc,),
        in_specs=_ssd_in_specs(True, nc)
        + [pl.BlockSpec((1, SSD_GROUPS * SSD_STATE, 256), lambda i: (nc - 1 - i, 0, 0)),
           pl.BlockSpec((CHUNK, SSD_D_INNER), lambda i: (nc - 1 - i, 0))] + const_specs,
        out_specs=[pl.BlockSpec((CHUNK, SSD_IN_PAD), lambda i: (nc - 1 - i, 0))] + const_specs,
        scratch_shapes=[pltpu.VMEM((SSD_GROUPS * SSD_STATE, 256), F32),
                        pltpu.VMEM((8, SSD_D_INNER), F32), pltpu.VMEM((8, SSD_D_INNER), F32)],
        compiler_params=_cparams(dimension_semantics=("arbitrary",)),
    )(zxd, zxd, zxd, zxd, zxd, zxd, states, d_y, *consts)


@jax.custom_vjp
def _rot_half(x):
    lane = lax.broadcasted_iota(jnp.int32, x.shape, 1)
    lo = (lane >= MLA_NOPE) & (lane < MLA_NOPE + MLA_ROPE // 2)
    hi = (lane >= MLA_NOPE + MLA_ROPE // 2) & (lane < MLA_QK)
    down = pltpu.roll(x, HEAD_SLOT - MLA_ROPE // 2, 1)
    up = pltpu.roll(x, MLA_ROPE // 2, 1)
    return jnp.where(lo, -down, jnp.where(hi, up, 0.0))


def _rot_half_fwd(x):
    return _rot_half(x), None


def _rot_half_bwd(_, ct):
    return (-_rot_half(ct),)


_rot_half.defvjp(_rot_half_fwd, _rot_half_bwd)


def _head_norm_rope(t, gain, cos, sin):
    n = t * lax.rsqrt(jnp.sum(t * t, axis=-1, keepdims=True) * (1.0 / MLA_QK) + EPS) * gain
    return n * cos + _rot_half(n) * sin


def _qk_prep(q_raw, kn_raw, kpe, cos, sin, qg, kg):
    qs, ks = [], []
    for h in range(MLA_HEADS):
        sl = slice(h * HEAD_SLOT, (h + 1) * HEAD_SLOT)
        qs.append(_head_norm_rope(q_raw[:, sl], qg, cos, sin))
        ks.append(_head_norm_rope(kn_raw[:, sl] + kpe, kg, cos, sin))
    return jnp.concatenate(qs, axis=1), jnp.concatenate(ks, axis=1)


def _lat_norm(kv_lat, q_lat, kvg, qg):
    return _rms(kv_lat, kvg), _rms(q_lat, qg)


_NEG = -1e30
_SCALE = MLA_QK ** -0.5


STRIP = 128
_EXP2_SCALE = _SCALE * math.log2(math.e)


def _strip_mask(kind, blk, c, t):
    if kind is None:
        return None
    kpos = blk * t + c * STRIP + lax.broadcasted_iota(jnp.int32, (1, STRIP), 1)
    if kind == 'keys':
        return kpos >= NPAD
    qpos = blk * t + lax.broadcasted_iota(jnp.int32, (t, 1), 0)
    return (kpos <= qpos) & ((kpos >= NPAD) | (kpos == qpos))


def _attn_fwd(q, k, v, name):
    lp = q.shape[0]
    t = _pick(lp, (384, 256, 128))
    nb = lp // t
    hp = HEADS_PER_STEP
    wide = hp * HEAD_SLOT
    heads = [slice(a * HEAD_SLOT, (a + 1) * HEAD_SLOT) for a in range(hp)]

    def body(q_ref, k_ref, v_ref, o_ref, lse_ref):
        qi = pl.program_id(1)

        def update(a, ki, carry, mask):
            rows = pl.ds(pl.multiple_of(ki * t, t), t)
            m, l, acc = carry
            s = lax.dot_general(q_ref[:, heads[a]], k_ref[rows, heads[a]], _NT, preferred_element_type=F32)
            if mask is not None:
                s = jnp.where(mask, s, _NEG)
            m_new = jnp.maximum(m, jnp.max(s, axis=-1, keepdims=True))
            alpha = jnp.exp2((m - m_new) * _EXP2_SCALE)
            p = jnp.exp2((s - m_new) * _EXP2_SCALE)
            l = alpha * l + jnp.sum(p, axis=-1, keepdims=True)
            acc = alpha * acc + lax.dot_general(p.astype(BF16), v_ref[rows, heads[a]], _NN,
                                                preferred_element_type=F32)
            return m_new, l, acc

        def step(ki, carry, mask):
            return tuple(update(a, ki, carry[a], mask) for a in range(hp))

        init = (jnp.full((t, 1), _NEG, F32), jnp.zeros((t, 1), F32), jnp.zeros((t, HEAD_SLOT), F32))
        key_ok = lax.broadcasted_iota(jnp.int32, (1, t), 1) >= NPAD
        carry = lax.cond(qi > 0, lambda c: step(0, c, key_ok), lambda c: c, (init,) * hp)
        carry = lax.fori_loop(1, qi, lambda ki, c: step(ki, c, None), carry)
        qpos = lax.broadcasted_iota(jnp.int32, (t, t), 0)
        kpos = lax.broadcasted_iota(jnp.int32, (t, t), 1)
        diag = (kpos <= qpos) & ((qi * t + kpos >= NPAD) | (kpos == qpos))
        for a in range(hp):
            m, l, acc = update(a, qi, carry[a], diag)
            o_ref[:, heads[a]] = acc / l * _row_mask(qi, t)
            lse_ref[a] = m * _SCALE + jnp.log(l)

    qspec = pl.BlockSpec((t, wide), lambda g, i: (i, g))
    kspec = pl.BlockSpec((lp, wide), lambda g, i: (0, g))
    return pl.pallas_call(
        body, name=name,
        out_shape=[jax.ShapeDtypeStruct((lp, MLA_WIDE), F32),
                   jax.ShapeDtypeStruct((MLA_HEADS, lp, 1), F32)],
        grid=(MLA_HEADS // hp, nb),
        in_specs=[qspec, kspec, kspec],
        out_specs=[qspec, pl.BlockSpec((hp, t, 1), lambda g, i: (g, i, 0))],
        compiler_params=_cparams(dimension_semantics=("parallel", "arbitrary")),
    )(q, k, v)


def _attn_delta(do, o, name):
    lp = do.shape[0]
    t = _pick(lp, (384, 256, 128))

    def body(do_ref, o_ref, dob_ref, delta_ref):
        dob_ref[...] = do_ref[...].astype(BF16)
        for h in range(MLA_HEADS):
            sl = slice(h * HEAD_SLOT, (h + 1) * HEAD_SLOT)
            delta_ref[h] = jnp.sum(do_ref[:, sl] * o_ref[:, sl], axis=-1, keepdims=True)

    spec = pl.BlockSpec((t, MLA_WIDE), lambda i: (i, 0))
    return pl.pallas_call(
        body, name=name,
        out_shape=[jax.ShapeDtypeStruct((lp, MLA_WIDE), BF16), jax.ShapeDtypeStruct((MLA_HEADS, lp, 1), F32)],
        grid=(lp // t,), in_specs=[spec, spec],
        out_specs=[spec, pl.BlockSpec((MLA_HEADS, t, 1), lambda i: (0, i, 0))],
        compiler_params=_cparams(dimension_semantics=("parallel",)),
    )(do, o)


def _attn_bwd(q, k, v, do, lse, delta, name):
    lp = q.shape[0]
    t = _pick(lp, (384, 256, 128))
    nb = lp // t
    ns = t // STRIP
    hp = HEADS_PER_STEP
    wide = hp * HEAD_SLOT
    heads = [slice(a * HEAD_SLOT, (a + 1) * HEAD_SLOT) for a in range(hp)]
    log2e = math.log2(math.e)

    def body(q_ref, k_ref, v_ref, do_ref, lse_ref, delta_ref, dq_ref, dk_ref, dv_ref,
             s_scr, dp_scr, p_scr, ds_scr):
        kj = pl.program_id(1)

        @pl.when(kj == 0)
        def _():
            dq_ref[...] = jnp.zeros_like(dq_ref)

        dk_ref[...] = jnp.zeros_like(dk_ref)
        dv_ref[...] = jnp.zeros_like(dv_ref)

        def tile(qi, kind):
            rows = pl.ds(pl.multiple_of(qi * t, t), t)
            for a in range(hp):
                qb, dob = q_ref[rows, heads[a]], do_ref[rows, heads[a]]
                kb, vb = k_ref[:, heads[a]], v_ref[:, heads[a]]
                s_scr[a] = lax.dot_general(qb, kb, _NT, preferred_element_type=F32)
                dp_scr[a] = lax.dot_general(dob, vb, _NT, preferred_element_type=F32)
                lse2 = lse_ref[a, rows, :] * log2e
                delta = delta_ref[a, rows, :]
                for c in range(ns):
                    cs = slice(c * STRIP, (c + 1) * STRIP)
                    pc = jnp.exp2(s_scr[a, :, cs] * _EXP2_SCALE - lse2)
                    pc = jnp.where(_strip_mask(kind, kj, c, t), pc, 0.0)
                    p_scr[a, :, cs] = pc.astype(BF16)
                    ds_scr[a, :, cs] = (pc * (dp_scr[a, :, cs] - delta)).astype(BF16)
                dq_ref[rows, heads[a]] += lax.dot_general(ds_scr[a], kb, _NN,
                                                          preferred_element_type=F32) * _SCALE
                dv_ref[:, heads[a]] += lax.dot_general(p_scr[a], dob, _TN, preferred_element_type=F32)
                dk_ref[:, heads[a]] += lax.dot_general(ds_scr[a], qb, _TN, preferred_element_type=F32)

        tile(kj, 'diag')

        def below(qi, carry):
            tile(qi, 'keys')
            return carry

        lax.fori_loop(kj + 1, nb, below, 0)
        dk_ref[...] = dk_ref[...] * _SCALE

    whole = pl.BlockSpec((lp, wide), lambda g, j: (0, g))
    kspec = pl.BlockSpec((t, wide), lambda g, j: (j, g))
    stat = pl.BlockSpec((hp, lp, 1), lambda g, j: (g, 0, 0))
    return pl.pallas_call(
        body, name=name,
        out_shape=[jax.ShapeDtypeStruct((lp, MLA_WIDE), F32)] * 3,
        grid=(MLA_HEADS // hp, nb),
        in_specs=[whole, kspec, kspec, whole, stat, stat],
        out_specs=[whole, kspec, kspec],
        scratch_shapes=[pltpu.VMEM((hp, t, t), F32), pltpu.VMEM((hp, t, t), F32),
                        pltpu.VMEM((hp, t, t), BF16), pltpu.VMEM((hp, t, t), BF16)],
        compiler_params=_cparams(dimension_semantics=("parallel", "arbitrary")),
    )(q, k, v, do, lse, delta)


def _rope_tables(lp):
    inv = 1.0 / (ROPE_THETA ** (jnp.arange(0, MLA_ROPE, 2, dtype=F32) / MLA_ROPE))
    pos = jnp.maximum(jnp.arange(lp, dtype=jnp.int32) - NPAD, 0).astype(F32)
    ang = pos[:, None] * inv[None, :]
    cos, sin = jnp.cos(ang), jnp.sin(ang)
    z32 = jnp.zeros((lp, HEAD_SLOT - MLA_QK), F32)
    cos_t = jnp.concatenate([jnp.ones((lp, MLA_NOPE), F32), cos, cos, z32], axis=1)
    sin_t = jnp.concatenate([jnp.zeros((lp, MLA_NOPE), F32), sin, sin, z32], axis=1)
    return cos_t, sin_t


def _loss_head(h, target, name):
    lp = h.shape[0]

    def body(h_ref, t_ref, d_ref, loss_ref):
        i = pl.program_id(0)

        @pl.when(i == 0)
        def _():
            d_ref[...] = jnp.zeros_like(d_ref)
            loss_ref[...] = jnp.zeros_like(loss_ref)

        @pl.when(i > 0)
        def _():
            err = h_ref[...] - t_ref[...]
            d_ref[...] = err * (1.0 / D_MODEL)
            loss_ref[...] += jnp.sum(err * err, axis=0, keepdims=True) * (0.5 / D_MODEL)

    return pl.pallas_call(
        body, name=name,
        out_shape=[jax.ShapeDtypeStruct((lp, D_MODEL), F32), jax.ShapeDtypeStruct((1, D_MODEL), F32)],
        grid=(lp // CHUNK,),
        in_specs=[pl.BlockSpec((CHUNK, D_MODEL), lambda i: (i, 0)),
                  pl.BlockSpec((CHUNK, D_MODEL), lambda i: (jnp.maximum(i - 1, 0), 0))],
        out_specs=[pl.BlockSpec((CHUNK, D_MODEL), lambda i: (i, 0)),
                   pl.BlockSpec((1, D_MODEL), lambda i: (0, 0))],
        compiler_params=_cparams(dimension_semantics=("arbitrary",)),
    )(h, target)


def _pad_cols(w, n):
    return jnp.pad(w, [(0, 0)] * (w.ndim - 1) + [(0, n - w.shape[-1])])


def _prep_weights(w):
    p = {}
    p['ssd_in'] = [_pad_cols(w['ssd_w_in'][j], SSD_IN_PAD).astype(BF16) for j in range(2)]
    p['ssd_out'] = [w['ssd_w_out'][j].astype(BF16) for j in range(2)]
    p['mla_in'], p['mla_qb'], p['mla_kvb'], p['mla_out'] = [], [], [], []
    for j in range(2):
        wi = w['mla_w_in'][j]
        kpe = jnp.pad(wi[:, MLA_Q_RANK + MLA_KV_RANK:], ((0, 0), (MLA_NOPE, HEAD_SLOT - MLA_QK)))
        p['mla_in'].append(jnp.concatenate(
            [wi[:, MLA_Q_RANK:MLA_Q_RANK + MLA_KV_RANK], kpe, wi[:, :MLA_Q_RANK]], axis=1).astype(BF16))
        qb = w['mla_w_q_b'][j].reshape(MLA_Q_RANK, MLA_HEADS, MLA_QK)
        p['mla_qb'].append(_pad_cols(qb, HEAD_SLOT).reshape(MLA_Q_RANK, MLA_WIDE).astype(BF16))
        kvb = w['mla_w_kv_b'][j].reshape(MLA_KV_RANK, MLA_HEADS, MLA_NOPE + MLA_V)
        kn = _pad_cols(kvb[:, :, :MLA_NOPE], HEAD_SLOT).reshape(MLA_KV_RANK, MLA_WIDE)
        vv = _pad_cols(kvb[:, :, MLA_NOPE:], HEAD_SLOT).reshape(MLA_KV_RANK, MLA_WIDE)
        p['mla_kvb'].append(jnp.concatenate([kn, vv], axis=1).astype(BF16))
        wo = w['mla_w_out'][j].reshape(MLA_HEADS, MLA_V, D_MODEL)
        p['mla_out'].append(jnp.pad(wo, ((0, 0), (0, HEAD_SLOT - MLA_V), (0, 0)))
                            .reshape(MLA_WIDE, D_MODEL).astype(BF16))
    p['up'] = [w['mlp_w_up'][i].astype(BF16) for i in range(4)]
    p['down'] = [w['mlp_w_down'][i].astype(BF16) for i in range(4)]
    return p


def _pad128(v):
    return _pad_cols(v.reshape(1, -1), 128)


def _sqrelu(u):
    r = jnp.maximum(u, 0.0)
    return r * r


def _local_step(x, target, w):
    seq = x.shape[0]
    lp = NPAD + N_META + seq
    p = _prep_weights(w)
    h = jnp.concatenate([jnp.zeros((NPAD, D_MODEL), F32), w['meta_tokens'], x], axis=0)
    cos_t, sin_t = _rope_tables(lp)
    rt = _pick(lp, (384, 256, 128))
    saved = []
    for i in range(4):
        j = i // 2
        s = {'h0': h}
        g_mix = w['ln_mix'][i].reshape(1, -1)
        hn = _rms_fwd(h, g_mix, f"rms_mix_f{i}")
        s['hn'] = hn
        if i % 2 == 0:
            zxd = _mm(hn, p['ssd_in'][j], 'nn', name=f"ssd_in_f{i}")
            consts = _ssd_consts(w['ssd_conv_w'][j], w['ssd_conv_b'][j].reshape(1, -1),
                                 _pad128(w['ssd_dt_bias'][j]), _pad128(w['ssd_a_log'][j]),
                                 _pad128(w['ssd_d'][j]), w['ssd_norm'][j].reshape(1, -1))
            yg, states = _ssd_fwd(zxd, consts, f"ssd_core_f{i}")
            s.update(zxd=zxd, consts=consts, yg=yg, states=states)
            h = _mm(yg, p['ssd_out'][j], 'nn', name=f"ssd_out_f{i}", epi=lambda r, hv: hv + r, extras=(h,))
        else:
            lat = _mm(hn, p['mla_in'][j], 'nn', name=f"mla_in_f{i}")
            kvg = w['mla_kv_a_norm'][j].reshape(1, -1)
            qag = w['mla_q_a_norm'][j].reshape(1, -1)
            kvn, qn = _row_call(lambda _, a, b, c, d: _lat_norm(a, b, c, d),
                                [(lat, MLA_KV_RANK, 0), (lat, MLA_Q_RANK, 1)], [kvg, qag],
                                [(MLA_KV_RANK, BF16), (MLA_Q_RANK, BF16)], n_rows=lp, tile=rt,
                                name=f"mla_latnorm_f{i}")
            q_raw = _mm(qn, p['mla_qb'][j], 'nn', name=f"mla_qb_f{i}")
            kv_raw = _mm(kvn, p['mla_kvb'][j], 'nn', name=f"mla_kvb_f{i}")
            qg = _pad_cols(w['mla_q_norm'][j].reshape(1, -1), HEAD_SLOT)
            kg = _pad_cols(w['mla_k_norm'][j].reshape(1, -1), HEAD_SLOT)

            def prep_fwd(_, qr, kn, kpe, vv, cs, sn, qgv, kgv):
                qq, kk = _qk_prep(qr, kn, kpe, cs, sn, qgv, kgv)
                return qq, kk, vv

            q, k, v = _row_call(prep_fwd,
                                [(q_raw, MLA_WIDE, 0), (kv_raw, MLA_WIDE, 0), (lat, HEAD_SLOT, 2),
                                 (kv_raw, MLA_WIDE, 1), (cos_t, HEAD_SLOT, 0), (sin_t, HEAD_SLOT, 0)],
                                [qg, kg], [(MLA_WIDE, BF16)] * 3, n_rows=lp, tile=rt,
                                name=f"mla_qkprep_f{i}")
            o, lse = _attn_fwd(q, k, v, f"mla_attn_f{i}")
            s.update(lat=lat, kvg=kvg, qag=qag, kvn=kvn, qn=qn, q_raw=q_raw, kv_raw=kv_raw, qg=qg, kg=kg,
                     q=q, k=k, v=v, o=o, lse=lse)
            h = _mm(o, p['mla_out'][j], 'nn', name=f"mla_out_f{i}", epi=lambda r, hv: hv + r, extras=(h,))
        s['h1'] = h
        g_mlp = w['ln_mlp'][i].reshape(1, -1)
        hn2 = _rms_fwd(h, g_mlp, f"rms_mlp_f{i}")
        u = _mm(hn2, p['up'][i], 'nn', name=f"mlp_up_f{i}")
        h = _mm(u, p['down'][i], 'nn', name=f"mlp_down_f{i}", a_fn=_sqrelu,
                epi=lambda r, hv: hv + r, extras=(h,))
        s.update(hn2=hn2, u=u, g_mix=g_mix, g_mlp=g_mlp)
        saved.append(s)

    dh, loss_row = _loss_head(h, target, "loss_head")

    g = {k_: [None] * (4 if k_ in ('ln_mix', 'ln_mlp', 'mlp_w_up', 'mlp_w_down') else 2)
         for k_ in w if k_ != 'meta_tokens'}
    stacked = {}

    def dw_into(nm, slab, a, b, **kw):
        stacked[nm] = _mm(a, b, 'tn', stack=(len(g[nm]), slab, stacked.get(nm)), **kw)

    for i in reversed(range(4)):
        j = i // 2
        s = saved[i]
        dw_into('mlp_w_down', i, s['u'], dh, name=f"mlp_down_dw{i}", a_fn=_sqrelu)
        du = _mm(dh, p['down'][i], 'nt', name=f"mlp_down_dx{i}", out_dtype=BF16,
                 epi=lambda r, uv: r * (2.0 * jnp.maximum(uv, 0.0)), extras=(s['u'],))
        dw_into('mlp_w_up', i, s['hn2'], du, name=f"mlp_up_dw{i}")
        d_hn2 = _mm(du, p['up'][i], 'nt', name=f"mlp_up_dx{i}")
        dh, dg = _rms_bwd(s['h1'], s['g_mlp'], d_hn2, dh, f"rms_mlp_b{i}")
        g['ln_mlp'][i] = dg[0]
        if i % 2 == 0:
            dw_into('ssd_w_out', j, s['yg'], dh, name=f"ssd_out_dw{i}")
            d_yg = _mm(dh, p['ssd_out'][j], 'nt', name=f"ssd_out_dx{i}")
            d_zxd, dcw, dcb, ddtb, dalog, ddsk, dng = _ssd_bwd(s['zxd'], s['states'], d_yg, s['consts'],
                                                              f"ssd_core_b{i}")
            g['ssd_conv_w'][j], g['ssd_conv_b'][j], g['ssd_norm'][j] = dcw, dcb[0], dng[0]
            g['ssd_dt_bias'][j], g['ssd_a_log'][j], g['ssd_d'][j] = (
                ddtb[0, :SSD_HEADS], dalog[0, :SSD_HEADS], ddsk[0, :SSD_HEADS])
            dw_into('ssd_w_in', j, s['hn'], d_zxd, name=f"ssd_in_dw{i}")
            d_hn = _mm(d_zxd, p['ssd_in'][j], 'nt', name=f"ssd_in_dx{i}")
        else:
            wo = _mm(s['o'], dh, 'tn', name=f"mla_out_dw{i}")
            g['mla_w_out'][j] = wo.reshape(MLA_HEADS, HEAD_SLOT, D_MODEL)[:, :MLA_V].reshape(-1, D_MODEL)
            do = _mm(dh, p['mla_out'][j], 'nt', name=f"mla_out_dx{i}")
            dob, delta = _attn_delta(do, s['o'], f"mla_attn_delta{i}")
            dq, dk, dv = _attn_bwd(s['q'], s['k'], s['v'], dob, s['lse'], delta, f"mla_attn_b{i}")

            def prep_bwd(_, qr, kn, kpe, cs, sn, dqv, dkv, dvv, qgv, kgv):
                _, vjp = jax.vjp(lambda a, b, c, d, e: _qk_prep(a, b, c, cs, sn, d, e), qr, kn, kpe, qgv, kgv)
                d_qr, d_kn, d_kpe, d_qg, d_kg = vjp((dqv, dkv))
                return d_qr, jnp.concatenate([d_kn, dvv], axis=1), d_kpe, d_qg, d_kg

            d_qraw, d_kvraw, d_kpe, d_qg, d_kg = _row_call(
                prep_bwd,
                [(s['q_raw'], MLA_WIDE, 0), (s['kv_raw'], MLA_WIDE, 0), (s['lat'], HEAD_SLOT, 2),
                 (cos_t, HEAD_SLOT, 0), (sin_t, HEAD_SLOT, 0), (dq, MLA_WIDE, 0), (dk, MLA_WIDE, 0),
                 (dv, MLA_WIDE, 0)],
                [s['qg'], s['kg']], [(MLA_WIDE, BF16), (2 * MLA_WIDE, BF16), (HEAD_SLOT, F32)],
                [(1, HEAD_SLOT), (1, HEAD_SLOT)], n_rows=lp, tile=_pick(lp, (128,)), name=f"mla_qkprep_b{i}")
            g['mla_q_norm'][j], g['mla_k_norm'][j] = d_qg[0, :MLA_QK], d_kg[0, :MLA_QK]
            wqb = _mm(s['qn'], d_qraw, 'tn', name=f"mla_qb_dw{i}")
            g['mla_w_q_b'][j] = wqb.reshape(MLA_Q_RANK, MLA_HEADS, HEAD_SLOT)[:, :, :MLA_QK].reshape(MLA_Q_RANK, -1)
            d_qn = _mm(d_qraw, p['mla_qb'][j], 'nt', name=f"mla_qb_dx{i}")
            wkvb = _mm(s['kvn'], d_kvraw, 'tn', name=f"mla_kvb_dw{i}").reshape(MLA_KV_RANK, 2, MLA_HEADS, HEAD_SLOT)
            g['mla_w_kv_b'][j] = jnp.concatenate([wkvb[:, 0, :, :MLA_NOPE], wkvb[:, 1, :, :MLA_V]],
                                                 axis=-1).reshape(MLA_KV_RANK, -1)
            d_kvn = _mm(d_kvraw, p['mla_kvb'][j], 'nt', name=f"mla_kvb_dx{i}")

            def lat_bwd(_, kvl, ql, dkvn, dqn, dkpe, kvgv, qagv):
                _, vjp = jax.vjp(_lat_norm, kvl, ql, kvgv, qagv)
                d_kvl, d_ql, d_kvg, d_qag = vjp((dkvn, dqn))
                return jnp.concatenate([d_kvl, dkpe, d_ql], axis=1), d_kvg, d_qag

            d_lat, d_kvg, d_qag = _row_call(
                lat_bwd, [(s['lat'], MLA_KV_RANK, 0), (s['lat'], MLA_Q_RANK, 1), (d_kvn, MLA_KV_RANK, 0),
                          (d_qn, MLA_Q_RANK, 0), (d_kpe, HEAD_SLOT, 0)],
                [s['kvg'], s['qag']], [(LAT_PAD, BF16)], [(1, MLA_KV_RANK), (1, MLA_Q_RANK)],
                n_rows=lp, tile=rt, name=f"mla_latnorm_b{i}")
            g['mla_kv_a_norm'][j], g['mla_q_a_norm'][j] = d_kvg[0], d_qag[0]
            win = _mm(s['hn'], d_lat, 'tn', name=f"mla_in_dw{i}")
            g['mla_w_in'][j] = jnp.concatenate(
                [win[:, MLA_KV_RANK + HEAD_SLOT:], win[:, :MLA_KV_RANK],
                 win[:, MLA_KV_RANK + MLA_NOPE:MLA_KV_RANK + MLA_QK]], axis=1)
            d_hn = _mm(d_lat, p['mla_in'][j], 'nt', name=f"mla_in_dx{i}")
        dh, dg = _rms_bwd(s['h0'], s['g_mix'], d_hn, dh, f"rms_mix_b{i}")
        g['ln_mix'][i] = dg[0]

    grads = {k_: jnp.stack(v_) for k_, v_ in g.items() if k_ not in stacked}
    grads.update(stacked)
    grads['meta_tokens'] = dh[NPAD:NPAD + N_META]
    return loss_row, dh[NPAD + N_META:], grads


def _all_gather8(shard, name):
    m_per, n = shard.shape

    def body(x_ref, out_ref, send_sems, recv_sems, local_sem):
        x, y, c = lax.axis_index("x"), lax.axis_index("y"), lax.axis_index("c")
        me, sibling = (x, y, c), (x, y, 1 - c)
        chips = [(1 - x, y), (x, 1 - y), (1 - x, 1 - y)]

        def rows(px, py, pc):
            return out_ref.at[pl.ds((4 * px + 2 * py + pc) * m_per, m_per), :]

        def copy(k, block, to, src=None):
            return pltpu.make_async_remote_copy(
                src_ref=rows(*block) if src is None else src, dst_ref=rows(*block),
                send_sem=send_sems.at[k], recv_sem=recv_sems.at[k], device_id=to, device_id_type=MESH)

        mine = pltpu.make_async_copy(x_ref, rows(*me), local_sem)
        mine.start()
        first = [copy(0, me, sibling, src=x_ref)]
        first += [copy(1 + j, me, (*chip, c), src=x_ref) for j, chip in enumerate(chips)]
        for cp in first:
            cp.start()
        passed = [copy(4 + j, (*chip, c), sibling) for j, chip in enumerate(chips)]
        for j, chip in enumerate(chips):
            copy(1 + j, (*chip, c), me).wait_recv()
            passed[j].start()
        copy(0, sibling, me).wait_recv()
        for j, chip in enumerate(chips):
            copy(4 + j, (*chip, 1 - c), me).wait_recv()
        for cp in first + passed:
            cp.wait_send()
        mine.wait()

    return pl.pallas_call(
        body, name=name,
        out_shape=jax.ShapeDtypeStruct((8 * m_per, n), shard.dtype),
        in_specs=[pl.BlockSpec(memory_space=pl.ANY)],
        out_specs=pl.BlockSpec(memory_space=pl.ANY),
        scratch_shapes=[pltpu.SemaphoreType.DMA((7,)), pltpu.SemaphoreType.DMA((7,)), pltpu.SemaphoreType.DMA],
    )(shard)


def _mesh_pos():
    return lax.axis_index("x"), lax.axis_index("y"), lax.axis_index("c")


def _half_rows(pc, h):
    return pl.ds(pl.multiple_of(pc * h, 16), h)


def _whole_view(ref, kind, shard_shape, k, pc):
    _, r, c = shard_shape
    rows = _half_rows(pc, r // 2)
    if kind == 'row':
        return ref.at[:, k, rows, :]
    if kind == 'col':
        return ref.at[:, rows, pl.ds(pl.multiple_of(k * c, 128), c)]
    return ref.at[k, :, rows, :]


def _whole_shape(kind, shard_shape, rows=None):
    l, r, c = shard_shape
    r = r if rows is None else rows
    return {'row': (l, 4, r, c), 'col': (l, r, 4 * c), 'colx': (4, l, r, c)}[kind]


def _gather_big(shards, kinds, name):
    n = len(shards)
    shapes = [s.shape for s in shards]

    def body(*refs):
        ins, outs = refs[:n], refs[n:2 * n]
        send_sems, recv_sems, local_sems = refs[2 * n:]
        x, y, c = _mesh_pos()
        me, sibling = (x, y, c), (x, y, 1 - c)
        chips = [(1 - x, y), (x, 1 - y), (1 - x, 1 - y)]

        def place(a, px, py, pc):
            return _whole_view(outs[a], kinds[a], shapes[a], 2 * px + py, pc)

        def own(a):
            return ins[a].at[:, _half_rows(c, shapes[a][1] // 2), :]

        def copy(a, k, block, to, src=None):
            return pltpu.make_async_remote_copy(
                src_ref=place(a, *block) if src is None else src, dst_ref=place(a, *block),
                send_sem=send_sems.at[7 * a + k], recv_sem=recv_sems.at[7 * a + k],
                device_id=to, device_id_type=MESH)

        mine = [pltpu.make_async_copy(own(a), place(a, *me), local_sems.at[a]) for a in range(n)]
        first = [copy(a, 1 + j, me, (*chip, c), src=own(a)) for j, chip in enumerate(chips) for a in range(n)]
        first += [copy(a, 0, me, sibling, src=own(a)) for a in range(n)]
        for cp in first + mine:
            cp.start()
        passed = []
        for j, chip in enumerate(chips):
            for a in range(n):
                copy(a, 1 + j, (*chip, c), me).wait_recv()
                passed.append(copy(a, 4 + j, (*chip, c), sibling))
                passed[-1].start()
        for a in range(n):
            copy(a, 0, sibling, me).wait_recv()
        for j, chip in enumerate(chips):
            for a in range(n):
                copy(a, 4 + j, (*chip, 1 - c), me).wait_recv()
        for cp in first + passed:
            cp.wait_send()
        for cp in mine:
            cp.wait()

    return pl.pallas_call(
        body, name=name,
        out_shape=[jax.ShapeDtypeStruct(_whole_shape(k, s.shape), s.dtype) for k, s in zip(kinds, shards)],
        in_specs=[pl.BlockSpec(memory_space=pl.ANY)] * n,
        out_specs=[pl.BlockSpec(memory_space=pl.ANY)] * n,
        scratch_shapes=[pltpu.SemaphoreType.DMA((7 * n,)), pltpu.SemaphoreType.DMA((7 * n,)),
                        pltpu.SemaphoreType.DMA((n,))],
    )(*shards)


def _rs_swap(wholes, kinds, shapes, name):
    n = len(wholes)

    def body(*refs):
        ins, outs = refs[:n], refs[n:2 * n]
        send_sems, recv_sems = refs[2 * n:]
        x, y, c = _mesh_pos()
        cps = []
        for a in range(n):
            rows = _half_rows(1 - c, shapes[a][1] // 2)
            src = ins[a].at[:, rows, :] if kinds[a] == 'col' else ins[a].at[:, :, rows, :]
            cps.append(pltpu.make_async_remote_copy(
                src_ref=src, dst_ref=outs[a], send_sem=send_sems.at[a], recv_sem=recv_sems.at[a],
                device_id=(x, y, 1 - c), device_id_type=MESH))
        for cp in cps:
            cp.start()
        for cp in cps:
            cp.wait()

    return pl.pallas_call(
        body, name=name,
        out_shape=[jax.ShapeDtypeStruct(_whole_shape(k, s, s[1] // 2), w.dtype)
                   for k, s, w in zip(kinds, shapes, wholes)],
        in_specs=[pl.BlockSpec(memory_space=pl.ANY)] * n,
        out_specs=[pl.BlockSpec(memory_space=pl.ANY)] * n,
        scratch_shapes=[pltpu.SemaphoreType.DMA((n,)), pltpu.SemaphoreType.DMA((n,))],
    )(*wholes)


def _rs_exchange(parts, kinds, shapes, name):
    n = len(parts)

    def body(*refs):
        ins, outs = refs[:n], refs[n:2 * n]
        send_sems, recv_sems, local_sems = refs[2 * n:]
        x, y, c = _mesh_pos()
        kme = 2 * x + y
        chips = [(1 - x, y), (x, 1 - y), (1 - x, 1 - y)]

        def slab(a, k):
            if kinds[a] == 'row':
                return ins[a].at[:, k]
            if kinds[a] == 'col':
                cw = shapes[a][2]
                return ins[a].at[:, :, pl.ds(pl.multiple_of(k * cw, 128), cw)]
            return ins[a].at[k]

        cps = [pltpu.make_async_remote_copy(
            src_ref=slab(a, 2 * px + py), dst_ref=outs[a].at[kme], send_sem=send_sems.at[3 * a + j],
            recv_sem=recv_sems.at[3 * a + j], device_id=(px, py, c), device_id_type=MESH)
            for j, (px, py) in enumerate(chips) for a in range(n)]
        cps_local = [pltpu.make_async_copy(slab(a, kme), outs[a].at[kme], local_sems.at[a]) for a in range(n)]
        for cp in cps + cps_local:
            cp.start()
        for cp in cps + cps_local:
            cp.wait()

    return pl.pallas_call(
        body, name=name,
        out_shape=[jax.ShapeDtypeStruct((4, s[0], s[1] // 2, s[2]), p.dtype) for s, p in zip(shapes, parts)],
        in_specs=[pl.BlockSpec(memory_space=pl.ANY)] * n,
        out_specs=[pl.BlockSpec(memory_space=pl.ANY)] * n,
        scratch_shapes=[pltpu.SemaphoreType.DMA((3 * n,)), pltpu.SemaphoreType.DMA((3 * n,)),
                        pltpu.SemaphoreType.DMA((n,))],
    )(*parts)


def _rs_share(shards, name):
    n = len(shards)

    def body(*refs):
        outs = refs[n:2 * n]
        send_sems, recv_sems = refs[2 * n:]
        x, y, c = _mesh_pos()
        cps = []
        for a in range(n):
            rows = outs[a].at[:, _half_rows(c, shards[a].shape[1] // 2), :]
            cps.append(pltpu.make_async_remote_copy(
                src_ref=rows, dst_ref=rows, send_sem=send_sems.at[a], recv_sem=recv_sems.at[a],
                device_id=(x, y, 1 - c), device_id_type=MESH))
        for cp in cps:
            cp.start()
        for cp in cps:
            cp.wait()

    return pl.pallas_call(
        body, name=name,
        out_shape=[jax.ShapeDtypeStruct(s.shape, s.dtype) for s in shards],
        in_specs=[pl.BlockSpec(memory_space=pl.ANY)] * n,
        out_specs=[pl.BlockSpec(memory_space=pl.ANY)] * n,
        input_output_aliases={a: a for a in range(n)},
        scratch_shapes=[pltpu.SemaphoreType.DMA((n,)), pltpu.SemaphoreType.DMA((n,))],
    )(*shards)


def _tile_rows(rows, cols, budget=2 * 1024 * 1024):
    for t in (1024, 512, 256, 128, 64, 32, 16, 8):
        if rows % t == 0 and t * cols * 4 <= budget:
            return t
    return rows


def _add_half(g3, r3, c_idx, name):
    a, h, n = r3.shape
    t = _tile_rows(h, n)
    nt = h // t

    def body(c_ref, g_ref, r_ref, o_ref):
        o_ref[...] = (g_ref[...] + r_ref[...]).astype(o_ref.dtype)

    return pl.pallas_call(
        body, name=name, out_shape=jax.ShapeDtypeStruct((a, h, n), BF16),
        grid_spec=pltpu.PrefetchScalarGridSpec(
            num_scalar_prefetch=1, grid=(a, nt),
            in_specs=[pl.BlockSpec((1, t, n), lambda k, i, c: (k, c[0] * nt + i, 0)),
                      pl.BlockSpec((1, t, n), lambda k, i, c: (k, i, 0))],
            out_specs=pl.BlockSpec((1, t, n), lambda k, i, c: (k, i, 0))),
        compiler_params=_cparams(dimension_semantics=("parallel", "parallel")),
    )(c_idx, g3, r3)


def _sum4(parts, c_idx, name):
    _, l, h, n = parts.shape
    t = _tile_rows(h, n, 1024 * 1024)
    nt = h // t

    def body(c_ref, p_ref, o_ref):
        pv = p_ref[...].astype(F32)
        o_ref[...] = ((pv[0] + pv[1]) + pv[2]) + pv[3]

    return pl.pallas_call(
        body, name=name, out_shape=jax.ShapeDtypeStruct((l, 2 * h, n), F32),
        grid_spec=pltpu.PrefetchScalarGridSpec(
            num_scalar_prefetch=1, grid=(l, nt),
            in_specs=[pl.BlockSpec((4, 1, t, n), lambda k, i, c: (0, k, i, 0))],
            out_specs=pl.BlockSpec((1, t, n), lambda k, i, c: (k, c[0] * nt + i, 0))),
        compiler_params=_cparams(dimension_semantics=("parallel", "parallel")),
    )(c_idx, parts)


def _sum8(parts, name):
    _, m, n = parts.shape

    def body(p_ref, o_ref):
        acc = p_ref[0]
        for d in range(1, 8):
            acc = acc + p_ref[d]
        o_ref[...] = acc

    return pl.pallas_call(body, name=name, out_shape=jax.ShapeDtypeStruct((m, n), F32))(parts)


def _adamw(wp, gp, mp, vp, name):
    r, n = wp.shape
    t = _tile_rows(r, n, 1024 * 1024)

    def body(w_ref, g_ref, m_ref, v_ref, d_ref, mo_ref, vo_ref):
        gv = g_ref[...]
        m2 = ADAM_B1 * m_ref[...] + (1.0 - ADAM_B1) * gv
        v2 = ADAM_B2 * v_ref[...] + (1.0 - ADAM_B2) * (gv * gv)
        m_hat = m2 / (1.0 - ADAM_B1 ** ADAM_STEP)
        v_hat = v2 / (1.0 - ADAM_B2 ** ADAM_STEP)
        d_ref[...] = -ADAM_LR * (m_hat / (jnp.sqrt(v_hat) + ADAM_EPS) + ADAM_WD * w_ref[...])
        mo_ref[...] = m2
        vo_ref[...] = v2

    spec = pl.BlockSpec((t, n), lambda i: (i, 0))
    return pl.pallas_call(
        body, name=name, out_shape=[jax.ShapeDtypeStruct((r, n), F32)] * 3, grid=(r // t,),
        in_specs=[spec] * 4, out_specs=[spec] * 3,
        compiler_params=_cparams(dimension_semantics=("parallel",)),
    )(wp, gp, mp, vp)


BIG = (('ssd_w_in', 'colx'), ('ssd_w_out', 'row'), ('mla_w_in', 'row'), ('mla_w_q_b', 'col'),
       ('mla_w_kv_b', 'col'), ('mla_w_out', 'row'), ('mlp_w_up', 'col'), ('mlp_w_down', 'row'))
SMALL_SHARDED = (('meta_tokens', 1), ('ssd_conv_w', 2), ('mla_q_a_norm', 1), ('mla_kv_a_norm', 1))
SMALL_REPL = ('ln_mix', 'ln_mlp', 'ssd_conv_b', 'ssd_dt_bias', 'ssd_a_log', 'ssd_d', 'ssd_norm',
              'mla_q_norm', 'mla_k_norm')
ALL_NAMES = ('meta_tokens', 'ln_mix', 'ln_mlp', 'ssd_w_in', 'ssd_conv_w', 'ssd_conv_b', 'ssd_dt_bias',
             'ssd_a_log', 'ssd_d', 'ssd_norm', 'ssd_w_out', 'mla_w_in', 'mla_q_a_norm', 'mla_w_q_b',
             'mla_kv_a_norm', 'mla_w_kv_b', 'mla_q_norm', 'mla_k_norm', 'mla_w_out', 'mlp_w_up', 'mlp_w_down')


def _pack(arrs, rows_mult):
    flat = jnp.concatenate([a.reshape(-1) for a in arrs])
    per = LANES * rows_mult
    pad = (-flat.shape[0]) % per
    if pad:
        flat = jnp.concatenate([flat, jnp.zeros((pad,), flat.dtype)])
    return flat.reshape(-1, LANES)


def _unpack(pack, shapes):
    flat = pack.reshape(-1)
    out, off = [], 0
    for shp in shapes:
        n = math.prod(shp)
        out.append(flat[off:off + n].reshape(shp))
        off += n
    return out


def _split4(full, axis):
    shp = full.shape
    r = full.reshape(shp[:axis] + (4, shp[axis] // 4) + shp[axis + 1:])
    return jnp.moveaxis(r, axis, 0)


def _join4(parts, axis):
    r = jnp.moveaxis(parts, 0, axis)
    shp = r.shape
    return r.reshape(shp[:axis] + (shp[axis] * shp[axis + 1],) + shp[axis + 2:])


def _gather_params(shards, table, dtype, c, name):
    pack = _pack([shards[n].astype(dtype) for n, _ in table], 16)
    half = pack.shape[0] // 2
    mine = lax.dynamic_slice_in_dim(pack, c * half, half, axis=0)
    full = _all_gather8(mine, name).reshape(4, -1)
    out, off = {}, 0
    for n, ax in table:
        cnt = math.prod(shards[n].shape)
        out[n] = _join4(full[:, off:off + cnt].reshape((4,) + shards[n].shape), ax)
        off += cnt
    return out


def kernel(x, meta_tokens, ln_mix, ln_mlp, ssd_w_in, ssd_conv_w, ssd_conv_b, ssd_dt_bias, ssd_a_log, ssd_d, ssd_norm, ssd_w_out, mla_w_in, mla_q_a_norm, mla_w_q_b, mla_kv_a_norm, mla_w_kv_b, mla_q_norm, mla_k_norm, mla_w_out, mlp_w_up, mlp_w_down, loss_target, m_meta_tokens, m_ln_mix, m_ln_mlp, m_ssd_w_in, m_ssd_conv_w, m_ssd_conv_b, m_ssd_dt_bias, m_ssd_a_log, m_ssd_d, m_ssd_norm, m_ssd_w_out, m_mla_w_in, m_mla_q_a_norm, m_mla_w_q_b, m_mla_kv_a_norm, m_mla_w_kv_b, m_mla_q_norm, m_mla_k_norm, m_mla_w_out, m_mlp_w_up, m_mlp_w_down, v_meta_tokens, v_ln_mix, v_ln_mlp, v_ssd_w_in, v_ssd_conv_w, v_ssd_conv_b, v_ssd_dt_bias, v_ssd_a_log, v_ssd_d, v_ssd_norm, v_ssd_w_out, v_mla_w_in, v_mla_q_a_norm, v_mla_w_q_b, v_mla_kv_a_norm, v_mla_w_kv_b, v_mla_q_norm, v_mla_k_norm, v_mla_w_out, v_mlp_w_up, v_mlp_w_down):
    w_sh = dict(meta_tokens=meta_tokens, ln_mix=ln_mix, ln_mlp=ln_mlp, ssd_w_in=ssd_w_in, ssd_conv_w=ssd_conv_w, ssd_conv_b=ssd_conv_b, ssd_dt_bias=ssd_dt_bias, ssd_a_log=ssd_a_log, ssd_d=ssd_d, ssd_norm=ssd_norm, ssd_w_out=ssd_w_out, mla_w_in=mla_w_in, mla_q_a_norm=mla_q_a_norm, mla_w_q_b=mla_w_q_b, mla_kv_a_norm=mla_kv_a_norm, mla_w_kv_b=mla_w_kv_b, mla_q_norm=mla_q_norm, mla_k_norm=mla_k_norm, mla_w_out=mla_w_out, mlp_w_up=mlp_w_up, mlp_w_down=mlp_w_down)
    m_sh = dict(meta_tokens=m_meta_tokens, ln_mix=m_ln_mix, ln_mlp=m_ln_mlp, ssd_w_in=m_ssd_w_in, ssd_conv_w=m_ssd_conv_w, ssd_conv_b=m_ssd_conv_b, ssd_dt_bias=m_ssd_dt_bias, ssd_a_log=m_ssd_a_log, ssd_d=m_ssd_d, ssd_norm=m_ssd_norm, ssd_w_out=m_ssd_w_out, mla_w_in=m_mla_w_in, mla_q_a_norm=m_mla_q_a_norm, mla_w_q_b=m_mla_w_q_b, mla_kv_a_norm=m_mla_kv_a_norm, mla_w_kv_b=m_mla_w_kv_b, mla_q_norm=m_mla_q_norm, mla_k_norm=m_mla_k_norm, mla_w_out=m_mla_w_out, mlp_w_up=m_mlp_w_up, mlp_w_down=m_mlp_w_down)
    v_sh = dict(meta_tokens=v_meta_tokens, ln_mix=v_ln_mix, ln_mlp=v_ln_mlp, ssd_w_in=v_ssd_w_in, ssd_conv_w=v_ssd_conv_w, ssd_conv_b=v_ssd_conv_b, ssd_dt_bias=v_ssd_dt_bias, ssd_a_log=v_ssd_a_log, ssd_d=v_ssd_d, ssd_norm=v_ssd_norm, ssd_w_out=v_ssd_w_out, mla_w_in=v_mla_w_in, mla_q_a_norm=v_mla_q_a_norm, mla_w_q_b=v_mla_w_q_b, mla_kv_a_norm=v_mla_kv_a_norm, mla_w_kv_b=v_mla_w_kv_b, mla_q_norm=v_mla_q_norm, mla_k_norm=v_mla_k_norm, mla_w_out=v_mla_w_out, mlp_w_up=v_mlp_w_up, mlp_w_down=v_mlp_w_down)

    cx, cy, cc = lax.axis_index("x"), lax.axis_index("y"), lax.axis_index("c")
    chip = 2 * cx + cy

    c_idx = cc.reshape(1).astype(jnp.int32)
    big_names = [n for n, _ in BIG]
    kinds = [k for _, k in BIG]
    shapes = [w_sh[n].shape for n in big_names]

    w = {n: w_sh[n] for n in SMALL_REPL}
    wholes = _gather_big([w_sh[n].astype(BF16) for n in big_names], kinds, "gather_big")
    for n, kind, s, f in zip(big_names, kinds, shapes, wholes):
        if kind == 'row':
            w[n] = f.reshape(s[0], 4 * s[1], s[2])
        elif kind == 'col':
            w[n] = f
        else:
            w[n] = jnp.concatenate([f[k] for k in range(4)], axis=-1)
    w.update(_gather_params(w_sh, SMALL_SHARDED, F32, cc, "gather_small"))

    loss_row, grad_x, grads = _local_step(x[0], loss_target[0], w)
    loss = lax.psum(jnp.sum(loss_row), ("x", "y", "c"))

    g_whole = []
    for n, kind, s in zip(big_names, kinds, shapes):
        if kind == 'row':
            g_whole.append(grads[n].reshape(s[0], 4, s[1], s[2]))
        elif kind == 'col':
            g_whole.append(grads[n])
        else:
            g_whole.append(jnp.stack([grads[n][..., k * s[2]:(k + 1) * s[2]] for k in range(4)]))
    recv = _rs_swap(g_whole, kinds, shapes, "rs_swap")
    parts = []
    for n, kind, s, g, r in zip(big_names, kinds, shapes, g_whole, recv):
        if kind == 'col':
            g3, r3 = g, r
        else:
            g3, r3 = g.reshape(-1, s[1], s[2]), r.reshape(-1, s[1] // 2, s[2])
        parts.append(_add_half(g3, r3, c_idx, f"rs_add_{n}").reshape(r.shape))
    got = _rs_exchange(parts, kinds, shapes, "rs_exchange")
    reduced = [_sum4(p, c_idx, f"rs_sum_{n}") for n, p in zip(big_names, got)]
    g_sh = dict(zip(big_names, _rs_share(reduced, "rs_share")))

    small_names = tuple(n for n, _ in SMALL_SHARDED) + SMALL_REPL
    sp = _pack([grads[n] for n in small_names], 8)
    srows = sp.shape[0]
    s_all = _sum8(_all_gather8(sp, "ar_small_gather").reshape(8, srows, LANES), "ar_small_sum")
    s_full = dict(zip(small_names, _unpack(s_all, [grads[n].shape for n in small_names])))
    for n, ax in SMALL_SHARDED:
        g_sh[n] = lax.dynamic_index_in_dim(_split4(s_full[n], ax), chip, axis=0, keepdims=False)
    for n in SMALL_REPL:
        g_sh[n] = s_full[n]

    delta, new_m, new_v = {}, {}, {}
    for n, s in zip(big_names, shapes):
        res = _adamw(*[t[n].reshape(-1, s[2]) for t in (w_sh, g_sh, m_sh, v_sh)], f"adamw_{n}")
        delta[n], new_m[n], new_v[n] = [r.reshape(s) for r in res]
    d_s, m_s, v_s = _adamw(*[_pack([t[n] for n in small_names], 8) for t in (w_sh, g_sh, m_sh, v_sh)],
                           "adamw_small")
    for dst, ps in ((delta, d_s), (new_m, m_s), (new_v, v_s)):
        dst.update(zip(small_names, _unpack(ps, [w_sh[n].shape for n in small_names])))

    return (loss, grad_x[None], *[g_sh[n] for n in ALL_NAMES], *[delta[n] for n in ALL_NAMES],
            *[new_m[n] for n in ALL_NAMES], *[new_v[n] for n in ALL_NAMES])
```

```python
import functools
import math

import jax
import jax.numpy as jnp
from jax import lax
from jax.experimental import pallas as pl
from jax.experimental.pallas import tpu as pltpu

F32 = jnp.float32
BF16 = jnp.bfloat16
MESH = pl.DeviceIdType.MESH
_NN = (((1,), (0,)), ((), ()))
_NT = (((1,), (1,)), ((), ()))
_TN = (((0,), (0,)), ((), ()))

D_MODEL = 1024
N_META = 16
EPS = 1e-6
SSD_D_INNER = 2048
SSD_HEADS = 32
SSD_HEAD_DIM = 64
SSD_GROUPS = 8
SSD_HPG = 4
SSD_STATE = 128
SSD_CONV = 4
CHUNK = 128
SSD_IN_DIM = 6176
SSD_IN_PAD = 6272
MLA_HEADS = 16
MLA_NOPE = 64
MLA_ROPE = 32
MLA_V = 64
MLA_QK = 96
MLA_Q_RANK = 384
MLA_KV_RANK = 256
HEAD_SLOT = 128
MLA_WIDE = MLA_HEADS * HEAD_SLOT
HEADS_PER_STEP = 2
LAT_PAD = 768
ROPE_THETA = 10000.0
D_FF = 4096
NPAD = CHUNK - N_META
ADAM_LR, ADAM_B1, ADAM_B2, ADAM_EPS, ADAM_WD, ADAM_STEP = 0.001, 0.9, 0.999, 1e-08, 0.01, 10
LANES = 1024
VMEM_LIMIT = 56 * 1024 * 1024


def _pick(n, cands):
    for c in cands:
        if n % c == 0:
            return c
    return n


def _cparams(**kw):
    return pltpu.CompilerParams(vmem_limit_bytes=VMEM_LIMIT, **kw)


def _mm(a, b, dims, *, name, out_dtype=F32, a_fn=None, epi=None, extras=(), stack=None):
    if dims == 'nn':
        (M, K), (K2, N) = a.shape, b.shape
    elif dims == 'nt':
        (M, K), (N, K2) = a.shape, b.shape
    else:
        (K, M), (K2, N) = a.shape, b.shape
    assert K == K2, (a.shape, b.shape, dims)
    if dims == 'tn':
        tm = _pick(M, (1024, 768, 512, 384, 256, 128))
        tn = _pick(N, (1024, 896, 768, 512, 384, 256, 128))
        tk = _pick(K, (1408, 1024, 512, 384, 256, 128))
    else:
        tm = _pick(M, (1408, 1024, 512, 384, 256, 128))
        tn = _pick(N, (512, 896, 768, 384, 256, 128))
        tk = _pick(K, (1024, 896, 768, 512, 384, 256, 128))
    nk = K // tk
    if dims == 'nn':
        a_spec = pl.BlockSpec((tm, tk), lambda i, j, k: (i, k))
        b_spec = pl.BlockSpec((tk, tn), lambda i, j, k: (k, j))
        dn = (((1,), (0,)), ((), ()))
    elif dims == 'nt':
        a_spec = pl.BlockSpec((tm, tk), lambda i, j, k: (i, k))
        b_spec = pl.BlockSpec((tn, tk), lambda i, j, k: (j, k))
        dn = (((1,), (1,)), ((), ()))
    else:
        a_spec = pl.BlockSpec((tk, tm), lambda i, j, k: (k, i))
        b_spec = pl.BlockSpec((tk, tn), lambda i, j, k: (k, j))
        dn = (((0,), (0,)), ((), ()))
    o_spec = pl.BlockSpec((tm, tn), lambda i, j, k: (i, j))
    n_ex = len(extras)
    out_shape = jax.ShapeDtypeStruct((M, N), out_dtype)
    out_spec, held, aliases = o_spec, (), {}
    if stack is not None:
        n_slabs, slab, buf = stack
        out_shape = jax.ShapeDtypeStruct((n_slabs, M, N), out_dtype)
        out_spec = pl.BlockSpec((None, tm, tn), lambda i, j, k: (slab, i, j))
        if buf is not None:
            held, aliases = (buf,), {2 + n_ex: 0}

    def body(a_ref, b_ref, *rest):
        ex_refs, o_ref, acc = rest[:n_ex], rest[n_ex + len(held)], rest[n_ex + len(held) + 1]
        k = pl.program_id(2)

        @pl.when(k == 0)
        def _():
            acc[...] = jnp.zeros_like(acc)

        av = a_ref[...]
        if a_fn is not None:
            av = a_fn(av)
        acc[...] += lax.dot_general(av.astype(BF16), b_ref[...].astype(BF16), dn,
                                    preferred_element_type=F32)

        @pl.when(k == nk - 1)
        def _():
            r = acc[...]
            if epi is not None:
                r = epi(r, *[e[...] for e in ex_refs])
            o_ref[...] = r.astype(out_dtype)

    return pl.pallas_call(
        body, name=name,
        out_shape=out_shape,
        grid=(M // tm, N // tn, nk),
        in_specs=[a_spec, b_spec] + [o_spec] * n_ex + [pl.BlockSpec(memory_space=pl.ANY)] * len(held),
        out_specs=out_spec,
        input_output_aliases=aliases,
        scratch_shapes=[pltpu.VMEM((tm, tn), F32)],
        compiler_params=_cparams(dimension_semantics=("parallel", "parallel", "arbitrary")),
    )(a, b, *extras, *held)


def _row_call(fn, rows, consts, out_rows, out_accs=(), *, n_rows, tile, name):
    n_r, n_c, n_o, n_a = len(rows), len(consts), len(out_rows), len(out_accs)
    steps = n_rows // tile

    def body(*refs):
        r_refs = refs[:n_r]
        c_refs = refs[n_r:n_r + n_c]
        o_refs = refs[n_r + n_c:n_r + n_c + n_o]
        a_refs = refs[n_r + n_c + n_o:]
        i = pl.program_id(0)
        res = fn(i, *[r[...] for r in r_refs], *[c[...] for c in c_refs])
        for o_ref, val in zip(o_refs, res[:n_o]):
            o_ref[...] = val.astype(o_ref.dtype)

        @pl.when(i == 0)
        def _():
            for a_ref in a_refs:
                a_ref[...] = jnp.zeros_like(a_ref)

        for a_ref, val in zip(a_refs, res[n_o:]):
            a_ref[...] += val

    in_specs = [pl.BlockSpec((tile, w), functools.partial(lambda i, cb: (i, cb), cb=cb))
                for (_, w, cb) in rows]
    in_specs += [pl.BlockSpec(c.shape, lambda i: (0, 0)) for c in consts]
    out_specs = [pl.BlockSpec((tile, c), lambda i: (i, 0)) for (c, _) in out_rows]
    out_specs += [pl.BlockSpec(s, lambda i: (0, 0)) for s in out_accs]
    out_shape = [jax.ShapeDtypeStruct((n_rows, c), dt) for (c, dt) in out_rows]
    out_shape += [jax.ShapeDtypeStruct(s, F32) for s in out_accs]
    return pl.pallas_call(
        body, name=name, out_shape=out_shape, grid=(steps,),
        in_specs=in_specs, out_specs=out_specs,
        compiler_params=_cparams(dimension_semantics=("arbitrary",)),
    )(*[r[0] for r in rows], *consts)


def _row_mask(i, tile):
    r = i * tile + lax.broadcasted_iota(jnp.int32, (tile, 1), 0)
    return (r >= NPAD).astype(F32)


def _rms(x, g):
    return x * lax.rsqrt(jnp.mean(x * x, axis=-1, keepdims=True) + EPS) * g


def _silu(x):
    return x * (0.5 * jnp.tanh(0.5 * x) + 0.5)


def _softplus(x):
    return jnp.maximum(x, 0.0) + jnp.log(1.0 + jnp.exp(-jnp.abs(x)))


def _rms_fwd(h, g, name):
    lp = h.shape[0]
    return _row_call(lambda i, hv, gv: (_rms(hv, gv),), [(h, D_MODEL, 0)], [g],
                     [(D_MODEL, BF16)], n_rows=lp, tile=_pick(lp, (384, 256, 128)), name=name)[0]


def _rms_bwd(h, g, d_hn, d_res, name):
    lp = h.shape[0]
    tile = _pick(lp, (384, 256, 128))

    def fn(i, hv, dv, rv, gv):
        _, vjp = jax.vjp(_rms, hv, gv)
        dh, dg = vjp(dv)
        return (rv + dh) * _row_mask(i, tile), dg

    return _row_call(fn, [(h, D_MODEL, 0), (d_hn, D_MODEL, 0), (d_res, D_MODEL, 0)], [g],
                     [(D_MODEL, F32)], [(1, D_MODEL)], n_rows=lp, tile=tile, name=name)


@functools.partial(jax.custom_vjp, nondiff_argnums=(1,))
def _roll_rows(x, s):
    return pltpu.roll(x, s, 0)


def _roll_rows_fwd(x, s):
    return pltpu.roll(x, s, 0), None


def _roll_rows_bwd(s, _, ct):
    return (pltpu.roll(ct, (ct.shape[0] - s) % ct.shape[0], 0),)


_roll_rows.defvjp(_roll_rows_fwd, _roll_rows_bwd)


def _conv_silu(cur, halo, w_rows, b):
    full = jnp.concatenate([halo, cur], axis=0)
    acc = cur * w_rows[SSD_CONV - 1] + b
    for k in range(SSD_CONV - 1):
        acc = acc + _roll_rows(full, SSD_CONV - 1 - k)[8:] * w_rows[k]
    return _silu(acc)


def _split3(v):
    hi = v.astype(BF16)
    r1 = v - hi.astype(F32)
    mid = r1.astype(BF16)
    lo = (r1 - mid.astype(F32)).astype(BF16)
    return hi, mid, lo


def _select_right(v, sel, dn):
    return sum(lax.dot_general(p, sel, dn, preferred_element_type=F32) for p in _split3(v))


@jax.custom_vjp
def _expand_heads(v, e_mat):
    return _select_right(v, e_mat, _NN)


def _expand_heads_fwd(v, e_mat):
    return _select_right(v, e_mat, _NN), e_mat


def _expand_heads_bwd(e_mat, ct):
    return _select_right(ct, e_mat, _NT), jnp.zeros_like(e_mat)


_expand_heads.defvjp(_expand_heads_fwd, _expand_heads_bwd)


@jax.custom_vjp
def _cumsum_rows(a, tri):
    return sum(lax.dot_general(tri, p, _NN, preferred_element_type=F32) for p in _split3(a))


def _cumsum_rows_fwd(a, tri):
    return _cumsum_rows(a, tri), tri


def _cumsum_rows_bwd(tri, ct):
    return (sum(lax.dot_general(tri, p, _TN, preferred_element_type=F32) for p in _split3(ct)),
            jnp.zeros_like(tri))


_cumsum_rows.defvjp(_cumsum_rows_fwd, _cumsum_rows_bwd)


def _ssd_chunk(mask, z, xs_pre, bc_pre, halo_x, halo_bc, dt_pre, st, cwx0, cwx1, cwx2, cwx3,
               cwb0, cwb1, cwb2, cwb3, cb_x, cb_bc, dtb, alog, dsk, ng):
    L = CHUNK
    lane_h = lax.broadcasted_iota(jnp.int32, (1, 128), 1)
    head_ok = (lane_h < SSD_HEADS).astype(F32)
    e_mat = (lax.broadcasted_iota(jnp.int32, (128, SSD_D_INNER), 1) // SSD_HEAD_DIM
             == lax.broadcasted_iota(jnp.int32, (128, SSD_D_INNER), 0)).astype(BF16)
    ri = lax.broadcasted_iota(jnp.int32, (L, L), 0)
    ci = lax.broadcasted_iota(jnp.int32, (L, L), 1)
    causal = ri >= ci

    xs = _conv_silu(xs_pre, halo_x, (cwx0, cwx1, cwx2, cwx3), cb_x) * mask
    bc = _conv_silu(bc_pre, halo_bc, (cwb0, cwb1, cwb2, cwb3), cb_bc) * mask
    dt = _softplus(dt_pre + dtb) * mask * head_ok
    a_dt = dt * (-jnp.exp(alog))
    a_cs = _cumsum_rows(a_dt, causal.astype(BF16))
    a_cs_t = a_cs.T
    row8 = lax.broadcasted_iota(jnp.int32, (8, 128), 0)
    last8 = jnp.where(row8 == 0, jnp.sum(a_dt, axis=0, keepdims=True), 0.0)
    dsk8 = jnp.where(row8 == 0, dsk, 0.0)
    wide = _expand_heads(jnp.concatenate([dt, a_cs, last8, dsk8], axis=0), e_mat)
    dt_e, acs_e = wide[0:L], wide[L:2 * L]
    last_e = jnp.sum(wide[2 * L:2 * L + 8], axis=0, keepdims=True)
    d_e = jnp.sum(wide[2 * L + 8:2 * L + 16], axis=0, keepdims=True)
    xdt = xs * dt_e
    dte_e = jnp.exp(last_e - acs_e)
    dfs_e = jnp.exp(acs_e)
    cd_e = jnp.exp(last_e)
    sub_h = lax.broadcasted_iota(jnp.int32, (128, L), 0)
    lane_hl = lax.broadcasted_iota(jnp.int32, (L, 128), 1)
    lane_g = lax.broadcasted_iota(jnp.int32, (1, SSD_HPG * SSD_HEAD_DIM), 1) // SSD_HEAD_DIM

    ys, new_st = [], []
    for g in range(SSD_GROUPS):
        b_g = bc[:, g * 128:(g + 1) * 128].astype(BF16)
        c_g = bc[:, 1024 + g * 128:1024 + (g + 1) * 128].astype(BF16)
        gs = slice(g * 256, (g + 1) * 256)
        xdt_g = xdt[:, gs]
        cb = lax.dot_general(c_g, b_g, (((1,), (1,)), ((), ())), preferred_element_type=F32)
        st_g = st[g * 128:(g + 1) * 128, :]
        y_g = lax.dot_general(c_g, st_g.astype(BF16), (((1,), (0,)), ((), ())),
                              preferred_element_type=F32) * dfs_e[:, gs]
        for j in range(SSD_HPG):
            h = g * SSD_HPG + j
            col = jnp.sum(jnp.where(lane_hl == h, a_cs, 0.0), axis=1, keepdims=True)
            row = jnp.sum(jnp.where(sub_h == h, a_cs_t, 0.0), axis=0, keepdims=True)
            dec = jnp.where(causal, jnp.exp(jnp.where(causal, col - row, 0.0)), 0.0)
            m_h = (cb * dec).astype(BF16)
            x_h = jnp.where(lane_g == j, xdt_g, 0.0).astype(BF16)
            y_g = y_g + lax.dot_general(m_h, x_h, (((1,), (0,)), ((), ())),
                                        preferred_element_type=F32)
        s_new = lax.dot_general(b_g, (xdt_g * dte_e[:, gs]).astype(BF16), (((0,), (0,)), ((), ())),
                                preferred_element_type=F32)
        new_st.append(st_g * cd_e[:, gs] + s_new)
        ys.append(y_g)
    y = jnp.concatenate(ys, axis=1) + xs * d_e
    gg = y * _silu(z)
    outs = []
    for g in range(SSD_GROUPS):
        sl = gg[:, g * 256:(g + 1) * 256]
        outs.append(sl * lax.rsqrt(jnp.mean(sl * sl, axis=-1, keepdims=True) + EPS))
    out = jnp.concatenate(outs, axis=1) * ng
    return out, jnp.concatenate(new_st, axis=0)


def _ssd_consts(conv_w, conv_b, dtb, alog, dsk, ng):
    return [conv_w, conv_b, dtb, alog, dsk, ng]


def _ssd_param_vals(cw_ref, cb_ref, dtb_ref, alog_ref, dsk_ref, ng_ref):
    cwx = [cw_ref[k:k + 1, 0:SSD_D_INNER] for k in range(SSD_CONV)]
    cwb = [cw_ref[k:k + 1, SSD_D_INNER:2 * SSD_D_INNER] for k in range(SSD_CONV)]
    return (*cwx, *cwb, cb_ref[:, 0:SSD_D_INNER], cb_ref[:, SSD_D_INNER:2 * SSD_D_INNER],
            dtb_ref[...], alog_ref[...], dsk_ref[...], ng_ref[...])


def _ssd_in_specs(rev, nc):
    def cidx(i):
        return (nc - 1 - i) if rev else i

    def halo(cb):
        return pl.BlockSpec((8, SSD_D_INNER), lambda i: (jnp.maximum(16 * cidx(i) - 1, 0), cb))

    return [
        pl.BlockSpec((CHUNK, SSD_D_INNER), lambda i: (cidx(i), 0)),
        pl.BlockSpec((CHUNK, SSD_D_INNER), lambda i: (cidx(i), 1)),
        pl.BlockSpec((CHUNK, SSD_D_INNER), lambda i: (cidx(i), 2)),
        halo(1), halo(2),
        pl.BlockSpec((CHUNK, 128), lambda i: (cidx(i), 48)),
    ]


class _Rider:
    def __init__(self, operands, out_shapes, scratch, start, finish):
        self.operands, self.out_shapes, self.scratch = list(operands), list(out_shapes), list(scratch)
        self.start, self.finish = start, finish


def _rider_split(rider, refs, n_in, n_out, n_scratch):
    if rider is None:
        return refs, None
    ni, no = len(rider.operands), len(rider.out_shapes)
    own = refs[:n_in] + refs[n_in + ni:n_in + ni + n_out] + refs[n_in + ni + n_out + no:n_in + ni + n_out + no + n_scratch]
    mine = (refs[n_in:n_in + ni], refs[n_in + ni + n_out:n_in + ni + n_out + no],
            refs[n_in + ni + n_out + no + n_scratch:])
    return own, mine


def _rider_args(rider):
    if rider is None:
        return [], [], [], []
    hbm = pl.BlockSpec(memory_space=pl.ANY)
    return ([hbm] * len(rider.operands), [hbm] * len(rider.out_shapes), rider.out_shapes, rider.scratch)


def _ssd_fwd(zxd, consts, name, rider=None):
    lp = zxd.shape[0]
    nc = lp // CHUNK

    def body(*refs):
        own, ride = _rider_split(rider, refs, 12, 2, 1)
        (z_ref, xs_ref, bc_ref, hx_ref, hb_ref, dt_ref, cw_ref, cb_ref, dtb_ref, alog_ref,
         dsk_ref, ng_ref, y_ref, st_ref, state) = own
        c = pl.program_id(0)

        @pl.when(c == 0)
        def _():
            state[...] = jnp.zeros_like(state)
            if ride is not None:
                rider.start(*ride)

        live = (c > 0).astype(F32)
        st_ref[0] = state[...]
        out, st_new = _ssd_chunk(
            _row_mask(c, CHUNK), z_ref[...], xs_ref[...], bc_ref[...], hx_ref[...] * live,
            hb_ref[...] * live, dt_ref[...], state[...],
            *_ssd_param_vals(cw_ref, cb_ref, dtb_ref, alog_ref, dsk_ref, ng_ref))
        y_ref[...] = out.astype(y_ref.dtype)
        state[...] = st_new

        if ride is not None:
            @pl.when(c == nc - 1)
            def _():
                rider.finish(*ride)

    r_in, r_out, r_shapes, r_scratch = _rider_args(rider)
    return pl.pallas_call(
        body, name=name,
        out_shape=[jax.ShapeDtypeStruct((lp, SSD_D_INNER), BF16),
                   jax.ShapeDtypeStruct((nc, SSD_GROUPS * SSD_STATE, 256), F32)] + r_shapes,
        grid=(nc,),
        in_specs=_ssd_in_specs(False, nc) + [pl.BlockSpec(c.shape, lambda i: (0, 0)) for c in consts] + r_in,
        out_specs=[pl.BlockSpec((CHUNK, SSD_D_INNER), lambda i: (i, 0)),
                   pl.BlockSpec((1, SSD_GROUPS * SSD_STATE, 256), lambda i: (i, 0, 0))] + r_out,
        scratch_shapes=[pltpu.VMEM((SSD_GROUPS * SSD_STATE, 256), F32)] + r_scratch,
        compiler_params=_cparams(dimension_semantics=("arbitrary",)),
    )(zxd, zxd, zxd, zxd, zxd, zxd, *consts, *(rider.operands if rider else ()))


def _ssd_bwd(zxd, states, d_y, consts, name):
    lp = zxd.shape[0]
    nc = lp // CHUNK

    def body(z_ref, xs_ref, bc_ref, hx_ref, hb_ref, dt_ref, st_ref, dy_ref, cw_ref, cb_ref, dtb_ref,
             alog_ref, dsk_ref, ng_ref, dz_ref, dcw_ref, dcb_ref, ddtb_ref, dalog_ref, ddsk_ref,
             dng_ref, d_state, d_hx, d_hb):
        i = pl.program_id(0)
        c = nc - 1 - i

        @pl.when(i == 0)
        def _():
            d_state[...] = jnp.zeros_like(d_state)
            d_hx[...] = jnp.zeros_like(d_hx)
            d_hb[...] = jnp.zeros_like(d_hb)
            for r in (dcw_ref, dcb_ref, ddtb_ref, dalog_ref, ddsk_ref, dng_ref):
                r[...] = jnp.zeros_like(r)

        live = (c > 0).astype(F32)
        fn = functools.partial(_ssd_chunk, _row_mask(c, CHUNK))
        prim = (z_ref[...], xs_ref[...], bc_ref[...], hx_ref[...] * live, hb_ref[...] * live,
                dt_ref[...], st_ref[0],
                *_ssd_param_vals(cw_ref, cb_ref, dtb_ref, alog_ref, dsk_ref, ng_ref))
        _, vjp = jax.vjp(fn, *prim)
        (d_z, d_xs, d_bc, g_hx, g_hb, d_dt, g_st, *d_par) = vjp((dy_ref[...], d_state[...]))
        zeros = jnp.zeros((CHUNK - 8, SSD_D_INNER), F32)
        d_xs = d_xs + jnp.concatenate([zeros, d_hx[...]], axis=0)
        d_bc = d_bc + jnp.concatenate([zeros, d_hb[...]], axis=0)
        dz_ref[:, 0:SSD_D_INNER] = d_z.astype(dz_ref.dtype)
        dz_ref[:, SSD_D_INNER:2 * SSD_D_INNER] = d_xs.astype(dz_ref.dtype)
        dz_ref[:, 2 * SSD_D_INNER:3 * SSD_D_INNER] = d_bc.astype(dz_ref.dtype)
        dz_ref[:, 3 * SSD_D_INNER:] = d_dt.astype(dz_ref.dtype)
        d_state[...] = g_st
        d_hx[...] = g_hx * live
        d_hb[...] = g_hb * live
        for k in range(SSD_CONV):
            dcw_ref[k:k + 1, 0:SSD_D_INNER] += d_par[k]
            dcw_ref[k:k + 1, SSD_D_INNER:2 * SSD_D_INNER] += d_par[SSD_CONV + k]
        dcb_ref[:, 0:SSD_D_INNER] += d_par[8]
        dcb_ref[:, SSD_D_INNER:2 * SSD_D_INNER] += d_par[9]
        ddtb_ref[...] += d_par[10]
        dalog_ref[...] += d_par[11]
        ddsk_ref[...] += d_par[12]
        dng_ref[...] += d_par[13]

    const_specs = [pl.BlockSpec(c.shape, lambda i: (0, 0)) for c in consts]
    return pl.pallas_call(
        body, name=name,
        out_shape=[jax.ShapeDtypeStruct((lp, SSD_IN_PAD), BF16)]
        + [jax.ShapeDtypeStruct(c.shape, F32) for c in consts],
        grid=(nc,),
        in_specs=_ssd_in_specs(True, nc)
        + [pl.BlockSpec((1, SSD_GROUPS * SSD_STATE, 256), lambda i: (nc - 1 - i, 0, 0)),
           pl.BlockSpec((CHUNK, SSD_D_INNER), lambda i: (nc - 1 - i, 0))] + const_specs,
        out_specs=[pl.BlockSpec((CHUNK, SSD_IN_PAD), lambda i: (nc - 1 - i, 0))] + const_specs,
        scratch_shapes=[pltpu.VMEM((SSD_GROUPS * SSD_STATE, 256), F32),
                        pltpu.VMEM((8, SSD_D_INNER), F32), pltpu.VMEM((8, SSD_D_INNER), F32)],
        compiler_params=_cparams(dimension_semantics=("arbitrary",)),
    )(zxd, zxd, zxd, zxd, zxd, zxd, states, d_y, *consts)


@jax.custom_vjp
def _rot_half(x):
    lane = lax.broadcasted_iota(jnp.int32, x.shape, 1)
    lo = (lane >= MLA_NOPE) & (lane < MLA_NOPE + MLA_ROPE // 2)
    hi = (lane >= MLA_NOPE + MLA_ROPE // 2) & (lane < MLA_QK)
    down = pltpu.roll(x, HEAD_SLOT - MLA_ROPE // 2, 1)
    up = pltpu.roll(x, MLA_ROPE // 2, 1)
    return jnp.where(lo, -down, jnp.where(hi, up, 0.0))


def _rot_half_fwd(x):
    return _rot_half(x), None


def _rot_half_bwd(_, ct):
    return (-_rot_half(ct),)


_rot_half.defvjp(_rot_half_fwd, _rot_half_bwd)


def _head_norm_rope(t, gain, cos, sin):
    n = t * lax.rsqrt(jnp.sum(t * t, axis=-1, keepdims=True) * (1.0 / MLA_QK) + EPS) * gain
    return n * cos + _rot_half(n) * sin


def _qk_prep(q_raw, kn_raw, kpe, cos, sin, qg, kg):
    qs, ks = [], []
    for h in range(MLA_HEADS):
        sl = slice(h * HEAD_SLOT, (h + 1) * HEAD_SLOT)
        qs.append(_head_norm_rope(q_raw[:, sl], qg, cos, sin))
        ks.append(_head_norm_rope(kn_raw[:, sl] + kpe, kg, cos, sin))
    return jnp.concatenate(qs, axis=1), jnp.concatenate(ks, axis=1)


def _lat_norm(kv_lat, q_lat, kvg, qg):
    return _rms(kv_lat, kvg), _rms(q_lat, qg)


_NEG = -1e30
_SCALE = MLA_QK ** -0.5


STRIP = 128
_EXP2_SCALE = _SCALE * math.log2(math.e)


def _strip_mask(kind, blk, c, t):
    if kind is None:
        return None
    kpos = blk * t + c * STRIP + lax.broadcasted_iota(jnp.int32, (1, STRIP), 1)
    if kind == 'keys':
        return kpos >= NPAD
    qpos = blk * t + lax.broadcasted_iota(jnp.int32, (t, 1), 0)
    return (kpos <= qpos) & ((kpos >= NPAD) | (kpos == qpos))


def _attn_fwd(q, k, v, name, rider=None):
    lp = q.shape[0]
    t = _pick(lp, (384, 256, 128))
    nb = lp // t
    hp = HEADS_PER_STEP
    wide = hp * HEAD_SLOT
    heads = [slice(a * HEAD_SLOT, (a + 1) * HEAD_SLOT) for a in range(hp)]

    def body(*refs):
        (q_ref, k_ref, v_ref, o_ref, lse_ref), ride = _rider_split(rider, refs, 3, 2, 0)
        qi = pl.program_id(1)
        if ride is not None:
            @pl.when((pl.program_id(0) == 0) & (qi == 0))
            def _():
                rider.start(*ride)

        def update(a, ki, carry, mask):
            rows = pl.ds(pl.multiple_of(ki * t, t), t)
            m, l, acc = carry
            s = lax.dot_general(q_ref[:, heads[a]], k_ref[rows, heads[a]], _NT, preferred_element_type=F32)
            if mask is not None:
                s = jnp.where(mask, s, _NEG)
            m_new = jnp.maximum(m, jnp.max(s, axis=-1, keepdims=True))
            alpha = jnp.exp2((m - m_new) * _EXP2_SCALE)
            p = jnp.exp2((s - m_new) * _EXP2_SCALE)
            l = alpha * l + jnp.sum(p, axis=-1, keepdims=True)
            acc = alpha * acc + lax.dot_general(p.astype(BF16), v_ref[rows, heads[a]], _NN,
                                                preferred_element_type=F32)
            return m_new, l, acc

        def step(ki, carry, mask):
            return tuple(update(a, ki, carry[a], mask) for a in range(hp))

        init = (jnp.full((t, 1), _NEG, F32), jnp.zeros((t, 1), F32), jnp.zeros((t, HEAD_SLOT), F32))
        key_ok = lax.broadcasted_iota(jnp.int32, (1, t), 1) >= NPAD
        carry = lax.cond(qi > 0, lambda c: step(0, c, key_ok), lambda c: c, (init,) * hp)
        carry = lax.fori_loop(1, qi, lambda ki, c: step(ki, c, None), carry)
        qpos = lax.broadcasted_iota(jnp.int32, (t, t), 0)
        kpos = lax.broadcasted_iota(jnp.int32, (t, t), 1)
        diag = (kpos <= qpos) & ((qi * t + kpos >= NPAD) | (kpos == qpos))
        for a in range(hp):
            m, l, acc = update(a, qi, carry[a], diag)
            o_ref[:, heads[a]] = acc / l * _row_mask(qi, t)
            lse_ref[a] = m * _SCALE + jnp.log(l)

        if ride is not None:
            @pl.when((pl.program_id(0) == MLA_HEADS // hp - 1) & (qi == nb - 1))
            def _():
                rider.finish(*ride)

    qspec = pl.BlockSpec((t, wide), lambda g, i: (i, g))
    kspec = pl.BlockSpec((lp, wide), lambda g, i: (0, g))
    r_in, r_out, r_shapes, r_scratch = _rider_args(rider)
    return pl.pallas_call(
        body, name=name,
        out_shape=[jax.ShapeDtypeStruct((lp, MLA_WIDE), F32),
                   jax.ShapeDtypeStruct((MLA_HEADS, lp, 1), F32)] + r_shapes,
        grid=(MLA_HEADS // hp, nb),
        in_specs=[qspec, kspec, kspec] + r_in,
        out_specs=[qspec, pl.BlockSpec((hp, t, 1), lambda g, i: (g, i, 0))] + r_out,
        scratch_shapes=r_scratch,
        compiler_params=_cparams(dimension_semantics=("arbitrary", "arbitrary")),
    )(q, k, v, *(rider.operands if rider else ()))


def _attn_delta(do, o, name):
    lp = do.shape[0]
    t = _pick(lp, (384, 256, 128))

    def body(do_ref, o_ref, dob_ref, delta_ref):
        dob_ref[...] = do_ref[...].astype(BF16)
        for h in range(MLA_HEADS):
            sl = slice(h * HEAD_SLOT, (h + 1) * HEAD_SLOT)
            delta_ref[h] = jnp.sum(do_ref[:, sl] * o_ref[:, sl], axis=-1, keepdims=True)

    spec = pl.BlockSpec((t, MLA_WIDE), lambda i: (i, 0))
    return pl.pallas_call(
        body, name=name,
        out_shape=[jax.ShapeDtypeStruct((lp, MLA_WIDE), BF16), jax.ShapeDtypeStruct((MLA_HEADS, lp, 1), F32)],
        grid=(lp // t,), in_specs=[spec, spec],
        out_specs=[spec, pl.BlockSpec((MLA_HEADS, t, 1), lambda i: (0, i, 0))],
        compiler_params=_cparams(dimension_semantics=("parallel",)),
    )(do, o)


def _attn_bwd(q, k, v, do, lse, delta, name):
    lp = q.shape[0]
    t = _pick(lp, (384, 256, 128))
    nb = lp // t
    ns = t // STRIP
    hp = HEADS_PER_STEP
    wide = hp * HEAD_SLOT
    heads = [slice(a * HEAD_SLOT, (a + 1) * HEAD_SLOT) for a in range(hp)]
    log2e = math.log2(math.e)

    def body(q_ref, k_ref, v_ref, do_ref, lse_ref, delta_ref, dq_ref, dk_ref, dv_ref,
             s_scr, dp_scr, p_scr, ds_scr):
        kj = pl.program_id(1)

        @pl.when(kj == 0)
        def _():
            dq_ref[...] = jnp.zeros_like(dq_ref)

        dk_ref[...] = jnp.zeros_like(dk_ref)
        dv_ref[...] = jnp.zeros_like(dv_ref)

        def tile(qi, kind):
            rows = pl.ds(pl.multiple_of(qi * t, t), t)
            for a in range(hp):
                qb, dob = q_ref[rows, heads[a]], do_ref[rows, heads[a]]
                kb, vb = k_ref[:, heads[a]], v_ref[:, heads[a]]
                s_scr[a] = lax.dot_general(qb, kb, _NT, preferred_element_type=F32)
                dp_scr[a] = lax.dot_general(dob, vb, _NT, preferred_element_type=F32)
                lse2 = lse_ref[a, rows, :] * log2e
                delta = delta_ref[a, rows, :]
                for c in range(ns):
                    cs = slice(c * STRIP, (c + 1) * STRIP)
                    pc = jnp.exp2(s_scr[a, :, cs] * _EXP2_SCALE - lse2)
                    pc = jnp.where(_strip_mask(kind, kj, c, t), pc, 0.0)
                    p_scr[a, :, cs] = pc.astype(BF16)
                    ds_scr[a, :, cs] = (pc * (dp_scr[a, :, cs] - delta)).astype(BF16)
                dq_ref[rows, heads[a]] += lax.dot_general(ds_scr[a], kb, _NN,
                                                          preferred_element_type=F32) * _SCALE
                dv_ref[:, heads[a]] += lax.dot_general(p_scr[a], dob, _TN, preferred_element_type=F32)
                dk_ref[:, heads[a]] += lax.dot_general(ds_scr[a], qb, _TN, preferred_element_type=F32)

        tile(kj, 'diag')

        def below(qi, carry):
            tile(qi, 'keys')
            return carry

        lax.fori_loop(kj + 1, nb, below, 0)
        dk_ref[...] = dk_ref[...] * _SCALE

    whole = pl.BlockSpec((lp, wide), lambda g, j: (0, g))
    kspec = pl.BlockSpec((t, wide), lambda g, j: (j, g))
    stat = pl.BlockSpec((hp, lp, 1), lambda g, j: (g, 0, 0))
    return pl.pallas_call(
        body, name=name,
        out_shape=[jax.ShapeDtypeStruct((lp, MLA_WIDE), F32)] * 3,
        grid=(MLA_HEADS // hp, nb),
        in_specs=[whole, kspec, kspec, whole, stat, stat],
        out_specs=[whole, kspec, kspec],
        scratch_shapes=[pltpu.VMEM((hp, t, t), F32), pltpu.VMEM((hp, t, t), F32),
                        pltpu.VMEM((hp, t, t), BF16), pltpu.VMEM((hp, t, t), BF16)],
        compiler_params=_cparams(dimension_semantics=("parallel", "arbitrary")),
    )(q, k, v, do, lse, delta)


def _rope_tables(lp):
    inv = 1.0 / (ROPE_THETA ** (jnp.arange(0, MLA_ROPE, 2, dtype=F32) / MLA_ROPE))
    pos = jnp.maximum(jnp.arange(lp, dtype=jnp.int32) - NPAD, 0).astype(F32)
    ang = pos[:, None] * inv[None, :]
    cos, sin = jnp.cos(ang), jnp.sin(ang)
    z32 = jnp.zeros((lp, HEAD_SLOT - MLA_QK), F32)
    cos_t = jnp.concatenate([jnp.ones((lp, MLA_NOPE), F32), cos, cos, z32], axis=1)
    sin_t = jnp.concatenate([jnp.zeros((lp, MLA_NOPE), F32), sin, sin, z32], axis=1)
    return cos_t, sin_t


def _loss_head(h, target, name):
    lp = h.shape[0]

    def body(h_ref, t_ref, d_ref, loss_ref):
        i = pl.program_id(0)

        @pl.when(i == 0)
        def _():
            d_ref[...] = jnp.zeros_like(d_ref)
            loss_ref[...] = jnp.zeros_like(loss_ref)

        @pl.when(i > 0)
        def _():
            err = h_ref[...] - t_ref[...]
            d_ref[...] = err * (1.0 / D_MODEL)
            loss_ref[...] += jnp.sum(err * err, axis=0, keepdims=True) * (0.5 / D_MODEL)

    return pl.pallas_call(
        body, name=name,
        out_shape=[jax.ShapeDtypeStruct((lp, D_MODEL), F32), jax.ShapeDtypeStruct((1, D_MODEL), F32)],
        grid=(lp // CHUNK,),
        in_specs=[pl.BlockSpec((CHUNK, D_MODEL), lambda i: (i, 0)),
                  pl.BlockSpec((CHUNK, D_MODEL), lambda i: (jnp.maximum(i - 1, 0), 0))],
        out_specs=[pl.BlockSpec((CHUNK, D_MODEL), lambda i: (i, 0)),
                   pl.BlockSpec((1, D_MODEL), lambda i: (0, 0))],
        compiler_params=_cparams(dimension_semantics=("arbitrary",)),
    )(h, target)


def _pad_cols(w, n):
    return jnp.pad(w, [(0, 0)] * (w.ndim - 1) + [(0, n - w.shape[-1])])


def _layer_slab(name, i):
    return i if name.startswith('mlp_') else i // 2


def _prep_layer(p, i, get):
    j = i // 2
    if i % 2 == 0:
        p['ssd_in'][j] = _pad_cols(get('ssd_w_in'), SSD_IN_PAD).astype(BF16)
        p['ssd_out'][j] = get('ssd_w_out').astype(BF16)
    else:
        wi = get('mla_w_in')
        kpe = jnp.pad(wi[:, MLA_Q_RANK + MLA_KV_RANK:], ((0, 0), (MLA_NOPE, HEAD_SLOT - MLA_QK)))
        p['mla_in'][j] = jnp.concatenate(
            [wi[:, MLA_Q_RANK:MLA_Q_RANK + MLA_KV_RANK], kpe, wi[:, :MLA_Q_RANK]], axis=1).astype(BF16)
        qb = get('mla_w_q_b').reshape(MLA_Q_RANK, MLA_HEADS, MLA_QK)
        p['mla_qb'][j] = _pad_cols(qb, HEAD_SLOT).reshape(MLA_Q_RANK, MLA_WIDE).astype(BF16)
        kvb = get('mla_w_kv_b').reshape(MLA_KV_RANK, MLA_HEADS, MLA_NOPE + MLA_V)
        kn = _pad_cols(kvb[:, :, :MLA_NOPE], HEAD_SLOT).reshape(MLA_KV_RANK, MLA_WIDE)
        vv = _pad_cols(kvb[:, :, MLA_NOPE:], HEAD_SLOT).reshape(MLA_KV_RANK, MLA_WIDE)
        p['mla_kvb'][j] = jnp.concatenate([kn, vv], axis=1).astype(BF16)
        wo = get('mla_w_out').reshape(MLA_HEADS, MLA_V, D_MODEL)
        p['mla_out'][j] = (jnp.pad(wo, ((0, 0), (0, HEAD_SLOT - MLA_V), (0, 0)))
                           .reshape(MLA_WIDE, D_MODEL).astype(BF16))
    p['up'][i] = get('mlp_w_up').astype(BF16)
    p['down'][i] = get('mlp_w_down').astype(BF16)


def _no_matrices():
    return {k: [None] * n for k, n in (('ssd_in', 2), ('ssd_out', 2), ('mla_in', 2), ('mla_qb', 2),
                                       ('mla_kvb', 2), ('mla_out', 2), ('up', 4), ('down', 4))}


class _ReadyWeights:
    def __init__(self, w):
        self.w, self.p = w, _no_matrices()

    def ensure(self, i):
        _prep_layer(self.p, i, lambda n: self.w[n][_layer_slab(n, i)])

    def rider(self, i):
        return None

    def deliver(self, i, outs):
        assert not outs


def _pad128(v):
    return _pad_cols(v.reshape(1, -1), 128)


def _sqrelu(u):
    r = jnp.maximum(u, 0.0)
    return r * r


def _local_step(x, target, w, big=None):
    seq = x.shape[0]
    lp = NPAD + N_META + seq
    big = _ReadyWeights(w) if big is None else big
    p = big.p
    h = jnp.concatenate([jnp.zeros((NPAD, D_MODEL), F32), w['meta_tokens'], x], axis=0)
    cos_t, sin_t = _rope_tables(lp)
    rt = _pick(lp, (384, 256, 128))
    saved = []
    for i in range(4):
        j = i // 2
        big.ensure(i)
        s = {'h0': h}
        g_mix = w['ln_mix'][i].reshape(1, -1)
        hn = _rms_fwd(h, g_mix, f"rms_mix_f{i}")
        s['hn'] = hn
        if i % 2 == 0:
            zxd = _mm(hn, p['ssd_in'][j], 'nn', name=f"ssd_in_f{i}")
            consts = _ssd_consts(w['ssd_conv_w'][j], w['ssd_conv_b'][j].reshape(1, -1),
                                 _pad128(w['ssd_dt_bias'][j]), _pad128(w['ssd_a_log'][j]),
                                 _pad128(w['ssd_d'][j]), w['ssd_norm'][j].reshape(1, -1))
            yg, states, *got = _ssd_fwd(zxd, consts, f"ssd_core_f{i}", rider=big.rider(i))
            big.deliver(i, got)
            s.update(zxd=zxd, consts=consts, yg=yg, states=states)
            h = _mm(yg, p['ssd_out'][j], 'nn', name=f"ssd_out_f{i}", epi=lambda r, hv: hv + r, extras=(h,))
        else:
            lat = _mm(hn, p['mla_in'][j], 'nn', name=f"mla_in_f{i}")
            kvg = w['mla_kv_a_norm'][j].reshape(1, -1)
            qag = w['mla_q_a_norm'][j].reshape(1, -1)
            kvn, qn = _row_call(lambda _, a, b, c, d: _lat_norm(a, b, c, d),
                                [(lat, MLA_KV_RANK, 0), (lat, MLA_Q_RANK, 1)], [kvg, qag],
                                [(MLA_KV_RANK, BF16), (MLA_Q_RANK, BF16)], n_rows=lp, tile=rt,
                                name=f"mla_latnorm_f{i}")
            q_raw = _mm(qn, p['mla_qb'][j], 'nn', name=f"mla_qb_f{i}")
            kv_raw = _mm(kvn, p['mla_kvb'][j], 'nn', name=f"mla_kvb_f{i}")
            qg = _pad_cols(w['mla_q_norm'][j].reshape(1, -1), HEAD_SLOT)
            kg = _pad_cols(w['mla_k_norm'][j].reshape(1, -1), HEAD_SLOT)

            def prep_fwd(_, qr, kn, kpe, vv, cs, sn, qgv, kgv):
                qq, kk = _qk_prep(qr, kn, kpe, cs, sn, qgv, kgv)
                return qq, kk, vv

            q, k, v = _row_call(prep_fwd,
                                [(q_raw, MLA_WIDE, 0), (kv_raw, MLA_WIDE, 0), (lat, HEAD_SLOT, 2),
                                 (kv_raw, MLA_WIDE, 1), (cos_t, HEAD_SLOT, 0), (sin_t, HEAD_SLOT, 0)],
                                [qg, kg], [(MLA_WIDE, BF16)] * 3, n_rows=lp, tile=rt,
                                name=f"mla_qkprep_f{i}")
            o, lse, *got = _attn_fwd(q, k, v, f"mla_attn_f{i}", rider=big.rider(i))
            big.deliver(i, got)
            s.update(lat=lat, kvg=kvg, qag=qag, kvn=kvn, qn=qn, q_raw=q_raw, kv_raw=kv_raw, qg=qg, kg=kg,
                     q=q, k=k, v=v, o=o, lse=lse)
            h = _mm(o, p['mla_out'][j], 'nn', name=f"mla_out_f{i}", epi=lambda r, hv: hv + r, extras=(h,))
        s['h1'] = h
        g_mlp = w['ln_mlp'][i].reshape(1, -1)
        hn2 = _rms_fwd(h, g_mlp, f"rms_mlp_f{i}")
        u = _mm(hn2, p['up'][i], 'nn', name=f"mlp_up_f{i}")
        h = _mm(u, p['down'][i], 'nn', name=f"mlp_down_f{i}", a_fn=_sqrelu,
                epi=lambda r, hv: hv + r, extras=(h,))
        s.update(hn2=hn2, u=u, g_mix=g_mix, g_mlp=g_mlp)
        saved.append(s)

    dh, loss_row = _loss_head(h, target, "loss_head")

    g = {k_: [None] * (4 if k_ in ('ln_mix', 'ln_mlp', 'mlp_w_up', 'mlp_w_down') else 2)
         for k_ in ALL_NAMES if k_ != 'meta_tokens'}
    stacked = {}

    def dw_into(nm, slab, a, b, **kw):
        stacked[nm] = _mm(a, b, 'tn', stack=(len(g[nm]), slab, stacked.get(nm)), **kw)

    for i in reversed(range(4)):
        j = i // 2
        s = saved[i]
        dw_into('mlp_w_down', i, s['u'], dh, name=f"mlp_down_dw{i}", a_fn=_sqrelu)
        du = _mm(dh, p['down'][i], 'nt', name=f"mlp_down_dx{i}", out_dtype=BF16,
                 epi=lambda r, uv: r * (2.0 * jnp.maximum(uv, 0.0)), extras=(s['u'],))
        dw_into('mlp_w_up', i, s['hn2'], du, name=f"mlp_up_dw{i}")
        d_hn2 = _mm(du, p['up'][i], 'nt', name=f"mlp_up_dx{i}")
        dh, dg = _rms_bwd(s['h1'], s['g_mlp'], d_hn2, dh, f"rms_mlp_b{i}")
        g['ln_mlp'][i] = dg[0]
        if i % 2 == 0:
            dw_into('ssd_w_out', j, s['yg'], dh, name=f"ssd_out_dw{i}")
            d_yg = _mm(dh, p['ssd_out'][j], 'nt', name=f"ssd_out_dx{i}")
            d_zxd, dcw, dcb, ddtb, dalog, ddsk, dng = _ssd_bwd(s['zxd'], s['states'], d_yg, s['consts'],
                                                              f"ssd_core_b{i}")
            g['ssd_conv_w'][j], g['ssd_conv_b'][j], g['ssd_norm'][j] = dcw, dcb[0], dng[0]
            g['ssd_dt_bias'][j], g['ssd_a_log'][j], g['ssd_d'][j] = (
                ddtb[0, :SSD_HEADS], dalog[0, :SSD_HEADS], ddsk[0, :SSD_HEADS])
            dw_into('ssd_w_in', j, s['hn'], d_zxd, name=f"ssd_in_dw{i}")
            d_hn = _mm(d_zxd, p['ssd_in'][j], 'nt', name=f"ssd_in_dx{i}")
        else:
            wo = _mm(s['o'], dh, 'tn', name=f"mla_out_dw{i}")
            g['mla_w_out'][j] = wo.reshape(MLA_HEADS, HEAD_SLOT, D_MODEL)[:, :MLA_V].reshape(-1, D_MODEL)
            do = _mm(dh, p['mla_out'][j], 'nt', name=f"mla_out_dx{i}")
            dob, delta = _attn_delta(do, s['o'], f"mla_attn_delta{i}")
            dq, dk, dv = _attn_bwd(s['q'], s['k'], s['v'], dob, s['lse'], delta, f"mla_attn_b{i}")

            def prep_bwd(_, qr, kn, kpe, cs, sn, dqv, dkv, dvv, qgv, kgv):
                _, vjp = jax.vjp(lambda a, b, c, d, e: _qk_prep(a, b, c, cs, sn, d, e), qr, kn, kpe, qgv, kgv)
                d_qr, d_kn, d_kpe, d_qg, d_kg = vjp((dqv, dkv))
                return d_qr, jnp.concatenate([d_kn, dvv], axis=1), d_kpe, d_qg, d_kg

            d_qraw, d_kvraw, d_kpe, d_qg, d_kg = _row_call(
                prep_bwd,
                [(s['q_raw'], MLA_WIDE, 0), (s['kv_raw'], MLA_WIDE, 0), (s['lat'], HEAD_SLOT, 2),
                 (cos_t, HEAD_SLOT, 0), (sin_t, HEAD_SLOT, 0), (dq, MLA_WIDE, 0), (dk, MLA_WIDE, 0),
                 (dv, MLA_WIDE, 0)],
                [s['qg'], s['kg']], [(MLA_WIDE, BF16), (2 * MLA_WIDE, BF16), (HEAD_SLOT, F32)],
                [(1, HEAD_SLOT), (1, HEAD_SLOT)], n_rows=lp, tile=_pick(lp, (128,)), name=f"mla_qkprep_b{i}")
            g['mla_q_norm'][j], g['mla_k_norm'][j] = d_qg[0, :MLA_QK], d_kg[0, :MLA_QK]
            wqb = _mm(s['qn'], d_qraw, 'tn', name=f"mla_qb_dw{i}")
            g['mla_w_q_b'][j] = wqb.reshape(MLA_Q_RANK, MLA_HEADS, HEAD_SLOT)[:, :, :MLA_QK].reshape(MLA_Q_RANK, -1)
            d_qn = _mm(d_qraw, p['mla_qb'][j], 'nt', name=f"mla_qb_dx{i}")
            wkvb = _mm(s['kvn'], d_kvraw, 'tn', name=f"mla_kvb_dw{i}").reshape(MLA_KV_RANK, 2, MLA_HEADS, HEAD_SLOT)
            g['mla_w_kv_b'][j] = jnp.concatenate([wkvb[:, 0, :, :MLA_NOPE], wkvb[:, 1, :, :MLA_V]],
                                                 axis=-1).reshape(MLA_KV_RANK, -1)
            d_kvn = _mm(d_kvraw, p['mla_kvb'][j], 'nt', name=f"mla_kvb_dx{i}")

            def lat_bwd(_, kvl, ql, dkvn, dqn, dkpe, kvgv, qagv):
                _, vjp = jax.vjp(_lat_norm, kvl, ql, kvgv, qagv)
                d_kvl, d_ql, d_kvg, d_qag = vjp((dkvn, dqn))
                return jnp.concatenate([d_kvl, dkpe, d_ql], axis=1), d_kvg, d_qag

            d_lat, d_kvg, d_qag = _row_call(
                lat_bwd, [(s['lat'], MLA_KV_RANK, 0), (s['lat'], MLA_Q_RANK, 1), (d_kvn, MLA_KV_RANK, 0),
                          (d_qn, MLA_Q_RANK, 0), (d_kpe, HEAD_SLOT, 0)],
                [s['kvg'], s['qag']], [(LAT_PAD, BF16)], [(1, MLA_KV_RANK), (1, MLA_Q_RANK)],
                n_rows=lp, tile=rt, name=f"mla_latnorm_b{i}")
            g['mla_kv_a_norm'][j], g['mla_q_a_norm'][j] = d_kvg[0], d_qag[0]
            win = _mm(s['hn'], d_lat, 'tn', name=f"mla_in_dw{i}")
            g['mla_w_in'][j] = jnp.concatenate(
                [win[:, MLA_KV_RANK + HEAD_SLOT:], win[:, :MLA_KV_RANK],
                 win[:, MLA_KV_RANK + MLA_NOPE:MLA_KV_RANK + MLA_QK]], axis=1)
            d_hn = _mm(d_lat, p['mla_in'][j], 'nt', name=f"mla_in_dx{i}")
        dh, dg = _rms_bwd(s['h0'], s['g_mix'], d_hn, dh, f"rms_mix_b{i}")
        g['ln_mix'][i] = dg[0]

    grads = {k_: jnp.stack(v_) for k_, v_ in g.items() if k_ not in stacked}
    grads.update(stacked)
    grads['meta_tokens'] = dh[NPAD:NPAD + N_META]
    return loss_row, dh[NPAD + N_META:], grads


def _all_gather8(shard, name):
    m_per, n = shard.shape

    def body(x_ref, out_ref, send_sems, recv_sems, local_sem):
        x, y, c = lax.axis_index("x"), lax.axis_index("y"), lax.axis_index("c")
        me, sibling = (x, y, c), (x, y, 1 - c)
        chips = [(1 - x, y), (x, 1 - y), (1 - x, 1 - y)]

        def rows(px, py, pc):
            return out_ref.at[pl.ds((4 * px + 2 * py + pc) * m_per, m_per), :]

        def copy(k, block, to, src=None):
            return pltpu.make_async_remote_copy(
                src_ref=rows(*block) if src is None else src, dst_ref=rows(*block),
                send_sem=send_sems.at[k], recv_sem=recv_sems.at[k], device_id=to, device_id_type=MESH)

        mine = pltpu.make_async_copy(x_ref, rows(*me), local_sem)
        mine.start()
        first = [copy(0, me, sibling, src=x_ref)]
        first += [copy(1 + j, me, (*chip, c), src=x_ref) for j, chip in enumerate(chips)]
        for cp in first:
            cp.start()
        passed = [copy(4 + j, (*chip, c), sibling) for j, chip in enumerate(chips)]
        for j, chip in enumerate(chips):
            copy(1 + j, (*chip, c), me).wait_recv()
            passed[j].start()
        copy(0, sibling, me).wait_recv()
        for j, chip in enumerate(chips):
            copy(4 + j, (*chip, 1 - c), me).wait_recv()
        for cp in first + passed:
            cp.wait_send()
        mine.wait()

    return pl.pallas_call(
        body, name=name,
        out_shape=jax.ShapeDtypeStruct((8 * m_per, n), shard.dtype),
        in_specs=[pl.BlockSpec(memory_space=pl.ANY)],
        out_specs=pl.BlockSpec(memory_space=pl.ANY),
        scratch_shapes=[pltpu.SemaphoreType.DMA((7,)), pltpu.SemaphoreType.DMA((7,)), pltpu.SemaphoreType.DMA],
    )(shard)


def _mesh_pos():
    return lax.axis_index("x"), lax.axis_index("y"), lax.axis_index("c")


def _half_rows(pc, h):
    return pl.ds(pl.multiple_of(pc * h, 16), h)


def _whole_view(ref, kind, shard_shape, k, pc):
    _, r, c = shard_shape
    rows = _half_rows(pc, r // 2)
    if kind == 'row':
        return ref.at[:, k, rows, :]
    if kind == 'col':
        return ref.at[:, rows, pl.ds(pl.multiple_of(k * c, 128), c)]
    return ref.at[k, :, rows, :]


def _whole_shape(kind, shard_shape, rows=None):
    l, r, c = shard_shape
    r = r if rows is None else rows
    return {'row': (l, 4, r, c), 'col': (l, r, 4 * c), 'colx': (4, l, r, c)}[kind]


def _gather_rider(shards, kinds):
    n = len(shards)
    shapes = [s.shape for s in shards]

    def plan(ins, outs, sems):
        send_sems, recv_sems, local_sems = sems
        x, y, c = _mesh_pos()
        me, sibling = (x, y, c), (x, y, 1 - c)
        chips = [(1 - x, y), (x, 1 - y), (1 - x, 1 - y)]

        def place(a, px, py, pc):
            return _whole_view(outs[a], kinds[a], shapes[a], 2 * px + py, pc)

        def own(a):
            return ins[a].at[:, _half_rows(c, shapes[a][1] // 2), :]

        def copy(a, k, block, to, src=None):
            return pltpu.make_async_remote_copy(
                src_ref=place(a, *block) if src is None else src, dst_ref=place(a, *block),
                send_sem=send_sems.at[7 * a + k], recv_sem=recv_sems.at[7 * a + k],
                device_id=to, device_id_type=MESH)

        mine = [pltpu.make_async_copy(own(a), place(a, *me), local_sems.at[a]) for a in range(n)]
        first = [copy(a, 1 + j, me, (*chip, c), src=own(a)) for j, chip in enumerate(chips) for a in range(n)]
        first += [copy(a, 0, me, sibling, src=own(a)) for a in range(n)]
        return copy, mine, first, chips, me, sibling, c

    def start(ins, outs, sems):
        _, mine, first, *_ = plan(ins, outs, sems)
        for cp in first + mine:
            cp.start()

    def finish(ins, outs, sems):
        copy, mine, first, chips, me, sibling, c = plan(ins, outs, sems)
        passed = []
        for j, chip in enumerate(chips):
            for a in range(n):
                copy(a, 1 + j, (*chip, c), me).wait_recv()
                passed.append(copy(a, 4 + j, (*chip, c), sibling))
                passed[-1].start()
        for a in range(n):
            copy(a, 0, sibling, me).wait_recv()
        for j, chip in enumerate(chips):
            for a in range(n):
                copy(a, 4 + j, (*chip, 1 - c), me).wait_recv()
        for cp in first + passed:
            cp.wait_send()
        for cp in mine:
            cp.wait()

    return _Rider(
        shards, [jax.ShapeDtypeStruct(_whole_shape(k, s.shape), s.dtype) for k, s in zip(kinds, shards)],
        [pltpu.SemaphoreType.DMA((7 * n,)), pltpu.SemaphoreType.DMA((7 * n,)), pltpu.SemaphoreType.DMA((n,))],
        start, finish)


def _run_rider(rider, name):
    ni, no = len(rider.operands), len(rider.out_shapes)

    def body(*refs):
        ride = (refs[:ni], refs[ni:ni + no], refs[ni + no:])
        rider.start(*ride)
        rider.finish(*ride)

    return pl.pallas_call(
        body, name=name, out_shape=rider.out_shapes,
        in_specs=[pl.BlockSpec(memory_space=pl.ANY)] * ni,
        out_specs=[pl.BlockSpec(memory_space=pl.ANY)] * no,
        scratch_shapes=rider.scratch,
    )(*rider.operands)


def _rs_swap(wholes, kinds, shapes, name):
    n = len(wholes)

    def body(*refs):
        ins, outs = refs[:n], refs[n:2 * n]
        send_sems, recv_sems = refs[2 * n:]
        x, y, c = _mesh_pos()
        cps = []
        for a in range(n):
            rows = _half_rows(1 - c, shapes[a][1] // 2)
            src = ins[a].at[:, rows, :] if kinds[a] == 'col' else ins[a].at[:, :, rows, :]
            cps.append(pltpu.make_async_remote_copy(
                src_ref=src, dst_ref=outs[a], send_sem=send_sems.at[a], recv_sem=recv_sems.at[a],
                device_id=(x, y, 1 - c), device_id_type=MESH))
        for cp in cps:
            cp.start()
        for cp in cps:
            cp.wait()

    return pl.pallas_call(
        body, name=name,
        out_shape=[jax.ShapeDtypeStruct(_whole_shape(k, s, s[1] // 2), w.dtype)
                   for k, s, w in zip(kinds, shapes, wholes)],
        in_specs=[pl.BlockSpec(memory_space=pl.ANY)] * n,
        out_specs=[pl.BlockSpec(memory_space=pl.ANY)] * n,
        scratch_shapes=[pltpu.SemaphoreType.DMA((n,)), pltpu.SemaphoreType.DMA((n,))],
    )(*wholes)


def _rs_exchange(parts, kinds, shapes, name):
    n = len(parts)

    def body(*refs):
        ins, outs = refs[:n], refs[n:2 * n]
        send_sems, recv_sems, local_sems = refs[2 * n:]
        x, y, c = _mesh_pos()
        kme = 2 * x + y
        chips = [(1 - x, y), (x, 1 - y), (1 - x, 1 - y)]

        def slab(a, k):
            if kinds[a] == 'row':
                return ins[a].at[:, k]
            if kinds[a] == 'col':
                cw = shapes[a][2]
                return ins[a].at[:, :, pl.ds(pl.multiple_of(k * cw, 128), cw)]
            return ins[a].at[k]

        cps = [pltpu.make_async_remote_copy(
            src_ref=slab(a, 2 * px + py), dst_ref=outs[a].at[kme], send_sem=send_sems.at[3 * a + j],
            recv_sem=recv_sems.at[3 * a + j], device_id=(px, py, c), device_id_type=MESH)
            for j, (px, py) in enumerate(chips) for a in range(n)]
        cps_local = [pltpu.make_async_copy(slab(a, kme), outs[a].at[kme], local_sems.at[a]) for a in range(n)]
        for cp in cps + cps_local:
            cp.start()
        for cp in cps + cps_local:
            cp.wait()

    return pl.pallas_call(
        body, name=name,
        out_shape=[jax.ShapeDtypeStruct((4, s[0], s[1] // 2, s[2]), p.dtype) for s, p in zip(shapes, parts)],
        in_specs=[pl.BlockSpec(memory_space=pl.ANY)] * n,
        out_specs=[pl.BlockSpec(memory_space=pl.ANY)] * n,
        scratch_shapes=[pltpu.SemaphoreType.DMA((3 * n,)), pltpu.SemaphoreType.DMA((3 * n,)),
                        pltpu.SemaphoreType.DMA((n,))],
    )(*parts)


def _rs_share(shards, name):
    n = len(shards)

    def body(*refs):
        outs = refs[n:2 * n]
        send_sems, recv_sems = refs[2 * n:]
        x, y, c = _mesh_pos()
        cps = []
        for a in range(n):
            rows = outs[a].at[:, _half_rows(c, shards[a].shape[1] // 2), :]
            cps.append(pltpu.make_async_remote_copy(
                src_ref=rows, dst_ref=rows, send_sem=send_sems.at[a], recv_sem=recv_sems.at[a],
                device_id=(x, y, 1 - c), device_id_type=MESH))
        for cp in cps:
            cp.start()
        for cp in cps:
            cp.wait()

    return pl.pallas_call(
        body, name=name,
        out_shape=[jax.ShapeDtypeStruct(s.shape, s.dtype) for s in shards],
        in_specs=[pl.BlockSpec(memory_space=pl.ANY)] * n,
        out_specs=[pl.BlockSpec(memory_space=pl.ANY)] * n,
        input_output_aliases={a: a for a in range(n)},
        scratch_shapes=[pltpu.SemaphoreType.DMA((n,)), pltpu.SemaphoreType.DMA((n,))],
    )(*shards)


def _tile_rows(rows, cols, budget=2 * 1024 * 1024):
    for t in (1024, 512, 256, 128, 64, 32, 16, 8):
        if rows % t == 0 and t * cols * 4 <= budget:
            return t
    return rows


def _add_half(g3, r3, c_idx, name):
    a, h, n = r3.shape
    t = _tile_rows(h, n)
    nt = h // t

    def body(c_ref, g_ref, r_ref, o_ref):
        o_ref[...] = (g_ref[...] + r_ref[...]).astype(o_ref.dtype)

    return pl.pallas_call(
        body, name=name, out_shape=jax.ShapeDtypeStruct((a, h, n), BF16),
        grid_spec=pltpu.PrefetchScalarGridSpec(
            num_scalar_prefetch=1, grid=(a, nt),
            in_specs=[pl.BlockSpec((1, t, n), lambda k, i, c: (k, c[0] * nt + i, 0)),
                      pl.BlockSpec((1, t, n), lambda k, i, c: (k, i, 0))],
            out_specs=pl.BlockSpec((1, t, n), lambda k, i, c: (k, i, 0))),
        compiler_params=_cparams(dimension_semantics=("parallel", "parallel")),
    )(c_idx, g3, r3)


def _sum4(parts, c_idx, name):
    _, l, h, n = parts.shape
    t = _tile_rows(h, n, 1024 * 1024)
    nt = h // t

    def body(c_ref, p_ref, o_ref):
        pv = p_ref[...].astype(F32)
        o_ref[...] = ((pv[0] + pv[1]) + pv[2]) + pv[3]

    return pl.pallas_call(
        body, name=name, out_shape=jax.ShapeDtypeStruct((l, 2 * h, n), F32),
        grid_spec=pltpu.PrefetchScalarGridSpec(
            num_scalar_prefetch=1, grid=(l, nt),
            in_specs=[pl.BlockSpec((4, 1, t, n), lambda k, i, c: (0, k, i, 0))],
            out_specs=pl.BlockSpec((1, t, n), lambda k, i, c: (k, c[0] * nt + i, 0))),
        compiler_params=_cparams(dimension_semantics=("parallel", "parallel")),
    )(c_idx, parts)


def _sum8(parts, name):
    _, m, n = parts.shape

    def body(p_ref, o_ref):
        acc = p_ref[0]
        for d in range(1, 8):
            acc = acc + p_ref[d]
        o_ref[...] = acc

    return pl.pallas_call(body, name=name, out_shape=jax.ShapeDtypeStruct((m, n), F32))(parts)


def _adamw(wp, gp, mp, vp, name):
    r, n = wp.shape
    t = _tile_rows(r, n, 1024 * 1024)

    def body(w_ref, g_ref, m_ref, v_ref, d_ref, mo_ref, vo_ref):
        gv = g_ref[...]
        m2 = ADAM_B1 * m_ref[...] + (1.0 - ADAM_B1) * gv
        v2 = ADAM_B2 * v_ref[...] + (1.0 - ADAM_B2) * (gv * gv)
        m_hat = m2 / (1.0 - ADAM_B1 ** ADAM_STEP)
        v_hat = v2 / (1.0 - ADAM_B2 ** ADAM_STEP)
        d_ref[...] = -ADAM_LR * (m_hat / (jnp.sqrt(v_hat) + ADAM_EPS) + ADAM_WD * w_ref[...])
        mo_ref[...] = m2
        vo_ref[...] = v2

    spec = pl.BlockSpec((t, n), lambda i: (i, 0))
    return pl.pallas_call(
        body, name=name, out_shape=[jax.ShapeDtypeStruct((r, n), F32)] * 3, grid=(r // t,),
        in_specs=[spec] * 4, out_specs=[spec] * 3,
        compiler_params=_cparams(dimension_semantics=("parallel",)),
    )(wp, gp, mp, vp)


BIG = (('ssd_w_in', 'colx'), ('ssd_w_out', 'row'), ('mla_w_in', 'row'), ('mla_w_q_b', 'col'),
       ('mla_w_kv_b', 'col'), ('mla_w_out', 'row'), ('mlp_w_up', 'col'), ('mlp_w_down', 'row'))
SMALL_SHARDED = (('meta_tokens', 1), ('ssd_conv_w', 2), ('mla_q_a_norm', 1), ('mla_kv_a_norm', 1))
SMALL_REPL = ('ln_mix', 'ln_mlp', 'ssd_conv_b', 'ssd_dt_bias', 'ssd_a_log', 'ssd_d', 'ssd_norm',
              'mla_q_norm', 'mla_k_norm')
ALL_NAMES = ('meta_tokens', 'ln_mix', 'ln_mlp', 'ssd_w_in', 'ssd_conv_w', 'ssd_conv_b', 'ssd_dt_bias',
             'ssd_a_log', 'ssd_d', 'ssd_norm', 'ssd_w_out', 'mla_w_in', 'mla_q_a_norm', 'mla_w_q_b',
             'mla_kv_a_norm', 'mla_w_kv_b', 'mla_q_norm', 'mla_k_norm', 'mla_w_out', 'mlp_w_up', 'mlp_w_down')


_MLA_BIG = ('mla_w_in', 'mla_w_q_b', 'mla_w_kv_b', 'mla_w_out')
GATHER_ROUNDS = (
    (('ssd_w_in', 0, 1), ('ssd_w_out', 0, 1), ('mlp_w_up', 0, 1), ('mlp_w_down', 0, 1)),
    tuple((n, 0, 1) for n in _MLA_BIG) + (('mlp_w_up', 1, 2), ('mlp_w_down', 1, 2)),
    (('ssd_w_in', 1, 2), ('ssd_w_out', 1, 2)) + tuple((n, 1, 2) for n in _MLA_BIG)
    + (('mlp_w_up', 2, 4), ('mlp_w_down', 2, 4)),
)


class _GatheredWeights:
    def __init__(self, shards):
        self.shards, self.p, self.whole = shards, _no_matrices(), {}

    def _round(self, r):
        spec = GATHER_ROUNDS[r]
        return _gather_rider([self.shards[n][l0:l1] for n, l0, l1 in spec], [dict(BIG)[n] for n, _, _ in spec])

    def _take(self, r, outs):
        for (n, l0, l1), o in zip(GATHER_ROUNDS[r], outs):
            kind = dict(BIG)[n]
            for l in range(l0, l1):
                if kind == 'row':
                    m = o[l - l0].reshape(-1, o.shape[-1])
                elif kind == 'col':
                    m = o[l - l0]
                else:
                    m = jnp.concatenate([o[k, l - l0] for k in range(4)], axis=-1)
                self.whole[(n, l)] = m

    def ensure(self, i):
        if i == 0:
            self._take(0, _run_rider(self._round(0), "gather_first"))
        _prep_layer(self.p, i, lambda n: self.whole[(n, _layer_slab(n, i))])

    def rider(self, i):
        return self._round(i + 1) if i + 1 < len(GATHER_ROUNDS) else None

    def deliver(self, i, outs):
        if i + 1 < len(GATHER_ROUNDS):
            self._take(i + 1, outs)


def _pack(arrs, rows_mult):
    flat = jnp.concatenate([a.reshape(-1) for a in arrs])
    per = LANES * rows_mult
    pad = (-flat.shape[0]) % per
    if pad:
        flat = jnp.concatenate([flat, jnp.zeros((pad,), flat.dtype)])
    return flat.reshape(-1, LANES)


def _unpack(pack, shapes):
    flat = pack.reshape(-1)
    out, off = [], 0
    for shp in shapes:
        n = math.prod(shp)
        out.append(flat[off:off + n].reshape(shp))
        off += n
    return out


def _split4(full, axis):
    shp = full.shape
    r = full.reshape(shp[:axis] + (4, shp[axis] // 4) + shp[axis + 1:])
    return jnp.moveaxis(r, axis, 0)


def _join4(parts, axis):
    r = jnp.moveaxis(parts, 0, axis)
    shp = r.shape
    return r.reshape(shp[:axis] + (shp[axis] * shp[axis + 1],) + shp[axis + 2:])


def _gather_params(shards, table, dtype, c, name):
    pack = _pack([shards[n].astype(dtype) for n, _ in table], 16)
    half = pack.shape[0] // 2
    mine = lax.dynamic_slice_in_dim(pack, c * half, half, axis=0)
    full = _all_gather8(mine, name).reshape(4, -1)
    out, off = {}, 0
    for n, ax in table:
        cnt = math.prod(shards[n].shape)
        out[n] = _join4(full[:, off:off + cnt].reshape((4,) + shards[n].shape), ax)
        off += cnt
    return out


def kernel(x, meta_tokens, ln_mix, ln_mlp, ssd_w_in, ssd_conv_w, ssd_conv_b, ssd_dt_bias, ssd_a_log, ssd_d, ssd_norm, ssd_w_out, mla_w_in, mla_q_a_norm, mla_w_q_b, mla_kv_a_norm, mla_w_kv_b, mla_q_norm, mla_k_norm, mla_w_out, mlp_w_up, mlp_w_down, loss_target, m_meta_tokens, m_ln_mix, m_ln_mlp, m_ssd_w_in, m_ssd_conv_w, m_ssd_conv_b, m_ssd_dt_bias, m_ssd_a_log, m_ssd_d, m_ssd_norm, m_ssd_w_out, m_mla_w_in, m_mla_q_a_norm, m_mla_w_q_b, m_mla_kv_a_norm, m_mla_w_kv_b, m_mla_q_norm, m_mla_k_norm, m_mla_w_out, m_mlp_w_up, m_mlp_w_down, v_meta_tokens, v_ln_mix, v_ln_mlp, v_ssd_w_in, v_ssd_conv_w, v_ssd_conv_b, v_ssd_dt_bias, v_ssd_a_log, v_ssd_d, v_ssd_norm, v_ssd_w_out, v_mla_w_in, v_mla_q_a_norm, v_mla_w_q_b, v_mla_kv_a_norm, v_mla_w_kv_b, v_mla_q_norm, v_mla_k_norm, v_mla_w_out, v_mlp_w_up, v_mlp_w_down):
    w_sh = dict(meta_tokens=meta_tokens, ln_mix=ln_mix, ln_mlp=ln_mlp, ssd_w_in=ssd_w_in, ssd_conv_w=ssd_conv_w, ssd_conv_b=ssd_conv_b, ssd_dt_bias=ssd_dt_bias, ssd_a_log=ssd_a_log, ssd_d=ssd_d, ssd_norm=ssd_norm, ssd_w_out=ssd_w_out, mla_w_in=mla_w_in, mla_q_a_norm=mla_q_a_norm, mla_w_q_b=mla_w_q_b, mla_kv_a_norm=mla_kv_a_norm, mla_w_kv_b=mla_w_kv_b, mla_q_norm=mla_q_norm, mla_k_norm=mla_k_norm, mla_w_out=mla_w_out, mlp_w_up=mlp_w_up, mlp_w_down=mlp_w_down)
    m_sh = dict(meta_tokens=m_meta_tokens, ln_mix=m_ln_mix, ln_mlp=m_ln_mlp, ssd_w_in=m_ssd_w_in, ssd_conv_w=m_ssd_conv_w, ssd_conv_b=m_ssd_conv_b, ssd_dt_bias=m_ssd_dt_bias, ssd_a_log=m_ssd_a_log, ssd_d=m_ssd_d, ssd_norm=m_ssd_norm, ssd_w_out=m_ssd_w_out, mla_w_in=m_mla_w_in, mla_q_a_norm=m_mla_q_a_norm, mla_w_q_b=m_mla_w_q_b, mla_kv_a_norm=m_mla_kv_a_norm, mla_w_kv_b=m_mla_w_kv_b, mla_q_norm=m_mla_q_norm, mla_k_norm=m_mla_k_norm, mla_w_out=m_mla_w_out, mlp_w_up=m_mlp_w_up, mlp_w_down=m_mlp_w_down)
    v_sh = dict(meta_tokens=v_meta_tokens, ln_mix=v_ln_mix, ln_mlp=v_ln_mlp, ssd_w_in=v_ssd_w_in, ssd_conv_w=v_ssd_conv_w, ssd_conv_b=v_ssd_conv_b, ssd_dt_bias=v_ssd_dt_bias, ssd_a_log=v_ssd_a_log, ssd_d=v_ssd_d, ssd_norm=v_ssd_norm, ssd_w_out=v_ssd_w_out, mla_w_in=v_mla_w_in, mla_q_a_norm=v_mla_q_a_norm, mla_w_q_b=v_mla_w_q_b, mla_kv_a_norm=v_mla_kv_a_norm, mla_w_kv_b=v_mla_w_kv_b, mla_q_norm=v_mla_q_norm, mla_k_norm=v_mla_k_norm, mla_w_out=v_mla_w_out, mlp_w_up=v_mlp_w_up, mlp_w_down=v_mlp_w_down)

    cx, cy, cc = lax.axis_index("x"), lax.axis_index("y"), lax.axis_index("c")
    chip = 2 * cx + cy

    c_idx = cc.reshape(1).astype(jnp.int32)
    big_names = [n for n, _ in BIG]
    kinds = [k for _, k in BIG]
    shapes = [w_sh[n].shape for n in big_names]

    w = {n: w_sh[n] for n in SMALL_REPL}
    w.update(_gather_params(w_sh, SMALL_SHARDED, F32, cc, "gather_small"))
    big = _GatheredWeights({n: w_sh[n].astype(BF16) for n in big_names})

    loss_row, grad_x, grads = _local_step(x[0], loss_target[0], w, big)
    loss = lax.psum(jnp.sum(loss_row), ("x", "y", "c"))

    g_whole = []
    for n, kind, s in zip(big_names, kinds, shapes):
        if kind == 'row':
            g_whole.append(grads[n].reshape(s[0], 4, s[1], s[2]))
        elif kind == 'col':
            g_whole.append(grads[n])
        else:
            g_whole.append(jnp.stack([grads[n][..., k * s[2]:(k + 1) * s[2]] for k in range(4)]))
    recv = _rs_swap(g_whole, kinds, shapes, "rs_swap")
    parts = []
    for n, kind, s, g, r in zip(big_names, kinds, shapes, g_whole, recv):
        if kind == 'col':
            g3, r3 = g, r
        else:
            g3, r3 = g.reshape(-1, s[1], s[2]), r.reshape(-1, s[1] // 2, s[2])
        parts.append(_add_half(g3, r3, c_idx, f"rs_add_{n}").reshape(r.shape))
    got = _rs_exchange(parts, kinds, shapes, "rs_exchange")
    reduced = [_sum4(p, c_idx, f"rs_sum_{n}") for n, p in zip(big_names, got)]
    g_sh = dict(zip(big_names, _rs_share(reduced, "rs_share")))

    small_names = tuple(n for n, _ in SMALL_SHARDED) + SMALL_REPL
    sp = _pack([grads[n] for n in small_names], 8)
    srows = sp.shape[0]
    s_all = _sum8(_all_gather8(sp, "ar_small_gather").reshape(8, srows, LANES), "ar_small_sum")
    s_full = dict(zip(small_names, _unpack(s_all, [grads[n].shape for n in small_names])))
    for n, ax in SMALL_SHARDED:
        g_sh[n] = lax.dynamic_index_in_dim(_split4(s_full[n], ax), chip, axis=0, keepdims=False)
    for n in SMALL_REPL:
        g_sh[n] = s_full[n]

    delta, new_m, new_v = {}, {}, {}
    for n, s in zip(big_names, shapes):
        res = _adamw(*[t[n].reshape(-1, s[2]) for t in (w_sh, g_sh, m_sh, v_sh)], f"adamw_{n}")
        delta[n], new_m[n], new_v[n] = [r.reshape(s) for r in res]
    d_s, m_s, v_s = _adamw(*[_pack([t[n] for n in small_names], 8) for t in (w_sh, g_sh, m_sh, v_sh)],
                           "adamw_small")
    for dst, ps in ((delta, d_s), (new_m, m_s), (new_v, v_s)):
        dst.update(zip(small_names, _unpack(ps, [w_sh[n].shape for n in small_names])))

    return (loss, grad_x[None], *[g_sh[n] for n in ALL_NAMES], *[delta[n] for n in ALL_NAMES],
            *[new_m[n] for n in ALL_NAMES], *[new_v[n] for n in ALL_NAMES])
```

```python
import functools
import math

import jax
import jax.numpy as jnp
from jax import lax
from jax.experimental import pallas as pl
from jax.experimental.pallas import tpu as pltpu

F32 = jnp.float32
BF16 = jnp.bfloat16
MESH = pl.DeviceIdType.MESH
_NN = (((1,), (0,)), ((), ()))
_NT = (((1,), (1,)), ((), ()))
_TN = (((0,), (0,)), ((), ()))

D_MODEL = 1024
N_META = 16
EPS = 1e-6
SSD_D_INNER = 2048
SSD_HEADS = 32
SSD_HEAD_DIM = 64
SSD_GROUPS = 8
SSD_HPG = 4
SSD_STATE = 128
SSD_CONV = 4
CHUNK = 128
SSD_IN_DIM = 6176
SSD_IN_PAD = 6272
MLA_HEADS = 16
MLA_NOPE = 64
MLA_ROPE = 32
MLA_V = 64
MLA_QK = 96
MLA_Q_RANK = 384
MLA_KV_RANK = 256
HEAD_SLOT = 128
MLA_WIDE = MLA_HEADS * HEAD_SLOT
HEADS_PER_STEP = 2
LAT_PAD = 768
ROPE_THETA = 10000.0
D_FF = 4096
NPAD = CHUNK - N_META
ADAM_LR, ADAM_B1, ADAM_B2, ADAM_EPS, ADAM_WD, ADAM_STEP = 0.001, 0.9, 0.999, 1e-08, 0.01, 10
LANES = 1024
VMEM_LIMIT = 56 * 1024 * 1024


def _pick(n, cands):
    for c in cands:
        if n % c == 0:
            return c
    return n


def _cparams(**kw):
    return pltpu.CompilerParams(vmem_limit_bytes=VMEM_LIMIT, **kw)


def _mm(a, b, dims, *, name, out_dtype=F32, a_fn=None, epi=None, extras=(), stack=None):
    if dims == 'nn':
        (M, K), (K2, N) = a.shape, b.shape
    elif dims == 'nt':
        (M, K), (N, K2) = a.shape, b.shape
    else:
        (K, M), (K2, N) = a.shape, b.shape
    assert K == K2, (a.shape, b.shape, dims)
    if dims == 'tn':
        tm = _pick(M, (1024, 768, 512, 384, 256, 128))
        tn = _pick(N, (1024, 896, 768, 512, 384, 256, 128))
        tk = _pick(K, (1408, 1024, 512, 384, 256, 128))
    else:
        tm = _pick(M, (1408, 1024, 512, 384, 256, 128))
        tn = _pick(N, (512, 896, 768, 384, 256, 128))
        tk = _pick(K, (1024, 896, 768, 512, 384, 256, 128))
    nk = K // tk
    if dims == 'nn':
        a_spec = pl.BlockSpec((tm, tk), lambda i, j, k: (i, k))
        b_spec = pl.BlockSpec((tk, tn), lambda i, j, k: (k, j))
        dn = (((1,), (0,)), ((), ()))
    elif dims == 'nt':
        a_spec = pl.BlockSpec((tm, tk), lambda i, j, k: (i, k))
        b_spec = pl.BlockSpec((tn, tk), lambda i, j, k: (j, k))
        dn = (((1,), (1,)), ((), ()))
    else:
        a_spec = pl.BlockSpec((tk, tm), lambda i, j, k: (k, i))
        b_spec = pl.BlockSpec((tk, tn), lambda i, j, k: (k, j))
        dn = (((0,), (0,)), ((), ()))
    o_spec = pl.BlockSpec((tm, tn), lambda i, j, k: (i, j))
    n_ex = len(extras)
    out_shape = jax.ShapeDtypeStruct((M, N), out_dtype)
    out_spec, held, aliases = o_spec, (), {}
    if stack is not None:
        n_slabs, slab, buf = stack
        out_shape = jax.ShapeDtypeStruct((n_slabs, M, N), out_dtype)
        out_spec = pl.BlockSpec((None, tm, tn), lambda i, j, k: (slab, i, j))
        if buf is not None:
            held, aliases = (buf,), {2 + n_ex: 0}

    def body(a_ref, b_ref, *rest):
        ex_refs, o_ref, acc = rest[:n_ex], rest[n_ex + len(held)], rest[n_ex + len(held) + 1]
        k = pl.program_id(2)

        @pl.when(k == 0)
        def _():
            acc[...] = jnp.zeros_like(acc)

        av = a_ref[...]
        if a_fn is not None:
            av = a_fn(av)
        acc[...] += lax.dot_general(av.astype(BF16), b_ref[...].astype(BF16), dn,
                                    preferred_element_type=F32)

        @pl.when(k == nk - 1)
        def _():
            r = acc[...]
            if epi is not None:
                r = epi(r, *[e[...] for e in ex_refs])
            o_ref[...] = r.astype(out_dtype)

    return pl.pallas_call(
        body, name=name,
        out_shape=out_shape,
        grid=(M // tm, N // tn, nk),
        in_specs=[a_spec, b_spec] + [o_spec] * n_ex + [pl.BlockSpec(memory_space=pl.ANY)] * len(held),
        out_specs=out_spec,
        input_output_aliases=aliases,
        scratch_shapes=[pltpu.VMEM((tm, tn), F32)],
        compiler_params=_cparams(dimension_semantics=("parallel", "parallel", "arbitrary")),
    )(a, b, *extras, *held)


def _row_call(fn, rows, consts, out_rows, out_accs=(), *, n_rows, tile, name):
    n_r, n_c, n_o, n_a = len(rows), len(consts), len(out_rows), len(out_accs)
    steps = n_rows // tile

    def body(*refs):
        r_refs = refs[:n_r]
        c_refs = refs[n_r:n_r + n_c]
        o_refs = refs[n_r + n_c:n_r + n_c + n_o]
        a_refs = refs[n_r + n_c + n_o:]
        i = pl.program_id(0)
        res = fn(i, *[r[...] for r in r_refs], *[c[...] for c in c_refs])
        for o_ref, val in zip(o_refs, res[:n_o]):
            o_ref[...] = val.astype(o_ref.dtype)

        @pl.when(i == 0)
        def _():
            for a_ref in a_refs:
                a_ref[...] = jnp.zeros_like(a_ref)

        for a_ref, val in zip(a_refs, res[n_o:]):
            a_ref[...] += val

    in_specs = [pl.BlockSpec((tile, w), functools.partial(lambda i, cb: (i, cb), cb=cb))
                for (_, w, cb) in rows]
    in_specs += [pl.BlockSpec(c.shape, lambda i: (0, 0)) for c in consts]
    out_specs = [pl.BlockSpec((tile, c), lambda i: (i, 0)) for (c, _) in out_rows]
    out_specs += [pl.BlockSpec(s, lambda i: (0, 0)) for s in out_accs]
    out_shape = [jax.ShapeDtypeStruct((n_rows, c), dt) for (c, dt) in out_rows]
    out_shape += [jax.ShapeDtypeStruct(s, F32) for s in out_accs]
    return pl.pallas_call(
        body, name=name, out_shape=out_shape, grid=(steps,),
        in_specs=in_specs, out_specs=out_specs,
        compiler_params=_cparams(dimension_semantics=("arbitrary",)),
    )(*[r[0] for r in rows], *consts)


def _row_mask(i, tile):
    r = i * tile + lax.broadcasted_iota(jnp.int32, (tile, 1), 0)
    return (r >= NPAD).astype(F32)


def _rms(x, g):
    return x * lax.rsqrt(jnp.mean(x * x, axis=-1, keepdims=True) + EPS) * g


def _silu(x):
    return x * (0.5 * jnp.tanh(0.5 * x) + 0.5)


def _softplus(x):
    return jnp.maximum(x, 0.0) + jnp.log(1.0 + jnp.exp(-jnp.abs(x)))


def _rms_fwd(h, g, name):
    lp = h.shape[0]
    return _row_call(lambda i, hv, gv: (_rms(hv, gv),), [(h, D_MODEL, 0)], [g],
                     [(D_MODEL, BF16)], n_rows=lp, tile=_pick(lp, (384, 256, 128)), name=name)[0]


def _rms_bwd(h, g, d_hn, d_res, name):
    lp = h.shape[0]
    tile = _pick(lp, (384, 256, 128))

    def fn(i, hv, dv, rv, gv):
        _, vjp = jax.vjp(_rms, hv, gv)
        dh, dg = vjp(dv)
        return (rv + dh) * _row_mask(i, tile), dg

    return _row_call(fn, [(h, D_MODEL, 0), (d_hn, D_MODEL, 0), (d_res, D_MODEL, 0)], [g],
                     [(D_MODEL, F32)], [(1, D_MODEL)], n_rows=lp, tile=tile, name=name)


@functools.partial(jax.custom_vjp, nondiff_argnums=(1,))
def _roll_rows(x, s):
    return pltpu.roll(x, s, 0)


def _roll_rows_fwd(x, s):
    return pltpu.roll(x, s, 0), None


def _roll_rows_bwd(s, _, ct):
    return (pltpu.roll(ct, (ct.shape[0] - s) % ct.shape[0], 0),)


_roll_rows.defvjp(_roll_rows_fwd, _roll_rows_bwd)


def _conv_silu(cur, halo, w_rows, b):
    full = jnp.concatenate([halo, cur], axis=0)
    acc = cur * w_rows[SSD_CONV - 1] + b
    for k in range(SSD_CONV - 1):
        acc = acc + _roll_rows(full, SSD_CONV - 1 - k)[8:] * w_rows[k]
    return _silu(acc)


def _split3(v):
    hi = v.astype(BF16)
    r1 = v - hi.astype(F32)
    mid = r1.astype(BF16)
    lo = (r1 - mid.astype(F32)).astype(BF16)
    return hi, mid, lo


def _select_right(v, sel, dn):
    return sum(lax.dot_general(p, sel, dn, preferred_element_type=F32) for p in _split3(v))


@jax.custom_vjp
def _expand_heads(v, e_mat):
    return _select_right(v, e_mat, _NN)


def _expand_heads_fwd(v, e_mat):
    return _select_right(v, e_mat, _NN), e_mat


def _expand_heads_bwd(e_mat, ct):
    return _select_right(ct, e_mat, _NT), jnp.zeros_like(e_mat)


_expand_heads.defvjp(_expand_heads_fwd, _expand_heads_bwd)


@jax.custom_vjp
def _cumsum_rows(a, tri):
    return sum(lax.dot_general(tri, p, _NN, preferred_element_type=F32) for p in _split3(a))


def _cumsum_rows_fwd(a, tri):
    return _cumsum_rows(a, tri), tri


def _cumsum_rows_bwd(tri, ct):
    return (sum(lax.dot_general(tri, p, _TN, preferred_element_type=F32) for p in _split3(ct)),
            jnp.zeros_like(tri))


_cumsum_rows.defvjp(_cumsum_rows_fwd, _cumsum_rows_bwd)


def _ssd_chunk(mask, z, xs_pre, bc_pre, halo_x, halo_bc, dt_pre, st, cwx0, cwx1, cwx2, cwx3,
               cwb0, cwb1, cwb2, cwb3, cb_x, cb_bc, dtb, alog, dsk, ng):
    L = CHUNK
    lane_h = lax.broadcasted_iota(jnp.int32, (1, 128), 1)
    head_ok = (lane_h < SSD_HEADS).astype(F32)
    e_mat = (lax.broadcasted_iota(jnp.int32, (128, SSD_D_INNER), 1) // SSD_HEAD_DIM
             == lax.broadcasted_iota(jnp.int32, (128, SSD_D_INNER), 0)).astype(BF16)
    ri = lax.broadcasted_iota(jnp.int32, (L, L), 0)
    ci = lax.broadcasted_iota(jnp.int32, (L, L), 1)
    causal = ri >= ci

    xs = _conv_silu(xs_pre, halo_x, (cwx0, cwx1, cwx2, cwx3), cb_x) * mask
    bc = _conv_silu(bc_pre, halo_bc, (cwb0, cwb1, cwb2, cwb3), cb_bc) * mask
    dt = _softplus(dt_pre + dtb) * mask * head_ok
    a_dt = dt * (-jnp.exp(alog))
    a_cs = _cumsum_rows(a_dt, causal.astype(BF16))
    a_cs_t = a_cs.T
    row8 = lax.broadcasted_iota(jnp.int32, (8, 128), 0)
    last8 = jnp.where(row8 == 0, jnp.sum(a_dt, axis=0, keepdims=True), 0.0)
    dsk8 = jnp.where(row8 == 0, dsk, 0.0)
    wide = _expand_heads(jnp.concatenate([dt, a_cs, last8, dsk8], axis=0), e_mat)
    dt_e, acs_e = wide[0:L], wide[L:2 * L]
    last_e = jnp.sum(wide[2 * L:2 * L + 8], axis=0, keepdims=True)
    d_e = jnp.sum(wide[2 * L + 8:2 * L + 16], axis=0, keepdims=True)
    xdt = xs * dt_e
    dte_e = jnp.exp(last_e - acs_e)
    dfs_e = jnp.exp(acs_e)
    cd_e = jnp.exp(last_e)
    sub_h = lax.broadcasted_iota(jnp.int32, (128, L), 0)
    lane_hl = lax.broadcasted_iota(jnp.int32, (L, 128), 1)
    lane_g = lax.broadcasted_iota(jnp.int32, (1, SSD_HPG * SSD_HEAD_DIM), 1) // SSD_HEAD_DIM

    ys, new_st = [], []
    for g in range(SSD_GROUPS):
        b_g = bc[:, g * 128:(g + 1) * 128].astype(BF16)
        c_g = bc[:, 1024 + g * 128:1024 + (g + 1) * 128].astype(BF16)
        gs = slice(g * 256, (g + 1) * 256)
        xdt_g = xdt[:, gs]
        cb = lax.dot_general(c_g, b_g, (((1,), (1,)), ((), ())), preferred_element_type=F32)
        st_g = st[g * 128:(g + 1) * 128, :]
        y_g = lax.dot_general(c_g, st_g.astype(BF16), (((1,), (0,)), ((), ())),
                              preferred_element_type=F32) * dfs_e[:, gs]
        for j in range(SSD_HPG):
            h = g * SSD_HPG + j
            col = jnp.sum(jnp.where(lane_hl == h, a_cs, 0.0), axis=1, keepdims=True)
            row = jnp.sum(jnp.where(sub_h == h, a_cs_t, 0.0), axis=0, keepdims=True)
            dec = jnp.where(causal, jnp.exp(jnp.where(causal, col - row, 0.0)), 0.0)
            m_h = (cb * dec).astype(BF16)
            x_h = jnp.where(lane_g == j, xdt_g, 0.0).astype(BF16)
            y_g = y_g + lax.dot_general(m_h, x_h, (((1,), (0,)), ((), ())),
                                        preferred_element_type=F32)
        s_new = lax.dot_general(b_g, (xdt_g * dte_e[:, gs]).astype(BF16), (((0,), (0,)), ((), ())),
                                preferred_element_type=F32)
        new_st.append(st_g * cd_e[:, gs] + s_new)
        ys.append(y_g)
    y = jnp.concatenate(ys, axis=1) + xs * d_e
    gg = y * _silu(z)
    outs = []
    for g in range(SSD_GROUPS):
        sl = gg[:, g * 256:(g + 1) * 256]
        outs.append(sl * lax.rsqrt(jnp.mean(sl * sl, axis=-1, keepdims=True) + EPS))
    out = jnp.concatenate(outs, axis=1) * ng
    return out, jnp.concatenate(new_st, axis=0)


def _ssd_consts(conv_w, conv_b, dtb, alog, dsk, ng):
    return [conv_w, conv_b, dtb, alog, dsk, ng]


def _ssd_param_vals(cw_ref, cb_ref, dtb_ref, alog_ref, dsk_ref, ng_ref):
    cwx = [cw_ref[k:k + 1, 0:SSD_D_INNER] for k in range(SSD_CONV)]
    cwb = [cw_ref[k:k + 1, SSD_D_INNER:2 * SSD_D_INNER] for k in range(SSD_CONV)]
    return (*cwx, *cwb, cb_ref[:, 0:SSD_D_INNER], cb_ref[:, SSD_D_INNER:2 * SSD_D_INNER],
            dtb_ref[...], alog_ref[...], dsk_ref[...], ng_ref[...])


def _ssd_in_specs(rev, nc):
    def cidx(i):
        return (nc - 1 - i) if rev else i

    def halo(cb):
        return pl.BlockSpec((8, SSD_D_INNER), lambda i: (jnp.maximum(16 * cidx(i) - 1, 0), cb))

    return [
        pl.BlockSpec((CHUNK, SSD_D_INNER), lambda i: (cidx(i), 0)),
        pl.BlockSpec((CHUNK, SSD_D_INNER), lambda i: (cidx(i), 1)),
        pl.BlockSpec((CHUNK, SSD_D_INNER), lambda i: (cidx(i), 2)),
        halo(1), halo(2),
        pl.BlockSpec((CHUNK, 128), lambda i: (cidx(i), 48)),
    ]


class _Rider:
    def __init__(self, operands, out_shapes, scratch, start, finish):
        self.operands, self.out_shapes, self.scratch = list(operands), list(out_shapes), list(scratch)
        self.start, self.finish = start, finish


def _rider_split(rider, refs, n_in, n_out, n_scratch):
    if rider is None:
        return refs, None
    ni, no = len(rider.operands), len(rider.out_shapes)
    own = refs[:n_in] + refs[n_in + ni:n_in + ni + n_out] + refs[n_in + ni + n_out + no:n_in + ni + n_out + no + n_scratch]
    mine = (refs[n_in:n_in + ni], refs[n_in + ni + n_out:n_in + ni + n_out + no],
            refs[n_in + ni + n_out + no + n_scratch:])
    return own, mine


def _rider_args(rider):
    if rider is None:
        return [], [], [], []
    hbm = pl.BlockSpec(memory_space=pl.ANY)
    return ([hbm] * len(rider.operands), [hbm] * len(rider.out_shapes), rider.out_shapes, rider.scratch)


def _ssd_fwd(zxd, consts, name, rider=None):
    lp = zxd.shape[0]
    nc = lp // CHUNK

    def body(*refs):
        own, ride = _rider_split(rider, refs, 12, 2, 1)
        (z_ref, xs_ref, bc_ref, hx_ref, hb_ref, dt_ref, cw_ref, cb_ref, dtb_ref, alog_ref,
         dsk_ref, ng_ref, y_ref, st_ref, state) = own
        c = pl.program_id(0)

        @pl.when(c == 0)
        def _():
            state[...] = jnp.zeros_like(state)
            if ride is not None:
                rider.start(*ride)

        live = (c > 0).astype(F32)
        st_ref[0] = state[...]
        out, st_new = _ssd_chunk(
            _row_mask(c, CHUNK), z_ref[...], xs_ref[...], bc_ref[...], hx_ref[...] * live,
            hb_ref[...] * live, dt_ref[...], state[...],
            *_ssd_param_vals(cw_ref, cb_ref, dtb_ref, alog_ref, dsk_ref, ng_ref))
        y_ref[...] = out.astype(y_ref.dtype)
        state[...] = st_new

        if ride is not None:
            @pl.when(c == nc - 1)
            def _():
                rider.finish(*ride)

    r_in, r_out, r_shapes, r_scratch = _rider_args(rider)
    return pl.pallas_call(
        body, name=name,
        out_shape=[jax.ShapeDtypeStruct((lp, SSD_D_INNER), BF16),
                   jax.ShapeDtypeStruct((nc, SSD_GROUPS * SSD_STATE, 256), F32)] + r_shapes,
        grid=(nc,),
        in_specs=_ssd_in_specs(False, nc) + [pl.BlockSpec(c.shape, lambda i: (0, 0)) for c in consts] + r_in,
        out_specs=[pl.BlockSpec((CHUNK, SSD_D_INNER), lambda i: (i, 0)),
                   pl.BlockSpec((1, SSD_GROUPS * SSD_STATE, 256), lambda i: (i, 0, 0))] + r_out,
        scratch_shapes=[pltpu.VMEM((SSD_GROUPS * SSD_STATE, 256), F32)] + r_scratch,
        compiler_params=_cparams(dimension_semantics=("arbitrary",)),
    )(zxd, zxd, zxd, zxd, zxd, zxd, *consts, *(rider.operands if rider else ()))


def _ssd_bwd(zxd, states, d_y, consts, name, rider=None):
    lp = zxd.shape[0]
    nc = lp // CHUNK

    def body(*refs):
        own, ride = _rider_split(rider, refs, 14, 7, 3)
        (z_ref, xs_ref, bc_ref, hx_ref, hb_ref, dt_ref, st_ref, dy_ref, cw_ref, cb_ref, dtb_ref,
         alog_ref, dsk_ref, ng_ref, dz_ref, dcw_ref, dcb_ref, ddtb_ref, dalog_ref, ddsk_ref,
         dng_ref, d_state, d_hx, d_hb) = own
        i = pl.program_id(0)
        c = nc - 1 - i

        @pl.when(i == 0)
        def _():
            d_state[...] = jnp.zeros_like(d_state)
            d_hx[...] = jnp.zeros_like(d_hx)
            d_hb[...] = jnp.zeros_like(d_hb)
            for r in (dcw_ref, dcb_ref, ddtb_ref, dalog_ref, ddsk_ref, dng_ref):
                r[...] = jnp.zeros_like(r)
            if ride is not None:
                rider.start(*ride)

        live = (c > 0).astype(F32)
        fn = functools.partial(_ssd_chunk, _row_mask(c, CHUNK))
        prim = (z_ref[...], xs_ref[...], bc_ref[...], hx_ref[...] * live, hb_ref[...] * live,
                dt_ref[...], st_ref[0],
                *_ssd_param_vals(cw_ref, cb_ref, dtb_ref, alog_ref, dsk_ref, ng_ref))
        _, vjp = jax.vjp(fn, *prim)
        (d_z, d_xs, d_bc, g_hx, g_hb, d_dt, g_st, *d_par) = vjp((dy_ref[...], d_state[...]))
        zeros = jnp.zeros((CHUNK - 8, SSD_D_INNER), F32)
        d_xs = d_xs + jnp.concatenate([zeros, d_hx[...]], axis=0)
        d_bc = d_bc + jnp.concatenate([zeros, d_hb[...]], axis=0)
        dz_ref[:, 0:SSD_D_INNER] = d_z.astype(dz_ref.dtype)
        dz_ref[:, SSD_D_INNER:2 * SSD_D_INNER] = d_xs.astype(dz_ref.dtype)
        dz_ref[:, 2 * SSD_D_INNER:3 * SSD_D_INNER] = d_bc.astype(dz_ref.dtype)
        dz_ref[:, 3 * SSD_D_INNER:] = d_dt.astype(dz_ref.dtype)
        d_state[...] = g_st
        d_hx[...] = g_hx * live
        d_hb[...] = g_hb * live
        for k in range(SSD_CONV):
            dcw_ref[k:k + 1, 0:SSD_D_INNER] += d_par[k]
            dcw_ref[k:k + 1, SSD_D_INNER:2 * SSD_D_INNER] += d_par[SSD_CONV + k]
        dcb_ref[:, 0:SSD_D_INNER] += d_par[8]
        dcb_ref[:, SSD_D_INNER:2 * SSD_D_INNER] += d_par[9]
        ddtb_ref[...] += d_par[10]
        dalog_ref[...] += d_par[11]
        ddsk_ref[...] += d_par[12]
        dng_ref[...] += d_par[13]

        if ride is not None:
            @pl.when(i == nc - 1)
            def _():
                rider.finish(*ride)

    const_specs = [pl.BlockSpec(c.shape, lambda i: (0, 0)) for c in consts]
    r_in, r_out, r_shapes, r_scratch = _rider_args(rider)
    return pl.pallas_call(
        body, name=name,
        out_shape=[jax.ShapeDtypeStruct((lp, SSD_IN_PAD), BF16)]
        + [jax.ShapeDtypeStruct(c.shape, F32) for c in consts] + r_shapes,
        grid=(nc,),
        in_specs=_ssd_in_specs(True, nc)
        + [pl.BlockSpec((1, SSD_GROUPS * SSD_STATE, 256), lambda i: (nc - 1 - i, 0, 0)),
           pl.BlockSpec((CHUNK, SSD_D_INNER), lambda i: (nc - 1 - i, 0))] + const_specs + r_in,
        out_specs=[pl.BlockSpec((CHUNK, SSD_IN_PAD), lambda i: (nc - 1 - i, 0))] + const_specs + r_out,
        scratch_shapes=[pltpu.VMEM((SSD_GROUPS * SSD_STATE, 256), F32),
                        pltpu.VMEM((8, SSD_D_INNER), F32), pltpu.VMEM((8, SSD_D_INNER), F32)] + r_scratch,
        compiler_params=_cparams(dimension_semantics=("arbitrary",)),
    )(zxd, zxd, zxd, zxd, zxd, zxd, states, d_y, *consts, *(rider.operands if rider else ()))


@jax.custom_vjp
def _rot_half(x):
    lane = lax.broadcasted_iota(jnp.int32, x.shape, 1)
    lo = (lane >= MLA_NOPE) & (lane < MLA_NOPE + MLA_ROPE // 2)
    hi = (lane >= MLA_NOPE + MLA_ROPE // 2) & (lane < MLA_QK)
    down = pltpu.roll(x, HEAD_SLOT - MLA_ROPE // 2, 1)
    up = pltpu.roll(x, MLA_ROPE // 2, 1)
    return jnp.where(lo, -down, jnp.where(hi, up, 0.0))


def _rot_half_fwd(x):
    return _rot_half(x), None


def _rot_half_bwd(_, ct):
    return (-_rot_half(ct),)


_rot_half.defvjp(_rot_half_fwd, _rot_half_bwd)


def _head_norm_rope(t, gain, cos, sin):
    n = t * lax.rsqrt(jnp.sum(t * t, axis=-1, keepdims=True) * (1.0 / MLA_QK) + EPS) * gain
    return n * cos + _rot_half(n) * sin


def _qk_prep(q_raw, kn_raw, kpe, cos, sin, qg, kg):
    qs, ks = [], []
    for h in range(MLA_HEADS):
        sl = slice(h * HEAD_SLOT, (h + 1) * HEAD_SLOT)
        qs.append(_head_norm_rope(q_raw[:, sl], qg, cos, sin))
        ks.append(_head_norm_rope(kn_raw[:, sl] + kpe, kg, cos, sin))
    return jnp.concatenate(qs, axis=1), jnp.concatenate(ks, axis=1)


def _lat_norm(kv_lat, q_lat, kvg, qg):
    return _rms(kv_lat, kvg), _rms(q_lat, qg)


_NEG = -1e30
_SCALE = MLA_QK ** -0.5


STRIP = 128
_EXP2_SCALE = _SCALE * math.log2(math.e)


def _strip_mask(kind, blk, c, t):
    if kind is None:
        return None
    kpos = blk * t + c * STRIP + lax.broadcasted_iota(jnp.int32, (1, STRIP), 1)
    if kind == 'keys':
        return kpos >= NPAD
    qpos = blk * t + lax.broadcasted_iota(jnp.int32, (t, 1), 0)
    return (kpos <= qpos) & ((kpos >= NPAD) | (kpos == qpos))


def _attn_fwd(q, k, v, name, rider=None):
    lp = q.shape[0]
    t = _pick(lp, (384, 256, 128))
    nb = lp // t
    hp = HEADS_PER_STEP
    wide = hp * HEAD_SLOT
    heads = [slice(a * HEAD_SLOT, (a + 1) * HEAD_SLOT) for a in range(hp)]

    def body(*refs):
        (q_ref, k_ref, v_ref, o_ref, lse_ref), ride = _rider_split(rider, refs, 3, 2, 0)
        qi = pl.program_id(1)
        if ride is not None:
            @pl.when((pl.program_id(0) == 0) & (qi == 0))
            def _():
                rider.start(*ride)

        def update(a, ki, carry, mask):
            rows = pl.ds(pl.multiple_of(ki * t, t), t)
            m, l, acc = carry
            s = lax.dot_general(q_ref[:, heads[a]], k_ref[rows, heads[a]], _NT, preferred_element_type=F32)
            if mask is not None:
                s = jnp.where(mask, s, _NEG)
            m_new = jnp.maximum(m, jnp.max(s, axis=-1, keepdims=True))
            alpha = jnp.exp2((m - m_new) * _EXP2_SCALE)
            p = jnp.exp2((s - m_new) * _EXP2_SCALE)
            l = alpha * l + jnp.sum(p, axis=-1, keepdims=True)
            acc = alpha * acc + lax.dot_general(p.astype(BF16), v_ref[rows, heads[a]], _NN,
                                                preferred_element_type=F32)
            return m_new, l, acc

        def step(ki, carry, mask):
            return tuple(update(a, ki, carry[a], mask) for a in range(hp))

        init = (jnp.full((t, 1), _NEG, F32), jnp.zeros((t, 1), F32), jnp.zeros((t, HEAD_SLOT), F32))
        key_ok = lax.broadcasted_iota(jnp.int32, (1, t), 1) >= NPAD
        carry = lax.cond(qi > 0, lambda c: step(0, c, key_ok), lambda c: c, (init,) * hp)
        carry = lax.fori_loop(1, qi, lambda ki, c: step(ki, c, None), carry)
        qpos = lax.broadcasted_iota(jnp.int32, (t, t), 0)
        kpos = lax.broadcasted_iota(jnp.int32, (t, t), 1)
        diag = (kpos <= qpos) & ((qi * t + kpos >= NPAD) | (kpos == qpos))
        for a in range(hp):
            m, l, acc = update(a, qi, carry[a], diag)
            o_ref[:, heads[a]] = acc / l * _row_mask(qi, t)
            lse_ref[a] = m * _SCALE + jnp.log(l)

        if ride is not None:
            @pl.when((pl.program_id(0) == MLA_HEADS // hp - 1) & (qi == nb - 1))
            def _():
                rider.finish(*ride)

    qspec = pl.BlockSpec((t, wide), lambda g, i: (i, g))
    kspec = pl.BlockSpec((lp, wide), lambda g, i: (0, g))
    r_in, r_out, r_shapes, r_scratch = _rider_args(rider)
    return pl.pallas_call(
        body, name=name,
        out_shape=[jax.ShapeDtypeStruct((lp, MLA_WIDE), F32),
                   jax.ShapeDtypeStruct((MLA_HEADS, lp, 1), F32)] + r_shapes,
        grid=(MLA_HEADS // hp, nb),
        in_specs=[qspec, kspec, kspec] + r_in,
        out_specs=[qspec, pl.BlockSpec((hp, t, 1), lambda g, i: (g, i, 0))] + r_out,
        scratch_shapes=r_scratch,
        compiler_params=_cparams(dimension_semantics=("arbitrary", "arbitrary")),
    )(q, k, v, *(rider.operands if rider else ()))


def _attn_delta(do, o, name):
    lp = do.shape[0]
    t = _pick(lp, (384, 256, 128))

    def body(do_ref, o_ref, dob_ref, delta_ref):
        dob_ref[...] = do_ref[...].astype(BF16)
        for h in range(MLA_HEADS):
            sl = slice(h * HEAD_SLOT, (h + 1) * HEAD_SLOT)
            delta_ref[h] = jnp.sum(do_ref[:, sl] * o_ref[:, sl], axis=-1, keepdims=True)

    spec = pl.BlockSpec((t, MLA_WIDE), lambda i: (i, 0))
    return pl.pallas_call(
        body, name=name,
        out_shape=[jax.ShapeDtypeStruct((lp, MLA_WIDE), BF16), jax.ShapeDtypeStruct((MLA_HEADS, lp, 1), F32)],
        grid=(lp // t,), in_specs=[spec, spec],
        out_specs=[spec, pl.BlockSpec((MLA_HEADS, t, 1), lambda i: (0, i, 0))],
        compiler_params=_cparams(dimension_semantics=("parallel",)),
    )(do, o)


def _attn_bwd(q, k, v, do, lse, delta, name, rider=None):
    lp = q.shape[0]
    t = _pick(lp, (384, 256, 128))
    nb = lp // t
    ns = t // STRIP
    hp = HEADS_PER_STEP
    wide = hp * HEAD_SLOT
    heads = [slice(a * HEAD_SLOT, (a + 1) * HEAD_SLOT) for a in range(hp)]
    log2e = math.log2(math.e)

    def body(*refs):
        own, ride = _rider_split(rider, refs, 6, 3, 4)
        (q_ref, k_ref, v_ref, do_ref, lse_ref, delta_ref, dq_ref, dk_ref, dv_ref,
         s_scr, dp_scr, p_scr, ds_scr) = own
        kj = pl.program_id(1)
        if ride is not None:
            @pl.when((pl.program_id(0) == 0) & (kj == 0))
            def _():
                rider.start(*ride)

        @pl.when(kj == 0)
        def _():
            dq_ref[...] = jnp.zeros_like(dq_ref)

        dk_ref[...] = jnp.zeros_like(dk_ref)
        dv_ref[...] = jnp.zeros_like(dv_ref)

        def tile(qi, kind):
            rows = pl.ds(pl.multiple_of(qi * t, t), t)
            for a in range(hp):
                qb, dob = q_ref[rows, heads[a]], do_ref[rows, heads[a]]
                kb, vb = k_ref[:, heads[a]], v_ref[:, heads[a]]
                s_scr[a] = lax.dot_general(qb, kb, _NT, preferred_element_type=F32)
                dp_scr[a] = lax.dot_general(dob, vb, _NT, preferred_element_type=F32)
                lse2 = lse_ref[a, rows, :] * log2e
                delta = delta_ref[a, rows, :]
                for c in range(ns):
                    cs = slice(c * STRIP, (c + 1) * STRIP)
                    pc = jnp.exp2(s_scr[a, :, cs] * _EXP2_SCALE - lse2)
                    pc = jnp.where(_strip_mask(kind, kj, c, t), pc, 0.0)
                    p_scr[a, :, cs] = pc.astype(BF16)
                    ds_scr[a, :, cs] = (pc * (dp_scr[a, :, cs] - delta)).astype(BF16)
                dq_ref[rows, heads[a]] += lax.dot_general(ds_scr[a], kb, _NN,
                                                          preferred_element_type=F32) * _SCALE
                dv_ref[:, heads[a]] += lax.dot_general(p_scr[a], dob, _TN, preferred_element_type=F32)
                dk_ref[:, heads[a]] += lax.dot_general(ds_scr[a], qb, _TN, preferred_element_type=F32)

        tile(kj, 'diag')

        def below(qi, carry):
            tile(qi, 'keys')
            return carry

        lax.fori_loop(kj + 1, nb, below, 0)
        dk_ref[...] = dk_ref[...] * _SCALE

        if ride is not None:
            @pl.when((pl.program_id(0) == MLA_HEADS // hp - 1) & (kj == nb - 1))
            def _():
                rider.finish(*ride)

    whole = pl.BlockSpec((lp, wide), lambda g, j: (0, g))
    kspec = pl.BlockSpec((t, wide), lambda g, j: (j, g))
    stat = pl.BlockSpec((hp, lp, 1), lambda g, j: (g, 0, 0))
    r_in, r_out, r_shapes, r_scratch = _rider_args(rider)
    return pl.pallas_call(
        body, name=name,
        out_shape=[jax.ShapeDtypeStruct((lp, MLA_WIDE), F32)] * 3 + r_shapes,
        grid=(MLA_HEADS // hp, nb),
        in_specs=[whole, kspec, kspec, whole, stat, stat] + r_in,
        out_specs=[whole, kspec, kspec] + r_out,
        scratch_shapes=[pltpu.VMEM((hp, t, t), F32), pltpu.VMEM((hp, t, t), F32),
                        pltpu.VMEM((hp, t, t), BF16), pltpu.VMEM((hp, t, t), BF16)] + r_scratch,
        compiler_params=_cparams(dimension_semantics=("arbitrary", "arbitrary")),
    )(q, k, v, do, lse, delta, *(rider.operands if rider else ()))


def _rope_tables(lp):
    inv = 1.0 / (ROPE_THETA ** (jnp.arange(0, MLA_ROPE, 2, dtype=F32) / MLA_ROPE))
    pos = jnp.maximum(jnp.arange(lp, dtype=jnp.int32) - NPAD, 0).astype(F32)
    ang = pos[:, None] * inv[None, :]
    cos, sin = jnp.cos(ang), jnp.sin(ang)
    z32 = jnp.zeros((lp, HEAD_SLOT - MLA_QK), F32)
    cos_t = jnp.concatenate([jnp.ones((lp, MLA_NOPE), F32), cos, cos, z32], axis=1)
    sin_t = jnp.concatenate([jnp.zeros((lp, MLA_NOPE), F32), sin, sin, z32], axis=1)
    return cos_t, sin_t


def _loss_head(h, target, name):
    lp = h.shape[0]

    def body(h_ref, t_ref, d_ref, loss_ref):
        i = pl.program_id(0)

        @pl.when(i == 0)
        def _():
            d_ref[...] = jnp.zeros_like(d_ref)
            loss_ref[...] = jnp.zeros_like(loss_ref)

        @pl.when(i > 0)
        def _():
            err = h_ref[...] - t_ref[...]
            d_ref[...] = err * (1.0 / D_MODEL)
            loss_ref[...] += jnp.sum(err * err, axis=0, keepdims=True) * (0.5 / D_MODEL)

    return pl.pallas_call(
        body, name=name,
        out_shape=[jax.ShapeDtypeStruct((lp, D_MODEL), F32), jax.ShapeDtypeStruct((1, D_MODEL), F32)],
        grid=(lp // CHUNK,),
        in_specs=[pl.BlockSpec((CHUNK, D_MODEL), lambda i: (i, 0)),
                  pl.BlockSpec((CHUNK, D_MODEL), lambda i: (jnp.maximum(i - 1, 0), 0))],
        out_specs=[pl.BlockSpec((CHUNK, D_MODEL), lambda i: (i, 0)),
                   pl.BlockSpec((1, D_MODEL), lambda i: (0, 0))],
        compiler_params=_cparams(dimension_semantics=("arbitrary",)),
    )(h, target)


def _pad_cols(w, n):
    return jnp.pad(w, [(0, 0)] * (w.ndim - 1) + [(0, n - w.shape[-1])])


def _layer_slab(name, i):
    return i if name.startswith('mlp_') else i // 2


def _prep_layer(p, i, get):
    j = i // 2
    if i % 2 == 0:
        p['ssd_in'][j] = _pad_cols(get('ssd_w_in'), SSD_IN_PAD).astype(BF16)
        p['ssd_out'][j] = get('ssd_w_out').astype(BF16)
    else:
        wi = get('mla_w_in')
        kpe = jnp.pad(wi[:, MLA_Q_RANK + MLA_KV_RANK:], ((0, 0), (MLA_NOPE, HEAD_SLOT - MLA_QK)))
        p['mla_in'][j] = jnp.concatenate(
            [wi[:, MLA_Q_RANK:MLA_Q_RANK + MLA_KV_RANK], kpe, wi[:, :MLA_Q_RANK]], axis=1).astype(BF16)
        qb = get('mla_w_q_b').reshape(MLA_Q_RANK, MLA_HEADS, MLA_QK)
        p['mla_qb'][j] = _pad_cols(qb, HEAD_SLOT).reshape(MLA_Q_RANK, MLA_WIDE).astype(BF16)
        kvb = get('mla_w_kv_b').reshape(MLA_KV_RANK, MLA_HEADS, MLA_NOPE + MLA_V)
        kn = _pad_cols(kvb[:, :, :MLA_NOPE], HEAD_SLOT).reshape(MLA_KV_RANK, MLA_WIDE)
        vv = _pad_cols(kvb[:, :, MLA_NOPE:], HEAD_SLOT).reshape(MLA_KV_RANK, MLA_WIDE)
        p['mla_kvb'][j] = jnp.concatenate([kn, vv], axis=1).astype(BF16)
        wo = get('mla_w_out').reshape(MLA_HEADS, MLA_V, D_MODEL)
        p['mla_out'][j] = (jnp.pad(wo, ((0, 0), (0, HEAD_SLOT - MLA_V), (0, 0)))
                           .reshape(MLA_WIDE, D_MODEL).astype(BF16))
    p['up'][i] = get('mlp_w_up').astype(BF16)
    p['down'][i] = get('mlp_w_down').astype(BF16)


def _no_matrices():
    return {k: [None] * n for k, n in (('ssd_in', 2), ('ssd_out', 2), ('mla_in', 2), ('mla_qb', 2),
                                       ('mla_kvb', 2), ('mla_out', 2), ('up', 4), ('down', 4))}


class _ReadyWeights:
    def __init__(self, w):
        self.w, self.p = w, _no_matrices()

    def ensure(self, i):
        _prep_layer(self.p, i, lambda n: self.w[n][_layer_slab(n, i)])

    def rider(self, i):
        return None

    def deliver(self, i, outs):
        assert not outs


class _KeepGrads:
    def __init__(self):
        self.rounds = {}

    def begin(self, r, grads):
        self.rounds[r] = grads
        return None

    def finish(self, r, outs):
        assert not outs

    def result(self):
        names = {n for g in self.rounds.values() for n in g}
        return {n: jnp.concatenate([self.rounds[r][n] for r in sorted(self.rounds, reverse=True)
                                    if n in self.rounds[r]], axis=0) for n in names}


def _pad128(v):
    return _pad_cols(v.reshape(1, -1), 128)


def _sqrelu(u):
    r = jnp.maximum(u, 0.0)
    return r * r


def _local_step(x, target, w, big=None, red=None):
    seq = x.shape[0]
    lp = NPAD + N_META + seq
    big = _ReadyWeights(w) if big is None else big
    p = big.p
    h = jnp.concatenate([jnp.zeros((NPAD, D_MODEL), F32), w['meta_tokens'], x], axis=0)
    cos_t, sin_t = _rope_tables(lp)
    rt = _pick(lp, (384, 256, 128))
    saved = []
    for i in range(4):
        j = i // 2
        big.ensure(i)
        s = {'h0': h}
        g_mix = w['ln_mix'][i].reshape(1, -1)
        hn = _rms_fwd(h, g_mix, f"rms_mix_f{i}")
        s['hn'] = hn
        if i % 2 == 0:
            zxd = _mm(hn, p['ssd_in'][j], 'nn', name=f"ssd_in_f{i}")
            consts = _ssd_consts(w['ssd_conv_w'][j], w['ssd_conv_b'][j].reshape(1, -1),
                                 _pad128(w['ssd_dt_bias'][j]), _pad128(w['ssd_a_log'][j]),
                                 _pad128(w['ssd_d'][j]), w['ssd_norm'][j].reshape(1, -1))
            yg, states, *got = _ssd_fwd(zxd, consts, f"ssd_core_f{i}", rider=big.rider(i))
            big.deliver(i, got)
            s.update(zxd=zxd, consts=consts, yg=yg, states=states)
            h = _mm(yg, p['ssd_out'][j], 'nn', name=f"ssd_out_f{i}", epi=lambda r, hv: hv + r, extras=(h,))
        else:
            lat = _mm(hn, p['mla_in'][j], 'nn', name=f"mla_in_f{i}")
            kvg = w['mla_kv_a_norm'][j].reshape(1, -1)
            qag = w['mla_q_a_norm'][j].reshape(1, -1)
            kvn, qn = _row_call(lambda _, a, b, c, d: _lat_norm(a, b, c, d),
                                [(lat, MLA_KV_RANK, 0), (lat, MLA_Q_RANK, 1)], [kvg, qag],
                                [(MLA_KV_RANK, BF16), (MLA_Q_RANK, BF16)], n_rows=lp, tile=rt,
                                name=f"mla_latnorm_f{i}")
            q_raw = _mm(qn, p['mla_qb'][j], 'nn', name=f"mla_qb_f{i}")
            kv_raw = _mm(kvn, p['mla_kvb'][j], 'nn', name=f"mla_kvb_f{i}")
            qg = _pad_cols(w['mla_q_norm'][j].reshape(1, -1), HEAD_SLOT)
            kg = _pad_cols(w['mla_k_norm'][j].reshape(1, -1), HEAD_SLOT)

            def prep_fwd(_, qr, kn, kpe, vv, cs, sn, qgv, kgv):
                qq, kk = _qk_prep(qr, kn, kpe, cs, sn, qgv, kgv)
                return qq, kk, vv

            q, k, v = _row_call(prep_fwd,
                                [(q_raw, MLA_WIDE, 0), (kv_raw, MLA_WIDE, 0), (lat, HEAD_SLOT, 2),
                                 (kv_raw, MLA_WIDE, 1), (cos_t, HEAD_SLOT, 0), (sin_t, HEAD_SLOT, 0)],
                                [qg, kg], [(MLA_WIDE, BF16)] * 3, n_rows=lp, tile=rt,
                                name=f"mla_qkprep_f{i}")
            o, lse, *got = _attn_fwd(q, k, v, f"mla_attn_f{i}", rider=big.rider(i))
            big.deliver(i, got)
            s.update(lat=lat, kvg=kvg, qag=qag, kvn=kvn, qn=qn, q_raw=q_raw, kv_raw=kv_raw, qg=qg, kg=kg,
                     q=q, k=k, v=v, o=o, lse=lse)
            h = _mm(o, p['mla_out'][j], 'nn', name=f"mla_out_f{i}", epi=lambda r, hv: hv + r, extras=(h,))
        s['h1'] = h
        g_mlp = w['ln_mlp'][i].reshape(1, -1)
        hn2 = _rms_fwd(h, g_mlp, f"rms_mlp_f{i}")
        u = _mm(hn2, p['up'][i], 'nn', name=f"mlp_up_f{i}")
        h = _mm(u, p['down'][i], 'nn', name=f"mlp_down_f{i}", a_fn=_sqrelu,
                epi=lambda r, hv: hv + r, extras=(h,))
        s.update(hn2=hn2, u=u, g_mix=g_mix, g_mlp=g_mlp)
        saved.append(s)

    dh, loss_row = _loss_head(h, target, "loss_head")

    large = {n for n, _ in BIG}
    g = {k_: [None] * (4 if k_ in ('ln_mix', 'ln_mlp') else 2)
         for k_ in ALL_NAMES if k_ != 'meta_tokens' and k_ not in large}
    red = _KeepGrads() if red is None else red
    round_of = {3: 0, 2: 0, 1: 1, 0: 2}
    rounds, pending = {}, None

    def slabs_in(nm, r):
        return next((l0, l1) for n, l0, l1 in GATHER_ROUNDS[2 - r] if n == nm)

    def dw_into(nm, i, a, b, **kw):
        (l0, l1), cur = slabs_in(nm, round_of[i]), rounds.setdefault(round_of[i], {})
        cur[nm] = _mm(a, b, 'tn', stack=(l1 - l0, _layer_slab(nm, i) - l0, cur.get(nm)), **kw)

    def put(nm, i, arr):
        rounds.setdefault(round_of[i], {})[nm] = arr[None]

    def host(fn, *args):
        nonlocal pending
        if pending is None or pending[1] is None:
            return fn(*args)
        (r, rider), pending = pending, None
        outs = fn(*args, rider=rider)
        own = len(outs) - len(rider.out_shapes)
        red.finish(r, outs[own:])
        return outs[:own]

    for i in reversed(range(4)):
        j = i // 2
        s = saved[i]
        dw_into('mlp_w_down', i, s['u'], dh, name=f"mlp_down_dw{i}", a_fn=_sqrelu)
        du = _mm(dh, p['down'][i], 'nt', name=f"mlp_down_dx{i}", out_dtype=BF16,
                 epi=lambda r, uv: r * (2.0 * jnp.maximum(uv, 0.0)), extras=(s['u'],))
        dw_into('mlp_w_up', i, s['hn2'], du, name=f"mlp_up_dw{i}")
        d_hn2 = _mm(du, p['up'][i], 'nt', name=f"mlp_up_dx{i}")
        dh, dg = _rms_bwd(s['h1'], s['g_mlp'], d_hn2, dh, f"rms_mlp_b{i}")
        g['ln_mlp'][i] = dg[0]
        if i % 2 == 0:
            dw_into('ssd_w_out', i, s['yg'], dh, name=f"ssd_out_dw{i}")
            d_yg = _mm(dh, p['ssd_out'][j], 'nt', name=f"ssd_out_dx{i}")
            d_zxd, dcw, dcb, ddtb, dalog, ddsk, dng = host(_ssd_bwd, s['zxd'], s['states'], d_yg, s['consts'],
                                                           f"ssd_core_b{i}")
            g['ssd_conv_w'][j], g['ssd_conv_b'][j], g['ssd_norm'][j] = dcw, dcb[0], dng[0]
            g['ssd_dt_bias'][j], g['ssd_a_log'][j], g['ssd_d'][j] = (
                ddtb[0, :SSD_HEADS], dalog[0, :SSD_HEADS], ddsk[0, :SSD_HEADS])
            dw_into('ssd_w_in', i, s['hn'], d_zxd, name=f"ssd_in_dw{i}")
            d_hn = _mm(d_zxd, p['ssd_in'][j], 'nt', name=f"ssd_in_dx{i}")
        else:
            wo = _mm(s['o'], dh, 'tn', name=f"mla_out_dw{i}")
            put('mla_w_out', i, wo.reshape(MLA_HEADS, HEAD_SLOT, D_MODEL)[:, :MLA_V].reshape(-1, D_MODEL))
            do = _mm(dh, p['mla_out'][j], 'nt', name=f"mla_out_dx{i}")
            dob, delta = _attn_delta(do, s['o'], f"mla_attn_delta{i}")
            dq, dk, dv = host(_attn_bwd, s['q'], s['k'], s['v'], dob, s['lse'], delta, f"mla_attn_b{i}")

            def prep_bwd(_, qr, kn, kpe, cs, sn, dqv, dkv, dvv, qgv, kgv):
                _, vjp = jax.vjp(lambda a, b, c, d, e: _qk_prep(a, b, c, cs, sn, d, e), qr, kn, kpe, qgv, kgv)
                d_qr, d_kn, d_kpe, d_qg, d_kg = vjp((dqv, dkv))
                return d_qr, jnp.concatenate([d_kn, dvv], axis=1), d_kpe, d_qg, d_kg

            d_qraw, d_kvraw, d_kpe, d_qg, d_kg = _row_call(
                prep_bwd,
                [(s['q_raw'], MLA_WIDE, 0), (s['kv_raw'], MLA_WIDE, 0), (s['lat'], HEAD_SLOT, 2),
                 (cos_t, HEAD_SLOT, 0), (sin_t, HEAD_SLOT, 0), (dq, MLA_WIDE, 0), (dk, MLA_WIDE, 0),
                 (dv, MLA_WIDE, 0)],
                [s['qg'], s['kg']], [(MLA_WIDE, BF16), (2 * MLA_WIDE, BF16), (HEAD_SLOT, F32)],
                [(1, HEAD_SLOT), (1, HEAD_SLOT)], n_rows=lp, tile=_pick(lp, (128,)), name=f"mla_qkprep_b{i}")
            g['mla_q_norm'][j], g['mla_k_norm'][j] = d_qg[0, :MLA_QK], d_kg[0, :MLA_QK]
            wqb = _mm(s['qn'], d_qraw, 'tn', name=f"mla_qb_dw{i}")
            put('mla_w_q_b', i, wqb.reshape(MLA_Q_RANK, MLA_HEADS, HEAD_SLOT)[:, :, :MLA_QK].reshape(MLA_Q_RANK, -1))
            d_qn = _mm(d_qraw, p['mla_qb'][j], 'nt', name=f"mla_qb_dx{i}")
            wkvb = _mm(s['kvn'], d_kvraw, 'tn', name=f"mla_kvb_dw{i}").reshape(MLA_KV_RANK, 2, MLA_HEADS, HEAD_SLOT)
            put('mla_w_kv_b', i, jnp.concatenate([wkvb[:, 0, :, :MLA_NOPE], wkvb[:, 1, :, :MLA_V]],
                                                 axis=-1).reshape(MLA_KV_RANK, -1))
            d_kvn = _mm(d_kvraw, p['mla_kvb'][j], 'nt', name=f"mla_kvb_dx{i}")

            def lat_bwd(_, kvl, ql, dkvn, dqn, dkpe, kvgv, qagv):
                _, vjp = jax.vjp(_lat_norm, kvl, ql, kvgv, qagv)
                d_kvl, d_ql, d_kvg, d_qag = vjp((dkvn, dqn))
                return jnp.concatenate([d_kvl, dkpe, d_ql], axis=1), d_kvg, d_qag

            d_lat, d_kvg, d_qag = _row_call(
                lat_bwd, [(s['lat'], MLA_KV_RANK, 0), (s['lat'], MLA_Q_RANK, 1), (d_kvn, MLA_KV_RANK, 0),
                          (d_qn, MLA_Q_RANK, 0), (d_kpe, HEAD_SLOT, 0)],
                [s['kvg'], s['qag']], [(LAT_PAD, BF16)], [(1, MLA_KV_RANK), (1, MLA_Q_RANK)],
                n_rows=lp, tile=rt, name=f"mla_latnorm_b{i}")
            g['mla_kv_a_norm'][j], g['mla_q_a_norm'][j] = d_kvg[0], d_qag[0]
            win = _mm(s['hn'], d_lat, 'tn', name=f"mla_in_dw{i}")
            put('mla_w_in', i, jnp.concatenate(
                [win[:, MLA_KV_RANK + HEAD_SLOT:], win[:, :MLA_KV_RANK],
                 win[:, MLA_KV_RANK + MLA_NOPE:MLA_KV_RANK + MLA_QK]], axis=1))
            d_hn = _mm(d_lat, p['mla_in'][j], 'nt', name=f"mla_in_dx{i}")
        dh, dg = _rms_bwd(s['h0'], s['g_mix'], d_hn, dh, f"rms_mix_b{i}")
        g['ln_mix'][i] = dg[0]
        if i <= 2:
            pending = (round_of[i], red.begin(round_of[i], rounds.pop(round_of[i])))

    if pending[1] is not None:
        red.finish(pending[0], _run_rider(pending[1], "rs_exchange_last"))
    grads = {k_: jnp.stack(v_) for k_, v_ in g.items()}
    grads['meta_tokens'] = dh[NPAD:NPAD + N_META]
    return loss_row, dh[NPAD + N_META:], grads, red


def _all_gather8(shard, name):
    m_per, n = shard.shape

    def body(x_ref, out_ref, send_sems, recv_sems, local_sem):
        x, y, c = lax.axis_index("x"), lax.axis_index("y"), lax.axis_index("c")
        me, sibling = (x, y, c), (x, y, 1 - c)
        chips = [(1 - x, y), (x, 1 - y), (1 - x, 1 - y)]

        def rows(px, py, pc):
            return out_ref.at[pl.ds((4 * px + 2 * py + pc) * m_per, m_per), :]

        def copy(k, block, to, src=None):
            return pltpu.make_async_remote_copy(
                src_ref=rows(*block) if src is None else src, dst_ref=rows(*block),
                send_sem=send_sems.at[k], recv_sem=recv_sems.at[k], device_id=to, device_id_type=MESH)

        mine = pltpu.make_async_copy(x_ref, rows(*me), local_sem)
        mine.start()
        first = [copy(0, me, sibling, src=x_ref)]
        first += [copy(1 + j, me, (*chip, c), src=x_ref) for j, chip in enumerate(chips)]
        for cp in first:
            cp.start()
        passed = [copy(4 + j, (*chip, c), sibling) for j, chip in enumerate(chips)]
        for j, chip in enumerate(chips):
            copy(1 + j, (*chip, c), me).wait_recv()
            passed[j].start()
        copy(0, sibling, me).wait_recv()
        for j, chip in enumerate(chips):
            copy(4 + j, (*chip, 1 - c), me).wait_recv()
        for cp in first + passed:
            cp.wait_send()
        mine.wait()

    return pl.pallas_call(
        body, name=name,
        out_shape=jax.ShapeDtypeStruct((8 * m_per, n), shard.dtype),
        in_specs=[pl.BlockSpec(memory_space=pl.ANY)],
        out_specs=pl.BlockSpec(memory_space=pl.ANY),
        scratch_shapes=[pltpu.SemaphoreType.DMA((7,)), pltpu.SemaphoreType.DMA((7,)), pltpu.SemaphoreType.DMA],
    )(shard)


def _mesh_pos():
    return lax.axis_index("x"), lax.axis_index("y"), lax.axis_index("c")


def _half_rows(pc, h):
    return pl.ds(pl.multiple_of(pc * h, 16), h)


def _whole_view(ref, kind, shard_shape, k, pc):
    _, r, c = shard_shape
    rows = _half_rows(pc, r // 2)
    if kind == 'row':
        return ref.at[:, k, rows, :]
    if kind == 'col':
        return ref.at[:, rows, pl.ds(pl.multiple_of(k * c, 128), c)]
    return ref.at[k, :, rows, :]


def _whole_shape(kind, shard_shape, rows=None):
    l, r, c = shard_shape
    r = r if rows is None else rows
    return {'row': (l, 4, r, c), 'col': (l, r, 4 * c), 'colx': (4, l, r, c)}[kind]


def _gather_rider(shards, kinds):
    n = len(shards)
    shapes = [s.shape for s in shards]

    def plan(ins, outs, sems):
        send_sems, recv_sems, local_sems = sems
        x, y, c = _mesh_pos()
        me, sibling = (x, y, c), (x, y, 1 - c)
        chips = [(1 - x, y), (x, 1 - y), (1 - x, 1 - y)]

        def place(a, px, py, pc):
            return _whole_view(outs[a], kinds[a], shapes[a], 2 * px + py, pc)

        def own(a):
            return ins[a].at[:, _half_rows(c, shapes[a][1] // 2), :]

        def copy(a, k, block, to, src=None):
            return pltpu.make_async_remote_copy(
                src_ref=place(a, *block) if src is None else src, dst_ref=place(a, *block),
                send_sem=send_sems.at[7 * a + k], recv_sem=recv_sems.at[7 * a + k],
                device_id=to, device_id_type=MESH)

        mine = [pltpu.make_async_copy(own(a), place(a, *me), local_sems.at[a]) for a in range(n)]
        first = [copy(a, 1 + j, me, (*chip, c), src=own(a)) for j, chip in enumerate(chips) for a in range(n)]
        first += [copy(a, 0, me, sibling, src=own(a)) for a in range(n)]
        return copy, mine, first, chips, me, sibling, c

    def start(ins, outs, sems):
        _, mine, first, *_ = plan(ins, outs, sems)
        for cp in first + mine:
            cp.start()

    def finish(ins, outs, sems):
        copy, mine, first, chips, me, sibling, c = plan(ins, outs, sems)
        passed = []
        for j, chip in enumerate(chips):
            for a in range(n):
                copy(a, 1 + j, (*chip, c), me).wait_recv()
                passed.append(copy(a, 4 + j, (*chip, c), sibling))
                passed[-1].start()
        for a in range(n):
            copy(a, 0, sibling, me).wait_recv()
        for j, chip in enumerate(chips):
            for a in range(n):
                copy(a, 4 + j, (*chip, 1 - c), me).wait_recv()
        for cp in first + passed:
            cp.wait_send()
        for cp in mine:
            cp.wait()

    return _Rider(
        shards, [jax.ShapeDtypeStruct(_whole_shape(k, s.shape), s.dtype) for k, s in zip(kinds, shards)],
        [pltpu.SemaphoreType.DMA((7 * n,)), pltpu.SemaphoreType.DMA((7 * n,)), pltpu.SemaphoreType.DMA((n,))],
        start, finish)


def _run_rider(rider, name):
    ni, no = len(rider.operands), len(rider.out_shapes)

    def body(*refs):
        ride = (refs[:ni], refs[ni:ni + no], refs[ni + no:])
        rider.start(*ride)
        rider.finish(*ride)

    return pl.pallas_call(
        body, name=name, out_shape=rider.out_shapes,
        in_specs=[pl.BlockSpec(memory_space=pl.ANY)] * ni,
        out_specs=[pl.BlockSpec(memory_space=pl.ANY)] * no,
        scratch_shapes=rider.scratch,
    )(*rider.operands)


def _rs_swap(wholes, kinds, shapes, name):
    n = len(wholes)

    def body(*refs):
        ins, outs = refs[:n], refs[n:2 * n]
        send_sems, recv_sems = refs[2 * n:]
        x, y, c = _mesh_pos()
        cps = []
        for a in range(n):
            rows = _half_rows(1 - c, shapes[a][1] // 2)
            src = ins[a].at[:, rows, :] if kinds[a] == 'col' else ins[a].at[:, :, rows, :]
            cps.append(pltpu.make_async_remote_copy(
                src_ref=src, dst_ref=outs[a], send_sem=send_sems.at[a], recv_sem=recv_sems.at[a],
                device_id=(x, y, 1 - c), device_id_type=MESH))
        for cp in cps:
            cp.start()
        for cp in cps:
            cp.wait()

    return pl.pallas_call(
        body, name=name,
        out_shape=[jax.ShapeDtypeStruct(_whole_shape(k, s, s[1] // 2), w.dtype)
                   for k, s, w in zip(kinds, shapes, wholes)],
        in_specs=[pl.BlockSpec(memory_space=pl.ANY)] * n,
        out_specs=[pl.BlockSpec(memory_space=pl.ANY)] * n,
        scratch_shapes=[pltpu.SemaphoreType.DMA((n,)), pltpu.SemaphoreType.DMA((n,))],
    )(*wholes)


def _exchange_rider(parts, kinds, shapes):
    n = len(parts)

    def plan(ins, outs, sems):
        send_sems, recv_sems, local_sems = sems
        x, y, c = _mesh_pos()
        kme = 2 * x + y
        chips = [(1 - x, y), (x, 1 - y), (1 - x, 1 - y)]

        def slab(a, k):
            if kinds[a] == 'row':
                return ins[a].at[:, k]
            if kinds[a] == 'col':
                cw = shapes[a][2]
                return ins[a].at[:, :, pl.ds(pl.multiple_of(k * cw, 128), cw)]
            return ins[a].at[k]

        cps = [pltpu.make_async_remote_copy(
            src_ref=slab(a, 2 * px + py), dst_ref=outs[a].at[kme], send_sem=send_sems.at[3 * a + j],
            recv_sem=recv_sems.at[3 * a + j], device_id=(px, py, c), device_id_type=MESH)
            for j, (px, py) in enumerate(chips) for a in range(n)]
        return cps + [pltpu.make_async_copy(slab(a, kme), outs[a].at[kme], local_sems.at[a]) for a in range(n)]

    def start(ins, outs, sems):
        for cp in plan(ins, outs, sems):
            cp.start()

    def finish(ins, outs, sems):
        for cp in plan(ins, outs, sems):
            cp.wait()

    return _Rider(
        parts, [jax.ShapeDtypeStruct((4, s[0], s[1] // 2, s[2]), p.dtype) for s, p in zip(shapes, parts)],
        [pltpu.SemaphoreType.DMA((3 * n,)), pltpu.SemaphoreType.DMA((3 * n,)), pltpu.SemaphoreType.DMA((n,))],
        start, finish)


def _rs_share(shards, slabs, name):
    n = len(shards)

    def body(*refs):
        outs = refs[n:2 * n]
        send_sems, recv_sems = refs[2 * n:]
        x, y, c = _mesh_pos()
        cps = []
        for a in range(n):
            l0, l1 = slabs[a]
            rows = outs[a].at[pl.ds(l0, l1 - l0), _half_rows(c, shards[a].shape[1] // 2), :]
            cps.append(pltpu.make_async_remote_copy(
                src_ref=rows, dst_ref=rows, send_sem=send_sems.at[a], recv_sem=recv_sems.at[a],
                device_id=(x, y, 1 - c), device_id_type=MESH))
        for cp in cps:
            cp.start()
        for cp in cps:
            cp.wait()

    return pl.pallas_call(
        body, name=name,
        out_shape=[jax.ShapeDtypeStruct(s.shape, s.dtype) for s in shards],
        in_specs=[pl.BlockSpec(memory_space=pl.ANY)] * n,
        out_specs=[pl.BlockSpec(memory_space=pl.ANY)] * n,
        input_output_aliases={a: a for a in range(n)},
        scratch_shapes=[pltpu.SemaphoreType.DMA((n,)), pltpu.SemaphoreType.DMA((n,))],
    )(*shards)


def _tile_rows(rows, cols, budget=2 * 1024 * 1024):
    for t in (1024, 512, 256, 128, 64, 32, 16, 8):
        if rows % t == 0 and t * cols * 4 <= budget:
            return t
    return rows


def _add_half(g3, r3, c_idx, name):
    a, h, n = r3.shape
    t = _tile_rows(h, n)
    nt = h // t

    def body(c_ref, g_ref, r_ref, o_ref):
        o_ref[...] = (g_ref[...] + r_ref[...]).astype(o_ref.dtype)

    return pl.pallas_call(
        body, name=name, out_shape=jax.ShapeDtypeStruct((a, h, n), BF16),
        grid_spec=pltpu.PrefetchScalarGridSpec(
            num_scalar_prefetch=1, grid=(a, nt),
            in_specs=[pl.BlockSpec((1, t, n), lambda k, i, c: (k, c[0] * nt + i, 0)),
                      pl.BlockSpec((1, t, n), lambda k, i, c: (k, i, 0))],
            out_specs=pl.BlockSpec((1, t, n), lambda k, i, c: (k, i, 0))),
        compiler_params=_cparams(dimension_semantics=("parallel", "parallel")),
    )(c_idx, g3, r3)


def _sum4(parts, c_idx, name, into):
    _, l, h, n = parts.shape
    n_slabs, l0, buf = into
    t = _tile_rows(h, n, 1024 * 1024)
    nt = h // t
    held = () if buf is None else (buf,)

    def body(c_ref, p_ref, *rest):
        pv = p_ref[...].astype(F32)
        rest[-1][...] = ((pv[0] + pv[1]) + pv[2]) + pv[3]

    return pl.pallas_call(
        body, name=name, out_shape=jax.ShapeDtypeStruct((n_slabs, 2 * h, n), F32),
        grid_spec=pltpu.PrefetchScalarGridSpec(
            num_scalar_prefetch=1, grid=(l, nt),
            in_specs=[pl.BlockSpec((4, 1, t, n), lambda k, i, c: (0, k, i, 0))]
            + [pl.BlockSpec(memory_space=pl.ANY)] * len(held),
            out_specs=pl.BlockSpec((1, t, n), lambda k, i, c: (l0 + k, c[0] * nt + i, 0))),
        input_output_aliases={2: 0} if held else {},
        compiler_params=_cparams(dimension_semantics=("parallel", "parallel")),
    )(c_idx, parts, *held)


def _sum8(parts, name):
    _, m, n = parts.shape

    def body(p_ref, o_ref):
        acc = p_ref[0]
        for d in range(1, 8):
            acc = acc + p_ref[d]
        o_ref[...] = acc

    return pl.pallas_call(body, name=name, out_shape=jax.ShapeDtypeStruct((m, n), F32))(parts)


def _adamw(wp, gp, mp, vp, name):
    r, n = wp.shape
    t = _tile_rows(r, n, 1024 * 1024)

    def body(w_ref, g_ref, m_ref, v_ref, d_ref, mo_ref, vo_ref):
        gv = g_ref[...]
        m2 = ADAM_B1 * m_ref[...] + (1.0 - ADAM_B1) * gv
        v2 = ADAM_B2 * v_ref[...] + (1.0 - ADAM_B2) * (gv * gv)
        m_hat = m2 / (1.0 - ADAM_B1 ** ADAM_STEP)
        v_hat = v2 / (1.0 - ADAM_B2 ** ADAM_STEP)
        d_ref[...] = -ADAM_LR * (m_hat / (jnp.sqrt(v_hat) + ADAM_EPS) + ADAM_WD * w_ref[...])
        mo_ref[...] = m2
        vo_ref[...] = v2

    spec = pl.BlockSpec((t, n), lambda i: (i, 0))
    return pl.pallas_call(
        body, name=name, out_shape=[jax.ShapeDtypeStruct((r, n), F32)] * 3, grid=(r // t,),
        in_specs=[spec] * 4, out_specs=[spec] * 3,
        compiler_params=_cparams(dimension_semantics=("parallel",)),
    )(wp, gp, mp, vp)


BIG = (('ssd_w_in', 'colx'), ('ssd_w_out', 'row'), ('mla_w_in', 'row'), ('mla_w_q_b', 'col'),
       ('mla_w_kv_b', 'col'), ('mla_w_out', 'row'), ('mlp_w_up', 'col'), ('mlp_w_down', 'row'))
SMALL_SHARDED = (('meta_tokens', 1), ('ssd_conv_w', 2), ('mla_q_a_norm', 1), ('mla_kv_a_norm', 1))
SMALL_REPL = ('ln_mix', 'ln_mlp', 'ssd_conv_b', 'ssd_dt_bias', 'ssd_a_log', 'ssd_d', 'ssd_norm',
              'mla_q_norm', 'mla_k_norm')
ALL_NAMES = ('meta_tokens', 'ln_mix', 'ln_mlp', 'ssd_w_in', 'ssd_conv_w', 'ssd_conv_b', 'ssd_dt_bias',
             'ssd_a_log', 'ssd_d', 'ssd_norm', 'ssd_w_out', 'mla_w_in', 'mla_q_a_norm', 'mla_w_q_b',
             'mla_kv_a_norm', 'mla_w_kv_b', 'mla_q_norm', 'mla_k_norm', 'mla_w_out', 'mlp_w_up', 'mlp_w_down')


_MLA_BIG = ('mla_w_in', 'mla_w_q_b', 'mla_w_kv_b', 'mla_w_out')
GATHER_ROUNDS = (
    (('ssd_w_in', 0, 1), ('ssd_w_out', 0, 1), ('mlp_w_up', 0, 1), ('mlp_w_down', 0, 1)),
    tuple((n, 0, 1) for n in _MLA_BIG) + (('mlp_w_up', 1, 2), ('mlp_w_down', 1, 2)),
    (('ssd_w_in', 1, 2), ('ssd_w_out', 1, 2)) + tuple((n, 1, 2) for n in _MLA_BIG)
    + (('mlp_w_up', 2, 4), ('mlp_w_down', 2, 4)),
)


class _GatheredWeights:
    def __init__(self, shards):
        self.shards, self.p, self.whole = shards, _no_matrices(), {}

    def _round(self, r):
        spec = GATHER_ROUNDS[r]
        return _gather_rider([self.shards[n][l0:l1] for n, l0, l1 in spec], [dict(BIG)[n] for n, _, _ in spec])

    def _take(self, r, outs):
        for (n, l0, l1), o in zip(GATHER_ROUNDS[r], outs):
            kind = dict(BIG)[n]
            for l in range(l0, l1):
                if kind == 'row':
                    m = o[l - l0].reshape(-1, o.shape[-1])
                elif kind == 'col':
                    m = o[l - l0]
                else:
                    m = jnp.concatenate([o[k, l - l0] for k in range(4)], axis=-1)
                self.whole[(n, l)] = m

    def ensure(self, i):
        if i == 0:
            self._take(0, _run_rider(self._round(0), "gather_first"))
        _prep_layer(self.p, i, lambda n: self.whole[(n, _layer_slab(n, i))])

    def rider(self, i):
        return self._round(i + 1) if i + 1 < len(GATHER_ROUNDS) else None

    def deliver(self, i, outs):
        if i + 1 < len(GATHER_ROUNDS):
            self._take(i + 1, outs)


class _ScatterGrads:
    def __init__(self, shard_shapes, c_idx):
        self.shard_shapes, self.c_idx, self.out = shard_shapes, c_idx, {}

    def begin(self, r, grads):
        spec = GATHER_ROUNDS[2 - r]
        kinds = [dict(BIG)[n] for n, _, _ in spec]
        shapes = [(l1 - l0,) + tuple(self.shard_shapes[n][1:]) for n, l0, l1 in spec]
        wholes = []
        for (n, _, _), kind, s in zip(spec, kinds, shapes):
            if kind == 'row':
                wholes.append(grads[n].reshape(s[0], 4, s[1], s[2]))
            elif kind == 'col':
                wholes.append(grads[n])
            else:
                wholes.append(jnp.stack([grads[n][..., k * s[2]:(k + 1) * s[2]] for k in range(4)]))
        recv = _rs_swap(wholes, kinds, shapes, f"rs_swap{r}")
        parts = []
        for (n, _, _), kind, s, gw, rc in zip(spec, kinds, shapes, wholes, recv):
            if kind == 'col':
                g3, r3 = gw, rc
            else:
                g3, r3 = gw.reshape(-1, s[1], s[2]), rc.reshape(-1, s[1] // 2, s[2])
            parts.append(_add_half(g3, r3, self.c_idx, f"rs_add{r}_{n}").reshape(rc.shape))
        return _exchange_rider(parts, kinds, shapes)

    def finish(self, r, outs):
        spec = GATHER_ROUNDS[2 - r]
        for (n, l0, _), part in zip(spec, outs):
            self.out[n] = _sum4(part, self.c_idx, f"rs_sum{r}_{n}",
                                into=(self.shard_shapes[n][0], l0, self.out.get(n)))
        shared = _rs_share([self.out[n] for n, _, _ in spec], [(l0, l1) for _, l0, l1 in spec], f"rs_share{r}")
        self.out.update(zip([n for n, _, _ in spec], shared))


def _pack(arrs, rows_mult):
    flat = jnp.concatenate([a.reshape(-1) for a in arrs])
    per = LANES * rows_mult
    pad = (-flat.shape[0]) % per
    if pad:
        flat = jnp.concatenate([flat, jnp.zeros((pad,), flat.dtype)])
    return flat.reshape(-1, LANES)


def _unpack(pack, shapes):
    flat = pack.reshape(-1)
    out, off = [], 0
    for shp in shapes:
        n = math.prod(shp)
        out.append(flat[off:off + n].reshape(shp))
        off += n
    return out


def _split4(full, axis):
    shp = full.shape
    r = full.reshape(shp[:axis] + (4, shp[axis] // 4) + shp[axis + 1:])
    return jnp.moveaxis(r, axis, 0)


def _join4(parts, axis):
    r = jnp.moveaxis(parts, 0, axis)
    shp = r.shape
    return r.reshape(shp[:axis] + (shp[axis] * shp[axis + 1],) + shp[axis + 2:])


def _gather_params(shards, table, dtype, c, name):
    pack = _pack([shards[n].astype(dtype) for n, _ in table], 16)
    half = pack.shape[0] // 2
    mine = lax.dynamic_slice_in_dim(pack, c * half, half, axis=0)
    full = _all_gather8(mine, name).reshape(4, -1)
    out, off = {}, 0
    for n, ax in table:
        cnt = math.prod(shards[n].shape)
        out[n] = _join4(full[:, off:off + cnt].reshape((4,) + shards[n].shape), ax)
        off += cnt
    return out


def kernel(x, meta_tokens, ln_mix, ln_mlp, ssd_w_in, ssd_conv_w, ssd_conv_b, ssd_dt_bias, ssd_a_log, ssd_d, ssd_norm, ssd_w_out, mla_w_in, mla_q_a_norm, mla_w_q_b, mla_kv_a_norm, mla_w_kv_b, mla_q_norm, mla_k_norm, mla_w_out, mlp_w_up, mlp_w_down, loss_target, m_meta_tokens, m_ln_mix, m_ln_mlp, m_ssd_w_in, m_ssd_conv_w, m_ssd_conv_b, m_ssd_dt_bias, m_ssd_a_log, m_ssd_d, m_ssd_norm, m_ssd_w_out, m_mla_w_in, m_mla_q_a_norm, m_mla_w_q_b, m_mla_kv_a_norm, m_mla_w_kv_b, m_mla_q_norm, m_mla_k_norm, m_mla_w_out, m_mlp_w_up, m_mlp_w_down, v_meta_tokens, v_ln_mix, v_ln_mlp, v_ssd_w_in, v_ssd_conv_w, v_ssd_conv_b, v_ssd_dt_bias, v_ssd_a_log, v_ssd_d, v_ssd_norm, v_ssd_w_out, v_mla_w_in, v_mla_q_a_norm, v_mla_w_q_b, v_mla_kv_a_norm, v_mla_w_kv_b, v_mla_q_norm, v_mla_k_norm, v_mla_w_out, v_mlp_w_up, v_mlp_w_down):
    w_sh = dict(meta_tokens=meta_tokens, ln_mix=ln_mix, ln_mlp=ln_mlp, ssd_w_in=ssd_w_in, ssd_conv_w=ssd_conv_w, ssd_conv_b=ssd_conv_b, ssd_dt_bias=ssd_dt_bias, ssd_a_log=ssd_a_log, ssd_d=ssd_d, ssd_norm=ssd_norm, ssd_w_out=ssd_w_out, mla_w_in=mla_w_in, mla_q_a_norm=mla_q_a_norm, mla_w_q_b=mla_w_q_b, mla_kv_a_norm=mla_kv_a_norm, mla_w_kv_b=mla_w_kv_b, mla_q_norm=mla_q_norm, mla_k_norm=mla_k_norm, mla_w_out=mla_w_out, mlp_w_up=mlp_w_up, mlp_w_down=mlp_w_down)
    m_sh = dict(meta_tokens=m_meta_tokens, ln_mix=m_ln_mix, ln_mlp=m_ln_mlp, ssd_w_in=m_ssd_w_in, ssd_conv_w=m_ssd_conv_w, ssd_conv_b=m_ssd_conv_b, ssd_dt_bias=m_ssd_dt_bias, ssd_a_log=m_ssd_a_log, ssd_d=m_ssd_d, ssd_norm=m_ssd_norm, ssd_w_out=m_ssd_w_out, mla_w_in=m_mla_w_in, mla_q_a_norm=m_mla_q_a_norm, mla_w_q_b=m_mla_w_q_b, mla_kv_a_norm=m_mla_kv_a_norm, mla_w_kv_b=m_mla_w_kv_b, mla_q_norm=m_mla_q_norm, mla_k_norm=m_mla_k_norm, mla_w_out=m_mla_w_out, mlp_w_up=m_mlp_w_up, mlp_w_down=m_mlp_w_down)
    v_sh = dict(meta_tokens=v_meta_tokens, ln_mix=v_ln_mix, ln_mlp=v_ln_mlp, ssd_w_in=v_ssd_w_in, ssd_conv_w=v_ssd_conv_w, ssd_conv_b=v_ssd_conv_b, ssd_dt_bias=v_ssd_dt_bias, ssd_a_log=v_ssd_a_log, ssd_d=v_ssd_d, ssd_norm=v_ssd_norm, ssd_w_out=v_ssd_w_out, mla_w_in=v_mla_w_in, mla_q_a_norm=v_mla_q_a_norm, mla_w_q_b=v_mla_w_q_b, mla_kv_a_norm=v_mla_kv_a_norm, mla_w_kv_b=v_mla_w_kv_b, mla_q_norm=v_mla_q_norm, mla_k_norm=v_mla_k_norm, mla_w_out=v_mla_w_out, mlp_w_up=v_mlp_w_up, mlp_w_down=v_mlp_w_down)

    cx, cy, cc = lax.axis_index("x"), lax.axis_index("y"), lax.axis_index("c")
    chip = 2 * cx + cy

    c_idx = cc.reshape(1).astype(jnp.int32)
    big_names = [n for n, _ in BIG]
    shapes = [w_sh[n].shape for n in big_names]

    w = {n: w_sh[n] for n in SMALL_REPL}
    w.update(_gather_params(w_sh, SMALL_SHARDED, F32, cc, "gather_small"))
    big = _GatheredWeights({n: w_sh[n].astype(BF16) for n in big_names})
    red = _ScatterGrads({n: w_sh[n].shape for n in big_names}, c_idx)

    loss_row, grad_x, grads, red = _local_step(x[0], loss_target[0], w, big, red)
    loss = lax.psum(jnp.sum(loss_row), ("x", "y", "c"))
    g_sh = dict(red.out)

    small_names = tuple(n for n, _ in SMALL_SHARDED) + SMALL_REPL
    sp = _pack([grads[n] for n in small_names], 8)
    srows = sp.shape[0]
    s_all = _sum8(_all_gather8(sp, "ar_small_gather").reshape(8, srows, LANES), "ar_small_sum")
    s_full = dict(zip(small_names, _unpack(s_all, [grads[n].shape for n in small_names])))
    for n, ax in SMALL_SHARDED:
        g_sh[n] = lax.dynamic_index_in_dim(_split4(s_full[n], ax), chip, axis=0, keepdims=False)
    for n in SMALL_REPL:
        g_sh[n] = s_full[n]

    delta, new_m, new_v = {}, {}, {}
    for n, s in zip(big_names, shapes):
        res = _adamw(*[t[n].reshape(-1, s[2]) for t in (w_sh, g_sh, m_sh, v_sh)], f"adamw_{n}")
        delta[n], new_m[n], new_v[n] = [r.reshape(s) for r in res]
    d_s, m_s, v_s = _adamw(*[_pack([t[n] for n in small_names], 8) for t in (w_sh, g_sh, m_sh, v_sh)],
                           "adamw_small")
    for dst, ps in ((delta, d_s), (new_m, m_s), (new_v, v_s)):
        dst.update(zip(small_names, _unpack(ps, [w_sh[n].shape for n in small_names])))

    return (loss, grad_x[None], *[g_sh[n] for n in ALL_NAMES], *[delta[n] for n in ALL_NAMES],
            *[new_m[n] for n in ALL_NAMES], *[new_v[n] for n in ALL_NAMES])
```

```python
import functools
import math

import jax
import jax.numpy as jnp
from jax import lax
from jax.experimental import pallas as pl
from jax.experimental.pallas import tpu as pltpu

F32 = jnp.float32
BF16 = jnp.bfloat16
MESH = pl.DeviceIdType.MESH
_NN = (((1,), (0,)), ((), ()))
_NT = (((1,), (1,)), ((), ()))
_TN = (((0,), (0,)), ((), ()))

D_MODEL = 1024
N_META = 16
EPS = 1e-6
SSD_D_INNER = 2048
SSD_HEADS = 32
SSD_HEAD_DIM = 64
SSD_GROUPS = 8
SSD_HPG = 4
SSD_STATE = 128
SSD_CONV = 4
CHUNK = 128
SSD_IN_DIM = 6176
SSD_IN_PAD = 6272
MLA_HEADS = 16
MLA_NOPE = 64
MLA_ROPE = 32
MLA_V = 64
MLA_QK = 96
MLA_Q_RANK = 384
MLA_KV_RANK = 256
HEAD_SLOT = 128
MLA_WIDE = MLA_HEADS * HEAD_SLOT
HEADS_PER_STEP = 2
LAT_PAD = 768
ROPE_THETA = 10000.0
D_FF = 4096
NPAD = CHUNK - N_META
ADAM_LR, ADAM_B1, ADAM_B2, ADAM_EPS, ADAM_WD, ADAM_STEP = 0.001, 0.9, 0.999, 1e-08, 0.01, 10
LANES = 1024
VMEM_LIMIT = 56 * 1024 * 1024


def _pick(n, cands):
    for c in cands:
        if n % c == 0:
            return c
    return n


def _cparams(**kw):
    return pltpu.CompilerParams(vmem_limit_bytes=VMEM_LIMIT, **kw)


def _mm(a, b, dims, *, name, out_dtype=F32, a_fn=None, epi=None, extras=(), stack=None):
    if dims == 'nn':
        (M, K), (K2, N) = a.shape, b.shape
    elif dims == 'nt':
        (M, K), (N, K2) = a.shape, b.shape
    else:
        (K, M), (K2, N) = a.shape, b.shape
    assert K == K2, (a.shape, b.shape, dims)
    if dims == 'tn':
        tm = _pick(M, (1024, 768, 512, 384, 256, 128))
        tn = _pick(N, (1024, 896, 768, 512, 384, 256, 128))
        tk = _pick(K, (1408, 1024, 512, 384, 256, 128))
    else:
        tm = _pick(M, (1408, 1024, 512, 384, 256, 128))
        tn = _pick(N, (1024, 896, 768, 512, 384, 256, 128))
        tk = _pick(K, (1024, 896, 768, 512, 384, 256, 128))
    nk = K // tk
    if dims == 'nn':
        a_spec = pl.BlockSpec((tm, tk), lambda i, j, k: (i, k))
        b_spec = pl.BlockSpec((tk, tn), lambda i, j, k: (k, j))
        dn = (((1,), (0,)), ((), ()))
    elif dims == 'nt':
        a_spec = pl.BlockSpec((tm, tk), lambda i, j, k: (i, k))
        b_spec = pl.BlockSpec((tn, tk), lambda i, j, k: (j, k))
        dn = (((1,), (1,)), ((), ()))
    else:
        a_spec = pl.BlockSpec((tk, tm), lambda i, j, k: (k, i))
        b_spec = pl.BlockSpec((tk, tn), lambda i, j, k: (k, j))
        dn = (((0,), (0,)), ((), ()))
    o_spec = pl.BlockSpec((tm, tn), lambda i, j, k: (i, j))
    n_ex = len(extras)
    out_shape = jax.ShapeDtypeStruct((M, N), out_dtype)
    out_spec, held, aliases = o_spec, (), {}
    if stack is not None:
        n_slabs, slab, buf = stack
        out_shape = jax.ShapeDtypeStruct((n_slabs, M, N), out_dtype)
        out_spec = pl.BlockSpec((None, tm, tn), lambda i, j, k: (slab, i, j))
        if buf is not None:
            held, aliases = (buf,), {2 + n_ex: 0}

    def body(a_ref, b_ref, *rest):
        ex_refs, o_ref, acc = rest[:n_ex], rest[n_ex + len(held)], rest[n_ex + len(held) + 1]
        k = pl.program_id(2)

        @pl.when(k == 0)
        def _():
            acc[...] = jnp.zeros_like(acc)

        av = a_ref[...]
        if a_fn is not None:
            av = a_fn(av)
        acc[...] += lax.dot_general(av.astype(BF16), b_ref[...].astype(BF16), dn,
                                    preferred_element_type=F32)

        @pl.when(k == nk - 1)
        def _():
            r = acc[...]
            if epi is not None:
                r = epi(r, *[e[...] for e in ex_refs])
            o_ref[...] = r.astype(out_dtype)

    return pl.pallas_call(
        body, name=name,
        out_shape=out_shape,
        grid=(M // tm, N // tn, nk),
        in_specs=[a_spec, b_spec] + [o_spec] * n_ex + [pl.BlockSpec(memory_space=pl.ANY)] * len(held),
        out_specs=out_spec,
        input_output_aliases=aliases,
        scratch_shapes=[pltpu.VMEM((tm, tn), F32)],
        compiler_params=_cparams(dimension_semantics=("parallel", "parallel", "arbitrary")),
    )(a, b, *extras, *held)


def _row_call(fn, rows, consts, out_rows, out_accs=(), *, n_rows, tile, name):
    n_r, n_c, n_o, n_a = len(rows), len(consts), len(out_rows), len(out_accs)
    steps = n_rows // tile

    def body(*refs):
        r_refs = refs[:n_r]
        c_refs = refs[n_r:n_r + n_c]
        o_refs = refs[n_r + n_c:n_r + n_c + n_o]
        a_refs = refs[n_r + n_c + n_o:]
        i = pl.program_id(0)
        res = fn(i, *[r[...] for r in r_refs], *[c[...] for c in c_refs])
        for o_ref, val in zip(o_refs, res[:n_o]):
            o_ref[...] = val.astype(o_ref.dtype)

        @pl.when(i == 0)
        def _():
            for a_ref in a_refs:
                a_ref[...] = jnp.zeros_like(a_ref)

        for a_ref, val in zip(a_refs, res[n_o:]):
            a_ref[...] += val

    in_specs = [pl.BlockSpec((tile, w), functools.partial(lambda i, cb: (i, cb), cb=cb))
                for (_, w, cb) in rows]
    in_specs += [pl.BlockSpec(c.shape, lambda i: (0, 0)) for c in consts]
    out_specs = [pl.BlockSpec((tile, c), lambda i: (i, 0)) for (c, _) in out_rows]
    out_specs += [pl.BlockSpec(s, lambda i: (0, 0)) for s in out_accs]
    out_shape = [jax.ShapeDtypeStruct((n_rows, c), dt) for (c, dt) in out_rows]
    out_shape += [jax.ShapeDtypeStruct(s, F32) for s in out_accs]
    return pl.pallas_call(
        body, name=name, out_shape=out_shape, grid=(steps,),
        in_specs=in_specs, out_specs=out_specs,
        compiler_params=_cparams(dimension_semantics=("arbitrary",)),
    )(*[r[0] for r in rows], *consts)


def _row_mask(i, tile):
    r = i * tile + lax.broadcasted_iota(jnp.int32, (tile, 1), 0)
    return (r >= NPAD).astype(F32)


def _rms(x, g):
    return x * lax.rsqrt(jnp.mean(x * x, axis=-1, keepdims=True) + EPS) * g


def _silu(x):
    return x * (0.5 * jnp.tanh(0.5 * x) + 0.5)


def _softplus(x):
    return jnp.maximum(x, 0.0) + jnp.log(1.0 + jnp.exp(-jnp.abs(x)))


def _rms_fwd(h, g, name):
    lp = h.shape[0]
    return _row_call(lambda i, hv, gv: (_rms(hv, gv),), [(h, D_MODEL, 0)], [g],
                     [(D_MODEL, BF16)], n_rows=lp, tile=_pick(lp, (384, 256, 128)), name=name)[0]


def _rms_bwd(h, g, d_hn, d_res, name):
    lp = h.shape[0]
    tile = _pick(lp, (384, 256, 128))

    def fn(i, hv, dv, rv, gv):
        _, vjp = jax.vjp(_rms, hv, gv)
        dh, dg = vjp(dv)
        return (rv + dh) * _row_mask(i, tile), dg

    return _row_call(fn, [(h, D_MODEL, 0), (d_hn, D_MODEL, 0), (d_res, D_MODEL, 0)], [g],
                     [(D_MODEL, F32)], [(1, D_MODEL)], n_rows=lp, tile=tile, name=name)


@functools.partial(jax.custom_vjp, nondiff_argnums=(1,))
def _roll_rows(x, s):
    return pltpu.roll(x, s, 0)


def _roll_rows_fwd(x, s):
    return pltpu.roll(x, s, 0), None


def _roll_rows_bwd(s, _, ct):
    return (pltpu.roll(ct, (ct.shape[0] - s) % ct.shape[0], 0),)


_roll_rows.defvjp(_roll_rows_fwd, _roll_rows_bwd)


def _conv_silu(cur, halo, w_rows, b):
    full = jnp.concatenate([halo, cur], axis=0)
    acc = cur * w_rows[SSD_CONV - 1] + b
    for k in range(SSD_CONV - 1):
        acc = acc + _roll_rows(full, SSD_CONV - 1 - k)[8:] * w_rows[k]
    return _silu(acc)


def _split3(v):
    hi = v.astype(BF16)
    r1 = v - hi.astype(F32)
    mid = r1.astype(BF16)
    lo = (r1 - mid.astype(F32)).astype(BF16)
    return hi, mid, lo


def _select_right(v, sel, dn):
    return sum(lax.dot_general(p, sel, dn, preferred_element_type=F32) for p in _split3(v))


@jax.custom_vjp
def _expand_heads(v, e_mat):
    return _select_right(v, e_mat, _NN)


def _expand_heads_fwd(v, e_mat):
    return _select_right(v, e_mat, _NN), e_mat


def _expand_heads_bwd(e_mat, ct):
    return _select_right(ct, e_mat, _NT), jnp.zeros_like(e_mat)


_expand_heads.defvjp(_expand_heads_fwd, _expand_heads_bwd)


@jax.custom_vjp
def _cumsum_rows(a, tri):
    return sum(lax.dot_general(tri, p, _NN, preferred_element_type=F32) for p in _split3(a))


def _cumsum_rows_fwd(a, tri):
    return _cumsum_rows(a, tri), tri


def _cumsum_rows_bwd(tri, ct):
    return (sum(lax.dot_general(tri, p, _TN, preferred_element_type=F32) for p in _split3(ct)),
            jnp.zeros_like(tri))


_cumsum_rows.defvjp(_cumsum_rows_fwd, _cumsum_rows_bwd)


def _ssd_chunk(mask, z, xs_pre, bc_pre, halo_x, halo_bc, dt_pre, st, cwx0, cwx1, cwx2, cwx3,
               cwb0, cwb1, cwb2, cwb3, cb_x, cb_bc, dtb, alog, dsk, ng):
    L = CHUNK
    lane_h = lax.broadcasted_iota(jnp.int32, (1, 128), 1)
    head_ok = (lane_h < SSD_HEADS).astype(F32)
    e_mat = (lax.broadcasted_iota(jnp.int32, (128, SSD_D_INNER), 1) // SSD_HEAD_DIM
             == lax.broadcasted_iota(jnp.int32, (128, SSD_D_INNER), 0)).astype(BF16)
    ri = lax.broadcasted_iota(jnp.int32, (L, L), 0)
    ci = lax.broadcasted_iota(jnp.int32, (L, L), 1)
    causal = ri >= ci

    xs = _conv_silu(xs_pre, halo_x, (cwx0, cwx1, cwx2, cwx3), cb_x) * mask
    bc = _conv_silu(bc_pre, halo_bc, (cwb0, cwb1, cwb2, cwb3), cb_bc) * mask
    dt = _softplus(dt_pre + dtb) * mask * head_ok
    a_dt = dt * (-jnp.exp(alog))
    a_cs = _cumsum_rows(a_dt, causal.astype(BF16))
    a_cs_t = a_cs.T
    row8 = lax.broadcasted_iota(jnp.int32, (8, 128), 0)
    last8 = jnp.where(row8 == 0, jnp.sum(a_dt, axis=0, keepdims=True), 0.0)
    dsk8 = jnp.where(row8 == 0, dsk, 0.0)
    wide = _expand_heads(jnp.concatenate([dt, a_cs, last8, dsk8], axis=0), e_mat)
    dt_e, acs_e = wide[0:L], wide[L:2 * L]
    last_e = jnp.sum(wide[2 * L:2 * L + 8], axis=0, keepdims=True)
    d_e = jnp.sum(wide[2 * L + 8:2 * L + 16], axis=0, keepdims=True)
    xdt = xs * dt_e
    dte_e = jnp.exp(last_e - acs_e)
    dfs_e = jnp.exp(acs_e)
    cd_e = jnp.exp(last_e)
    sub_h = lax.broadcasted_iota(jnp.int32, (128, L), 0)
    lane_hl = lax.broadcasted_iota(jnp.int32, (L, 128), 1)
    lane_g = lax.broadcasted_iota(jnp.int32, (1, SSD_HPG * SSD_HEAD_DIM), 1) // SSD_HEAD_DIM

    ys, new_st = [], []
    for g in range(SSD_GROUPS):
        b_g = bc[:, g * 128:(g + 1) * 128].astype(BF16)
        c_g = bc[:, 1024 + g * 128:1024 + (g + 1) * 128].astype(BF16)
        gs = slice(g * 256, (g + 1) * 256)
        xdt_g = xdt[:, gs]
        cb = lax.dot_general(c_g, b_g, (((1,), (1,)), ((), ())), preferred_element_type=F32)
        st_g = st[g * 128:(g + 1) * 128, :]
        y_g = lax.dot_general(c_g, st_g.astype(BF16), (((1,), (0,)), ((), ())),
                              preferred_element_type=F32) * dfs_e[:, gs]
        for j in range(SSD_HPG):
            h = g * SSD_HPG + j
            col = jnp.sum(jnp.where(lane_hl == h, a_cs, 0.0), axis=1, keepdims=True)
            row = jnp.sum(jnp.where(sub_h == h, a_cs_t, 0.0), axis=0, keepdims=True)
            dec = jnp.where(causal, jnp.exp(jnp.where(causal, col - row, 0.0)), 0.0)
            m_h = (cb * dec).astype(BF16)
            x_h = jnp.where(lane_g == j, xdt_g, 0.0).astype(BF16)
            y_g = y_g + lax.dot_general(m_h, x_h, (((1,), (0,)), ((), ())),
                                        preferred_element_type=F32)
        s_new = lax.dot_general(b_g, (xdt_g * dte_e[:, gs]).astype(BF16), (((0,), (0,)), ((), ())),
                                preferred_element_type=F32)
        new_st.append(st_g * cd_e[:, gs] + s_new)
        ys.append(y_g)
    y = jnp.concatenate(ys, axis=1) + xs * d_e
    gg = y * _silu(z)
    outs = []
    for g in range(SSD_GROUPS):
        sl = gg[:, g * 256:(g + 1) * 256]
        outs.append(sl * lax.rsqrt(jnp.mean(sl * sl, axis=-1, keepdims=True) + EPS))
    out = jnp.concatenate(outs, axis=1) * ng
    return out, jnp.concatenate(new_st, axis=0)


def _ssd_consts(conv_w, conv_b, dtb, alog, dsk, ng):
    return [conv_w, conv_b, dtb, alog, dsk, ng]


def _ssd_param_vals(cw_ref, cb_ref, dtb_ref, alog_ref, dsk_ref, ng_ref):
    cwx = [cw_ref[k:k + 1, 0:SSD_D_INNER] for k in range(SSD_CONV)]
    cwb = [cw_ref[k:k + 1, SSD_D_INNER:2 * SSD_D_INNER] for k in range(SSD_CONV)]
    return (*cwx, *cwb, cb_ref[:, 0:SSD_D_INNER], cb_ref[:, SSD_D_INNER:2 * SSD_D_INNER],
            dtb_ref[...], alog_ref[...], dsk_ref[...], ng_ref[...])


def _ssd_in_specs(rev, nc):
    def cidx(i):
        return (nc - 1 - i) if rev else i

    def halo(cb):
        return pl.BlockSpec((8, SSD_D_INNER), lambda i: (jnp.maximum(16 * cidx(i) - 1, 0), cb))

    return [
        pl.BlockSpec((CHUNK, SSD_D_INNER), lambda i: (cidx(i), 0)),
        pl.BlockSpec((CHUNK, SSD_D_INNER), lambda i: (cidx(i), 1)),
        pl.BlockSpec((CHUNK, SSD_D_INNER), lambda i: (cidx(i), 2)),
        halo(1), halo(2),
        pl.BlockSpec((CHUNK, 128), lambda i: (cidx(i), 48)),
    ]


class _Rider:
    def __init__(self, operands, out_shapes, scratch, start, finish):
        self.operands, self.out_shapes, self.scratch = list(operands), list(out_shapes), list(scratch)
        self.start, self.finish = start, finish


def _rider_split(rider, refs, n_in, n_out, n_scratch):
    if rider is None:
        return refs, None
    ni, no = len(rider.operands), len(rider.out_shapes)
    own = refs[:n_in] + refs[n_in + ni:n_in + ni + n_out] + refs[n_in + ni + n_out + no:n_in + ni + n_out + no + n_scratch]
    mine = (refs[n_in:n_in + ni], refs[n_in + ni + n_out:n_in + ni + n_out + no],
            refs[n_in + ni + n_out + no + n_scratch:])
    return own, mine


def _rider_args(rider):
    if rider is None:
        return [], [], [], []
    hbm = pl.BlockSpec(memory_space=pl.ANY)
    return ([hbm] * len(rider.operands), [hbm] * len(rider.out_shapes), rider.out_shapes, rider.scratch)


def _ssd_fwd(zxd, consts, name, rider=None):
    lp = zxd.shape[0]
    nc = lp // CHUNK

    def body(*refs):
        own, ride = _rider_split(rider, refs, 12, 2, 1)
        (z_ref, xs_ref, bc_ref, hx_ref, hb_ref, dt_ref, cw_ref, cb_ref, dtb_ref, alog_ref,
         dsk_ref, ng_ref, y_ref, st_ref, state) = own
        c = pl.program_id(0)

        @pl.when(c == 0)
        def _():
            state[...] = jnp.zeros_like(state)
            if ride is not None:
                rider.start(*ride)

        live = (c > 0).astype(F32)
        st_ref[0] = state[...]
        out, st_new = _ssd_chunk(
            _row_mask(c, CHUNK), z_ref[...], xs_ref[...], bc_ref[...], hx_ref[...] * live,
            hb_ref[...] * live, dt_ref[...], state[...],
            *_ssd_param_vals(cw_ref, cb_ref, dtb_ref, alog_ref, dsk_ref, ng_ref))
        y_ref[...] = out.astype(y_ref.dtype)
        state[...] = st_new

        if ride is not None:
            @pl.when(c == nc - 1)
            def _():
                rider.finish(*ride)

    r_in, r_out, r_shapes, r_scratch = _rider_args(rider)
    return pl.pallas_call(
        body, name=name,
        out_shape=[jax.ShapeDtypeStruct((lp, SSD_D_INNER), BF16),
                   jax.ShapeDtypeStruct((nc, SSD_GROUPS * SSD_STATE, 256), F32)] + r_shapes,
        grid=(nc,),
        in_specs=_ssd_in_specs(False, nc) + [pl.BlockSpec(c.shape, lambda i: (0, 0)) for c in consts] + r_in,
        out_specs=[pl.BlockSpec((CHUNK, SSD_D_INNER), lambda i: (i, 0)),
                   pl.BlockSpec((1, SSD_GROUPS * SSD_STATE, 256), lambda i: (i, 0, 0))] + r_out,
        scratch_shapes=[pltpu.VMEM((SSD_GROUPS * SSD_STATE, 256), F32)] + r_scratch,
        compiler_params=_cparams(dimension_semantics=("arbitrary",)),
    )(zxd, zxd, zxd, zxd, zxd, zxd, *consts, *(rider.operands if rider else ()))


def _ssd_bwd(zxd, states, d_y, consts, name, rider=None):
    lp = zxd.shape[0]
    nc = lp // CHUNK

    def body(*refs):
        own, ride = _rider_split(rider, refs, 14, 7, 3)
        (z_ref, xs_ref, bc_ref, hx_ref, hb_ref, dt_ref, st_ref, dy_ref, cw_ref, cb_ref, dtb_ref,
         alog_ref, dsk_ref, ng_ref, dz_ref, dcw_ref, dcb_ref, ddtb_ref, dalog_ref, ddsk_ref,
         dng_ref, d_state, d_hx, d_hb) = own
        i = pl.program_id(0)
        c = nc - 1 - i

        @pl.when(i == 0)
        def _():
            d_state[...] = jnp.zeros_like(d_state)
            d_hx[...] = jnp.zeros_like(d_hx)
            d_hb[...] = jnp.zeros_like(d_hb)
            for r in (dcw_ref, dcb_ref, ddtb_ref, dalog_ref, ddsk_ref, dng_ref):
                r[...] = jnp.zeros_like(r)
            if ride is not None:
                rider.start(*ride)

        live = (c > 0).astype(F32)
        fn = functools.partial(_ssd_chunk, _row_mask(c, CHUNK))
        prim = (z_ref[...], xs_ref[...], bc_ref[...], hx_ref[...] * live, hb_ref[...] * live,
                dt_ref[...], st_ref[0],
                *_ssd_param_vals(cw_ref, cb_ref, dtb_ref, alog_ref, dsk_ref, ng_ref))
        _, vjp = jax.vjp(fn, *prim)
        (d_z, d_xs, d_bc, g_hx, g_hb, d_dt, g_st, *d_par) = vjp((dy_ref[...], d_state[...]))
        zeros = jnp.zeros((CHUNK - 8, SSD_D_INNER), F32)
        d_xs = d_xs + jnp.concatenate([zeros, d_hx[...]], axis=0)
        d_bc = d_bc + jnp.concatenate([zeros, d_hb[...]], axis=0)
        dz_ref[:, 0:SSD_D_INNER] = d_z.astype(dz_ref.dtype)
        dz_ref[:, SSD_D_INNER:2 * SSD_D_INNER] = d_xs.astype(dz_ref.dtype)
        dz_ref[:, 2 * SSD_D_INNER:3 * SSD_D_INNER] = d_bc.astype(dz_ref.dtype)
        dz_ref[:, 3 * SSD_D_INNER:] = d_dt.astype(dz_ref.dtype)
        d_state[...] = g_st
        d_hx[...] = g_hx * live
        d_hb[...] = g_hb * live
        for k in range(SSD_CONV):
            dcw_ref[k:k + 1, 0:SSD_D_INNER] += d_par[k]
            dcw_ref[k:k + 1, SSD_D_INNER:2 * SSD_D_INNER] += d_par[SSD_CONV + k]
        dcb_ref[:, 0:SSD_D_INNER] += d_par[8]
        dcb_ref[:, SSD_D_INNER:2 * SSD_D_INNER] += d_par[9]
        ddtb_ref[...] += d_par[10]
        dalog_ref[...] += d_par[11]
        ddsk_ref[...] += d_par[12]
        dng_ref[...] += d_par[13]

        if ride is not None:
            @pl.when(i == nc - 1)
            def _():
                rider.finish(*ride)

    const_specs = [pl.BlockSpec(c.shape, lambda i: (0, 0)) for c in consts]
    r_in, r_out, r_shapes, r_scratch = _rider_args(rider)
    return pl.pallas_call(
        body, name=name,
        out_shape=[jax.ShapeDtypeStruct((lp, SSD_IN_PAD), BF16)]
        + [jax.ShapeDtypeStruct(c.shape, F32) for c in consts] + r_shapes,
        grid=(nc,),
        in_specs=_ssd_in_specs(True, nc)
        + [pl.BlockSpec((1, SSD_GROUPS * SSD_STATE, 256), lambda i: (nc - 1 - i, 0, 0)),
           pl.BlockSpec((CHUNK, SSD_D_INNER), lambda i: (nc - 1 - i, 0))] + const_specs + r_in,
        out_specs=[pl.BlockSpec((CHUNK, SSD_IN_PAD), lambda i: (nc - 1 - i, 0))] + const_specs + r_out,
        scratch_shapes=[pltpu.VMEM((SSD_GROUPS * SSD_STATE, 256), F32),
                        pltpu.VMEM((8, SSD_D_INNER), F32), pltpu.VMEM((8, SSD_D_INNER), F32)] + r_scratch,
        compiler_params=_cparams(dimension_semantics=("arbitrary",)),
    )(zxd, zxd, zxd, zxd, zxd, zxd, states, d_y, *consts, *(rider.operands if rider else ()))


@jax.custom_vjp
def _rot_half(x):
    lane = lax.broadcasted_iota(jnp.int32, x.shape, 1)
    lo = (lane >= MLA_NOPE) & (lane < MLA_NOPE + MLA_ROPE // 2)
    hi = (lane >= MLA_NOPE + MLA_ROPE // 2) & (lane < MLA_QK)
    down = pltpu.roll(x, HEAD_SLOT - MLA_ROPE // 2, 1)
    up = pltpu.roll(x, MLA_ROPE // 2, 1)
    return jnp.where(lo, -down, jnp.where(hi, up, 0.0))


def _rot_half_fwd(x):
    return _rot_half(x), None


def _rot_half_bwd(_, ct):
    return (-_rot_half(ct),)


_rot_half.defvjp(_rot_half_fwd, _rot_half_bwd)


def _head_norm_rope(t, gain, cos, sin):
    n = t * lax.rsqrt(jnp.sum(t * t, axis=-1, keepdims=True) * (1.0 / MLA_QK) + EPS) * gain
    return n * cos + _rot_half(n) * sin


def _qk_prep(q_raw, kn_raw, kpe, cos, sin, qg, kg):
    qs, ks = [], []
    for h in range(MLA_HEADS):
        sl = slice(h * HEAD_SLOT, (h + 1) * HEAD_SLOT)
        qs.append(_head_norm_rope(q_raw[:, sl], qg, cos, sin))
        ks.append(_head_norm_rope(kn_raw[:, sl] + kpe, kg, cos, sin))
    return jnp.concatenate(qs, axis=1), jnp.concatenate(ks, axis=1)


def _lat_norm(kv_lat, q_lat, kvg, qg):
    return _rms(kv_lat, kvg), _rms(q_lat, qg)


_NEG = -1e30
_SCALE = MLA_QK ** -0.5


STRIP = 128
_EXP2_SCALE = _SCALE * math.log2(math.e)


def _strip_mask(kind, blk, c, t):
    if kind is None:
        return None
    kpos = blk * t + c * STRIP + lax.broadcasted_iota(jnp.int32, (1, STRIP), 1)
    if kind == 'keys':
        return kpos >= NPAD
    qpos = blk * t + lax.broadcasted_iota(jnp.int32, (t, 1), 0)
    return (kpos <= qpos) & ((kpos >= NPAD) | (kpos == qpos))


def _attn_fwd(q, k, v, name, rider=None):
    lp = q.shape[0]
    t = _pick(lp, (384, 256, 128))
    tk = t
    nb = lp // t
    hp = HEADS_PER_STEP
    wide = hp * HEAD_SLOT
    heads = [slice(a * HEAD_SLOT, (a + 1) * HEAD_SLOT) for a in range(hp)]

    def body(*refs):
        (q_ref, k_ref, v_ref, o_ref, lse_ref), ride = _rider_split(rider, refs, 3, 2, 0)
        qi = pl.program_id(1)
        if ride is not None:
            @pl.when((pl.program_id(0) == 0) & (qi == 0))
            def _():
                rider.start(*ride)

        def update(a, ki, carry, mask):
            rows = pl.ds(pl.multiple_of(ki * tk, tk), tk)
            m, l, acc = carry
            s = lax.dot_general(q_ref[:, heads[a]], k_ref[rows, heads[a]], _NT, preferred_element_type=F32)
            if mask is not None:
                s = jnp.where(mask, s, _NEG)
            m_new = jnp.maximum(m, jnp.max(s, axis=-1, keepdims=True))
            alpha = jnp.exp2((m - m_new) * _EXP2_SCALE)
            p = jnp.exp2((s - m_new) * _EXP2_SCALE)
            l = alpha * l + jnp.sum(p, axis=-1, keepdims=True)
            acc = alpha * acc + lax.dot_general(p.astype(BF16), v_ref[rows, heads[a]], _NN,
                                                preferred_element_type=F32)
            return m_new, l, acc

        def step(ki, carry, mask):
            return tuple(update(a, ki, carry[a], mask) for a in range(hp))

        init = (jnp.full((t, 1), _NEG, F32), jnp.zeros((t, 1), F32), jnp.zeros((t, HEAD_SLOT), F32))
        key_ok = lax.broadcasted_iota(jnp.int32, (1, tk), 1) >= NPAD
        carry = lax.cond(qi > 0, lambda c: step(0, c, key_ok), lambda c: c, (init,) * hp)
        carry = lax.fori_loop(1, qi * (t // tk), lambda ki, c: step(ki, c, None), carry)
        qpos = lax.broadcasted_iota(jnp.int32, (t, tk), 0)
        for d in range(t // tk):
            kpos = d * tk + lax.broadcasted_iota(jnp.int32, (t, tk), 1)
            diag = (kpos <= qpos) & ((qi * t + kpos >= NPAD) | (kpos == qpos))
            carry = step(qi * (t // tk) + d, carry, diag)
        for a in range(hp):
            m, l, acc = carry[a]
            o_ref[:, heads[a]] = acc / l * _row_mask(qi, t)
            lse_ref[a] = m * _SCALE + jnp.log(l)

        if ride is not None:
            @pl.when((pl.program_id(0) == MLA_HEADS // hp - 1) & (qi == nb - 1))
            def _():
                rider.finish(*ride)

    qspec = pl.BlockSpec((t, wide), lambda g, i: (i, g))
    kspec = pl.BlockSpec((lp, wide), lambda g, i: (0, g))
    r_in, r_out, r_shapes, r_scratch = _rider_args(rider)
    return pl.pallas_call(
        body, name=name,
        out_shape=[jax.ShapeDtypeStruct((lp, MLA_WIDE), F32),
                   jax.ShapeDtypeStruct((MLA_HEADS, lp, 1), F32)] + r_shapes,
        grid=(MLA_HEADS // hp, nb),
        in_specs=[qspec, kspec, kspec] + r_in,
        out_specs=[qspec, pl.BlockSpec((hp, t, 1), lambda g, i: (g, i, 0))] + r_out,
        scratch_shapes=r_scratch,
        compiler_params=_cparams(dimension_semantics=("arbitrary", "arbitrary")),
    )(q, k, v, *(rider.operands if rider else ()))


def _attn_delta(do, o, name):
    lp = do.shape[0]
    t = _pick(lp, (384, 256, 128))

    def body(do_ref, o_ref, dob_ref, delta_ref):
        dob_ref[...] = do_ref[...].astype(BF16)
        for h in range(MLA_HEADS):
            sl = slice(h * HEAD_SLOT, (h + 1) * HEAD_SLOT)
            delta_ref[h] = jnp.sum(do_ref[:, sl] * o_ref[:, sl], axis=-1, keepdims=True)

    spec = pl.BlockSpec((t, MLA_WIDE), lambda i: (i, 0))
    return pl.pallas_call(
        body, name=name,
        out_shape=[jax.ShapeDtypeStruct((lp, MLA_WIDE), BF16), jax.ShapeDtypeStruct((MLA_HEADS, lp, 1), F32)],
        grid=(lp // t,), in_specs=[spec, spec],
        out_specs=[spec, pl.BlockSpec((MLA_HEADS, t, 1), lambda i: (0, i, 0))],
        compiler_params=_cparams(dimension_semantics=("parallel",)),
    )(do, o)


def _attn_bwd(q, k, v, do, lse, delta, name, rider=None):
    lp = q.shape[0]
    t = _pick(lp, (384, 256, 128))
    nb = lp // t
    ns = t // STRIP
    hp = HEADS_PER_STEP
    wide = hp * HEAD_SLOT
    heads = [slice(a * HEAD_SLOT, (a + 1) * HEAD_SLOT) for a in range(hp)]
    log2e = math.log2(math.e)

    def body(*refs):
        own, ride = _rider_split(rider, refs, 6, 3, 4)
        (q_ref, k_ref, v_ref, do_ref, lse_ref, delta_ref, dq_ref, dk_ref, dv_ref,
         s_scr, dp_scr, p_scr, ds_scr) = own
        kj = pl.program_id(1)
        if ride is not None:
            @pl.when((pl.program_id(0) == 0) & (kj == 0))
            def _():
                rider.start(*ride)

        @pl.when(kj == 0)
        def _():
            dq_ref[...] = jnp.zeros_like(dq_ref)

        dk_ref[...] = jnp.zeros_like(dk_ref)
        dv_ref[...] = jnp.zeros_like(dv_ref)

        def tile(qi, kind):
            rows = pl.ds(pl.multiple_of(qi * t, t), t)
            for a in range(hp):
                qb, dob = q_ref[rows, heads[a]], do_ref[rows, heads[a]]
                kb, vb = k_ref[:, heads[a]], v_ref[:, heads[a]]
                s_scr[a] = lax.dot_general(qb, kb, _NT, preferred_element_type=F32)
                dp_scr[a] = lax.dot_general(dob, vb, _NT, preferred_element_type=F32)
                lse2 = lse_ref[a, rows, :] * log2e
                delta = delta_ref[a, rows, :]
                for c in range(ns):
                    cs = slice(c * STRIP, (c + 1) * STRIP)
                    pc = jnp.exp2(s_scr[a, :, cs] * _EXP2_SCALE - lse2)
                    pc = jnp.where(_strip_mask(kind, kj, c, t), pc, 0.0)
                    p_scr[a, :, cs] = pc.astype(BF16)
                    ds_scr[a, :, cs] = (pc * (dp_scr[a, :, cs] - delta)).astype(BF16)
                dq_ref[rows, heads[a]] += lax.dot_general(ds_scr[a], kb, _NN,
                                                          preferred_element_type=F32) * _SCALE
                dv_ref[:, heads[a]] += lax.dot_general(p_scr[a], dob, _TN, preferred_element_type=F32)
                dk_ref[:, heads[a]] += lax.dot_general(ds_scr[a], qb, _TN, preferred_element_type=F32)

        tile(kj, 'diag')

        def below(qi, carry):
            tile(qi, 'keys')
            return carry

        lax.fori_loop(kj + 1, nb, below, 0)
        dk_ref[...] = dk_ref[...] * _SCALE

        if ride is not None:
            @pl.when((pl.program_id(0) == MLA_HEADS // hp - 1) & (kj == nb - 1))
            def _():
                rider.finish(*ride)

    whole = pl.BlockSpec((lp, wide), lambda g, j: (0, g))
    kspec = pl.BlockSpec((t, wide), lambda g, j: (j, g))
    stat = pl.BlockSpec((hp, lp, 1), lambda g, j: (g, 0, 0))
    r_in, r_out, r_shapes, r_scratch = _rider_args(rider)
    return pl.pallas_call(
        body, name=name,
        out_shape=[jax.ShapeDtypeStruct((lp, MLA_WIDE), F32)] * 3 + r_shapes,
        grid=(MLA_HEADS // hp, nb),
        in_specs=[whole, kspec, kspec, whole, stat, stat] + r_in,
        out_specs=[whole, kspec, kspec] + r_out,
        scratch_shapes=[pltpu.VMEM((hp, t, t), F32), pltpu.VMEM((hp, t, t), F32),
                        pltpu.VMEM((hp, t, t), BF16), pltpu.VMEM((hp, t, t), BF16)] + r_scratch,
        compiler_params=_cparams(dimension_semantics=("arbitrary", "arbitrary")),
    )(q, k, v, do, lse, delta, *(rider.operands if rider else ()))


def _rope_tables(lp):
    inv = 1.0 / (ROPE_THETA ** (jnp.arange(0, MLA_ROPE, 2, dtype=F32) / MLA_ROPE))
    pos = jnp.maximum(jnp.arange(lp, dtype=jnp.int32) - NPAD, 0).astype(F32)
    ang = pos[:, None] * inv[None, :]
    cos, sin = jnp.cos(ang), jnp.sin(ang)
    z32 = jnp.zeros((lp, HEAD_SLOT - MLA_QK), F32)
    cos_t = jnp.concatenate([jnp.ones((lp, MLA_NOPE), F32), cos, cos, z32], axis=1)
    sin_t = jnp.concatenate([jnp.zeros((lp, MLA_NOPE), F32), sin, sin, z32], axis=1)
    return cos_t, sin_t


def _loss_head(h, target, name):
    lp = h.shape[0]

    def body(h_ref, t_ref, d_ref, loss_ref):
        i = pl.program_id(0)

        @pl.when(i == 0)
        def _():
            d_ref[...] = jnp.zeros_like(d_ref)
            loss_ref[...] = jnp.zeros_like(loss_ref)

        @pl.when(i > 0)
        def _():
            err = h_ref[...] - t_ref[...]
            d_ref[...] = err * (1.0 / D_MODEL)
            loss_ref[...] += jnp.sum(err * err, axis=0, keepdims=True) * (0.5 / D_MODEL)

    return pl.pallas_call(
        body, name=name,
        out_shape=[jax.ShapeDtypeStruct((lp, D_MODEL), F32), jax.ShapeDtypeStruct((1, D_MODEL), F32)],
        grid=(lp // CHUNK,),
        in_specs=[pl.BlockSpec((CHUNK, D_MODEL), lambda i: (i, 0)),
                  pl.BlockSpec((CHUNK, D_MODEL), lambda i: (jnp.maximum(i - 1, 0), 0))],
        out_specs=[pl.BlockSpec((CHUNK, D_MODEL), lambda i: (i, 0)),
                   pl.BlockSpec((1, D_MODEL), lambda i: (0, 0))],
        compiler_params=_cparams(dimension_semantics=("arbitrary",)),
    )(h, target)


def _pad_cols(w, n):
    return jnp.pad(w, [(0, 0)] * (w.ndim - 1) + [(0, n - w.shape[-1])])


def _layer_slab(name, i):
    return i if name.startswith('mlp_') else i // 2


def _prep_layer(p, i, get):
    j = i // 2
    if i % 2 == 0:
        p['ssd_in'][j] = _pad_cols(get('ssd_w_in'), SSD_IN_PAD).astype(BF16)
        p['ssd_out'][j] = get('ssd_w_out').astype(BF16)
    else:
        wi = get('mla_w_in')
        kpe = jnp.pad(wi[:, MLA_Q_RANK + MLA_KV_RANK:], ((0, 0), (MLA_NOPE, HEAD_SLOT - MLA_QK)))
        p['mla_in'][j] = jnp.concatenate(
            [wi[:, MLA_Q_RANK:MLA_Q_RANK + MLA_KV_RANK], kpe, wi[:, :MLA_Q_RANK]], axis=1).astype(BF16)
        qb = get('mla_w_q_b').reshape(MLA_Q_RANK, MLA_HEADS, MLA_QK)
        p['mla_qb'][j] = _pad_cols(qb, HEAD_SLOT).reshape(MLA_Q_RANK, MLA_WIDE).astype(BF16)
        kvb = get('mla_w_kv_b').reshape(MLA_KV_RANK, MLA_HEADS, MLA_NOPE + MLA_V)
        kn = _pad_cols(kvb[:, :, :MLA_NOPE], HEAD_SLOT).reshape(MLA_KV_RANK, MLA_WIDE)
        vv = _pad_cols(kvb[:, :, MLA_NOPE:], HEAD_SLOT).reshape(MLA_KV_RANK, MLA_WIDE)
        p['mla_kvb'][j] = jnp.concatenate([kn, vv], axis=1).astype(BF16)
        wo = get('mla_w_out').reshape(MLA_HEADS, MLA_V, D_MODEL)
        p['mla_out'][j] = (jnp.pad(wo, ((0, 0), (0, HEAD_SLOT - MLA_V), (0, 0)))
                           .reshape(MLA_WIDE, D_MODEL).astype(BF16))
    p['up'][i] = get('mlp_w_up').astype(BF16)
    p['down'][i] = get('mlp_w_down').astype(BF16)


def _no_matrices():
    return {k: [None] * n for k, n in (('ssd_in', 2), ('ssd_out', 2), ('mla_in', 2), ('mla_qb', 2),
                                       ('mla_kvb', 2), ('mla_out', 2), ('up', 4), ('down', 4))}


class _ReadyWeights:
    def __init__(self, w):
        self.w, self.p = w, _no_matrices()

    def ensure(self, i):
        _prep_layer(self.p, i, lambda n: self.w[n][_layer_slab(n, i)])

    def rider(self, i):
        return None

    def deliver(self, i, outs):
        assert not outs


class _KeepGrads:
    def __init__(self):
        self.rounds = {}

    def begin(self, r, grads):
        self.rounds[r] = grads
        return None

    def finish(self, r, outs):
        assert not outs

    def result(self):
        names = {n for g in self.rounds.values() for n in g}
        return {n: jnp.concatenate([self.rounds[r][n] for r in sorted(self.rounds, reverse=True)
                                    if n in self.rounds[r]], axis=0) for n in names}


def _pad128(v):
    return _pad_cols(v.reshape(1, -1), 128)


def _sqrelu(u):
    r = jnp.maximum(u, 0.0)
    return r * r


def _local_step(x, target, w, big=None, red=None):
    seq = x.shape[0]
    lp = NPAD + N_META + seq
    big = _ReadyWeights(w) if big is None else big
    p = big.p
    h = jnp.concatenate([jnp.zeros((NPAD, D_MODEL), F32), w['meta_tokens'], x], axis=0)
    cos_t, sin_t = _rope_tables(lp)
    rt = _pick(lp, (384, 256, 128))
    saved = []
    for i in range(4):
        j = i // 2
        big.ensure(i)
        s = {'h0': h}
        g_mix = w['ln_mix'][i].reshape(1, -1)
        hn = _rms_fwd(h, g_mix, f"rms_mix_f{i}")
        s['hn'] = hn
        if i % 2 == 0:
            zxd = _mm(hn, p['ssd_in'][j], 'nn', name=f"ssd_in_f{i}")
            consts = _ssd_consts(w['ssd_conv_w'][j], w['ssd_conv_b'][j].reshape(1, -1),
                                 _pad128(w['ssd_dt_bias'][j]), _pad128(w['ssd_a_log'][j]),
                                 _pad128(w['ssd_d'][j]), w['ssd_norm'][j].reshape(1, -1))
            yg, states, *got = _ssd_fwd(zxd, consts, f"ssd_core_f{i}", rider=big.rider(i))
            big.deliver(i, got)
            s.update(zxd=zxd, consts=consts, yg=yg, states=states)
            h = _mm(yg, p['ssd_out'][j], 'nn', name=f"ssd_out_f{i}", epi=lambda r, hv: hv + r, extras=(h,))
        else:
            lat = _mm(hn, p['mla_in'][j], 'nn', name=f"mla_in_f{i}")
            kvg = w['mla_kv_a_norm'][j].reshape(1, -1)
            qag = w['mla_q_a_norm'][j].reshape(1, -1)
            kvn, qn = _row_call(lambda _, a, b, c, d: _lat_norm(a, b, c, d),
                                [(lat, MLA_KV_RANK, 0), (lat, MLA_Q_RANK, 1)], [kvg, qag],
                                [(MLA_KV_RANK, BF16), (MLA_Q_RANK, BF16)], n_rows=lp, tile=rt,
                                name=f"mla_latnorm_f{i}")
            q_raw = _mm(qn, p['mla_qb'][j], 'nn', name=f"mla_qb_f{i}")
            kv_raw = _mm(kvn, p['mla_kvb'][j], 'nn', name=f"mla_kvb_f{i}")
            qg = _pad_cols(w['mla_q_norm'][j].reshape(1, -1), HEAD_SLOT)
            kg = _pad_cols(w['mla_k_norm'][j].reshape(1, -1), HEAD_SLOT)

            def prep_fwd(_, qr, kn, kpe, vv, cs, sn, qgv, kgv):
                qq, kk = _qk_prep(qr, kn, kpe, cs, sn, qgv, kgv)
                return qq, kk, vv

            q, k, v = _row_call(prep_fwd,
                                [(q_raw, MLA_WIDE, 0), (kv_raw, MLA_WIDE, 0), (lat, HEAD_SLOT, 2),
                                 (kv_raw, MLA_WIDE, 1), (cos_t, HEAD_SLOT, 0), (sin_t, HEAD_SLOT, 0)],
                                [qg, kg], [(MLA_WIDE, BF16)] * 3, n_rows=lp, tile=rt,
                                name=f"mla_qkprep_f{i}")
            o, lse, *got = _attn_fwd(q, k, v, f"mla_attn_f{i}", rider=big.rider(i))
            big.deliver(i, got)
            s.update(lat=lat, kvg=kvg, qag=qag, kvn=kvn, qn=qn, q_raw=q_raw, kv_raw=kv_raw, qg=qg, kg=kg,
                     q=q, k=k, v=v, o=o, lse=lse)
            h = _mm(o, p['mla_out'][j], 'nn', name=f"mla_out_f{i}", epi=lambda r, hv: hv + r, extras=(h,))
        s['h1'] = h
        g_mlp = w['ln_mlp'][i].reshape(1, -1)
        hn2 = _rms_fwd(h, g_mlp, f"rms_mlp_f{i}")
        u = _mm(hn2, p['up'][i], 'nn', name=f"mlp_up_f{i}", out_dtype=BF16)
        h = _mm(u, p['down'][i], 'nn', name=f"mlp_down_f{i}", a_fn=_sqrelu,
                epi=lambda r, hv: hv + r, extras=(h,))
        s.update(hn2=hn2, u=u, g_mix=g_mix, g_mlp=g_mlp)
        saved.append(s)

    dh, loss_row = _loss_head(h, target, "loss_head")

    large = {n for n, _ in BIG}
    g = {k_: [None] * (4 if k_ in ('ln_mix', 'ln_mlp') else 2)
         for k_ in ALL_NAMES if k_ != 'meta_tokens' and k_ not in large}
    red = _KeepGrads() if red is None else red
    rounds, pending = {}, None

    def round_of(nm, i):
        return next(r for r, spec in enumerate(REDUCE_ROUNDS)
                    for n, l0, l1 in spec if n == nm and l0 <= _layer_slab(nm, i) < l1)

    def slabs_in(nm, r):
        return next((l0, l1) for n, l0, l1 in REDUCE_ROUNDS[r] if n == nm)

    def dw_into(nm, i, a, b, **kw):
        r = round_of(nm, i)
        (l0, l1), cur = slabs_in(nm, r), rounds.setdefault(r, {})
        cur[nm] = _mm(a, b, 'tn', stack=(l1 - l0, _layer_slab(nm, i) - l0, cur.get(nm)), **kw)

    def put(nm, i, arr):
        rounds.setdefault(round_of(nm, i), {})[nm] = arr[None]

    def hand_over(r):
        nonlocal pending
        pending = (r, red.begin(r, rounds.pop(r)))

    def host(fn, *args):
        nonlocal pending
        if pending is None or pending[1] is None:
            return fn(*args)
        (r, rider), pending = pending, None
        outs = fn(*args, rider=rider)
        own = len(outs) - len(rider.out_shapes)
        red.finish(r, outs[own:])
        return outs[:own]

    for i in reversed(range(4)):
        j = i // 2
        s = saved[i]
        dw_into('mlp_w_down', i, s['u'], dh, name=f"mlp_down_dw{i}", a_fn=_sqrelu)
        du = _mm(dh, p['down'][i], 'nt', name=f"mlp_down_dx{i}", out_dtype=BF16,
                 epi=lambda r, uv: r * (2.0 * jnp.maximum(uv, 0.0)), extras=(s['u'],))
        dw_into('mlp_w_up', i, s['hn2'], du, name=f"mlp_up_dw{i}")
        d_hn2 = _mm(du, p['up'][i], 'nt', name=f"mlp_up_dx{i}")
        dh, dg = _rms_bwd(s['h1'], s['g_mlp'], d_hn2, dh, f"rms_mlp_b{i}")
        g['ln_mlp'][i] = dg[0]
        if i == 0:
            hand_over(1)
        if i % 2 == 0:
            dw_into('ssd_w_out', i, s['yg'], dh, name=f"ssd_out_dw{i}")
            d_yg = _mm(dh, p['ssd_out'][j], 'nt', name=f"ssd_out_dx{i}")
            d_zxd, dcw, dcb, ddtb, dalog, ddsk, dng = host(_ssd_bwd, s['zxd'], s['states'], d_yg, s['consts'],
                                                           f"ssd_core_b{i}")
            g['ssd_conv_w'][j], g['ssd_conv_b'][j], g['ssd_norm'][j] = dcw, dcb[0], dng[0]
            g['ssd_dt_bias'][j], g['ssd_a_log'][j], g['ssd_d'][j] = (
                ddtb[0, :SSD_HEADS], dalog[0, :SSD_HEADS], ddsk[0, :SSD_HEADS])
            dw_into('ssd_w_in', i, s['hn'], d_zxd, name=f"ssd_in_dw{i}")
            d_hn = _mm(d_zxd, p['ssd_in'][j], 'nt', name=f"ssd_in_dx{i}")
        else:
            wo = _mm(s['o'], dh, 'tn', name=f"mla_out_dw{i}")
            put('mla_w_out', i, wo.reshape(MLA_HEADS, HEAD_SLOT, D_MODEL)[:, :MLA_V].reshape(-1, D_MODEL))
            do = _mm(dh, p['mla_out'][j], 'nt', name=f"mla_out_dx{i}")
            dob, delta = _attn_delta(do, s['o'], f"mla_attn_delta{i}")
            dq, dk, dv = host(_attn_bwd, s['q'], s['k'], s['v'], dob, s['lse'], delta, f"mla_attn_b{i}")

            def prep_bwd(_, qr, kn, kpe, cs, sn, dqv, dkv, dvv, qgv, kgv):
                _, vjp = jax.vjp(lambda a, b, c, d, e: _qk_prep(a, b, c, cs, sn, d, e), qr, kn, kpe, qgv, kgv)
                d_qr, d_kn, d_kpe, d_qg, d_kg = vjp((dqv, dkv))
                return d_qr, jnp.concatenate([d_kn, dvv], axis=1), d_kpe, d_qg, d_kg

            d_qraw, d_kvraw, d_kpe, d_qg, d_kg = _row_call(
                prep_bwd,
                [(s['q_raw'], MLA_WIDE, 0), (s['kv_raw'], MLA_WIDE, 0), (s['lat'], HEAD_SLOT, 2),
                 (cos_t, HEAD_SLOT, 0), (sin_t, HEAD_SLOT, 0), (dq, MLA_WIDE, 0), (dk, MLA_WIDE, 0),
                 (dv, MLA_WIDE, 0)],
                [s['qg'], s['kg']], [(MLA_WIDE, BF16), (2 * MLA_WIDE, BF16), (HEAD_SLOT, F32)],
                [(1, HEAD_SLOT), (1, HEAD_SLOT)], n_rows=lp, tile=_pick(lp, (128,)), name=f"mla_qkprep_b{i}")
            g['mla_q_norm'][j], g['mla_k_norm'][j] = d_qg[0, :MLA_QK], d_kg[0, :MLA_QK]
            wqb = _mm(s['qn'], d_qraw, 'tn', name=f"mla_qb_dw{i}")
            put('mla_w_q_b', i, wqb.reshape(MLA_Q_RANK, MLA_HEADS, HEAD_SLOT)[:, :, :MLA_QK].reshape(MLA_Q_RANK, -1))
            d_qn = _mm(d_qraw, p['mla_qb'][j], 'nt', name=f"mla_qb_dx{i}")
            wkvb = _mm(s['kvn'], d_kvraw, 'tn', name=f"mla_kvb_dw{i}").reshape(MLA_KV_RANK, 2, MLA_HEADS, HEAD_SLOT)
            put('mla_w_kv_b', i, jnp.concatenate([wkvb[:, 0, :, :MLA_NOPE], wkvb[:, 1, :, :MLA_V]],
                                                 axis=-1).reshape(MLA_KV_RANK, -1))
            d_kvn = _mm(d_kvraw, p['mla_kvb'][j], 'nt', name=f"mla_kvb_dx{i}")

            def lat_bwd(_, kvl, ql, dkvn, dqn, dkpe, kvgv, qagv):
                _, vjp = jax.vjp(_lat_norm, kvl, ql, kvgv, qagv)
                d_kvl, d_ql, d_kvg, d_qag = vjp((dkvn, dqn))
                return jnp.concatenate([d_kvl, dkpe, d_ql], axis=1), d_kvg, d_qag

            d_lat, d_kvg, d_qag = _row_call(
                lat_bwd, [(s['lat'], MLA_KV_RANK, 0), (s['lat'], MLA_Q_RANK, 1), (d_kvn, MLA_KV_RANK, 0),
                          (d_qn, MLA_Q_RANK, 0), (d_kpe, HEAD_SLOT, 0)],
                [s['kvg'], s['qag']], [(LAT_PAD, BF16)], [(1, MLA_KV_RANK), (1, MLA_Q_RANK)],
                n_rows=lp, tile=rt, name=f"mla_latnorm_b{i}")
            g['mla_kv_a_norm'][j], g['mla_q_a_norm'][j] = d_kvg[0], d_qag[0]
            win = _mm(s['hn'], d_lat, 'tn', name=f"mla_in_dw{i}")
            put('mla_w_in', i, jnp.concatenate(
                [win[:, MLA_KV_RANK + HEAD_SLOT:], win[:, :MLA_KV_RANK],
                 win[:, MLA_KV_RANK + MLA_NOPE:MLA_KV_RANK + MLA_QK]], axis=1))
            d_hn = _mm(d_lat, p['mla_in'][j], 'nt', name=f"mla_in_dx{i}")
        dh, dg = _rms_bwd(s['h0'], s['g_mix'], d_hn, dh, f"rms_mix_b{i}")
        g['ln_mix'][i] = dg[0]
        if i == 2:
            hand_over(0)
    hand_over(2)

    if pending[1] is not None:
        red.finish(pending[0], _run_rider(pending[1], "rs_exchange_last"))
    grads = {k_: jnp.stack(v_) for k_, v_ in g.items()}
    grads['meta_tokens'] = dh[NPAD:NPAD + N_META]
    return loss_row, dh[NPAD + N_META:], grads, red


def _all_gather8(shard, name):
    m_per, n = shard.shape

    def body(x_ref, out_ref, send_sems, recv_sems, local_sem):
        x, y, c = lax.axis_index("x"), lax.axis_index("y"), lax.axis_index("c")
        me, sibling = (x, y, c), (x, y, 1 - c)
        chips = [(1 - x, y), (x, 1 - y), (1 - x, 1 - y)]

        def rows(px, py, pc):
            return out_ref.at[pl.ds((4 * px + 2 * py + pc) * m_per, m_per), :]

        def copy(k, block, to, src=None):
            return pltpu.make_async_remote_copy(
                src_ref=rows(*block) if src is None else src, dst_ref=rows(*block),
                send_sem=send_sems.at[k], recv_sem=recv_sems.at[k], device_id=to, device_id_type=MESH)

        mine = pltpu.make_async_copy(x_ref, rows(*me), local_sem)
        mine.start()
        first = [copy(0, me, sibling, src=x_ref)]
        first += [copy(1 + j, me, (*chip, c), src=x_ref) for j, chip in enumerate(chips)]
        for cp in first:
            cp.start()
        passed = [copy(4 + j, (*chip, c), sibling) for j, chip in enumerate(chips)]
        for j, chip in enumerate(chips):
            copy(1 + j, (*chip, c), me).wait_recv()
            passed[j].start()
        copy(0, sibling, me).wait_recv()
        for j, chip in enumerate(chips):
            copy(4 + j, (*chip, 1 - c), me).wait_recv()
        for cp in first + passed:
            cp.wait_send()
        mine.wait()

    return pl.pallas_call(
        body, name=name,
        out_shape=jax.ShapeDtypeStruct((8 * m_per, n), shard.dtype),
        in_specs=[pl.BlockSpec(memory_space=pl.ANY)],
        out_specs=pl.BlockSpec(memory_space=pl.ANY),
        scratch_shapes=[pltpu.SemaphoreType.DMA((7,)), pltpu.SemaphoreType.DMA((7,)), pltpu.SemaphoreType.DMA],
    )(shard)


def _mesh_pos():
    return lax.axis_index("x"), lax.axis_index("y"), lax.axis_index("c")


def _half_rows(pc, h):
    return pl.ds(pl.multiple_of(pc * h, 16), h)


def _whole_view(ref, kind, shard_shape, k, pc):
    _, r, c = shard_shape
    rows = _half_rows(pc, r // 2)
    if kind == 'row':
        return ref.at[:, k, rows, :]
    if kind == 'col':
        return ref.at[:, rows, pl.ds(pl.multiple_of(k * c, 128), c)]
    return ref.at[k, :, rows, :]


def _whole_shape(kind, shard_shape, rows=None):
    l, r, c = shard_shape
    r = r if rows is None else rows
    return {'row': (l, 4, r, c), 'col': (l, r, 4 * c), 'colx': (4, l, r, c)}[kind]


def _gather_rider(shards, kinds):
    n = len(shards)
    shapes = [s.shape for s in shards]

    def plan(ins, outs, sems):
        send_sems, recv_sems, local_sems = sems
        x, y, c = _mesh_pos()
        me, sibling = (x, y, c), (x, y, 1 - c)
        chips = [(1 - x, y), (x, 1 - y), (1 - x, 1 - y)]

        def place(a, px, py, pc):
            return _whole_view(outs[a], kinds[a], shapes[a], 2 * px + py, pc)

        def own(a):
            return ins[a].at[:, _half_rows(c, shapes[a][1] // 2), :]

        def copy(a, k, block, to, src=None):
            return pltpu.make_async_remote_copy(
                src_ref=place(a, *block) if src is None else src, dst_ref=place(a, *block),
                send_sem=send_sems.at[7 * a + k], recv_sem=recv_sems.at[7 * a + k],
                device_id=to, device_id_type=MESH)

        mine = [pltpu.make_async_copy(own(a), place(a, *me), local_sems.at[a]) for a in range(n)]
        first = [copy(a, 1 + j, me, (*chip, c), src=own(a)) for j, chip in enumerate(chips) for a in range(n)]
        first += [copy(a, 0, me, sibling, src=own(a)) for a in range(n)]
        return copy, mine, first, chips, me, sibling, c

    def start(ins, outs, sems):
        _, mine, first, *_ = plan(ins, outs, sems)
        for cp in first + mine:
            cp.start()

    def finish(ins, outs, sems):
        copy, mine, first, chips, me, sibling, c = plan(ins, outs, sems)
        passed = []
        for j, chip in enumerate(chips):
            for a in range(n):
                copy(a, 1 + j, (*chip, c), me).wait_recv()
                passed.append(copy(a, 4 + j, (*chip, c), sibling))
                passed[-1].start()
        for a in range(n):
            copy(a, 0, sibling, me).wait_recv()
        for j, chip in enumerate(chips):
            for a in range(n):
                copy(a, 4 + j, (*chip, 1 - c), me).wait_recv()
        for cp in first + passed:
            cp.wait_send()
        for cp in mine:
            cp.wait()

    return _Rider(
        shards, [jax.ShapeDtypeStruct(_whole_shape(k, s.shape), s.dtype) for k, s in zip(kinds, shards)],
        [pltpu.SemaphoreType.DMA((7 * n,)), pltpu.SemaphoreType.DMA((7 * n,)), pltpu.SemaphoreType.DMA((n,))],
        start, finish)


def _run_rider(rider, name):
    ni, no = len(rider.operands), len(rider.out_shapes)

    def body(*refs):
        ride = (refs[:ni], refs[ni:ni + no], refs[ni + no:])
        rider.start(*ride)
        rider.finish(*ride)

    return pl.pallas_call(
        body, name=name, out_shape=rider.out_shapes,
        in_specs=[pl.BlockSpec(memory_space=pl.ANY)] * ni,
        out_specs=[pl.BlockSpec(memory_space=pl.ANY)] * no,
        scratch_shapes=rider.scratch,
    )(*rider.operands)


def _rs_swap(wholes, kinds, shapes, name):
    n = len(wholes)

    def body(*refs):
        ins, outs = refs[:n], refs[n:2 * n]
        send_sems, recv_sems = refs[2 * n:]
        x, y, c = _mesh_pos()
        cps = []
        for a in range(n):
            rows = _half_rows(1 - c, shapes[a][1] // 2)
            src = ins[a].at[:, rows, :] if kinds[a] == 'col' else ins[a].at[:, :, rows, :]
            cps.append(pltpu.make_async_remote_copy(
                src_ref=src, dst_ref=outs[a], send_sem=send_sems.at[a], recv_sem=recv_sems.at[a],
                device_id=(x, y, 1 - c), device_id_type=MESH))
        for cp in cps:
            cp.start()
        for cp in cps:
            cp.wait()

    return pl.pallas_call(
        body, name=name,
        out_shape=[jax.ShapeDtypeStruct(_whole_shape(k, s, s[1] // 2), w.dtype)
                   for k, s, w in zip(kinds, shapes, wholes)],
        in_specs=[pl.BlockSpec(memory_space=pl.ANY)] * n,
        out_specs=[pl.BlockSpec(memory_space=pl.ANY)] * n,
        scratch_shapes=[pltpu.SemaphoreType.DMA((n,)), pltpu.SemaphoreType.DMA((n,))],
    )(*wholes)


def _exchange_rider(parts, kinds, shapes):
    n = len(parts)

    def plan(ins, outs, sems):
        send_sems, recv_sems, local_sems = sems
        x, y, c = _mesh_pos()
        kme = 2 * x + y
        chips = [(1 - x, y), (x, 1 - y), (1 - x, 1 - y)]

        def slab(a, k):
            if kinds[a] == 'row':
                return ins[a].at[:, k]
            if kinds[a] == 'col':
                cw = shapes[a][2]
                return ins[a].at[:, :, pl.ds(pl.multiple_of(k * cw, 128), cw)]
            return ins[a].at[k]

        cps = [pltpu.make_async_remote_copy(
            src_ref=slab(a, 2 * px + py), dst_ref=outs[a].at[kme], send_sem=send_sems.at[3 * a + j],
            recv_sem=recv_sems.at[3 * a + j], device_id=(px, py, c), device_id_type=MESH)
            for j, (px, py) in enumerate(chips) for a in range(n)]
        return cps + [pltpu.make_async_copy(slab(a, kme), outs[a].at[kme], local_sems.at[a]) for a in range(n)]

    def start(ins, outs, sems):
        for cp in plan(ins, outs, sems):
            cp.start()

    def finish(ins, outs, sems):
        for cp in plan(ins, outs, sems):
            cp.wait()

    return _Rider(
        parts, [jax.ShapeDtypeStruct((4, s[0], s[1] // 2, s[2]), p.dtype) for s, p in zip(shapes, parts)],
        [pltpu.SemaphoreType.DMA((3 * n,)), pltpu.SemaphoreType.DMA((3 * n,)), pltpu.SemaphoreType.DMA((n,))],
        start, finish)


def _rs_share(shards, slabs, name):
    n = len(shards)

    def body(*refs):
        outs = refs[n:2 * n]
        send_sems, recv_sems = refs[2 * n:]
        x, y, c = _mesh_pos()
        cps = []
        for a in range(n):
            l0, l1 = slabs[a]
            rows = outs[a].at[pl.ds(l0, l1 - l0), _half_rows(c, shards[a].shape[1] // 2), :]
            cps.append(pltpu.make_async_remote_copy(
                src_ref=rows, dst_ref=rows, send_sem=send_sems.at[a], recv_sem=recv_sems.at[a],
                device_id=(x, y, 1 - c), device_id_type=MESH))
        for cp in cps:
            cp.start()
        for cp in cps:
            cp.wait()

    return pl.pallas_call(
        body, name=name,
        out_shape=[jax.ShapeDtypeStruct(s.shape, s.dtype) for s in shards],
        in_specs=[pl.BlockSpec(memory_space=pl.ANY)] * n,
        out_specs=[pl.BlockSpec(memory_space=pl.ANY)] * n,
        input_output_aliases={a: a for a in range(n)},
        scratch_shapes=[pltpu.SemaphoreType.DMA((n,)), pltpu.SemaphoreType.DMA((n,))],
    )(*shards)


def _tile_rows(rows, cols, budget=2 * 1024 * 1024):
    for t in (1024, 512, 256, 128, 64, 32, 16, 8):
        if rows % t == 0 and t * cols * 4 <= budget:
            return t
    return rows


def _add_half(g3, r3, c_idx, name):
    a, h, n = r3.shape
    t = _tile_rows(h, n)
    nt = h // t

    def body(c_ref, g_ref, r_ref, o_ref):
        o_ref[...] = (g_ref[...] + r_ref[...]).astype(o_ref.dtype)

    return pl.pallas_call(
        body, name=name, out_shape=jax.ShapeDtypeStruct((a, h, n), BF16),
        grid_spec=pltpu.PrefetchScalarGridSpec(
            num_scalar_prefetch=1, grid=(a, nt),
            in_specs=[pl.BlockSpec((1, t, n), lambda k, i, c: (k, c[0] * nt + i, 0)),
                      pl.BlockSpec((1, t, n), lambda k, i, c: (k, i, 0))],
            out_specs=pl.BlockSpec((1, t, n), lambda k, i, c: (k, i, 0))),
        compiler_params=_cparams(dimension_semantics=("parallel", "parallel")),
    )(c_idx, g3, r3)


def _sum4(parts, c_idx, name, into):
    _, l, h, n = parts.shape
    n_slabs, l0, buf = into
    t = _tile_rows(h, n, 1024 * 1024)
    nt = h // t
    held = () if buf is None else (buf,)

    def body(c_ref, p_ref, *rest):
        pv = p_ref[...].astype(F32)
        rest[-1][...] = ((pv[0] + pv[1]) + pv[2]) + pv[3]

    return pl.pallas_call(
        body, name=name, out_shape=jax.ShapeDtypeStruct((n_slabs, 2 * h, n), F32),
        grid_spec=pltpu.PrefetchScalarGridSpec(
            num_scalar_prefetch=1, grid=(l, nt),
            in_specs=[pl.BlockSpec((4, 1, t, n), lambda k, i, c: (0, k, i, 0))]
            + [pl.BlockSpec(memory_space=pl.ANY)] * len(held),
            out_specs=pl.BlockSpec((1, t, n), lambda k, i, c: (l0 + k, c[0] * nt + i, 0))),
        input_output_aliases={2: 0} if held else {},
        compiler_params=_cparams(dimension_semantics=("parallel", "parallel")),
    )(c_idx, parts, *held)


def _sum8(parts, name):
    _, m, n = parts.shape

    def body(p_ref, o_ref):
        acc = p_ref[0]
        for d in range(1, 8):
            acc = acc + p_ref[d]
        o_ref[...] = acc

    return pl.pallas_call(body, name=name, out_shape=jax.ShapeDtypeStruct((m, n), F32))(parts)


def _adamw(wp, gp, mp, vp, name):
    r, n = wp.shape
    t = _tile_rows(r, n, 1024 * 1024)

    def body(w_ref, g_ref, m_ref, v_ref, d_ref, mo_ref, vo_ref):
        gv = g_ref[...]
        m2 = ADAM_B1 * m_ref[...] + (1.0 - ADAM_B1) * gv
        v2 = ADAM_B2 * v_ref[...] + (1.0 - ADAM_B2) * (gv * gv)
        m_hat = m2 / (1.0 - ADAM_B1 ** ADAM_STEP)
        v_hat = v2 / (1.0 - ADAM_B2 ** ADAM_STEP)
        d_ref[...] = -ADAM_LR * (m_hat / (jnp.sqrt(v_hat) + ADAM_EPS) + ADAM_WD * w_ref[...])
        mo_ref[...] = m2
        vo_ref[...] = v2

    spec = pl.BlockSpec((t, n), lambda i: (i, 0))
    return pl.pallas_call(
        body, name=name, out_shape=[jax.ShapeDtypeStruct((r, n), F32)] * 3, grid=(r // t,),
        in_specs=[spec] * 4, out_specs=[spec] * 3,
        compiler_params=_cparams(dimension_semantics=("parallel",)),
    )(wp, gp, mp, vp)


BIG = (('ssd_w_in', 'colx'), ('ssd_w_out', 'row'), ('mla_w_in', 'row'), ('mla_w_q_b', 'col'),
       ('mla_w_kv_b', 'col'), ('mla_w_out', 'row'), ('mlp_w_up', 'col'), ('mlp_w_down', 'row'))
SMALL_SHARDED = (('meta_tokens', 1), ('ssd_conv_w', 2), ('mla_q_a_norm', 1), ('mla_kv_a_norm', 1))
SMALL_REPL = ('ln_mix', 'ln_mlp', 'ssd_conv_b', 'ssd_dt_bias', 'ssd_a_log', 'ssd_d', 'ssd_norm',
              'mla_q_norm', 'mla_k_norm')
ALL_NAMES = ('meta_tokens', 'ln_mix', 'ln_mlp', 'ssd_w_in', 'ssd_conv_w', 'ssd_conv_b', 'ssd_dt_bias',
             'ssd_a_log', 'ssd_d', 'ssd_norm', 'ssd_w_out', 'mla_w_in', 'mla_q_a_norm', 'mla_w_q_b',
             'mla_kv_a_norm', 'mla_w_kv_b', 'mla_q_norm', 'mla_k_norm', 'mla_w_out', 'mlp_w_up', 'mlp_w_down')


_MLA_BIG = ('mla_w_in', 'mla_w_q_b', 'mla_w_kv_b', 'mla_w_out')
GATHER_ROUNDS = (
    (('ssd_w_in', 0, 1), ('ssd_w_out', 0, 1), ('mlp_w_up', 0, 1), ('mlp_w_down', 0, 1)),
    tuple((n, 0, 1) for n in _MLA_BIG) + (('mlp_w_up', 1, 2), ('mlp_w_down', 1, 2)),
    (('ssd_w_in', 1, 2), ('ssd_w_out', 1, 2)) + tuple((n, 1, 2) for n in _MLA_BIG)
    + (('mlp_w_up', 2, 4), ('mlp_w_down', 2, 4)),
)


REDUCE_ROUNDS = (
    GATHER_ROUNDS[2],
    tuple((n, 0, 1) for n in _MLA_BIG) + (('mlp_w_up', 0, 2), ('mlp_w_down', 0, 2)),
    (('ssd_w_in', 0, 1), ('ssd_w_out', 0, 1)),
)


class _GatheredWeights:
    def __init__(self, shards):
        self.shards, self.p, self.whole = shards, _no_matrices(), {}

    def _round(self, r):
        spec = GATHER_ROUNDS[r]
        return _gather_rider([self.shards[n][l0:l1] for n, l0, l1 in spec], [dict(BIG)[n] for n, _, _ in spec])

    def _take(self, r, outs):
        for (n, l0, l1), o in zip(GATHER_ROUNDS[r], outs):
            kind = dict(BIG)[n]
            for l in range(l0, l1):
                if kind == 'row':
                    m = o[l - l0].reshape(-1, o.shape[-1])
                elif kind == 'col':
                    m = o[l - l0]
                else:
                    m = jnp.concatenate([o[k, l - l0] for k in range(4)], axis=-1)
                self.whole[(n, l)] = m

    def ensure(self, i):
        if i == 0:
            self._take(0, _run_rider(self._round(0), "gather_first"))
        _prep_layer(self.p, i, lambda n: self.whole[(n, _layer_slab(n, i))])

    def rider(self, i):
        return self._round(i + 1) if i + 1 < len(GATHER_ROUNDS) else None

    def deliver(self, i, outs):
        if i + 1 < len(GATHER_ROUNDS):
            self._take(i + 1, outs)


class _ScatterGrads:
    def __init__(self, shard_shapes, c_idx):
        self.shard_shapes, self.c_idx, self.out = shard_shapes, c_idx, {}

    def begin(self, r, grads):
        spec = REDUCE_ROUNDS[r]
        kinds = [dict(BIG)[n] for n, _, _ in spec]
        shapes = [(l1 - l0,) + tuple(self.shard_shapes[n][1:]) for n, l0, l1 in spec]
        wholes = []
        for (n, _, _), kind, s in zip(spec, kinds, shapes):
            if kind == 'row':
                wholes.append(grads[n].reshape(s[0], 4, s[1], s[2]))
            elif kind == 'col':
                wholes.append(grads[n])
            else:
                wholes.append(jnp.stack([grads[n][..., k * s[2]:(k + 1) * s[2]] for k in range(4)]))
        recv = _rs_swap(wholes, kinds, shapes, f"rs_swap{r}")
        parts = []
        for (n, _, _), kind, s, gw, rc in zip(spec, kinds, shapes, wholes, recv):
            if kind == 'col':
                g3, r3 = gw, rc
            else:
                g3, r3 = gw.reshape(-1, s[1], s[2]), rc.reshape(-1, s[1] // 2, s[2])
            parts.append(_add_half(g3, r3, self.c_idx, f"rs_add{r}_{n}").reshape(rc.shape))
        return _exchange_rider(parts, kinds, shapes)

    def finish(self, r, outs):
        spec = REDUCE_ROUNDS[r]
        for (n, l0, _), part in zip(spec, outs):
            self.out[n] = _sum4(part, self.c_idx, f"rs_sum{r}_{n}",
                                into=(self.shard_shapes[n][0], l0, self.out.get(n)))
        shared = _rs_share([self.out[n] for n, _, _ in spec], [(l0, l1) for _, l0, l1 in spec], f"rs_share{r}")
        self.out.update(zip([n for n, _, _ in spec], shared))


def _pack(arrs, rows_mult):
    flat = jnp.concatenate([a.reshape(-1) for a in arrs])
    per = LANES * rows_mult
    pad = (-flat.shape[0]) % per
    if pad:
        flat = jnp.concatenate([flat, jnp.zeros((pad,), flat.dtype)])
    return flat.reshape(-1, LANES)


def _unpack(pack, shapes):
    flat = pack.reshape(-1)
    out, off = [], 0
    for shp in shapes:
        n = math.prod(shp)
        out.append(flat[off:off + n].reshape(shp))
        off += n
    return out


def _split4(full, axis):
    shp = full.shape
    r = full.reshape(shp[:axis] + (4, shp[axis] // 4) + shp[axis + 1:])
    return jnp.moveaxis(r, axis, 0)


def _join4(parts, axis):
    r = jnp.moveaxis(parts, 0, axis)
    shp = r.shape
    return r.reshape(shp[:axis] + (shp[axis] * shp[axis + 1],) + shp[axis + 2:])


def _gather_params(shards, table, dtype, c, name):
    pack = _pack([shards[n].astype(dtype) for n, _ in table], 16)
    half = pack.shape[0] // 2
    mine = lax.dynamic_slice_in_dim(pack, c * half, half, axis=0)
    full = _all_gather8(mine, name).reshape(4, -1)
    out, off = {}, 0
    for n, ax in table:
        cnt = math.prod(shards[n].shape)
        out[n] = _join4(full[:, off:off + cnt].reshape((4,) + shards[n].shape), ax)
        off += cnt
    return out


def kernel(x, meta_tokens, ln_mix, ln_mlp, ssd_w_in, ssd_conv_w, ssd_conv_b, ssd_dt_bias, ssd_a_log, ssd_d, ssd_norm, ssd_w_out, mla_w_in, mla_q_a_norm, mla_w_q_b, mla_kv_a_norm, mla_w_kv_b, mla_q_norm, mla_k_norm, mla_w_out, mlp_w_up, mlp_w_down, loss_target, m_meta_tokens, m_ln_mix, m_ln_mlp, m_ssd_w_in, m_ssd_conv_w, m_ssd_conv_b, m_ssd_dt_bias, m_ssd_a_log, m_ssd_d, m_ssd_norm, m_ssd_w_out, m_mla_w_in, m_mla_q_a_norm, m_mla_w_q_b, m_mla_kv_a_norm, m_mla_w_kv_b, m_mla_q_norm, m_mla_k_norm, m_mla_w_out, m_mlp_w_up, m_mlp_w_down, v_meta_tokens, v_ln_mix, v_ln_mlp, v_ssd_w_in, v_ssd_conv_w, v_ssd_conv_b, v_ssd_dt_bias, v_ssd_a_log, v_ssd_d, v_ssd_norm, v_ssd_w_out, v_mla_w_in, v_mla_q_a_norm, v_mla_w_q_b, v_mla_kv_a_norm, v_mla_w_kv_b, v_mla_q_norm, v_mla_k_norm, v_mla_w_out, v_mlp_w_up, v_mlp_w_down):
    w_sh = dict(meta_tokens=meta_tokens, ln_mix=ln_mix, ln_mlp=ln_mlp, ssd_w_in=ssd_w_in, ssd_conv_w=ssd_conv_w, ssd_conv_b=ssd_conv_b, ssd_dt_bias=ssd_dt_bias, ssd_a_log=ssd_a_log, ssd_d=ssd_d, ssd_norm=ssd_norm, ssd_w_out=ssd_w_out, mla_w_in=mla_w_in, mla_q_a_norm=mla_q_a_norm, mla_w_q_b=mla_w_q_b, mla_kv_a_norm=mla_kv_a_norm, mla_w_kv_b=mla_w_kv_b, mla_q_norm=mla_q_norm, mla_k_norm=mla_k_norm, mla_w_out=mla_w_out, mlp_w_up=mlp_w_up, mlp_w_down=mlp_w_down)
    m_sh = dict(meta_tokens=m_meta_tokens, ln_mix=m_ln_mix, ln_mlp=m_ln_mlp, ssd_w_in=m_ssd_w_in, ssd_conv_w=m_ssd_conv_w, ssd_conv_b=m_ssd_conv_b, ssd_dt_bias=m_ssd_dt_bias, ssd_a_log=m_ssd_a_log, ssd_d=m_ssd_d, ssd_norm=m_ssd_norm, ssd_w_out=m_ssd_w_out, mla_w_in=m_mla_w_in, mla_q_a_norm=m_mla_q_a_norm, mla_w_q_b=m_mla_w_q_b, mla_kv_a_norm=m_mla_kv_a_norm, mla_w_kv_b=m_mla_w_kv_b, mla_q_norm=m_mla_q_norm, mla_k_norm=m_mla_k_norm, mla_w_out=m_mla_w_out, mlp_w_up=m_mlp_w_up, mlp_w_down=m_mlp_w_down)
    v_sh = dict(meta_tokens=v_meta_tokens, ln_mix=v_ln_mix, ln_mlp=v_ln_mlp, ssd_w_in=v_ssd_w_in, ssd_conv_w=v_ssd_conv_w, ssd_conv_b=v_ssd_conv_b, ssd_dt_bias=v_ssd_dt_bias, ssd_a_log=v_ssd_a_log, ssd_d=v_ssd_d, ssd_norm=v_ssd_norm, ssd_w_out=v_ssd_w_out, mla_w_in=v_mla_w_in, mla_q_a_norm=v_mla_q_a_norm, mla_w_q_b=v_mla_w_q_b, mla_kv_a_norm=v_mla_kv_a_norm, mla_w_kv_b=v_mla_w_kv_b, mla_q_norm=v_mla_q_norm, mla_k_norm=v_mla_k_norm, mla_w_out=v_mla_w_out, mlp_w_up=v_mlp_w_up, mlp_w_down=v_mlp_w_down)

    cx, cy, cc = lax.axis_index("x"), lax.axis_index("y"), lax.axis_index("c")
    chip = 2 * cx + cy

    c_idx = cc.reshape(1).astype(jnp.int32)
    big_names = [n for n, _ in BIG]
    shapes = [w_sh[n].shape for n in big_names]

    w = {n: w_sh[n] for n in SMALL_REPL}
    w.update(_gather_params(w_sh, SMALL_SHARDED, F32, cc, "gather_small"))
    big = _GatheredWeights({n: w_sh[n].astype(BF16) for n in big_names})
    red = _ScatterGrads({n: w_sh[n].shape for n in big_names}, c_idx)

    loss_row, grad_x, grads, red = _local_step(x[0], loss_target[0], w, big, red)
    loss = lax.psum(jnp.sum(loss_row), ("x", "y", "c"))
    g_sh = dict(red.out)

    small_names = tuple(n for n, _ in SMALL_SHARDED) + SMALL_REPL
    sp = _pack([grads[n] for n in small_names], 8)
    srows = sp.shape[0]
    s_all = _sum8(_all_gather8(sp, "ar_small_gather").reshape(8, srows, LANES), "ar_small_sum")
    s_full = dict(zip(small_names, _unpack(s_all, [grads[n].shape for n in small_names])))
    for n, ax in SMALL_SHARDED:
        g_sh[n] = lax.dynamic_index_in_dim(_split4(s_full[n], ax), chip, axis=0, keepdims=False)
    for n in SMALL_REPL:
        g_sh[n] = s_full[n]

    delta, new_m, new_v = {}, {}, {}
    for n, s in zip(big_names, shapes):
        res = _adamw(*[t[n].reshape(-1, s[2]) for t in (w_sh, g_sh, m_sh, v_sh)], f"adamw_{n}")
        delta[n], new_m[n], new_v[n] = [r.reshape(s) for r in res]
    d_s, m_s, v_s = _adamw(*[_pack([t[n] for n in small_names], 8) for t in (w_sh, g_sh, m_sh, v_sh)],
                           "adamw_small")
    for dst, ps in ((delta, d_s), (new_m, m_s), (new_v, v_s)):
        dst.update(zip(small_names, _unpack(ps, [w_sh[n].shape for n in small_names])))

    return (loss, grad_x[None], *[g_sh[n] for n in ALL_NAMES], *[delta[n] for n in ALL_NAMES],
            *[new_m[n] for n in ALL_NAMES], *[new_v[n] for n in ALL_NAMES])
```

```python
import functools
import math

import jax
import jax.numpy as jnp
from jax import lax
from jax.experimental import pallas as pl
from jax.experimental.pallas import tpu as pltpu

F32 = jnp.float32
BF16 = jnp.bfloat16
MESH = pl.DeviceIdType.MESH
_NN = (((1,), (0,)), ((), ()))
_NT = (((1,), (1,)), ((), ()))
_TN = (((0,), (0,)), ((), ()))

D_MODEL = 1024
N_META = 16
EPS = 1e-6
SSD_D_INNER = 2048
SSD_HEADS = 32
SSD_HEAD_DIM = 64
SSD_GROUPS = 8
SSD_HPG = 4
SSD_STATE = 128
SSD_CONV = 4
CHUNK = 128
SSD_IN_DIM = 6176
SSD_IN_PAD = 6272
MLA_HEADS = 16
MLA_NOPE = 64
MLA_ROPE = 32
MLA_V = 64
MLA_QK = 96
MLA_Q_RANK = 384
MLA_KV_RANK = 256
HEAD_SLOT = 128
MLA_WIDE = MLA_HEADS * HEAD_SLOT
HEADS_PER_STEP = 2
LAT_PAD = 768
ROPE_THETA = 10000.0
D_FF = 4096
NPAD = CHUNK - N_META
ADAM_LR, ADAM_B1, ADAM_B2, ADAM_EPS, ADAM_WD, ADAM_STEP = 0.001, 0.9, 0.999, 1e-08, 0.01, 10
LANES = 1024
VMEM_LIMIT = 56 * 1024 * 1024


def _pick(n, cands):
    for c in cands:
        if n % c == 0:
            return c
    return n


def _cparams(**kw):
    return pltpu.CompilerParams(vmem_limit_bytes=VMEM_LIMIT, **kw)


def _mm(a, b, dims, *, name, out_dtype=F32, a_fn=None, epi=None, extras=(), stack=None, norm_gain=None):
    if dims == 'nn':
        (M, K), (K2, N) = a.shape, b.shape
    elif dims == 'nt':
        (M, K), (N, K2) = a.shape, b.shape
    else:
        (K, M), (K2, N) = a.shape, b.shape
    assert K == K2, (a.shape, b.shape, dims)
    if dims == 'tn':
        tm = _pick(M, (1024, 768, 512, 384, 256, 128))
        tn = _pick(N, (1024, 896, 768, 512, 384, 256, 128))
        tk = _pick(K, (1408, 1024, 512, 384, 256, 128))
    else:
        tm = _pick(M, (704, 512, 384, 256, 128) if norm_gain is not None else (1408, 1024, 512, 384, 256, 128))
        tn = _pick(N, (1024, 896, 768, 512, 384, 256, 128))
        tk = _pick(K, (1024, 896, 768, 512, 384, 256, 128))
    nk = K // tk
    if dims == 'nn':
        a_spec = pl.BlockSpec((tm, tk), lambda i, j, k: (i, k))
        b_spec = pl.BlockSpec((tk, tn), lambda i, j, k: (k, j))
        dn = (((1,), (0,)), ((), ()))
    elif dims == 'nt':
        a_spec = pl.BlockSpec((tm, tk), lambda i, j, k: (i, k))
        b_spec = pl.BlockSpec((tn, tk), lambda i, j, k: (j, k))
        dn = (((1,), (1,)), ((), ()))
    else:
        a_spec = pl.BlockSpec((tk, tm), lambda i, j, k: (k, i))
        b_spec = pl.BlockSpec((tk, tn), lambda i, j, k: (k, j))
        dn = (((0,), (0,)), ((), ()))
    o_spec = pl.BlockSpec((tm, tn), lambda i, j, k: (i, j))
    n_ex = len(extras)
    out_shape = jax.ShapeDtypeStruct((M, N), out_dtype)
    out_spec, held, aliases = o_spec, (), {}
    if stack is not None:
        n_slabs, slab, buf = stack
        out_shape = jax.ShapeDtypeStruct((n_slabs, M, N), out_dtype)
        out_spec = pl.BlockSpec((None, tm, tn), lambda i, j, k: (slab, i, j))
        if buf is not None:
            held, aliases = (buf,), {2 + n_ex: 0}

    gains = ()
    if norm_gain is not None:
        assert tn == N and stack is None, "the rms epilogue needs whole rows"
        gains = (norm_gain,)
        out_shape = [out_shape, jax.ShapeDtypeStruct((M, N), BF16)]
        out_spec = [out_spec, o_spec]

    def body(a_ref, b_ref, *rest):
        ex_refs, rest = rest[:n_ex], rest[n_ex:]
        g_refs, rest = rest[:len(gains)], rest[len(gains) + len(held):]
        o_ref, acc = rest[0], rest[-1]
        k = pl.program_id(2)

        @pl.when(k == 0)
        def _():
            acc[...] = jnp.zeros_like(acc)

        av = a_ref[...]
        if a_fn is not None:
            av = a_fn(av)
        acc[...] += lax.dot_general(av.astype(BF16), b_ref[...].astype(BF16), dn,
                                    preferred_element_type=F32)

        @pl.when(k == nk - 1)
        def _():
            r = acc[...]
            if epi is not None:
                r = epi(r, *[e[...] for e in ex_refs])
            o_ref[...] = r.astype(out_dtype)
            if gains:
                rest[1][...] = _rms(r, g_refs[0][...]).astype(BF16)

    return pl.pallas_call(
        body, name=name,
        out_shape=out_shape,
        grid=(M // tm, N // tn, nk),
        in_specs=[a_spec, b_spec] + [o_spec] * n_ex
        + [pl.BlockSpec((1, tn), lambda i, j, k: (0, j))] * len(gains)
        + [pl.BlockSpec(memory_space=pl.ANY)] * len(held),
        out_specs=out_spec,
        input_output_aliases=aliases,
        scratch_shapes=[pltpu.VMEM((tm, tn), F32)],
        compiler_params=_cparams(dimension_semantics=("parallel", "parallel", "arbitrary")),
    )(a, b, *extras, *gains, *held)


def _row_call(fn, rows, consts, out_rows, out_accs=(), *, n_rows, tile, name):
    n_r, n_c, n_o, n_a = len(rows), len(consts), len(out_rows), len(out_accs)
    steps = n_rows // tile

    def body(*refs):
        r_refs = refs[:n_r]
        c_refs = refs[n_r:n_r + n_c]
        o_refs = refs[n_r + n_c:n_r + n_c + n_o]
        a_refs = refs[n_r + n_c + n_o:]
        i = pl.program_id(0)
        res = fn(i, *[r[...] for r in r_refs], *[c[...] for c in c_refs])
        for o_ref, val in zip(o_refs, res[:n_o]):
            o_ref[...] = val.astype(o_ref.dtype)

        @pl.when(i == 0)
        def _():
            for a_ref in a_refs:
                a_ref[...] = jnp.zeros_like(a_ref)

        for a_ref, val in zip(a_refs, res[n_o:]):
            a_ref[...] += val

    in_specs = [pl.BlockSpec((tile, w), functools.partial(lambda i, cb: (i, cb), cb=cb))
                for (_, w, cb) in rows]
    in_specs += [pl.BlockSpec(c.shape, lambda i: (0, 0)) for c in consts]
    out_specs = [pl.BlockSpec((tile, c), lambda i: (i, 0)) for (c, _) in out_rows]
    out_specs += [pl.BlockSpec(s, lambda i: (0, 0)) for s in out_accs]
    out_shape = [jax.ShapeDtypeStruct((n_rows, c), dt) for (c, dt) in out_rows]
    out_shape += [jax.ShapeDtypeStruct(s, F32) for s in out_accs]
    return pl.pallas_call(
        body, name=name, out_shape=out_shape, grid=(steps,),
        in_specs=in_specs, out_specs=out_specs,
        compiler_params=_cparams(dimension_semantics=("arbitrary",)),
    )(*[r[0] for r in rows], *consts)


def _row_mask(i, tile):
    r = i * tile + lax.broadcasted_iota(jnp.int32, (tile, 1), 0)
    return (r >= NPAD).astype(F32)


def _rms(x, g):
    return x * lax.rsqrt(jnp.mean(x * x, axis=-1, keepdims=True) + EPS) * g


def _silu(x):
    return x * (0.5 * jnp.tanh(0.5 * x) + 0.5)


def _softplus(x):
    return jnp.maximum(x, 0.0) + jnp.log(1.0 + jnp.exp(-jnp.abs(x)))


def _rms_fwd(h, g, name):
    lp = h.shape[0]
    return _row_call(lambda i, hv, gv: (_rms(hv, gv),), [(h, D_MODEL, 0)], [g],
                     [(D_MODEL, BF16)], n_rows=lp, tile=_pick(lp, (384, 256, 128)), name=name)[0]


def _rms_bwd(h, g, d_hn, d_res, name):
    lp = h.shape[0]
    tile = _pick(lp, (384, 256, 128))

    def fn(i, hv, dv, rv, gv):
        _, vjp = jax.vjp(_rms, hv, gv)
        dh, dg = vjp(dv)
        return (rv + dh) * _row_mask(i, tile), dg

    return _row_call(fn, [(h, D_MODEL, 0), (d_hn, D_MODEL, 0), (d_res, D_MODEL, 0)], [g],
                     [(D_MODEL, F32)], [(1, D_MODEL)], n_rows=lp, tile=tile, name=name)


@functools.partial(jax.custom_vjp, nondiff_argnums=(1,))
def _roll_rows(x, s):
    return pltpu.roll(x, s, 0)


def _roll_rows_fwd(x, s):
    return pltpu.roll(x, s, 0), None


def _roll_rows_bwd(s, _, ct):
    return (pltpu.roll(ct, (ct.shape[0] - s) % ct.shape[0], 0),)


_roll_rows.defvjp(_roll_rows_fwd, _roll_rows_bwd)


def _conv_silu(cur, halo, w_rows, b):
    full = jnp.concatenate([halo, cur], axis=0)
    acc = cur * w_rows[SSD_CONV - 1] + b
    for k in range(SSD_CONV - 1):
        acc = acc + _roll_rows(full, SSD_CONV - 1 - k)[8:] * w_rows[k]
    return _silu(acc)


def _split3(v):
    hi = v.astype(BF16)
    r1 = v - hi.astype(F32)
    mid = r1.astype(BF16)
    lo = (r1 - mid.astype(F32)).astype(BF16)
    return hi, mid, lo


def _select_right(v, sel, dn):
    return sum(lax.dot_general(p, sel, dn, preferred_element_type=F32) for p in _split3(v))


@jax.custom_vjp
def _expand_heads(v, e_mat):
    return _select_right(v, e_mat, _NN)


def _expand_heads_fwd(v, e_mat):
    return _select_right(v, e_mat, _NN), e_mat


def _expand_heads_bwd(e_mat, ct):
    return _select_right(ct, e_mat, _NT), jnp.zeros_like(e_mat)


_expand_heads.defvjp(_expand_heads_fwd, _expand_heads_bwd)


@jax.custom_vjp
def _cumsum_rows(a, tri):
    return sum(lax.dot_general(tri, p, _NN, preferred_element_type=F32) for p in _split3(a))


def _cumsum_rows_fwd(a, tri):
    return _cumsum_rows(a, tri), tri


def _cumsum_rows_bwd(tri, ct):
    return (sum(lax.dot_general(tri, p, _TN, preferred_element_type=F32) for p in _split3(ct)),
            jnp.zeros_like(tri))


_cumsum_rows.defvjp(_cumsum_rows_fwd, _cumsum_rows_bwd)


def _ssd_chunk(mask, z, xs_pre, bc_pre, halo_x, halo_bc, dt_pre, st, cwx0, cwx1, cwx2, cwx3,
               cwb0, cwb1, cwb2, cwb3, cb_x, cb_bc, dtb, alog, dsk, ng):
    L = CHUNK
    lane_h = lax.broadcasted_iota(jnp.int32, (1, 128), 1)
    head_ok = (lane_h < SSD_HEADS).astype(F32)
    e_mat = (lax.broadcasted_iota(jnp.int32, (128, SSD_D_INNER), 1) // SSD_HEAD_DIM
             == lax.broadcasted_iota(jnp.int32, (128, SSD_D_INNER), 0)).astype(BF16)
    ri = lax.broadcasted_iota(jnp.int32, (L, L), 0)
    ci = lax.broadcasted_iota(jnp.int32, (L, L), 1)
    causal = ri >= ci

    xs = _conv_silu(xs_pre, halo_x, (cwx0, cwx1, cwx2, cwx3), cb_x) * mask
    bc = _conv_silu(bc_pre, halo_bc, (cwb0, cwb1, cwb2, cwb3), cb_bc) * mask
    dt = _softplus(dt_pre + dtb) * mask * head_ok
    a_dt = dt * (-jnp.exp(alog))
    a_cs = _cumsum_rows(a_dt, causal.astype(BF16))
    a_cs_t = a_cs.T
    row8 = lax.broadcasted_iota(jnp.int32, (8, 128), 0)
    last8 = jnp.where(row8 == 0, jnp.sum(a_dt, axis=0, keepdims=True), 0.0)
    dsk8 = jnp.where(row8 == 0, dsk, 0.0)
    wide = _expand_heads(jnp.concatenate([dt, a_cs, last8, dsk8], axis=0), e_mat)
    dt_e, acs_e = wide[0:L], wide[L:2 * L]
    last_e = jnp.sum(wide[2 * L:2 * L + 8], axis=0, keepdims=True)
    d_e = jnp.sum(wide[2 * L + 8:2 * L + 16], axis=0, keepdims=True)
    xdt = xs * dt_e
    dte_e = jnp.exp(last_e - acs_e)
    dfs_e = jnp.exp(acs_e)
    cd_e = jnp.exp(last_e)
    sub_h = lax.broadcasted_iota(jnp.int32, (128, L), 0)
    lane_hl = lax.broadcasted_iota(jnp.int32, (L, 128), 1)
    lane_g = lax.broadcasted_iota(jnp.int32, (1, SSD_HPG * SSD_HEAD_DIM), 1) // SSD_HEAD_DIM

    ys, new_st = [], []
    for g in range(SSD_GROUPS):
        b_g = bc[:, g * 128:(g + 1) * 128].astype(BF16)
        c_g = bc[:, 1024 + g * 128:1024 + (g + 1) * 128].astype(BF16)
        gs = slice(g * 256, (g + 1) * 256)
        xdt_g = xdt[:, gs]
        cb = lax.dot_general(c_g, b_g, (((1,), (1,)), ((), ())), preferred_element_type=F32)
        st_g = st[g * 128:(g + 1) * 128, :]
        y_g = lax.dot_general(c_g, st_g.astype(BF16), (((1,), (0,)), ((), ())),
                              preferred_element_type=F32) * dfs_e[:, gs]
        for j in range(SSD_HPG):
            h = g * SSD_HPG + j
            col = jnp.sum(jnp.where(lane_hl == h, a_cs, 0.0), axis=1, keepdims=True)
            row = jnp.sum(jnp.where(sub_h == h, a_cs_t, 0.0), axis=0, keepdims=True)
            dec = jnp.where(causal, jnp.exp(jnp.where(causal, col - row, 0.0)), 0.0)
            m_h = (cb * dec).astype(BF16)
            x_h = jnp.where(lane_g == j, xdt_g, 0.0).astype(BF16)
            y_g = y_g + lax.dot_general(m_h, x_h, (((1,), (0,)), ((), ())),
                                        preferred_element_type=F32)
        s_new = lax.dot_general(b_g, (xdt_g * dte_e[:, gs]).astype(BF16), (((0,), (0,)), ((), ())),
                                preferred_element_type=F32)
        new_st.append(st_g * cd_e[:, gs] + s_new)
        ys.append(y_g)
    y = jnp.concatenate(ys, axis=1) + xs * d_e
    gg = y * _silu(z)
    outs = []
    for g in range(SSD_GROUPS):
        sl = gg[:, g * 256:(g + 1) * 256]
        outs.append(sl * lax.rsqrt(jnp.mean(sl * sl, axis=-1, keepdims=True) + EPS))
    out = jnp.concatenate(outs, axis=1) * ng
    return out, jnp.concatenate(new_st, axis=0)


def _ssd_consts(conv_w, conv_b, dtb, alog, dsk, ng):
    return [conv_w, conv_b, dtb, alog, dsk, ng]


def _ssd_param_vals(cw_ref, cb_ref, dtb_ref, alog_ref, dsk_ref, ng_ref):
    cwx = [cw_ref[k:k + 1, 0:SSD_D_INNER] for k in range(SSD_CONV)]
    cwb = [cw_ref[k:k + 1, SSD_D_INNER:2 * SSD_D_INNER] for k in range(SSD_CONV)]
    return (*cwx, *cwb, cb_ref[:, 0:SSD_D_INNER], cb_ref[:, SSD_D_INNER:2 * SSD_D_INNER],
            dtb_ref[...], alog_ref[...], dsk_ref[...], ng_ref[...])


def _ssd_in_specs(rev, nc):
    def cidx(i):
        return (nc - 1 - i) if rev else i

    def halo(cb):
        return pl.BlockSpec((8, SSD_D_INNER), lambda i: (jnp.maximum(16 * cidx(i) - 1, 0), cb))

    return [
        pl.BlockSpec((CHUNK, SSD_D_INNER), lambda i: (cidx(i), 0)),
        pl.BlockSpec((CHUNK, SSD_D_INNER), lambda i: (cidx(i), 1)),
        pl.BlockSpec((CHUNK, SSD_D_INNER), lambda i: (cidx(i), 2)),
        halo(1), halo(2),
        pl.BlockSpec((CHUNK, 128), lambda i: (cidx(i), 48)),
    ]


class _Rider:
    def __init__(self, operands, out_shapes, scratch, start, finish):
        self.operands, self.out_shapes, self.scratch = list(operands), list(out_shapes), list(scratch)
        self.start, self.finish = start, finish


def _rider_split(rider, refs, n_in, n_out, n_scratch):
    if rider is None:
        return refs, None
    ni, no = len(rider.operands), len(rider.out_shapes)
    own = refs[:n_in] + refs[n_in + ni:n_in + ni + n_out] + refs[n_in + ni + n_out + no:n_in + ni + n_out + no + n_scratch]
    mine = (refs[n_in:n_in + ni], refs[n_in + ni + n_out:n_in + ni + n_out + no],
            refs[n_in + ni + n_out + no + n_scratch:])
    return own, mine


def _rider_args(rider):
    if rider is None:
        return [], [], [], []
    hbm = pl.BlockSpec(memory_space=pl.ANY)
    return ([hbm] * len(rider.operands), [hbm] * len(rider.out_shapes), rider.out_shapes, rider.scratch)


def _ssd_fwd(zxd, consts, name, rider=None):
    lp = zxd.shape[0]
    nc = lp // CHUNK

    def body(*refs):
        own, ride = _rider_split(rider, refs, 12, 2, 1)
        (z_ref, xs_ref, bc_ref, hx_ref, hb_ref, dt_ref, cw_ref, cb_ref, dtb_ref, alog_ref,
         dsk_ref, ng_ref, y_ref, st_ref, state) = own
        c = pl.program_id(0)

        @pl.when(c == 0)
        def _():
            state[...] = jnp.zeros_like(state)
            if ride is not None:
                rider.start(*ride)

        live = (c > 0).astype(F32)
        st_ref[0] = state[...]
        out, st_new = _ssd_chunk(
            _row_mask(c, CHUNK), z_ref[...], xs_ref[...], bc_ref[...], hx_ref[...] * live,
            hb_ref[...] * live, dt_ref[...], state[...],
            *_ssd_param_vals(cw_ref, cb_ref, dtb_ref, alog_ref, dsk_ref, ng_ref))
        y_ref[...] = out.astype(y_ref.dtype)
        state[...] = st_new

        if ride is not None:
            @pl.when(c == nc - 1)
            def _():
                rider.finish(*ride)

    r_in, r_out, r_shapes, r_scratch = _rider_args(rider)
    return pl.pallas_call(
        body, name=name,
        out_shape=[jax.ShapeDtypeStruct((lp, SSD_D_INNER), BF16),
                   jax.ShapeDtypeStruct((nc, SSD_GROUPS * SSD_STATE, 256), F32)] + r_shapes,
        grid=(nc,),
        in_specs=_ssd_in_specs(False, nc) + [pl.BlockSpec(c.shape, lambda i: (0, 0)) for c in consts] + r_in,
        out_specs=[pl.BlockSpec((CHUNK, SSD_D_INNER), lambda i: (i, 0)),
                   pl.BlockSpec((1, SSD_GROUPS * SSD_STATE, 256), lambda i: (i, 0, 0))] + r_out,
        scratch_shapes=[pltpu.VMEM((SSD_GROUPS * SSD_STATE, 256), F32)] + r_scratch,
        compiler_params=_cparams(dimension_semantics=("arbitrary",)),
    )(zxd, zxd, zxd, zxd, zxd, zxd, *consts, *(rider.operands if rider else ()))


def _ssd_bwd(zxd, states, d_y, consts, name, rider=None):
    lp = zxd.shape[0]
    nc = lp // CHUNK

    def body(*refs):
        own, ride = _rider_split(rider, refs, 14, 7, 3)
        (z_ref, xs_ref, bc_ref, hx_ref, hb_ref, dt_ref, st_ref, dy_ref, cw_ref, cb_ref, dtb_ref,
         alog_ref, dsk_ref, ng_ref, dz_ref, dcw_ref, dcb_ref, ddtb_ref, dalog_ref, ddsk_ref,
         dng_ref, d_state, d_hx, d_hb) = own
        i = pl.program_id(0)
        c = nc - 1 - i

        @pl.when(i == 0)
        def _():
            d_state[...] = jnp.zeros_like(d_state)
            d_hx[...] = jnp.zeros_like(d_hx)
            d_hb[...] = jnp.zeros_like(d_hb)
            for r in (dcw_ref, dcb_ref, ddtb_ref, dalog_ref, ddsk_ref, dng_ref):
                r[...] = jnp.zeros_like(r)
            if ride is not None:
                rider.start(*ride)

        live = (c > 0).astype(F32)
        fn = functools.partial(_ssd_chunk, _row_mask(c, CHUNK))
        prim = (z_ref[...], xs_ref[...], bc_ref[...], hx_ref[...] * live, hb_ref[...] * live,
                dt_ref[...], st_ref[0],
                *_ssd_param_vals(cw_ref, cb_ref, dtb_ref, alog_ref, dsk_ref, ng_ref))
        _, vjp = jax.vjp(fn, *prim)
        (d_z, d_xs, d_bc, g_hx, g_hb, d_dt, g_st, *d_par) = vjp((dy_ref[...], d_state[...]))
        zeros = jnp.zeros((CHUNK - 8, SSD_D_INNER), F32)
        d_xs = d_xs + jnp.concatenate([zeros, d_hx[...]], axis=0)
        d_bc = d_bc + jnp.concatenate([zeros, d_hb[...]], axis=0)
        dz_ref[:, 0:SSD_D_INNER] = d_z.astype(dz_ref.dtype)
        dz_ref[:, SSD_D_INNER:2 * SSD_D_INNER] = d_xs.astype(dz_ref.dtype)
        dz_ref[:, 2 * SSD_D_INNER:3 * SSD_D_INNER] = d_bc.astype(dz_ref.dtype)
        dz_ref[:, 3 * SSD_D_INNER:] = d_dt.astype(dz_ref.dtype)
        d_state[...] = g_st
        d_hx[...] = g_hx * live
        d_hb[...] = g_hb * live
        for k in range(SSD_CONV):
            dcw_ref[k:k + 1, 0:SSD_D_INNER] += d_par[k]
            dcw_ref[k:k + 1, SSD_D_INNER:2 * SSD_D_INNER] += d_par[SSD_CONV + k]
        dcb_ref[:, 0:SSD_D_INNER] += d_par[8]
        dcb_ref[:, SSD_D_INNER:2 * SSD_D_INNER] += d_par[9]
        ddtb_ref[...] += d_par[10]
        dalog_ref[...] += d_par[11]
        ddsk_ref[...] += d_par[12]
        dng_ref[...] += d_par[13]

        if ride is not None:
            @pl.when(i == nc - 1)
            def _():
                rider.finish(*ride)

    const_specs = [pl.BlockSpec(c.shape, lambda i: (0, 0)) for c in consts]
    r_in, r_out, r_shapes, r_scratch = _rider_args(rider)
    return pl.pallas_call(
        body, name=name,
        out_shape=[jax.ShapeDtypeStruct((lp, SSD_IN_PAD), BF16)]
        + [jax.ShapeDtypeStruct(c.shape, F32) for c in consts] + r_shapes,
        grid=(nc,),
        in_specs=_ssd_in_specs(True, nc)
        + [pl.BlockSpec((1, SSD_GROUPS * SSD_STATE, 256), lambda i: (nc - 1 - i, 0, 0)),
           pl.BlockSpec((CHUNK, SSD_D_INNER), lambda i: (nc - 1 - i, 0))] + const_specs + r_in,
        out_specs=[pl.BlockSpec((CHUNK, SSD_IN_PAD), lambda i: (nc - 1 - i, 0))] + const_specs + r_out,
        scratch_shapes=[pltpu.VMEM((SSD_GROUPS * SSD_STATE, 256), F32),
                        pltpu.VMEM((8, SSD_D_INNER), F32), pltpu.VMEM((8, SSD_D_INNER), F32)] + r_scratch,
        compiler_params=_cparams(dimension_semantics=("arbitrary",)),
    )(zxd, zxd, zxd, zxd, zxd, zxd, states, d_y, *consts, *(rider.operands if rider else ()))


@jax.custom_vjp
def _rot_half(x):
    lane = lax.broadcasted_iota(jnp.int32, x.shape, 1)
    lo = (lane >= MLA_NOPE) & (lane < MLA_NOPE + MLA_ROPE // 2)
    hi = (lane >= MLA_NOPE + MLA_ROPE // 2) & (lane < MLA_QK)
    down = pltpu.roll(x, HEAD_SLOT - MLA_ROPE // 2, 1)
    up = pltpu.roll(x, MLA_ROPE // 2, 1)
    return jnp.where(lo, -down, jnp.where(hi, up, 0.0))


def _rot_half_fwd(x):
    return _rot_half(x), None


def _rot_half_bwd(_, ct):
    return (-_rot_half(ct),)


_rot_half.defvjp(_rot_half_fwd, _rot_half_bwd)


def _head_norm_rope(t, gain, cos, sin):
    n = t * lax.rsqrt(jnp.sum(t * t, axis=-1, keepdims=True) * (1.0 / MLA_QK) + EPS) * gain
    return n * cos + _rot_half(n) * sin


def _qk_prep(q_raw, kn_raw, kpe, cos, sin, qg, kg):
    qs, ks = [], []
    for h in range(MLA_HEADS):
        sl = slice(h * HEAD_SLOT, (h + 1) * HEAD_SLOT)
        qs.append(_head_norm_rope(q_raw[:, sl], qg, cos, sin))
        ks.append(_head_norm_rope(kn_raw[:, sl] + kpe, kg, cos, sin))
    return jnp.concatenate(qs, axis=1), jnp.concatenate(ks, axis=1)


def _lat_norm(kv_lat, q_lat, kvg, qg):
    return _rms(kv_lat, kvg), _rms(q_lat, qg)


_NEG = -1e30
_SCALE = MLA_QK ** -0.5


STRIP = 128
_EXP2_SCALE = _SCALE * math.log2(math.e)


def _strip_mask(kind, blk, c, t):
    if kind is None:
        return None
    kpos = blk * t + c * STRIP + lax.broadcasted_iota(jnp.int32, (1, STRIP), 1)
    if kind == 'keys':
        return kpos >= NPAD
    qpos = blk * t + lax.broadcasted_iota(jnp.int32, (t, 1), 0)
    return (kpos <= qpos) & ((kpos >= NPAD) | (kpos == qpos))


def _attn_fwd(q, k, v, name, rider=None):
    lp = q.shape[0]
    t = tk = _pick(lp, (384, 256, 128))
    nb = lp // t
    hp = HEADS_PER_STEP
    wide = hp * HEAD_SLOT
    heads = [slice(a * HEAD_SLOT, (a + 1) * HEAD_SLOT) for a in range(hp)]

    def body(*refs):
        (q_ref, k_ref, v_ref, o_ref, lse_ref), ride = _rider_split(rider, refs, 3, 2, 0)
        qi = pl.program_id(1)
        if ride is not None:
            @pl.when((pl.program_id(0) == 0) & (qi == 0))
            def _():
                rider.start(*ride)

        def update(a, ki, carry, mask):
            rows = pl.ds(pl.multiple_of(ki * tk, tk), tk)
            m, acc = carry
            s = lax.dot_general(q_ref[:, heads[a]], k_ref[rows, heads[a]], _NT, preferred_element_type=F32)
            if mask is not None:
                s = jnp.where(mask, s, _NEG)
            m_new = jnp.maximum(m, jnp.max(s, axis=-1, keepdims=True))
            alpha = jnp.exp2((m - m_new) * _EXP2_SCALE)
            p = jnp.exp2((s - m_new) * _EXP2_SCALE)
            acc = alpha * acc + lax.dot_general(p.astype(BF16), v_ref[rows, heads[a]], _NN,
                                                preferred_element_type=F32)
            return m_new, acc

        def step(ki, carry, mask):
            return tuple(update(a, ki, carry[a], mask) for a in range(hp))

        init = (jnp.full((t, 1), _NEG, F32), jnp.zeros((t, HEAD_SLOT), F32))
        ones_lane = lax.broadcasted_iota(jnp.int32, (1, HEAD_SLOT), 1) == MLA_V
        key_ok = lax.broadcasted_iota(jnp.int32, (1, tk), 1) >= NPAD
        n_full = (qi * t) // tk
        carry = lax.cond(n_full > 0, lambda c: step(0, c, key_ok), lambda c: c, (init,) * hp)
        carry = lax.fori_loop(1, n_full, lambda ki, c: step(ki, c, None), carry)
        qpos = qi * t + lax.broadcasted_iota(jnp.int32, (t, tk), 0)
        kpos = n_full * tk + lax.broadcasted_iota(jnp.int32, (t, tk), 1)
        carry = step(n_full, carry, (kpos <= qpos) & ((kpos >= NPAD) | (kpos == qpos)))
        for a in range(hp):
            m, acc = carry[a]
            l = jnp.sum(jnp.where(ones_lane, acc, 0.0), axis=-1, keepdims=True)
            o_ref[:, heads[a]] = jnp.where(ones_lane, 0.0, acc / l * _row_mask(qi, t))
            lse_ref[a] = m * _SCALE + jnp.log(l)

        if ride is not None:
            @pl.when((pl.program_id(0) == MLA_HEADS // hp - 1) & (qi == nb - 1))
            def _():
                rider.finish(*ride)

    qspec = pl.BlockSpec((t, wide), lambda g, i: (i, g))
    kspec = pl.BlockSpec((lp, wide), lambda g, i: (0, g))
    r_in, r_out, r_shapes, r_scratch = _rider_args(rider)
    return pl.pallas_call(
        body, name=name,
        out_shape=[jax.ShapeDtypeStruct((lp, MLA_WIDE), F32),
                   jax.ShapeDtypeStruct((MLA_HEADS, lp, 1), F32)] + r_shapes,
        grid=(MLA_HEADS // hp, nb),
        in_specs=[qspec, kspec, kspec] + r_in,
        out_specs=[qspec, pl.BlockSpec((hp, t, 1), lambda g, i: (g, i, 0))] + r_out,
        scratch_shapes=r_scratch,
        compiler_params=_cparams(dimension_semantics=("arbitrary", "arbitrary")),
    )(q, k, v, *(rider.operands if rider else ()))


def _attn_delta(do, o, name):
    lp = do.shape[0]
    t = _pick(lp, (384, 256, 128))

    def body(do_ref, o_ref, dob_ref, delta_ref):
        dob_ref[...] = do_ref[...].astype(BF16)
        for h in range(MLA_HEADS):
            sl = slice(h * HEAD_SLOT, (h + 1) * HEAD_SLOT)
            delta_ref[h] = jnp.sum(do_ref[:, sl] * o_ref[:, sl], axis=-1, keepdims=True)

    spec = pl.BlockSpec((t, MLA_WIDE), lambda i: (i, 0))
    return pl.pallas_call(
        body, name=name,
        out_shape=[jax.ShapeDtypeStruct((lp, MLA_WIDE), BF16), jax.ShapeDtypeStruct((MLA_HEADS, lp, 1), F32)],
        grid=(lp // t,), in_specs=[spec, spec],
        out_specs=[spec, pl.BlockSpec((MLA_HEADS, t, 1), lambda i: (0, i, 0))],
        compiler_params=_cparams(dimension_semantics=("parallel",)),
    )(do, o)


def _attn_bwd(q, k, v, do, lse, delta, name, rider=None):
    lp = q.shape[0]
    t = _pick(lp, (384, 256, 128))
    nb = lp // t
    ns = t // STRIP
    hp = HEADS_PER_STEP
    wide = hp * HEAD_SLOT
    heads = [slice(a * HEAD_SLOT, (a + 1) * HEAD_SLOT) for a in range(hp)]
    log2e = math.log2(math.e)

    def body(*refs):
        own, ride = _rider_split(rider, refs, 6, 3, 4)
        (q_ref, k_ref, v_ref, do_ref, lse_ref, delta_ref, dq_ref, dk_ref, dv_ref,
         s_scr, dp_scr, p_scr, ds_scr) = own
        kj = pl.program_id(1)
        if ride is not None:
            @pl.when((pl.program_id(0) == 0) & (kj == 0))
            def _():
                rider.start(*ride)

        @pl.when(kj == 0)
        def _():
            dq_ref[...] = jnp.zeros_like(dq_ref)

        dk_ref[...] = jnp.zeros_like(dk_ref)
        dv_ref[...] = jnp.zeros_like(dv_ref)

        def tile(qi, kind):
            rows = pl.ds(pl.multiple_of(qi * t, t), t)
            for a in range(hp):
                qb, dob = q_ref[rows, heads[a]], do_ref[rows, heads[a]]
                kb, vb = k_ref[:, heads[a]], v_ref[:, heads[a]]
                s_scr[a] = lax.dot_general(qb, kb, _NT, preferred_element_type=F32)
                dp_scr[a] = lax.dot_general(dob, vb, _NT, preferred_element_type=F32)
                lse2 = lse_ref[a, rows, :] * log2e
                delta = delta_ref[a, rows, :]
                for c in range(ns):
                    cs = slice(c * STRIP, (c + 1) * STRIP)
                    pc = jnp.exp2(s_scr[a, :, cs] * _EXP2_SCALE - lse2)
                    pc = jnp.where(_strip_mask(kind, kj, c, t), pc, 0.0)
                    p_scr[a, :, cs] = pc.astype(BF16)
                    ds_scr[a, :, cs] = (pc * (dp_scr[a, :, cs] - delta)).astype(BF16)
                dq_ref[rows, heads[a]] += lax.dot_general(ds_scr[a], kb, _NN,
                                                          preferred_element_type=F32) * _SCALE
                dv_ref[:, heads[a]] += lax.dot_general(p_scr[a], dob, _TN, preferred_element_type=F32)
                dk_ref[:, heads[a]] += lax.dot_general(ds_scr[a], qb, _TN, preferred_element_type=F32)

        tile(kj, 'diag')

        def below(qi, carry):
            tile(qi, 'keys')
            return carry

        lax.fori_loop(kj + 1, nb, below, 0)
        dk_ref[...] = dk_ref[...] * _SCALE

        if ride is not None:
            @pl.when((pl.program_id(0) == MLA_HEADS // hp - 1) & (kj == nb - 1))
            def _():
                rider.finish(*ride)

    whole = pl.BlockSpec((lp, wide), lambda g, j: (0, g))
    kspec = pl.BlockSpec((t, wide), lambda g, j: (j, g))
    stat = pl.BlockSpec((hp, lp, 1), lambda g, j: (g, 0, 0))
    r_in, r_out, r_shapes, r_scratch = _rider_args(rider)
    return pl.pallas_call(
        body, name=name,
        out_shape=[jax.ShapeDtypeStruct((lp, MLA_WIDE), F32)] * 3 + r_shapes,
        grid=(MLA_HEADS // hp, nb),
        in_specs=[whole, kspec, kspec, whole, stat, stat] + r_in,
        out_specs=[whole, kspec, kspec] + r_out,
        scratch_shapes=[pltpu.VMEM((hp, t, t), F32), pltpu.VMEM((hp, t, t), F32),
                        pltpu.VMEM((hp, t, t), BF16), pltpu.VMEM((hp, t, t), BF16)] + r_scratch,
        compiler_params=_cparams(dimension_semantics=("arbitrary", "arbitrary")),
    )(q, k, v, do, lse, delta, *(rider.operands if rider else ()))


def _rope_tables(lp):
    inv = 1.0 / (ROPE_THETA ** (jnp.arange(0, MLA_ROPE, 2, dtype=F32) / MLA_ROPE))
    pos = jnp.maximum(jnp.arange(lp, dtype=jnp.int32) - NPAD, 0).astype(F32)
    ang = pos[:, None] * inv[None, :]
    cos, sin = jnp.cos(ang), jnp.sin(ang)
    z32 = jnp.zeros((lp, HEAD_SLOT - MLA_QK), F32)
    cos_t = jnp.concatenate([jnp.ones((lp, MLA_NOPE), F32), cos, cos, z32], axis=1)
    sin_t = jnp.concatenate([jnp.zeros((lp, MLA_NOPE), F32), sin, sin, z32], axis=1)
    return cos_t, sin_t


def _loss_head(h, target, name):
    lp = h.shape[0]

    def body(h_ref, t_ref, d_ref, loss_ref):
        i = pl.program_id(0)

        @pl.when(i == 0)
        def _():
            d_ref[...] = jnp.zeros_like(d_ref)
            loss_ref[...] = jnp.zeros_like(loss_ref)

        @pl.when(i > 0)
        def _():
            err = h_ref[...] - t_ref[...]
            d_ref[...] = err * (1.0 / D_MODEL)
            loss_ref[...] += jnp.sum(err * err, axis=0, keepdims=True) * (0.5 / D_MODEL)

    return pl.pallas_call(
        body, name=name,
        out_shape=[jax.ShapeDtypeStruct((lp, D_MODEL), F32), jax.ShapeDtypeStruct((1, D_MODEL), F32)],
        grid=(lp // CHUNK,),
        in_specs=[pl.BlockSpec((CHUNK, D_MODEL), lambda i: (i, 0)),
                  pl.BlockSpec((CHUNK, D_MODEL), lambda i: (jnp.maximum(i - 1, 0), 0))],
        out_specs=[pl.BlockSpec((CHUNK, D_MODEL), lambda i: (i, 0)),
                   pl.BlockSpec((1, D_MODEL), lambda i: (0, 0))],
        compiler_params=_cparams(dimension_semantics=("arbitrary",)),
    )(h, target)


def _pad_cols(w, n):
    return jnp.pad(w, [(0, 0)] * (w.ndim - 1) + [(0, n - w.shape[-1])])


def _layer_slab(name, i):
    return i if name.startswith('mlp_') else i // 2


def _prep_layer(p, i, get):
    j = i // 2
    if i % 2 == 0:
        p['ssd_in'][j] = _pad_cols(get('ssd_w_in'), SSD_IN_PAD).astype(BF16)
        p['ssd_out'][j] = get('ssd_w_out').astype(BF16)
    else:
        wi = get('mla_w_in')
        kpe = jnp.pad(wi[:, MLA_Q_RANK + MLA_KV_RANK:], ((0, 0), (MLA_NOPE, HEAD_SLOT - MLA_QK)))
        p['mla_in'][j] = jnp.concatenate(
            [wi[:, MLA_Q_RANK:MLA_Q_RANK + MLA_KV_RANK], kpe, wi[:, :MLA_Q_RANK]], axis=1).astype(BF16)
        qb = get('mla_w_q_b').reshape(MLA_Q_RANK, MLA_HEADS, MLA_QK)
        p['mla_qb'][j] = _pad_cols(qb, HEAD_SLOT).reshape(MLA_Q_RANK, MLA_WIDE).astype(BF16)
        kvb = get('mla_w_kv_b').reshape(MLA_KV_RANK, MLA_HEADS, MLA_NOPE + MLA_V)
        kn = _pad_cols(kvb[:, :, :MLA_NOPE], HEAD_SLOT).reshape(MLA_KV_RANK, MLA_WIDE)
        vv = _pad_cols(kvb[:, :, MLA_NOPE:], HEAD_SLOT).reshape(MLA_KV_RANK, MLA_WIDE)
        p['mla_kvb'][j] = jnp.concatenate([kn, vv], axis=1).astype(BF16)
        wo = get('mla_w_out').reshape(MLA_HEADS, MLA_V, D_MODEL)
        p['mla_out'][j] = (jnp.pad(wo, ((0, 0), (0, HEAD_SLOT - MLA_V), (0, 0)))
                           .reshape(MLA_WIDE, D_MODEL).astype(BF16))
    p['up'][i] = get('mlp_w_up').astype(BF16)
    p['down'][i] = get('mlp_w_down').astype(BF16)


def _no_matrices():
    return {k: [None] * n for k, n in (('ssd_in', 2), ('ssd_out', 2), ('mla_in', 2), ('mla_qb', 2),
                                       ('mla_kvb', 2), ('mla_out', 2), ('up', 4), ('down', 4))}


class _ReadyWeights:
    def __init__(self, w):
        self.w, self.p = w, _no_matrices()

    def ensure(self, i):
        _prep_layer(self.p, i, lambda n: self.w[n][_layer_slab(n, i)])

    def rider(self, i):
        return None

    def deliver(self, i, outs):
        assert not outs


class _KeepGrads:
    def __init__(self):
        self.rounds = {}

    def begin(self, r, grads):
        self.rounds[r] = grads
        return None

    def finish(self, r, outs):
        assert not outs

    def result(self):
        names = {n for g in self.rounds.values() for n in g}
        return {n: jnp.concatenate([self.rounds[r][n] for r in sorted(self.rounds, reverse=True)
                                    if n in self.rounds[r]], axis=0) for n in names}


def _pad128(v):
    return _pad_cols(v.reshape(1, -1), 128)


def _sqrelu(u):
    r = jnp.maximum(u, 0.0)
    return r * r


def _local_step(x, target, w, big=None, red=None):
    seq = x.shape[0]
    lp = NPAD + N_META + seq
    big = _ReadyWeights(w) if big is None else big
    p = big.p
    h = jnp.concatenate([jnp.zeros((NPAD, D_MODEL), F32), w['meta_tokens'], x], axis=0)
    cos_t, sin_t = _rope_tables(lp)
    rt = _pick(lp, (384, 256, 128))
    saved = []
    for i in range(4):
        j = i // 2
        big.ensure(i)
        s = {'h0': h}
        g_mix = w['ln_mix'][i].reshape(1, -1)
        g_mlp = w['ln_mlp'][i].reshape(1, -1)
        if i == 0:
            hn = _rms_fwd(h, g_mix, f"rms_mix_f{i}")
        s['hn'] = hn
        if i % 2 == 0:
            zxd = _mm(hn, p['ssd_in'][j], 'nn', name=f"ssd_in_f{i}")
            consts = _ssd_consts(w['ssd_conv_w'][j], w['ssd_conv_b'][j].reshape(1, -1),
                                 _pad128(w['ssd_dt_bias'][j]), _pad128(w['ssd_a_log'][j]),
                                 _pad128(w['ssd_d'][j]), w['ssd_norm'][j].reshape(1, -1))
            yg, states, *got = _ssd_fwd(zxd, consts, f"ssd_core_f{i}", rider=big.rider(i))
            big.deliver(i, got)
            s.update(zxd=zxd, consts=consts, yg=yg, states=states)
            h, hn2 = _mm(yg, p['ssd_out'][j], 'nn', name=f"ssd_out_f{i}", epi=lambda r, hv: hv + r,
                         extras=(h,), norm_gain=g_mlp)
        else:
            lat = _mm(hn, p['mla_in'][j], 'nn', name=f"mla_in_f{i}")
            kvg = w['mla_kv_a_norm'][j].reshape(1, -1)
            qag = w['mla_q_a_norm'][j].reshape(1, -1)
            kvn, qn = _row_call(lambda _, a, b, c, d: _lat_norm(a, b, c, d),
                                [(lat, MLA_KV_RANK, 0), (lat, MLA_Q_RANK, 1)], [kvg, qag],
                                [(MLA_KV_RANK, BF16), (MLA_Q_RANK, BF16)], n_rows=lp, tile=rt,
                                name=f"mla_latnorm_f{i}")
            q_raw = _mm(qn, p['mla_qb'][j], 'nn', name=f"mla_qb_f{i}")
            kv_raw = _mm(kvn, p['mla_kvb'][j], 'nn', name=f"mla_kvb_f{i}")
            qg = _pad_cols(w['mla_q_norm'][j].reshape(1, -1), HEAD_SLOT)
            kg = _pad_cols(w['mla_k_norm'][j].reshape(1, -1), HEAD_SLOT)

            def prep_fwd(_, qr, kn, kpe, vv, cs, sn, qgv, kgv):
                qq, kk = _qk_prep(qr, kn, kpe, cs, sn, qgv, kgv)
                ones = lax.broadcasted_iota(jnp.int32, vv.shape, 1) % HEAD_SLOT == MLA_V
                return qq, kk, jnp.where(ones, 1.0, vv)

            q, k, v = _row_call(prep_fwd,
                                [(q_raw, MLA_WIDE, 0), (kv_raw, MLA_WIDE, 0), (lat, HEAD_SLOT, 2),
                                 (kv_raw, MLA_WIDE, 1), (cos_t, HEAD_SLOT, 0), (sin_t, HEAD_SLOT, 0)],
                                [qg, kg], [(MLA_WIDE, BF16)] * 3, n_rows=lp, tile=rt,
                                name=f"mla_qkprep_f{i}")
            o, lse, *got = _attn_fwd(q, k, v, f"mla_attn_f{i}", rider=big.rider(i))
            big.deliver(i, got)
            s.update(lat=lat, kvg=kvg, qag=qag, kvn=kvn, qn=qn, q_raw=q_raw, kv_raw=kv_raw, qg=qg, kg=kg,
                     q=q, k=k, v=v, o=o, lse=lse)
            h, hn2 = _mm(o, p['mla_out'][j], 'nn', name=f"mla_out_f{i}", epi=lambda r, hv: hv + r,
                         extras=(h,), norm_gain=g_mlp)
        s['h1'] = h
        u = _mm(hn2, p['up'][i], 'nn', name=f"mlp_up_f{i}", out_dtype=BF16)
        if i < 3:
            h, hn = _mm(u, p['down'][i], 'nn', name=f"mlp_down_f{i}", a_fn=_sqrelu, epi=lambda r, hv: hv + r,
                        extras=(h,), norm_gain=w['ln_mix'][i + 1].reshape(1, -1))
        else:
            h = _mm(u, p['down'][i], 'nn', name=f"mlp_down_f{i}", a_fn=_sqrelu,
                    epi=lambda r, hv: hv + r, extras=(h,))
        s.update(hn2=hn2, u=u, g_mix=g_mix, g_mlp=g_mlp)
        saved.append(s)

    dh, loss_row = _loss_head(h, target, "loss_head")

    large = {n for n, _ in BIG}
    g = {k_: [None] * (4 if k_ in ('ln_mix', 'ln_mlp') else 2)
         for k_ in ALL_NAMES if k_ != 'meta_tokens' and k_ not in large}
    red = _KeepGrads() if red is None else red
    rounds, pending = {}, None

    def round_of(nm, i):
        return next(r for r, spec in enumerate(REDUCE_ROUNDS)
                    for n, l0, l1 in spec if n == nm and l0 <= _layer_slab(nm, i) < l1)

    def slabs_in(nm, r):
        return next((l0, l1) for n, l0, l1 in REDUCE_ROUNDS[r] if n == nm)

    def dw_into(nm, i, a, b, **kw):
        r = round_of(nm, i)
        (l0, l1), cur = slabs_in(nm, r), rounds.setdefault(r, {})
        cur[nm] = _mm(a, b, 'tn', stack=(l1 - l0, _layer_slab(nm, i) - l0, cur.get(nm)), **kw)

    def put(nm, i, arr):
        rounds.setdefault(round_of(nm, i), {})[nm] = arr[None]

    def hand_over(r):
        nonlocal pending
        pending = (r, red.begin(r, rounds.pop(r)))

    def host(fn, *args):
        nonlocal pending
        if pending is None or pending[1] is None:
            return fn(*args)
        (r, rider), pending = pending, None
        outs = fn(*args, rider=rider)
        own = len(outs) - len(rider.out_shapes)
        red.finish(r, outs[own:])
        return outs[:own]

    for i in reversed(range(4)):
        j = i // 2
        s = saved[i]
        dw_into('mlp_w_down', i, s['u'], dh, name=f"mlp_down_dw{i}", a_fn=_sqrelu)
        du = _mm(dh, p['down'][i], 'nt', name=f"mlp_down_dx{i}", out_dtype=BF16,
                 epi=lambda r, uv: r * (2.0 * jnp.maximum(uv, 0.0)), extras=(s['u'],))
        dw_into('mlp_w_up', i, s['hn2'], du, name=f"mlp_up_dw{i}")
        d_hn2 = _mm(du, p['up'][i], 'nt', name=f"mlp_up_dx{i}")
        dh, dg = _rms_bwd(s['h1'], s['g_mlp'], d_hn2, dh, f"rms_mlp_b{i}")
        g['ln_mlp'][i] = dg[0]
        if i % 2 == 0:
            dw_into('ssd_w_out', i, s['yg'], dh, name=f"ssd_out_dw{i}")
            d_yg = _mm(dh, p['ssd_out'][j], 'nt', name=f"ssd_out_dx{i}")
            if i == 0:
                hand_over(1)
            d_zxd, dcw, dcb, ddtb, dalog, ddsk, dng = host(_ssd_bwd, s['zxd'], s['states'], d_yg, s['consts'],
                                                           f"ssd_core_b{i}")
            g['ssd_conv_w'][j], g['ssd_conv_b'][j], g['ssd_norm'][j] = dcw, dcb[0], dng[0]
            g['ssd_dt_bias'][j], g['ssd_a_log'][j], g['ssd_d'][j] = (
                ddtb[0, :SSD_HEADS], dalog[0, :SSD_HEADS], ddsk[0, :SSD_HEADS])
            dw_into('ssd_w_in', i, s['hn'], d_zxd, name=f"ssd_in_dw{i}")
            d_hn = _mm(d_zxd, p['ssd_in'][j], 'nt', name=f"ssd_in_dx{i}")
        else:
            wo = _mm(s['o'], dh, 'tn', name=f"mla_out_dw{i}")
            put('mla_w_out', i, wo.reshape(MLA_HEADS, HEAD_SLOT, D_MODEL)[:, :MLA_V].reshape(-1, D_MODEL))
            do = _mm(dh, p['mla_out'][j], 'nt', name=f"mla_out_dx{i}")
            dob, delta = _attn_delta(do, s['o'], f"mla_attn_delta{i}")
            dq, dk, dv = host(_attn_bwd, s['q'], s['k'], s['v'], dob, s['lse'], delta, f"mla_attn_b{i}")

            def prep_bwd(_, qr, kn, kpe, cs, sn, dqv, dkv, dvv, qgv, kgv):
                _, vjp = jax.vjp(lambda a, b, c, d, e: _qk_prep(a, b, c, cs, sn, d, e), qr, kn, kpe, qgv, kgv)
                d_qr, d_kn, d_kpe, d_qg, d_kg = vjp((dqv, dkv))
                return d_qr, jnp.concatenate([d_kn, dvv], axis=1), d_kpe, d_qg, d_kg

            d_qraw, d_kvraw, d_kpe, d_qg, d_kg = _row_call(
                prep_bwd,
                [(s['q_raw'], MLA_WIDE, 0), (s['kv_raw'], MLA_WIDE, 0), (s['lat'], HEAD_SLOT, 2),
                 (cos_t, HEAD_SLOT, 0), (sin_t, HEAD_SLOT, 0), (dq, MLA_WIDE, 0), (dk, MLA_WIDE, 0),
                 (dv, MLA_WIDE, 0)],
                [s['qg'], s['kg']], [(MLA_WIDE, BF16), (2 * MLA_WIDE, BF16), (HEAD_SLOT, F32)],
                [(1, HEAD_SLOT), (1, HEAD_SLOT)], n_rows=lp, tile=_pick(lp, (128,)), name=f"mla_qkprep_b{i}")
            g['mla_q_norm'][j], g['mla_k_norm'][j] = d_qg[0, :MLA_QK], d_kg[0, :MLA_QK]
            wqb = _mm(s['qn'], d_qraw, 'tn', name=f"mla_qb_dw{i}")
            put('mla_w_q_b', i, wqb.reshape(MLA_Q_RANK, MLA_HEADS, HEAD_SLOT)[:, :, :MLA_QK].reshape(MLA_Q_RANK, -1))
            d_qn = _mm(d_qraw, p['mla_qb'][j], 'nt', name=f"mla_qb_dx{i}")
            wkvb = _mm(s['kvn'], d_kvraw, 'tn', name=f"mla_kvb_dw{i}").reshape(MLA_KV_RANK, 2, MLA_HEADS, HEAD_SLOT)
            put('mla_w_kv_b', i, jnp.concatenate([wkvb[:, 0, :, :MLA_NOPE], wkvb[:, 1, :, :MLA_V]],
                                                 axis=-1).reshape(MLA_KV_RANK, -1))
            d_kvn = _mm(d_kvraw, p['mla_kvb'][j], 'nt', name=f"mla_kvb_dx{i}")

            def lat_bwd(_, kvl, ql, dkvn, dqn, dkpe, kvgv, qagv):
                _, vjp = jax.vjp(_lat_norm, kvl, ql, kvgv, qagv)
                d_kvl, d_ql, d_kvg, d_qag = vjp((dkvn, dqn))
                return jnp.concatenate([d_kvl, dkpe, d_ql], axis=1), d_kvg, d_qag

            d_lat, d_kvg, d_qag = _row_call(
                lat_bwd, [(s['lat'], MLA_KV_RANK, 0), (s['lat'], MLA_Q_RANK, 1), (d_kvn, MLA_KV_RANK, 0),
                          (d_qn, MLA_Q_RANK, 0), (d_kpe, HEAD_SLOT, 0)],
                [s['kvg'], s['qag']], [(LAT_PAD, BF16)], [(1, MLA_KV_RANK), (1, MLA_Q_RANK)],
                n_rows=lp, tile=rt, name=f"mla_latnorm_b{i}")
            g['mla_kv_a_norm'][j], g['mla_q_a_norm'][j] = d_kvg[0], d_qag[0]
            win = _mm(s['hn'], d_lat, 'tn', name=f"mla_in_dw{i}")
            put('mla_w_in', i, jnp.concatenate(
                [win[:, MLA_KV_RANK + HEAD_SLOT:], win[:, :MLA_KV_RANK],
                 win[:, MLA_KV_RANK + MLA_NOPE:MLA_KV_RANK + MLA_QK]], axis=1))
            d_hn = _mm(d_lat, p['mla_in'][j], 'nt', name=f"mla_in_dx{i}")
        dh, dg = _rms_bwd(s['h0'], s['g_mix'], d_hn, dh, f"rms_mix_b{i}")
        g['ln_mix'][i] = dg[0]
        if i == 2:
            hand_over(0)
    hand_over(2)

    if pending[1] is not None:
        red.finish(pending[0], _run_rider(pending[1], "rs_exchange_last"))
    grads = {k_: jnp.stack(v_) for k_, v_ in g.items()}
    grads['meta_tokens'] = dh[NPAD:NPAD + N_META]
    return loss_row, dh[NPAD + N_META:], grads, red


def _all_gather8(shard, name):
    m_per, n = shard.shape

    def body(x_ref, out_ref, send_sems, recv_sems, local_sem):
        x, y, c = lax.axis_index("x"), lax.axis_index("y"), lax.axis_index("c")
        me, sibling = (x, y, c), (x, y, 1 - c)
        chips = [(1 - x, y), (x, 1 - y), (1 - x, 1 - y)]

        def rows(px, py, pc):
            return out_ref.at[pl.ds((4 * px + 2 * py + pc) * m_per, m_per), :]

        def copy(k, block, to, src=None):
            return pltpu.make_async_remote_copy(
                src_ref=rows(*block) if src is None else src, dst_ref=rows(*block),
                send_sem=send_sems.at[k], recv_sem=recv_sems.at[k], device_id=to, device_id_type=MESH)

        mine = pltpu.make_async_copy(x_ref, rows(*me), local_sem)
        mine.start()
        first = [copy(0, me, sibling, src=x_ref)]
        first += [copy(1 + j, me, (*chip, c), src=x_ref) for j, chip in enumerate(chips)]
        for cp in first:
            cp.start()
        passed = [copy(4 + j, (*chip, c), sibling) for j, chip in enumerate(chips)]
        for j, chip in enumerate(chips):
            copy(1 + j, (*chip, c), me).wait_recv()
            passed[j].start()
        copy(0, sibling, me).wait_recv()
        for j, chip in enumerate(chips):
            copy(4 + j, (*chip, 1 - c), me).wait_recv()
        for cp in first + passed:
            cp.wait_send()
        mine.wait()

    return pl.pallas_call(
        body, name=name,
        out_shape=jax.ShapeDtypeStruct((8 * m_per, n), shard.dtype),
        in_specs=[pl.BlockSpec(memory_space=pl.ANY)],
        out_specs=pl.BlockSpec(memory_space=pl.ANY),
        scratch_shapes=[pltpu.SemaphoreType.DMA((7,)), pltpu.SemaphoreType.DMA((7,)), pltpu.SemaphoreType.DMA],
    )(shard)


def _mesh_pos():
    return lax.axis_index("x"), lax.axis_index("y"), lax.axis_index("c")


def _half_rows(pc, h):
    return pl.ds(pl.multiple_of(pc * h, 16), h)


def _whole_view(ref, kind, shard_shape, k, pc):
    _, r, c = shard_shape
    rows = _half_rows(pc, r // 2)
    if kind == 'row':
        return ref.at[:, k, rows, :]
    if kind == 'col':
        return ref.at[:, rows, pl.ds(pl.multiple_of(k * c, 128), c)]
    return ref.at[k, :, rows, :]


def _whole_shape(kind, shard_shape, rows=None):
    l, r, c = shard_shape
    r = r if rows is None else rows
    return {'row': (l, 4, r, c), 'col': (l, r, 4 * c), 'colx': (4, l, r, c)}[kind]


def _gather_rider(shards, kinds):
    n = len(shards)
    shapes = [s.shape for s in shards]

    def plan(ins, outs, sems):
        send_sems, recv_sems, local_sems = sems
        x, y, c = _mesh_pos()
        me, sibling = (x, y, c), (x, y, 1 - c)
        chips = [(1 - x, y), (x, 1 - y), (1 - x, 1 - y)]

        def place(a, px, py, pc):
            return _whole_view(outs[a], kinds[a], shapes[a], 2 * px + py, pc)

        def own(a):
            return ins[a].at[:, _half_rows(c, shapes[a][1] // 2), :]

        def copy(a, k, block, to, src=None):
            return pltpu.make_async_remote_copy(
                src_ref=place(a, *block) if src is None else src, dst_ref=place(a, *block),
                send_sem=send_sems.at[7 * a + k], recv_sem=recv_sems.at[7 * a + k],
                device_id=to, device_id_type=MESH)

        mine = [pltpu.make_async_copy(own(a), place(a, *me), local_sems.at[a]) for a in range(n)]
        first = [copy(a, 1 + j, me, (*chip, c), src=own(a)) for j, chip in enumerate(chips) for a in range(n)]
        first += [copy(a, 0, me, sibling, src=own(a)) for a in range(n)]
        return copy, mine, first, chips, me, sibling, c

    def start(ins, outs, sems):
        _, mine, first, *_ = plan(ins, outs, sems)
        for cp in first + mine:
            cp.start()

    def finish(ins, outs, sems):
        copy, mine, first, chips, me, sibling, c = plan(ins, outs, sems)
        passed = []
        for j, chip in enumerate(chips):
            for a in range(n):
                copy(a, 1 + j, (*chip, c), me).wait_recv()
                passed.append(copy(a, 4 + j, (*chip, c), sibling))
                passed[-1].start()
        for a in range(n):
            copy(a, 0, sibling, me).wait_recv()
        for j, chip in enumerate(chips):
            for a in range(n):
                copy(a, 4 + j, (*chip, 1 - c), me).wait_recv()
        for cp in first + passed:
            cp.wait_send()
        for cp in mine:
            cp.wait()

    return _Rider(
        shards, [jax.ShapeDtypeStruct(_whole_shape(k, s.shape), s.dtype) for k, s in zip(kinds, shards)],
        [pltpu.SemaphoreType.DMA((7 * n,)), pltpu.SemaphoreType.DMA((7 * n,)), pltpu.SemaphoreType.DMA((n,))],
        start, finish)


def _run_rider(rider, name):
    ni, no = len(rider.operands), len(rider.out_shapes)

    def body(*refs):
        ride = (refs[:ni], refs[ni:ni + no], refs[ni + no:])
        rider.start(*ride)
        rider.finish(*ride)

    return pl.pallas_call(
        body, name=name, out_shape=rider.out_shapes,
        in_specs=[pl.BlockSpec(memory_space=pl.ANY)] * ni,
        out_specs=[pl.BlockSpec(memory_space=pl.ANY)] * no,
        scratch_shapes=rider.scratch,
    )(*rider.operands)


def _rs_swap(wholes, kinds, shapes, name):
    n = len(wholes)

    def body(*refs):
        ins, outs = refs[:n], refs[n:2 * n]
        send_sems, recv_sems = refs[2 * n:]
        x, y, c = _mesh_pos()
        cps = []
        for a in range(n):
            rows = _half_rows(1 - c, shapes[a][1] // 2)
            src = ins[a].at[:, rows, :] if kinds[a] == 'col' else ins[a].at[:, :, rows, :]
            cps.append(pltpu.make_async_remote_copy(
                src_ref=src, dst_ref=outs[a], send_sem=send_sems.at[a], recv_sem=recv_sems.at[a],
                device_id=(x, y, 1 - c), device_id_type=MESH))
        for cp in cps:
            cp.start()
        for cp in cps:
            cp.wait()

    return pl.pallas_call(
        body, name=name,
        out_shape=[jax.ShapeDtypeStruct(_whole_shape(k, s, s[1] // 2), w.dtype)
                   for k, s, w in zip(kinds, shapes, wholes)],
        in_specs=[pl.BlockSpec(memory_space=pl.ANY)] * n,
        out_specs=[pl.BlockSpec(memory_space=pl.ANY)] * n,
        scratch_shapes=[pltpu.SemaphoreType.DMA((n,)), pltpu.SemaphoreType.DMA((n,))],
    )(*wholes)


def _exchange_rider(parts, kinds, shapes):
    n = len(parts)

    def plan(ins, outs, sems):
        send_sems, recv_sems, local_sems = sems
        x, y, c = _mesh_pos()
        kme = 2 * x + y
        chips = [(1 - x, y), (x, 1 - y), (1 - x, 1 - y)]

        def slab(a, k):
            if kinds[a] == 'row':
                return ins[a].at[:, k]
            if kinds[a] == 'col':
                cw = shapes[a][2]
                return ins[a].at[:, :, pl.ds(pl.multiple_of(k * cw, 128), cw)]
            return ins[a].at[k]

        cps = [pltpu.make_async_remote_copy(
            src_ref=slab(a, 2 * px + py), dst_ref=outs[a].at[kme], send_sem=send_sems.at[3 * a + j],
            recv_sem=recv_sems.at[3 * a + j], device_id=(px, py, c), device_id_type=MESH)
            for j, (px, py) in enumerate(chips) for a in range(n)]
        return cps + [pltpu.make_async_copy(slab(a, kme), outs[a].at[kme], local_sems.at[a]) for a in range(n)]

    def start(ins, outs, sems):
        for cp in plan(ins, outs, sems):
            cp.start()

    def finish(ins, outs, sems):
        for cp in plan(ins, outs, sems):
            cp.wait()

    return _Rider(
        parts, [jax.ShapeDtypeStruct((4, s[0], s[1] // 2, s[2]), p.dtype) for s, p in zip(shapes, parts)],
        [pltpu.SemaphoreType.DMA((3 * n,)), pltpu.SemaphoreType.DMA((3 * n,)), pltpu.SemaphoreType.DMA((n,))],
        start, finish)


def _rs_share(shards, slabs, name):
    n = len(shards)

    def body(*refs):
        outs = refs[n:2 * n]
        send_sems, recv_sems = refs[2 * n:]
        x, y, c = _mesh_pos()
        cps = []
        for a in range(n):
            l0, l1 = slabs[a]
            rows = outs[a].at[pl.ds(l0, l1 - l0), _half_rows(c, shards[a].shape[1] // 2), :]
            cps.append(pltpu.make_async_remote_copy(
                src_ref=rows, dst_ref=rows, send_sem=send_sems.at[a], recv_sem=recv_sems.at[a],
                device_id=(x, y, 1 - c), device_id_type=MESH))
        for cp in cps:
            cp.start()
        for cp in cps:
            cp.wait()

    return pl.pallas_call(
        body, name=name,
        out_shape=[jax.ShapeDtypeStruct(s.shape, s.dtype) for s in shards],
        in_specs=[pl.BlockSpec(memory_space=pl.ANY)] * n,
        out_specs=[pl.BlockSpec(memory_space=pl.ANY)] * n,
        input_output_aliases={a: a for a in range(n)},
        scratch_shapes=[pltpu.SemaphoreType.DMA((n,)), pltpu.SemaphoreType.DMA((n,))],
    )(*shards)


def _tile_rows(rows, cols, budget=2 * 1024 * 1024):
    for t in (1024, 512, 256, 128, 64, 32, 16, 8):
        if rows % t == 0 and t * cols * 4 <= budget:
            return t
    return rows


def _add_half(g3, r3, c_idx, name):
    a, h, n = r3.shape
    t = _tile_rows(h, n)
    nt = h // t

    def body(c_ref, g_ref, r_ref, o_ref):
        o_ref[...] = (g_ref[...] + r_ref[...]).astype(o_ref.dtype)

    return pl.pallas_call(
        body, name=name, out_shape=jax.ShapeDtypeStruct((a, h, n), BF16),
        grid_spec=pltpu.PrefetchScalarGridSpec(
            num_scalar_prefetch=1, grid=(a, nt),
            in_specs=[pl.BlockSpec((1, t, n), lambda k, i, c: (k, c[0] * nt + i, 0)),
                      pl.BlockSpec((1, t, n), lambda k, i, c: (k, i, 0))],
            out_specs=pl.BlockSpec((1, t, n), lambda k, i, c: (k, i, 0))),
        compiler_params=_cparams(dimension_semantics=("parallel", "parallel")),
    )(c_idx, g3, r3)


def _sum4(parts, c_idx, name, into):
    _, l, h, n = parts.shape
    n_slabs, l0, buf = into
    t = _tile_rows(h, n, 1024 * 1024)
    nt = h // t
    held = () if buf is None else (buf,)

    def body(c_ref, p_ref, *rest):
        pv = p_ref[...].astype(F32)
        rest[-1][...] = ((pv[0] + pv[1]) + pv[2]) + pv[3]

    return pl.pallas_call(
        body, name=name, out_shape=jax.ShapeDtypeStruct((n_slabs, 2 * h, n), F32),
        grid_spec=pltpu.PrefetchScalarGridSpec(
            num_scalar_prefetch=1, grid=(l, nt),
            in_specs=[pl.BlockSpec((4, 1, t, n), lambda k, i, c: (0, k, i, 0))]
            + [pl.BlockSpec(memory_space=pl.ANY)] * len(held),
            out_specs=pl.BlockSpec((1, t, n), lambda k, i, c: (l0 + k, c[0] * nt + i, 0))),
        input_output_aliases={2: 0} if held else {},
        compiler_params=_cparams(dimension_semantics=("parallel", "parallel")),
    )(c_idx, parts, *held)


def _sum8(parts, name):
    _, m, n = parts.shape

    def body(p_ref, o_ref):
        acc = p_ref[0]
        for d in range(1, 8):
            acc = acc + p_ref[d]
        o_ref[...] = acc

    return pl.pallas_call(body, name=name, out_shape=jax.ShapeDtypeStruct((m, n), F32))(parts)


def _adamw(wp, gp, mp, vp, name):
    r, n = wp.shape
    t = _tile_rows(r, n, 1024 * 1024)

    def body(w_ref, g_ref, m_ref, v_ref, d_ref, mo_ref, vo_ref):
        gv = g_ref[...]
        m2 = ADAM_B1 * m_ref[...] + (1.0 - ADAM_B1) * gv
        v2 = ADAM_B2 * v_ref[...] + (1.0 - ADAM_B2) * (gv * gv)
        m_hat = m2 / (1.0 - ADAM_B1 ** ADAM_STEP)
        v_hat = v2 / (1.0 - ADAM_B2 ** ADAM_STEP)
        d_ref[...] = -ADAM_LR * (m_hat / (jnp.sqrt(v_hat) + ADAM_EPS) + ADAM_WD * w_ref[...])
        mo_ref[...] = m2
        vo_ref[...] = v2

    spec = pl.BlockSpec((t, n), lambda i: (i, 0))
    return pl.pallas_call(
        body, name=name, out_shape=[jax.ShapeDtypeStruct((r, n), F32)] * 3, grid=(r // t,),
        in_specs=[spec] * 4, out_specs=[spec] * 3,
        compiler_params=_cparams(dimension_semantics=("parallel",)),
    )(wp, gp, mp, vp)


BIG = (('ssd_w_in', 'colx'), ('ssd_w_out', 'row'), ('mla_w_in', 'row'), ('mla_w_q_b', 'col'),
       ('mla_w_kv_b', 'col'), ('mla_w_out', 'row'), ('mlp_w_up', 'col'), ('mlp_w_down', 'row'))
SMALL_SHARDED = (('meta_tokens', 1), ('ssd_conv_w', 2), ('mla_q_a_norm', 1), ('mla_kv_a_norm', 1))
SMALL_REPL = ('ln_mix', 'ln_mlp', 'ssd_conv_b', 'ssd_dt_bias', 'ssd_a_log', 'ssd_d', 'ssd_norm',
              'mla_q_norm', 'mla_k_norm')
ALL_NAMES = ('meta_tokens', 'ln_mix', 'ln_mlp', 'ssd_w_in', 'ssd_conv_w', 'ssd_conv_b', 'ssd_dt_bias',
             'ssd_a_log', 'ssd_d', 'ssd_norm', 'ssd_w_out', 'mla_w_in', 'mla_q_a_norm', 'mla_w_q_b',
             'mla_kv_a_norm', 'mla_w_kv_b', 'mla_q_norm', 'mla_k_norm', 'mla_w_out', 'mlp_w_up', 'mlp_w_down')


_MLA_BIG = ('mla_w_in', 'mla_w_q_b', 'mla_w_kv_b', 'mla_w_out')
GATHER_ROUNDS = (
    (('ssd_w_in', 0, 1), ('ssd_w_out', 0, 1), ('mlp_w_up', 0, 1), ('mlp_w_down', 0, 1)),
    tuple((n, 0, 1) for n in _MLA_BIG) + (('mlp_w_up', 1, 2), ('mlp_w_down', 1, 2)),
    (('ssd_w_in', 1, 2), ('ssd_w_out', 1, 2)) + tuple((n, 1, 2) for n in _MLA_BIG)
    + (('mlp_w_up', 2, 4), ('mlp_w_down', 2, 4)),
)


REDUCE_ROUNDS = (
    GATHER_ROUNDS[2],
    tuple((n, 0, 1) for n in _MLA_BIG) + (('mlp_w_up', 0, 2), ('mlp_w_down', 0, 2), ('ssd_w_out', 0, 1)),
    (('ssd_w_in', 0, 1),),
)


class _GatheredWeights:
    def __init__(self, shards):
        self.shards, self.p, self.whole = shards, _no_matrices(), {}

    def _round(self, r):
        spec = GATHER_ROUNDS[r]
        return _gather_rider([self.shards[n][l0:l1] for n, l0, l1 in spec], [dict(BIG)[n] for n, _, _ in spec])

    def _take(self, r, outs):
        for (n, l0, l1), o in zip(GATHER_ROUNDS[r], outs):
            kind = dict(BIG)[n]
            for l in range(l0, l1):
                if kind == 'row':
                    m = o[l - l0].reshape(-1, o.shape[-1])
                elif kind == 'col':
                    m = o[l - l0]
                else:
                    m = jnp.concatenate([o[k, l - l0] for k in range(4)], axis=-1)
                self.whole[(n, l)] = m

    def ensure(self, i):
        if i == 0:
            self._take(0, _run_rider(self._round(0), "gather_first"))
        _prep_layer(self.p, i, lambda n: self.whole[(n, _layer_slab(n, i))])

    def rider(self, i):
        return self._round(i + 1) if i + 1 < len(GATHER_ROUNDS) else None

    def deliver(self, i, outs):
        if i + 1 < len(GATHER_ROUNDS):
            self._take(i + 1, outs)


class _ScatterGrads:
    def __init__(self, shard_shapes, c_idx):
        self.shard_shapes, self.c_idx, self.out = shard_shapes, c_idx, {}

    def begin(self, r, grads):
        spec = REDUCE_ROUNDS[r]
        kinds = [dict(BIG)[n] for n, _, _ in spec]
        shapes = [(l1 - l0,) + tuple(self.shard_shapes[n][1:]) for n, l0, l1 in spec]
        wholes = []
        for (n, _, _), kind, s in zip(spec, kinds, shapes):
            if kind == 'row':
                wholes.append(grads[n].reshape(s[0], 4, s[1], s[2]))
            elif kind == 'col':
                wholes.append(grads[n])
            else:
                wholes.append(jnp.stack([grads[n][..., k * s[2]:(k + 1) * s[2]] for k in range(4)]))
        recv = _rs_swap(wholes, kinds, shapes, f"rs_swap{r}")
        parts = []
        for (n, _, _), kind, s, gw, rc in zip(spec, kinds, shapes, wholes, recv):
            if kind == 'col':
                g3, r3 = gw, rc
            else:
                g3, r3 = gw.reshape(-1, s[1], s[2]), rc.reshape(-1, s[1] // 2, s[2])
            parts.append(_add_half(g3, r3, self.c_idx, f"rs_add{r}_{n}").reshape(rc.shape))
        return _exchange_rider(parts, kinds, shapes)

    def finish(self, r, outs):
        spec = REDUCE_ROUNDS[r]
        for (n, l0, _), part in zip(spec, outs):
            self.out[n] = _sum4(part, self.c_idx, f"rs_sum{r}_{n}",
                                into=(self.shard_shapes[n][0], l0, self.out.get(n)))
        shared = _rs_share([self.out[n] for n, _, _ in spec], [(l0, l1) for _, l0, l1 in spec], f"rs_share{r}")
        self.out.update(zip([n for n, _, _ in spec], shared))


def _pack(arrs, rows_mult):
    flat = jnp.concatenate([a.reshape(-1) for a in arrs])
    per = LANES * rows_mult
    pad = (-flat.shape[0]) % per
    if pad:
        flat = jnp.concatenate([flat, jnp.zeros((pad,), flat.dtype)])
    return flat.reshape(-1, LANES)


def _unpack(pack, shapes):
    flat = pack.reshape(-1)
    out, off = [], 0
    for shp in shapes:
        n = math.prod(shp)
        out.append(flat[off:off + n].reshape(shp))
        off += n
    return out


def _split4(full, axis):
    shp = full.shape
    r = full.reshape(shp[:axis] + (4, shp[axis] // 4) + shp[axis + 1:])
    return jnp.moveaxis(r, axis, 0)


def _join4(parts, axis):
    r = jnp.moveaxis(parts, 0, axis)
    shp = r.shape
    return r.reshape(shp[:axis] + (shp[axis] * shp[axis + 1],) + shp[axis + 2:])


def _gather_params(shards, table, dtype, c, name):
    pack = _pack([shards[n].astype(dtype) for n, _ in table], 16)
    half = pack.shape[0] // 2
    mine = lax.dynamic_slice_in_dim(pack, c * half, half, axis=0)
    full = _all_gather8(mine, name).reshape(4, -1)
    out, off = {}, 0
    for n, ax in table:
        cnt = math.prod(shards[n].shape)
        out[n] = _join4(full[:, off:off + cnt].reshape((4,) + shards[n].shape), ax)
        off += cnt
    return out


def kernel(x, meta_tokens, ln_mix, ln_mlp, ssd_w_in, ssd_conv_w, ssd_conv_b, ssd_dt_bias, ssd_a_log, ssd_d, ssd_norm, ssd_w_out, mla_w_in, mla_q_a_norm, mla_w_q_b, mla_kv_a_norm, mla_w_kv_b, mla_q_norm, mla_k_norm, mla_w_out, mlp_w_up, mlp_w_down, loss_target, m_meta_tokens, m_ln_mix, m_ln_mlp, m_ssd_w_in, m_ssd_conv_w, m_ssd_conv_b, m_ssd_dt_bias, m_ssd_a_log, m_ssd_d, m_ssd_norm, m_ssd_w_out, m_mla_w_in, m_mla_q_a_norm, m_mla_w_q_b, m_mla_kv_a_norm, m_mla_w_kv_b, m_mla_q_norm, m_mla_k_norm, m_mla_w_out, m_mlp_w_up, m_mlp_w_down, v_meta_tokens, v_ln_mix, v_ln_mlp, v_ssd_w_in, v_ssd_conv_w, v_ssd_conv_b, v_ssd_dt_bias, v_ssd_a_log, v_ssd_d, v_ssd_norm, v_ssd_w_out, v_mla_w_in, v_mla_q_a_norm, v_mla_w_q_b, v_mla_kv_a_norm, v_mla_w_kv_b, v_mla_q_norm, v_mla_k_norm, v_mla_w_out, v_mlp_w_up, v_mlp_w_down):
    w_sh = dict(meta_tokens=meta_tokens, ln_mix=ln_mix, ln_mlp=ln_mlp, ssd_w_in=ssd_w_in, ssd_conv_w=ssd_conv_w, ssd_conv_b=ssd_conv_b, ssd_dt_bias=ssd_dt_bias, ssd_a_log=ssd_a_log, ssd_d=ssd_d, ssd_norm=ssd_norm, ssd_w_out=ssd_w_out, mla_w_in=mla_w_in, mla_q_a_norm=mla_q_a_norm, mla_w_q_b=mla_w_q_b, mla_kv_a_norm=mla_kv_a_norm, mla_w_kv_b=mla_w_kv_b, mla_q_norm=mla_q_norm, mla_k_norm=mla_k_norm, mla_w_out=mla_w_out, mlp_w_up=mlp_w_up, mlp_w_down=mlp_w_down)
    m_sh = dict(meta_tokens=m_meta_tokens, ln_mix=m_ln_mix, ln_mlp=m_ln_mlp, ssd_w_in=m_ssd_w_in, ssd_conv_w=m_ssd_conv_w, ssd_conv_b=m_ssd_conv_b, ssd_dt_bias=m_ssd_dt_bias, ssd_a_log=m_ssd_a_log, ssd_d=m_ssd_d, ssd_norm=m_ssd_norm, ssd_w_out=m_ssd_w_out, mla_w_in=m_mla_w_in, mla_q_a_norm=m_mla_q_a_norm, mla_w_q_b=m_mla_w_q_b, mla_kv_a_norm=m_mla_kv_a_norm, mla_w_kv_b=m_mla_w_kv_b, mla_q_norm=m_mla_q_norm, mla_k_norm=m_mla_k_norm, mla_w_out=m_mla_w_out, mlp_w_up=m_mlp_w_up, mlp_w_down=m_mlp_w_down)
    v_sh = dict(meta_tokens=v_meta_tokens, ln_mix=v_ln_mix, ln_mlp=v_ln_mlp, ssd_w_in=v_ssd_w_in, ssd_conv_w=v_ssd_conv_w, ssd_conv_b=v_ssd_conv_b, ssd_dt_bias=v_ssd_dt_bias, ssd_a_log=v_ssd_a_log, ssd_d=v_ssd_d, ssd_norm=v_ssd_norm, ssd_w_out=v_ssd_w_out, mla_w_in=v_mla_w_in, mla_q_a_norm=v_mla_q_a_norm, mla_w_q_b=v_mla_w_q_b, mla_kv_a_norm=v_mla_kv_a_norm, mla_w_kv_b=v_mla_w_kv_b, mla_q_norm=v_mla_q_norm, mla_k_norm=v_mla_k_norm, mla_w_out=v_mla_w_out, mlp_w_up=v_mlp_w_up, mlp_w_down=v_mlp_w_down)

    cx, cy, cc = lax.axis_index("x"), lax.axis_index("y"), lax.axis_index("c")
    chip = 2 * cx + cy

    c_idx = cc.reshape(1).astype(jnp.int32)
    big_names = [n for n, _ in BIG]
    shapes = [w_sh[n].shape for n in big_names]

    w = {n: w_sh[n] for n in SMALL_REPL}
    w.update(_gather_params(w_sh, SMALL_SHARDED, F32, cc, "gather_small"))
    big = _GatheredWeights({n: w_sh[n].astype(BF16) for n in big_names})
    red = _ScatterGrads({n: w_sh[n].shape for n in big_names}, c_idx)

    loss_row, grad_x, grads, red = _local_step(x[0], loss_target[0], w, big, red)
    loss = lax.psum(jnp.sum(loss_row), ("x", "y", "c"))
    g_sh = dict(red.out)

    small_names = tuple(n for n, _ in SMALL_SHARDED) + SMALL_REPL
    sp = _pack([grads[n] for n in small_names], 8)
    srows = sp.shape[0]
    s_all = _sum8(_all_gather8(sp, "ar_small_gather").reshape(8, srows, LANES), "ar_small_sum")
    s_full = dict(zip(small_names, _unpack(s_all, [grads[n].shape for n in small_names])))
    for n, ax in SMALL_SHARDED:
        g_sh[n] = lax.dynamic_index_in_dim(_split4(s_full[n], ax), chip, axis=0, keepdims=False)
    for n in SMALL_REPL:
        g_sh[n] = s_full[n]

    delta, new_m, new_v = {}, {}, {}
    for n, s in zip(big_names, shapes):
        res = _adamw(*[t[n].reshape(-1, s[2]) for t in (w_sh, g_sh, m_sh, v_sh)], f"adamw_{n}")
        delta[n], new_m[n], new_v[n] = [r.reshape(s) for r in res]
    d_s, m_s, v_s = _adamw(*[_pack([t[n] for n in small_names], 8) for t in (w_sh, g_sh, m_sh, v_sh)],
                           "adamw_small")
    for dst, ps in ((delta, d_s), (new_m, m_s), (new_v, v_s)):
        dst.update(zip(small_names, _unpack(ps, [w_sh[n].shape for n in small_names])))

    return (loss, grad_x[None], *[g_sh[n] for n in ALL_NAMES], *[delta[n] for n in ALL_NAMES],
            *[new_m[n] for n in ALL_NAMES], *[new_v[n] for n in ALL_NAMES])
```

```python
import functools
import math

import jax
import jax.numpy as jnp
from jax import lax
from jax.experimental import pallas as pl
from jax.experimental.pallas import tpu as pltpu

F32 = jnp.float32
BF16 = jnp.bfloat16
MESH = pl.DeviceIdType.MESH
_NN = (((1,), (0,)), ((), ()))
_NT = (((1,), (1,)), ((), ()))
_TN = (((0,), (0,)), ((), ()))

D_MODEL = 1024
N_META = 16
EPS = 1e-6
SSD_D_INNER = 2048
SSD_HEADS = 32
SSD_HEAD_DIM = 64
SSD_GROUPS = 8
SSD_HPG = 4
SSD_STATE = 128
SSD_CONV = 4
CHUNK = 128
SSD_IN_DIM = 6176
SSD_IN_PAD = 6272
MLA_HEADS = 16
MLA_NOPE = 64
MLA_ROPE = 32
MLA_V = 64
MLA_QK = 96
MLA_Q_RANK = 384
MLA_KV_RANK = 256
HEAD_SLOT = 128
MLA_WIDE = MLA_HEADS * HEAD_SLOT
HEADS_PER_STEP = 2
LAT_PAD = 768
ROPE_THETA = 10000.0
D_FF = 4096
NPAD = CHUNK - N_META
ADAM_LR, ADAM_B1, ADAM_B2, ADAM_EPS, ADAM_WD, ADAM_STEP = 0.001, 0.9, 0.999, 1e-08, 0.01, 10
LANES = 1024
VMEM_LIMIT = 56 * 1024 * 1024


def _pick(n, cands):
    for c in cands:
        if n % c == 0:
            return c
    return n


def _cparams(**kw):
    return pltpu.CompilerParams(vmem_limit_bytes=VMEM_LIMIT, **kw)


def _mm(a, b, dims, *, name, out_dtype=F32, a_fn=None, epi=None, extras=(), stack=None, norm_gain=None):
    if dims == 'nn':
        (M, K), (K2, N) = a.shape, b.shape
    elif dims == 'nt':
        (M, K), (N, K2) = a.shape, b.shape
    else:
        (K, M), (K2, N) = a.shape, b.shape
    assert K == K2, (a.shape, b.shape, dims)
    if dims == 'tn':
        tm = _pick(M, (1024, 768, 512, 384, 256, 128))
        tn = _pick(N, (1024, 896, 768, 512, 384, 256, 128))
        tk = _pick(K, (1408, 1024, 512, 384, 256, 128))
    else:
        tm = _pick(M, (704, 512, 384, 256, 128) if norm_gain is not None else (1408, 1024, 512, 384, 256, 128))
        tn = _pick(N, (1024, 896, 768, 512, 384, 256, 128))
        tk = _pick(K, (1024, 896, 768, 512, 384, 256, 128))
    nk = K // tk
    if dims == 'nn':
        a_spec = pl.BlockSpec((tm, tk), lambda i, j, k: (i, k))
        b_spec = pl.BlockSpec((tk, tn), lambda i, j, k: (k, j))
        dn = (((1,), (0,)), ((), ()))
    elif dims == 'nt':
        a_spec = pl.BlockSpec((tm, tk), lambda i, j, k: (i, k))
        b_spec = pl.BlockSpec((tn, tk), lambda i, j, k: (j, k))
        dn = (((1,), (1,)), ((), ()))
    else:
        a_spec = pl.BlockSpec((tk, tm), lambda i, j, k: (k, i))
        b_spec = pl.BlockSpec((tk, tn), lambda i, j, k: (k, j))
        dn = (((0,), (0,)), ((), ()))
    o_spec = pl.BlockSpec((tm, tn), lambda i, j, k: (i, j))
    n_ex = len(extras)
    out_shape = jax.ShapeDtypeStruct((M, N), out_dtype)
    out_spec, held, aliases = o_spec, (), {}
    if stack is not None:
        n_slabs, slab, buf = stack
        out_shape = jax.ShapeDtypeStruct((n_slabs, M, N), out_dtype)
        out_spec = pl.BlockSpec((None, tm, tn), lambda i, j, k: (slab, i, j))
        if buf is not None:
            held, aliases = (buf,), {2 + n_ex: 0}

    gains = ()
    if norm_gain is not None:
        assert tn == N and stack is None, "the rms epilogue needs whole rows"
        gains = (norm_gain,)
        out_shape = [out_shape, jax.ShapeDtypeStruct((M, N), BF16)]
        out_spec = [out_spec, o_spec]

    def body(a_ref, b_ref, *rest):
        ex_refs, rest = rest[:n_ex], rest[n_ex:]
        g_refs, rest = rest[:len(gains)], rest[len(gains) + len(held):]
        o_ref, acc = rest[0], rest[-1]
        k = pl.program_id(2)

        @pl.when(k == 0)
        def _():
            acc[...] = jnp.zeros_like(acc)

        av = a_ref[...]
        if a_fn is not None:
            av = a_fn(av)
        acc[...] += lax.dot_general(av.astype(BF16), b_ref[...].astype(BF16), dn,
                                    preferred_element_type=F32)

        @pl.when(k == nk - 1)
        def _():
            r = acc[...]
            if epi is not None:
                r = epi(r, *[e[...] for e in ex_refs])
            o_ref[...] = r.astype(out_dtype)
            if gains:
                rest[1][...] = _rms(r, g_refs[0][...]).astype(BF16)

    return pl.pallas_call(
        body, name=name,
        out_shape=out_shape,
        grid=(M // tm, N // tn, nk),
        in_specs=[a_spec, b_spec] + [o_spec] * n_ex
        + [pl.BlockSpec((1, tn), lambda i, j, k: (0, j))] * len(gains)
        + [pl.BlockSpec(memory_space=pl.ANY)] * len(held),
        out_specs=out_spec,
        input_output_aliases=aliases,
        scratch_shapes=[pltpu.VMEM((tm, tn), F32)],
        compiler_params=_cparams(dimension_semantics=("parallel", "parallel", "arbitrary")),
    )(a, b, *extras, *gains, *held)


def _mm_rms_bwd(cot, w_t, h, d_res, gain, name):
    (M, K), (N, _) = cot.shape, w_t.shape
    tm = _pick(M, (704, 512, 384, 256, 128))
    tk = _pick(K, (1024, 896, 768, 512, 384, 256, 128))
    nk = K // tk

    def body(a_ref, b_ref, h_ref, r_ref, g_ref, dh_ref, dg_ref, acc):
        i, k = pl.program_id(0), pl.program_id(1)

        @pl.when(k == 0)
        def _():
            acc[...] = jnp.zeros_like(acc)

        acc[...] += lax.dot_general(a_ref[...].astype(BF16), b_ref[...].astype(BF16), _NT,
                                    preferred_element_type=F32)

        @pl.when(k == nk - 1)
        def _():
            _, vjp = jax.vjp(_rms, h_ref[...], g_ref[...])
            dh, dg = vjp(acc[...])
            dh_ref[...] = (r_ref[...] + dh) * _row_mask(i, tm)

            @pl.when(i == 0)
            def _():
                dg_ref[...] = dg

            @pl.when(i > 0)
            def _():
                dg_ref[...] += dg

    rows = pl.BlockSpec((tm, N), lambda i, k: (i, 0))
    vec = pl.BlockSpec((1, N), lambda i, k: (0, 0))
    return pl.pallas_call(
        body, name=name,
        out_shape=[jax.ShapeDtypeStruct((M, N), F32), jax.ShapeDtypeStruct((1, N), F32)],
        grid=(M // tm, nk),
        in_specs=[pl.BlockSpec((tm, tk), lambda i, k: (i, k)), pl.BlockSpec((N, tk), lambda i, k: (0, k)),
                  rows, rows, vec],
        out_specs=[rows, vec],
        scratch_shapes=[pltpu.VMEM((tm, N), F32)],
        compiler_params=_cparams(dimension_semantics=("arbitrary", "arbitrary")),
    )(cot, w_t, h, d_res, gain)


def _mm_attn_do(dh, w_out_t, o, name):
    (M, K), (N, _) = dh.shape, w_out_t.shape
    tm = _pick(M, (704, 512, 384, 256, 128))
    tn = 8 * HEAD_SLOT

    def body(a_ref, b_ref, o_ref, dob_ref, delta_ref):
        do = lax.dot_general(a_ref[...].astype(BF16), b_ref[...], _NT, preferred_element_type=F32)
        dob_ref[...] = do.astype(BF16)
        for hh in range(tn // HEAD_SLOT):
            sl = slice(hh * HEAD_SLOT, (hh + 1) * HEAD_SLOT)
            delta_ref[hh] = jnp.sum(do[:, sl] * o_ref[:, sl], axis=-1, keepdims=True)

    tile = pl.BlockSpec((tm, tn), lambda i, j: (i, j))
    return pl.pallas_call(
        body, name=name,
        out_shape=[jax.ShapeDtypeStruct((M, N), BF16), jax.ShapeDtypeStruct((N // HEAD_SLOT, M, 1), F32)],
        grid=(M // tm, N // tn),
        in_specs=[pl.BlockSpec((tm, K), lambda i, j: (i, 0)), pl.BlockSpec((tn, K), lambda i, j: (j, 0)), tile],
        out_specs=[tile, pl.BlockSpec((tn // HEAD_SLOT, tm, 1), lambda i, j: (j, i, 0))],
        compiler_params=_cparams(dimension_semantics=("parallel", "parallel")),
    )(dh, w_out_t, o)


def _row_call(fn, rows, consts, out_rows, out_accs=(), *, n_rows, tile, name):
    n_r, n_c, n_o, n_a = len(rows), len(consts), len(out_rows), len(out_accs)
    steps = n_rows // tile

    def body(*refs):
        r_refs = refs[:n_r]
        c_refs = refs[n_r:n_r + n_c]
        o_refs = refs[n_r + n_c:n_r + n_c + n_o]
        a_refs = refs[n_r + n_c + n_o:]
        i = pl.program_id(0)
        res = fn(i, *[r[...] for r in r_refs], *[c[...] for c in c_refs])
        for o_ref, val in zip(o_refs, res[:n_o]):
            o_ref[...] = val.astype(o_ref.dtype)

        @pl.when(i == 0)
        def _():
            for a_ref in a_refs:
                a_ref[...] = jnp.zeros_like(a_ref)

        for a_ref, val in zip(a_refs, res[n_o:]):
            a_ref[...] += val

    in_specs = [pl.BlockSpec((tile, w), functools.partial(lambda i, cb: (i, cb), cb=cb))
                for (_, w, cb) in rows]
    in_specs += [pl.BlockSpec(c.shape, lambda i: (0, 0)) for c in consts]
    out_specs = [pl.BlockSpec((tile, c), lambda i: (i, 0)) for (c, _) in out_rows]
    out_specs += [pl.BlockSpec(s, lambda i: (0, 0)) for s in out_accs]
    out_shape = [jax.ShapeDtypeStruct((n_rows, c), dt) for (c, dt) in out_rows]
    out_shape += [jax.ShapeDtypeStruct(s, F32) for s in out_accs]
    return pl.pallas_call(
        body, name=name, out_shape=out_shape, grid=(steps,),
        in_specs=in_specs, out_specs=out_specs,
        compiler_params=_cparams(dimension_semantics=("arbitrary",)),
    )(*[r[0] for r in rows], *consts)


def _row_mask(i, tile):
    r = i * tile + lax.broadcasted_iota(jnp.int32, (tile, 1), 0)
    return (r >= NPAD).astype(F32)


def _rms(x, g):
    return x * lax.rsqrt(jnp.mean(x * x, axis=-1, keepdims=True) + EPS) * g


def _silu(x):
    return x * (0.5 * jnp.tanh(0.5 * x) + 0.5)


def _softplus(x):
    return jnp.maximum(x, 0.0) + jnp.log(1.0 + jnp.exp(-jnp.abs(x)))


def _rms_fwd(h, g, name):
    lp = h.shape[0]
    return _row_call(lambda i, hv, gv: (_rms(hv, gv),), [(h, D_MODEL, 0)], [g],
                     [(D_MODEL, BF16)], n_rows=lp, tile=_pick(lp, (384, 256, 128)), name=name)[0]


def _rms_bwd(h, g, d_hn, d_res, name):
    lp = h.shape[0]
    tile = _pick(lp, (384, 256, 128))

    def fn(i, hv, dv, rv, gv):
        _, vjp = jax.vjp(_rms, hv, gv)
        dh, dg = vjp(dv)
        return (rv + dh) * _row_mask(i, tile), dg

    return _row_call(fn, [(h, D_MODEL, 0), (d_hn, D_MODEL, 0), (d_res, D_MODEL, 0)], [g],
                     [(D_MODEL, F32)], [(1, D_MODEL)], n_rows=lp, tile=tile, name=name)


@functools.partial(jax.custom_vjp, nondiff_argnums=(1,))
def _roll_rows(x, s):
    return pltpu.roll(x, s, 0)


def _roll_rows_fwd(x, s):
    return pltpu.roll(x, s, 0), None


def _roll_rows_bwd(s, _, ct):
    return (pltpu.roll(ct, (ct.shape[0] - s) % ct.shape[0], 0),)


_roll_rows.defvjp(_roll_rows_fwd, _roll_rows_bwd)


def _conv_silu(cur, halo, w_rows, b):
    full = jnp.concatenate([halo, cur], axis=0)
    acc = cur * w_rows[SSD_CONV - 1] + b
    for k in range(SSD_CONV - 1):
        acc = acc + _roll_rows(full, SSD_CONV - 1 - k)[8:] * w_rows[k]
    return _silu(acc)


def _split3(v):
    hi = v.astype(BF16)
    r1 = v - hi.astype(F32)
    mid = r1.astype(BF16)
    lo = (r1 - mid.astype(F32)).astype(BF16)
    return hi, mid, lo


def _select_right(v, sel, dn):
    return sum(lax.dot_general(p, sel, dn, preferred_element_type=F32) for p in _split3(v))


@jax.custom_vjp
def _expand_heads(v, e_mat):
    return _select_right(v, e_mat, _NN)


def _expand_heads_fwd(v, e_mat):
    return _select_right(v, e_mat, _NN), e_mat


def _expand_heads_bwd(e_mat, ct):
    return _select_right(ct, e_mat, _NT), jnp.zeros_like(e_mat)


_expand_heads.defvjp(_expand_heads_fwd, _expand_heads_bwd)


@jax.custom_vjp
def _cumsum_rows(a, tri):
    return sum(lax.dot_general(tri, p, _NN, preferred_element_type=F32) for p in _split3(a))


def _cumsum_rows_fwd(a, tri):
    return _cumsum_rows(a, tri), tri


def _cumsum_rows_bwd(tri, ct):
    return (sum(lax.dot_general(tri, p, _TN, preferred_element_type=F32) for p in _split3(ct)),
            jnp.zeros_like(tri))


_cumsum_rows.defvjp(_cumsum_rows_fwd, _cumsum_rows_bwd)


def _ssd_chunk(mask, z, xs_pre, bc_pre, halo_x, halo_bc, dt_pre, st, cwx0, cwx1, cwx2, cwx3,
               cwb0, cwb1, cwb2, cwb3, cb_x, cb_bc, dtb, alog, dsk, ng):
    L = CHUNK
    lane_h = lax.broadcasted_iota(jnp.int32, (1, 128), 1)
    head_ok = (lane_h < SSD_HEADS).astype(F32)
    e_mat = (lax.broadcasted_iota(jnp.int32, (128, SSD_D_INNER), 1) // SSD_HEAD_DIM
             == lax.broadcasted_iota(jnp.int32, (128, SSD_D_INNER), 0)).astype(BF16)
    ri = lax.broadcasted_iota(jnp.int32, (L, L), 0)
    ci = lax.broadcasted_iota(jnp.int32, (L, L), 1)
    causal = ri >= ci

    xs = _conv_silu(xs_pre, halo_x, (cwx0, cwx1, cwx2, cwx3), cb_x) * mask
    bc = _conv_silu(bc_pre, halo_bc, (cwb0, cwb1, cwb2, cwb3), cb_bc) * mask
    dt = _softplus(dt_pre + dtb) * mask * head_ok
    a_dt = dt * (-jnp.exp(alog))
    a_cs = _cumsum_rows(a_dt, causal.astype(BF16))
    a_cs_t = a_cs.T
    row8 = lax.broadcasted_iota(jnp.int32, (8, 128), 0)
    last8 = jnp.where(row8 == 0, jnp.sum(a_dt, axis=0, keepdims=True), 0.0)
    dsk8 = jnp.where(row8 == 0, dsk, 0.0)
    wide = _expand_heads(jnp.concatenate([dt, a_cs, last8, dsk8], axis=0), e_mat)
    dt_e, acs_e = wide[0:L], wide[L:2 * L]
    last_e = jnp.sum(wide[2 * L:2 * L + 8], axis=0, keepdims=True)
    d_e = jnp.sum(wide[2 * L + 8:2 * L + 16], axis=0, keepdims=True)
    xdt = xs * dt_e
    dte_e = jnp.exp(last_e - acs_e)
    dfs_e = jnp.exp(acs_e)
    cd_e = jnp.exp(last_e)
    sub_h = lax.broadcasted_iota(jnp.int32, (128, L), 0)
    lane_hl = lax.broadcasted_iota(jnp.int32, (L, 128), 1)
    lane_g = lax.broadcasted_iota(jnp.int32, (1, SSD_HPG * SSD_HEAD_DIM), 1) // SSD_HEAD_DIM

    ys, new_st = [], []
    for g in range(SSD_GROUPS):
        b_g = bc[:, g * 128:(g + 1) * 128].astype(BF16)
        c_g = bc[:, 1024 + g * 128:1024 + (g + 1) * 128].astype(BF16)
        gs = slice(g * 256, (g + 1) * 256)
        xdt_g = xdt[:, gs]
        cb = lax.dot_general(c_g, b_g, (((1,), (1,)), ((), ())), preferred_element_type=F32)
        st_g = st[g * 128:(g + 1) * 128, :]
        y_g = lax.dot_general(c_g, st_g.astype(BF16), (((1,), (0,)), ((), ())),
                              preferred_element_type=F32) * dfs_e[:, gs]
        for j in range(SSD_HPG):
            h = g * SSD_HPG + j
            col = jnp.sum(jnp.where(lane_hl == h, a_cs, 0.0), axis=1, keepdims=True)
            row = jnp.sum(jnp.where(sub_h == h, a_cs_t, 0.0), axis=0, keepdims=True)
            dec = jnp.where(causal, jnp.exp(jnp.where(causal, col - row, 0.0)), 0.0)
            m_h = (cb * dec).astype(BF16)
            x_h = jnp.where(lane_g == j, xdt_g, 0.0).astype(BF16)
            y_g = y_g + lax.dot_general(m_h, x_h, (((1,), (0,)), ((), ())),
                                        preferred_element_type=F32)
        s_new = lax.dot_general(b_g, (xdt_g * dte_e[:, gs]).astype(BF16), (((0,), (0,)), ((), ())),
                                preferred_element_type=F32)
        new_st.append(st_g * cd_e[:, gs] + s_new)
        ys.append(y_g)
    y = jnp.concatenate(ys, axis=1) + xs * d_e
    gg = y * _silu(z)
    outs = []
    for g in range(SSD_GROUPS):
        sl = gg[:, g * 256:(g + 1) * 256]
        outs.append(sl * lax.rsqrt(jnp.mean(sl * sl, axis=-1, keepdims=True) + EPS))
    out = jnp.concatenate(outs, axis=1) * ng
    return out, jnp.concatenate(new_st, axis=0)


def _ssd_consts(conv_w, conv_b, dtb, alog, dsk, ng):
    return [conv_w, conv_b, dtb, alog, dsk, ng]


def _ssd_param_vals(cw_ref, cb_ref, dtb_ref, alog_ref, dsk_ref, ng_ref):
    cwx = [cw_ref[k:k + 1, 0:SSD_D_INNER] for k in range(SSD_CONV)]
    cwb = [cw_ref[k:k + 1, SSD_D_INNER:2 * SSD_D_INNER] for k in range(SSD_CONV)]
    return (*cwx, *cwb, cb_ref[:, 0:SSD_D_INNER], cb_ref[:, SSD_D_INNER:2 * SSD_D_INNER],
            dtb_ref[...], alog_ref[...], dsk_ref[...], ng_ref[...])


def _ssd_in_specs(rev, nc):
    def cidx(i):
        return (nc - 1 - i) if rev else i

    def halo(cb):
        return pl.BlockSpec((8, SSD_D_INNER), lambda i: (jnp.maximum(16 * cidx(i) - 1, 0), cb))

    return [
        pl.BlockSpec((CHUNK, SSD_D_INNER), lambda i: (cidx(i), 0)),
        pl.BlockSpec((CHUNK, SSD_D_INNER), lambda i: (cidx(i), 1)),
        pl.BlockSpec((CHUNK, SSD_D_INNER), lambda i: (cidx(i), 2)),
        halo(1), halo(2),
        pl.BlockSpec((CHUNK, 128), lambda i: (cidx(i), 48)),
    ]


class _Rider:
    def __init__(self, operands, out_shapes, scratch, start, finish):
        self.operands, self.out_shapes, self.scratch = list(operands), list(out_shapes), list(scratch)
        self.start, self.finish = start, finish


def _rider_split(rider, refs, n_in, n_out, n_scratch):
    if rider is None:
        return refs, None
    ni, no = len(rider.operands), len(rider.out_shapes)
    own = refs[:n_in] + refs[n_in + ni:n_in + ni + n_out] + refs[n_in + ni + n_out + no:n_in + ni + n_out + no + n_scratch]
    mine = (refs[n_in:n_in + ni], refs[n_in + ni + n_out:n_in + ni + n_out + no],
            refs[n_in + ni + n_out + no + n_scratch:])
    return own, mine


def _rider_args(rider):
    if rider is None:
        return [], [], [], []
    hbm = pl.BlockSpec(memory_space=pl.ANY)
    return ([hbm] * len(rider.operands), [hbm] * len(rider.out_shapes), rider.out_shapes, rider.scratch)


def _ssd_fwd(zxd, consts, name, rider=None):
    lp = zxd.shape[0]
    nc = lp // CHUNK

    def body(*refs):
        own, ride = _rider_split(rider, refs, 12, 2, 1)
        (z_ref, xs_ref, bc_ref, hx_ref, hb_ref, dt_ref, cw_ref, cb_ref, dtb_ref, alog_ref,
         dsk_ref, ng_ref, y_ref, st_ref, state) = own
        c = pl.program_id(0)

        @pl.when(c == 0)
        def _():
            state[...] = jnp.zeros_like(state)
            if ride is not None:
                rider.start(*ride)

        live = (c > 0).astype(F32)
        st_ref[0] = state[...]
        out, st_new = _ssd_chunk(
            _row_mask(c, CHUNK), z_ref[...], xs_ref[...], bc_ref[...], hx_ref[...] * live,
            hb_ref[...] * live, dt_ref[...], state[...],
            *_ssd_param_vals(cw_ref, cb_ref, dtb_ref, alog_ref, dsk_ref, ng_ref))
        y_ref[...] = out.astype(y_ref.dtype)
        state[...] = st_new

        if ride is not None:
            @pl.when(c == nc - 1)
            def _():
                rider.finish(*ride)

    r_in, r_out, r_shapes, r_scratch = _rider_args(rider)
    return pl.pallas_call(
        body, name=name,
        out_shape=[jax.ShapeDtypeStruct((lp, SSD_D_INNER), BF16),
                   jax.ShapeDtypeStruct((nc, SSD_GROUPS * SSD_STATE, 256), F32)] + r_shapes,
        grid=(nc,),
        in_specs=_ssd_in_specs(False, nc) + [pl.BlockSpec(c.shape, lambda i: (0, 0)) for c in consts] + r_in,
        out_specs=[pl.BlockSpec((CHUNK, SSD_D_INNER), lambda i: (i, 0)),
                   pl.BlockSpec((1, SSD_GROUPS * SSD_STATE, 256), lambda i: (i, 0, 0))] + r_out,
        scratch_shapes=[pltpu.VMEM((SSD_GROUPS * SSD_STATE, 256), F32)] + r_scratch,
        compiler_params=_cparams(dimension_semantics=("arbitrary",)),
    )(zxd, zxd, zxd, zxd, zxd, zxd, *consts, *(rider.operands if rider else ()))


def _ssd_bwd(zxd, states, d_y, consts, name, rider=None):
    lp = zxd.shape[0]
    nc = lp // CHUNK

    def body(*refs):
        own, ride = _rider_split(rider, refs, 14, 7, 3)
        (z_ref, xs_ref, bc_ref, hx_ref, hb_ref, dt_ref, st_ref, dy_ref, cw_ref, cb_ref, dtb_ref,
         alog_ref, dsk_ref, ng_ref, dz_ref, dcw_ref, dcb_ref, ddtb_ref, dalog_ref, ddsk_ref,
         dng_ref, d_state, d_hx, d_hb) = own
        i = pl.program_id(0)
        c = nc - 1 - i

        @pl.when(i == 0)
        def _():
            d_state[...] = jnp.zeros_like(d_state)
            d_hx[...] = jnp.zeros_like(d_hx)
            d_hb[...] = jnp.zeros_like(d_hb)
            for r in (dcw_ref, dcb_ref, ddtb_ref, dalog_ref, ddsk_ref, dng_ref):
                r[...] = jnp.zeros_like(r)
            if ride is not None:
                rider.start(*ride)

        live = (c > 0).astype(F32)
        fn = functools.partial(_ssd_chunk, _row_mask(c, CHUNK))
        prim = (z_ref[...], xs_ref[...], bc_ref[...], hx_ref[...] * live, hb_ref[...] * live,
                dt_ref[...], st_ref[0],
                *_ssd_param_vals(cw_ref, cb_ref, dtb_ref, alog_ref, dsk_ref, ng_ref))
        _, vjp = jax.vjp(fn, *prim)
        (d_z, d_xs, d_bc, g_hx, g_hb, d_dt, g_st, *d_par) = vjp((dy_ref[...], d_state[...]))
        zeros = jnp.zeros((CHUNK - 8, SSD_D_INNER), F32)
        d_xs = d_xs + jnp.concatenate([zeros, d_hx[...]], axis=0)
        d_bc = d_bc + jnp.concatenate([zeros, d_hb[...]], axis=0)
        dz_ref[:, 0:SSD_D_INNER] = d_z.astype(dz_ref.dtype)
        dz_ref[:, SSD_D_INNER:2 * SSD_D_INNER] = d_xs.astype(dz_ref.dtype)
        dz_ref[:, 2 * SSD_D_INNER:3 * SSD_D_INNER] = d_bc.astype(dz_ref.dtype)
        dz_ref[:, 3 * SSD_D_INNER:] = d_dt.astype(dz_ref.dtype)
        d_state[...] = g_st
        d_hx[...] = g_hx * live
        d_hb[...] = g_hb * live
        for k in range(SSD_CONV):
            dcw_ref[k:k + 1, 0:SSD_D_INNER] += d_par[k]
            dcw_ref[k:k + 1, SSD_D_INNER:2 * SSD_D_INNER] += d_par[SSD_CONV + k]
        dcb_ref[:, 0:SSD_D_INNER] += d_par[8]
        dcb_ref[:, SSD_D_INNER:2 * SSD_D_INNER] += d_par[9]
        ddtb_ref[...] += d_par[10]
        dalog_ref[...] += d_par[11]
        ddsk_ref[...] += d_par[12]
        dng_ref[...] += d_par[13]

        if ride is not None:
            @pl.when(i == nc - 1)
            def _():
                rider.finish(*ride)

    const_specs = [pl.BlockSpec(c.shape, lambda i: (0, 0)) for c in consts]
    r_in, r_out, r_shapes, r_scratch = _rider_args(rider)
    return pl.pallas_call(
        body, name=name,
        out_shape=[jax.ShapeDtypeStruct((lp, SSD_IN_PAD), BF16)]
        + [jax.ShapeDtypeStruct(c.shape, F32) for c in consts] + r_shapes,
        grid=(nc,),
        in_specs=_ssd_in_specs(True, nc)
        + [pl.BlockSpec((1, SSD_GROUPS * SSD_STATE, 256), lambda i: (nc - 1 - i, 0, 0)),
           pl.BlockSpec((CHUNK, SSD_D_INNER), lambda i: (nc - 1 - i, 0))] + const_specs + r_in,
        out_specs=[pl.BlockSpec((CHUNK, SSD_IN_PAD), lambda i: (nc - 1 - i, 0))] + const_specs + r_out,
        scratch_shapes=[pltpu.VMEM((SSD_GROUPS * SSD_STATE, 256), F32),
                        pltpu.VMEM((8, SSD_D_INNER), F32), pltpu.VMEM((8, SSD_D_INNER), F32)] + r_scratch,
        compiler_params=_cparams(dimension_semantics=("arbitrary",)),
    )(zxd, zxd, zxd, zxd, zxd, zxd, states, d_y, *consts, *(rider.operands if rider else ()))


@jax.custom_vjp
def _rot_half(x):
    lane = lax.broadcasted_iota(jnp.int32, x.shape, 1)
    lo = (lane >= MLA_NOPE) & (lane < MLA_NOPE + MLA_ROPE // 2)
    hi = (lane >= MLA_NOPE + MLA_ROPE // 2) & (lane < MLA_QK)
    down = pltpu.roll(x, HEAD_SLOT - MLA_ROPE // 2, 1)
    up = pltpu.roll(x, MLA_ROPE // 2, 1)
    return jnp.where(lo, -down, jnp.where(hi, up, 0.0))


def _rot_half_fwd(x):
    return _rot_half(x), None


def _rot_half_bwd(_, ct):
    return (-_rot_half(ct),)


_rot_half.defvjp(_rot_half_fwd, _rot_half_bwd)


def _head_norm_rope(t, gain, cos, sin):
    n = t * lax.rsqrt(jnp.sum(t * t, axis=-1, keepdims=True) * (1.0 / MLA_QK) + EPS) * gain
    return n * cos + _rot_half(n) * sin


def _qk_prep(q_raw, kn_raw, kpe, cos, sin, qg, kg):
    qs, ks = [], []
    for h in range(MLA_HEADS):
        sl = slice(h * HEAD_SLOT, (h + 1) * HEAD_SLOT)
        qs.append(_head_norm_rope(q_raw[:, sl], qg, cos, sin))
        ks.append(_head_norm_rope(kn_raw[:, sl] + kpe, kg, cos, sin))
    return jnp.concatenate(qs, axis=1), jnp.concatenate(ks, axis=1)


def _lat_norm(kv_lat, q_lat, kvg, qg):
    return _rms(kv_lat, kvg), _rms(q_lat, qg)


_NEG = -1e30
_SCALE = MLA_QK ** -0.5


STRIP = 128
_EXP2_SCALE = _SCALE * math.log2(math.e)


def _strip_mask(kind, blk, c, t):
    if kind is None:
        return None
    kpos = blk * t + c * STRIP + lax.broadcasted_iota(jnp.int32, (1, STRIP), 1)
    if kind == 'keys':
        return kpos >= NPAD
    qpos = blk * t + lax.broadcasted_iota(jnp.int32, (t, 1), 0)
    return (kpos <= qpos) & ((kpos >= NPAD) | (kpos == qpos))


def _attn_fwd(q, k, v, name, rider=None):
    lp = q.shape[0]
    t = tk = _pick(lp, (384, 256, 128))
    nb = lp // t
    hp = HEADS_PER_STEP
    wide = hp * HEAD_SLOT
    heads = [slice(a * HEAD_SLOT, (a + 1) * HEAD_SLOT) for a in range(hp)]

    def body(*refs):
        (q_ref, k_ref, v_ref, o_ref, lse_ref), ride = _rider_split(rider, refs, 3, 2, 0)
        qi = pl.program_id(1)
        if ride is not None:
            @pl.when((pl.program_id(0) == 0) & (qi == 0))
            def _():
                rider.start(*ride)

        def update(a, ki, carry, mask):
            rows = pl.ds(pl.multiple_of(ki * tk, tk), tk)
            m, acc = carry
            s = lax.dot_general(q_ref[:, heads[a]], k_ref[rows, heads[a]], _NT, preferred_element_type=F32)
            if mask is not None:
                s = jnp.where(mask, s, _NEG)
            m_new = jnp.maximum(m, jnp.max(s, axis=-1, keepdims=True))
            alpha = jnp.exp2((m - m_new) * _EXP2_SCALE)
            p = jnp.exp2((s - m_new) * _EXP2_SCALE)
            acc = alpha * acc + lax.dot_general(p.astype(BF16), v_ref[rows, heads[a]], _NN,
                                                preferred_element_type=F32)
            return m_new, acc

        def step(ki, carry, mask):
            return tuple(update(a, ki, carry[a], mask) for a in range(hp))

        init = (jnp.full((t, 1), _NEG, F32), jnp.zeros((t, HEAD_SLOT), F32))
        ones_lane = lax.broadcasted_iota(jnp.int32, (1, HEAD_SLOT), 1) == MLA_V
        key_ok = lax.broadcasted_iota(jnp.int32, (1, tk), 1) >= NPAD
        n_full = (qi * t) // tk
        carry = lax.cond(n_full > 0, lambda c: step(0, c, key_ok), lambda c: c, (init,) * hp)
        carry = lax.fori_loop(1, n_full, lambda ki, c: step(ki, c, None), carry)
        qpos = qi * t + lax.broadcasted_iota(jnp.int32, (t, tk), 0)
        kpos = n_full * tk + lax.broadcasted_iota(jnp.int32, (t, tk), 1)
        carry = step(n_full, carry, (kpos <= qpos) & ((kpos >= NPAD) | (kpos == qpos)))
        for a in range(hp):
            m, acc = carry[a]
            l = jnp.sum(jnp.where(ones_lane, acc, 0.0), axis=-1, keepdims=True)
            o_ref[:, heads[a]] = jnp.where(ones_lane, 0.0, acc / l * _row_mask(qi, t))
            lse_ref[a] = m * _SCALE + jnp.log(l)

        if ride is not None:
            @pl.when((pl.program_id(0) == MLA_HEADS // hp - 1) & (qi == nb - 1))
            def _():
                rider.finish(*ride)

    qspec = pl.BlockSpec((t, wide), lambda g, i: (i, g))
    kspec = pl.BlockSpec((lp, wide), lambda g, i: (0, g))
    r_in, r_out, r_shapes, r_scratch = _rider_args(rider)
    return pl.pallas_call(
        body, name=name,
        out_shape=[jax.ShapeDtypeStruct((lp, MLA_WIDE), F32),
                   jax.ShapeDtypeStruct((MLA_HEADS, lp, 1), F32)] + r_shapes,
        grid=(MLA_HEADS // hp, nb),
        in_specs=[qspec, kspec, kspec] + r_in,
        out_specs=[qspec, pl.BlockSpec((hp, t, 1), lambda g, i: (g, i, 0))] + r_out,
        scratch_shapes=r_scratch,
        compiler_params=_cparams(dimension_semantics=("arbitrary", "arbitrary")),
    )(q, k, v, *(rider.operands if rider else ()))


def _attn_delta(do, o, name):
    lp = do.shape[0]
    t = _pick(lp, (384, 256, 128))

    def body(do_ref, o_ref, dob_ref, delta_ref):
        dob_ref[...] = do_ref[...].astype(BF16)
        for h in range(MLA_HEADS):
            sl = slice(h * HEAD_SLOT, (h + 1) * HEAD_SLOT)
            delta_ref[h] = jnp.sum(do_ref[:, sl] * o_ref[:, sl], axis=-1, keepdims=True)

    spec = pl.BlockSpec((t, MLA_WIDE), lambda i: (i, 0))
    return pl.pallas_call(
        body, name=name,
        out_shape=[jax.ShapeDtypeStruct((lp, MLA_WIDE), BF16), jax.ShapeDtypeStruct((MLA_HEADS, lp, 1), F32)],
        grid=(lp // t,), in_specs=[spec, spec],
        out_specs=[spec, pl.BlockSpec((MLA_HEADS, t, 1), lambda i: (0, i, 0))],
        compiler_params=_cparams(dimension_semantics=("parallel",)),
    )(do, o)


def _attn_bwd(q, k, v, do, lse, delta, name, rider=None):
    lp = q.shape[0]
    t = _pick(lp, (384, 256, 128))
    nb = lp // t
    ns = t // STRIP
    hp = HEADS_PER_STEP
    wide = hp * HEAD_SLOT
    heads = [slice(a * HEAD_SLOT, (a + 1) * HEAD_SLOT) for a in range(hp)]
    log2e = math.log2(math.e)

    def body(*refs):
        own, ride = _rider_split(rider, refs, 6, 3, 4)
        (q_ref, k_ref, v_ref, do_ref, lse_ref, delta_ref, dq_ref, dk_ref, dv_ref,
         s_scr, dp_scr, p_scr, ds_scr) = own
        kj = pl.program_id(1)
        if ride is not None:
            @pl.when((pl.program_id(0) == 0) & (kj == 0))
            def _():
                rider.start(*ride)

        @pl.when(kj == 0)
        def _():
            dq_ref[...] = jnp.zeros_like(dq_ref)

        dk_ref[...] = jnp.zeros_like(dk_ref)
        dv_ref[...] = jnp.zeros_like(dv_ref)

        def tile(qi, kind):
            rows = pl.ds(pl.multiple_of(qi * t, t), t)
            for a in range(hp):
                qb, dob = q_ref[rows, heads[a]], do_ref[rows, heads[a]]
                kb, vb = k_ref[:, heads[a]], v_ref[:, heads[a]]
                s_scr[a] = lax.dot_general(qb, kb, _NT, preferred_element_type=F32)
                dp_scr[a] = lax.dot_general(dob, vb, _NT, preferred_element_type=F32)
                lse2 = lse_ref[a, rows, :] * log2e
                delta = delta_ref[a, rows, :]
                for c in range(ns):
                    cs = slice(c * STRIP, (c + 1) * STRIP)
                    pc = jnp.exp2(s_scr[a, :, cs] * _EXP2_SCALE - lse2)
                    pc = jnp.where(_strip_mask(kind, kj, c, t), pc, 0.0)
                    p_scr[a, :, cs] = pc.astype(BF16)
                    ds_scr[a, :, cs] = (pc * (dp_scr[a, :, cs] - delta)).astype(BF16)
                dq_ref[rows, heads[a]] += lax.dot_general(ds_scr[a], kb, _NN,
                                                          preferred_element_type=F32) * _SCALE
                dv_ref[:, heads[a]] += lax.dot_general(p_scr[a], dob, _TN, preferred_element_type=F32)
                dk_ref[:, heads[a]] += lax.dot_general(ds_scr[a], qb, _TN, preferred_element_type=F32)

        tile(kj, 'diag')

        def below(qi, carry):
            tile(qi, 'keys')
            return carry

        lax.fori_loop(kj + 1, nb, below, 0)
        dk_ref[...] = dk_ref[...] * _SCALE

        if ride is not None:
            @pl.when((pl.program_id(0) == MLA_HEADS // hp - 1) & (kj == nb - 1))
            def _():
                rider.finish(*ride)

    whole = pl.BlockSpec((lp, wide), lambda g, j: (0, g))
    kspec = pl.BlockSpec((t, wide), lambda g, j: (j, g))
    stat = pl.BlockSpec((hp, lp, 1), lambda g, j: (g, 0, 0))
    r_in, r_out, r_shapes, r_scratch = _rider_args(rider)
    return pl.pallas_call(
        body, name=name,
        out_shape=[jax.ShapeDtypeStruct((lp, MLA_WIDE), F32)] * 3 + r_shapes,
        grid=(MLA_HEADS // hp, nb),
        in_specs=[whole, kspec, kspec, whole, stat, stat] + r_in,
        out_specs=[whole, kspec, kspec] + r_out,
        scratch_shapes=[pltpu.VMEM((hp, t, t), F32), pltpu.VMEM((hp, t, t), F32),
                        pltpu.VMEM((hp, t, t), BF16), pltpu.VMEM((hp, t, t), BF16)] + r_scratch,
        compiler_params=_cparams(dimension_semantics=("arbitrary", "arbitrary")),
    )(q, k, v, do, lse, delta, *(rider.operands if rider else ()))


def _rope_tables(lp):
    inv = 1.0 / (ROPE_THETA ** (jnp.arange(0, MLA_ROPE, 2, dtype=F32) / MLA_ROPE))
    pos = jnp.maximum(jnp.arange(lp, dtype=jnp.int32) - NPAD, 0).astype(F32)
    ang = pos[:, None] * inv[None, :]
    cos, sin = jnp.cos(ang), jnp.sin(ang)
    z32 = jnp.zeros((lp, HEAD_SLOT - MLA_QK), F32)
    cos_t = jnp.concatenate([jnp.ones((lp, MLA_NOPE), F32), cos, cos, z32], axis=1)
    sin_t = jnp.concatenate([jnp.zeros((lp, MLA_NOPE), F32), sin, sin, z32], axis=1)
    return cos_t, sin_t


def _loss_head(h, target, name):
    lp = h.shape[0]

    def body(h_ref, t_ref, d_ref, loss_ref):
        i = pl.program_id(0)

        @pl.when(i == 0)
        def _():
            d_ref[...] = jnp.zeros_like(d_ref)
            loss_ref[...] = jnp.zeros_like(loss_ref)

        @pl.when(i > 0)
        def _():
            err = h_ref[...] - t_ref[...]
            d_ref[...] = err * (1.0 / D_MODEL)
            loss_ref[...] += jnp.sum(err * err, axis=0, keepdims=True) * (0.5 / D_MODEL)

    return pl.pallas_call(
        body, name=name,
        out_shape=[jax.ShapeDtypeStruct((lp, D_MODEL), F32), jax.ShapeDtypeStruct((1, D_MODEL), F32)],
        grid=(lp // CHUNK,),
        in_specs=[pl.BlockSpec((CHUNK, D_MODEL), lambda i: (i, 0)),
                  pl.BlockSpec((CHUNK, D_MODEL), lambda i: (jnp.maximum(i - 1, 0), 0))],
        out_specs=[pl.BlockSpec((CHUNK, D_MODEL), lambda i: (i, 0)),
                   pl.BlockSpec((1, D_MODEL), lambda i: (0, 0))],
        compiler_params=_cparams(dimension_semantics=("arbitrary",)),
    )(h, target)


def _pad_cols(w, n):
    return jnp.pad(w, [(0, 0)] * (w.ndim - 1) + [(0, n - w.shape[-1])])


def _layer_slab(name, i):
    return i if name.startswith('mlp_') else i // 2


def _prep_layer(p, i, get):
    j = i // 2
    if i % 2 == 0:
        p['ssd_in'][j] = _pad_cols(get('ssd_w_in'), SSD_IN_PAD).astype(BF16)
        p['ssd_out'][j] = get('ssd_w_out').astype(BF16)
    else:
        wi = get('mla_w_in')
        kpe = jnp.pad(wi[:, MLA_Q_RANK + MLA_KV_RANK:], ((0, 0), (MLA_NOPE, HEAD_SLOT - MLA_QK)))
        p['mla_in'][j] = jnp.concatenate(
            [wi[:, MLA_Q_RANK:MLA_Q_RANK + MLA_KV_RANK], kpe, wi[:, :MLA_Q_RANK]], axis=1).astype(BF16)
        qb = get('mla_w_q_b').reshape(MLA_Q_RANK, MLA_HEADS, MLA_QK)
        p['mla_qb'][j] = _pad_cols(qb, HEAD_SLOT).reshape(MLA_Q_RANK, MLA_WIDE).astype(BF16)
        kvb = get('mla_w_kv_b').reshape(MLA_KV_RANK, MLA_HEADS, MLA_NOPE + MLA_V)
        kn = _pad_cols(kvb[:, :, :MLA_NOPE], HEAD_SLOT).reshape(MLA_KV_RANK, MLA_WIDE)
        vv = _pad_cols(kvb[:, :, MLA_NOPE:], HEAD_SLOT).reshape(MLA_KV_RANK, MLA_WIDE)
        p['mla_kvb'][j] = jnp.concatenate([kn, vv], axis=1).astype(BF16)
        wo = get('mla_w_out').reshape(MLA_HEADS, MLA_V, D_MODEL)
        p['mla_out'][j] = (jnp.pad(wo, ((0, 0), (0, HEAD_SLOT - MLA_V), (0, 0)))
                           .reshape(MLA_WIDE, D_MODEL).astype(BF16))
    p['up'][i] = get('mlp_w_up').astype(BF16)
    p['down'][i] = get('mlp_w_down').astype(BF16)


def _no_matrices():
    return {k: [None] * n for k, n in (('ssd_in', 2), ('ssd_out', 2), ('mla_in', 2), ('mla_qb', 2),
                                       ('mla_kvb', 2), ('mla_out', 2), ('up', 4), ('down', 4))}


class _ReadyWeights:
    def __init__(self, w):
        self.w, self.p = w, _no_matrices()

    def ensure(self, i):
        _prep_layer(self.p, i, lambda n: self.w[n][_layer_slab(n, i)])

    def rider(self, i):
        return None

    def deliver(self, i, outs):
        assert not outs


class _KeepGrads:
    def __init__(self):
        self.rounds = {}

    def begin(self, r, grads):
        self.rounds[r] = grads
        return None

    def finish(self, r, outs):
        assert not outs

    def result(self):
        names = {n for g in self.rounds.values() for n in g}
        return {n: jnp.concatenate([self.rounds[r][n] for r in sorted(self.rounds, reverse=True)
                                    if n in self.rounds[r]], axis=0) for n in names}


def _pad128(v):
    return _pad_cols(v.reshape(1, -1), 128)


def _sqrelu(u):
    r = jnp.maximum(u, 0.0)
    return r * r


def _local_step(x, target, w, big=None, red=None):
    seq = x.shape[0]
    lp = NPAD + N_META + seq
    big = _ReadyWeights(w) if big is None else big
    p = big.p
    h = jnp.concatenate([jnp.zeros((NPAD, D_MODEL), F32), w['meta_tokens'], x], axis=0)
    cos_t, sin_t = _rope_tables(lp)
    rt = _pick(lp, (384, 256, 128))
    saved = []
    for i in range(4):
        j = i // 2
        big.ensure(i)
        s = {'h0': h}
        g_mix = w['ln_mix'][i].reshape(1, -1)
        g_mlp = w['ln_mlp'][i].reshape(1, -1)
        if i == 0:
            hn = _rms_fwd(h, g_mix, f"rms_mix_f{i}")
        s['hn'] = hn
        if i % 2 == 0:
            zxd = _mm(hn, p['ssd_in'][j], 'nn', name=f"ssd_in_f{i}")
            consts = _ssd_consts(w['ssd_conv_w'][j], w['ssd_conv_b'][j].reshape(1, -1),
                                 _pad128(w['ssd_dt_bias'][j]), _pad128(w['ssd_a_log'][j]),
                                 _pad128(w['ssd_d'][j]), w['ssd_norm'][j].reshape(1, -1))
            yg, states, *got = _ssd_fwd(zxd, consts, f"ssd_core_f{i}", rider=big.rider(i))
            big.deliver(i, got)
            s.update(zxd=zxd, consts=consts, yg=yg, states=states)
            h, hn2 = _mm(yg, p['ssd_out'][j], 'nn', name=f"ssd_out_f{i}", epi=lambda r, hv: hv + r,
                         extras=(h,), norm_gain=g_mlp)
        else:
            lat = _mm(hn, p['mla_in'][j], 'nn', name=f"mla_in_f{i}")
            kvg = w['mla_kv_a_norm'][j].reshape(1, -1)
            qag = w['mla_q_a_norm'][j].reshape(1, -1)
            kvn, qn = _row_call(lambda _, a, b, c, d: _lat_norm(a, b, c, d),
                                [(lat, MLA_KV_RANK, 0), (lat, MLA_Q_RANK, 1)], [kvg, qag],
                                [(MLA_KV_RANK, BF16), (MLA_Q_RANK, BF16)], n_rows=lp, tile=rt,
                                name=f"mla_latnorm_f{i}")
            q_raw = _mm(qn, p['mla_qb'][j], 'nn', name=f"mla_qb_f{i}")
            kv_raw = _mm(kvn, p['mla_kvb'][j], 'nn', name=f"mla_kvb_f{i}")
            qg = _pad_cols(w['mla_q_norm'][j].reshape(1, -1), HEAD_SLOT)
            kg = _pad_cols(w['mla_k_norm'][j].reshape(1, -1), HEAD_SLOT)

            def prep_fwd(_, qr, kn, kpe, vv, cs, sn, qgv, kgv):
                qq, kk = _qk_prep(qr, kn, kpe, cs, sn, qgv, kgv)
                ones = lax.broadcasted_iota(jnp.int32, vv.shape, 1) % HEAD_SLOT == MLA_V
                return qq, kk, jnp.where(ones, 1.0, vv)

            q, k, v = _row_call(prep_fwd,
                                [(q_raw, MLA_WIDE, 0), (kv_raw, MLA_WIDE, 0), (lat, HEAD_SLOT, 2),
                                 (kv_raw, MLA_WIDE, 1), (cos_t, HEAD_SLOT, 0), (sin_t, HEAD_SLOT, 0)],
                                [qg, kg], [(MLA_WIDE, BF16)] * 3, n_rows=lp, tile=rt,
                                name=f"mla_qkprep_f{i}")
            o, lse, *got = _attn_fwd(q, k, v, f"mla_attn_f{i}", rider=big.rider(i))
            big.deliver(i, got)
            s.update(lat=lat, kvg=kvg, qag=qag, kvn=kvn, qn=qn, q_raw=q_raw, kv_raw=kv_raw, qg=qg, kg=kg,
                     q=q, k=k, v=v, o=o, lse=lse)
            h, hn2 = _mm(o, p['mla_out'][j], 'nn', name=f"mla_out_f{i}", epi=lambda r, hv: hv + r,
                         extras=(h,), norm_gain=g_mlp)
        s['h1'] = h
        u = _mm(hn2, p['up'][i], 'nn', name=f"mlp_up_f{i}", out_dtype=BF16)
        if i < 3:
            h, hn = _mm(u, p['down'][i], 'nn', name=f"mlp_down_f{i}", a_fn=_sqrelu, epi=lambda r, hv: hv + r,
                        extras=(h,), norm_gain=w['ln_mix'][i + 1].reshape(1, -1))
        else:
            h = _mm(u, p['down'][i], 'nn', name=f"mlp_down_f{i}", a_fn=_sqrelu,
                    epi=lambda r, hv: hv + r, extras=(h,))
        s.update(hn2=hn2, u=u, g_mix=g_mix, g_mlp=g_mlp)
        saved.append(s)

    dh, loss_row = _loss_head(h, target, "loss_head")

    large = {n for n, _ in BIG}
    g = {k_: [None] * (4 if k_ in ('ln_mix', 'ln_mlp') else 2)
         for k_ in ALL_NAMES if k_ != 'meta_tokens' and k_ not in large}
    red = _KeepGrads() if red is None else red
    rounds, pending = {}, None

    def round_of(nm, i):
        return next(r for r, spec in enumerate(REDUCE_ROUNDS)
                    for n, l0, l1 in spec if n == nm and l0 <= _layer_slab(nm, i) < l1)

    def slabs_in(nm, r):
        return next((l0, l1) for n, l0, l1 in REDUCE_ROUNDS[r] if n == nm)

    def dw_into(nm, i, a, b, **kw):
        r = round_of(nm, i)
        (l0, l1), cur = slabs_in(nm, r), rounds.setdefault(r, {})
        cur[nm] = _mm(a, b, 'tn', stack=(l1 - l0, _layer_slab(nm, i) - l0, cur.get(nm)), **kw)

    def put(nm, i, arr):
        rounds.setdefault(round_of(nm, i), {})[nm] = arr[None]

    def hand_over(r):
        nonlocal pending
        pending = (r, red.begin(r, rounds.pop(r)))

    def host(fn, *args):
        nonlocal pending
        if pending is None or pending[1] is None:
            return fn(*args)
        (r, rider), pending = pending, None
        outs = fn(*args, rider=rider)
        own = len(outs) - len(rider.out_shapes)
        red.finish(r, outs[own:])
        return outs[:own]

    for i in reversed(range(4)):
        j = i // 2
        s = saved[i]
        dw_into('mlp_w_down', i, s['u'], dh, name=f"mlp_down_dw{i}", a_fn=_sqrelu)
        du = _mm(dh, p['down'][i], 'nt', name=f"mlp_down_dx{i}", out_dtype=BF16,
                 epi=lambda r, uv: r * (2.0 * jnp.maximum(uv, 0.0)), extras=(s['u'],))
        dw_into('mlp_w_up', i, s['hn2'], du, name=f"mlp_up_dw{i}")
        dh, dg = _mm_rms_bwd(du, p['up'][i], s['h1'], dh, s['g_mlp'], f"mlp_up_dx{i}")
        g['ln_mlp'][i] = dg[0]
        if i % 2 == 0:
            dw_into('ssd_w_out', i, s['yg'], dh, name=f"ssd_out_dw{i}")
            d_yg = _mm(dh, p['ssd_out'][j], 'nt', name=f"ssd_out_dx{i}")
            if i == 0:
                hand_over(1)
            d_zxd, dcw, dcb, ddtb, dalog, ddsk, dng = host(_ssd_bwd, s['zxd'], s['states'], d_yg, s['consts'],
                                                           f"ssd_core_b{i}")
            g['ssd_conv_w'][j], g['ssd_conv_b'][j], g['ssd_norm'][j] = dcw, dcb[0], dng[0]
            g['ssd_dt_bias'][j], g['ssd_a_log'][j], g['ssd_d'][j] = (
                ddtb[0, :SSD_HEADS], dalog[0, :SSD_HEADS], ddsk[0, :SSD_HEADS])
            dw_into('ssd_w_in', i, s['hn'], d_zxd, name=f"ssd_in_dw{i}")
            dh, dg = _mm_rms_bwd(d_zxd, p['ssd_in'][j], s['h0'], dh, s['g_mix'], f"ssd_in_dx{i}")
        else:
            wo = _mm(s['o'], dh, 'tn', name=f"mla_out_dw{i}")
            put('mla_w_out', i, wo.reshape(MLA_HEADS, HEAD_SLOT, D_MODEL)[:, :MLA_V].reshape(-1, D_MODEL))
            dob, delta = _mm_attn_do(dh, p['mla_out'][j], s['o'], f"mla_out_dx{i}")
            dq, dk, dv = host(_attn_bwd, s['q'], s['k'], s['v'], dob, s['lse'], delta, f"mla_attn_b{i}")

            def prep_bwd(_, qr, kn, kpe, cs, sn, dqv, dkv, dvv, qgv, kgv):
                _, vjp = jax.vjp(lambda a, b, c, d, e: _qk_prep(a, b, c, cs, sn, d, e), qr, kn, kpe, qgv, kgv)
                d_qr, d_kn, d_kpe, d_qg, d_kg = vjp((dqv, dkv))
                return d_qr, jnp.concatenate([d_kn, dvv], axis=1), d_kpe, d_qg, d_kg

            d_qraw, d_kvraw, d_kpe, d_qg, d_kg = _row_call(
                prep_bwd,
                [(s['q_raw'], MLA_WIDE, 0), (s['kv_raw'], MLA_WIDE, 0), (s['lat'], HEAD_SLOT, 2),
                 (cos_t, HEAD_SLOT, 0), (sin_t, HEAD_SLOT, 0), (dq, MLA_WIDE, 0), (dk, MLA_WIDE, 0),
                 (dv, MLA_WIDE, 0)],
                [s['qg'], s['kg']], [(MLA_WIDE, BF16), (2 * MLA_WIDE, BF16), (HEAD_SLOT, F32)],
                [(1, HEAD_SLOT), (1, HEAD_SLOT)], n_rows=lp, tile=_pick(lp, (128,)), name=f"mla_qkprep_b{i}")
            g['mla_q_norm'][j], g['mla_k_norm'][j] = d_qg[0, :MLA_QK], d_kg[0, :MLA_QK]
            wqb = _mm(s['qn'], d_qraw, 'tn', name=f"mla_qb_dw{i}")
            put('mla_w_q_b', i, wqb.reshape(MLA_Q_RANK, MLA_HEADS, HEAD_SLOT)[:, :, :MLA_QK].reshape(MLA_Q_RANK, -1))
            d_qn = _mm(d_qraw, p['mla_qb'][j], 'nt', name=f"mla_qb_dx{i}")
            wkvb = _mm(s['kvn'], d_kvraw, 'tn', name=f"mla_kvb_dw{i}").reshape(MLA_KV_RANK, 2, MLA_HEADS, HEAD_SLOT)
            put('mla_w_kv_b', i, jnp.concatenate([wkvb[:, 0, :, :MLA_NOPE], wkvb[:, 1, :, :MLA_V]],
                                                 axis=-1).reshape(MLA_KV_RANK, -1))
            d_kvn = _mm(d_kvraw, p['mla_kvb'][j], 'nt', name=f"mla_kvb_dx{i}")

            def lat_bwd(_, kvl, ql, dkvn, dqn, dkpe, kvgv, qagv):
                _, vjp = jax.vjp(_lat_norm, kvl, ql, kvgv, qagv)
                d_kvl, d_ql, d_kvg, d_qag = vjp((dkvn, dqn))
                return jnp.concatenate([d_kvl, dkpe, d_ql], axis=1), d_kvg, d_qag

            d_lat, d_kvg, d_qag = _row_call(
                lat_bwd, [(s['lat'], MLA_KV_RANK, 0), (s['lat'], MLA_Q_RANK, 1), (d_kvn, MLA_KV_RANK, 0),
                          (d_qn, MLA_Q_RANK, 0), (d_kpe, HEAD_SLOT, 0)],
                [s['kvg'], s['qag']], [(LAT_PAD, BF16)], [(1, MLA_KV_RANK), (1, MLA_Q_RANK)],
                n_rows=lp, tile=rt, name=f"mla_latnorm_b{i}")
            g['mla_kv_a_norm'][j], g['mla_q_a_norm'][j] = d_kvg[0], d_qag[0]
            win = _mm(s['hn'], d_lat, 'tn', name=f"mla_in_dw{i}")
            put('mla_w_in', i, jnp.concatenate(
                [win[:, MLA_KV_RANK + HEAD_SLOT:], win[:, :MLA_KV_RANK],
                 win[:, MLA_KV_RANK + MLA_NOPE:MLA_KV_RANK + MLA_QK]], axis=1))
            dh, dg = _mm_rms_bwd(d_lat, p['mla_in'][j], s['h0'], dh, s['g_mix'], f"mla_in_dx{i}")
        g['ln_mix'][i] = dg[0]
        if i == 2:
            hand_over(0)
    hand_over(2)

    if pending[1] is not None:
        red.finish(pending[0], _run_rider(pending[1], "rs_exchange_last"))
    grads = {k_: jnp.stack(v_) for k_, v_ in g.items()}
    grads['meta_tokens'] = dh[NPAD:NPAD + N_META]
    return loss_row, dh[NPAD + N_META:], grads, red


def _all_gather8(shard, name):
    m_per, n = shard.shape

    def body(x_ref, out_ref, send_sems, recv_sems, local_sem):
        x, y, c = lax.axis_index("x"), lax.axis_index("y"), lax.axis_index("c")
        me, sibling = (x, y, c), (x, y, 1 - c)
        chips = [(1 - x, y), (x, 1 - y), (1 - x, 1 - y)]

        def rows(px, py, pc):
            return out_ref.at[pl.ds((4 * px + 2 * py + pc) * m_per, m_per), :]

        def copy(k, block, to, src=None):
            return pltpu.make_async_remote_copy(
                src_ref=rows(*block) if src is None else src, dst_ref=rows(*block),
                send_sem=send_sems.at[k], recv_sem=recv_sems.at[k], device_id=to, device_id_type=MESH)

        mine = pltpu.make_async_copy(x_ref, rows(*me), local_sem)
        mine.start()
        first = [copy(0, me, sibling, src=x_ref)]
        first += [copy(1 + j, me, (*chip, c), src=x_ref) for j, chip in enumerate(chips)]
        for cp in first:
            cp.start()
        passed = [copy(4 + j, (*chip, c), sibling) for j, chip in enumerate(chips)]
        for j, chip in enumerate(chips):
            copy(1 + j, (*chip, c), me).wait_recv()
            passed[j].start()
        copy(0, sibling, me).wait_recv()
        for j, chip in enumerate(chips):
            copy(4 + j, (*chip, 1 - c), me).wait_recv()
        for cp in first + passed:
            cp.wait_send()
        mine.wait()

    return pl.pallas_call(
        body, name=name,
        out_shape=jax.ShapeDtypeStruct((8 * m_per, n), shard.dtype),
        in_specs=[pl.BlockSpec(memory_space=pl.ANY)],
        out_specs=pl.BlockSpec(memory_space=pl.ANY),
        scratch_shapes=[pltpu.SemaphoreType.DMA((7,)), pltpu.SemaphoreType.DMA((7,)), pltpu.SemaphoreType.DMA],
    )(shard)


def _mesh_pos():
    return lax.axis_index("x"), lax.axis_index("y"), lax.axis_index("c")


def _half_rows(pc, h):
    return pl.ds(pl.multiple_of(pc * h, 16), h)


def _whole_view(ref, kind, shard_shape, k, pc):
    _, r, c = shard_shape
    rows = _half_rows(pc, r // 2)
    if kind == 'row':
        return ref.at[:, k, rows, :]
    if kind == 'col':
        return ref.at[:, rows, pl.ds(pl.multiple_of(k * c, 128), c)]
    return ref.at[k, :, rows, :]


def _whole_shape(kind, shard_shape, rows=None):
    l, r, c = shard_shape
    r = r if rows is None else rows
    return {'row': (l, 4, r, c), 'col': (l, r, 4 * c), 'colx': (4, l, r, c)}[kind]


def _gather_rider(shards, kinds):
    n = len(shards)
    shapes = [s.shape for s in shards]

    def plan(ins, outs, sems):
        send_sems, recv_sems, local_sems = sems
        x, y, c = _mesh_pos()
        me, sibling = (x, y, c), (x, y, 1 - c)
        chips = [(1 - x, y), (x, 1 - y), (1 - x, 1 - y)]

        def place(a, px, py, pc):
            return _whole_view(outs[a], kinds[a], shapes[a], 2 * px + py, pc)

        def own(a):
            return ins[a].at[:, _half_rows(c, shapes[a][1] // 2), :]

        def copy(a, k, block, to, src=None):
            return pltpu.make_async_remote_copy(
                src_ref=place(a, *block) if src is None else src, dst_ref=place(a, *block),
                send_sem=send_sems.at[7 * a + k], recv_sem=recv_sems.at[7 * a + k],
                device_id=to, device_id_type=MESH)

        mine = [pltpu.make_async_copy(own(a), place(a, *me), local_sems.at[a]) for a in range(n)]
        first = [copy(a, 1 + j, me, (*chip, c), src=own(a)) for j, chip in enumerate(chips) for a in range(n)]
        first += [copy(a, 0, me, sibling, src=own(a)) for a in range(n)]
        return copy, mine, first, chips, me, sibling, c

    def start(ins, outs, sems):
        _, mine, first, *_ = plan(ins, outs, sems)
        for cp in first + mine:
            cp.start()

    def finish(ins, outs, sems):
        copy, mine, first, chips, me, sibling, c = plan(ins, outs, sems)
        passed = []
        for j, chip in enumerate(chips):
            for a in range(n):
                copy(a, 1 + j, (*chip, c), me).wait_recv()
                passed.append(copy(a, 4 + j, (*chip, c), sibling))
                passed[-1].start()
        for a in range(n):
            copy(a, 0, sibling, me).wait_recv()
        for j, chip in enumerate(chips):
            for a in range(n):
                copy(a, 4 + j, (*chip, 1 - c), me).wait_recv()
        for cp in first + passed:
            cp.wait_send()
        for cp in mine:
            cp.wait()

    return _Rider(
        shards, [jax.ShapeDtypeStruct(_whole_shape(k, s.shape), s.dtype) for k, s in zip(kinds, shards)],
        [pltpu.SemaphoreType.DMA((7 * n,)), pltpu.SemaphoreType.DMA((7 * n,)), pltpu.SemaphoreType.DMA((n,))],
        start, finish)


def _run_rider(rider, name):
    ni, no = len(rider.operands), len(rider.out_shapes)

    def body(*refs):
        ride = (refs[:ni], refs[ni:ni + no], refs[ni + no:])
        rider.start(*ride)
        rider.finish(*ride)

    return pl.pallas_call(
        body, name=name, out_shape=rider.out_shapes,
        in_specs=[pl.BlockSpec(memory_space=pl.ANY)] * ni,
        out_specs=[pl.BlockSpec(memory_space=pl.ANY)] * no,
        scratch_shapes=rider.scratch,
    )(*rider.operands)


def _rs_swap(wholes, kinds, shapes, name):
    n = len(wholes)

    def body(*refs):
        ins, outs = refs[:n], refs[n:2 * n]
        send_sems, recv_sems = refs[2 * n:]
        x, y, c = _mesh_pos()
        cps = []
        for a in range(n):
            rows = _half_rows(1 - c, shapes[a][1] // 2)
            src = ins[a].at[:, rows, :] if kinds[a] == 'col' else ins[a].at[:, :, rows, :]
            cps.append(pltpu.make_async_remote_copy(
                src_ref=src, dst_ref=outs[a], send_sem=send_sems.at[a], recv_sem=recv_sems.at[a],
                device_id=(x, y, 1 - c), device_id_type=MESH))
        for cp in cps:
            cp.start()
        for cp in cps:
            cp.wait()

    return pl.pallas_call(
        body, name=name,
        out_shape=[jax.ShapeDtypeStruct(_whole_shape(k, s, s[1] // 2), w.dtype)
                   for k, s, w in zip(kinds, shapes, wholes)],
        in_specs=[pl.BlockSpec(memory_space=pl.ANY)] * n,
        out_specs=[pl.BlockSpec(memory_space=pl.ANY)] * n,
        scratch_shapes=[pltpu.SemaphoreType.DMA((n,)), pltpu.SemaphoreType.DMA((n,))],
    )(*wholes)


def _exchange_rider(parts, kinds, shapes):
    n = len(parts)

    def plan(ins, outs, sems):
        send_sems, recv_sems, local_sems = sems
        x, y, c = _mesh_pos()
        kme = 2 * x + y
        chips = [(1 - x, y), (x, 1 - y), (1 - x, 1 - y)]

        def slab(a, k):
            if kinds[a] == 'row':
                return ins[a].at[:, k]
            if kinds[a] == 'col':
                cw = shapes[a][2]
                return ins[a].at[:, :, pl.ds(pl.multiple_of(k * cw, 128), cw)]
            return ins[a].at[k]

        cps = [pltpu.make_async_remote_copy(
            src_ref=slab(a, 2 * px + py), dst_ref=outs[a].at[kme], send_sem=send_sems.at[3 * a + j],
            recv_sem=recv_sems.at[3 * a + j], device_id=(px, py, c), device_id_type=MESH)
            for j, (px, py) in enumerate(chips) for a in range(n)]
        return cps + [pltpu.make_async_copy(slab(a, kme), outs[a].at[kme], local_sems.at[a]) for a in range(n)]

    def start(ins, outs, sems):
        for cp in plan(ins, outs, sems):
            cp.start()

    def finish(ins, outs, sems):
        for cp in plan(ins, outs, sems):
            cp.wait()

    return _Rider(
        parts, [jax.ShapeDtypeStruct((4, s[0], s[1] // 2, s[2]), p.dtype) for s, p in zip(shapes, parts)],
        [pltpu.SemaphoreType.DMA((3 * n,)), pltpu.SemaphoreType.DMA((3 * n,)), pltpu.SemaphoreType.DMA((n,))],
        start, finish)


def _rs_share(shards, slabs, name):
    n = len(shards)

    def body(*refs):
        outs = refs[n:2 * n]
        send_sems, recv_sems = refs[2 * n:]
        x, y, c = _mesh_pos()
        cps = []
        for a in range(n):
            l0, l1 = slabs[a]
            rows = outs[a].at[pl.ds(l0, l1 - l0), _half_rows(c, shards[a].shape[1] // 2), :]
            cps.append(pltpu.make_async_remote_copy(
                src_ref=rows, dst_ref=rows, send_sem=send_sems.at[a], recv_sem=recv_sems.at[a],
                device_id=(x, y, 1 - c), device_id_type=MESH))
        for cp in cps:
            cp.start()
        for cp in cps:
            cp.wait()

    return pl.pallas_call(
        body, name=name,
        out_shape=[jax.ShapeDtypeStruct(s.shape, s.dtype) for s in shards],
        in_specs=[pl.BlockSpec(memory_space=pl.ANY)] * n,
        out_specs=[pl.BlockSpec(memory_space=pl.ANY)] * n,
        input_output_aliases={a: a for a in range(n)},
        scratch_shapes=[pltpu.SemaphoreType.DMA((n,)), pltpu.SemaphoreType.DMA((n,))],
    )(*shards)


def _tile_rows(rows, cols, budget=2 * 1024 * 1024):
    for t in (1024, 512, 256, 128, 64, 32, 16, 8):
        if rows % t == 0 and t * cols * 4 <= budget:
            return t
    return rows


def _add_half(g3, r3, c_idx, name):
    a, h, n = r3.shape
    t = _tile_rows(h, n)
    nt = h // t

    def body(c_ref, g_ref, r_ref, o_ref):
        o_ref[...] = (g_ref[...] + r_ref[...]).astype(o_ref.dtype)

    return pl.pallas_call(
        body, name=name, out_shape=jax.ShapeDtypeStruct((a, h, n), BF16),
        grid_spec=pltpu.PrefetchScalarGridSpec(
            num_scalar_prefetch=1, grid=(a, nt),
            in_specs=[pl.BlockSpec((1, t, n), lambda k, i, c: (k, c[0] * nt + i, 0)),
                      pl.BlockSpec((1, t, n), lambda k, i, c: (k, i, 0))],
            out_specs=pl.BlockSpec((1, t, n), lambda k, i, c: (k, i, 0))),
        compiler_params=_cparams(dimension_semantics=("parallel", "parallel")),
    )(c_idx, g3, r3)


def _sum4(parts, c_idx, name, into):
    _, l, h, n = parts.shape
    n_slabs, l0, buf = into
    t = _tile_rows(h, n, 1024 * 1024)
    nt = h // t
    held = () if buf is None else (buf,)

    def body(c_ref, p_ref, *rest):
        pv = p_ref[...].astype(F32)
        rest[-1][...] = ((pv[0] + pv[1]) + pv[2]) + pv[3]

    return pl.pallas_call(
        body, name=name, out_shape=jax.ShapeDtypeStruct((n_slabs, 2 * h, n), F32),
        grid_spec=pltpu.PrefetchScalarGridSpec(
            num_scalar_prefetch=1, grid=(l, nt),
            in_specs=[pl.BlockSpec((4, 1, t, n), lambda k, i, c: (0, k, i, 0))]
            + [pl.BlockSpec(memory_space=pl.ANY)] * len(held),
            out_specs=pl.BlockSpec((1, t, n), lambda k, i, c: (l0 + k, c[0] * nt + i, 0))),
        input_output_aliases={2: 0} if held else {},
        compiler_params=_cparams(dimension_semantics=("parallel", "parallel")),
    )(c_idx, parts, *held)


def _sum8(parts, name):
    _, m, n = parts.shape

    def body(p_ref, o_ref):
        acc = p_ref[0]
        for d in range(1, 8):
            acc = acc + p_ref[d]
        o_ref[...] = acc

    return pl.pallas_call(body, name=name, out_shape=jax.ShapeDtypeStruct((m, n), F32))(parts)


def _adamw(wp, gp, mp, vp, name):
    r, n = wp.shape
    t = _tile_rows(r, n, 1024 * 1024)

    def body(w_ref, g_ref, m_ref, v_ref, d_ref, mo_ref, vo_ref):
        gv = g_ref[...]
        m2 = ADAM_B1 * m_ref[...] + (1.0 - ADAM_B1) * gv
        v2 = ADAM_B2 * v_ref[...] + (1.0 - ADAM_B2) * (gv * gv)
        m_hat = m2 / (1.0 - ADAM_B1 ** ADAM_STEP)
        v_hat = v2 / (1.0 - ADAM_B2 ** ADAM_STEP)
        d_ref[...] = -ADAM_LR * (m_hat / (jnp.sqrt(v_hat) + ADAM_EPS) + ADAM_WD * w_ref[...])
        mo_ref[...] = m2
        vo_ref[...] = v2

    spec = pl.BlockSpec((t, n), lambda i: (i, 0))
    return pl.pallas_call(
        body, name=name, out_shape=[jax.ShapeDtypeStruct((r, n), F32)] * 3, grid=(r // t,),
        in_specs=[spec] * 4, out_specs=[spec] * 3,
        compiler_params=_cparams(dimension_semantics=("parallel",)),
    )(wp, gp, mp, vp)


BIG = (('ssd_w_in', 'colx'), ('ssd_w_out', 'row'), ('mla_w_in', 'row'), ('mla_w_q_b', 'col'),
       ('mla_w_kv_b', 'col'), ('mla_w_out', 'row'), ('mlp_w_up', 'col'), ('mlp_w_down', 'row'))
SMALL_SHARDED = (('meta_tokens', 1), ('ssd_conv_w', 2), ('mla_q_a_norm', 1), ('mla_kv_a_norm', 1))
SMALL_REPL = ('ln_mix', 'ln_mlp', 'ssd_conv_b', 'ssd_dt_bias', 'ssd_a_log', 'ssd_d', 'ssd_norm',
              'mla_q_norm', 'mla_k_norm')
ALL_NAMES = ('meta_tokens', 'ln_mix', 'ln_mlp', 'ssd_w_in', 'ssd_conv_w', 'ssd_conv_b', 'ssd_dt_bias',
             'ssd_a_log', 'ssd_d', 'ssd_norm', 'ssd_w_out', 'mla_w_in', 'mla_q_a_norm', 'mla_w_q_b',
             'mla_kv_a_norm', 'mla_w_kv_b', 'mla_q_norm', 'mla_k_norm', 'mla_w_out', 'mlp_w_up', 'mlp_w_down')


_MLA_BIG = ('mla_w_in', 'mla_w_q_b', 'mla_w_kv_b', 'mla_w_out')
GATHER_ROUNDS = (
    (('ssd_w_in', 0, 1), ('ssd_w_out', 0, 1), ('mlp_w_up', 0, 1), ('mlp_w_down', 0, 1)),
    tuple((n, 0, 1) for n in _MLA_BIG) + (('mlp_w_up', 1, 2), ('mlp_w_down', 1, 2)),
    (('ssd_w_in', 1, 2), ('ssd_w_out', 1, 2)) + tuple((n, 1, 2) for n in _MLA_BIG)
    + (('mlp_w_up', 2, 4), ('mlp_w_down', 2, 4)),
)


REDUCE_ROUNDS = (
    GATHER_ROUNDS[2],
    tuple((n, 0, 1) for n in _MLA_BIG) + (('mlp_w_up', 0, 2), ('mlp_w_down', 0, 2), ('ssd_w_out', 0, 1)),
    (('ssd_w_in', 0, 1),),
)


class _GatheredWeights:
    def __init__(self, shards):
        self.shards, self.p, self.whole = shards, _no_matrices(), {}

    def _round(self, r):
        spec = GATHER_ROUNDS[r]
        return _gather_rider([self.shards[n][l0:l1] for n, l0, l1 in spec], [dict(BIG)[n] for n, _, _ in spec])

    def _take(self, r, outs):
        for (n, l0, l1), o in zip(GATHER_ROUNDS[r], outs):
            kind = dict(BIG)[n]
            for l in range(l0, l1):
                if kind == 'row':
                    m = o[l - l0].reshape(-1, o.shape[-1])
                elif kind == 'col':
                    m = o[l - l0]
                else:
                    m = jnp.concatenate([o[k, l - l0] for k in range(4)], axis=-1)
                self.whole[(n, l)] = m

    def ensure(self, i):
        if i == 0:
            self._take(0, _run_rider(self._round(0), "gather_first"))
        _prep_layer(self.p, i, lambda n: self.whole[(n, _layer_slab(n, i))])

    def rider(self, i):
        return self._round(i + 1) if i + 1 < len(GATHER_ROUNDS) else None

    def deliver(self, i, outs):
        if i + 1 < len(GATHER_ROUNDS):
            self._take(i + 1, outs)


class _ScatterGrads:
    def __init__(self, shard_shapes, c_idx):
        self.shard_shapes, self.c_idx, self.out = shard_shapes, c_idx, {}

    def begin(self, r, grads):
        spec = REDUCE_ROUNDS[r]
        kinds = [dict(BIG)[n] for n, _, _ in spec]
        shapes = [(l1 - l0,) + tuple(self.shard_shapes[n][1:]) for n, l0, l1 in spec]
        wholes = []
        for (n, _, _), kind, s in zip(spec, kinds, shapes):
            if kind == 'row':
                wholes.append(grads[n].reshape(s[0], 4, s[1], s[2]))
            elif kind == 'col':
                wholes.append(grads[n])
            else:
                wholes.append(jnp.stack([grads[n][..., k * s[2]:(k + 1) * s[2]] for k in range(4)]))
        recv = _rs_swap(wholes, kinds, shapes, f"rs_swap{r}")
        parts = []
        for (n, _, _), kind, s, gw, rc in zip(spec, kinds, shapes, wholes, recv):
            if kind == 'col':
                g3, r3 = gw, rc
            else:
                g3, r3 = gw.reshape(-1, s[1], s[2]), rc.reshape(-1, s[1] // 2, s[2])
            parts.append(_add_half(g3, r3, self.c_idx, f"rs_add{r}_{n}").reshape(rc.shape))
        return _exchange_rider(parts, kinds, shapes)

    def finish(self, r, outs):
        spec = REDUCE_ROUNDS[r]
        for (n, l0, _), part in zip(spec, outs):
            self.out[n] = _sum4(part, self.c_idx, f"rs_sum{r}_{n}",
                                into=(self.shard_shapes[n][0], l0, self.out.get(n)))
        shared = _rs_share([self.out[n] for n, _, _ in spec], [(l0, l1) for _, l0, l1 in spec], f"rs_share{r}")
        self.out.update(zip([n for n, _, _ in spec], shared))


def _pack(arrs, rows_mult):
    flat = jnp.concatenate([a.reshape(-1) for a in arrs])
    per = LANES * rows_mult
    pad = (-flat.shape[0]) % per
    if pad:
        flat = jnp.concatenate([flat, jnp.zeros((pad,), flat.dtype)])
    return flat.reshape(-1, LANES)


def _unpack(pack, shapes):
    flat = pack.reshape(-1)
    out, off = [], 0
    for shp in shapes:
        n = math.prod(shp)
        out.append(flat[off:off + n].reshape(shp))
        off += n
    return out


def _split4(full, axis):
    shp = full.shape
    r = full.reshape(shp[:axis] + (4, shp[axis] // 4) + shp[axis + 1:])
    return jnp.moveaxis(r, axis, 0)


def _join4(parts, axis):
    r = jnp.moveaxis(parts, 0, axis)
    shp = r.shape
    return r.reshape(shp[:axis] + (shp[axis] * shp[axis + 1],) + shp[axis + 2:])


def _gather_params(shards, table, dtype, c, name):
    pack = _pack([shards[n].astype(dtype) for n, _ in table], 16)
    half = pack.shape[0] // 2
    mine = lax.dynamic_slice_in_dim(pack, c * half, half, axis=0)
    full = _all_gather8(mine, name).reshape(4, -1)
    out, off = {}, 0
    for n, ax in table:
        cnt = math.prod(shards[n].shape)
        out[n] = _join4(full[:, off:off + cnt].reshape((4,) + shards[n].shape), ax)
        off += cnt
    return out


def kernel(x, meta_tokens, ln_mix, ln_mlp, ssd_w_in, ssd_conv_w, ssd_conv_b, ssd_dt_bias, ssd_a_log, ssd_d, ssd_norm, ssd_w_out, mla_w_in, mla_q_a_norm, mla_w_q_b, mla_kv_a_norm, mla_w_kv_b, mla_q_norm, mla_k_norm, mla_w_out, mlp_w_up, mlp_w_down, loss_target, m_meta_tokens, m_ln_mix, m_ln_mlp, m_ssd_w_in, m_ssd_conv_w, m_ssd_conv_b, m_ssd_dt_bias, m_ssd_a_log, m_ssd_d, m_ssd_norm, m_ssd_w_out, m_mla_w_in, m_mla_q_a_norm, m_mla_w_q_b, m_mla_kv_a_norm, m_mla_w_kv_b, m_mla_q_norm, m_mla_k_norm, m_mla_w_out, m_mlp_w_up, m_mlp_w_down, v_meta_tokens, v_ln_mix, v_ln_mlp, v_ssd_w_in, v_ssd_conv_w, v_ssd_conv_b, v_ssd_dt_bias, v_ssd_a_log, v_ssd_d, v_ssd_norm, v_ssd_w_out, v_mla_w_in, v_mla_q_a_norm, v_mla_w_q_b, v_mla_kv_a_norm, v_mla_w_kv_b, v_mla_q_norm, v_mla_k_norm, v_mla_w_out, v_mlp_w_up, v_mlp_w_down):
    w_sh = dict(meta_tokens=meta_tokens, ln_mix=ln_mix, ln_mlp=ln_mlp, ssd_w_in=ssd_w_in, ssd_conv_w=ssd_conv_w, ssd_conv_b=ssd_conv_b, ssd_dt_bias=ssd_dt_bias, ssd_a_log=ssd_a_log, ssd_d=ssd_d, ssd_norm=ssd_norm, ssd_w_out=ssd_w_out, mla_w_in=mla_w_in, mla_q_a_norm=mla_q_a_norm, mla_w_q_b=mla_w_q_b, mla_kv_a_norm=mla_kv_a_norm, mla_w_kv_b=mla_w_kv_b, mla_q_norm=mla_q_norm, mla_k_norm=mla_k_norm, mla_w_out=mla_w_out, mlp_w_up=mlp_w_up, mlp_w_down=mlp_w_down)
    m_sh = dict(meta_tokens=m_meta_tokens, ln_mix=m_ln_mix, ln_mlp=m_ln_mlp, ssd_w_in=m_ssd_w_in, ssd_conv_w=m_ssd_conv_w, ssd_conv_b=m_ssd_conv_b, ssd_dt_bias=m_ssd_dt_bias, ssd_a_log=m_ssd_a_log, ssd_d=m_ssd_d, ssd_norm=m_ssd_norm, ssd_w_out=m_ssd_w_out, mla_w_in=m_mla_w_in, mla_q_a_norm=m_mla_q_a_norm, mla_w_q_b=m_mla_w_q_b, mla_kv_a_norm=m_mla_kv_a_norm, mla_w_kv_b=m_mla_w_kv_b, mla_q_norm=m_mla_q_norm, mla_k_norm=m_mla_k_norm, mla_w_out=m_mla_w_out, mlp_w_up=m_mlp_w_up, mlp_w_down=m_mlp_w_down)
    v_sh = dict(meta_tokens=v_meta_tokens, ln_mix=v_ln_mix, ln_mlp=v_ln_mlp, ssd_w_in=v_ssd_w_in, ssd_conv_w=v_ssd_conv_w, ssd_conv_b=v_ssd_conv_b, ssd_dt_bias=v_ssd_dt_bias, ssd_a_log=v_ssd_a_log, ssd_d=v_ssd_d, ssd_norm=v_ssd_norm, ssd_w_out=v_ssd_w_out, mla_w_in=v_mla_w_in, mla_q_a_norm=v_mla_q_a_norm, mla_w_q_b=v_mla_w_q_b, mla_kv_a_norm=v_mla_kv_a_norm, mla_w_kv_b=v_mla_w_kv_b, mla_q_norm=v_mla_q_norm, mla_k_norm=v_mla_k_norm, mla_w_out=v_mla_w_out, mlp_w_up=v_mlp_w_up, mlp_w_down=v_mlp_w_down)

    cx, cy, cc = lax.axis_index("x"), lax.axis_index("y"), lax.axis_index("c")
    chip = 2 * cx + cy

    c_idx = cc.reshape(1).astype(jnp.int32)
    big_names = [n for n, _ in BIG]
    shapes = [w_sh[n].shape for n in big_names]

    w = {n: w_sh[n] for n in SMALL_REPL}
    w.update(_gather_params(w_sh, SMALL_SHARDED, F32, cc, "gather_small"))
    big = _GatheredWeights({n: w_sh[n].astype(BF16) for n in big_names})
    red = _ScatterGrads({n: w_sh[n].shape for n in big_names}, c_idx)

    loss_row, grad_x, grads, red = _local_step(x[0], loss_target[0], w, big, red)
    loss = lax.psum(jnp.sum(loss_row), ("x", "y", "c"))
    g_sh = dict(red.out)

    small_names = tuple(n for n, _ in SMALL_SHARDED) + SMALL_REPL
    sp = _pack([grads[n] for n in small_names], 8)
    srows = sp.shape[0]
    s_all = _sum8(_all_gather8(sp, "ar_small_gather").reshape(8, srows, LANES), "ar_small_sum")
    s_full = dict(zip(small_names, _unpack(s_all, [grads[n].shape for n in small_names])))
    for n, ax in SMALL_SHARDED:
        g_sh[n] = lax.dynamic_index_in_dim(_split4(s_full[n], ax), chip, axis=0, keepdims=False)
    for n in SMALL_REPL:
        g_sh[n] = s_full[n]

    delta, new_m, new_v = {}, {}, {}
    for n, s in zip(big_names, shapes):
        res = _adamw(*[t[n].reshape(-1, s[2]) for t in (w_sh, g_sh, m_sh, v_sh)], f"adamw_{n}")
        delta[n], new_m[n], new_v[n] = [r.reshape(s) for r in res]
    d_s, m_s, v_s = _adamw(*[_pack([t[n] for n in small_names], 8) for t in (w_sh, g_sh, m_sh, v_sh)],
                           "adamw_small")
    for dst, ps in ((delta, d_s), (new_m, m_s), (new_v, v_s)):
        dst.update(zip(small_names, _unpack(ps, [w_sh[n].shape for n in small_names])))

    return (loss, grad_x[None], *[g_sh[n] for n in ALL_NAMES], *[delta[n] for n in ALL_NAMES],
            *[new_m[n] for n in ALL_NAMES], *[new_v[n] for n in ALL_NAMES])
```

```python
import functools
import math

import jax
import jax.numpy as jnp
from jax import lax
from jax.experimental import pallas as pl
from jax.experimental.pallas import tpu as pltpu

F32 = jnp.float32
BF16 = jnp.bfloat16
MESH = pl.DeviceIdType.MESH
_NN = (((1,), (0,)), ((), ()))
_NT = (((1,), (1,)), ((), ()))
_TN = (((0,), (0,)), ((), ()))

D_MODEL = 1024
N_META = 16
EPS = 1e-6
SSD_D_INNER = 2048
SSD_HEADS = 32
SSD_HEAD_DIM = 64
SSD_GROUPS = 8
SSD_HPG = 4
SSD_STATE = 128
SSD_CONV = 4
CHUNK = 128
SSD_IN_DIM = 6176
SSD_IN_PAD = 6272
MLA_HEADS = 16
MLA_NOPE = 64
MLA_ROPE = 32
MLA_V = 64
MLA_QK = 96
MLA_Q_RANK = 384
MLA_KV_RANK = 256
HEAD_SLOT = 128
MLA_WIDE = MLA_HEADS * HEAD_SLOT
HEADS_PER_STEP = 2
LAT_PAD = 768
ROPE_THETA = 10000.0
D_FF = 4096
NPAD = CHUNK - N_META
ADAM_LR, ADAM_B1, ADAM_B2, ADAM_EPS, ADAM_WD, ADAM_STEP = 0.001, 0.9, 0.999, 1e-08, 0.01, 10
LANES = 1024
V7X_VMEM_BYTES = 64 * 1024 * 1024
VMEM_LIMIT = V7X_VMEM_BYTES * 7 // 8


def _pick(n, cands):
    for c in cands:
        if n % c == 0:
            return c
    return n


def _cparams(**kw):
    return pltpu.CompilerParams(vmem_limit_bytes=VMEM_LIMIT, **kw)


def _mm(a, b, dims, *, name, out_dtype=F32, a_fn=None, epi=None, extras=(), stack=None, norm_gain=None):
    if dims == 'nn':
        (M, K), (K2, N) = a.shape, b.shape
    elif dims == 'nt':
        (M, K), (N, K2) = a.shape, b.shape
    else:
        (K, M), (K2, N) = a.shape, b.shape
    assert K == K2, (a.shape, b.shape, dims)
    if dims == 'tn':
        tm = _pick(M, (1024, 768, 512, 384, 256, 128))
        tn = _pick(N, (1024, 896, 768, 512, 384, 256, 128))
        tk = _pick(K, (1408, 1024, 512, 384, 256, 128))
    else:
        tm = _pick(M, (704, 512, 384, 256, 128) if norm_gain is not None else (1408, 1024, 512, 384, 256, 128))
        tn = _pick(N, (1024, 896, 768, 512, 384, 256, 128))
        tk = _pick(K, (1024, 896, 768, 512, 384, 256, 128))
    nk = K // tk
    if dims == 'nn':
        a_spec = pl.BlockSpec((tm, tk), lambda i, j, k: (i, k))
        b_spec = pl.BlockSpec((tk, tn), lambda i, j, k: (k, j))
        dn = (((1,), (0,)), ((), ()))
    elif dims == 'nt':
        a_spec = pl.BlockSpec((tm, tk), lambda i, j, k: (i, k))
        b_spec = pl.BlockSpec((tn, tk), lambda i, j, k: (j, k))
        dn = (((1,), (1,)), ((), ()))
    else:
        a_spec = pl.BlockSpec((tk, tm), lambda i, j, k: (k, i))
        b_spec = pl.BlockSpec((tk, tn), lambda i, j, k: (k, j))
        dn = (((0,), (0,)), ((), ()))
    o_spec = pl.BlockSpec((tm, tn), lambda i, j, k: (i, j))
    n_ex = len(extras)
    out_shape = jax.ShapeDtypeStruct((M, N), out_dtype)
    out_spec, held, aliases = o_spec, (), {}
    if stack is not None:
        n_slabs, slab, buf = stack
        out_shape = jax.ShapeDtypeStruct((n_slabs, M, N), out_dtype)
        out_spec = pl.BlockSpec((None, tm, tn), lambda i, j, k: (slab, i, j))
        if buf is not None:
            held, aliases = (buf,), {2 + n_ex: 0}

    gains = ()
    if norm_gain is not None:
        assert tn == N and stack is None, "the rms epilogue needs whole rows"
        gains = (norm_gain,)
        out_shape = [out_shape, jax.ShapeDtypeStruct((M, N), BF16)]
        out_spec = [out_spec, o_spec]

    def body(a_ref, b_ref, *rest):
        ex_refs, rest = rest[:n_ex], rest[n_ex:]
        g_refs, rest = rest[:len(gains)], rest[len(gains) + len(held):]
        o_ref, acc = rest[0], rest[-1]
        k = pl.program_id(2)

        @pl.when(k == 0)
        def _():
            acc[...] = jnp.zeros_like(acc)

        av = a_ref[...]
        if a_fn is not None:
            av = a_fn(av)
        acc[...] += lax.dot_general(av.astype(BF16), b_ref[...].astype(BF16), dn,
                                    preferred_element_type=F32)

        @pl.when(k == nk - 1)
        def _():
            r = acc[...]
            if epi is not None:
                r = epi(r, *[e[...] for e in ex_refs])
            o_ref[...] = r.astype(out_dtype)
            if gains:
                rest[1][...] = _rms(r, g_refs[0][...]).astype(BF16)

    return pl.pallas_call(
        body, name=name,
        out_shape=out_shape,
        grid=(M // tm, N // tn, nk),
        in_specs=[a_spec, b_spec] + [o_spec] * n_ex
        + [pl.BlockSpec((1, tn), lambda i, j, k: (0, j))] * len(gains)
        + [pl.BlockSpec(memory_space=pl.ANY)] * len(held),
        out_specs=out_spec,
        input_output_aliases=aliases,
        scratch_shapes=[pltpu.VMEM((tm, tn), F32)],
        compiler_params=_cparams(dimension_semantics=("parallel", "parallel", "arbitrary")),
    )(a, b, *extras, *gains, *held)


def _mm_rms_bwd(cot, w_t, h, d_res, gain, name):
    (M, K), (N, _) = cot.shape, w_t.shape
    tm = _pick(M, (704, 512, 384, 256, 128))
    tk = _pick(K, (1024, 896, 768, 512, 384, 256, 128))
    nk = K // tk

    def body(a_ref, b_ref, h_ref, r_ref, g_ref, dh_ref, dg_ref, acc):
        i, k = pl.program_id(0), pl.program_id(1)

        @pl.when(k == 0)
        def _():
            acc[...] = jnp.zeros_like(acc)

        acc[...] += lax.dot_general(a_ref[...].astype(BF16), b_ref[...].astype(BF16), _NT,
                                    preferred_element_type=F32)

        @pl.when(k == nk - 1)
        def _():
            _, vjp = jax.vjp(_rms, h_ref[...], g_ref[...])
            dh, dg = vjp(acc[...])
            dh_ref[...] = (r_ref[...] + dh) * _row_mask(i, tm)

            @pl.when(i == 0)
            def _():
                dg_ref[...] = dg

            @pl.when(i > 0)
            def _():
                dg_ref[...] += dg

    rows = pl.BlockSpec((tm, N), lambda i, k: (i, 0))
    vec = pl.BlockSpec((1, N), lambda i, k: (0, 0))
    return pl.pallas_call(
        body, name=name,
        out_shape=[jax.ShapeDtypeStruct((M, N), F32), jax.ShapeDtypeStruct((1, N), F32)],
        grid=(M // tm, nk),
        in_specs=[pl.BlockSpec((tm, tk), lambda i, k: (i, k)), pl.BlockSpec((N, tk), lambda i, k: (0, k)),
                  rows, rows, vec],
        out_specs=[rows, vec],
        scratch_shapes=[pltpu.VMEM((tm, N), F32)],
        compiler_params=_cparams(dimension_semantics=("arbitrary", "arbitrary")),
    )(cot, w_t, h, d_res, gain)


def _mm_attn_do(dh, w_out_t, o, name):
    (M, K), (N, _) = dh.shape, w_out_t.shape
    tm = _pick(M, (704, 512, 384, 256, 128))
    tn = 8 * HEAD_SLOT

    def body(a_ref, b_ref, o_ref, dob_ref, delta_ref):
        do = lax.dot_general(a_ref[...].astype(BF16), b_ref[...], _NT, preferred_element_type=F32)
        dob_ref[...] = do.astype(BF16)
        for hh in range(tn // HEAD_SLOT):
            sl = slice(hh * HEAD_SLOT, (hh + 1) * HEAD_SLOT)
            delta_ref[hh] = jnp.sum(do[:, sl] * o_ref[:, sl], axis=-1, keepdims=True)

    tile = pl.BlockSpec((tm, tn), lambda i, j: (i, j))
    return pl.pallas_call(
        body, name=name,
        out_shape=[jax.ShapeDtypeStruct((M, N), BF16), jax.ShapeDtypeStruct((N // HEAD_SLOT, M, 1), F32)],
        grid=(M // tm, N // tn),
        in_specs=[pl.BlockSpec((tm, K), lambda i, j: (i, 0)), pl.BlockSpec((tn, K), lambda i, j: (j, 0)), tile],
        out_specs=[tile, pl.BlockSpec((tn // HEAD_SLOT, tm, 1), lambda i, j: (j, i, 0))],
        compiler_params=_cparams(dimension_semantics=("parallel", "parallel")),
    )(dh, w_out_t, o)


def _row_call(fn, rows, consts, out_rows, out_accs=(), *, n_rows, tile, name):
    n_r, n_c, n_o, n_a = len(rows), len(consts), len(out_rows), len(out_accs)
    steps = n_rows // tile

    def body(*refs):
        r_refs = refs[:n_r]
        c_refs = refs[n_r:n_r + n_c]
        o_refs = refs[n_r + n_c:n_r + n_c + n_o]
        a_refs = refs[n_r + n_c + n_o:]
        i = pl.program_id(0)
        res = fn(i, *[r[...] for r in r_refs], *[c[...] for c in c_refs])
        for o_ref, val in zip(o_refs, res[:n_o]):
            o_ref[...] = val.astype(o_ref.dtype)

        @pl.when(i == 0)
        def _():
            for a_ref in a_refs:
                a_ref[...] = jnp.zeros_like(a_ref)

        for a_ref, val in zip(a_refs, res[n_o:]):
            a_ref[...] += val

    in_specs = [pl.BlockSpec((tile, w), functools.partial(lambda i, cb: (i, cb), cb=cb))
                for (_, w, cb) in rows]
    in_specs += [pl.BlockSpec(c.shape, lambda i: (0, 0)) for c in consts]
    out_specs = [pl.BlockSpec((tile, c), lambda i: (i, 0)) for (c, _) in out_rows]
    out_specs += [pl.BlockSpec(s, lambda i: (0, 0)) for s in out_accs]
    out_shape = [jax.ShapeDtypeStruct((n_rows, c), dt) for (c, dt) in out_rows]
    out_shape += [jax.ShapeDtypeStruct(s, F32) for s in out_accs]
    return pl.pallas_call(
        body, name=name, out_shape=out_shape, grid=(steps,),
        in_specs=in_specs, out_specs=out_specs,
        compiler_params=_cparams(dimension_semantics=("arbitrary",)),
    )(*[r[0] for r in rows], *consts)


def _row_mask(i, tile):
    r = i * tile + lax.broadcasted_iota(jnp.int32, (tile, 1), 0)
    return (r >= NPAD).astype(F32)


def _rms(x, g):
    return x * lax.rsqrt(jnp.mean(x * x, axis=-1, keepdims=True) + EPS) * g


def _silu(x):
    return x * (0.5 * jnp.tanh(0.5 * x) + 0.5)


def _softplus(x):
    return jnp.maximum(x, 0.0) + jnp.log(1.0 + jnp.exp(-jnp.abs(x)))


def _rms_fwd(h, g, name):
    lp = h.shape[0]
    return _row_call(lambda i, hv, gv: (_rms(hv, gv),), [(h, D_MODEL, 0)], [g],
                     [(D_MODEL, BF16)], n_rows=lp, tile=_pick(lp, (384, 256, 128)), name=name)[0]


@functools.partial(jax.custom_vjp, nondiff_argnums=(1,))
def _roll_rows(x, s):
    return pltpu.roll(x, s, 0)


def _roll_rows_fwd(x, s):
    return pltpu.roll(x, s, 0), None


def _roll_rows_bwd(s, _, ct):
    return (pltpu.roll(ct, (ct.shape[0] - s) % ct.shape[0], 0),)


_roll_rows.defvjp(_roll_rows_fwd, _roll_rows_bwd)


def _conv_silu(cur, halo, w_rows, b):
    full = jnp.concatenate([halo, cur], axis=0)
    acc = cur * w_rows[SSD_CONV - 1] + b
    for k in range(SSD_CONV - 1):
        acc = acc + _roll_rows(full, SSD_CONV - 1 - k)[8:] * w_rows[k]
    return _silu(acc)


def _split3(v):
    hi = v.astype(BF16)
    r1 = v - hi.astype(F32)
    mid = r1.astype(BF16)
    lo = (r1 - mid.astype(F32)).astype(BF16)
    return hi, mid, lo


def _select_right(v, sel, dn):
    return sum(lax.dot_general(p, sel, dn, preferred_element_type=F32) for p in _split3(v))


@jax.custom_vjp
def _expand_heads(v, e_mat):
    return _select_right(v, e_mat, _NN)


def _expand_heads_fwd(v, e_mat):
    return _select_right(v, e_mat, _NN), e_mat


def _expand_heads_bwd(e_mat, ct):
    return _select_right(ct, e_mat, _NT), jnp.zeros_like(e_mat)


_expand_heads.defvjp(_expand_heads_fwd, _expand_heads_bwd)


@jax.custom_vjp
def _cumsum_rows(a, tri):
    return sum(lax.dot_general(tri, p, _NN, preferred_element_type=F32) for p in _split3(a))


def _cumsum_rows_fwd(a, tri):
    return _cumsum_rows(a, tri), tri


def _cumsum_rows_bwd(tri, ct):
    return (sum(lax.dot_general(tri, p, _TN, preferred_element_type=F32) for p in _split3(ct)),
            jnp.zeros_like(tri))


_cumsum_rows.defvjp(_cumsum_rows_fwd, _cumsum_rows_bwd)


def _ssd_chunk(mask, z, xs_pre, bc_pre, halo_x, halo_bc, dt_pre, st, cwx0, cwx1, cwx2, cwx3,
               cwb0, cwb1, cwb2, cwb3, cb_x, cb_bc, dtb, alog, dsk, ng):
    L = CHUNK
    lane_h = lax.broadcasted_iota(jnp.int32, (1, 128), 1)
    head_ok = (lane_h < SSD_HEADS).astype(F32)
    e_mat = (lax.broadcasted_iota(jnp.int32, (128, SSD_D_INNER), 1) // SSD_HEAD_DIM
             == lax.broadcasted_iota(jnp.int32, (128, SSD_D_INNER), 0)).astype(BF16)
    ri = lax.broadcasted_iota(jnp.int32, (L, L), 0)
    ci = lax.broadcasted_iota(jnp.int32, (L, L), 1)
    causal = ri >= ci

    xs = _conv_silu(xs_pre, halo_x, (cwx0, cwx1, cwx2, cwx3), cb_x) * mask
    bc = _conv_silu(bc_pre, halo_bc, (cwb0, cwb1, cwb2, cwb3), cb_bc) * mask
    dt = _softplus(dt_pre + dtb) * mask * head_ok
    a_dt = dt * (-jnp.exp(alog))
    a_cs = _cumsum_rows(a_dt, causal.astype(BF16))
    a_cs_t = a_cs.T
    row8 = lax.broadcasted_iota(jnp.int32, (8, 128), 0)
    last8 = jnp.where(row8 == 0, jnp.sum(a_dt, axis=0, keepdims=True), 0.0)
    dsk8 = jnp.where(row8 == 0, dsk, 0.0)
    wide = _expand_heads(jnp.concatenate([dt, a_cs, last8, dsk8], axis=0), e_mat)
    dt_e, acs_e = wide[0:L], wide[L:2 * L]
    last_e = jnp.sum(wide[2 * L:2 * L + 8], axis=0, keepdims=True)
    d_e = jnp.sum(wide[2 * L + 8:2 * L + 16], axis=0, keepdims=True)
    xdt = xs * dt_e
    dte_e = jnp.exp(last_e - acs_e)
    dfs_e = jnp.exp(acs_e)
    cd_e = jnp.exp(last_e)
    sub_h = lax.broadcasted_iota(jnp.int32, (128, L), 0)
    lane_hl = lax.broadcasted_iota(jnp.int32, (L, 128), 1)
    lane_g = lax.broadcasted_iota(jnp.int32, (1, SSD_HPG * SSD_HEAD_DIM), 1) // SSD_HEAD_DIM

    ys, new_st = [], []
    for g in range(SSD_GROUPS):
        b_g = bc[:, g * 128:(g + 1) * 128].astype(BF16)
        c_g = bc[:, 1024 + g * 128:1024 + (g + 1) * 128].astype(BF16)
        gs = slice(g * 256, (g + 1) * 256)
        xdt_g = xdt[:, gs]
        cb = lax.dot_general(c_g, b_g, (((1,), (1,)), ((), ())), preferred_element_type=F32)
        st_g = st[g * 128:(g + 1) * 128, :]
        y_g = lax.dot_general(c_g, st_g.astype(BF16), (((1,), (0,)), ((), ())),
                              preferred_element_type=F32) * dfs_e[:, gs]
        for j in range(SSD_HPG):
            h = g * SSD_HPG + j
            col = jnp.sum(jnp.where(lane_hl == h, a_cs, 0.0), axis=1, keepdims=True)
            row = jnp.sum(jnp.where(sub_h == h, a_cs_t, 0.0), axis=0, keepdims=True)
            dec = jnp.where(causal, jnp.exp(jnp.where(causal, col - row, 0.0)), 0.0)
            m_h = (cb * dec).astype(BF16)
            x_h = jnp.where(lane_g == j, xdt_g, 0.0).astype(BF16)
            y_g = y_g + lax.dot_general(m_h, x_h, (((1,), (0,)), ((), ())),
                                        preferred_element_type=F32)
        s_new = lax.dot_general(b_g, (xdt_g * dte_e[:, gs]).astype(BF16), (((0,), (0,)), ((), ())),
                                preferred_element_type=F32)
        new_st.append(st_g * cd_e[:, gs] + s_new)
        ys.append(y_g)
    y = jnp.concatenate(ys, axis=1) + xs * d_e
    gg = y * _silu(z)
    outs = []
    for g in range(SSD_GROUPS):
        sl = gg[:, g * 256:(g + 1) * 256]
        outs.append(sl * lax.rsqrt(jnp.mean(sl * sl, axis=-1, keepdims=True) + EPS))
    out = jnp.concatenate(outs, axis=1) * ng
    return out, jnp.concatenate(new_st, axis=0)


def _ssd_consts(conv_w, conv_b, dtb, alog, dsk, ng):
    return [conv_w, conv_b, dtb, alog, dsk, ng]


def _ssd_param_vals(cw_ref, cb_ref, dtb_ref, alog_ref, dsk_ref, ng_ref):
    cwx = [cw_ref[k:k + 1, 0:SSD_D_INNER] for k in range(SSD_CONV)]
    cwb = [cw_ref[k:k + 1, SSD_D_INNER:2 * SSD_D_INNER] for k in range(SSD_CONV)]
    return (*cwx, *cwb, cb_ref[:, 0:SSD_D_INNER], cb_ref[:, SSD_D_INNER:2 * SSD_D_INNER],
            dtb_ref[...], alog_ref[...], dsk_ref[...], ng_ref[...])


def _ssd_in_specs(rev, nc):
    def cidx(i):
        return (nc - 1 - i) if rev else i

    def halo(cb):
        return pl.BlockSpec((8, SSD_D_INNER), lambda i: (jnp.maximum(16 * cidx(i) - 1, 0), cb))

    return [
        pl.BlockSpec((CHUNK, SSD_D_INNER), lambda i: (cidx(i), 0)),
        pl.BlockSpec((CHUNK, SSD_D_INNER), lambda i: (cidx(i), 1)),
        pl.BlockSpec((CHUNK, SSD_D_INNER), lambda i: (cidx(i), 2)),
        halo(1), halo(2),
        pl.BlockSpec((CHUNK, 128), lambda i: (cidx(i), 48)),
    ]


class _Rider:
    def __init__(self, operands, out_shapes, scratch, start, finish):
        self.operands, self.out_shapes, self.scratch = list(operands), list(out_shapes), list(scratch)
        self.start, self.finish = start, finish


def _rider_split(rider, refs, n_in, n_out, n_scratch):
    if rider is None:
        return refs, None
    ni, no = len(rider.operands), len(rider.out_shapes)
    own = refs[:n_in] + refs[n_in + ni:n_in + ni + n_out] + refs[n_in + ni + n_out + no:n_in + ni + n_out + no + n_scratch]
    mine = (refs[n_in:n_in + ni], refs[n_in + ni + n_out:n_in + ni + n_out + no],
            refs[n_in + ni + n_out + no + n_scratch:])
    return own, mine


def _rider_args(rider):
    if rider is None:
        return [], [], [], []
    hbm = pl.BlockSpec(memory_space=pl.ANY)
    return ([hbm] * len(rider.operands), [hbm] * len(rider.out_shapes), rider.out_shapes, rider.scratch)


def _ssd_fwd(zxd, consts, name, rider=None):
    lp = zxd.shape[0]
    nc = lp // CHUNK

    def body(*refs):
        own, ride = _rider_split(rider, refs, 12, 2, 1)
        (z_ref, xs_ref, bc_ref, hx_ref, hb_ref, dt_ref, cw_ref, cb_ref, dtb_ref, alog_ref,
         dsk_ref, ng_ref, y_ref, st_ref, state) = own
        c = pl.program_id(0)

        @pl.when(c == 0)
        def _():
            state[...] = jnp.zeros_like(state)
            if ride is not None:
                rider.start(*ride)

        live = (c > 0).astype(F32)
        st_ref[0] = state[...]
        out, st_new = _ssd_chunk(
            _row_mask(c, CHUNK), z_ref[...], xs_ref[...], bc_ref[...], hx_ref[...] * live,
            hb_ref[...] * live, dt_ref[...], state[...],
            *_ssd_param_vals(cw_ref, cb_ref, dtb_ref, alog_ref, dsk_ref, ng_ref))
        y_ref[...] = out.astype(y_ref.dtype)
        state[...] = st_new

        if ride is not None:
            @pl.when(c == nc - 1)
            def _():
                rider.finish(*ride)

    r_in, r_out, r_shapes, r_scratch = _rider_args(rider)
    return pl.pallas_call(
        body, name=name,
        out_shape=[jax.ShapeDtypeStruct((lp, SSD_D_INNER), BF16),
                   jax.ShapeDtypeStruct((nc, SSD_GROUPS * SSD_STATE, 256), F32)] + r_shapes,
        grid=(nc,),
        in_specs=_ssd_in_specs(False, nc) + [pl.BlockSpec(c.shape, lambda i: (0, 0)) for c in consts] + r_in,
        out_specs=[pl.BlockSpec((CHUNK, SSD_D_INNER), lambda i: (i, 0)),
                   pl.BlockSpec((1, SSD_GROUPS * SSD_STATE, 256), lambda i: (i, 0, 0))] + r_out,
        scratch_shapes=[pltpu.VMEM((SSD_GROUPS * SSD_STATE, 256), F32)] + r_scratch,
        compiler_params=_cparams(dimension_semantics=("arbitrary",)),
    )(zxd, zxd, zxd, zxd, zxd, zxd, *consts, *(rider.operands if rider else ()))


def _ssd_bwd(zxd, states, d_y, consts, name, rider=None):
    lp = zxd.shape[0]
    nc = lp // CHUNK

    def body(*refs):
        own, ride = _rider_split(rider, refs, 14, 7, 3)
        (z_ref, xs_ref, bc_ref, hx_ref, hb_ref, dt_ref, st_ref, dy_ref, cw_ref, cb_ref, dtb_ref,
         alog_ref, dsk_ref, ng_ref, dz_ref, dcw_ref, dcb_ref, ddtb_ref, dalog_ref, ddsk_ref,
         dng_ref, d_state, d_hx, d_hb) = own
        i = pl.program_id(0)
        c = nc - 1 - i

        @pl.when(i == 0)
        def _():
            d_state[...] = jnp.zeros_like(d_state)
            d_hx[...] = jnp.zeros_like(d_hx)
            d_hb[...] = jnp.zeros_like(d_hb)
            for r in (dcw_ref, dcb_ref, ddtb_ref, dalog_ref, ddsk_ref, dng_ref):
                r[...] = jnp.zeros_like(r)
            if ride is not None:
                rider.start(*ride)

        live = (c > 0).astype(F32)
        fn = functools.partial(_ssd_chunk, _row_mask(c, CHUNK))
        prim = (z_ref[...], xs_ref[...], bc_ref[...], hx_ref[...] * live, hb_ref[...] * live,
                dt_ref[...], st_ref[0],
                *_ssd_param_vals(cw_ref, cb_ref, dtb_ref, alog_ref, dsk_ref, ng_ref))
        _, vjp = jax.vjp(fn, *prim)
        (d_z, d_xs, d_bc, g_hx, g_hb, d_dt, g_st, *d_par) = vjp((dy_ref[...], d_state[...]))
        zeros = jnp.zeros((CHUNK - 8, SSD_D_INNER), F32)
        d_xs = d_xs + jnp.concatenate([zeros, d_hx[...]], axis=0)
        d_bc = d_bc + jnp.concatenate([zeros, d_hb[...]], axis=0)
        dz_ref[:, 0:SSD_D_INNER] = d_z.astype(dz_ref.dtype)
        dz_ref[:, SSD_D_INNER:2 * SSD_D_INNER] = d_xs.astype(dz_ref.dtype)
        dz_ref[:, 2 * SSD_D_INNER:3 * SSD_D_INNER] = d_bc.astype(dz_ref.dtype)
        dz_ref[:, 3 * SSD_D_INNER:] = d_dt.astype(dz_ref.dtype)
        d_state[...] = g_st
        d_hx[...] = g_hx * live
        d_hb[...] = g_hb * live
        for k in range(SSD_CONV):
            dcw_ref[k:k + 1, 0:SSD_D_INNER] += d_par[k]
            dcw_ref[k:k + 1, SSD_D_INNER:2 * SSD_D_INNER] += d_par[SSD_CONV + k]
        dcb_ref[:, 0:SSD_D_INNER] += d_par[8]
        dcb_ref[:, SSD_D_INNER:2 * SSD_D_INNER] += d_par[9]
        ddtb_ref[...] += d_par[10]
        dalog_ref[...] += d_par[11]
        ddsk_ref[...] += d_par[12]
        dng_ref[...] += d_par[13]

        if ride is not None:
            @pl.when(i == nc - 1)
            def _():
                rider.finish(*ride)

    const_specs = [pl.BlockSpec(c.shape, lambda i: (0, 0)) for c in consts]
    r_in, r_out, r_shapes, r_scratch = _rider_args(rider)
    return pl.pallas_call(
        body, name=name,
        out_shape=[jax.ShapeDtypeStruct((lp, SSD_IN_PAD), BF16)]
        + [jax.ShapeDtypeStruct(c.shape, F32) for c in consts] + r_shapes,
        grid=(nc,),
        in_specs=_ssd_in_specs(True, nc)
        + [pl.BlockSpec((1, SSD_GROUPS * SSD_STATE, 256), lambda i: (nc - 1 - i, 0, 0)),
           pl.BlockSpec((CHUNK, SSD_D_INNER), lambda i: (nc - 1 - i, 0))] + const_specs + r_in,
        out_specs=[pl.BlockSpec((CHUNK, SSD_IN_PAD), lambda i: (nc - 1 - i, 0))] + const_specs + r_out,
        scratch_shapes=[pltpu.VMEM((SSD_GROUPS * SSD_STATE, 256), F32),
                        pltpu.VMEM((8, SSD_D_INNER), F32), pltpu.VMEM((8, SSD_D_INNER), F32)] + r_scratch,
        compiler_params=_cparams(dimension_semantics=("arbitrary",)),
    )(zxd, zxd, zxd, zxd, zxd, zxd, states, d_y, *consts, *(rider.operands if rider else ()))


@jax.custom_vjp
def _rot_half(x):
    lane = lax.broadcasted_iota(jnp.int32, x.shape, 1)
    lo = (lane >= MLA_NOPE) & (lane < MLA_NOPE + MLA_ROPE // 2)
    hi = (lane >= MLA_NOPE + MLA_ROPE // 2) & (lane < MLA_QK)
    down = pltpu.roll(x, HEAD_SLOT - MLA_ROPE // 2, 1)
    up = pltpu.roll(x, MLA_ROPE // 2, 1)
    return jnp.where(lo, -down, jnp.where(hi, up, 0.0))


def _rot_half_fwd(x):
    return _rot_half(x), None


def _rot_half_bwd(_, ct):
    return (-_rot_half(ct),)


_rot_half.defvjp(_rot_half_fwd, _rot_half_bwd)


def _head_norm_rope(t, gain, cos, sin):
    n = t * lax.rsqrt(jnp.sum(t * t, axis=-1, keepdims=True) * (1.0 / MLA_QK) + EPS) * gain
    return n * cos + _rot_half(n) * sin


def _qk_prep(q_raw, kn_raw, kpe, cos, sin, qg, kg):
    qs, ks = [], []
    for h in range(MLA_HEADS):
        sl = slice(h * HEAD_SLOT, (h + 1) * HEAD_SLOT)
        qs.append(_head_norm_rope(q_raw[:, sl], qg, cos, sin))
        ks.append(_head_norm_rope(kn_raw[:, sl] + kpe, kg, cos, sin))
    return jnp.concatenate(qs, axis=1), jnp.concatenate(ks, axis=1)


def _lat_norm(kv_lat, q_lat, kvg, qg):
    return _rms(kv_lat, kvg), _rms(q_lat, qg)


_NEG = -1e30
_SCALE = MLA_QK ** -0.5


STRIP = 128
_EXP2_SCALE = _SCALE * math.log2(math.e)


def _strip_mask(kind, blk, c, t):
    if kind is None:
        return None
    kpos = blk * t + c * STRIP + lax.broadcasted_iota(jnp.int32, (1, STRIP), 1)
    if kind == 'keys':
        return kpos >= NPAD
    qpos = blk * t + lax.broadcasted_iota(jnp.int32, (t, 1), 0)
    return (kpos <= qpos) & ((kpos >= NPAD) | (kpos == qpos))


def _attn_fwd(q, k, v, name, rider=None):
    lp = q.shape[0]
    t = tk = _pick(lp, (384, 256, 128))
    nb = lp // t
    hp = HEADS_PER_STEP
    wide = hp * HEAD_SLOT
    heads = [slice(a * HEAD_SLOT, (a + 1) * HEAD_SLOT) for a in range(hp)]

    def body(*refs):
        (q_ref, k_ref, v_ref, o_ref, lse_ref), ride = _rider_split(rider, refs, 3, 2, 0)
        qi = pl.program_id(1)
        if ride is not None:
            @pl.when((pl.program_id(0) == 0) & (qi == 0))
            def _():
                rider.start(*ride)

        def scores(ki):
            rows = pl.ds(pl.multiple_of(ki * tk, tk), tk)
            return tuple(lax.dot_general(q_ref[:, heads[a]], k_ref[rows, heads[a]], _NT,
                                         preferred_element_type=F32) for a in range(hp))

        def update(a, ki, carry, s, mask):
            rows = pl.ds(pl.multiple_of(ki * tk, tk), tk)
            m, acc = carry
            s = jnp.where(mask, s, _NEG)
            m_new = jnp.maximum(m, jnp.max(s, axis=-1, keepdims=True))
            alpha = jnp.exp2((m - m_new) * _EXP2_SCALE)
            p = jnp.concatenate(
                [jnp.exp2((s[:, c:c + STRIP] - m_new) * _EXP2_SCALE).astype(BF16) for c in range(0, tk, STRIP)],
                axis=1)
            acc = alpha * acc + lax.dot_general(p, v_ref[rows, heads[a]], _NN, preferred_element_type=F32)
            return m_new, acc

        init = (jnp.full((t, 1), _NEG, F32), jnp.zeros((t, HEAD_SLOT), F32))
        ones_lane = lax.broadcasted_iota(jnp.int32, (1, HEAD_SLOT), 1) == MLA_V
        key_pos = lax.broadcasted_iota(jnp.int32, (1, tk), 1)
        n_full = (qi * t) // tk

        def before(ki, state):
            carry, s = state
            s_next = scores(ki + 1)
            key_ok = ki * tk + key_pos >= NPAD
            return tuple(update(a, ki, carry[a], s[a], key_ok) for a in range(hp)), s_next

        carry, s = lax.fori_loop(0, n_full, before, ((init,) * hp, scores(0)))
        qpos = qi * t + lax.broadcasted_iota(jnp.int32, (t, tk), 0)
        kpos = n_full * tk + lax.broadcasted_iota(jnp.int32, (t, tk), 1)
        diag = (kpos <= qpos) & ((kpos >= NPAD) | (kpos == qpos))
        carry = tuple(update(a, n_full, carry[a], s[a], diag) for a in range(hp))
        for a in range(hp):
            m, acc = carry[a]
            l = jnp.sum(jnp.where(ones_lane, acc, 0.0), axis=-1, keepdims=True)
            o_ref[:, heads[a]] = jnp.where(ones_lane, 0.0, acc / l * _row_mask(qi, t))
            lse_ref[a] = m * _SCALE + jnp.log(l)

        if ride is not None:
            @pl.when((pl.program_id(0) == MLA_HEADS // hp - 1) & (qi == nb - 1))
            def _():
                rider.finish(*ride)

    qspec = pl.BlockSpec((t, wide), lambda g, i: (i, g))
    kspec = pl.BlockSpec((lp, wide), lambda g, i: (0, g))
    r_in, r_out, r_shapes, r_scratch = _rider_args(rider)
    return pl.pallas_call(
        body, name=name,
        out_shape=[jax.ShapeDtypeStruct((lp, MLA_WIDE), F32),
                   jax.ShapeDtypeStruct((MLA_HEADS, lp, 1), F32)] + r_shapes,
        grid=(MLA_HEADS // hp, nb),
        in_specs=[qspec, kspec, kspec] + r_in,
        out_specs=[qspec, pl.BlockSpec((hp, t, 1), lambda g, i: (g, i, 0))] + r_out,
        scratch_shapes=r_scratch,
        compiler_params=_cparams(dimension_semantics=("arbitrary", "arbitrary")),
    )(q, k, v, *(rider.operands if rider else ()))


def _attn_bwd(q, k, v, do, lse, delta, name, rider=None):
    lp = q.shape[0]
    t = _pick(lp, (384, 256, 128))
    nb = lp // t
    ns = t // STRIP
    hp = HEADS_PER_STEP
    wide = hp * HEAD_SLOT
    heads = [slice(a * HEAD_SLOT, (a + 1) * HEAD_SLOT) for a in range(hp)]
    log2e = math.log2(math.e)

    def body(*refs):
        own, ride = _rider_split(rider, refs, 6, 3, 4)
        (q_ref, k_ref, v_ref, do_ref, lse_ref, delta_ref, dq_ref, dk_ref, dv_ref,
         s_scr, dp_scr, p_scr, ds_scr) = own
        kj = pl.program_id(1)
        if ride is not None:
            @pl.when((pl.program_id(0) == 0) & (kj == 0))
            def _():
                rider.start(*ride)

        @pl.when(kj == 0)
        def _():
            dq_ref[...] = jnp.zeros_like(dq_ref)

        dk_ref[...] = jnp.zeros_like(dk_ref)
        dv_ref[...] = jnp.zeros_like(dv_ref)

        def tile(qi, kind):
            rows = pl.ds(pl.multiple_of(qi * t, t), t)
            for a in range(hp):
                qb, dob = q_ref[rows, heads[a]], do_ref[rows, heads[a]]
                kb, vb = k_ref[:, heads[a]], v_ref[:, heads[a]]
                s_scr[a] = lax.dot_general(qb, kb, _NT, preferred_element_type=F32)
                dp_scr[a] = lax.dot_general(dob, vb, _NT, preferred_element_type=F32)
                lse2 = lse_ref[a, rows, :] * log2e
                delta = delta_ref[a, rows, :]
                for c in range(ns):
                    cs = slice(c * STRIP, (c + 1) * STRIP)
                    pc = jnp.exp2(s_scr[a, :, cs] * _EXP2_SCALE - lse2)
                    pc = jnp.where(_strip_mask(kind, kj, c, t), pc, 0.0)
                    p_scr[a, :, cs] = pc.astype(BF16)
                    ds_scr[a, :, cs] = (pc * (dp_scr[a, :, cs] - delta)).astype(BF16)
                dq_ref[rows, heads[a]] += lax.dot_general(ds_scr[a], kb, _NN,
                                                          preferred_element_type=F32) * _SCALE
                dv_ref[:, heads[a]] += lax.dot_general(p_scr[a], dob, _TN, preferred_element_type=F32)
                dk_ref[:, heads[a]] += lax.dot_general(ds_scr[a], qb, _TN, preferred_element_type=F32)

        tile(kj, 'diag')

        def below(qi, carry):
            tile(qi, 'keys')
            return carry

        lax.fori_loop(kj + 1, nb, below, 0)
        dk_ref[...] = dk_ref[...] * _SCALE

        if ride is not None:
            @pl.when((pl.program_id(0) == MLA_HEADS // hp - 1) & (kj == nb - 1))
            def _():
                rider.finish(*ride)

    whole = pl.BlockSpec((lp, wide), lambda g, j: (0, g))
    kspec = pl.BlockSpec((t, wide), lambda g, j: (j, g))
    stat = pl.BlockSpec((hp, lp, 1), lambda g, j: (g, 0, 0))
    r_in, r_out, r_shapes, r_scratch = _rider_args(rider)
    return pl.pallas_call(
        body, name=name,
        out_shape=[jax.ShapeDtypeStruct((lp, MLA_WIDE), F32)] * 3 + r_shapes,
        grid=(MLA_HEADS // hp, nb),
        in_specs=[whole, kspec, kspec, whole, stat, stat] + r_in,
        out_specs=[whole, kspec, kspec] + r_out,
        scratch_shapes=[pltpu.VMEM((hp, t, t), F32), pltpu.VMEM((hp, t, t), F32),
                        pltpu.VMEM((hp, t, t), BF16), pltpu.VMEM((hp, t, t), BF16)] + r_scratch,
        compiler_params=_cparams(dimension_semantics=("arbitrary", "arbitrary")),
    )(q, k, v, do, lse, delta, *(rider.operands if rider else ()))


def _rope_tables(lp):
    inv = 1.0 / (ROPE_THETA ** (jnp.arange(0, MLA_ROPE, 2, dtype=F32) / MLA_ROPE))
    pos = jnp.maximum(jnp.arange(lp, dtype=jnp.int32) - NPAD, 0).astype(F32)
    ang = pos[:, None] * inv[None, :]
    cos, sin = jnp.cos(ang), jnp.sin(ang)
    z32 = jnp.zeros((lp, HEAD_SLOT - MLA_QK), F32)
    cos_t = jnp.concatenate([jnp.ones((lp, MLA_NOPE), F32), cos, cos, z32], axis=1)
    sin_t = jnp.concatenate([jnp.zeros((lp, MLA_NOPE), F32), sin, sin, z32], axis=1)
    return cos_t, sin_t


def _loss_head(h, target, name):
    lp = h.shape[0]

    def body(h_ref, t_ref, d_ref, loss_ref):
        i = pl.program_id(0)

        @pl.when(i == 0)
        def _():
            d_ref[...] = jnp.zeros_like(d_ref)
            loss_ref[...] = jnp.zeros_like(loss_ref)

        @pl.when(i > 0)
        def _():
            err = h_ref[...] - t_ref[...]
            d_ref[...] = err * (1.0 / D_MODEL)
            loss_ref[...] += jnp.sum(err * err, axis=0, keepdims=True) * (0.5 / D_MODEL)

    return pl.pallas_call(
        body, name=name,
        out_shape=[jax.ShapeDtypeStruct((lp, D_MODEL), F32), jax.ShapeDtypeStruct((1, D_MODEL), F32)],
        grid=(lp // CHUNK,),
        in_specs=[pl.BlockSpec((CHUNK, D_MODEL), lambda i: (i, 0)),
                  pl.BlockSpec((CHUNK, D_MODEL), lambda i: (jnp.maximum(i - 1, 0), 0))],
        out_specs=[pl.BlockSpec((CHUNK, D_MODEL), lambda i: (i, 0)),
                   pl.BlockSpec((1, D_MODEL), lambda i: (0, 0))],
        compiler_params=_cparams(dimension_semantics=("arbitrary",)),
    )(h, target)


def _pad_cols(w, n):
    return jnp.pad(w, [(0, 0)] * (w.ndim - 1) + [(0, n - w.shape[-1])])


def _layer_slab(name, i):
    return i if name.startswith('mlp_') else i // 2


def _prep_layer(p, i, get):
    j = i // 2
    if i % 2 == 0:
        p['ssd_in'][j] = _pad_cols(get('ssd_w_in'), SSD_IN_PAD).astype(BF16)
        p['ssd_out'][j] = get('ssd_w_out').astype(BF16)
    else:
        wi = get('mla_w_in')
        kpe = jnp.pad(wi[:, MLA_Q_RANK + MLA_KV_RANK:], ((0, 0), (MLA_NOPE, HEAD_SLOT - MLA_QK)))
        p['mla_in'][j] = jnp.concatenate(
            [wi[:, MLA_Q_RANK:MLA_Q_RANK + MLA_KV_RANK], kpe, wi[:, :MLA_Q_RANK]], axis=1).astype(BF16)
        qb = get('mla_w_q_b').reshape(MLA_Q_RANK, MLA_HEADS, MLA_QK)
        p['mla_qb'][j] = _pad_cols(qb, HEAD_SLOT).reshape(MLA_Q_RANK, MLA_WIDE).astype(BF16)
        kvb = get('mla_w_kv_b').reshape(MLA_KV_RANK, MLA_HEADS, MLA_NOPE + MLA_V)
        kn = _pad_cols(kvb[:, :, :MLA_NOPE], HEAD_SLOT).reshape(MLA_KV_RANK, MLA_WIDE)
        vv = _pad_cols(kvb[:, :, MLA_NOPE:], HEAD_SLOT).reshape(MLA_KV_RANK, MLA_WIDE)
        p['mla_kvb'][j] = jnp.concatenate([kn, vv], axis=1).astype(BF16)
        wo = get('mla_w_out').reshape(MLA_HEADS, MLA_V, D_MODEL)
        p['mla_out'][j] = (jnp.pad(wo, ((0, 0), (0, HEAD_SLOT - MLA_V), (0, 0)))
                           .reshape(MLA_WIDE, D_MODEL).astype(BF16))
    p['up'][i] = get('mlp_w_up').astype(BF16)
    p['down'][i] = get('mlp_w_down').astype(BF16)


def _no_matrices():
    return {k: [None] * n for k, n in (('ssd_in', 2), ('ssd_out', 2), ('mla_in', 2), ('mla_qb', 2),
                                       ('mla_kvb', 2), ('mla_out', 2), ('up', 4), ('down', 4))}


class _ReadyWeights:
    def __init__(self, w):
        self.w, self.p = w, _no_matrices()

    def ensure(self, i):
        _prep_layer(self.p, i, lambda n: self.w[n][_layer_slab(n, i)])

    def rider(self, i):
        return None

    def deliver(self, i, outs):
        assert not outs


class _KeepGrads:
    def __init__(self):
        self.rounds = {}

    def begin(self, r, grads):
        self.rounds[r] = grads
        return None

    def finish(self, r, outs):
        assert not outs

    def result(self):
        names = {n for g in self.rounds.values() for n in g}
        return {n: jnp.concatenate([self.rounds[r][n] for r in sorted(self.rounds, reverse=True)
                                    if n in self.rounds[r]], axis=0) for n in names}


def _pad128(v):
    return _pad_cols(v.reshape(1, -1), 128)


def _sqrelu(u):
    r = jnp.maximum(u, 0.0)
    return r * r


def _local_step(x, target, w, big=None, red=None):
    seq = x.shape[0]
    lp = NPAD + N_META + seq
    big = _ReadyWeights(w) if big is None else big
    p = big.p
    h = jnp.concatenate([jnp.zeros((NPAD, D_MODEL), F32), w['meta_tokens'], x], axis=0)
    cos_t, sin_t = _rope_tables(lp)
    rt = _pick(lp, (384, 256, 128))
    saved = []
    for i in range(4):
        j = i // 2
        big.ensure(i)
        s = {'h0': h}
        g_mix = w['ln_mix'][i].reshape(1, -1)
        g_mlp = w['ln_mlp'][i].reshape(1, -1)
        if i == 0:
            hn = _rms_fwd(h, g_mix, f"rms_mix_f{i}")
        s['hn'] = hn
        if i % 2 == 0:
            zxd = _mm(hn, p['ssd_in'][j], 'nn', name=f"ssd_in_f{i}")
            consts = _ssd_consts(w['ssd_conv_w'][j], w['ssd_conv_b'][j].reshape(1, -1),
                                 _pad128(w['ssd_dt_bias'][j]), _pad128(w['ssd_a_log'][j]),
                                 _pad128(w['ssd_d'][j]), w['ssd_norm'][j].reshape(1, -1))
            yg, states, *got = _ssd_fwd(zxd, consts, f"ssd_core_f{i}", rider=big.rider(i))
            big.deliver(i, got)
            s.update(zxd=zxd, consts=consts, yg=yg, states=states)
            h, hn2 = _mm(yg, p['ssd_out'][j], 'nn', name=f"ssd_out_f{i}", epi=lambda r, hv: hv + r,
                         extras=(h,), norm_gain=g_mlp)
        else:
            lat = _mm(hn, p['mla_in'][j], 'nn', name=f"mla_in_f{i}")
            kvg = w['mla_kv_a_norm'][j].reshape(1, -1)
            qag = w['mla_q_a_norm'][j].reshape(1, -1)
            kvn, qn = _row_call(lambda _, a, b, c, d: _lat_norm(a, b, c, d),
                                [(lat, MLA_KV_RANK, 0), (lat, MLA_Q_RANK, 1)], [kvg, qag],
                                [(MLA_KV_RANK, BF16), (MLA_Q_RANK, BF16)], n_rows=lp, tile=rt,
                                name=f"mla_latnorm_f{i}")
            q_raw = _mm(qn, p['mla_qb'][j], 'nn', name=f"mla_qb_f{i}")
            kv_raw = _mm(kvn, p['mla_kvb'][j], 'nn', name=f"mla_kvb_f{i}")
            qg = _pad_cols(w['mla_q_norm'][j].reshape(1, -1), HEAD_SLOT)
            kg = _pad_cols(w['mla_k_norm'][j].reshape(1, -1), HEAD_SLOT)

            def prep_fwd(_, qr, kn, kpe, vv, cs, sn, qgv, kgv):
                qq, kk = _qk_prep(qr, kn, kpe, cs, sn, qgv, kgv)
                ones = lax.broadcasted_iota(jnp.int32, vv.shape, 1) % HEAD_SLOT == MLA_V
                return qq, kk, jnp.where(ones, 1.0, vv)

            q, k, v = _row_call(prep_fwd,
                                [(q_raw, MLA_WIDE, 0), (kv_raw, MLA_WIDE, 0), (lat, HEAD_SLOT, 2),
                                 (kv_raw, MLA_WIDE, 1), (cos_t, HEAD_SLOT, 0), (sin_t, HEAD_SLOT, 0)],
                                [qg, kg], [(MLA_WIDE, BF16)] * 3, n_rows=lp, tile=rt,
                                name=f"mla_qkprep_f{i}")
            o, lse, *got = _attn_fwd(q, k, v, f"mla_attn_f{i}", rider=big.rider(i))
            big.deliver(i, got)
            s.update(lat=lat, kvg=kvg, qag=qag, kvn=kvn, qn=qn, q_raw=q_raw, kv_raw=kv_raw, qg=qg, kg=kg,
                     q=q, k=k, v=v, o=o, lse=lse)
            h, hn2 = _mm(o, p['mla_out'][j], 'nn', name=f"mla_out_f{i}", epi=lambda r, hv: hv + r,
                         extras=(h,), norm_gain=g_mlp)
        s['h1'] = h
        u = _mm(hn2, p['up'][i], 'nn', name=f"mlp_up_f{i}", out_dtype=BF16)
        if i < 3:
            h, hn = _mm(u, p['down'][i], 'nn', name=f"mlp_down_f{i}", a_fn=_sqrelu, epi=lambda r, hv: hv + r,
                        extras=(h,), norm_gain=w['ln_mix'][i + 1].reshape(1, -1))
        else:
            h = _mm(u, p['down'][i], 'nn', name=f"mlp_down_f{i}", a_fn=_sqrelu,
                    epi=lambda r, hv: hv + r, extras=(h,))
        s.update(hn2=hn2, u=u, g_mix=g_mix, g_mlp=g_mlp)
        saved.append(s)

    dh, loss_row = _loss_head(h, target, "loss_head")

    large = {n for n, _ in BIG}
    g = {k_: [None] * (4 if k_ in ('ln_mix', 'ln_mlp') else 2)
         for k_ in ALL_NAMES if k_ != 'meta_tokens' and k_ not in large}
    red = _KeepGrads() if red is None else red
    rounds, pending = {}, None

    def round_of(nm, i):
        return next(r for r, spec in enumerate(REDUCE_ROUNDS)
                    for n, l0, l1 in spec if n == nm and l0 <= _layer_slab(nm, i) < l1)

    def slabs_in(nm, r):
        return next((l0, l1) for n, l0, l1 in REDUCE_ROUNDS[r] if n == nm)

    def dw_into(nm, i, a, b, **kw):
        r = round_of(nm, i)
        (l0, l1), cur = slabs_in(nm, r), rounds.setdefault(r, {})
        cur[nm] = _mm(a, b, 'tn', stack=(l1 - l0, _layer_slab(nm, i) - l0, cur.get(nm)), **kw)

    def put(nm, i, arr):
        rounds.setdefault(round_of(nm, i), {})[nm] = arr[None]

    def hand_over(r):
        nonlocal pending
        pending = (r, red.begin(r, rounds.pop(r)))

    def host(fn, *args):
        nonlocal pending
        if pending is None or pending[1] is None:
            return fn(*args)
        (r, rider), pending = pending, None
        outs = fn(*args, rider=rider)
        own = len(outs) - len(rider.out_shapes)
        red.finish(r, outs[own:])
        return outs[:own]

    for i in reversed(range(4)):
        j = i // 2
        s = saved[i]
        dw_into('mlp_w_down', i, s['u'], dh, name=f"mlp_down_dw{i}", a_fn=_sqrelu)
        du = _mm(dh, p['down'][i], 'nt', name=f"mlp_down_dx{i}", out_dtype=BF16,
                 epi=lambda r, uv: r * (2.0 * jnp.maximum(uv, 0.0)), extras=(s['u'],))
        dw_into('mlp_w_up', i, s['hn2'], du, name=f"mlp_up_dw{i}")
        dh, dg = _mm_rms_bwd(du, p['up'][i], s['h1'], dh, s['g_mlp'], f"mlp_up_dx{i}")
        g['ln_mlp'][i] = dg[0]
        if i % 2 == 0:
            dw_into('ssd_w_out', i, s['yg'], dh, name=f"ssd_out_dw{i}")
            d_yg = _mm(dh, p['ssd_out'][j], 'nt', name=f"ssd_out_dx{i}")
            if i == 0:
                hand_over(1)
            d_zxd, dcw, dcb, ddtb, dalog, ddsk, dng = host(_ssd_bwd, s['zxd'], s['states'], d_yg, s['consts'],
                                                           f"ssd_core_b{i}")
            g['ssd_conv_w'][j], g['ssd_conv_b'][j], g['ssd_norm'][j] = dcw, dcb[0], dng[0]
            g['ssd_dt_bias'][j], g['ssd_a_log'][j], g['ssd_d'][j] = (
                ddtb[0, :SSD_HEADS], dalog[0, :SSD_HEADS], ddsk[0, :SSD_HEADS])
            dw_into('ssd_w_in', i, s['hn'], d_zxd, name=f"ssd_in_dw{i}")
            dh, dg = _mm_rms_bwd(d_zxd, p['ssd_in'][j], s['h0'], dh, s['g_mix'], f"ssd_in_dx{i}")
        else:
            wo = _mm(s['o'], dh, 'tn', name=f"mla_out_dw{i}")
            put('mla_w_out', i, wo.reshape(MLA_HEADS, HEAD_SLOT, D_MODEL)[:, :MLA_V].reshape(-1, D_MODEL))
            dob, delta = _mm_attn_do(dh, p['mla_out'][j], s['o'], f"mla_out_dx{i}")
            dq, dk, dv = host(_attn_bwd, s['q'], s['k'], s['v'], dob, s['lse'], delta, f"mla_attn_b{i}")

            def prep_bwd(_, qr, kn, kpe, cs, sn, dqv, dkv, dvv, qgv, kgv):
                _, vjp = jax.vjp(lambda a, b, c, d, e: _qk_prep(a, b, c, cs, sn, d, e), qr, kn, kpe, qgv, kgv)
                d_qr, d_kn, d_kpe, d_qg, d_kg = vjp((dqv, dkv))
                return d_qr, jnp.concatenate([d_kn, dvv], axis=1), d_kpe, d_qg, d_kg

            d_qraw, d_kvraw, d_kpe, d_qg, d_kg = _row_call(
                prep_bwd,
                [(s['q_raw'], MLA_WIDE, 0), (s['kv_raw'], MLA_WIDE, 0), (s['lat'], HEAD_SLOT, 2),
                 (cos_t, HEAD_SLOT, 0), (sin_t, HEAD_SLOT, 0), (dq, MLA_WIDE, 0), (dk, MLA_WIDE, 0),
                 (dv, MLA_WIDE, 0)],
                [s['qg'], s['kg']], [(MLA_WIDE, BF16), (2 * MLA_WIDE, BF16), (HEAD_SLOT, F32)],
                [(1, HEAD_SLOT), (1, HEAD_SLOT)], n_rows=lp, tile=_pick(lp, (128,)), name=f"mla_qkprep_b{i}")
            g['mla_q_norm'][j], g['mla_k_norm'][j] = d_qg[0, :MLA_QK], d_kg[0, :MLA_QK]
            wqb = _mm(s['qn'], d_qraw, 'tn', name=f"mla_qb_dw{i}")
            put('mla_w_q_b', i, wqb.reshape(MLA_Q_RANK, MLA_HEADS, HEAD_SLOT)[:, :, :MLA_QK].reshape(MLA_Q_RANK, -1))
            d_qn = _mm(d_qraw, p['mla_qb'][j], 'nt', name=f"mla_qb_dx{i}")
            wkvb = _mm(s['kvn'], d_kvraw, 'tn', name=f"mla_kvb_dw{i}").reshape(MLA_KV_RANK, 2, MLA_HEADS, HEAD_SLOT)
            put('mla_w_kv_b', i, jnp.concatenate([wkvb[:, 0, :, :MLA_NOPE], wkvb[:, 1, :, :MLA_V]],
                                                 axis=-1).reshape(MLA_KV_RANK, -1))
            d_kvn = _mm(d_kvraw, p['mla_kvb'][j], 'nt', name=f"mla_kvb_dx{i}")

            def lat_bwd(_, kvl, ql, dkvn, dqn, dkpe, kvgv, qagv):
                _, vjp = jax.vjp(_lat_norm, kvl, ql, kvgv, qagv)
                d_kvl, d_ql, d_kvg, d_qag = vjp((dkvn, dqn))
                return jnp.concatenate([d_kvl, dkpe, d_ql], axis=1), d_kvg, d_qag

            d_lat, d_kvg, d_qag = _row_call(
                lat_bwd, [(s['lat'], MLA_KV_RANK, 0), (s['lat'], MLA_Q_RANK, 1), (d_kvn, MLA_KV_RANK, 0),
                          (d_qn, MLA_Q_RANK, 0), (d_kpe, HEAD_SLOT, 0)],
                [s['kvg'], s['qag']], [(LAT_PAD, BF16)], [(1, MLA_KV_RANK), (1, MLA_Q_RANK)],
                n_rows=lp, tile=rt, name=f"mla_latnorm_b{i}")
            g['mla_kv_a_norm'][j], g['mla_q_a_norm'][j] = d_kvg[0], d_qag[0]
            win = _mm(s['hn'], d_lat, 'tn', name=f"mla_in_dw{i}")
            put('mla_w_in', i, jnp.concatenate(
                [win[:, MLA_KV_RANK + HEAD_SLOT:], win[:, :MLA_KV_RANK],
                 win[:, MLA_KV_RANK + MLA_NOPE:MLA_KV_RANK + MLA_QK]], axis=1))
            dh, dg = _mm_rms_bwd(d_lat, p['mla_in'][j], s['h0'], dh, s['g_mix'], f"mla_in_dx{i}")
        g['ln_mix'][i] = dg[0]
        if i == 2:
            hand_over(0)
    hand_over(2)

    if pending[1] is not None:
        red.finish(pending[0], _run_rider(pending[1], "rs_exchange_last"))
    grads = {k_: jnp.stack(v_) for k_, v_ in g.items()}
    grads['meta_tokens'] = dh[NPAD:NPAD + N_META]
    return loss_row, dh[NPAD + N_META:], grads, red


def _all_gather8(shard, name):
    m_per, n = shard.shape

    def body(x_ref, out_ref, send_sems, recv_sems, local_sem):
        x, y, c = lax.axis_index("x"), lax.axis_index("y"), lax.axis_index("c")
        me, sibling = (x, y, c), (x, y, 1 - c)
        chips = [(1 - x, y), (x, 1 - y), (1 - x, 1 - y)]

        def rows(px, py, pc):
            return out_ref.at[pl.ds((4 * px + 2 * py + pc) * m_per, m_per), :]

        def copy(k, block, to, src=None):
            return pltpu.make_async_remote_copy(
                src_ref=rows(*block) if src is None else src, dst_ref=rows(*block),
                send_sem=send_sems.at[k], recv_sem=recv_sems.at[k], device_id=to, device_id_type=MESH)

        mine = pltpu.make_async_copy(x_ref, rows(*me), local_sem)
        mine.start()
        first = [copy(0, me, sibling, src=x_ref)]
        first += [copy(1 + j, me, (*chip, c), src=x_ref) for j, chip in enumerate(chips)]
        for cp in first:
            cp.start()
        passed = [copy(4 + j, (*chip, c), sibling) for j, chip in enumerate(chips)]
        for j, chip in enumerate(chips):
            copy(1 + j, (*chip, c), me).wait_recv()
            passed[j].start()
        copy(0, sibling, me).wait_recv()
        for j, chip in enumerate(chips):
            copy(4 + j, (*chip, 1 - c), me).wait_recv()
        for cp in first + passed:
            cp.wait_send()
        mine.wait()

    return pl.pallas_call(
        body, name=name,
        out_shape=jax.ShapeDtypeStruct((8 * m_per, n), shard.dtype),
        in_specs=[pl.BlockSpec(memory_space=pl.ANY)],
        out_specs=pl.BlockSpec(memory_space=pl.ANY),
        scratch_shapes=[pltpu.SemaphoreType.DMA((7,)), pltpu.SemaphoreType.DMA((7,)), pltpu.SemaphoreType.DMA],
    )(shard)


def _mesh_pos():
    return lax.axis_index("x"), lax.axis_index("y"), lax.axis_index("c")


def _half_rows(pc, h):
    return pl.ds(pl.multiple_of(pc * h, 16), h)


def _whole_view(ref, kind, shard_shape, k, pc):
    _, r, c = shard_shape
    rows = _half_rows(pc, r // 2)
    if kind == 'row':
        return ref.at[:, k, rows, :]
    if kind == 'col':
        return ref.at[:, rows, pl.ds(pl.multiple_of(k * c, 128), c)]
    return ref.at[k, :, rows, :]


def _whole_shape(kind, shard_shape, rows=None):
    l, r, c = shard_shape
    r = r if rows is None else rows
    return {'row': (l, 4, r, c), 'col': (l, r, 4 * c), 'colx': (4, l, r, c)}[kind]


def _gather_rider(shards, kinds):
    n = len(shards)
    shapes = [s.shape for s in shards]

    def plan(ins, outs, sems):
        send_sems, recv_sems, local_sems = sems
        x, y, c = _mesh_pos()
        me, sibling = (x, y, c), (x, y, 1 - c)
        chips = [(1 - x, y), (x, 1 - y), (1 - x, 1 - y)]

        def place(a, px, py, pc):
            return _whole_view(outs[a], kinds[a], shapes[a], 2 * px + py, pc)

        def own(a):
            return ins[a].at[:, _half_rows(c, shapes[a][1] // 2), :]

        def copy(a, k, block, to, src=None):
            return pltpu.make_async_remote_copy(
                src_ref=place(a, *block) if src is None else src, dst_ref=place(a, *block),
                send_sem=send_sems.at[7 * a + k], recv_sem=recv_sems.at[7 * a + k],
                device_id=to, device_id_type=MESH)

        mine = [pltpu.make_async_copy(own(a), place(a, *me), local_sems.at[a]) for a in range(n)]
        first = [copy(a, 1 + j, me, (*chip, c), src=own(a)) for j, chip in enumerate(chips) for a in range(n)]
        first += [copy(a, 0, me, sibling, src=own(a)) for a in range(n)]
        return copy, mine, first, chips, me, sibling, c

    def start(ins, outs, sems):
        _, mine, first, *_ = plan(ins, outs, sems)
        for cp in first + mine:
            cp.start()

    def finish(ins, outs, sems):
        copy, mine, first, chips, me, sibling, c = plan(ins, outs, sems)
        passed = []
        for j, chip in enumerate(chips):
            for a in range(n):
                copy(a, 1 + j, (*chip, c), me).wait_recv()
                passed.append(copy(a, 4 + j, (*chip, c), sibling))
                passed[-1].start()
        for a in range(n):
            copy(a, 0, sibling, me).wait_recv()
        for j, chip in enumerate(chips):
            for a in range(n):
                copy(a, 4 + j, (*chip, 1 - c), me).wait_recv()
        for cp in first + passed:
            cp.wait_send()
        for cp in mine:
            cp.wait()

    return _Rider(
        shards, [jax.ShapeDtypeStruct(_whole_shape(k, s.shape), s.dtype) for k, s in zip(kinds, shards)],
        [pltpu.SemaphoreType.DMA((7 * n,)), pltpu.SemaphoreType.DMA((7 * n,)), pltpu.SemaphoreType.DMA((n,))],
        start, finish)


def _run_rider(rider, name):
    ni, no = len(rider.operands), len(rider.out_shapes)

    def body(*refs):
        ride = (refs[:ni], refs[ni:ni + no], refs[ni + no:])
        rider.start(*ride)
        rider.finish(*ride)

    return pl.pallas_call(
        body, name=name, out_shape=rider.out_shapes,
        in_specs=[pl.BlockSpec(memory_space=pl.ANY)] * ni,
        out_specs=[pl.BlockSpec(memory_space=pl.ANY)] * no,
        scratch_shapes=rider.scratch,
    )(*rider.operands)


def _rs_swap(wholes, kinds, shapes, name):
    n = len(wholes)

    def body(*refs):
        ins, outs = refs[:n], refs[n:2 * n]
        send_sems, recv_sems = refs[2 * n:]
        x, y, c = _mesh_pos()
        cps = []
        for a in range(n):
            rows = _half_rows(1 - c, shapes[a][1] // 2)
            src = ins[a].at[:, rows, :] if kinds[a] == 'col' else ins[a].at[:, :, rows, :]
            cps.append(pltpu.make_async_remote_copy(
                src_ref=src, dst_ref=outs[a], send_sem=send_sems.at[a], recv_sem=recv_sems.at[a],
                device_id=(x, y, 1 - c), device_id_type=MESH))
        for cp in cps:
            cp.start()
        for cp in cps:
            cp.wait()

    return pl.pallas_call(
        body, name=name,
        out_shape=[jax.ShapeDtypeStruct(_whole_shape(k, s, s[1] // 2), w.dtype)
                   for k, s, w in zip(kinds, shapes, wholes)],
        in_specs=[pl.BlockSpec(memory_space=pl.ANY)] * n,
        out_specs=[pl.BlockSpec(memory_space=pl.ANY)] * n,
        scratch_shapes=[pltpu.SemaphoreType.DMA((n,)), pltpu.SemaphoreType.DMA((n,))],
    )(*wholes)


def _exchange_rider(parts, kinds, shapes):
    n = len(parts)

    def plan(ins, outs, sems):
        send_sems, recv_sems, local_sems = sems
        x, y, c = _mesh_pos()
        kme = 2 * x + y
        chips = [(1 - x, y), (x, 1 - y), (1 - x, 1 - y)]

        def slab(a, k):
            if kinds[a] == 'row':
                return ins[a].at[:, k]
            if kinds[a] == 'col':
                cw = shapes[a][2]
                return ins[a].at[:, :, pl.ds(pl.multiple_of(k * cw, 128), cw)]
            return ins[a].at[k]

        cps = [pltpu.make_async_remote_copy(
            src_ref=slab(a, 2 * px + py), dst_ref=outs[a].at[kme], send_sem=send_sems.at[3 * a + j],
            recv_sem=recv_sems.at[3 * a + j], device_id=(px, py, c), device_id_type=MESH)
            for j, (px, py) in enumerate(chips) for a in range(n)]
        return cps + [pltpu.make_async_copy(slab(a, kme), outs[a].at[kme], local_sems.at[a]) for a in range(n)]

    def start(ins, outs, sems):
        for cp in plan(ins, outs, sems):
            cp.start()

    def finish(ins, outs, sems):
        for cp in plan(ins, outs, sems):
            cp.wait()

    return _Rider(
        parts, [jax.ShapeDtypeStruct((4, s[0], s[1] // 2, s[2]), p.dtype) for s, p in zip(shapes, parts)],
        [pltpu.SemaphoreType.DMA((3 * n,)), pltpu.SemaphoreType.DMA((3 * n,)), pltpu.SemaphoreType.DMA((n,))],
        start, finish)


def _rs_share(shards, slabs, name):
    n = len(shards)

    def body(*refs):
        outs = refs[n:2 * n]
        send_sems, recv_sems = refs[2 * n:]
        x, y, c = _mesh_pos()
        cps = []
        for a in range(n):
            l0, l1 = slabs[a]
            rows = outs[a].at[pl.ds(l0, l1 - l0), _half_rows(c, shards[a].shape[1] // 2), :]
            cps.append(pltpu.make_async_remote_copy(
                src_ref=rows, dst_ref=rows, send_sem=send_sems.at[a], recv_sem=recv_sems.at[a],
                device_id=(x, y, 1 - c), device_id_type=MESH))
        for cp in cps:
            cp.start()
        for cp in cps:
            cp.wait()

    return pl.pallas_call(
        body, name=name,
        out_shape=[jax.ShapeDtypeStruct(s.shape, s.dtype) for s in shards],
        in_specs=[pl.BlockSpec(memory_space=pl.ANY)] * n,
        out_specs=[pl.BlockSpec(memory_space=pl.ANY)] * n,
        input_output_aliases={a: a for a in range(n)},
        scratch_shapes=[pltpu.SemaphoreType.DMA((n,)), pltpu.SemaphoreType.DMA((n,))],
    )(*shards)


def _tile_rows(rows, cols, budget=2 * 1024 * 1024):
    for t in (1024, 512, 256, 128, 64, 32, 16, 8):
        if rows % t == 0 and t * cols * 4 <= budget:
            return t
    return rows


def _add_half(g3, r3, c_idx, name):
    a, h, n = r3.shape
    t = _tile_rows(h, n)
    nt = h // t

    def body(c_ref, g_ref, r_ref, o_ref):
        o_ref[...] = (g_ref[...] + r_ref[...]).astype(o_ref.dtype)

    return pl.pallas_call(
        body, name=name, out_shape=jax.ShapeDtypeStruct((a, h, n), BF16),
        grid_spec=pltpu.PrefetchScalarGridSpec(
            num_scalar_prefetch=1, grid=(a, nt),
            in_specs=[pl.BlockSpec((1, t, n), lambda k, i, c: (k, c[0] * nt + i, 0)),
                      pl.BlockSpec((1, t, n), lambda k, i, c: (k, i, 0))],
            out_specs=pl.BlockSpec((1, t, n), lambda k, i, c: (k, i, 0))),
        compiler_params=_cparams(dimension_semantics=("parallel", "parallel")),
    )(c_idx, g3, r3)


def _sum4(parts, c_idx, name, into):
    _, l, h, n = parts.shape
    n_slabs, l0, buf = into
    t = _tile_rows(h, n, 1024 * 1024)
    nt = h // t
    held = () if buf is None else (buf,)

    def body(c_ref, p_ref, *rest):
        pv = p_ref[...].astype(F32)
        rest[-1][...] = ((pv[0] + pv[1]) + pv[2]) + pv[3]

    return pl.pallas_call(
        body, name=name, out_shape=jax.ShapeDtypeStruct((n_slabs, 2 * h, n), F32),
        grid_spec=pltpu.PrefetchScalarGridSpec(
            num_scalar_prefetch=1, grid=(l, nt),
            in_specs=[pl.BlockSpec((4, 1, t, n), lambda k, i, c: (0, k, i, 0))]
            + [pl.BlockSpec(memory_space=pl.ANY)] * len(held),
            out_specs=pl.BlockSpec((1, t, n), lambda k, i, c: (l0 + k, c[0] * nt + i, 0))),
        input_output_aliases={2: 0} if held else {},
        compiler_params=_cparams(dimension_semantics=("parallel", "parallel")),
    )(c_idx, parts, *held)


def _sum8(parts, name):
    _, m, n = parts.shape

    def body(p_ref, o_ref):
        acc = p_ref[0]
        for d in range(1, 8):
            acc = acc + p_ref[d]
        o_ref[...] = acc

    return pl.pallas_call(body, name=name, out_shape=jax.ShapeDtypeStruct((m, n), F32))(parts)


def _adamw(wp, gp, mp, vp, name):
    r, n = wp.shape
    t = _tile_rows(r, n, 1024 * 1024)

    def body(w_ref, g_ref, m_ref, v_ref, d_ref, mo_ref, vo_ref):
        gv = g_ref[...]
        m2 = ADAM_B1 * m_ref[...] + (1.0 - ADAM_B1) * gv
        v2 = ADAM_B2 * v_ref[...] + (1.0 - ADAM_B2) * (gv * gv)
        m_hat = m2 / (1.0 - ADAM_B1 ** ADAM_STEP)
        v_hat = v2 / (1.0 - ADAM_B2 ** ADAM_STEP)
        d_ref[...] = -ADAM_LR * (m_hat / (jnp.sqrt(v_hat) + ADAM_EPS) + ADAM_WD * w_ref[...])
        mo_ref[...] = m2
        vo_ref[...] = v2

    spec = pl.BlockSpec((t, n), lambda i: (i, 0))
    return pl.pallas_call(
        body, name=name, out_shape=[jax.ShapeDtypeStruct((r, n), F32)] * 3, grid=(r // t,),
        in_specs=[spec] * 4, out_specs=[spec] * 3,
        compiler_params=_cparams(dimension_semantics=("parallel",)),
    )(wp, gp, mp, vp)


BIG = (('ssd_w_in', 'colx'), ('ssd_w_out', 'row'), ('mla_w_in', 'row'), ('mla_w_q_b', 'col'),
       ('mla_w_kv_b', 'col'), ('mla_w_out', 'row'), ('mlp_w_up', 'col'), ('mlp_w_down', 'row'))
SMALL_SHARDED = (('meta_tokens', 1), ('ssd_conv_w', 2), ('mla_q_a_norm', 1), ('mla_kv_a_norm', 1))
SMALL_REPL = ('ln_mix', 'ln_mlp', 'ssd_conv_b', 'ssd_dt_bias', 'ssd_a_log', 'ssd_d', 'ssd_norm',
              'mla_q_norm', 'mla_k_norm')
ALL_NAMES = ('meta_tokens', 'ln_mix', 'ln_mlp', 'ssd_w_in', 'ssd_conv_w', 'ssd_conv_b', 'ssd_dt_bias',
             'ssd_a_log', 'ssd_d', 'ssd_norm', 'ssd_w_out', 'mla_w_in', 'mla_q_a_norm', 'mla_w_q_b',
             'mla_kv_a_norm', 'mla_w_kv_b', 'mla_q_norm', 'mla_k_norm', 'mla_w_out', 'mlp_w_up', 'mlp_w_down')


_MLA_BIG = ('mla_w_in', 'mla_w_q_b', 'mla_w_kv_b', 'mla_w_out')
GATHER_ROUNDS = (
    (('ssd_w_in', 0, 1), ('ssd_w_out', 0, 1), ('mlp_w_up', 0, 1), ('mlp_w_down', 0, 1)),
    tuple((n, 0, 1) for n in _MLA_BIG) + (('mlp_w_up', 1, 2), ('mlp_w_down', 1, 2)),
    (('ssd_w_in', 1, 2), ('ssd_w_out', 1, 2)) + tuple((n, 1, 2) for n in _MLA_BIG)
    + (('mlp_w_up', 2, 4), ('mlp_w_down', 2, 4)),
)


REDUCE_ROUNDS = (
    GATHER_ROUNDS[2],
    tuple((n, 0, 1) for n in _MLA_BIG) + (('mlp_w_up', 0, 2), ('mlp_w_down', 0, 2), ('ssd_w_out', 0, 1)),
    (('ssd_w_in', 0, 1),),
)


class _GatheredWeights:
    def __init__(self, shards):
        self.shards, self.p, self.whole = shards, _no_matrices(), {}

    def _round(self, r):
        spec = GATHER_ROUNDS[r]
        return _gather_rider([self.shards[n][l0:l1] for n, l0, l1 in spec], [dict(BIG)[n] for n, _, _ in spec])

    def _take(self, r, outs):
        for (n, l0, l1), o in zip(GATHER_ROUNDS[r], outs):
            kind = dict(BIG)[n]
            for l in range(l0, l1):
                if kind == 'row':
                    m = o[l - l0].reshape(-1, o.shape[-1])
                elif kind == 'col':
                    m = o[l - l0]
                else:
                    m = jnp.concatenate([o[k, l - l0] for k in range(4)], axis=-1)
                self.whole[(n, l)] = m

    def ensure(self, i):
        if i == 0:
            self._take(0, _run_rider(self._round(0), "gather_first"))
        _prep_layer(self.p, i, lambda n: self.whole[(n, _layer_slab(n, i))])

    def rider(self, i):
        return self._round(i + 1) if i + 1 < len(GATHER_ROUNDS) else None

    def deliver(self, i, outs):
        if i + 1 < len(GATHER_ROUNDS):
            self._take(i + 1, outs)


class _ScatterGrads:
    def __init__(self, shard_shapes, c_idx):
        self.shard_shapes, self.c_idx, self.out = shard_shapes, c_idx, {}

    def begin(self, r, grads):
        spec = REDUCE_ROUNDS[r]
        kinds = [dict(BIG)[n] for n, _, _ in spec]
        shapes = [(l1 - l0,) + tuple(self.shard_shapes[n][1:]) for n, l0, l1 in spec]
        wholes = []
        for (n, _, _), kind, s in zip(spec, kinds, shapes):
            if kind == 'row':
                wholes.append(grads[n].reshape(s[0], 4, s[1], s[2]))
            elif kind == 'col':
                wholes.append(grads[n])
            else:
                wholes.append(jnp.stack([grads[n][..., k * s[2]:(k + 1) * s[2]] for k in range(4)]))
        recv = _rs_swap(wholes, kinds, shapes, f"rs_swap{r}")
        parts = []
        for (n, _, _), kind, s, gw, rc in zip(spec, kinds, shapes, wholes, recv):
            if kind == 'col':
                g3, r3 = gw, rc
            else:
                g3, r3 = gw.reshape(-1, s[1], s[2]), rc.reshape(-1, s[1] // 2, s[2])
            parts.append(_add_half(g3, r3, self.c_idx, f"rs_add{r}_{n}").reshape(rc.shape))
        return _exchange_rider(parts, kinds, shapes)

    def finish(self, r, outs):
        spec = REDUCE_ROUNDS[r]
        for (n, l0, _), part in zip(spec, outs):
            self.out[n] = _sum4(part, self.c_idx, f"rs_sum{r}_{n}",
                                into=(self.shard_shapes[n][0], l0, self.out.get(n)))
        shared = _rs_share([self.out[n] for n, _, _ in spec], [(l0, l1) for _, l0, l1 in spec], f"rs_share{r}")
        self.out.update(zip([n for n, _, _ in spec], shared))


def _pack(arrs, rows_mult):
    flat = jnp.concatenate([a.reshape(-1) for a in arrs])
    per = LANES * rows_mult
    pad = (-flat.shape[0]) % per
    if pad:
        flat = jnp.concatenate([flat, jnp.zeros((pad,), flat.dtype)])
    return flat.reshape(-1, LANES)


def _unpack(pack, shapes):
    flat = pack.reshape(-1)
    out, off = [], 0
    for shp in shapes:
        n = math.prod(shp)
        out.append(flat[off:off + n].reshape(shp))
        off += n
    return out


def _split4(full, axis):
    shp = full.shape
    r = full.reshape(shp[:axis] + (4, shp[axis] // 4) + shp[axis + 1:])
    return jnp.moveaxis(r, axis, 0)


def _join4(parts, axis):
    r = jnp.moveaxis(parts, 0, axis)
    shp = r.shape
    return r.reshape(shp[:axis] + (shp[axis] * shp[axis + 1],) + shp[axis + 2:])


def _gather_params(shards, table, dtype, c, name):
    pack = _pack([shards[n].astype(dtype) for n, _ in table], 16)
    half = pack.shape[0] // 2
    mine = lax.dynamic_slice_in_dim(pack, c * half, half, axis=0)
    full = _all_gather8(mine, name).reshape(4, -1)
    out, off = {}, 0
    for n, ax in table:
        cnt = math.prod(shards[n].shape)
        out[n] = _join4(full[:, off:off + cnt].reshape((4,) + shards[n].shape), ax)
        off += cnt
    return out


def kernel(x, meta_tokens, ln_mix, ln_mlp, ssd_w_in, ssd_conv_w, ssd_conv_b, ssd_dt_bias, ssd_a_log, ssd_d, ssd_norm, ssd_w_out, mla_w_in, mla_q_a_norm, mla_w_q_b, mla_kv_a_norm, mla_w_kv_b, mla_q_norm, mla_k_norm, mla_w_out, mlp_w_up, mlp_w_down, loss_target, m_meta_tokens, m_ln_mix, m_ln_mlp, m_ssd_w_in, m_ssd_conv_w, m_ssd_conv_b, m_ssd_dt_bias, m_ssd_a_log, m_ssd_d, m_ssd_norm, m_ssd_w_out, m_mla_w_in, m_mla_q_a_norm, m_mla_w_q_b, m_mla_kv_a_norm, m_mla_w_kv_b, m_mla_q_norm, m_mla_k_norm, m_mla_w_out, m_mlp_w_up, m_mlp_w_down, v_meta_tokens, v_ln_mix, v_ln_mlp, v_ssd_w_in, v_ssd_conv_w, v_ssd_conv_b, v_ssd_dt_bias, v_ssd_a_log, v_ssd_d, v_ssd_norm, v_ssd_w_out, v_mla_w_in, v_mla_q_a_norm, v_mla_w_q_b, v_mla_kv_a_norm, v_mla_w_kv_b, v_mla_q_norm, v_mla_k_norm, v_mla_w_out, v_mlp_w_up, v_mlp_w_down):
    w_sh = dict(meta_tokens=meta_tokens, ln_mix=ln_mix, ln_mlp=ln_mlp, ssd_w_in=ssd_w_in, ssd_conv_w=ssd_conv_w, ssd_conv_b=ssd_conv_b, ssd_dt_bias=ssd_dt_bias, ssd_a_log=ssd_a_log, ssd_d=ssd_d, ssd_norm=ssd_norm, ssd_w_out=ssd_w_out, mla_w_in=mla_w_in, mla_q_a_norm=mla_q_a_norm, mla_w_q_b=mla_w_q_b, mla_kv_a_norm=mla_kv_a_norm, mla_w_kv_b=mla_w_kv_b, mla_q_norm=mla_q_norm, mla_k_norm=mla_k_norm, mla_w_out=mla_w_out, mlp_w_up=mlp_w_up, mlp_w_down=mlp_w_down)
    m_sh = dict(meta_tokens=m_meta_tokens, ln_mix=m_ln_mix, ln_mlp=m_ln_mlp, ssd_w_in=m_ssd_w_in, ssd_conv_w=m_ssd_conv_w, ssd_conv_b=m_ssd_conv_b, ssd_dt_bias=m_ssd_dt_bias, ssd_a_log=m_ssd_a_log, ssd_d=m_ssd_d, ssd_norm=m_ssd_norm, ssd_w_out=m_ssd_w_out, mla_w_in=m_mla_w_in, mla_q_a_norm=m_mla_q_a_norm, mla_w_q_b=m_mla_w_q_b, mla_kv_a_norm=m_mla_kv_a_norm, mla_w_kv_b=m_mla_w_kv_b, mla_q_norm=m_mla_q_norm, mla_k_norm=m_mla_k_norm, mla_w_out=m_mla_w_out, mlp_w_up=m_mlp_w_up, mlp_w_down=m_mlp_w_down)
    v_sh = dict(meta_tokens=v_meta_tokens, ln_mix=v_ln_mix, ln_mlp=v_ln_mlp, ssd_w_in=v_ssd_w_in, ssd_conv_w=v_ssd_conv_w, ssd_conv_b=v_ssd_conv_b, ssd_dt_bias=v_ssd_dt_bias, ssd_a_log=v_ssd_a_log, ssd_d=v_ssd_d, ssd_norm=v_ssd_norm, ssd_w_out=v_ssd_w_out, mla_w_in=v_mla_w_in, mla_q_a_norm=v_mla_q_a_norm, mla_w_q_b=v_mla_w_q_b, mla_kv_a_norm=v_mla_kv_a_norm, mla_w_kv_b=v_mla_w_kv_b, mla_q_norm=v_mla_q_norm, mla_k_norm=v_mla_k_norm, mla_w_out=v_mla_w_out, mlp_w_up=v_mlp_w_up, mlp_w_down=v_mlp_w_down)

    cx, cy, cc = lax.axis_index("x"), lax.axis_index("y"), lax.axis_index("c")
    chip = 2 * cx + cy

    c_idx = cc.reshape(1).astype(jnp.int32)
    big_names = [n for n, _ in BIG]
    shapes = [w_sh[n].shape for n in big_names]

    w = {n: w_sh[n] for n in SMALL_REPL}
    w.update(_gather_params(w_sh, SMALL_SHARDED, F32, cc, "gather_small"))
    big = _GatheredWeights({n: w_sh[n].astype(BF16) for n in big_names})
    red = _ScatterGrads({n: w_sh[n].shape for n in big_names}, c_idx)

    loss_row, grad_x, grads, red = _local_step(x[0], loss_target[0], w, big, red)
    loss = lax.psum(jnp.sum(loss_row), ("x", "y", "c"))
    g_sh = dict(red.out)

    small_names = tuple(n for n, _ in SMALL_SHARDED) + SMALL_REPL
    sp = _pack([grads[n] for n in small_names], 8)
    srows = sp.shape[0]
    s_all = _sum8(_all_gather8(sp, "ar_small_gather").reshape(8, srows, LANES), "ar_small_sum")
    s_full = dict(zip(small_names, _unpack(s_all, [grads[n].shape for n in small_names])))
    for n, ax in SMALL_SHARDED:
        g_sh[n] = lax.dynamic_index_in_dim(_split4(s_full[n], ax), chip, axis=0, keepdims=False)
    for n in SMALL_REPL:
        g_sh[n] = s_full[n]

    delta, new_m, new_v = {}, {}, {}
    for n, s in zip(big_names, shapes):
        res = _adamw(*[t[n].reshape(-1, s[2]) for t in (w_sh, g_sh, m_sh, v_sh)], f"adamw_{n}")
        delta[n], new_m[n], new_v[n] = [r.reshape(s) for r in res]
    d_s, m_s, v_s = _adamw(*[_pack([t[n] for n in small_names], 8) for t in (w_sh, g_sh, m_sh, v_sh)],
                           "adamw_small")
    for dst, ps in ((delta, d_s), (new_m, m_s), (new_v, v_s)):
        dst.update(zip(small_names, _unpack(ps, [w_sh[n].shape for n in small_names])))

    return (loss, grad_x[None], *[g_sh[n] for n in ALL_NAMES], *[delta[n] for n in ALL_NAMES],
            *[new_m[n] for n in ALL_NAMES], *[new_v[n] for n in ALL_NAMES])
```

```python
import functools
import math

import jax
import jax.numpy as jnp
from jax import lax
from jax.experimental import pallas as pl
from jax.experimental.pallas import tpu as pltpu

F32 = jnp.float32
BF16 = jnp.bfloat16
MESH = pl.DeviceIdType.MESH
_NN = (((1,), (0,)), ((), ()))
_NT = (((1,), (1,)), ((), ()))
_TN = (((0,), (0,)), ((), ()))

D_MODEL = 1024
N_META = 16
EPS = 1e-6
SSD_D_INNER = 2048
SSD_HEADS = 32
SSD_HEAD_DIM = 64
SSD_GROUPS = 8
SSD_HPG = 4
SSD_STATE = 128
SSD_CONV = 4
CHUNK = 128
SSD_IN_DIM = 6176
SSD_IN_PAD = 6272
MLA_HEADS = 16
MLA_NOPE = 64
MLA_ROPE = 32
MLA_V = 64
MLA_QK = 96
MLA_Q_RANK = 384
MLA_KV_RANK = 256
HEAD_SLOT = 128
MLA_WIDE = MLA_HEADS * HEAD_SLOT
HEADS_PER_STEP = 2
LAT_PAD = 768
ROPE_THETA = 10000.0
D_FF = 4096
NPAD = CHUNK - N_META
ADAM_LR, ADAM_B1, ADAM_B2, ADAM_EPS, ADAM_WD, ADAM_STEP = 0.001, 0.9, 0.999, 1e-08, 0.01, 10
LANES = 1024
V7X_VMEM_BYTES = 64 * 1024 * 1024
VMEM_LIMIT = V7X_VMEM_BYTES * 7 // 8


def _pick(n, cands):
    for c in cands:
        if n % c == 0:
            return c
    return n


def _cparams(**kw):
    return pltpu.CompilerParams(vmem_limit_bytes=VMEM_LIMIT, **kw)


def _mm(a, b, dims, *, name, out_dtype=F32, a_fn=None, epi=None, extras=(), stack=None, norm_gain=None,
        rider=None):
    if dims == 'nn':
        (M, K), (K2, N) = a.shape, b.shape
    elif dims == 'nt':
        (M, K), (N, K2) = a.shape, b.shape
    else:
        (K, M), (K2, N) = a.shape, b.shape
    assert K == K2, (a.shape, b.shape, dims)
    if dims == 'tn':
        tm = _pick(M, (1024, 768, 512, 384, 256, 128))
        tn = _pick(N, (1024, 896, 768, 512, 384, 256, 128))
        tk = _pick(K, (1408, 1024, 512, 384, 256, 128))
    else:
        tm = _pick(M, (704, 512, 384, 256, 128) if norm_gain is not None else (1408, 1024, 512, 384, 256, 128))
        tn = _pick(N, (1024, 896, 768, 512, 384, 256, 128))
        tk = _pick(K, (1024, 896, 768, 512, 384, 256, 128))
    nk = K // tk
    if dims == 'nn':
        a_spec = pl.BlockSpec((tm, tk), lambda i, j, k: (i, k))
        b_spec = pl.BlockSpec((tk, tn), lambda i, j, k: (k, j))
        dn = (((1,), (0,)), ((), ()))
    elif dims == 'nt':
        a_spec = pl.BlockSpec((tm, tk), lambda i, j, k: (i, k))
        b_spec = pl.BlockSpec((tn, tk), lambda i, j, k: (j, k))
        dn = (((1,), (1,)), ((), ()))
    else:
        a_spec = pl.BlockSpec((tk, tm), lambda i, j, k: (k, i))
        b_spec = pl.BlockSpec((tk, tn), lambda i, j, k: (k, j))
        dn = (((0,), (0,)), ((), ()))
    o_spec = pl.BlockSpec((tm, tn), lambda i, j, k: (i, j))
    n_ex = len(extras)
    out_shape = jax.ShapeDtypeStruct((M, N), out_dtype)
    out_spec, held, aliases = o_spec, (), {}
    if stack is not None:
        n_slabs, slab, buf = stack
        out_shape = jax.ShapeDtypeStruct((n_slabs, M, N), out_dtype)
        out_spec = pl.BlockSpec((None, tm, tn), lambda i, j, k: (slab, i, j))
        if buf is not None:
            held, aliases = (buf,), {2 + n_ex: 0}

    gains = ()
    if norm_gain is not None:
        assert tn == N and stack is None, "the rms epilogue needs whole rows"
        gains = (norm_gain,)
        out_shape = [out_shape, jax.ShapeDtypeStruct((M, N), BF16)]
        out_spec = [out_spec, o_spec]

    n_own_in = 2 + n_ex + len(gains) + len(held)
    steps = (M // tm, N // tn, nk)

    def body(*refs):
        (a_ref, b_ref, *rest), ride = _rider_split(rider, refs, n_own_in, 1 + len(gains), 1)
        ex_refs, rest = rest[:n_ex], rest[n_ex:]
        g_refs, rest = rest[:len(gains)], rest[len(gains) + len(held):]
        o_ref, acc = rest[0], rest[-1]
        k = pl.program_id(2)
        if ride is not None:
            @pl.when((pl.program_id(0) == 0) & (pl.program_id(1) == 0) & (k == 0))
            def _():
                rider.start(*ride)

        @pl.when(k == 0)
        def _():
            acc[...] = jnp.zeros_like(acc)

        av = a_ref[...]
        if a_fn is not None:
            av = a_fn(av)
        acc[...] += lax.dot_general(av.astype(BF16), b_ref[...].astype(BF16), dn,
                                    preferred_element_type=F32)

        @pl.when(k == nk - 1)
        def _():
            r = acc[...]
            if epi is not None:
                r = epi(r, *[e[...] for e in ex_refs])
            o_ref[...] = r.astype(out_dtype)
            if gains:
                rest[1][...] = _rms(r, g_refs[0][...]).astype(BF16)

        if ride is not None:
            @pl.when((pl.program_id(0) == steps[0] - 1) & (pl.program_id(1) == steps[1] - 1) & (k == nk - 1))
            def _():
                rider.finish(*ride)

    r_in, r_out, r_shapes, r_scratch = _rider_args(rider)
    own_shapes = out_shape if isinstance(out_shape, list) else [out_shape]
    own_specs = out_spec if isinstance(out_spec, list) else [out_spec]
    res = pl.pallas_call(
        body, name=name,
        out_shape=own_shapes + r_shapes,
        grid=steps,
        in_specs=[a_spec, b_spec] + [o_spec] * n_ex
        + [pl.BlockSpec((1, tn), lambda i, j, k: (0, j))] * len(gains)
        + [pl.BlockSpec(memory_space=pl.ANY)] * len(held) + r_in,
        out_specs=own_specs + r_out,
        input_output_aliases=aliases,
        scratch_shapes=[pltpu.VMEM((tm, tn), F32)] + r_scratch,
        compiler_params=_cparams(dimension_semantics=("arbitrary", "arbitrary", "arbitrary")),
    )(a, b, *extras, *gains, *held, *(rider.operands if rider else ()))
    return res[0] if len(res) == 1 else res


def _mm_rms_bwd(cot, w_t, h, d_res, gain, name):
    (M, K), (N, _) = cot.shape, w_t.shape
    tm = _pick(M, (704, 512, 384, 256, 128))
    tk = _pick(K, (1024, 896, 768, 512, 384, 256, 128))
    nk = K // tk

    def body(a_ref, b_ref, h_ref, r_ref, g_ref, dh_ref, dg_ref, acc):
        i, k = pl.program_id(0), pl.program_id(1)

        @pl.when(k == 0)
        def _():
            acc[...] = jnp.zeros_like(acc)

        acc[...] += lax.dot_general(a_ref[...].astype(BF16), b_ref[...].astype(BF16), _NT,
                                    preferred_element_type=F32)

        @pl.when(k == nk - 1)
        def _():
            _, vjp = jax.vjp(_rms, h_ref[...], g_ref[...])
            dh, dg = vjp(acc[...])
            dh_ref[...] = (r_ref[...] + dh) * _row_mask(i, tm)

            @pl.when(i == 0)
            def _():
                dg_ref[...] = dg

            @pl.when(i > 0)
            def _():
                dg_ref[...] += dg

    rows = pl.BlockSpec((tm, N), lambda i, k: (i, 0))
    vec = pl.BlockSpec((1, N), lambda i, k: (0, 0))
    return pl.pallas_call(
        body, name=name,
        out_shape=[jax.ShapeDtypeStruct((M, N), F32), jax.ShapeDtypeStruct((1, N), F32)],
        grid=(M // tm, nk),
        in_specs=[pl.BlockSpec((tm, tk), lambda i, k: (i, k)), pl.BlockSpec((N, tk), lambda i, k: (0, k)),
                  rows, rows, vec],
        out_specs=[rows, vec],
        scratch_shapes=[pltpu.VMEM((tm, N), F32)],
        compiler_params=_cparams(dimension_semantics=("arbitrary", "arbitrary")),
    )(cot, w_t, h, d_res, gain)


def _mm_attn_do(dh, w_out_t, o, name):
    (M, K), (N, _) = dh.shape, w_out_t.shape
    tm = _pick(M, (704, 512, 384, 256, 128))
    tn = 8 * HEAD_SLOT

    def body(a_ref, b_ref, o_ref, dob_ref, delta_ref):
        do = lax.dot_general(a_ref[...].astype(BF16), b_ref[...], _NT, preferred_element_type=F32)
        dob_ref[...] = do.astype(BF16)
        for hh in range(tn // HEAD_SLOT):
            sl = slice(hh * HEAD_SLOT, (hh + 1) * HEAD_SLOT)
            delta_ref[hh] = jnp.sum(do[:, sl] * o_ref[:, sl], axis=-1, keepdims=True)

    tile = pl.BlockSpec((tm, tn), lambda i, j: (i, j))
    return pl.pallas_call(
        body, name=name,
        out_shape=[jax.ShapeDtypeStruct((M, N), BF16), jax.ShapeDtypeStruct((N // HEAD_SLOT, M, 1), F32)],
        grid=(M // tm, N // tn),
        in_specs=[pl.BlockSpec((tm, K), lambda i, j: (i, 0)), pl.BlockSpec((tn, K), lambda i, j: (j, 0)), tile],
        out_specs=[tile, pl.BlockSpec((tn // HEAD_SLOT, tm, 1), lambda i, j: (j, i, 0))],
        compiler_params=_cparams(dimension_semantics=("parallel", "parallel")),
    )(dh, w_out_t, o)


def _row_call(fn, rows, consts, out_rows, out_accs=(), *, n_rows, tile, name):
    n_r, n_c, n_o, n_a = len(rows), len(consts), len(out_rows), len(out_accs)
    steps = n_rows // tile

    def body(*refs):
        r_refs = refs[:n_r]
        c_refs = refs[n_r:n_r + n_c]
        o_refs = refs[n_r + n_c:n_r + n_c + n_o]
        a_refs = refs[n_r + n_c + n_o:]
        i = pl.program_id(0)
        res = fn(i, *[r[...] for r in r_refs], *[c[...] for c in c_refs])
        for o_ref, val in zip(o_refs, res[:n_o]):
            o_ref[...] = val.astype(o_ref.dtype)

        @pl.when(i == 0)
        def _():
            for a_ref in a_refs:
                a_ref[...] = jnp.zeros_like(a_ref)

        for a_ref, val in zip(a_refs, res[n_o:]):
            a_ref[...] += val

    in_specs = [pl.BlockSpec((tile, w), functools.partial(lambda i, cb: (i, cb), cb=cb))
                for (_, w, cb) in rows]
    in_specs += [pl.BlockSpec(c.shape, lambda i: (0, 0)) for c in consts]
    out_specs = [pl.BlockSpec((tile, c), lambda i: (i, 0)) for (c, _) in out_rows]
    out_specs += [pl.BlockSpec(s, lambda i: (0, 0)) for s in out_accs]
    out_shape = [jax.ShapeDtypeStruct((n_rows, c), dt) for (c, dt) in out_rows]
    out_shape += [jax.ShapeDtypeStruct(s, F32) for s in out_accs]
    return pl.pallas_call(
        body, name=name, out_shape=out_shape, grid=(steps,),
        in_specs=in_specs, out_specs=out_specs,
        compiler_params=_cparams(dimension_semantics=("arbitrary",)),
    )(*[r[0] for r in rows], *consts)


def _row_mask(i, tile):
    r = i * tile + lax.broadcasted_iota(jnp.int32, (tile, 1), 0)
    return (r >= NPAD).astype(F32)


def _rms(x, g):
    return x * lax.rsqrt(jnp.mean(x * x, axis=-1, keepdims=True) + EPS) * g


def _silu(x):
    return x * (0.5 * jnp.tanh(0.5 * x) + 0.5)


def _softplus(x):
    return jnp.maximum(x, 0.0) + jnp.log(1.0 + jnp.exp(-jnp.abs(x)))


def _rms_fwd(h, g, name):
    lp = h.shape[0]
    return _row_call(lambda i, hv, gv: (_rms(hv, gv),), [(h, D_MODEL, 0)], [g],
                     [(D_MODEL, BF16)], n_rows=lp, tile=_pick(lp, (384, 256, 128)), name=name)[0]


@functools.partial(jax.custom_vjp, nondiff_argnums=(1,))
def _roll_rows(x, s):
    return pltpu.roll(x, s, 0)


def _roll_rows_fwd(x, s):
    return pltpu.roll(x, s, 0), None


def _roll_rows_bwd(s, _, ct):
    return (pltpu.roll(ct, (ct.shape[0] - s) % ct.shape[0], 0),)


_roll_rows.defvjp(_roll_rows_fwd, _roll_rows_bwd)


def _conv_silu(cur, halo, w_rows, b):
    full = jnp.concatenate([halo, cur], axis=0)
    acc = cur * w_rows[SSD_CONV - 1] + b
    for k in range(SSD_CONV - 1):
        acc = acc + _roll_rows(full, SSD_CONV - 1 - k)[8:] * w_rows[k]
    return _silu(acc)


def _split3(v):
    hi = v.astype(BF16)
    r1 = v - hi.astype(F32)
    mid = r1.astype(BF16)
    lo = (r1 - mid.astype(F32)).astype(BF16)
    return hi, mid, lo


def _select_right(v, sel, dn):
    return sum(lax.dot_general(p, sel, dn, preferred_element_type=F32) for p in _split3(v))


@jax.custom_vjp
def _expand_heads(v, e_mat):
    return _select_right(v, e_mat, _NN)


def _expand_heads_fwd(v, e_mat):
    return _select_right(v, e_mat, _NN), e_mat


def _expand_heads_bwd(e_mat, ct):
    return _select_right(ct, e_mat, _NT), jnp.zeros_like(e_mat)


_expand_heads.defvjp(_expand_heads_fwd, _expand_heads_bwd)


@jax.custom_vjp
def _cumsum_rows(a, tri):
    return sum(lax.dot_general(tri, p, _NN, preferred_element_type=F32) for p in _split3(a))


def _cumsum_rows_fwd(a, tri):
    return _cumsum_rows(a, tri), tri


def _cumsum_rows_bwd(tri, ct):
    return (sum(lax.dot_general(tri, p, _TN, preferred_element_type=F32) for p in _split3(ct)),
            jnp.zeros_like(tri))


_cumsum_rows.defvjp(_cumsum_rows_fwd, _cumsum_rows_bwd)


def _ssd_chunk(mask, z, xs_pre, bc_pre, halo_x, halo_bc, dt_pre, st, cwx0, cwx1, cwx2, cwx3,
               cwb0, cwb1, cwb2, cwb3, cb_x, cb_bc, dtb, alog, dsk, ng):
    L = CHUNK
    lane_h = lax.broadcasted_iota(jnp.int32, (1, 128), 1)
    head_ok = (lane_h < SSD_HEADS).astype(F32)
    e_mat = (lax.broadcasted_iota(jnp.int32, (128, SSD_D_INNER), 1) // SSD_HEAD_DIM
             == lax.broadcasted_iota(jnp.int32, (128, SSD_D_INNER), 0)).astype(BF16)
    ri = lax.broadcasted_iota(jnp.int32, (L, L), 0)
    ci = lax.broadcasted_iota(jnp.int32, (L, L), 1)
    causal = ri >= ci

    xs = _conv_silu(xs_pre, halo_x, (cwx0, cwx1, cwx2, cwx3), cb_x) * mask
    bc = _conv_silu(bc_pre, halo_bc, (cwb0, cwb1, cwb2, cwb3), cb_bc) * mask
    dt = _softplus(dt_pre + dtb) * mask * head_ok
    a_dt = dt * (-jnp.exp(alog))
    a_cs = _cumsum_rows(a_dt, causal.astype(BF16))
    a_cs_t = a_cs.T
    row8 = lax.broadcasted_iota(jnp.int32, (8, 128), 0)
    last8 = jnp.where(row8 == 0, jnp.sum(a_dt, axis=0, keepdims=True), 0.0)
    dsk8 = jnp.where(row8 == 0, dsk, 0.0)
    wide = _expand_heads(jnp.concatenate([dt, a_cs, last8, dsk8], axis=0), e_mat)
    dt_e, acs_e = wide[0:L], wide[L:2 * L]
    last_e = jnp.sum(wide[2 * L:2 * L + 8], axis=0, keepdims=True)
    d_e = jnp.sum(wide[2 * L + 8:2 * L + 16], axis=0, keepdims=True)
    xdt = xs * dt_e
    dte_e = jnp.exp(last_e - acs_e)
    dfs_e = jnp.exp(acs_e)
    cd_e = jnp.exp(last_e)
    sub_h = lax.broadcasted_iota(jnp.int32, (128, L), 0)
    lane_hl = lax.broadcasted_iota(jnp.int32, (L, 128), 1)
    lane_g = lax.broadcasted_iota(jnp.int32, (1, SSD_HPG * SSD_HEAD_DIM), 1) // SSD_HEAD_DIM

    ys, new_st = [], []
    for g in range(SSD_GROUPS):
        b_g = bc[:, g * 128:(g + 1) * 128].astype(BF16)
        c_g = bc[:, 1024 + g * 128:1024 + (g + 1) * 128].astype(BF16)
        gs = slice(g * 256, (g + 1) * 256)
        xdt_g = xdt[:, gs]
        cb = lax.dot_general(c_g, b_g, (((1,), (1,)), ((), ())), preferred_element_type=F32)
        st_g = st[g * 128:(g + 1) * 128, :]
        y_g = lax.dot_general(c_g, st_g.astype(BF16), (((1,), (0,)), ((), ())),
                              preferred_element_type=F32) * dfs_e[:, gs]
        for j in range(SSD_HPG):
            h = g * SSD_HPG + j
            col = jnp.sum(jnp.where(lane_hl == h, a_cs, 0.0), axis=1, keepdims=True)
            row = jnp.sum(jnp.where(sub_h == h, a_cs_t, 0.0), axis=0, keepdims=True)
            dec = jnp.where(causal, jnp.exp(jnp.where(causal, col - row, 0.0)), 0.0)
            m_h = (cb * dec).astype(BF16)
            x_h = jnp.where(lane_g == j, xdt_g, 0.0).astype(BF16)
            y_g = y_g + lax.dot_general(m_h, x_h, (((1,), (0,)), ((), ())),
                                        preferred_element_type=F32)
        s_new = lax.dot_general(b_g, (xdt_g * dte_e[:, gs]).astype(BF16), (((0,), (0,)), ((), ())),
                                preferred_element_type=F32)
        new_st.append(st_g * cd_e[:, gs] + s_new)
        ys.append(y_g)
    y = jnp.concatenate(ys, axis=1) + xs * d_e
    gg = y * _silu(z)
    outs = []
    for g in range(SSD_GROUPS):
        sl = gg[:, g * 256:(g + 1) * 256]
        outs.append(sl * lax.rsqrt(jnp.mean(sl * sl, axis=-1, keepdims=True) + EPS))
    out = jnp.concatenate(outs, axis=1) * ng
    return out, jnp.concatenate(new_st, axis=0)


def _ssd_consts(conv_w, conv_b, dtb, alog, dsk, ng):
    return [conv_w, conv_b, dtb, alog, dsk, ng]


def _ssd_param_vals(cw_ref, cb_ref, dtb_ref, alog_ref, dsk_ref, ng_ref):
    cwx = [cw_ref[k:k + 1, 0:SSD_D_INNER] for k in range(SSD_CONV)]
    cwb = [cw_ref[k:k + 1, SSD_D_INNER:2 * SSD_D_INNER] for k in range(SSD_CONV)]
    return (*cwx, *cwb, cb_ref[:, 0:SSD_D_INNER], cb_ref[:, SSD_D_INNER:2 * SSD_D_INNER],
            dtb_ref[...], alog_ref[...], dsk_ref[...], ng_ref[...])


def _ssd_in_specs(rev, nc):
    def cidx(i):
        return (nc - 1 - i) if rev else i

    def halo(cb):
        return pl.BlockSpec((8, SSD_D_INNER), lambda i: (jnp.maximum(16 * cidx(i) - 1, 0), cb))

    return [
        pl.BlockSpec((CHUNK, SSD_D_INNER), lambda i: (cidx(i), 0)),
        pl.BlockSpec((CHUNK, SSD_D_INNER), lambda i: (cidx(i), 1)),
        pl.BlockSpec((CHUNK, SSD_D_INNER), lambda i: (cidx(i), 2)),
        halo(1), halo(2),
        pl.BlockSpec((CHUNK, 128), lambda i: (cidx(i), 48)),
    ]


class _Rider:
    def __init__(self, operands, out_shapes, scratch, start, finish):
        self.operands, self.out_shapes, self.scratch = list(operands), list(out_shapes), list(scratch)
        self.start, self.finish = start, finish


def _rider_split(rider, refs, n_in, n_out, n_scratch):
    if rider is None:
        return refs, None
    ni, no = len(rider.operands), len(rider.out_shapes)
    own = refs[:n_in] + refs[n_in + ni:n_in + ni + n_out] + refs[n_in + ni + n_out + no:n_in + ni + n_out + no + n_scratch]
    mine = (refs[n_in:n_in + ni], refs[n_in + ni + n_out:n_in + ni + n_out + no],
            refs[n_in + ni + n_out + no + n_scratch:])
    return own, mine


def _rider_args(rider):
    if rider is None:
        return [], [], [], []
    hbm = pl.BlockSpec(memory_space=pl.ANY)
    return ([hbm] * len(rider.operands), [hbm] * len(rider.out_shapes), rider.out_shapes, rider.scratch)


def _ssd_fwd(zxd, consts, name, rider=None):
    lp = zxd.shape[0]
    nc = lp // CHUNK

    def body(*refs):
        own, ride = _rider_split(rider, refs, 12, 2, 1)
        (z_ref, xs_ref, bc_ref, hx_ref, hb_ref, dt_ref, cw_ref, cb_ref, dtb_ref, alog_ref,
         dsk_ref, ng_ref, y_ref, st_ref, state) = own
        c = pl.program_id(0)

        @pl.when(c == 0)
        def _():
            state[...] = jnp.zeros_like(state)
            if ride is not None:
                rider.start(*ride)

        live = (c > 0).astype(F32)
        st_ref[0] = state[...]
        out, st_new = _ssd_chunk(
            _row_mask(c, CHUNK), z_ref[...], xs_ref[...], bc_ref[...], hx_ref[...] * live,
            hb_ref[...] * live, dt_ref[...], state[...],
            *_ssd_param_vals(cw_ref, cb_ref, dtb_ref, alog_ref, dsk_ref, ng_ref))
        y_ref[...] = out.astype(y_ref.dtype)
        state[...] = st_new

        if ride is not None:
            @pl.when(c == nc - 1)
            def _():
                rider.finish(*ride)

    r_in, r_out, r_shapes, r_scratch = _rider_args(rider)
    return pl.pallas_call(
        body, name=name,
        out_shape=[jax.ShapeDtypeStruct((lp, SSD_D_INNER), BF16),
                   jax.ShapeDtypeStruct((nc, SSD_GROUPS * SSD_STATE, 256), F32)] + r_shapes,
        grid=(nc,),
        in_specs=_ssd_in_specs(False, nc) + [pl.BlockSpec(c.shape, lambda i: (0, 0)) for c in consts] + r_in,
        out_specs=[pl.BlockSpec((CHUNK, SSD_D_INNER), lambda i: (i, 0)),
                   pl.BlockSpec((1, SSD_GROUPS * SSD_STATE, 256), lambda i: (i, 0, 0))] + r_out,
        scratch_shapes=[pltpu.VMEM((SSD_GROUPS * SSD_STATE, 256), F32)] + r_scratch,
        compiler_params=_cparams(dimension_semantics=("arbitrary",)),
    )(zxd, zxd, zxd, zxd, zxd, zxd, *consts, *(rider.operands if rider else ()))


def _ssd_bwd(zxd, states, d_y, consts, name, rider=None):
    lp = zxd.shape[0]
    nc = lp // CHUNK

    def body(*refs):
        own, ride = _rider_split(rider, refs, 14, 7, 3)
        (z_ref, xs_ref, bc_ref, hx_ref, hb_ref, dt_ref, st_ref, dy_ref, cw_ref, cb_ref, dtb_ref,
         alog_ref, dsk_ref, ng_ref, dz_ref, dcw_ref, dcb_ref, ddtb_ref, dalog_ref, ddsk_ref,
         dng_ref, d_state, d_hx, d_hb) = own
        i = pl.program_id(0)
        c = nc - 1 - i

        @pl.when(i == 0)
        def _():
            d_state[...] = jnp.zeros_like(d_state)
            d_hx[...] = jnp.zeros_like(d_hx)
            d_hb[...] = jnp.zeros_like(d_hb)
            for r in (dcw_ref, dcb_ref, ddtb_ref, dalog_ref, ddsk_ref, dng_ref):
                r[...] = jnp.zeros_like(r)
            if ride is not None:
                rider.start(*ride)

        live = (c > 0).astype(F32)
        fn = functools.partial(_ssd_chunk, _row_mask(c, CHUNK))
        prim = (z_ref[...], xs_ref[...], bc_ref[...], hx_ref[...] * live, hb_ref[...] * live,
                dt_ref[...], st_ref[0],
                *_ssd_param_vals(cw_ref, cb_ref, dtb_ref, alog_ref, dsk_ref, ng_ref))
        _, vjp = jax.vjp(fn, *prim)
        (d_z, d_xs, d_bc, g_hx, g_hb, d_dt, g_st, *d_par) = vjp((dy_ref[...], d_state[...]))
        zeros = jnp.zeros((CHUNK - 8, SSD_D_INNER), F32)
        d_xs = d_xs + jnp.concatenate([zeros, d_hx[...]], axis=0)
        d_bc = d_bc + jnp.concatenate([zeros, d_hb[...]], axis=0)
        dz_ref[:, 0:SSD_D_INNER] = d_z.astype(dz_ref.dtype)
        dz_ref[:, SSD_D_INNER:2 * SSD_D_INNER] = d_xs.astype(dz_ref.dtype)
        dz_ref[:, 2 * SSD_D_INNER:3 * SSD_D_INNER] = d_bc.astype(dz_ref.dtype)
        dz_ref[:, 3 * SSD_D_INNER:] = d_dt.astype(dz_ref.dtype)
        d_state[...] = g_st
        d_hx[...] = g_hx * live
        d_hb[...] = g_hb * live
        for k in range(SSD_CONV):
            dcw_ref[k:k + 1, 0:SSD_D_INNER] += d_par[k]
            dcw_ref[k:k + 1, SSD_D_INNER:2 * SSD_D_INNER] += d_par[SSD_CONV + k]
        dcb_ref[:, 0:SSD_D_INNER] += d_par[8]
        dcb_ref[:, SSD_D_INNER:2 * SSD_D_INNER] += d_par[9]
        ddtb_ref[...] += d_par[10]
        dalog_ref[...] += d_par[11]
        ddsk_ref[...] += d_par[12]
        dng_ref[...] += d_par[13]

        if ride is not None:
            @pl.when(i == nc - 1)
            def _():
                rider.finish(*ride)

    const_specs = [pl.BlockSpec(c.shape, lambda i: (0, 0)) for c in consts]
    r_in, r_out, r_shapes, r_scratch = _rider_args(rider)
    return pl.pallas_call(
        body, name=name,
        out_shape=[jax.ShapeDtypeStruct((lp, SSD_IN_PAD), BF16)]
        + [jax.ShapeDtypeStruct(c.shape, F32) for c in consts] + r_shapes,
        grid=(nc,),
        in_specs=_ssd_in_specs(True, nc)
        + [pl.BlockSpec((1, SSD_GROUPS * SSD_STATE, 256), lambda i: (nc - 1 - i, 0, 0)),
           pl.BlockSpec((CHUNK, SSD_D_INNER), lambda i: (nc - 1 - i, 0))] + const_specs + r_in,
        out_specs=[pl.BlockSpec((CHUNK, SSD_IN_PAD), lambda i: (nc - 1 - i, 0))] + const_specs + r_out,
        scratch_shapes=[pltpu.VMEM((SSD_GROUPS * SSD_STATE, 256), F32),
                        pltpu.VMEM((8, SSD_D_INNER), F32), pltpu.VMEM((8, SSD_D_INNER), F32)] + r_scratch,
        compiler_params=_cparams(dimension_semantics=("arbitrary",)),
    )(zxd, zxd, zxd, zxd, zxd, zxd, states, d_y, *consts, *(rider.operands if rider else ()))


@jax.custom_vjp
def _rot_half(x):
    lane = lax.broadcasted_iota(jnp.int32, x.shape, 1)
    lo = (lane >= MLA_NOPE) & (lane < MLA_NOPE + MLA_ROPE // 2)
    hi = (lane >= MLA_NOPE + MLA_ROPE // 2) & (lane < MLA_QK)
    down = pltpu.roll(x, HEAD_SLOT - MLA_ROPE // 2, 1)
    up = pltpu.roll(x, MLA_ROPE // 2, 1)
    return jnp.where(lo, -down, jnp.where(hi, up, 0.0))


def _rot_half_fwd(x):
    return _rot_half(x), None


def _rot_half_bwd(_, ct):
    return (-_rot_half(ct),)


_rot_half.defvjp(_rot_half_fwd, _rot_half_bwd)


def _head_norm_rope(t, gain, cos, sin):
    n = t * lax.rsqrt(jnp.sum(t * t, axis=-1, keepdims=True) * (1.0 / MLA_QK) + EPS) * gain
    return n * cos + _rot_half(n) * sin


def _qk_prep(q_raw, kn_raw, kpe, cos, sin, qg, kg):
    qs, ks = [], []
    for h in range(MLA_HEADS):
        sl = slice(h * HEAD_SLOT, (h + 1) * HEAD_SLOT)
        qs.append(_head_norm_rope(q_raw[:, sl], qg, cos, sin))
        ks.append(_head_norm_rope(kn_raw[:, sl] + kpe, kg, cos, sin))
    return jnp.concatenate(qs, axis=1), jnp.concatenate(ks, axis=1)


def _lat_norm(kv_lat, q_lat, kvg, qg):
    return _rms(kv_lat, kvg), _rms(q_lat, qg)


_NEG = -1e30
_SCALE = MLA_QK ** -0.5


STRIP = 128
_EXP2_SCALE = _SCALE * math.log2(math.e)


def _strip_mask(kind, blk, c, t):
    if kind is None:
        return None
    kpos = blk * t + c * STRIP + lax.broadcasted_iota(jnp.int32, (1, STRIP), 1)
    if kind == 'keys':
        return kpos >= NPAD
    qpos = blk * t + lax.broadcasted_iota(jnp.int32, (t, 1), 0)
    return (kpos <= qpos) & ((kpos >= NPAD) | (kpos == qpos))


def _attn_fwd(q, k, v, name, rider=None):
    lp = q.shape[0]
    t = tk = _pick(lp, (384, 256, 128))
    nb = lp // t
    hp = HEADS_PER_STEP
    wide = hp * HEAD_SLOT
    heads = [slice(a * HEAD_SLOT, (a + 1) * HEAD_SLOT) for a in range(hp)]

    def body(*refs):
        (q_ref, k_ref, v_ref, o_ref, lse_ref), ride = _rider_split(rider, refs, 3, 2, 0)
        qi = pl.program_id(1)
        if ride is not None:
            @pl.when((pl.program_id(0) == 0) & (qi == 0))
            def _():
                rider.start(*ride)

        def scores(ki):
            rows = pl.ds(pl.multiple_of(ki * tk, tk), tk)
            return tuple(lax.dot_general(q_ref[:, heads[a]], k_ref[rows, heads[a]], _NT,
                                         preferred_element_type=F32) for a in range(hp))

        def update(a, ki, carry, s, mask):
            rows = pl.ds(pl.multiple_of(ki * tk, tk), tk)
            m, acc = carry
            s = jnp.where(mask, s, _NEG)
            m_new = jnp.maximum(m, jnp.max(s, axis=-1, keepdims=True))
            alpha = jnp.exp2((m - m_new) * _EXP2_SCALE)
            p = jnp.concatenate(
                [jnp.exp2((s[:, c:c + STRIP] - m_new) * _EXP2_SCALE).astype(BF16) for c in range(0, tk, STRIP)],
                axis=1)
            acc = alpha * acc + lax.dot_general(p, v_ref[rows, heads[a]], _NN, preferred_element_type=F32)
            return m_new, acc

        init = (jnp.full((t, 1), _NEG, F32), jnp.zeros((t, HEAD_SLOT), F32))
        ones_lane = lax.broadcasted_iota(jnp.int32, (1, HEAD_SLOT), 1) == MLA_V
        key_pos = lax.broadcasted_iota(jnp.int32, (1, tk), 1)
        n_full = (qi * t) // tk

        def before(ki, state):
            carry, s = state
            s_next = scores(ki + 1)
            key_ok = ki * tk + key_pos >= NPAD
            return tuple(update(a, ki, carry[a], s[a], key_ok) for a in range(hp)), s_next

        carry, s = lax.fori_loop(0, n_full, before, ((init,) * hp, scores(0)))
        qpos = qi * t + lax.broadcasted_iota(jnp.int32, (t, tk), 0)
        kpos = n_full * tk + lax.broadcasted_iota(jnp.int32, (t, tk), 1)
        diag = (kpos <= qpos) & ((kpos >= NPAD) | (kpos == qpos))
        carry = tuple(update(a, n_full, carry[a], s[a], diag) for a in range(hp))
        for a in range(hp):
            m, acc = carry[a]
            l = jnp.sum(jnp.where(ones_lane, acc, 0.0), axis=-1, keepdims=True)
            o_ref[:, heads[a]] = jnp.where(ones_lane, 0.0, acc / l * _row_mask(qi, t))
            lse_ref[a] = m * _SCALE + jnp.log(l)

        if ride is not None:
            @pl.when((pl.program_id(0) == MLA_HEADS // hp - 1) & (qi == nb - 1))
            def _():
                rider.finish(*ride)

    qspec = pl.BlockSpec((t, wide), lambda g, i: (i, g))
    kspec = pl.BlockSpec((lp, wide), lambda g, i: (0, g))
    r_in, r_out, r_shapes, r_scratch = _rider_args(rider)
    return pl.pallas_call(
        body, name=name,
        out_shape=[jax.ShapeDtypeStruct((lp, MLA_WIDE), F32),
                   jax.ShapeDtypeStruct((MLA_HEADS, lp, 1), F32)] + r_shapes,
        grid=(MLA_HEADS // hp, nb),
        in_specs=[qspec, kspec, kspec] + r_in,
        out_specs=[qspec, pl.BlockSpec((hp, t, 1), lambda g, i: (g, i, 0))] + r_out,
        scratch_shapes=r_scratch,
        compiler_params=_cparams(dimension_semantics=("arbitrary", "arbitrary")),
    )(q, k, v, *(rider.operands if rider else ()))


def _attn_bwd(q, k, v, do, lse, delta, name, rider=None):
    lp = q.shape[0]
    t = _pick(lp, (384, 256, 128))
    nb = lp // t
    ns = t // STRIP
    hp = HEADS_PER_STEP
    wide = hp * HEAD_SLOT
    heads = [slice(a * HEAD_SLOT, (a + 1) * HEAD_SLOT) for a in range(hp)]
    log2e = math.log2(math.e)

    def body(*refs):
        own, ride = _rider_split(rider, refs, 6, 3, 4)
        (q_ref, k_ref, v_ref, do_ref, lse_ref, delta_ref, dq_ref, dk_ref, dv_ref,
         s_scr, dp_scr, p_scr, ds_scr) = own
        kj = pl.program_id(1)
        if ride is not None:
            @pl.when((pl.program_id(0) == 0) & (kj == 0))
            def _():
                rider.start(*ride)

        @pl.when(kj == 0)
        def _():
            dq_ref[...] = jnp.zeros_like(dq_ref)

        dk_ref[...] = jnp.zeros_like(dk_ref)
        dv_ref[...] = jnp.zeros_like(dv_ref)

        def tile(qi, kind):
            rows = pl.ds(pl.multiple_of(qi * t, t), t)
            for a in range(hp):
                qb, dob = q_ref[rows, heads[a]], do_ref[rows, heads[a]]
                kb, vb = k_ref[:, heads[a]], v_ref[:, heads[a]]
                s_scr[a] = lax.dot_general(qb, kb, _NT, preferred_element_type=F32)
                dp_scr[a] = lax.dot_general(dob, vb, _NT, preferred_element_type=F32)
                lse2 = lse_ref[a, rows, :] * log2e
                delta = delta_ref[a, rows, :]
                for c in range(ns):
                    cs = slice(c * STRIP, (c + 1) * STRIP)
                    pc = jnp.exp2(s_scr[a, :, cs] * _EXP2_SCALE - lse2)
                    pc = jnp.where(_strip_mask(kind, kj, c, t), pc, 0.0)
                    p_scr[a, :, cs] = pc.astype(BF16)
                    ds_scr[a, :, cs] = (pc * (dp_scr[a, :, cs] - delta)).astype(BF16)
                dq_ref[rows, heads[a]] += lax.dot_general(ds_scr[a], kb, _NN,
                                                          preferred_element_type=F32) * _SCALE
                dv_ref[:, heads[a]] += lax.dot_general(p_scr[a], dob, _TN, preferred_element_type=F32)
                dk_ref[:, heads[a]] += lax.dot_general(ds_scr[a], qb, _TN, preferred_element_type=F32)

        tile(kj, 'diag')

        def below(qi, carry):
            tile(qi, 'keys')
            return carry

        lax.fori_loop(kj + 1, nb, below, 0)
        dk_ref[...] = dk_ref[...] * _SCALE

        if ride is not None:
            @pl.when((pl.program_id(0) == MLA_HEADS // hp - 1) & (kj == nb - 1))
            def _():
                rider.finish(*ride)

    whole = pl.BlockSpec((lp, wide), lambda g, j: (0, g))
    kspec = pl.BlockSpec((t, wide), lambda g, j: (j, g))
    stat = pl.BlockSpec((hp, lp, 1), lambda g, j: (g, 0, 0))
    r_in, r_out, r_shapes, r_scratch = _rider_args(rider)
    return pl.pallas_call(
        body, name=name,
        out_shape=[jax.ShapeDtypeStruct((lp, MLA_WIDE), F32)] * 3 + r_shapes,
        grid=(MLA_HEADS // hp, nb),
        in_specs=[whole, kspec, kspec, whole, stat, stat] + r_in,
        out_specs=[whole, kspec, kspec] + r_out,
        scratch_shapes=[pltpu.VMEM((hp, t, t), F32), pltpu.VMEM((hp, t, t), F32),
                        pltpu.VMEM((hp, t, t), BF16), pltpu.VMEM((hp, t, t), BF16)] + r_scratch,
        compiler_params=_cparams(dimension_semantics=("arbitrary", "arbitrary")),
    )(q, k, v, do, lse, delta, *(rider.operands if rider else ()))


def _rope_tables(lp):
    inv = 1.0 / (ROPE_THETA ** (jnp.arange(0, MLA_ROPE, 2, dtype=F32) / MLA_ROPE))
    pos = jnp.maximum(jnp.arange(lp, dtype=jnp.int32) - NPAD, 0).astype(F32)
    ang = pos[:, None] * inv[None, :]
    cos, sin = jnp.cos(ang), jnp.sin(ang)
    z32 = jnp.zeros((lp, HEAD_SLOT - MLA_QK), F32)
    cos_t = jnp.concatenate([jnp.ones((lp, MLA_NOPE), F32), cos, cos, z32], axis=1)
    sin_t = jnp.concatenate([jnp.zeros((lp, MLA_NOPE), F32), sin, sin, z32], axis=1)
    return cos_t, sin_t


def _loss_head(h, target, name):
    lp = h.shape[0]

    def body(h_ref, t_ref, d_ref, loss_ref):
        i = pl.program_id(0)

        @pl.when(i == 0)
        def _():
            d_ref[...] = jnp.zeros_like(d_ref)
            loss_ref[...] = jnp.zeros_like(loss_ref)

        @pl.when(i > 0)
        def _():
            err = h_ref[...] - t_ref[...]
            d_ref[...] = err * (1.0 / D_MODEL)
            loss_ref[...] += jnp.sum(err * err, axis=0, keepdims=True) * (0.5 / D_MODEL)

    return pl.pallas_call(
        body, name=name,
        out_shape=[jax.ShapeDtypeStruct((lp, D_MODEL), F32), jax.ShapeDtypeStruct((1, D_MODEL), F32)],
        grid=(lp // CHUNK,),
        in_specs=[pl.BlockSpec((CHUNK, D_MODEL), lambda i: (i, 0)),
                  pl.BlockSpec((CHUNK, D_MODEL), lambda i: (jnp.maximum(i - 1, 0), 0))],
        out_specs=[pl.BlockSpec((CHUNK, D_MODEL), lambda i: (i, 0)),
                   pl.BlockSpec((1, D_MODEL), lambda i: (0, 0))],
        compiler_params=_cparams(dimension_semantics=("arbitrary",)),
    )(h, target)


def _pad_cols(w, n):
    return jnp.pad(w, [(0, 0)] * (w.ndim - 1) + [(0, n - w.shape[-1])])


def _layer_slab(name, i):
    return i if name.startswith('mlp_') else i // 2


def _prep_matrix(key, raw):
    if key == 'ssd_in':
        return _pad_cols(raw('ssd_w_in'), SSD_IN_PAD).astype(BF16)
    if key == 'mla_in':
        wi = raw('mla_w_in')
        kpe = jnp.pad(wi[:, MLA_Q_RANK + MLA_KV_RANK:], ((0, 0), (MLA_NOPE, HEAD_SLOT - MLA_QK)))
        return jnp.concatenate(
            [wi[:, MLA_Q_RANK:MLA_Q_RANK + MLA_KV_RANK], kpe, wi[:, :MLA_Q_RANK]], axis=1).astype(BF16)
    if key == 'mla_qb':
        qb = raw('mla_w_q_b').reshape(MLA_Q_RANK, MLA_HEADS, MLA_QK)
        return _pad_cols(qb, HEAD_SLOT).reshape(MLA_Q_RANK, MLA_WIDE).astype(BF16)
    if key == 'mla_kvb':
        kvb = raw('mla_w_kv_b').reshape(MLA_KV_RANK, MLA_HEADS, MLA_NOPE + MLA_V)
        kn = _pad_cols(kvb[:, :, :MLA_NOPE], HEAD_SLOT).reshape(MLA_KV_RANK, MLA_WIDE)
        vv = _pad_cols(kvb[:, :, MLA_NOPE:], HEAD_SLOT).reshape(MLA_KV_RANK, MLA_WIDE)
        return jnp.concatenate([kn, vv], axis=1).astype(BF16)
    if key == 'mla_out':
        wo = raw('mla_w_out').reshape(MLA_HEADS, MLA_V, D_MODEL)
        return jnp.pad(wo, ((0, 0), (0, HEAD_SLOT - MLA_V), (0, 0))).reshape(MLA_WIDE, D_MODEL).astype(BF16)
    return raw({'ssd_out': 'ssd_w_out', 'up': 'mlp_w_up', 'down': 'mlp_w_down'}[key]).astype(BF16)


class _Matrices:
    def __init__(self):
        self.p = {k: _Slabs(k, self) for k in ('ssd_in', 'ssd_out', 'mla_in', 'mla_qb', 'mla_kvb',
                                               'mla_out', 'up', 'down')}
        self.made = {}

    def matrix(self, key, slab):
        if (key, slab) not in self.made:
            self.made[(key, slab)] = _prep_matrix(key, lambda n: self.raw(n, slab))
        return self.made[(key, slab)]


class _Slabs:
    def __init__(self, key, owner):
        self.key, self.owner = key, owner

    def __getitem__(self, slab):
        return self.owner.matrix(self.key, slab)


class _ReadyWeights(_Matrices):
    def __init__(self, w):
        super().__init__()
        self.w = w

    def raw(self, name, slab):
        return self.w[name][slab]

    def start(self):
        pass

    def rider(self, host):
        return None

    def deliver(self, host, outs):
        assert not outs


class _KeepGrads:
    def __init__(self):
        self.rounds = {}

    def begin(self, r, grads):
        self.rounds[r] = grads
        return None

    def finish(self, r, outs):
        assert not outs

    def result(self):
        names = {n for g in self.rounds.values() for n in g}
        return {n: jnp.concatenate([self.rounds[r][n] for r in sorted(self.rounds, reverse=True)
                                    if n in self.rounds[r]], axis=0) for n in names}


def _pad128(v):
    return _pad_cols(v.reshape(1, -1), 128)


def _sqrelu(u):
    r = jnp.maximum(u, 0.0)
    return r * r


def _local_step(x, target, w, big=None, red=None):
    seq = x.shape[0]
    lp = NPAD + N_META + seq
    big = _ReadyWeights(w) if big is None else big
    p = big.p
    h = jnp.concatenate([jnp.zeros((NPAD, D_MODEL), F32), w['meta_tokens'], x], axis=0)
    cos_t, sin_t = _rope_tables(lp)
    rt = _pick(lp, (384, 256, 128))
    saved = []
    big.start()
    for i in range(4):
        j = i // 2
        s = {'h0': h}
        g_mix = w['ln_mix'][i].reshape(1, -1)
        g_mlp = w['ln_mlp'][i].reshape(1, -1)
        if i == 0:
            hn = _rms_fwd(h, g_mix, f"rms_mix_f{i}")
        s['hn'] = hn
        if i % 2 == 0:
            rid = big.rider(f"ssd_in_f{i}")
            zxd = _mm(hn, p['ssd_in'][j], 'nn', name=f"ssd_in_f{i}", rider=rid)
            if rid is not None:
                zxd, *got = zxd
                big.deliver(f"ssd_in_f{i}", got)
            consts = _ssd_consts(w['ssd_conv_w'][j], w['ssd_conv_b'][j].reshape(1, -1),
                                 _pad128(w['ssd_dt_bias'][j]), _pad128(w['ssd_a_log'][j]),
                                 _pad128(w['ssd_d'][j]), w['ssd_norm'][j].reshape(1, -1))
            yg, states, *got = _ssd_fwd(zxd, consts, f"ssd_core_f{i}", rider=big.rider(f"ssd_core_f{i}"))
            big.deliver(f"ssd_core_f{i}", got)
            s.update(zxd=zxd, consts=consts, yg=yg, states=states)
            h, hn2 = _mm(yg, p['ssd_out'][j], 'nn', name=f"ssd_out_f{i}", epi=lambda r, hv: hv + r,
                         extras=(h,), norm_gain=g_mlp)
        else:
            lat = _mm(hn, p['mla_in'][j], 'nn', name=f"mla_in_f{i}")
            kvg = w['mla_kv_a_norm'][j].reshape(1, -1)
            qag = w['mla_q_a_norm'][j].reshape(1, -1)
            kvn, qn = _row_call(lambda _, a, b, c, d: _lat_norm(a, b, c, d),
                                [(lat, MLA_KV_RANK, 0), (lat, MLA_Q_RANK, 1)], [kvg, qag],
                                [(MLA_KV_RANK, BF16), (MLA_Q_RANK, BF16)], n_rows=lp, tile=rt,
                                name=f"mla_latnorm_f{i}")
            q_raw = _mm(qn, p['mla_qb'][j], 'nn', name=f"mla_qb_f{i}")
            kv_raw = _mm(kvn, p['mla_kvb'][j], 'nn', name=f"mla_kvb_f{i}")
            qg = _pad_cols(w['mla_q_norm'][j].reshape(1, -1), HEAD_SLOT)
            kg = _pad_cols(w['mla_k_norm'][j].reshape(1, -1), HEAD_SLOT)

            def prep_fwd(_, qr, kn, kpe, vv, cs, sn, qgv, kgv):
                qq, kk = _qk_prep(qr, kn, kpe, cs, sn, qgv, kgv)
                ones = lax.broadcasted_iota(jnp.int32, vv.shape, 1) % HEAD_SLOT == MLA_V
                return qq, kk, jnp.where(ones, 1.0, vv)

            q, k, v = _row_call(prep_fwd,
                                [(q_raw, MLA_WIDE, 0), (kv_raw, MLA_WIDE, 0), (lat, HEAD_SLOT, 2),
                                 (kv_raw, MLA_WIDE, 1), (cos_t, HEAD_SLOT, 0), (sin_t, HEAD_SLOT, 0)],
                                [qg, kg], [(MLA_WIDE, BF16)] * 3, n_rows=lp, tile=rt,
                                name=f"mla_qkprep_f{i}")
            o, lse, *got = _attn_fwd(q, k, v, f"mla_attn_f{i}", rider=big.rider(f"mla_attn_f{i}"))
            big.deliver(f"mla_attn_f{i}", got)
            s.update(lat=lat, kvg=kvg, qag=qag, kvn=kvn, qn=qn, q_raw=q_raw, kv_raw=kv_raw, qg=qg, kg=kg,
                     q=q, k=k, v=v, o=o, lse=lse)
            h, hn2 = _mm(o, p['mla_out'][j], 'nn', name=f"mla_out_f{i}", epi=lambda r, hv: hv + r,
                         extras=(h,), norm_gain=g_mlp)
        s['h1'] = h
        u = _mm(hn2, p['up'][i], 'nn', name=f"mlp_up_f{i}", out_dtype=BF16)
        if i < 3:
            h, hn = _mm(u, p['down'][i], 'nn', name=f"mlp_down_f{i}", a_fn=_sqrelu, epi=lambda r, hv: hv + r,
                        extras=(h,), norm_gain=w['ln_mix'][i + 1].reshape(1, -1))
        else:
            h = _mm(u, p['down'][i], 'nn', name=f"mlp_down_f{i}", a_fn=_sqrelu,
                    epi=lambda r, hv: hv + r, extras=(h,))
        s.update(hn2=hn2, u=u, g_mix=g_mix, g_mlp=g_mlp)
        saved.append(s)

    dh, loss_row = _loss_head(h, target, "loss_head")

    large = {n for n, _ in BIG}
    g = {k_: [None] * (4 if k_ in ('ln_mix', 'ln_mlp') else 2)
         for k_ in ALL_NAMES if k_ != 'meta_tokens' and k_ not in large}
    red = _KeepGrads() if red is None else red
    rounds, pending = {}, None

    def round_of(nm, i):
        return next(r for r, spec in enumerate(REDUCE_ROUNDS)
                    for n, l0, l1 in spec if n == nm and l0 <= _layer_slab(nm, i) < l1)

    def slabs_in(nm, r):
        return next((l0, l1) for n, l0, l1 in REDUCE_ROUNDS[r] if n == nm)

    def dw_into(nm, i, a, b, **kw):
        r = round_of(nm, i)
        (l0, l1), cur = slabs_in(nm, r), rounds.setdefault(r, {})
        cur[nm] = _mm(a, b, 'tn', stack=(l1 - l0, _layer_slab(nm, i) - l0, cur.get(nm)), **kw)

    def put(nm, i, arr):
        rounds.setdefault(round_of(nm, i), {})[nm] = arr[None]

    def hand_over(r):
        nonlocal pending
        pending = (r, red.begin(r, rounds.pop(r)))

    def host(fn, *args):
        nonlocal pending
        if pending is None or pending[1] is None:
            return fn(*args)
        (r, rider), pending = pending, None
        outs = fn(*args, rider=rider)
        own = len(outs) - len(rider.out_shapes)
        red.finish(r, outs[own:])
        return outs[:own]

    for i in reversed(range(4)):
        j = i // 2
        s = saved[i]
        dw_into('mlp_w_down', i, s['u'], dh, name=f"mlp_down_dw{i}", a_fn=_sqrelu)
        du = _mm(dh, p['down'][i], 'nt', name=f"mlp_down_dx{i}", out_dtype=BF16,
                 epi=lambda r, uv: r * (2.0 * jnp.maximum(uv, 0.0)), extras=(s['u'],))
        dw_into('mlp_w_up', i, s['hn2'], du, name=f"mlp_up_dw{i}")
        dh, dg = _mm_rms_bwd(du, p['up'][i], s['h1'], dh, s['g_mlp'], f"mlp_up_dx{i}")
        g['ln_mlp'][i] = dg[0]
        if i % 2 == 0:
            dw_into('ssd_w_out', i, s['yg'], dh, name=f"ssd_out_dw{i}")
            d_yg = _mm(dh, p['ssd_out'][j], 'nt', name=f"ssd_out_dx{i}")
            if i == 0:
                hand_over(1)
            d_zxd, dcw, dcb, ddtb, dalog, ddsk, dng = host(_ssd_bwd, s['zxd'], s['states'], d_yg, s['consts'],
                                                           f"ssd_core_b{i}")
            g['ssd_conv_w'][j], g['ssd_conv_b'][j], g['ssd_norm'][j] = dcw, dcb[0], dng[0]
            g['ssd_dt_bias'][j], g['ssd_a_log'][j], g['ssd_d'][j] = (
                ddtb[0, :SSD_HEADS], dalog[0, :SSD_HEADS], ddsk[0, :SSD_HEADS])
            dw_into('ssd_w_in', i, s['hn'], d_zxd, name=f"ssd_in_dw{i}")
            dh, dg = _mm_rms_bwd(d_zxd, p['ssd_in'][j], s['h0'], dh, s['g_mix'], f"ssd_in_dx{i}")
        else:
            wo = _mm(s['o'], dh, 'tn', name=f"mla_out_dw{i}")
            put('mla_w_out', i, wo.reshape(MLA_HEADS, HEAD_SLOT, D_MODEL)[:, :MLA_V].reshape(-1, D_MODEL))
            dob, delta = _mm_attn_do(dh, p['mla_out'][j], s['o'], f"mla_out_dx{i}")
            dq, dk, dv = host(_attn_bwd, s['q'], s['k'], s['v'], dob, s['lse'], delta, f"mla_attn_b{i}")

            def prep_bwd(_, qr, kn, kpe, cs, sn, dqv, dkv, dvv, qgv, kgv):
                _, vjp = jax.vjp(lambda a, b, c, d, e: _qk_prep(a, b, c, cs, sn, d, e), qr, kn, kpe, qgv, kgv)
                d_qr, d_kn, d_kpe, d_qg, d_kg = vjp((dqv, dkv))
                return d_qr, jnp.concatenate([d_kn, dvv], axis=1), d_kpe, d_qg, d_kg

            d_qraw, d_kvraw, d_kpe, d_qg, d_kg = _row_call(
                prep_bwd,
                [(s['q_raw'], MLA_WIDE, 0), (s['kv_raw'], MLA_WIDE, 0), (s['lat'], HEAD_SLOT, 2),
                 (cos_t, HEAD_SLOT, 0), (sin_t, HEAD_SLOT, 0), (dq, MLA_WIDE, 0), (dk, MLA_WIDE, 0),
                 (dv, MLA_WIDE, 0)],
                [s['qg'], s['kg']], [(MLA_WIDE, BF16), (2 * MLA_WIDE, BF16), (HEAD_SLOT, F32)],
                [(1, HEAD_SLOT), (1, HEAD_SLOT)], n_rows=lp, tile=_pick(lp, (128,)), name=f"mla_qkprep_b{i}")
            g['mla_q_norm'][j], g['mla_k_norm'][j] = d_qg[0, :MLA_QK], d_kg[0, :MLA_QK]
            wqb = _mm(s['qn'], d_qraw, 'tn', name=f"mla_qb_dw{i}")
            put('mla_w_q_b', i, wqb.reshape(MLA_Q_RANK, MLA_HEADS, HEAD_SLOT)[:, :, :MLA_QK].reshape(MLA_Q_RANK, -1))
            d_qn = _mm(d_qraw, p['mla_qb'][j], 'nt', name=f"mla_qb_dx{i}")
            wkvb = _mm(s['kvn'], d_kvraw, 'tn', name=f"mla_kvb_dw{i}").reshape(MLA_KV_RANK, 2, MLA_HEADS, HEAD_SLOT)
            put('mla_w_kv_b', i, jnp.concatenate([wkvb[:, 0, :, :MLA_NOPE], wkvb[:, 1, :, :MLA_V]],
                                                 axis=-1).reshape(MLA_KV_RANK, -1))
            d_kvn = _mm(d_kvraw, p['mla_kvb'][j], 'nt', name=f"mla_kvb_dx{i}")

            def lat_bwd(_, kvl, ql, dkvn, dqn, dkpe, kvgv, qagv):
                _, vjp = jax.vjp(_lat_norm, kvl, ql, kvgv, qagv)
                d_kvl, d_ql, d_kvg, d_qag = vjp((dkvn, dqn))
                return jnp.concatenate([d_kvl, dkpe, d_ql], axis=1), d_kvg, d_qag

            d_lat, d_kvg, d_qag = _row_call(
                lat_bwd, [(s['lat'], MLA_KV_RANK, 0), (s['lat'], MLA_Q_RANK, 1), (d_kvn, MLA_KV_RANK, 0),
                          (d_qn, MLA_Q_RANK, 0), (d_kpe, HEAD_SLOT, 0)],
                [s['kvg'], s['qag']], [(LAT_PAD, BF16)], [(1, MLA_KV_RANK), (1, MLA_Q_RANK)],
                n_rows=lp, tile=rt, name=f"mla_latnorm_b{i}")
            g['mla_kv_a_norm'][j], g['mla_q_a_norm'][j] = d_kvg[0], d_qag[0]
            win = _mm(s['hn'], d_lat, 'tn', name=f"mla_in_dw{i}")
            put('mla_w_in', i, jnp.concatenate(
                [win[:, MLA_KV_RANK + HEAD_SLOT:], win[:, :MLA_KV_RANK],
                 win[:, MLA_KV_RANK + MLA_NOPE:MLA_KV_RANK + MLA_QK]], axis=1))
            dh, dg = _mm_rms_bwd(d_lat, p['mla_in'][j], s['h0'], dh, s['g_mix'], f"mla_in_dx{i}")
        g['ln_mix'][i] = dg[0]
        if i == 2:
            hand_over(0)
    hand_over(2)

    if pending[1] is not None:
        red.finish(pending[0], _run_rider(pending[1], "rs_exchange_last"))
    grads = {k_: jnp.stack(v_) for k_, v_ in g.items()}
    grads['meta_tokens'] = dh[NPAD:NPAD + N_META]
    return loss_row, dh[NPAD + N_META:], grads, red


def _all_gather8(shard, name):
    m_per, n = shard.shape

    def body(x_ref, out_ref, send_sems, recv_sems, local_sem):
        x, y, c = lax.axis_index("x"), lax.axis_index("y"), lax.axis_index("c")
        me, sibling = (x, y, c), (x, y, 1 - c)
        chips = [(1 - x, y), (x, 1 - y), (1 - x, 1 - y)]

        def rows(px, py, pc):
            return out_ref.at[pl.ds((4 * px + 2 * py + pc) * m_per, m_per), :]

        def copy(k, block, to, src=None):
            return pltpu.make_async_remote_copy(
                src_ref=rows(*block) if src is None else src, dst_ref=rows(*block),
                send_sem=send_sems.at[k], recv_sem=recv_sems.at[k], device_id=to, device_id_type=MESH)

        mine = pltpu.make_async_copy(x_ref, rows(*me), local_sem)
        mine.start()
        first = [copy(0, me, sibling, src=x_ref)]
        first += [copy(1 + j, me, (*chip, c), src=x_ref) for j, chip in enumerate(chips)]
        for cp in first:
            cp.start()
        passed = [copy(4 + j, (*chip, c), sibling) for j, chip in enumerate(chips)]
        for j, chip in enumerate(chips):
            copy(1 + j, (*chip, c), me).wait_recv()
            passed[j].start()
        copy(0, sibling, me).wait_recv()
        for j, chip in enumerate(chips):
            copy(4 + j, (*chip, 1 - c), me).wait_recv()
        for cp in first + passed:
            cp.wait_send()
        mine.wait()

    return pl.pallas_call(
        body, name=name,
        out_shape=jax.ShapeDtypeStruct((8 * m_per, n), shard.dtype),
        in_specs=[pl.BlockSpec(memory_space=pl.ANY)],
        out_specs=pl.BlockSpec(memory_space=pl.ANY),
        scratch_shapes=[pltpu.SemaphoreType.DMA((7,)), pltpu.SemaphoreType.DMA((7,)), pltpu.SemaphoreType.DMA],
    )(shard)


def _mesh_pos():
    return lax.axis_index("x"), lax.axis_index("y"), lax.axis_index("c")


def _half_rows(pc, h):
    return pl.ds(pl.multiple_of(pc * h, 16), h)


def _whole_view(ref, kind, shard_shape, k, pc):
    _, r, c = shard_shape
    rows = _half_rows(pc, r // 2)
    if kind == 'row':
        return ref.at[:, k, rows, :]
    if kind == 'col':
        return ref.at[:, rows, pl.ds(pl.multiple_of(k * c, 128), c)]
    return ref.at[k, :, rows, :]


def _whole_shape(kind, shard_shape, rows=None):
    l, r, c = shard_shape
    r = r if rows is None else rows
    return {'row': (l, 4, r, c), 'col': (l, r, 4 * c), 'colx': (4, l, r, c)}[kind]


def _gather_rider(shards, kinds):
    n = len(shards)
    shapes = [s.shape for s in shards]

    def plan(ins, outs, sems):
        send_sems, recv_sems, local_sems = sems
        x, y, c = _mesh_pos()
        me, sibling = (x, y, c), (x, y, 1 - c)
        chips = [(1 - x, y), (x, 1 - y), (1 - x, 1 - y)]

        def place(a, px, py, pc):
            return _whole_view(outs[a], kinds[a], shapes[a], 2 * px + py, pc)

        def own(a):
            return ins[a].at[:, _half_rows(c, shapes[a][1] // 2), :]

        def copy(a, k, block, to, src=None):
            return pltpu.make_async_remote_copy(
                src_ref=place(a, *block) if src is None else src, dst_ref=place(a, *block),
                send_sem=send_sems.at[7 * a + k], recv_sem=recv_sems.at[7 * a + k],
                device_id=to, device_id_type=MESH)

        mine = [pltpu.make_async_copy(own(a), place(a, *me), local_sems.at[a]) for a in range(n)]
        first = [copy(a, 1 + j, me, (*chip, c), src=own(a)) for j, chip in enumerate(chips) for a in range(n)]
        first += [copy(a, 0, me, sibling, src=own(a)) for a in range(n)]
        return copy, mine, first, chips, me, sibling, c

    def start(ins, outs, sems):
        _, mine, first, *_ = plan(ins, outs, sems)
        for cp in first + mine:
            cp.start()

    def finish(ins, outs, sems):
        copy, mine, first, chips, me, sibling, c = plan(ins, outs, sems)
        passed = []
        for j, chip in enumerate(chips):
            for a in range(n):
                copy(a, 1 + j, (*chip, c), me).wait_recv()
                passed.append(copy(a, 4 + j, (*chip, c), sibling))
                passed[-1].start()
        for a in range(n):
            copy(a, 0, sibling, me).wait_recv()
        for j, chip in enumerate(chips):
            for a in range(n):
                copy(a, 4 + j, (*chip, 1 - c), me).wait_recv()
        for cp in first + passed:
            cp.wait_send()
        for cp in mine:
            cp.wait()

    return _Rider(
        shards, [jax.ShapeDtypeStruct(_whole_shape(k, s.shape), s.dtype) for k, s in zip(kinds, shards)],
        [pltpu.SemaphoreType.DMA((7 * n,)), pltpu.SemaphoreType.DMA((7 * n,)), pltpu.SemaphoreType.DMA((n,))],
        start, finish)


def _run_rider(rider, name):
    ni, no = len(rider.operands), len(rider.out_shapes)

    def body(*refs):
        ride = (refs[:ni], refs[ni:ni + no], refs[ni + no:])
        rider.start(*ride)
        rider.finish(*ride)

    return pl.pallas_call(
        body, name=name, out_shape=rider.out_shapes,
        in_specs=[pl.BlockSpec(memory_space=pl.ANY)] * ni,
        out_specs=[pl.BlockSpec(memory_space=pl.ANY)] * no,
        scratch_shapes=rider.scratch,
    )(*rider.operands)


def _rs_swap(wholes, kinds, shapes, name):
    n = len(wholes)

    def body(*refs):
        ins, outs = refs[:n], refs[n:2 * n]
        send_sems, recv_sems = refs[2 * n:]
        x, y, c = _mesh_pos()
        cps = []
        for a in range(n):
            rows = _half_rows(1 - c, shapes[a][1] // 2)
            src = ins[a].at[:, rows, :] if kinds[a] == 'col' else ins[a].at[:, :, rows, :]
            cps.append(pltpu.make_async_remote_copy(
                src_ref=src, dst_ref=outs[a], send_sem=send_sems.at[a], recv_sem=recv_sems.at[a],
                device_id=(x, y, 1 - c), device_id_type=MESH))
        for cp in cps:
            cp.start()
        for cp in cps:
            cp.wait()

    return pl.pallas_call(
        body, name=name,
        out_shape=[jax.ShapeDtypeStruct(_whole_shape(k, s, s[1] // 2), w.dtype)
                   for k, s, w in zip(kinds, shapes, wholes)],
        in_specs=[pl.BlockSpec(memory_space=pl.ANY)] * n,
        out_specs=[pl.BlockSpec(memory_space=pl.ANY)] * n,
        scratch_shapes=[pltpu.SemaphoreType.DMA((n,)), pltpu.SemaphoreType.DMA((n,))],
    )(*wholes)


def _exchange_rider(parts, kinds, shapes):
    n = len(parts)

    def plan(ins, outs, sems):
        send_sems, recv_sems, local_sems = sems
        x, y, c = _mesh_pos()
        kme = 2 * x + y
        chips = [(1 - x, y), (x, 1 - y), (1 - x, 1 - y)]

        def slab(a, k):
            if kinds[a] == 'row':
                return ins[a].at[:, k]
            if kinds[a] == 'col':
                cw = shapes[a][2]
                return ins[a].at[:, :, pl.ds(pl.multiple_of(k * cw, 128), cw)]
            return ins[a].at[k]

        cps = [pltpu.make_async_remote_copy(
            src_ref=slab(a, 2 * px + py), dst_ref=outs[a].at[kme], send_sem=send_sems.at[3 * a + j],
            recv_sem=recv_sems.at[3 * a + j], device_id=(px, py, c), device_id_type=MESH)
            for j, (px, py) in enumerate(chips) for a in range(n)]
        return cps + [pltpu.make_async_copy(slab(a, kme), outs[a].at[kme], local_sems.at[a]) for a in range(n)]

    def start(ins, outs, sems):
        for cp in plan(ins, outs, sems):
            cp.start()

    def finish(ins, outs, sems):
        for cp in plan(ins, outs, sems):
            cp.wait()

    return _Rider(
        parts, [jax.ShapeDtypeStruct((4, s[0], s[1] // 2, s[2]), p.dtype) for s, p in zip(shapes, parts)],
        [pltpu.SemaphoreType.DMA((3 * n,)), pltpu.SemaphoreType.DMA((3 * n,)), pltpu.SemaphoreType.DMA((n,))],
        start, finish)


def _rs_share(shards, slabs, name):
    n = len(shards)

    def body(*refs):
        outs = refs[n:2 * n]
        send_sems, recv_sems = refs[2 * n:]
        x, y, c = _mesh_pos()
        cps = []
        for a in range(n):
            l0, l1 = slabs[a]
            rows = outs[a].at[pl.ds(l0, l1 - l0), _half_rows(c, shards[a].shape[1] // 2), :]
            cps.append(pltpu.make_async_remote_copy(
                src_ref=rows, dst_ref=rows, send_sem=send_sems.at[a], recv_sem=recv_sems.at[a],
                device_id=(x, y, 1 - c), device_id_type=MESH))
        for cp in cps:
            cp.start()
        for cp in cps:
            cp.wait()

    return pl.pallas_call(
        body, name=name,
        out_shape=[jax.ShapeDtypeStruct(s.shape, s.dtype) for s in shards],
        in_specs=[pl.BlockSpec(memory_space=pl.ANY)] * n,
        out_specs=[pl.BlockSpec(memory_space=pl.ANY)] * n,
        input_output_aliases={a: a for a in range(n)},
        scratch_shapes=[pltpu.SemaphoreType.DMA((n,)), pltpu.SemaphoreType.DMA((n,))],
    )(*shards)


def _tile_rows(rows, cols, budget=2 * 1024 * 1024):
    for t in (1024, 512, 256, 128, 64, 32, 16, 8):
        if rows % t == 0 and t * cols * 4 <= budget:
            return t
    return rows


def _add_half(g3, r3, c_idx, name):
    a, h, n = r3.shape
    t = _tile_rows(h, n)
    nt = h // t

    def body(c_ref, g_ref, r_ref, o_ref):
        o_ref[...] = (g_ref[...] + r_ref[...]).astype(o_ref.dtype)

    return pl.pallas_call(
        body, name=name, out_shape=jax.ShapeDtypeStruct((a, h, n), BF16),
        grid_spec=pltpu.PrefetchScalarGridSpec(
            num_scalar_prefetch=1, grid=(a, nt),
            in_specs=[pl.BlockSpec((1, t, n), lambda k, i, c: (k, c[0] * nt + i, 0)),
                      pl.BlockSpec((1, t, n), lambda k, i, c: (k, i, 0))],
            out_specs=pl.BlockSpec((1, t, n), lambda k, i, c: (k, i, 0))),
        compiler_params=_cparams(dimension_semantics=("parallel", "parallel")),
    )(c_idx, g3, r3)


def _sum4(parts, c_idx, name, into):
    _, l, h, n = parts.shape
    n_slabs, l0, buf = into
    t = _tile_rows(h, n, 1024 * 1024)
    nt = h // t
    held = () if buf is None else (buf,)

    def body(c_ref, p_ref, *rest):
        pv = p_ref[...].astype(F32)
        rest[-1][...] = ((pv[0] + pv[1]) + pv[2]) + pv[3]

    return pl.pallas_call(
        body, name=name, out_shape=jax.ShapeDtypeStruct((n_slabs, 2 * h, n), F32),
        grid_spec=pltpu.PrefetchScalarGridSpec(
            num_scalar_prefetch=1, grid=(l, nt),
            in_specs=[pl.BlockSpec((4, 1, t, n), lambda k, i, c: (0, k, i, 0))]
            + [pl.BlockSpec(memory_space=pl.ANY)] * len(held),
            out_specs=pl.BlockSpec((1, t, n), lambda k, i, c: (l0 + k, c[0] * nt + i, 0))),
        input_output_aliases={2: 0} if held else {},
        compiler_params=_cparams(dimension_semantics=("parallel", "parallel")),
    )(c_idx, parts, *held)


def _sum8(parts, name):
    _, m, n = parts.shape

    def body(p_ref, o_ref):
        acc = p_ref[0]
        for d in range(1, 8):
            acc = acc + p_ref[d]
        o_ref[...] = acc

    return pl.pallas_call(body, name=name, out_shape=jax.ShapeDtypeStruct((m, n), F32))(parts)


def _adamw(wp, gp, mp, vp, name):
    r, n = wp.shape
    t = _tile_rows(r, n, 1024 * 1024)

    def body(w_ref, g_ref, m_ref, v_ref, d_ref, mo_ref, vo_ref):
        gv = g_ref[...]
        m2 = ADAM_B1 * m_ref[...] + (1.0 - ADAM_B1) * gv
        v2 = ADAM_B2 * v_ref[...] + (1.0 - ADAM_B2) * (gv * gv)
        m_hat = m2 / (1.0 - ADAM_B1 ** ADAM_STEP)
        v_hat = v2 / (1.0 - ADAM_B2 ** ADAM_STEP)
        d_ref[...] = -ADAM_LR * (m_hat / (jnp.sqrt(v_hat) + ADAM_EPS) + ADAM_WD * w_ref[...])
        mo_ref[...] = m2
        vo_ref[...] = v2

    spec = pl.BlockSpec((t, n), lambda i: (i, 0))
    return pl.pallas_call(
        body, name=name, out_shape=[jax.ShapeDtypeStruct((r, n), F32)] * 3, grid=(r // t,),
        in_specs=[spec] * 4, out_specs=[spec] * 3,
        compiler_params=_cparams(dimension_semantics=("parallel",)),
    )(wp, gp, mp, vp)


BIG = (('ssd_w_in', 'colx'), ('ssd_w_out', 'row'), ('mla_w_in', 'row'), ('mla_w_q_b', 'col'),
       ('mla_w_kv_b', 'col'), ('mla_w_out', 'row'), ('mlp_w_up', 'col'), ('mlp_w_down', 'row'))
SMALL_SHARDED = (('meta_tokens', 1), ('ssd_conv_w', 2), ('mla_q_a_norm', 1), ('mla_kv_a_norm', 1))
SMALL_REPL = ('ln_mix', 'ln_mlp', 'ssd_conv_b', 'ssd_dt_bias', 'ssd_a_log', 'ssd_d', 'ssd_norm',
              'mla_q_norm', 'mla_k_norm')
ALL_NAMES = ('meta_tokens', 'ln_mix', 'ln_mlp', 'ssd_w_in', 'ssd_conv_w', 'ssd_conv_b', 'ssd_dt_bias',
             'ssd_a_log', 'ssd_d', 'ssd_norm', 'ssd_w_out', 'mla_w_in', 'mla_q_a_norm', 'mla_w_q_b',
             'mla_kv_a_norm', 'mla_w_kv_b', 'mla_q_norm', 'mla_k_norm', 'mla_w_out', 'mlp_w_up', 'mlp_w_down')


_MLA_BIG = ('mla_w_in', 'mla_w_q_b', 'mla_w_kv_b', 'mla_w_out')
GATHER_ROUNDS = (
    (('ssd_w_in', 0, 1),),
    (('ssd_w_out', 0, 1), ('mlp_w_up', 0, 1)),
    (('mlp_w_down', 0, 1),) + tuple((n, 0, 1) for n in _MLA_BIG) + (('mlp_w_up', 1, 2), ('mlp_w_down', 1, 2)),
    (('ssd_w_in', 1, 2), ('ssd_w_out', 1, 2)) + tuple((n, 1, 2) for n in _MLA_BIG)
    + (('mlp_w_up', 2, 4), ('mlp_w_down', 2, 4)),
)
GATHER_HOSTS = {'ssd_in_f0': 1, 'ssd_core_f0': 2, 'mla_attn_f1': 3}


REDUCE_ROUNDS = (
    GATHER_ROUNDS[3],
    tuple((n, 0, 1) for n in _MLA_BIG) + (('mlp_w_up', 0, 2), ('mlp_w_down', 0, 2), ('ssd_w_out', 0, 1)),
    (('ssd_w_in', 0, 1),),
)


class _GatheredWeights(_Matrices):
    def __init__(self, shards):
        super().__init__()
        self.shards, self.whole = shards, {}

    def raw(self, name, slab):
        return self.whole[(name, slab)]

    def _round(self, r):
        spec = GATHER_ROUNDS[r]
        return _gather_rider([self.shards[n][l0:l1] for n, l0, l1 in spec], [dict(BIG)[n] for n, _, _ in spec])

    def _take(self, r, outs):
        for (n, l0, l1), o in zip(GATHER_ROUNDS[r], outs):
            kind = dict(BIG)[n]
            for l in range(l0, l1):
                if kind == 'row':
                    m = o[l - l0].reshape(-1, o.shape[-1])
                elif kind == 'col':
                    m = o[l - l0]
                else:
                    m = jnp.concatenate([o[k, l - l0] for k in range(4)], axis=-1)
                self.whole[(n, l)] = m

    def start(self):
        self._take(0, _run_rider(self._round(0), "gather_first"))

    def rider(self, host):
        return self._round(GATHER_HOSTS[host]) if host in GATHER_HOSTS else None

    def deliver(self, host, outs):
        if host in GATHER_HOSTS:
            self._take(GATHER_HOSTS[host], outs)


class _ScatterGrads:
    def __init__(self, shard_shapes, c_idx):
        self.shard_shapes, self.c_idx, self.out = shard_shapes, c_idx, {}

    def begin(self, r, grads):
        spec = REDUCE_ROUNDS[r]
        kinds = [dict(BIG)[n] for n, _, _ in spec]
        shapes = [(l1 - l0,) + tuple(self.shard_shapes[n][1:]) for n, l0, l1 in spec]
        wholes = []
        for (n, _, _), kind, s in zip(spec, kinds, shapes):
            if kind == 'row':
                wholes.append(grads[n].reshape(s[0], 4, s[1], s[2]))
            elif kind == 'col':
                wholes.append(grads[n])
            else:
                wholes.append(jnp.stack([grads[n][..., k * s[2]:(k + 1) * s[2]] for k in range(4)]))
        recv = _rs_swap(wholes, kinds, shapes, f"rs_swap{r}")
        parts = []
        for (n, _, _), kind, s, gw, rc in zip(spec, kinds, shapes, wholes, recv):
            if kind == 'col':
                g3, r3 = gw, rc
            else:
                g3, r3 = gw.reshape(-1, s[1], s[2]), rc.reshape(-1, s[1] // 2, s[2])
            parts.append(_add_half(g3, r3, self.c_idx, f"rs_add{r}_{n}").reshape(rc.shape))
        return _exchange_rider(parts, kinds, shapes)

    def finish(self, r, outs):
        spec = REDUCE_ROUNDS[r]
        for (n, l0, _), part in zip(spec, outs):
            self.out[n] = _sum4(part, self.c_idx, f"rs_sum{r}_{n}",
                                into=(self.shard_shapes[n][0], l0, self.out.get(n)))
        shared = _rs_share([self.out[n] for n, _, _ in spec], [(l0, l1) for _, l0, l1 in spec], f"rs_share{r}")
        self.out.update(zip([n for n, _, _ in spec], shared))


def _pack(arrs, rows_mult):
    flat = jnp.concatenate([a.reshape(-1) for a in arrs])
    per = LANES * rows_mult
    pad = (-flat.shape[0]) % per
    if pad:
        flat = jnp.concatenate([flat, jnp.zeros((pad,), flat.dtype)])
    return flat.reshape(-1, LANES)


def _unpack(pack, shapes):
    flat = pack.reshape(-1)
    out, off = [], 0
    for shp in shapes:
        n = math.prod(shp)
        out.append(flat[off:off + n].reshape(shp))
        off += n
    return out


def _split4(full, axis):
    shp = full.shape
    r = full.reshape(shp[:axis] + (4, shp[axis] // 4) + shp[axis + 1:])
    return jnp.moveaxis(r, axis, 0)


def _join4(parts, axis):
    r = jnp.moveaxis(parts, 0, axis)
    shp = r.shape
    return r.reshape(shp[:axis] + (shp[axis] * shp[axis + 1],) + shp[axis + 2:])


def _gather_params(shards, table, dtype, c, name):
    pack = _pack([shards[n].astype(dtype) for n, _ in table], 16)
    half = pack.shape[0] // 2
    mine = lax.dynamic_slice_in_dim(pack, c * half, half, axis=0)
    full = _all_gather8(mine, name).reshape(4, -1)
    out, off = {}, 0
    for n, ax in table:
        cnt = math.prod(shards[n].shape)
        out[n] = _join4(full[:, off:off + cnt].reshape((4,) + shards[n].shape), ax)
        off += cnt
    return out


def kernel(x, meta_tokens, ln_mix, ln_mlp, ssd_w_in, ssd_conv_w, ssd_conv_b, ssd_dt_bias, ssd_a_log, ssd_d, ssd_norm, ssd_w_out, mla_w_in, mla_q_a_norm, mla_w_q_b, mla_kv_a_norm, mla_w_kv_b, mla_q_norm, mla_k_norm, mla_w_out, mlp_w_up, mlp_w_down, loss_target, m_meta_tokens, m_ln_mix, m_ln_mlp, m_ssd_w_in, m_ssd_conv_w, m_ssd_conv_b, m_ssd_dt_bias, m_ssd_a_log, m_ssd_d, m_ssd_norm, m_ssd_w_out, m_mla_w_in, m_mla_q_a_norm, m_mla_w_q_b, m_mla_kv_a_norm, m_mla_w_kv_b, m_mla_q_norm, m_mla_k_norm, m_mla_w_out, m_mlp_w_up, m_mlp_w_down, v_meta_tokens, v_ln_mix, v_ln_mlp, v_ssd_w_in, v_ssd_conv_w, v_ssd_conv_b, v_ssd_dt_bias, v_ssd_a_log, v_ssd_d, v_ssd_norm, v_ssd_w_out, v_mla_w_in, v_mla_q_a_norm, v_mla_w_q_b, v_mla_kv_a_norm, v_mla_w_kv_b, v_mla_q_norm, v_mla_k_norm, v_mla_w_out, v_mlp_w_up, v_mlp_w_down):
    w_sh = dict(meta_tokens=meta_tokens, ln_mix=ln_mix, ln_mlp=ln_mlp, ssd_w_in=ssd_w_in, ssd_conv_w=ssd_conv_w, ssd_conv_b=ssd_conv_b, ssd_dt_bias=ssd_dt_bias, ssd_a_log=ssd_a_log, ssd_d=ssd_d, ssd_norm=ssd_norm, ssd_w_out=ssd_w_out, mla_w_in=mla_w_in, mla_q_a_norm=mla_q_a_norm, mla_w_q_b=mla_w_q_b, mla_kv_a_norm=mla_kv_a_norm, mla_w_kv_b=mla_w_kv_b, mla_q_norm=mla_q_norm, mla_k_norm=mla_k_norm, mla_w_out=mla_w_out, mlp_w_up=mlp_w_up, mlp_w_down=mlp_w_down)
    m_sh = dict(meta_tokens=m_meta_tokens, ln_mix=m_ln_mix, ln_mlp=m_ln_mlp, ssd_w_in=m_ssd_w_in, ssd_conv_w=m_ssd_conv_w, ssd_conv_b=m_ssd_conv_b, ssd_dt_bias=m_ssd_dt_bias, ssd_a_log=m_ssd_a_log, ssd_d=m_ssd_d, ssd_norm=m_ssd_norm, ssd_w_out=m_ssd_w_out, mla_w_in=m_mla_w_in, mla_q_a_norm=m_mla_q_a_norm, mla_w_q_b=m_mla_w_q_b, mla_kv_a_norm=m_mla_kv_a_norm, mla_w_kv_b=m_mla_w_kv_b, mla_q_norm=m_mla_q_norm, mla_k_norm=m_mla_k_norm, mla_w_out=m_mla_w_out, mlp_w_up=m_mlp_w_up, mlp_w_down=m_mlp_w_down)
    v_sh = dict(meta_tokens=v_meta_tokens, ln_mix=v_ln_mix, ln_mlp=v_ln_mlp, ssd_w_in=v_ssd_w_in, ssd_conv_w=v_ssd_conv_w, ssd_conv_b=v_ssd_conv_b, ssd_dt_bias=v_ssd_dt_bias, ssd_a_log=v_ssd_a_log, ssd_d=v_ssd_d, ssd_norm=v_ssd_norm, ssd_w_out=v_ssd_w_out, mla_w_in=v_mla_w_in, mla_q_a_norm=v_mla_q_a_norm, mla_w_q_b=v_mla_w_q_b, mla_kv_a_norm=v_mla_kv_a_norm, mla_w_kv_b=v_mla_w_kv_b, mla_q_norm=v_mla_q_norm, mla_k_norm=v_mla_k_norm, mla_w_out=v_mla_w_out, mlp_w_up=v_mlp_w_up, mlp_w_down=v_mlp_w_down)

    cx, cy, cc = lax.axis_index("x"), lax.axis_index("y"), lax.axis_index("c")
    chip = 2 * cx + cy

    c_idx = cc.reshape(1).astype(jnp.int32)
    big_names = [n for n, _ in BIG]
    shapes = [w_sh[n].shape for n in big_names]

    w = {n: w_sh[n] for n in SMALL_REPL}
    w.update(_gather_params(w_sh, SMALL_SHARDED, F32, cc, "gather_small"))
    big = _GatheredWeights({n: w_sh[n].astype(BF16) for n in big_names})
    red = _ScatterGrads({n: w_sh[n].shape for n in big_names}, c_idx)

    loss_row, grad_x, grads, red = _local_step(x[0], loss_target[0], w, big, red)
    loss = lax.psum(jnp.sum(loss_row), ("x", "y", "c"))
    g_sh = dict(red.out)

    small_names = tuple(n for n, _ in SMALL_SHARDED) + SMALL_REPL
    sp = _pack([grads[n] for n in small_names], 8)
    srows = sp.shape[0]
    s_all = _sum8(_all_gather8(sp, "ar_small_gather").reshape(8, srows, LANES), "ar_small_sum")
    s_full = dict(zip(small_names, _unpack(s_all, [grads[n].shape for n in small_names])))
    for n, ax in SMALL_SHARDED:
        g_sh[n] = lax.dynamic_index_in_dim(_split4(s_full[n], ax), chip, axis=0, keepdims=False)
    for n in SMALL_REPL:
        g_sh[n] = s_full[n]

    delta, new_m, new_v = {}, {}, {}
    for n, s in zip(big_names, shapes):
        res = _adamw(*[t[n].reshape(-1, s[2]) for t in (w_sh, g_sh, m_sh, v_sh)], f"adamw_{n}")
        delta[n], new_m[n], new_v[n] = [r.reshape(s) for r in res]
    d_s, m_s, v_s = _adamw(*[_pack([t[n] for n in small_names], 8) for t in (w_sh, g_sh, m_sh, v_sh)],
                           "adamw_small")
    for dst, ps in ((delta, d_s), (new_m, m_s), (new_v, v_s)):
        dst.update(zip(small_names, _unpack(ps, [w_sh[n].shape for n in small_names])))

    return (loss, grad_x[None], *[g_sh[n] for n in ALL_NAMES], *[delta[n] for n in ALL_NAMES],
            *[new_m[n] for n in ALL_NAMES], *[new_v[n] for n in ALL_NAMES])
```

```python
import functools
import math

import jax
import jax.numpy as jnp
from jax import lax
from jax.experimental import pallas as pl
from jax.experimental.pallas import tpu as pltpu

F32 = jnp.float32
BF16 = jnp.bfloat16
MESH = pl.DeviceIdType.MESH
_NN = (((1,), (0,)), ((), ()))
_NT = (((1,), (1,)), ((), ()))
_TN = (((0,), (0,)), ((), ()))

D_MODEL = 1024
N_META = 16
EPS = 1e-6
SSD_D_INNER = 2048
SSD_HEADS = 32
SSD_HEAD_DIM = 64
SSD_GROUPS = 8
SSD_HPG = 4
SSD_STATE = 128
SSD_CONV = 4
CHUNK = 128
SSD_IN_DIM = 6176
SSD_IN_PAD = 6272
MLA_HEADS = 16
MLA_NOPE = 64
MLA_ROPE = 32
MLA_V = 64
MLA_QK = 96
MLA_Q_RANK = 384
MLA_KV_RANK = 256
HEAD_SLOT = 128
MLA_WIDE = MLA_HEADS * HEAD_SLOT
HEADS_PER_STEP = 2
LAT_PAD = 768
ROPE_THETA = 10000.0
D_FF = 4096
NPAD = CHUNK - N_META
ADAM_LR, ADAM_B1, ADAM_B2, ADAM_EPS, ADAM_WD, ADAM_STEP = 0.001, 0.9, 0.999, 1e-08, 0.01, 10
LANES = 1024
V7X_VMEM_BYTES = 64 * 1024 * 1024
VMEM_LIMIT = V7X_VMEM_BYTES * 7 // 8


def _pick(n, cands):
    for c in cands:
        if n % c == 0:
            return c
    return n


def _cparams(**kw):
    return pltpu.CompilerParams(vmem_limit_bytes=VMEM_LIMIT, **kw)


def _mm(a, b, dims, *, name, out_dtype=F32, a_fn=None, epi=None, extras=(), stack=None, norm_gain=None,
        rider=None):
    if dims == 'nn':
        (M, K), (K2, N) = a.shape, b.shape
    elif dims == 'nt':
        (M, K), (N, K2) = a.shape, b.shape
    else:
        (K, M), (K2, N) = a.shape, b.shape
    assert K == K2, (a.shape, b.shape, dims)
    if dims == 'tn':
        tm = _pick(M, (1024, 768, 512, 384, 256, 128))
        tn = _pick(N, (1024, 896, 768, 512, 384, 256, 128))
        tk = _pick(K, (1408, 1024, 512, 384, 256, 128))
    else:
        tm = _pick(M, (704, 512, 384, 256, 128) if norm_gain is not None else (1408, 1024, 512, 384, 256, 128))
        tn = _pick(N, (1024, 896, 768, 512, 384, 256, 128))
        tk = _pick(K, (1024, 896, 768, 512, 384, 256, 128))
    nk = K // tk
    if dims == 'nn':
        a_spec = pl.BlockSpec((tm, tk), lambda i, j, k: (i, k))
        b_spec = pl.BlockSpec((tk, tn), lambda i, j, k: (k, j))
        dn = (((1,), (0,)), ((), ()))
    elif dims == 'nt':
        a_spec = pl.BlockSpec((tm, tk), lambda i, j, k: (i, k))
        b_spec = pl.BlockSpec((tn, tk), lambda i, j, k: (j, k))
        dn = (((1,), (1,)), ((), ()))
    else:
        a_spec = pl.BlockSpec((tk, tm), lambda i, j, k: (k, i))
        b_spec = pl.BlockSpec((tk, tn), lambda i, j, k: (k, j))
        dn = (((0,), (0,)), ((), ()))
    o_spec = pl.BlockSpec((tm, tn), lambda i, j, k: (i, j))
    n_ex = len(extras)
    out_shape = jax.ShapeDtypeStruct((M, N), out_dtype)
    out_spec, held, aliases = o_spec, (), {}
    if stack is not None:
        n_slabs, slab, buf = stack
        out_shape = jax.ShapeDtypeStruct((n_slabs, M, N), out_dtype)
        out_spec = pl.BlockSpec((None, tm, tn), lambda i, j, k: (slab, i, j))
        if buf is not None:
            held, aliases = (buf,), {2 + n_ex: 0}

    gains = ()
    if norm_gain is not None:
        assert tn == N and stack is None, "the rms epilogue needs whole rows"
        gains = (norm_gain,)
        out_shape = [out_shape, jax.ShapeDtypeStruct((M, N), BF16)]
        out_spec = [out_spec, o_spec]

    n_own_in = 2 + n_ex + len(gains) + len(held)
    steps = (M // tm, N // tn, nk)

    def body(*refs):
        (a_ref, b_ref, *rest), ride = _rider_split(rider, refs, n_own_in, 1 + len(gains), 1)
        ex_refs, rest = rest[:n_ex], rest[n_ex:]
        g_refs, rest = rest[:len(gains)], rest[len(gains) + len(held):]
        o_ref, acc = rest[0], rest[-1]
        k = pl.program_id(2)
        if ride is not None:
            @pl.when((pl.program_id(0) == 0) & (pl.program_id(1) == 0) & (k == 0))
            def _():
                rider.start(*ride)

        @pl.when(k == 0)
        def _():
            acc[...] = jnp.zeros_like(acc)

        av = a_ref[...]
        if a_fn is not None:
            av = a_fn(av)
        acc[...] += lax.dot_general(av.astype(BF16), b_ref[...].astype(BF16), dn,
                                    preferred_element_type=F32)

        @pl.when(k == nk - 1)
        def _():
            r = acc[...]
            if epi is not None:
                r = epi(r, *[e[...] for e in ex_refs])
            o_ref[...] = r.astype(out_dtype)
            if gains:
                rest[1][...] = _rms(r, g_refs[0][...]).astype(BF16)

        if ride is not None:
            @pl.when((pl.program_id(0) == steps[0] - 1) & (pl.program_id(1) == steps[1] - 1) & (k == nk - 1))
            def _():
                rider.finish(*ride)

    r_in, r_out, r_shapes, r_scratch = _rider_args(rider)
    own_shapes = out_shape if isinstance(out_shape, list) else [out_shape]
    own_specs = out_spec if isinstance(out_spec, list) else [out_spec]
    res = pl.pallas_call(
        body, name=name,
        out_shape=own_shapes + r_shapes,
        grid=steps,
        in_specs=[a_spec, b_spec] + [o_spec] * n_ex
        + [pl.BlockSpec((1, tn), lambda i, j, k: (0, j))] * len(gains)
        + [pl.BlockSpec(memory_space=pl.ANY)] * len(held) + r_in,
        out_specs=own_specs + r_out,
        input_output_aliases=aliases,
        scratch_shapes=[pltpu.VMEM((tm, tn), F32)] + r_scratch,
        compiler_params=_cparams(dimension_semantics=("arbitrary", "arbitrary", "arbitrary")),
    )(a, b, *extras, *gains, *held, *(rider.operands if rider else ()))
    return res[0] if len(res) == 1 else res


def _mm_rms_bwd(cot, w_t, h, d_res, gain, name):
    (M, K), (N, _) = cot.shape, w_t.shape
    tm = _pick(M, (704, 512, 384, 256, 128))
    tk = _pick(K, (1024, 896, 768, 512, 384, 256, 128))
    nk = K // tk

    def body(a_ref, b_ref, h_ref, r_ref, g_ref, dh_ref, dg_ref, acc):
        i, k = pl.program_id(0), pl.program_id(1)

        @pl.when(k == 0)
        def _():
            acc[...] = jnp.zeros_like(acc)

        acc[...] += lax.dot_general(a_ref[...].astype(BF16), b_ref[...].astype(BF16), _NT,
                                    preferred_element_type=F32)

        @pl.when(k == nk - 1)
        def _():
            _, vjp = jax.vjp(_rms, h_ref[...], g_ref[...])
            dh, dg = vjp(acc[...])
            dh_ref[...] = (r_ref[...] + dh) * _row_mask(i, tm)

            @pl.when(i == 0)
            def _():
                dg_ref[...] = dg

            @pl.when(i > 0)
            def _():
                dg_ref[...] += dg

    rows = pl.BlockSpec((tm, N), lambda i, k: (i, 0))
    vec = pl.BlockSpec((1, N), lambda i, k: (0, 0))
    return pl.pallas_call(
        body, name=name,
        out_shape=[jax.ShapeDtypeStruct((M, N), F32), jax.ShapeDtypeStruct((1, N), F32)],
        grid=(M // tm, nk),
        in_specs=[pl.BlockSpec((tm, tk), lambda i, k: (i, k)), pl.BlockSpec((N, tk), lambda i, k: (0, k)),
                  rows, rows, vec],
        out_specs=[rows, vec],
        scratch_shapes=[pltpu.VMEM((tm, N), F32)],
        compiler_params=_cparams(dimension_semantics=("arbitrary", "arbitrary")),
    )(cot, w_t, h, d_res, gain)


def _mm_attn_do(dh, w_out_t, o, name):
    (M, K), (N, _) = dh.shape, w_out_t.shape
    tm = _pick(M, (704, 512, 384, 256, 128))
    tn = 8 * HEAD_SLOT

    def body(a_ref, b_ref, o_ref, dob_ref, delta_ref):
        do = lax.dot_general(a_ref[...].astype(BF16), b_ref[...], _NT, preferred_element_type=F32)
        dob_ref[...] = do.astype(BF16)
        for hh in range(tn // HEAD_SLOT):
            sl = slice(hh * HEAD_SLOT, (hh + 1) * HEAD_SLOT)
            delta_ref[hh] = jnp.sum(do[:, sl] * o_ref[:, sl], axis=-1, keepdims=True)

    tile = pl.BlockSpec((tm, tn), lambda i, j: (i, j))
    return pl.pallas_call(
        body, name=name,
        out_shape=[jax.ShapeDtypeStruct((M, N), BF16), jax.ShapeDtypeStruct((N // HEAD_SLOT, M, 1), F32)],
        grid=(M // tm, N // tn),
        in_specs=[pl.BlockSpec((tm, K), lambda i, j: (i, 0)), pl.BlockSpec((tn, K), lambda i, j: (j, 0)), tile],
        out_specs=[tile, pl.BlockSpec((tn // HEAD_SLOT, tm, 1), lambda i, j: (j, i, 0))],
        compiler_params=_cparams(dimension_semantics=("parallel", "parallel")),
    )(dh, w_out_t, o)


def _row_call(fn, rows, consts, out_rows, out_accs=(), *, n_rows, tile, name):
    n_r, n_c, n_o, n_a = len(rows), len(consts), len(out_rows), len(out_accs)
    steps = n_rows // tile

    def body(*refs):
        r_refs = refs[:n_r]
        c_refs = refs[n_r:n_r + n_c]
        o_refs = refs[n_r + n_c:n_r + n_c + n_o]
        a_refs = refs[n_r + n_c + n_o:]
        i = pl.program_id(0)
        res = fn(i, *[r[...] for r in r_refs], *[c[...] for c in c_refs])
        for o_ref, val in zip(o_refs, res[:n_o]):
            o_ref[...] = val.astype(o_ref.dtype)

        @pl.when(i == 0)
        def _():
            for a_ref in a_refs:
                a_ref[...] = jnp.zeros_like(a_ref)

        for a_ref, val in zip(a_refs, res[n_o:]):
            a_ref[...] += val

    in_specs = [pl.BlockSpec((tile, w), functools.partial(lambda i, cb: (i, cb), cb=cb))
                for (_, w, cb) in rows]
    in_specs += [pl.BlockSpec(c.shape, lambda i: (0, 0)) for c in consts]
    out_specs = [pl.BlockSpec((tile, c), lambda i: (i, 0)) for (c, _) in out_rows]
    out_specs += [pl.BlockSpec(s, lambda i: (0, 0)) for s in out_accs]
    out_shape = [jax.ShapeDtypeStruct((n_rows, c), dt) for (c, dt) in out_rows]
    out_shape += [jax.ShapeDtypeStruct(s, F32) for s in out_accs]
    return pl.pallas_call(
        body, name=name, out_shape=out_shape, grid=(steps,),
        in_specs=in_specs, out_specs=out_specs,
        compiler_params=_cparams(dimension_semantics=("arbitrary",)),
    )(*[r[0] for r in rows], *consts)


def _row_mask(i, tile):
    r = i * tile + lax.broadcasted_iota(jnp.int32, (tile, 1), 0)
    return (r >= NPAD).astype(F32)


def _rms(x, g):
    return x * lax.rsqrt(jnp.mean(x * x, axis=-1, keepdims=True) + EPS) * g


def _silu(x):
    return x * (0.5 * jnp.tanh(0.5 * x) + 0.5)


def _softplus(x):
    return jnp.maximum(x, 0.0) + jnp.log(1.0 + jnp.exp(-jnp.abs(x)))


def _rms_fwd(h, g, name):
    lp = h.shape[0]
    return _row_call(lambda i, hv, gv: (_rms(hv, gv),), [(h, D_MODEL, 0)], [g],
                     [(D_MODEL, BF16)], n_rows=lp, tile=_pick(lp, (384, 256, 128)), name=name)[0]


@functools.partial(jax.custom_vjp, nondiff_argnums=(1,))
def _roll_rows(x, s):
    return pltpu.roll(x, s, 0)


def _roll_rows_fwd(x, s):
    return pltpu.roll(x, s, 0), None


def _roll_rows_bwd(s, _, ct):
    return (pltpu.roll(ct, (ct.shape[0] - s) % ct.shape[0], 0),)


_roll_rows.defvjp(_roll_rows_fwd, _roll_rows_bwd)


def _conv_silu(cur, halo, w_rows, b):
    full = jnp.concatenate([halo, cur], axis=0)
    acc = cur * w_rows[SSD_CONV - 1] + b
    for k in range(SSD_CONV - 1):
        acc = acc + _roll_rows(full, SSD_CONV - 1 - k)[8:] * w_rows[k]
    return _silu(acc)


def _split3(v):
    hi = v.astype(BF16)
    r1 = v - hi.astype(F32)
    mid = r1.astype(BF16)
    lo = (r1 - mid.astype(F32)).astype(BF16)
    return hi, mid, lo


def _select_right(v, sel, dn):
    return sum(lax.dot_general(p, sel, dn, preferred_element_type=F32) for p in _split3(v))


@jax.custom_vjp
def _expand_heads(v, e_mat):
    return _select_right(v, e_mat, _NN)


def _expand_heads_fwd(v, e_mat):
    return _select_right(v, e_mat, _NN), e_mat


def _expand_heads_bwd(e_mat, ct):
    return _select_right(ct, e_mat, _NT), jnp.zeros_like(e_mat)


_expand_heads.defvjp(_expand_heads_fwd, _expand_heads_bwd)


@jax.custom_vjp
def _cumsum_rows(a, tri):
    return sum(lax.dot_general(tri, p, _NN, preferred_element_type=F32) for p in _split3(a))


def _cumsum_rows_fwd(a, tri):
    return _cumsum_rows(a, tri), tri


def _cumsum_rows_bwd(tri, ct):
    return (sum(lax.dot_general(tri, p, _TN, preferred_element_type=F32) for p in _split3(ct)),
            jnp.zeros_like(tri))


_cumsum_rows.defvjp(_cumsum_rows_fwd, _cumsum_rows_bwd)


def _ssd_chunk(mask, z, xs_pre, bc_pre, halo_x, halo_bc, dt_pre, st, cwx0, cwx1, cwx2, cwx3,
               cwb0, cwb1, cwb2, cwb3, cb_x, cb_bc, dtb, alog, dsk, ng):
    L = CHUNK
    lane_h = lax.broadcasted_iota(jnp.int32, (1, 128), 1)
    head_ok = (lane_h < SSD_HEADS).astype(F32)
    e_mat = (lax.broadcasted_iota(jnp.int32, (128, SSD_D_INNER), 1) // SSD_HEAD_DIM
             == lax.broadcasted_iota(jnp.int32, (128, SSD_D_INNER), 0)).astype(BF16)
    ri = lax.broadcasted_iota(jnp.int32, (L, L), 0)
    ci = lax.broadcasted_iota(jnp.int32, (L, L), 1)
    causal = ri >= ci

    xs = _conv_silu(xs_pre, halo_x, (cwx0, cwx1, cwx2, cwx3), cb_x) * mask
    bc = _conv_silu(bc_pre, halo_bc, (cwb0, cwb1, cwb2, cwb3), cb_bc) * mask
    dt = _softplus(dt_pre + dtb) * mask * head_ok
    a_dt = dt * (-jnp.exp(alog))
    a_cs = _cumsum_rows(a_dt, causal.astype(BF16))
    a_cs_t = a_cs.T
    row8 = lax.broadcasted_iota(jnp.int32, (8, 128), 0)
    last8 = jnp.where(row8 == 0, jnp.sum(a_dt, axis=0, keepdims=True), 0.0)
    dsk8 = jnp.where(row8 == 0, dsk, 0.0)
    wide = _expand_heads(jnp.concatenate([dt, a_cs, last8, dsk8], axis=0), e_mat)
    dt_e, acs_e = wide[0:L], wide[L:2 * L]
    last_e = jnp.sum(wide[2 * L:2 * L + 8], axis=0, keepdims=True)
    d_e = jnp.sum(wide[2 * L + 8:2 * L + 16], axis=0, keepdims=True)
    xdt = xs * dt_e
    dte_e = jnp.exp(last_e - acs_e)
    dfs_e = jnp.exp(acs_e)
    cd_e = jnp.exp(last_e)
    sub_h = lax.broadcasted_iota(jnp.int32, (128, L), 0)
    lane_hl = lax.broadcasted_iota(jnp.int32, (L, 128), 1)
    lane_g = lax.broadcasted_iota(jnp.int32, (1, SSD_HPG * SSD_HEAD_DIM), 1) // SSD_HEAD_DIM

    ys, new_st = [], []
    for g in range(SSD_GROUPS):
        b_g = bc[:, g * 128:(g + 1) * 128].astype(BF16)
        c_g = bc[:, 1024 + g * 128:1024 + (g + 1) * 128].astype(BF16)
        gs = slice(g * 256, (g + 1) * 256)
        xdt_g = xdt[:, gs]
        cb = lax.dot_general(c_g, b_g, (((1,), (1,)), ((), ())), preferred_element_type=F32)
        st_g = st[g * 128:(g + 1) * 128, :]
        y_g = lax.dot_general(c_g, st_g.astype(BF16), (((1,), (0,)), ((), ())),
                              preferred_element_type=F32) * dfs_e[:, gs]
        for j in range(SSD_HPG):
            h = g * SSD_HPG + j
            col = jnp.sum(jnp.where(lane_hl == h, a_cs, 0.0), axis=1, keepdims=True)
            row = jnp.sum(jnp.where(sub_h == h, a_cs_t, 0.0), axis=0, keepdims=True)
            dec = jnp.where(causal, jnp.exp(jnp.where(causal, col - row, 0.0)), 0.0)
            m_h = (cb * dec).astype(BF16)
            x_h = jnp.where(lane_g == j, xdt_g, 0.0).astype(BF16)
            y_g = y_g + lax.dot_general(m_h, x_h, (((1,), (0,)), ((), ())),
                                        preferred_element_type=F32)
        s_new = lax.dot_general(b_g, (xdt_g * dte_e[:, gs]).astype(BF16), (((0,), (0,)), ((), ())),
                                preferred_element_type=F32)
        new_st.append(st_g * cd_e[:, gs] + s_new)
        ys.append(y_g)
    y = jnp.concatenate(ys, axis=1) + xs * d_e
    gg = y * _silu(z)
    outs = []
    for g in range(SSD_GROUPS):
        sl = gg[:, g * 256:(g + 1) * 256]
        outs.append(sl * lax.rsqrt(jnp.mean(sl * sl, axis=-1, keepdims=True) + EPS))
    out = jnp.concatenate(outs, axis=1) * ng
    return out, jnp.concatenate(new_st, axis=0)


def _ssd_consts(conv_w, conv_b, dtb, alog, dsk, ng):
    return [conv_w, conv_b, dtb, alog, dsk, ng]


def _ssd_param_vals(cw_ref, cb_ref, dtb_ref, alog_ref, dsk_ref, ng_ref):
    cwx = [cw_ref[k:k + 1, 0:SSD_D_INNER] for k in range(SSD_CONV)]
    cwb = [cw_ref[k:k + 1, SSD_D_INNER:2 * SSD_D_INNER] for k in range(SSD_CONV)]
    return (*cwx, *cwb, cb_ref[:, 0:SSD_D_INNER], cb_ref[:, SSD_D_INNER:2 * SSD_D_INNER],
            dtb_ref[...], alog_ref[...], dsk_ref[...], ng_ref[...])


def _ssd_in_specs(rev, nc):
    def cidx(i):
        return (nc - 1 - i) if rev else i

    def halo(cb):
        return pl.BlockSpec((8, SSD_D_INNER), lambda i: (jnp.maximum(16 * cidx(i) - 1, 0), cb))

    return [
        pl.BlockSpec((CHUNK, SSD_D_INNER), lambda i: (cidx(i), 0)),
        pl.BlockSpec((CHUNK, SSD_D_INNER), lambda i: (cidx(i), 1)),
        pl.BlockSpec((CHUNK, SSD_D_INNER), lambda i: (cidx(i), 2)),
        halo(1), halo(2),
        pl.BlockSpec((CHUNK, 128), lambda i: (cidx(i), 48)),
    ]


class _Rider:
    def __init__(self, operands, out_shapes, scratch, start, finish):
        self.operands, self.out_shapes, self.scratch = list(operands), list(out_shapes), list(scratch)
        self.start, self.finish = start, finish


def _rider_split(rider, refs, n_in, n_out, n_scratch):
    if rider is None:
        return refs, None
    ni, no = len(rider.operands), len(rider.out_shapes)
    own = refs[:n_in] + refs[n_in + ni:n_in + ni + n_out] + refs[n_in + ni + n_out + no:n_in + ni + n_out + no + n_scratch]
    mine = (refs[n_in:n_in + ni], refs[n_in + ni + n_out:n_in + ni + n_out + no],
            refs[n_in + ni + n_out + no + n_scratch:])
    return own, mine


def _rider_args(rider):
    if rider is None:
        return [], [], [], []
    hbm = pl.BlockSpec(memory_space=pl.ANY)
    return ([hbm] * len(rider.operands), [hbm] * len(rider.out_shapes), rider.out_shapes, rider.scratch)


def _ssd_fwd(zxd, consts, name, rider=None):
    lp = zxd.shape[0]
    nc = lp // CHUNK

    def body(*refs):
        own, ride = _rider_split(rider, refs, 12, 2, 1)
        (z_ref, xs_ref, bc_ref, hx_ref, hb_ref, dt_ref, cw_ref, cb_ref, dtb_ref, alog_ref,
         dsk_ref, ng_ref, y_ref, st_ref, state) = own
        c = pl.program_id(0)

        @pl.when(c == 0)
        def _():
            state[...] = jnp.zeros_like(state)
            if ride is not None:
                rider.start(*ride)

        live = (c > 0).astype(F32)
        st_ref[0] = state[...]
        out, st_new = _ssd_chunk(
            _row_mask(c, CHUNK), z_ref[...], xs_ref[...], bc_ref[...], hx_ref[...] * live,
            hb_ref[...] * live, dt_ref[...], state[...],
            *_ssd_param_vals(cw_ref, cb_ref, dtb_ref, alog_ref, dsk_ref, ng_ref))
        y_ref[...] = out.astype(y_ref.dtype)
        state[...] = st_new

        if ride is not None:
            @pl.when(c == nc - 1)
            def _():
                rider.finish(*ride)

    r_in, r_out, r_shapes, r_scratch = _rider_args(rider)
    return pl.pallas_call(
        body, name=name,
        out_shape=[jax.ShapeDtypeStruct((lp, SSD_D_INNER), BF16),
                   jax.ShapeDtypeStruct((nc, SSD_GROUPS * SSD_STATE, 256), F32)] + r_shapes,
        grid=(nc,),
        in_specs=_ssd_in_specs(False, nc) + [pl.BlockSpec(c.shape, lambda i: (0, 0)) for c in consts] + r_in,
        out_specs=[pl.BlockSpec((CHUNK, SSD_D_INNER), lambda i: (i, 0)),
                   pl.BlockSpec((1, SSD_GROUPS * SSD_STATE, 256), lambda i: (i, 0, 0))] + r_out,
        scratch_shapes=[pltpu.VMEM((SSD_GROUPS * SSD_STATE, 256), F32)] + r_scratch,
        compiler_params=_cparams(dimension_semantics=("arbitrary",)),
    )(zxd, zxd, zxd, zxd, zxd, zxd, *consts, *(rider.operands if rider else ()))


def _ssd_bwd(zxd, states, d_y, consts, name, rider=None):
    lp = zxd.shape[0]
    nc = lp // CHUNK

    def body(*refs):
        own, ride = _rider_split(rider, refs, 14, 7, 3)
        (z_ref, xs_ref, bc_ref, hx_ref, hb_ref, dt_ref, st_ref, dy_ref, cw_ref, cb_ref, dtb_ref,
         alog_ref, dsk_ref, ng_ref, dz_ref, dcw_ref, dcb_ref, ddtb_ref, dalog_ref, ddsk_ref,
         dng_ref, d_state, d_hx, d_hb) = own
        i = pl.program_id(0)
        c = nc - 1 - i

        @pl.when(i == 0)
        def _():
            d_state[...] = jnp.zeros_like(d_state)
            d_hx[...] = jnp.zeros_like(d_hx)
            d_hb[...] = jnp.zeros_like(d_hb)
            for r in (dcw_ref, dcb_ref, ddtb_ref, dalog_ref, ddsk_ref, dng_ref):
                r[...] = jnp.zeros_like(r)
            if ride is not None:
                rider.start(*ride)

        live = (c > 0).astype(F32)
        fn = functools.partial(_ssd_chunk, _row_mask(c, CHUNK))
        prim = (z_ref[...], xs_ref[...], bc_ref[...], hx_ref[...] * live, hb_ref[...] * live,
                dt_ref[...], st_ref[0],
                *_ssd_param_vals(cw_ref, cb_ref, dtb_ref, alog_ref, dsk_ref, ng_ref))
        _, vjp = jax.vjp(fn, *prim)
        (d_z, d_xs, d_bc, g_hx, g_hb, d_dt, g_st, *d_par) = vjp((dy_ref[...], d_state[...]))
        zeros = jnp.zeros((CHUNK - 8, SSD_D_INNER), F32)
        d_xs = d_xs + jnp.concatenate([zeros, d_hx[...]], axis=0)
        d_bc = d_bc + jnp.concatenate([zeros, d_hb[...]], axis=0)
        dz_ref[:, 0:SSD_D_INNER] = d_z.astype(dz_ref.dtype)
        dz_ref[:, SSD_D_INNER:2 * SSD_D_INNER] = d_xs.astype(dz_ref.dtype)
        dz_ref[:, 2 * SSD_D_INNER:3 * SSD_D_INNER] = d_bc.astype(dz_ref.dtype)
        dz_ref[:, 3 * SSD_D_INNER:] = d_dt.astype(dz_ref.dtype)
        d_state[...] = g_st
        d_hx[...] = g_hx * live
        d_hb[...] = g_hb * live
        for k in range(SSD_CONV):
            dcw_ref[k:k + 1, 0:SSD_D_INNER] += d_par[k]
            dcw_ref[k:k + 1, SSD_D_INNER:2 * SSD_D_INNER] += d_par[SSD_CONV + k]
        dcb_ref[:, 0:SSD_D_INNER] += d_par[8]
        dcb_ref[:, SSD_D_INNER:2 * SSD_D_INNER] += d_par[9]
        ddtb_ref[...] += d_par[10]
        dalog_ref[...] += d_par[11]
        ddsk_ref[...] += d_par[12]
        dng_ref[...] += d_par[13]

        if ride is not None:
            @pl.when(i == nc - 1)
            def _():
                rider.finish(*ride)

    const_specs = [pl.BlockSpec(c.shape, lambda i: (0, 0)) for c in consts]
    r_in, r_out, r_shapes, r_scratch = _rider_args(rider)
    return pl.pallas_call(
        body, name=name,
        out_shape=[jax.ShapeDtypeStruct((lp, SSD_IN_PAD), BF16)]
        + [jax.ShapeDtypeStruct(c.shape, F32) for c in consts] + r_shapes,
        grid=(nc,),
        in_specs=_ssd_in_specs(True, nc)
        + [pl.BlockSpec((1, SSD_GROUPS * SSD_STATE, 256), lambda i: (nc - 1 - i, 0, 0)),
           pl.BlockSpec((CHUNK, SSD_D_INNER), lambda i: (nc - 1 - i, 0))] + const_specs + r_in,
        out_specs=[pl.BlockSpec((CHUNK, SSD_IN_PAD), lambda i: (nc - 1 - i, 0))] + const_specs + r_out,
        scratch_shapes=[pltpu.VMEM((SSD_GROUPS * SSD_STATE, 256), F32),
                        pltpu.VMEM((8, SSD_D_INNER), F32), pltpu.VMEM((8, SSD_D_INNER), F32)] + r_scratch,
        compiler_params=_cparams(dimension_semantics=("arbitrary",)),
    )(zxd, zxd, zxd, zxd, zxd, zxd, states, d_y, *consts, *(rider.operands if rider else ()))


@jax.custom_vjp
def _rot_half(x):
    lane = lax.broadcasted_iota(jnp.int32, x.shape, 1)
    lo = (lane >= MLA_NOPE) & (lane < MLA_NOPE + MLA_ROPE // 2)
    hi = (lane >= MLA_NOPE + MLA_ROPE // 2) & (lane < MLA_QK)
    down = pltpu.roll(x, HEAD_SLOT - MLA_ROPE // 2, 1)
    up = pltpu.roll(x, MLA_ROPE // 2, 1)
    return jnp.where(lo, -down, jnp.where(hi, up, 0.0))


def _rot_half_fwd(x):
    return _rot_half(x), None


def _rot_half_bwd(_, ct):
    return (-_rot_half(ct),)


_rot_half.defvjp(_rot_half_fwd, _rot_half_bwd)


def _head_norm_rope(t, gain, cos, sin):
    n = t * lax.rsqrt(jnp.sum(t * t, axis=-1, keepdims=True) * (1.0 / MLA_QK) + EPS) * gain
    return n * cos + _rot_half(n) * sin


def _qk_prep(q_raw, kn_raw, kpe, cos, sin, qg, kg):
    qs, ks = [], []
    for h in range(MLA_HEADS):
        sl = slice(h * HEAD_SLOT, (h + 1) * HEAD_SLOT)
        qs.append(_head_norm_rope(q_raw[:, sl], qg, cos, sin))
        ks.append(_head_norm_rope(kn_raw[:, sl] + kpe, kg, cos, sin))
    return jnp.concatenate(qs, axis=1), jnp.concatenate(ks, axis=1)


def _lat_norm(kv_lat, q_lat, kvg, qg):
    return _rms(kv_lat, kvg), _rms(q_lat, qg)


_NEG = -1e30
_SCALE = MLA_QK ** -0.5


STRIP = 128
_EXP2_SCALE = _SCALE * math.log2(math.e)


def _strip_mask(kind, blk, c, t):
    if kind is None:
        return None
    kpos = blk * t + c * STRIP + lax.broadcasted_iota(jnp.int32, (1, STRIP), 1)
    if kind == 'keys':
        return kpos >= NPAD
    qpos = blk * t + lax.broadcasted_iota(jnp.int32, (t, 1), 0)
    return (kpos <= qpos) & ((kpos >= NPAD) | (kpos == qpos))


def _attn_fwd(q, k, v, name, rider=None):
    lp = q.shape[0]
    t = tk = _pick(lp, (384, 256, 128))
    nb = lp // t
    hp = HEADS_PER_STEP
    wide = hp * HEAD_SLOT
    heads = [slice(a * HEAD_SLOT, (a + 1) * HEAD_SLOT) for a in range(hp)]

    def body(*refs):
        (q_ref, k_ref, v_ref, o_ref, lse_ref), ride = _rider_split(rider, refs, 3, 2, 0)
        qi = pl.program_id(1)
        if ride is not None:
            @pl.when((pl.program_id(0) == 0) & (qi == 0))
            def _():
                rider.start(*ride)

        def scores(ki):
            rows = pl.ds(pl.multiple_of(ki * tk, tk), tk)
            return tuple(lax.dot_general(q_ref[:, heads[a]], k_ref[rows, heads[a]], _NT,
                                         preferred_element_type=F32) for a in range(hp))

        def update(a, ki, carry, s, mask):
            rows = pl.ds(pl.multiple_of(ki * tk, tk), tk)
            m, acc = carry
            s = jnp.where(mask, s, _NEG)
            m_new = jnp.maximum(m, jnp.max(s, axis=-1, keepdims=True))
            alpha = jnp.exp2((m - m_new) * _EXP2_SCALE)
            p = jnp.concatenate(
                [jnp.exp2((s[:, c:c + STRIP] - m_new) * _EXP2_SCALE).astype(BF16) for c in range(0, tk, STRIP)],
                axis=1)
            acc = alpha * acc + lax.dot_general(p, v_ref[rows, heads[a]], _NN, preferred_element_type=F32)
            return m_new, acc

        init = (jnp.full((t, 1), _NEG, F32), jnp.zeros((t, HEAD_SLOT), F32))
        ones_lane = lax.broadcasted_iota(jnp.int32, (1, HEAD_SLOT), 1) == MLA_V
        key_pos = lax.broadcasted_iota(jnp.int32, (1, tk), 1)
        n_full = (qi * t) // tk

        def before(ki, state):
            carry, s = state
            s_next = scores(ki + 1)
            key_ok = ki * tk + key_pos >= NPAD
            return tuple(update(a, ki, carry[a], s[a], key_ok) for a in range(hp)), s_next

        carry, s = lax.fori_loop(0, n_full, before, ((init,) * hp, scores(0)))
        qpos = qi * t + lax.broadcasted_iota(jnp.int32, (t, tk), 0)
        kpos = n_full * tk + lax.broadcasted_iota(jnp.int32, (t, tk), 1)
        diag = (kpos <= qpos) & ((kpos >= NPAD) | (kpos == qpos))
        carry = tuple(update(a, n_full, carry[a], s[a], diag) for a in range(hp))
        for a in range(hp):
            m, acc = carry[a]
            l = jnp.sum(jnp.where(ones_lane, acc, 0.0), axis=-1, keepdims=True)
            o_ref[:, heads[a]] = jnp.where(ones_lane, 0.0, acc / l * _row_mask(qi, t))
            lse_ref[a] = m * _SCALE + jnp.log(l)

        if ride is not None:
            @pl.when((pl.program_id(0) == MLA_HEADS // hp - 1) & (qi == nb - 1))
            def _():
                rider.finish(*ride)

    qspec = pl.BlockSpec((t, wide), lambda g, i: (i, g))
    kspec = pl.BlockSpec((lp, wide), lambda g, i: (0, g))
    r_in, r_out, r_shapes, r_scratch = _rider_args(rider)
    return pl.pallas_call(
        body, name=name,
        out_shape=[jax.ShapeDtypeStruct((lp, MLA_WIDE), F32),
                   jax.ShapeDtypeStruct((MLA_HEADS, lp, 1), F32)] + r_shapes,
        grid=(MLA_HEADS // hp, nb),
        in_specs=[qspec, kspec, kspec] + r_in,
        out_specs=[qspec, pl.BlockSpec((hp, t, 1), lambda g, i: (g, i, 0))] + r_out,
        scratch_shapes=r_scratch,
        compiler_params=_cparams(dimension_semantics=("arbitrary", "arbitrary")),
    )(q, k, v, *(rider.operands if rider else ()))


def _attn_bwd(q, k, v, do, lse, delta, name, rider=None):
    lp = q.shape[0]
    t = _pick(lp, (384, 256, 128))
    nb = lp // t
    ns = t // STRIP
    hp = HEADS_PER_STEP
    wide = hp * HEAD_SLOT
    heads = [slice(a * HEAD_SLOT, (a + 1) * HEAD_SLOT) for a in range(hp)]
    log2e = math.log2(math.e)

    def body(*refs):
        own, ride = _rider_split(rider, refs, 6, 3, 4)
        (q_ref, k_ref, v_ref, do_ref, lse_ref, delta_ref, dq_ref, dk_ref, dv_ref,
         s_scr, dp_scr, p_scr, ds_scr) = own
        kj = pl.program_id(1)
        if ride is not None:
            @pl.when((pl.program_id(0) == 0) & (kj == 0))
            def _():
                rider.start(*ride)

        @pl.when(kj == 0)
        def _():
            dq_ref[...] = jnp.zeros_like(dq_ref)

        dk_ref[...] = jnp.zeros_like(dk_ref)
        dv_ref[...] = jnp.zeros_like(dv_ref)

        def tile(qi, kind):
            rows = pl.ds(pl.multiple_of(qi * t, t), t)
            for a in range(hp):
                qb, dob = q_ref[rows, heads[a]], do_ref[rows, heads[a]]
                kb, vb = k_ref[:, heads[a]], v_ref[:, heads[a]]
                s_scr[a] = lax.dot_general(qb, kb, _NT, preferred_element_type=F32)
                dp_scr[a] = lax.dot_general(dob, vb, _NT, preferred_element_type=F32)
                lse2 = lse_ref[a, rows, :] * log2e
                delta = delta_ref[a, rows, :]
                for c in range(ns):
                    cs = slice(c * STRIP, (c + 1) * STRIP)
                    pc = jnp.exp2(s_scr[a, :, cs] * _EXP2_SCALE - lse2)
                    pc = jnp.where(_strip_mask(kind, kj, c, t), pc, 0.0)
                    p_scr[a, :, cs] = pc.astype(BF16)
                    ds_scr[a, :, cs] = (pc * (dp_scr[a, :, cs] - delta)).astype(BF16)
                dq_ref[rows, heads[a]] += lax.dot_general(ds_scr[a], kb, _NN,
                                                          preferred_element_type=F32) * _SCALE
                dv_ref[:, heads[a]] += lax.dot_general(p_scr[a], dob, _TN, preferred_element_type=F32)
                dk_ref[:, heads[a]] += lax.dot_general(ds_scr[a], qb, _TN, preferred_element_type=F32)

        tile(kj, 'diag')

        def below(qi, carry):
            tile(qi, 'keys')
            return carry

        lax.fori_loop(kj + 1, nb, below, 0)
        dk_ref[...] = dk_ref[...] * _SCALE

        if ride is not None:
            @pl.when((pl.program_id(0) == MLA_HEADS // hp - 1) & (kj == nb - 1))
            def _():
                rider.finish(*ride)

    whole = pl.BlockSpec((lp, wide), lambda g, j: (0, g))
    kspec = pl.BlockSpec((t, wide), lambda g, j: (j, g))
    stat = pl.BlockSpec((hp, lp, 1), lambda g, j: (g, 0, 0))
    r_in, r_out, r_shapes, r_scratch = _rider_args(rider)
    return pl.pallas_call(
        body, name=name,
        out_shape=[jax.ShapeDtypeStruct((lp, MLA_WIDE), F32)] * 3 + r_shapes,
        grid=(MLA_HEADS // hp, nb),
        in_specs=[whole, kspec, kspec, whole, stat, stat] + r_in,
        out_specs=[whole, kspec, kspec] + r_out,
        scratch_shapes=[pltpu.VMEM((hp, t, t), F32), pltpu.VMEM((hp, t, t), F32),
                        pltpu.VMEM((hp, t, t), BF16), pltpu.VMEM((hp, t, t), BF16)] + r_scratch,
        compiler_params=_cparams(dimension_semantics=("arbitrary", "arbitrary")),
    )(q, k, v, do, lse, delta, *(rider.operands if rider else ()))


def _rope_tables(lp):
    inv = 1.0 / (ROPE_THETA ** (jnp.arange(0, MLA_ROPE, 2, dtype=F32) / MLA_ROPE))
    pos = jnp.maximum(jnp.arange(lp, dtype=jnp.int32) - NPAD, 0).astype(F32)
    ang = pos[:, None] * inv[None, :]
    cos, sin = jnp.cos(ang), jnp.sin(ang)
    z32 = jnp.zeros((lp, HEAD_SLOT - MLA_QK), F32)
    cos_t = jnp.concatenate([jnp.ones((lp, MLA_NOPE), F32), cos, cos, z32], axis=1)
    sin_t = jnp.concatenate([jnp.zeros((lp, MLA_NOPE), F32), sin, sin, z32], axis=1)
    return cos_t, sin_t


def _loss_head(h, target, name):
    lp = h.shape[0]

    def body(h_ref, t_ref, d_ref, loss_ref):
        i = pl.program_id(0)

        @pl.when(i == 0)
        def _():
            d_ref[...] = jnp.zeros_like(d_ref)
            loss_ref[...] = jnp.zeros_like(loss_ref)

        @pl.when(i > 0)
        def _():
            err = h_ref[...] - t_ref[...]
            d_ref[...] = err * (1.0 / D_MODEL)
            loss_ref[...] += jnp.sum(err * err, axis=0, keepdims=True) * (0.5 / D_MODEL)

    return pl.pallas_call(
        body, name=name,
        out_shape=[jax.ShapeDtypeStruct((lp, D_MODEL), F32), jax.ShapeDtypeStruct((1, D_MODEL), F32)],
        grid=(lp // CHUNK,),
        in_specs=[pl.BlockSpec((CHUNK, D_MODEL), lambda i: (i, 0)),
                  pl.BlockSpec((CHUNK, D_MODEL), lambda i: (jnp.maximum(i - 1, 0), 0))],
        out_specs=[pl.BlockSpec((CHUNK, D_MODEL), lambda i: (i, 0)),
                   pl.BlockSpec((1, D_MODEL), lambda i: (0, 0))],
        compiler_params=_cparams(dimension_semantics=("arbitrary",)),
    )(h, target)


def _pad_cols(w, n):
    return jnp.pad(w, [(0, 0)] * (w.ndim - 1) + [(0, n - w.shape[-1])])


def _layer_slab(name, i):
    return i if name.startswith('mlp_') else i // 2


def _prep_matrix(key, raw):
    if key == 'ssd_in':
        return _pad_cols(raw('ssd_w_in'), SSD_IN_PAD).astype(BF16)
    if key == 'mla_in':
        wi = raw('mla_w_in')
        kpe = jnp.pad(wi[:, MLA_Q_RANK + MLA_KV_RANK:], ((0, 0), (MLA_NOPE, HEAD_SLOT - MLA_QK)))
        return jnp.concatenate(
            [wi[:, MLA_Q_RANK:MLA_Q_RANK + MLA_KV_RANK], kpe, wi[:, :MLA_Q_RANK]], axis=1).astype(BF16)
    if key == 'mla_qb':
        qb = raw('mla_w_q_b').reshape(MLA_Q_RANK, MLA_HEADS, MLA_QK)
        return _pad_cols(qb, HEAD_SLOT).reshape(MLA_Q_RANK, MLA_WIDE).astype(BF16)
    if key == 'mla_kvb':
        kvb = raw('mla_w_kv_b').reshape(MLA_KV_RANK, MLA_HEADS, MLA_NOPE + MLA_V)
        kn = _pad_cols(kvb[:, :, :MLA_NOPE], HEAD_SLOT).reshape(MLA_KV_RANK, MLA_WIDE)
        vv = _pad_cols(kvb[:, :, MLA_NOPE:], HEAD_SLOT).reshape(MLA_KV_RANK, MLA_WIDE)
        return jnp.concatenate([kn, vv], axis=1).astype(BF16)
    if key == 'mla_out':
        wo = raw('mla_w_out').reshape(MLA_HEADS, MLA_V, D_MODEL)
        return jnp.pad(wo, ((0, 0), (0, HEAD_SLOT - MLA_V), (0, 0))).reshape(MLA_WIDE, D_MODEL).astype(BF16)
    return raw({'ssd_out': 'ssd_w_out', 'up': 'mlp_w_up', 'down': 'mlp_w_down'}[key]).astype(BF16)


class _Matrices:
    def __init__(self):
        self.p = {k: _Slabs(k, self) for k in ('ssd_in', 'ssd_out', 'mla_in', 'mla_qb', 'mla_kvb',
                                               'mla_out', 'up', 'down')}
        self.made = {}

    def matrix(self, key, slab):
        if (key, slab) not in self.made:
            self.made[(key, slab)] = _prep_matrix(key, lambda n: self.raw(n, slab))
        return self.made[(key, slab)]


class _Slabs:
    def __init__(self, key, owner):
        self.key, self.owner = key, owner

    def __getitem__(self, slab):
        return self.owner.matrix(self.key, slab)


class _ReadyWeights(_Matrices):
    def __init__(self, w):
        super().__init__()
        self.w = w

    def raw(self, name, slab):
        return self.w[name][slab]

    def start(self):
        pass

    def rider(self, host):
        return None

    def deliver(self, host, outs):
        assert not outs


class _KeepGrads:
    def __init__(self):
        self.rounds = {}

    def begin(self, r, grads):
        self.rounds[r] = grads
        return None

    def finish(self, r, outs):
        assert not outs

    def result(self):
        names = {n for g in self.rounds.values() for n in g}
        return {n: jnp.concatenate([self.rounds[r][n] for r in sorted(self.rounds, reverse=True)
                                    if n in self.rounds[r]], axis=0) for n in names}


def _pad128(v):
    return _pad_cols(v.reshape(1, -1), 128)


def _sqrelu(u):
    r = jnp.maximum(u, 0.0)
    return r * r


def _local_step(x, target, w, big=None, red=None):
    seq = x.shape[0]
    lp = NPAD + N_META + seq
    big = _ReadyWeights(w) if big is None else big
    p = big.p
    h = jnp.concatenate([jnp.zeros((NPAD, D_MODEL), F32), w['meta_tokens'], x], axis=0)
    cos_t, sin_t = _rope_tables(lp)
    rt = _pick(lp, (384, 256, 128))
    saved = []
    big.start()
    for i in range(4):
        j = i // 2
        s = {'h0': h}
        g_mix = w['ln_mix'][i].reshape(1, -1)
        g_mlp = w['ln_mlp'][i].reshape(1, -1)
        if i == 0:
            hn = _rms_fwd(h, g_mix, f"rms_mix_f{i}")
        s['hn'] = hn
        if i % 2 == 0:
            rid = big.rider(f"ssd_in_f{i}")
            zxd = _mm(hn, p['ssd_in'][j], 'nn', name=f"ssd_in_f{i}", rider=rid)
            if rid is not None:
                zxd, *got = zxd
                big.deliver(f"ssd_in_f{i}", got)
            consts = _ssd_consts(w['ssd_conv_w'][j], w['ssd_conv_b'][j].reshape(1, -1),
                                 _pad128(w['ssd_dt_bias'][j]), _pad128(w['ssd_a_log'][j]),
                                 _pad128(w['ssd_d'][j]), w['ssd_norm'][j].reshape(1, -1))
            yg, states, *got = _ssd_fwd(zxd, consts, f"ssd_core_f{i}", rider=big.rider(f"ssd_core_f{i}"))
            big.deliver(f"ssd_core_f{i}", got)
            s.update(zxd=zxd, consts=consts, yg=yg, states=states)
            h, hn2 = _mm(yg, p['ssd_out'][j], 'nn', name=f"ssd_out_f{i}", epi=lambda r, hv: hv + r,
                         extras=(h,), norm_gain=g_mlp)
        else:
            lat = _mm(hn, p['mla_in'][j], 'nn', name=f"mla_in_f{i}")
            kvg = w['mla_kv_a_norm'][j].reshape(1, -1)
            qag = w['mla_q_a_norm'][j].reshape(1, -1)
            kvn, qn = _row_call(lambda _, a, b, c, d: _lat_norm(a, b, c, d),
                                [(lat, MLA_KV_RANK, 0), (lat, MLA_Q_RANK, 1)], [kvg, qag],
                                [(MLA_KV_RANK, BF16), (MLA_Q_RANK, BF16)], n_rows=lp, tile=rt,
                                name=f"mla_latnorm_f{i}")
            q_raw = _mm(qn, p['mla_qb'][j], 'nn', name=f"mla_qb_f{i}")
            kv_raw = _mm(kvn, p['mla_kvb'][j], 'nn', name=f"mla_kvb_f{i}")
            qg = _pad_cols(w['mla_q_norm'][j].reshape(1, -1), HEAD_SLOT)
            kg = _pad_cols(w['mla_k_norm'][j].reshape(1, -1), HEAD_SLOT)

            def prep_fwd(_, qr, kn, kpe, vv, cs, sn, qgv, kgv):
                qq, kk = _qk_prep(qr, kn, kpe, cs, sn, qgv, kgv)
                ones = lax.broadcasted_iota(jnp.int32, vv.shape, 1) % HEAD_SLOT == MLA_V
                return qq, kk, jnp.where(ones, 1.0, vv)

            q, k, v = _row_call(prep_fwd,
                                [(q_raw, MLA_WIDE, 0), (kv_raw, MLA_WIDE, 0), (lat, HEAD_SLOT, 2),
                                 (kv_raw, MLA_WIDE, 1), (cos_t, HEAD_SLOT, 0), (sin_t, HEAD_SLOT, 0)],
                                [qg, kg], [(MLA_WIDE, BF16)] * 3, n_rows=lp, tile=rt,
                                name=f"mla_qkprep_f{i}")
            o, lse, *got = _attn_fwd(q, k, v, f"mla_attn_f{i}", rider=big.rider(f"mla_attn_f{i}"))
            big.deliver(f"mla_attn_f{i}", got)
            s.update(lat=lat, kvg=kvg, qag=qag, kvn=kvn, qn=qn, q_raw=q_raw, kv_raw=kv_raw, qg=qg, kg=kg,
                     q=q, k=k, v=v, o=o, lse=lse)
            h, hn2 = _mm(o, p['mla_out'][j], 'nn', name=f"mla_out_f{i}", epi=lambda r, hv: hv + r,
                         extras=(h,), norm_gain=g_mlp)
        s['h1'] = h
        u = _mm(hn2, p['up'][i], 'nn', name=f"mlp_up_f{i}", out_dtype=BF16)
        if i < 3:
            h, hn = _mm(u, p['down'][i], 'nn', name=f"mlp_down_f{i}", a_fn=_sqrelu, epi=lambda r, hv: hv + r,
                        extras=(h,), norm_gain=w['ln_mix'][i + 1].reshape(1, -1))
        else:
            h = _mm(u, p['down'][i], 'nn', name=f"mlp_down_f{i}", a_fn=_sqrelu,
                    epi=lambda r, hv: hv + r, extras=(h,))
        s.update(hn2=hn2, u=u, g_mix=g_mix, g_mlp=g_mlp)
        saved.append(s)

    dh, loss_row = _loss_head(h, target, "loss_head")

    large = {n for n, _ in BIG}
    g = {k_: [None] * (4 if k_ in ('ln_mix', 'ln_mlp') else 2)
         for k_ in ALL_NAMES if k_ != 'meta_tokens' and k_ not in large}
    red = _KeepGrads() if red is None else red
    rounds, pending = {}, None

    def round_of(nm, i):
        return next(r for r, spec in enumerate(REDUCE_ROUNDS)
                    for n, l0, l1 in spec if n == nm and l0 <= _layer_slab(nm, i) < l1)

    def slabs_in(nm, r):
        return next((l0, l1) for n, l0, l1 in REDUCE_ROUNDS[r] if n == nm)

    def dw_into(nm, i, a, b, **kw):
        r = round_of(nm, i)
        (l0, l1), cur = slabs_in(nm, r), rounds.setdefault(r, {})
        cur[nm] = _mm(a, b, 'tn', stack=(l1 - l0, _layer_slab(nm, i) - l0, cur.get(nm)), **kw)

    def put(nm, i, arr):
        rounds.setdefault(round_of(nm, i), {})[nm] = arr[None]

    def hand_over(r):
        nonlocal pending
        pending = (r, red.begin(r, rounds.pop(r)))

    def host(fn, *args):
        nonlocal pending
        if pending is None or pending[1] is None:
            return fn(*args)
        (r, rider), pending = pending, None
        outs = fn(*args, rider=rider)
        own = len(outs) - len(rider.out_shapes)
        red.finish(r, outs[own:])
        return outs[:own]

    for i in reversed(range(4)):
        j = i // 2
        s = saved[i]
        dw_into('mlp_w_down', i, s['u'], dh, name=f"mlp_down_dw{i}", a_fn=_sqrelu)
        du = _mm(dh, p['down'][i], 'nt', name=f"mlp_down_dx{i}", out_dtype=BF16,
                 epi=lambda r, uv: r * (2.0 * jnp.maximum(uv, 0.0)), extras=(s['u'],))
        dw_into('mlp_w_up', i, s['hn2'], du, name=f"mlp_up_dw{i}")
        dh, dg = _mm_rms_bwd(du, p['up'][i], s['h1'], dh, s['g_mlp'], f"mlp_up_dx{i}")
        g['ln_mlp'][i] = dg[0]
        if i % 2 == 0:
            dw_into('ssd_w_out', i, s['yg'], dh, name=f"ssd_out_dw{i}")
            d_yg = _mm(dh, p['ssd_out'][j], 'nt', name=f"ssd_out_dx{i}")
            if i == 0:
                hand_over(1)
            d_zxd, dcw, dcb, ddtb, dalog, ddsk, dng = host(_ssd_bwd, s['zxd'], s['states'], d_yg, s['consts'],
                                                           f"ssd_core_b{i}")
            g['ssd_conv_w'][j], g['ssd_conv_b'][j], g['ssd_norm'][j] = dcw, dcb[0], dng[0]
            g['ssd_dt_bias'][j], g['ssd_a_log'][j], g['ssd_d'][j] = (
                ddtb[0, :SSD_HEADS], dalog[0, :SSD_HEADS], ddsk[0, :SSD_HEADS])
            dw_into('ssd_w_in', i, s['hn'], d_zxd, name=f"ssd_in_dw{i}")
            dh, dg = _mm_rms_bwd(d_zxd, p['ssd_in'][j], s['h0'], dh, s['g_mix'], f"ssd_in_dx{i}")
        else:
            wo = _mm(s['o'], dh, 'tn', name=f"mla_out_dw{i}")
            put('mla_w_out', i, wo.reshape(MLA_HEADS, HEAD_SLOT, D_MODEL)[:, :MLA_V].reshape(-1, D_MODEL))
            dob, delta = _mm_attn_do(dh, p['mla_out'][j], s['o'], f"mla_out_dx{i}")
            dq, dk, dv = host(_attn_bwd, s['q'], s['k'], s['v'], dob, s['lse'], delta, f"mla_attn_b{i}")

            def prep_bwd(_, qr, kn, kpe, cs, sn, dqv, dkv, dvv, qgv, kgv):
                _, vjp = jax.vjp(lambda a, b, c, d, e: _qk_prep(a, b, c, cs, sn, d, e), qr, kn, kpe, qgv, kgv)
                d_qr, d_kn, d_kpe, d_qg, d_kg = vjp((dqv, dkv))
                return d_qr, jnp.concatenate([d_kn, dvv], axis=1), d_kpe, d_qg, d_kg

            d_qraw, d_kvraw, d_kpe, d_qg, d_kg = _row_call(
                prep_bwd,
                [(s['q_raw'], MLA_WIDE, 0), (s['kv_raw'], MLA_WIDE, 0), (s['lat'], HEAD_SLOT, 2),
                 (cos_t, HEAD_SLOT, 0), (sin_t, HEAD_SLOT, 0), (dq, MLA_WIDE, 0), (dk, MLA_WIDE, 0),
                 (dv, MLA_WIDE, 0)],
                [s['qg'], s['kg']], [(MLA_WIDE, BF16), (2 * MLA_WIDE, BF16), (HEAD_SLOT, F32)],
                [(1, HEAD_SLOT), (1, HEAD_SLOT)], n_rows=lp, tile=_pick(lp, (128,)), name=f"mla_qkprep_b{i}")
            g['mla_q_norm'][j], g['mla_k_norm'][j] = d_qg[0, :MLA_QK], d_kg[0, :MLA_QK]
            wqb = _mm(s['qn'], d_qraw, 'tn', name=f"mla_qb_dw{i}")
            put('mla_w_q_b', i, wqb.reshape(MLA_Q_RANK, MLA_HEADS, HEAD_SLOT)[:, :, :MLA_QK].reshape(MLA_Q_RANK, -1))
            d_qn = _mm(d_qraw, p['mla_qb'][j], 'nt', name=f"mla_qb_dx{i}")
            wkvb = _mm(s['kvn'], d_kvraw, 'tn', name=f"mla_kvb_dw{i}").reshape(MLA_KV_RANK, 2, MLA_HEADS, HEAD_SLOT)
            put('mla_w_kv_b', i, jnp.concatenate([wkvb[:, 0, :, :MLA_NOPE], wkvb[:, 1, :, :MLA_V]],
                                                 axis=-1).reshape(MLA_KV_RANK, -1))
            d_kvn = _mm(d_kvraw, p['mla_kvb'][j], 'nt', name=f"mla_kvb_dx{i}")

            def lat_bwd(_, kvl, ql, dkvn, dqn, dkpe, kvgv, qagv):
                _, vjp = jax.vjp(_lat_norm, kvl, ql, kvgv, qagv)
                d_kvl, d_ql, d_kvg, d_qag = vjp((dkvn, dqn))
                return jnp.concatenate([d_kvl, dkpe, d_ql], axis=1), d_kvg, d_qag

            d_lat, d_kvg, d_qag = _row_call(
                lat_bwd, [(s['lat'], MLA_KV_RANK, 0), (s['lat'], MLA_Q_RANK, 1), (d_kvn, MLA_KV_RANK, 0),
                          (d_qn, MLA_Q_RANK, 0), (d_kpe, HEAD_SLOT, 0)],
                [s['kvg'], s['qag']], [(LAT_PAD, BF16)], [(1, MLA_KV_RANK), (1, MLA_Q_RANK)],
                n_rows=lp, tile=rt, name=f"mla_latnorm_b{i}")
            g['mla_kv_a_norm'][j], g['mla_q_a_norm'][j] = d_kvg[0], d_qag[0]
            win = _mm(s['hn'], d_lat, 'tn', name=f"mla_in_dw{i}")
            put('mla_w_in', i, jnp.concatenate(
                [win[:, MLA_KV_RANK + HEAD_SLOT:], win[:, :MLA_KV_RANK],
                 win[:, MLA_KV_RANK + MLA_NOPE:MLA_KV_RANK + MLA_QK]], axis=1))
            dh, dg = _mm_rms_bwd(d_lat, p['mla_in'][j], s['h0'], dh, s['g_mix'], f"mla_in_dx{i}")
        g['ln_mix'][i] = dg[0]
        if i == 2:
            hand_over(0)
    hand_over(2)

    if pending[1] is not None:
        red.finish(pending[0], _run_rider(pending[1], "rs_exchange_last"))
    grads = {k_: jnp.stack(v_) for k_, v_ in g.items()}
    grads['meta_tokens'] = dh[NPAD:NPAD + N_META]
    return loss_row, dh[NPAD + N_META:], grads, red


def _all_gather8(shard, name):
    m_per, n = shard.shape

    def body(x_ref, out_ref, send_sems, recv_sems, local_sem):
        x, y, c = lax.axis_index("x"), lax.axis_index("y"), lax.axis_index("c")
        me, sibling = (x, y, c), (x, y, 1 - c)
        chips = [(1 - x, y), (x, 1 - y), (1 - x, 1 - y)]

        def rows(px, py, pc):
            return out_ref.at[pl.ds((4 * px + 2 * py + pc) * m_per, m_per), :]

        def copy(k, block, to, src=None):
            return pltpu.make_async_remote_copy(
                src_ref=rows(*block) if src is None else src, dst_ref=rows(*block),
                send_sem=send_sems.at[k], recv_sem=recv_sems.at[k], device_id=to, device_id_type=MESH)

        mine = pltpu.make_async_copy(x_ref, rows(*me), local_sem)
        mine.start()
        first = [copy(0, me, sibling, src=x_ref)]
        first += [copy(1 + j, me, (*chip, c), src=x_ref) for j, chip in enumerate(chips)]
        for cp in first:
            cp.start()
        passed = [copy(4 + j, (*chip, c), sibling) for j, chip in enumerate(chips)]
        for j, chip in enumerate(chips):
            copy(1 + j, (*chip, c), me).wait_recv()
            passed[j].start()
        copy(0, sibling, me).wait_recv()
        for j, chip in enumerate(chips):
            copy(4 + j, (*chip, 1 - c), me).wait_recv()
        for cp in first + passed:
            cp.wait_send()
        mine.wait()

    return pl.pallas_call(
        body, name=name,
        out_shape=jax.ShapeDtypeStruct((8 * m_per, n), shard.dtype),
        in_specs=[pl.BlockSpec(memory_space=pl.ANY)],
        out_specs=pl.BlockSpec(memory_space=pl.ANY),
        scratch_shapes=[pltpu.SemaphoreType.DMA((7,)), pltpu.SemaphoreType.DMA((7,)), pltpu.SemaphoreType.DMA],
    )(shard)


def _mesh_pos():
    return lax.axis_index("x"), lax.axis_index("y"), lax.axis_index("c")


def _half_rows(pc, h):
    return pl.ds(pl.multiple_of(pc * h, 16), h)


def _whole_view(ref, kind, shard_shape, k, pc):
    _, r, c = shard_shape
    rows = _half_rows(pc, r // 2)
    if kind == 'row':
        return ref.at[:, k, rows, :]
    if kind == 'col':
        return ref.at[:, rows, pl.ds(pl.multiple_of(k * c, 128), c)]
    return ref.at[k, :, rows, :]


def _whole_shape(kind, shard_shape, rows=None):
    l, r, c = shard_shape
    r = r if rows is None else rows
    return {'row': (l, 4, r, c), 'col': (l, r, 4 * c), 'colx': (4, l, r, c)}[kind]


def _gather_rider(shards, kinds):
    n = len(shards)
    shapes = [s.shape for s in shards]

    def plan(ins, outs, sems):
        send_sems, recv_sems, local_sems = sems
        x, y, c = _mesh_pos()
        me, sibling = (x, y, c), (x, y, 1 - c)
        chips = [(1 - x, y), (x, 1 - y), (1 - x, 1 - y)]

        def place(a, px, py, pc):
            return _whole_view(outs[a], kinds[a], shapes[a], 2 * px + py, pc)

        def own(a):
            return ins[a].at[:, _half_rows(c, shapes[a][1] // 2), :]

        def copy(a, k, block, to, src=None):
            return pltpu.make_async_remote_copy(
                src_ref=place(a, *block) if src is None else src, dst_ref=place(a, *block),
                send_sem=send_sems.at[7 * a + k], recv_sem=recv_sems.at[7 * a + k],
                device_id=to, device_id_type=MESH)

        mine = [pltpu.make_async_copy(own(a), place(a, *me), local_sems.at[a]) for a in range(n)]
        first = [copy(a, 1 + j, me, (*chip, c), src=own(a)) for j, chip in enumerate(chips) for a in range(n)]
        first += [copy(a, 0, me, sibling, src=own(a)) for a in range(n)]
        return copy, mine, first, chips, me, sibling, c

    def start(ins, outs, sems):
        _, mine, first, *_ = plan(ins, outs, sems)
        for cp in first + mine:
            cp.start()

    def finish(ins, outs, sems):
        copy, mine, first, chips, me, sibling, c = plan(ins, outs, sems)
        passed = []
        for j, chip in enumerate(chips):
            for a in range(n):
                copy(a, 1 + j, (*chip, c), me).wait_recv()
                passed.append(copy(a, 4 + j, (*chip, c), sibling))
                passed[-1].start()
        for a in range(n):
            copy(a, 0, sibling, me).wait_recv()
        for j, chip in enumerate(chips):
            for a in range(n):
                copy(a, 4 + j, (*chip, 1 - c), me).wait_recv()
        for cp in first + passed:
            cp.wait_send()
        for cp in mine:
            cp.wait()

    return _Rider(
        shards, [jax.ShapeDtypeStruct(_whole_shape(k, s.shape), s.dtype) for k, s in zip(kinds, shards)],
        [pltpu.SemaphoreType.DMA((7 * n,)), pltpu.SemaphoreType.DMA((7 * n,)), pltpu.SemaphoreType.DMA((n,))],
        start, finish)


def _run_rider(rider, name):
    ni, no = len(rider.operands), len(rider.out_shapes)

    def body(*refs):
        ride = (refs[:ni], refs[ni:ni + no], refs[ni + no:])
        rider.start(*ride)
        rider.finish(*ride)

    return pl.pallas_call(
        body, name=name, out_shape=rider.out_shapes,
        in_specs=[pl.BlockSpec(memory_space=pl.ANY)] * ni,
        out_specs=[pl.BlockSpec(memory_space=pl.ANY)] * no,
        scratch_shapes=rider.scratch,
    )(*rider.operands)


def _rs_swap(wholes, kinds, shapes, name):
    n = len(wholes)

    def body(*refs):
        ins, outs = refs[:n], refs[n:2 * n]
        send_sems, recv_sems = refs[2 * n:]
        x, y, c = _mesh_pos()
        cps = []
        for a in range(n):
            rows = _half_rows(1 - c, shapes[a][1] // 2)
            src = ins[a].at[:, rows, :] if kinds[a] == 'col' else ins[a].at[:, :, rows, :]
            cps.append(pltpu.make_async_remote_copy(
                src_ref=src, dst_ref=outs[a], send_sem=send_sems.at[a], recv_sem=recv_sems.at[a],
                device_id=(x, y, 1 - c), device_id_type=MESH))
        for cp in cps:
            cp.start()
        for cp in cps:
            cp.wait()

    return pl.pallas_call(
        body, name=name,
        out_shape=[jax.ShapeDtypeStruct(_whole_shape(k, s, s[1] // 2), w.dtype)
                   for k, s, w in zip(kinds, shapes, wholes)],
        in_specs=[pl.BlockSpec(memory_space=pl.ANY)] * n,
        out_specs=[pl.BlockSpec(memory_space=pl.ANY)] * n,
        scratch_shapes=[pltpu.SemaphoreType.DMA((n,)), pltpu.SemaphoreType.DMA((n,))],
    )(*wholes)


def _exchange_rider(parts, kinds, shapes):
    n = len(parts)

    def plan(ins, outs, sems):
        send_sems, recv_sems, local_sems = sems
        x, y, c = _mesh_pos()
        kme = 2 * x + y
        chips = [(1 - x, y), (x, 1 - y), (1 - x, 1 - y)]

        def slab(a, k):
            if kinds[a] == 'row':
                return ins[a].at[:, k]
            if kinds[a] == 'col':
                cw = shapes[a][2]
                return ins[a].at[:, :, pl.ds(pl.multiple_of(k * cw, 128), cw)]
            return ins[a].at[k]

        cps = [pltpu.make_async_remote_copy(
            src_ref=slab(a, 2 * px + py), dst_ref=outs[a].at[kme], send_sem=send_sems.at[3 * a + j],
            recv_sem=recv_sems.at[3 * a + j], device_id=(px, py, c), device_id_type=MESH)
            for j, (px, py) in enumerate(chips) for a in range(n)]
        return cps + [pltpu.make_async_copy(slab(a, kme), outs[a].at[kme], local_sems.at[a]) for a in range(n)]

    def start(ins, outs, sems):
        for cp in plan(ins, outs, sems):
            cp.start()

    def finish(ins, outs, sems):
        for cp in plan(ins, outs, sems):
            cp.wait()

    return _Rider(
        parts, [jax.ShapeDtypeStruct((4, s[0], s[1] // 2, s[2]), p.dtype) for s, p in zip(shapes, parts)],
        [pltpu.SemaphoreType.DMA((3 * n,)), pltpu.SemaphoreType.DMA((3 * n,)), pltpu.SemaphoreType.DMA((n,))],
        start, finish)


def _rs_share(shards, slabs, name):
    n = len(shards)

    def body(*refs):
        outs = refs[n:2 * n]
        send_sems, recv_sems = refs[2 * n:]
        x, y, c = _mesh_pos()
        cps = []
        for a in range(n):
            l0, l1 = slabs[a]
            rows = outs[a].at[pl.ds(l0, l1 - l0), _half_rows(c, shards[a].shape[1] // 2), :]
            cps.append(pltpu.make_async_remote_copy(
                src_ref=rows, dst_ref=rows, send_sem=send_sems.at[a], recv_sem=recv_sems.at[a],
                device_id=(x, y, 1 - c), device_id_type=MESH))
        for cp in cps:
            cp.start()
        for cp in cps:
            cp.wait()

    return pl.pallas_call(
        body, name=name,
        out_shape=[jax.ShapeDtypeStruct(s.shape, s.dtype) for s in shards],
        in_specs=[pl.BlockSpec(memory_space=pl.ANY)] * n,
        out_specs=[pl.BlockSpec(memory_space=pl.ANY)] * n,
        input_output_aliases={a: a for a in range(n)},
        scratch_shapes=[pltpu.SemaphoreType.DMA((n,)), pltpu.SemaphoreType.DMA((n,))],
    )(*shards)


def _tile_rows(rows, cols, budget=2 * 1024 * 1024):
    for t in (1024, 512, 256, 128, 64, 32, 16, 8):
        if rows % t == 0 and t * cols * 4 <= budget:
            return t
    return rows


def _add_half(g3, r3, c_idx, name):
    a, h, n = r3.shape
    t = _tile_rows(h, n)
    nt = h // t

    def body(c_ref, g_ref, r_ref, o_ref):
        o_ref[...] = (g_ref[...] + r_ref[...]).astype(o_ref.dtype)

    return pl.pallas_call(
        body, name=name, out_shape=jax.ShapeDtypeStruct((a, h, n), BF16),
        grid_spec=pltpu.PrefetchScalarGridSpec(
            num_scalar_prefetch=1, grid=(a, nt),
            in_specs=[pl.BlockSpec((1, t, n), lambda k, i, c: (k, c[0] * nt + i, 0)),
                      pl.BlockSpec((1, t, n), lambda k, i, c: (k, i, 0))],
            out_specs=pl.BlockSpec((1, t, n), lambda k, i, c: (k, i, 0))),
        compiler_params=_cparams(dimension_semantics=("parallel", "parallel")),
    )(c_idx, g3, r3)


def _sum4(parts, c_idx, name, into):
    _, l, h, n = parts.shape
    n_slabs, l0, buf = into
    t = _tile_rows(h, n, 1024 * 1024)
    nt = h // t
    held = () if buf is None else (buf,)

    def body(c_ref, p_ref, *rest):
        pv = p_ref[...].astype(F32)
        rest[-1][...] = ((pv[0] + pv[1]) + pv[2]) + pv[3]

    return pl.pallas_call(
        body, name=name, out_shape=jax.ShapeDtypeStruct((n_slabs, 2 * h, n), F32),
        grid_spec=pltpu.PrefetchScalarGridSpec(
            num_scalar_prefetch=1, grid=(l, nt),
            in_specs=[pl.BlockSpec((4, 1, t, n), lambda k, i, c: (0, k, i, 0))]
            + [pl.BlockSpec(memory_space=pl.ANY)] * len(held),
            out_specs=pl.BlockSpec((1, t, n), lambda k, i, c: (l0 + k, c[0] * nt + i, 0))),
        input_output_aliases={2: 0} if held else {},
        compiler_params=_cparams(dimension_semantics=("parallel", "parallel")),
    )(c_idx, parts, *held)


def _sum8(parts, name):
    _, m, n = parts.shape

    def body(p_ref, o_ref):
        acc = p_ref[0]
        for d in range(1, 8):
            acc = acc + p_ref[d]
        o_ref[...] = acc

    return pl.pallas_call(body, name=name, out_shape=jax.ShapeDtypeStruct((m, n), F32))(parts)


def _adamw(wp, gp, mp, vp, name):
    r, n = wp.shape
    t = _tile_rows(r, n, 1024 * 1024)

    def body(w_ref, g_ref, m_ref, v_ref, go_ref, d_ref, mo_ref, vo_ref):
        gv = g_ref[...]
        go_ref[...] = gv
        m2 = ADAM_B1 * m_ref[...] + (1.0 - ADAM_B1) * gv
        v2 = ADAM_B2 * v_ref[...] + (1.0 - ADAM_B2) * (gv * gv)
        m_hat = m2 / (1.0 - ADAM_B1 ** ADAM_STEP)
        v_hat = v2 / (1.0 - ADAM_B2 ** ADAM_STEP)
        d_ref[...] = -ADAM_LR * (m_hat / (jnp.sqrt(v_hat) + ADAM_EPS) + ADAM_WD * w_ref[...])
        mo_ref[...] = m2
        vo_ref[...] = v2

    spec = pl.BlockSpec((t, n), lambda i: (i, 0))
    return pl.pallas_call(
        body, name=name, out_shape=[jax.ShapeDtypeStruct((r, n), F32)] * 4, grid=(r // t,),
        in_specs=[spec] * 4, out_specs=[spec] * 4,
        compiler_params=_cparams(dimension_semantics=("parallel",)),
    )(wp, gp, mp, vp)


BIG = (('ssd_w_in', 'colx'), ('ssd_w_out', 'row'), ('mla_w_in', 'row'), ('mla_w_q_b', 'col'),
       ('mla_w_kv_b', 'col'), ('mla_w_out', 'row'), ('mlp_w_up', 'col'), ('mlp_w_down', 'row'))
SMALL_SHARDED = (('meta_tokens', 1), ('ssd_conv_w', 2), ('mla_q_a_norm', 1), ('mla_kv_a_norm', 1))
SMALL_REPL = ('ln_mix', 'ln_mlp', 'ssd_conv_b', 'ssd_dt_bias', 'ssd_a_log', 'ssd_d', 'ssd_norm',
              'mla_q_norm', 'mla_k_norm')
ALL_NAMES = ('meta_tokens', 'ln_mix', 'ln_mlp', 'ssd_w_in', 'ssd_conv_w', 'ssd_conv_b', 'ssd_dt_bias',
             'ssd_a_log', 'ssd_d', 'ssd_norm', 'ssd_w_out', 'mla_w_in', 'mla_q_a_norm', 'mla_w_q_b',
             'mla_kv_a_norm', 'mla_w_kv_b', 'mla_q_norm', 'mla_k_norm', 'mla_w_out', 'mlp_w_up', 'mlp_w_down')


_MLA_BIG = ('mla_w_in', 'mla_w_q_b', 'mla_w_kv_b', 'mla_w_out')
GATHER_ROUNDS = (
    (('ssd_w_in', 0, 1),),
    (('ssd_w_out', 0, 1), ('mlp_w_up', 0, 1)),
    (('mlp_w_down', 0, 1),) + tuple((n, 0, 1) for n in _MLA_BIG) + (('mlp_w_up', 1, 2), ('mlp_w_down', 1, 2)),
    (('ssd_w_in', 1, 2), ('ssd_w_out', 1, 2)) + tuple((n, 1, 2) for n in _MLA_BIG)
    + (('mlp_w_up', 2, 4), ('mlp_w_down', 2, 4)),
)
GATHER_HOSTS = {'ssd_in_f0': 1, 'ssd_core_f0': 2, 'mla_attn_f1': 3}


REDUCE_ROUNDS = (
    GATHER_ROUNDS[3],
    tuple((n, 0, 1) for n in _MLA_BIG) + (('mlp_w_up', 0, 2), ('mlp_w_down', 0, 2), ('ssd_w_out', 0, 1)),
    (('ssd_w_in', 0, 1),),
)


class _GatheredWeights(_Matrices):
    def __init__(self, shards):
        super().__init__()
        self.shards, self.whole = shards, {}

    def raw(self, name, slab):
        return self.whole[(name, slab)]

    def _round(self, r):
        spec = GATHER_ROUNDS[r]
        return _gather_rider([self.shards[n][l0:l1] for n, l0, l1 in spec], [dict(BIG)[n] for n, _, _ in spec])

    def _take(self, r, outs):
        for (n, l0, l1), o in zip(GATHER_ROUNDS[r], outs):
            kind = dict(BIG)[n]
            for l in range(l0, l1):
                if kind == 'row':
                    m = o[l - l0].reshape(-1, o.shape[-1])
                elif kind == 'col':
                    m = o[l - l0]
                else:
                    m = jnp.concatenate([o[k, l - l0] for k in range(4)], axis=-1)
                self.whole[(n, l)] = m

    def start(self):
        self._take(0, _run_rider(self._round(0), "gather_first"))

    def rider(self, host):
        return self._round(GATHER_HOSTS[host]) if host in GATHER_HOSTS else None

    def deliver(self, host, outs):
        if host in GATHER_HOSTS:
            self._take(GATHER_HOSTS[host], outs)


class _ScatterGrads:
    def __init__(self, shard_shapes, c_idx):
        self.shard_shapes, self.c_idx, self.out = shard_shapes, c_idx, {}

    def begin(self, r, grads):
        spec = REDUCE_ROUNDS[r]
        kinds = [dict(BIG)[n] for n, _, _ in spec]
        shapes = [(l1 - l0,) + tuple(self.shard_shapes[n][1:]) for n, l0, l1 in spec]
        wholes = []
        for (n, _, _), kind, s in zip(spec, kinds, shapes):
            if kind == 'row':
                wholes.append(grads[n].reshape(s[0], 4, s[1], s[2]))
            elif kind == 'col':
                wholes.append(grads[n])
            else:
                wholes.append(jnp.stack([grads[n][..., k * s[2]:(k + 1) * s[2]] for k in range(4)]))
        recv = _rs_swap(wholes, kinds, shapes, f"rs_swap{r}")
        parts = []
        for (n, _, _), kind, s, gw, rc in zip(spec, kinds, shapes, wholes, recv):
            if kind == 'col':
                g3, r3 = gw, rc
            else:
                g3, r3 = gw.reshape(-1, s[1], s[2]), rc.reshape(-1, s[1] // 2, s[2])
            parts.append(_add_half(g3, r3, self.c_idx, f"rs_add{r}_{n}").reshape(rc.shape))
        return _exchange_rider(parts, kinds, shapes)

    def finish(self, r, outs):
        spec = REDUCE_ROUNDS[r]
        for (n, l0, _), part in zip(spec, outs):
            self.out[n] = _sum4(part, self.c_idx, f"rs_sum{r}_{n}",
                                into=(self.shard_shapes[n][0], l0, self.out.get(n)))
        shared = _rs_share([self.out[n] for n, _, _ in spec], [(l0, l1) for _, l0, l1 in spec], f"rs_share{r}")
        self.out.update(zip([n for n, _, _ in spec], shared))


def _pack(arrs, rows_mult):
    flat = jnp.concatenate([a.reshape(-1) for a in arrs])
    per = LANES * rows_mult
    pad = (-flat.shape[0]) % per
    if pad:
        flat = jnp.concatenate([flat, jnp.zeros((pad,), flat.dtype)])
    return flat.reshape(-1, LANES)


def _unpack(pack, shapes):
    flat = pack.reshape(-1)
    out, off = [], 0
    for shp in shapes:
        n = math.prod(shp)
        out.append(flat[off:off + n].reshape(shp))
        off += n
    return out


def _split4(full, axis):
    shp = full.shape
    r = full.reshape(shp[:axis] + (4, shp[axis] // 4) + shp[axis + 1:])
    return jnp.moveaxis(r, axis, 0)


def _join4(parts, axis):
    r = jnp.moveaxis(parts, 0, axis)
    shp = r.shape
    return r.reshape(shp[:axis] + (shp[axis] * shp[axis + 1],) + shp[axis + 2:])


def _gather_params(shards, table, dtype, c, name):
    pack = _pack([shards[n].astype(dtype) for n, _ in table], 16)
    half = pack.shape[0] // 2
    mine = lax.dynamic_slice_in_dim(pack, c * half, half, axis=0)
    full = _all_gather8(mine, name).reshape(4, -1)
    out, off = {}, 0
    for n, ax in table:
        cnt = math.prod(shards[n].shape)
        out[n] = _join4(full[:, off:off + cnt].reshape((4,) + shards[n].shape), ax)
        off += cnt
    return out


def kernel(x, meta_tokens, ln_mix, ln_mlp, ssd_w_in, ssd_conv_w, ssd_conv_b, ssd_dt_bias, ssd_a_log, ssd_d, ssd_norm, ssd_w_out, mla_w_in, mla_q_a_norm, mla_w_q_b, mla_kv_a_norm, mla_w_kv_b, mla_q_norm, mla_k_norm, mla_w_out, mlp_w_up, mlp_w_down, loss_target, m_meta_tokens, m_ln_mix, m_ln_mlp, m_ssd_w_in, m_ssd_conv_w, m_ssd_conv_b, m_ssd_dt_bias, m_ssd_a_log, m_ssd_d, m_ssd_norm, m_ssd_w_out, m_mla_w_in, m_mla_q_a_norm, m_mla_w_q_b, m_mla_kv_a_norm, m_mla_w_kv_b, m_mla_q_norm, m_mla_k_norm, m_mla_w_out, m_mlp_w_up, m_mlp_w_down, v_meta_tokens, v_ln_mix, v_ln_mlp, v_ssd_w_in, v_ssd_conv_w, v_ssd_conv_b, v_ssd_dt_bias, v_ssd_a_log, v_ssd_d, v_ssd_norm, v_ssd_w_out, v_mla_w_in, v_mla_q_a_norm, v_mla_w_q_b, v_mla_kv_a_norm, v_mla_w_kv_b, v_mla_q_norm, v_mla_k_norm, v_mla_w_out, v_mlp_w_up, v_mlp_w_down):
    w_sh = dict(meta_tokens=meta_tokens, ln_mix=ln_mix, ln_mlp=ln_mlp, ssd_w_in=ssd_w_in, ssd_conv_w=ssd_conv_w, ssd_conv_b=ssd_conv_b, ssd_dt_bias=ssd_dt_bias, ssd_a_log=ssd_a_log, ssd_d=ssd_d, ssd_norm=ssd_norm, ssd_w_out=ssd_w_out, mla_w_in=mla_w_in, mla_q_a_norm=mla_q_a_norm, mla_w_q_b=mla_w_q_b, mla_kv_a_norm=mla_kv_a_norm, mla_w_kv_b=mla_w_kv_b, mla_q_norm=mla_q_norm, mla_k_norm=mla_k_norm, mla_w_out=mla_w_out, mlp_w_up=mlp_w_up, mlp_w_down=mlp_w_down)
    m_sh = dict(meta_tokens=m_meta_tokens, ln_mix=m_ln_mix, ln_mlp=m_ln_mlp, ssd_w_in=m_ssd_w_in, ssd_conv_w=m_ssd_conv_w, ssd_conv_b=m_ssd_conv_b, ssd_dt_bias=m_ssd_dt_bias, ssd_a_log=m_ssd_a_log, ssd_d=m_ssd_d, ssd_norm=m_ssd_norm, ssd_w_out=m_ssd_w_out, mla_w_in=m_mla_w_in, mla_q_a_norm=m_mla_q_a_norm, mla_w_q_b=m_mla_w_q_b, mla_kv_a_norm=m_mla_kv_a_norm, mla_w_kv_b=m_mla_w_kv_b, mla_q_norm=m_mla_q_norm, mla_k_norm=m_mla_k_norm, mla_w_out=m_mla_w_out, mlp_w_up=m_mlp_w_up, mlp_w_down=m_mlp_w_down)
    v_sh = dict(meta_tokens=v_meta_tokens, ln_mix=v_ln_mix, ln_mlp=v_ln_mlp, ssd_w_in=v_ssd_w_in, ssd_conv_w=v_ssd_conv_w, ssd_conv_b=v_ssd_conv_b, ssd_dt_bias=v_ssd_dt_bias, ssd_a_log=v_ssd_a_log, ssd_d=v_ssd_d, ssd_norm=v_ssd_norm, ssd_w_out=v_ssd_w_out, mla_w_in=v_mla_w_in, mla_q_a_norm=v_mla_q_a_norm, mla_w_q_b=v_mla_w_q_b, mla_kv_a_norm=v_mla_kv_a_norm, mla_w_kv_b=v_mla_w_kv_b, mla_q_norm=v_mla_q_norm, mla_k_norm=v_mla_k_norm, mla_w_out=v_mla_w_out, mlp_w_up=v_mlp_w_up, mlp_w_down=v_mlp_w_down)

    cx, cy, cc = lax.axis_index("x"), lax.axis_index("y"), lax.axis_index("c")
    chip = 2 * cx + cy

    c_idx = cc.reshape(1).astype(jnp.int32)
    big_names = [n for n, _ in BIG]
    shapes = [w_sh[n].shape for n in big_names]

    w = {n: w_sh[n] for n in SMALL_REPL}
    w.update(_gather_params(w_sh, SMALL_SHARDED, F32, cc, "gather_small"))
    big = _GatheredWeights({n: w_sh[n].astype(BF16) for n in big_names})
    red = _ScatterGrads({n: w_sh[n].shape for n in big_names}, c_idx)

    loss_row, grad_x, grads, red = _local_step(x[0], loss_target[0], w, big, red)
    loss = lax.psum(jnp.sum(loss_row), ("x", "y", "c"))
    g_sh = dict(red.out)

    small_names = tuple(n for n, _ in SMALL_SHARDED) + SMALL_REPL
    sp = _pack([grads[n] for n in small_names], 8)
    srows = sp.shape[0]
    s_all = _sum8(_all_gather8(sp, "ar_small_gather").reshape(8, srows, LANES), "ar_small_sum")
    s_full = dict(zip(small_names, _unpack(s_all, [grads[n].shape for n in small_names])))
    for n, ax in SMALL_SHARDED:
        g_sh[n] = lax.dynamic_index_in_dim(_split4(s_full[n], ax), chip, axis=0, keepdims=False)
    for n in SMALL_REPL:
        g_sh[n] = s_full[n]

    delta, new_m, new_v = {}, {}, {}
    for n, s in zip(big_names, shapes):
        res = _adamw(*[t[n].reshape(-1, s[2]) for t in (w_sh, g_sh, m_sh, v_sh)], f"adamw_{n}")
        g_sh[n], delta[n], new_m[n], new_v[n] = [r.reshape(s) for r in res]
    _, d_s, m_s, v_s = _adamw(*[_pack([t[n] for n in small_names], 8) for t in (w_sh, g_sh, m_sh, v_sh)],
                              "adamw_small")
    for dst, ps in ((delta, d_s), (new_m, m_s), (new_v, v_s)):
        dst.update(zip(small_names, _unpack(ps, [w_sh[n].shape for n in small_names])))

    return (loss, grad_x[None], *[g_sh[n] for n in ALL_NAMES], *[delta[n] for n in ALL_NAMES],
            *[new_m[n] for n in ALL_NAMES], *[new_v[n] for n in ALL_NAMES])
```

```python
import functools
import math

import jax
import jax.numpy as jnp
from jax import lax
from jax.experimental import pallas as pl
from jax.experimental.pallas import tpu as pltpu

F32 = jnp.float32
BF16 = jnp.bfloat16
MESH = pl.DeviceIdType.MESH
_NN = (((1,), (0,)), ((), ()))
_NT = (((1,), (1,)), ((), ()))
_TN = (((0,), (0,)), ((), ()))

D_MODEL = 1024
N_META = 16
EPS = 1e-6
SSD_D_INNER = 2048
SSD_HEADS = 32
SSD_HEAD_DIM = 64
SSD_GROUPS = 8
SSD_HPG = 4
SSD_STATE = 128
SSD_CONV = 4
CHUNK = 128
SSD_IN_DIM = 6176
SSD_IN_PAD = 6272
MLA_HEADS = 16
MLA_NOPE = 64
MLA_ROPE = 32
MLA_V = 64
MLA_QK = 96
MLA_Q_RANK = 384
MLA_KV_RANK = 256
HEAD_SLOT = 128
MLA_WIDE = MLA_HEADS * HEAD_SLOT
HEADS_PER_STEP = 2
LAT_PAD = 768
ROPE_THETA = 10000.0
D_FF = 4096
NPAD = CHUNK - N_META
ADAM_LR, ADAM_B1, ADAM_B2, ADAM_EPS, ADAM_WD, ADAM_STEP = 0.001, 0.9, 0.999, 1e-08, 0.01, 10
LANES = 1024
V7X_VMEM_BYTES = 64 * 1024 * 1024
VMEM_LIMIT = V7X_VMEM_BYTES * 7 // 8


def _pick(n, cands):
    for c in cands:
        if n % c == 0:
            return c
    return n


def _cparams(**kw):
    return pltpu.CompilerParams(vmem_limit_bytes=VMEM_LIMIT, **kw)


def _mm(a, b, dims, *, name, out_dtype=F32, a_fn=None, epi=None, extras=(), stack=None, norm_gain=None,
        rider=None):
    if dims == 'nn':
        (M, K), (K2, N) = a.shape, b.shape
    elif dims == 'nt':
        (M, K), (N, K2) = a.shape, b.shape
    else:
        (K, M), (K2, N) = a.shape, b.shape
    assert K == K2, (a.shape, b.shape, dims)
    if dims == 'tn':
        tm = _pick(M, (1024, 768, 512, 384, 256, 128))
        tn = _pick(N, (1024, 896, 768, 512, 384, 256, 128))
        tk = _pick(K, (1408, 1024, 512, 384, 256, 128))
    else:
        tm = _pick(M, (704, 512, 384, 256, 128) if norm_gain is not None else (1408, 1024, 512, 384, 256, 128))
        tn = _pick(N, (1024, 896, 768, 512, 384, 256, 128))
        tk = _pick(K, (1024, 896, 768, 512, 384, 256, 128))
    nk = K // tk
    if dims == 'nn':
        a_spec = pl.BlockSpec((tm, tk), lambda i, j, k: (i, k))
        b_spec = pl.BlockSpec((tk, tn), lambda i, j, k: (k, j))
        dn = (((1,), (0,)), ((), ()))
    elif dims == 'nt':
        a_spec = pl.BlockSpec((tm, tk), lambda i, j, k: (i, k))
        b_spec = pl.BlockSpec((tn, tk), lambda i, j, k: (j, k))
        dn = (((1,), (1,)), ((), ()))
    else:
        a_spec = pl.BlockSpec((tk, tm), lambda i, j, k: (k, i))
        b_spec = pl.BlockSpec((tk, tn), lambda i, j, k: (k, j))
        dn = (((0,), (0,)), ((), ()))
    o_spec = pl.BlockSpec((tm, tn), lambda i, j, k: (i, j))
    n_ex = len(extras)
    out_shape = jax.ShapeDtypeStruct((M, N), out_dtype)
    out_spec, held, aliases = o_spec, (), {}
    if stack is not None:
        n_slabs, slab, buf = stack
        out_shape = jax.ShapeDtypeStruct((n_slabs, M, N), out_dtype)
        out_spec = pl.BlockSpec((None, tm, tn), lambda i, j, k: (slab, i, j))
        if buf is not None:
            held, aliases = (buf,), {2 + n_ex: 0}

    gains = ()
    if norm_gain is not None:
        assert tn == N and stack is None, "the rms epilogue needs whole rows"
        gains = (norm_gain,)
        out_shape = [out_shape, jax.ShapeDtypeStruct((M, N), BF16)]
        out_spec = [out_spec, o_spec]

    n_own_in = 2 + n_ex + len(gains) + len(held)
    steps = (M // tm, N // tn, nk)

    def body(*refs):
        (a_ref, b_ref, *rest), ride = _rider_split(rider, refs, n_own_in, 1 + len(gains), 1)
        ex_refs, rest = rest[:n_ex], rest[n_ex:]
        g_refs, rest = rest[:len(gains)], rest[len(gains) + len(held):]
        o_ref, acc = rest[0], rest[-1]
        k = pl.program_id(2)
        if ride is not None:
            @pl.when((pl.program_id(0) == 0) & (pl.program_id(1) == 0) & (k == 0))
            def _():
                rider.start(*ride)

        @pl.when(k == 0)
        def _():
            acc[...] = jnp.zeros_like(acc)

        av = a_ref[...]
        if a_fn is not None:
            av = a_fn(av)
        acc[...] += lax.dot_general(av.astype(BF16), b_ref[...].astype(BF16), dn,
                                    preferred_element_type=F32)

        @pl.when(k == nk - 1)
        def _():
            r = acc[...]
            if epi is not None:
                r = epi(r, *[e[...] for e in ex_refs])
            o_ref[...] = r.astype(out_dtype)
            if gains:
                rest[1][...] = _rms(r, g_refs[0][...]).astype(BF16)

        if ride is not None:
            @pl.when((pl.program_id(0) == steps[0] - 1) & (pl.program_id(1) == steps[1] - 1) & (k == nk - 1))
            def _():
                rider.finish(*ride)

    r_in, r_out, r_shapes, r_scratch = _rider_args(rider)
    own_shapes = out_shape if isinstance(out_shape, list) else [out_shape]
    own_specs = out_spec if isinstance(out_spec, list) else [out_spec]
    res = pl.pallas_call(
        body, name=name,
        out_shape=own_shapes + r_shapes,
        grid=steps,
        in_specs=[a_spec, b_spec] + [o_spec] * n_ex
        + [pl.BlockSpec((1, tn), lambda i, j, k: (0, j))] * len(gains)
        + [pl.BlockSpec(memory_space=pl.ANY)] * len(held) + r_in,
        out_specs=own_specs + r_out,
        input_output_aliases=aliases,
        scratch_shapes=[pltpu.VMEM((tm, tn), F32)] + r_scratch,
        compiler_params=_cparams(dimension_semantics=("arbitrary", "arbitrary", "arbitrary")),
    )(a, b, *extras, *gains, *held, *(rider.operands if rider else ()))
    return res[0] if len(res) == 1 else res


def _mm_rms_bwd(cot, w_t, h, d_res, gain, name):
    (M, K), (N, _) = cot.shape, w_t.shape
    tm = _pick(M, (704, 512, 384, 256, 128))
    tk = _pick(K, (1024, 896, 768, 512, 384, 256, 128))
    nk = K // tk

    def body(a_ref, b_ref, h_ref, r_ref, g_ref, dh_ref, dg_ref, acc):
        i, k = pl.program_id(0), pl.program_id(1)

        @pl.when(k == 0)
        def _():
            acc[...] = jnp.zeros_like(acc)

        acc[...] += lax.dot_general(a_ref[...].astype(BF16), b_ref[...].astype(BF16), _NT,
                                    preferred_element_type=F32)

        @pl.when(k == nk - 1)
        def _():
            _, vjp = jax.vjp(_rms, h_ref[...], g_ref[...])
            dh, dg = vjp(acc[...])
            dh_ref[...] = (r_ref[...] + dh) * _row_mask(i, tm)

            @pl.when(i == 0)
            def _():
                dg_ref[...] = dg

            @pl.when(i > 0)
            def _():
                dg_ref[...] += dg

    rows = pl.BlockSpec((tm, N), lambda i, k: (i, 0))
    vec = pl.BlockSpec((1, N), lambda i, k: (0, 0))
    return pl.pallas_call(
        body, name=name,
        out_shape=[jax.ShapeDtypeStruct((M, N), F32), jax.ShapeDtypeStruct((1, N), F32)],
        grid=(M // tm, nk),
        in_specs=[pl.BlockSpec((tm, tk), lambda i, k: (i, k)), pl.BlockSpec((N, tk), lambda i, k: (0, k)),
                  rows, rows, vec],
        out_specs=[rows, vec],
        scratch_shapes=[pltpu.VMEM((tm, N), F32)],
        compiler_params=_cparams(dimension_semantics=("arbitrary", "arbitrary")),
    )(cot, w_t, h, d_res, gain)


def _mm_attn_do(dh, w_out_t, o, name):
    (M, K), (N, _) = dh.shape, w_out_t.shape
    tm = _pick(M, (704, 512, 384, 256, 128))
    tn = 8 * HEAD_SLOT

    def body(a_ref, b_ref, o_ref, dob_ref, delta_ref):
        do = lax.dot_general(a_ref[...].astype(BF16), b_ref[...], _NT, preferred_element_type=F32)
        dob_ref[...] = do.astype(BF16)
        for hh in range(tn // HEAD_SLOT):
            sl = slice(hh * HEAD_SLOT, (hh + 1) * HEAD_SLOT)
            delta_ref[hh] = jnp.sum(do[:, sl] * o_ref[:, sl], axis=-1, keepdims=True)

    tile = pl.BlockSpec((tm, tn), lambda i, j: (i, j))
    return pl.pallas_call(
        body, name=name,
        out_shape=[jax.ShapeDtypeStruct((M, N), BF16), jax.ShapeDtypeStruct((N // HEAD_SLOT, M, 1), F32)],
        grid=(M // tm, N // tn),
        in_specs=[pl.BlockSpec((tm, K), lambda i, j: (i, 0)), pl.BlockSpec((tn, K), lambda i, j: (j, 0)), tile],
        out_specs=[tile, pl.BlockSpec((tn // HEAD_SLOT, tm, 1), lambda i, j: (j, i, 0))],
        compiler_params=_cparams(dimension_semantics=("parallel", "parallel")),
    )(dh, w_out_t, o)


def _row_call(fn, rows, consts, out_rows, out_accs=(), *, n_rows, tile, name):
    n_r, n_c, n_o, n_a = len(rows), len(consts), len(out_rows), len(out_accs)
    steps = n_rows // tile

    def body(*refs):
        r_refs = refs[:n_r]
        c_refs = refs[n_r:n_r + n_c]
        o_refs = refs[n_r + n_c:n_r + n_c + n_o]
        a_refs = refs[n_r + n_c + n_o:]
        i = pl.program_id(0)
        res = fn(i, *[r[...] for r in r_refs], *[c[...] for c in c_refs])
        for o_ref, val in zip(o_refs, res[:n_o]):
            o_ref[...] = val.astype(o_ref.dtype)

        @pl.when(i == 0)
        def _():
            for a_ref in a_refs:
                a_ref[...] = jnp.zeros_like(a_ref)

        for a_ref, val in zip(a_refs, res[n_o:]):
            a_ref[...] += val

    in_specs = [pl.BlockSpec((tile, w), functools.partial(lambda i, cb: (i, cb), cb=cb))
                for (_, w, cb) in rows]
    in_specs += [pl.BlockSpec(c.shape, lambda i: (0, 0)) for c in consts]
    out_specs = [pl.BlockSpec((tile, c), lambda i: (i, 0)) for (c, _) in out_rows]
    out_specs += [pl.BlockSpec(s, lambda i: (0, 0)) for s in out_accs]
    out_shape = [jax.ShapeDtypeStruct((n_rows, c), dt) for (c, dt) in out_rows]
    out_shape += [jax.ShapeDtypeStruct(s, F32) for s in out_accs]
    return pl.pallas_call(
        body, name=name, out_shape=out_shape, grid=(steps,),
        in_specs=in_specs, out_specs=out_specs,
        compiler_params=_cparams(dimension_semantics=("arbitrary",)),
    )(*[r[0] for r in rows], *consts)


def _row_mask(i, tile):
    r = i * tile + lax.broadcasted_iota(jnp.int32, (tile, 1), 0)
    return (r >= NPAD).astype(F32)


def _rms(x, g):
    return x * lax.rsqrt(jnp.mean(x * x, axis=-1, keepdims=True) + EPS) * g


def _silu(x):
    return x * (0.5 * jnp.tanh(0.5 * x) + 0.5)


def _softplus(x):
    return jnp.maximum(x, 0.0) + jnp.log(1.0 + jnp.exp(-jnp.abs(x)))


def _rms_fwd(h, g, name):
    lp = h.shape[0]
    return _row_call(lambda i, hv, gv: (_rms(hv, gv),), [(h, D_MODEL, 0)], [g],
                     [(D_MODEL, BF16)], n_rows=lp, tile=_pick(lp, (384, 256, 128)), name=name)[0]


@functools.partial(jax.custom_vjp, nondiff_argnums=(1,))
def _roll_rows(x, s):
    return pltpu.roll(x, s, 0)


def _roll_rows_fwd(x, s):
    return pltpu.roll(x, s, 0), None


def _roll_rows_bwd(s, _, ct):
    return (pltpu.roll(ct, (ct.shape[0] - s) % ct.shape[0], 0),)


_roll_rows.defvjp(_roll_rows_fwd, _roll_rows_bwd)


def _conv_silu(cur, halo, w_rows, b):
    full = jnp.concatenate([halo, cur], axis=0)
    acc = cur * w_rows[SSD_CONV - 1] + b
    for k in range(SSD_CONV - 1):
        acc = acc + _roll_rows(full, SSD_CONV - 1 - k)[8:] * w_rows[k]
    return _silu(acc)


def _split3(v):
    hi = v.astype(BF16)
    r1 = v - hi.astype(F32)
    mid = r1.astype(BF16)
    lo = (r1 - mid.astype(F32)).astype(BF16)
    return hi, mid, lo


def _select_right(v, sel, dn):
    return sum(lax.dot_general(p, sel, dn, preferred_element_type=F32) for p in _split3(v))


@jax.custom_vjp
def _expand_heads(v, e_mat):
    return _select_right(v, e_mat, _NN)


def _expand_heads_fwd(v, e_mat):
    return _select_right(v, e_mat, _NN), e_mat


def _expand_heads_bwd(e_mat, ct):
    return _select_right(ct, e_mat, _NT), jnp.zeros_like(e_mat)


_expand_heads.defvjp(_expand_heads_fwd, _expand_heads_bwd)


@jax.custom_vjp
def _cumsum_rows(a, tri):
    return sum(lax.dot_general(tri, p, _NN, preferred_element_type=F32) for p in _split3(a))


def _cumsum_rows_fwd(a, tri):
    return _cumsum_rows(a, tri), tri


def _cumsum_rows_bwd(tri, ct):
    return (sum(lax.dot_general(tri, p, _TN, preferred_element_type=F32) for p in _split3(ct)),
            jnp.zeros_like(tri))


_cumsum_rows.defvjp(_cumsum_rows_fwd, _cumsum_rows_bwd)


def _ssd_chunk(mask, z, xs_pre, bc_pre, halo_x, halo_bc, dt_pre, st, cwx0, cwx1, cwx2, cwx3,
               cwb0, cwb1, cwb2, cwb3, cb_x, cb_bc, dtb, alog, dsk, ng):
    L = CHUNK
    lane_h = lax.broadcasted_iota(jnp.int32, (1, 128), 1)
    head_ok = (lane_h < SSD_HEADS).astype(F32)
    e_mat = (lax.broadcasted_iota(jnp.int32, (128, SSD_D_INNER), 1) // SSD_HEAD_DIM
             == lax.broadcasted_iota(jnp.int32, (128, SSD_D_INNER), 0)).astype(BF16)
    ri = lax.broadcasted_iota(jnp.int32, (L, L), 0)
    ci = lax.broadcasted_iota(jnp.int32, (L, L), 1)
    causal = ri >= ci

    xs = _conv_silu(xs_pre, halo_x, (cwx0, cwx1, cwx2, cwx3), cb_x) * mask
    bc = _conv_silu(bc_pre, halo_bc, (cwb0, cwb1, cwb2, cwb3), cb_bc) * mask
    dt = _softplus(dt_pre + dtb) * mask * head_ok
    a_dt = dt * (-jnp.exp(alog))
    a_cs = _cumsum_rows(a_dt, causal.astype(BF16))
    a_cs_t = a_cs.T
    row8 = lax.broadcasted_iota(jnp.int32, (8, 128), 0)
    last8 = jnp.where(row8 == 0, jnp.sum(a_dt, axis=0, keepdims=True), 0.0)
    dsk8 = jnp.where(row8 == 0, dsk, 0.0)
    wide = _expand_heads(jnp.concatenate([dt, a_cs, last8, dsk8], axis=0), e_mat)
    dt_e, acs_e = wide[0:L], wide[L:2 * L]
    last_e = jnp.sum(wide[2 * L:2 * L + 8], axis=0, keepdims=True)
    d_e = jnp.sum(wide[2 * L + 8:2 * L + 16], axis=0, keepdims=True)
    xdt = xs * dt_e
    dte_e = jnp.exp(last_e - acs_e)
    dfs_e = jnp.exp(acs_e)
    cd_e = jnp.exp(last_e)
    sub_h = lax.broadcasted_iota(jnp.int32, (128, L), 0)
    lane_hl = lax.broadcasted_iota(jnp.int32, (L, 128), 1)
    lane_g = lax.broadcasted_iota(jnp.int32, (1, SSD_HPG * SSD_HEAD_DIM), 1) // SSD_HEAD_DIM

    ys, new_st = [], []
    for g in range(SSD_GROUPS):
        b_g = bc[:, g * 128:(g + 1) * 128].astype(BF16)
        c_g = bc[:, 1024 + g * 128:1024 + (g + 1) * 128].astype(BF16)
        gs = slice(g * 256, (g + 1) * 256)
        xdt_g = xdt[:, gs]
        cb = lax.dot_general(c_g, b_g, (((1,), (1,)), ((), ())), preferred_element_type=F32)
        st_g = st[g * 128:(g + 1) * 128, :]
        y_g = lax.dot_general(c_g, st_g.astype(BF16), (((1,), (0,)), ((), ())),
                              preferred_element_type=F32) * dfs_e[:, gs]
        for j in range(SSD_HPG):
            h = g * SSD_HPG + j
            col = jnp.sum(jnp.where(lane_hl == h, a_cs, 0.0), axis=1, keepdims=True)
            row = jnp.sum(jnp.where(sub_h == h, a_cs_t, 0.0), axis=0, keepdims=True)
            dec = jnp.where(causal, jnp.exp(jnp.where(causal, col - row, 0.0)), 0.0)
            m_h = (cb * dec).astype(BF16)
            x_h = jnp.where(lane_g == j, xdt_g, 0.0).astype(BF16)
            y_g = y_g + lax.dot_general(m_h, x_h, (((1,), (0,)), ((), ())),
                                        preferred_element_type=F32)
        s_new = lax.dot_general(b_g, (xdt_g * dte_e[:, gs]).astype(BF16), (((0,), (0,)), ((), ())),
                                preferred_element_type=F32)
        new_st.append(st_g * cd_e[:, gs] + s_new)
        ys.append(y_g)
    y = jnp.concatenate(ys, axis=1) + xs * d_e
    gg = y * _silu(z)
    outs = []
    for g in range(SSD_GROUPS):
        sl = gg[:, g * 256:(g + 1) * 256]
        outs.append(sl * lax.rsqrt(jnp.mean(sl * sl, axis=-1, keepdims=True) + EPS))
    out = jnp.concatenate(outs, axis=1) * ng
    return out, jnp.concatenate(new_st, axis=0)


def _ssd_consts(conv_w, conv_b, dtb, alog, dsk, ng):
    return [conv_w, conv_b, dtb, alog, dsk, ng]


def _ssd_param_vals(cw_ref, cb_ref, dtb_ref, alog_ref, dsk_ref, ng_ref):
    cwx = [cw_ref[k:k + 1, 0:SSD_D_INNER] for k in range(SSD_CONV)]
    cwb = [cw_ref[k:k + 1, SSD_D_INNER:2 * SSD_D_INNER] for k in range(SSD_CONV)]
    return (*cwx, *cwb, cb_ref[:, 0:SSD_D_INNER], cb_ref[:, SSD_D_INNER:2 * SSD_D_INNER],
            dtb_ref[...], alog_ref[...], dsk_ref[...], ng_ref[...])


def _ssd_in_specs(rev, nc):
    def cidx(i):
        return (nc - 1 - i) if rev else i

    def halo(cb):
        return pl.BlockSpec((8, SSD_D_INNER), lambda i: (jnp.maximum(16 * cidx(i) - 1, 0), cb))

    return [
        pl.BlockSpec((CHUNK, SSD_D_INNER), lambda i: (cidx(i), 0)),
        pl.BlockSpec((CHUNK, SSD_D_INNER), lambda i: (cidx(i), 1)),
        pl.BlockSpec((CHUNK, SSD_D_INNER), lambda i: (cidx(i), 2)),
        halo(1), halo(2),
        pl.BlockSpec((CHUNK, 128), lambda i: (cidx(i), 48)),
    ]


class _Rider:
    def __init__(self, operands, out_shapes, scratch, start, finish):
        self.operands, self.out_shapes, self.scratch = list(operands), list(out_shapes), list(scratch)
        self.start, self.finish = start, finish


def _rider_split(rider, refs, n_in, n_out, n_scratch):
    if rider is None:
        return refs, None
    ni, no = len(rider.operands), len(rider.out_shapes)
    own = refs[:n_in] + refs[n_in + ni:n_in + ni + n_out] + refs[n_in + ni + n_out + no:n_in + ni + n_out + no + n_scratch]
    mine = (refs[n_in:n_in + ni], refs[n_in + ni + n_out:n_in + ni + n_out + no],
            refs[n_in + ni + n_out + no + n_scratch:])
    return own, mine


def _rider_args(rider):
    if rider is None:
        return [], [], [], []
    hbm = pl.BlockSpec(memory_space=pl.ANY)
    return ([hbm] * len(rider.operands), [hbm] * len(rider.out_shapes), rider.out_shapes, rider.scratch)


def _ssd_fwd(zxd, consts, name, rider=None):
    lp = zxd.shape[0]
    nc = lp // CHUNK

    def body(*refs):
        own, ride = _rider_split(rider, refs, 12, 2, 1)
        (z_ref, xs_ref, bc_ref, hx_ref, hb_ref, dt_ref, cw_ref, cb_ref, dtb_ref, alog_ref,
         dsk_ref, ng_ref, y_ref, st_ref, state) = own
        c = pl.program_id(0)

        @pl.when(c == 0)
        def _():
            state[...] = jnp.zeros_like(state)
            if ride is not None:
                rider.start(*ride)

        live = (c > 0).astype(F32)
        st_ref[0] = state[...]
        out, st_new = _ssd_chunk(
            _row_mask(c, CHUNK), z_ref[...], xs_ref[...], bc_ref[...], hx_ref[...] * live,
            hb_ref[...] * live, dt_ref[...], state[...],
            *_ssd_param_vals(cw_ref, cb_ref, dtb_ref, alog_ref, dsk_ref, ng_ref))
        y_ref[...] = out.astype(y_ref.dtype)
        state[...] = st_new

        if ride is not None:
            @pl.when(c == nc - 1)
            def _():
                rider.finish(*ride)

    r_in, r_out, r_shapes, r_scratch = _rider_args(rider)
    return pl.pallas_call(
        body, name=name,
        out_shape=[jax.ShapeDtypeStruct((lp, SSD_D_INNER), BF16),
                   jax.ShapeDtypeStruct((nc, SSD_GROUPS * SSD_STATE, 256), F32)] + r_shapes,
        grid=(nc,),
        in_specs=_ssd_in_specs(False, nc) + [pl.BlockSpec(c.shape, lambda i: (0, 0)) for c in consts] + r_in,
        out_specs=[pl.BlockSpec((CHUNK, SSD_D_INNER), lambda i: (i, 0)),
                   pl.BlockSpec((1, SSD_GROUPS * SSD_STATE, 256), lambda i: (i, 0, 0))] + r_out,
        scratch_shapes=[pltpu.VMEM((SSD_GROUPS * SSD_STATE, 256), F32)] + r_scratch,
        compiler_params=_cparams(dimension_semantics=("arbitrary",)),
    )(zxd, zxd, zxd, zxd, zxd, zxd, *consts, *(rider.operands if rider else ()))


def _ssd_bwd(zxd, states, d_y, consts, name, rider=None):
    lp = zxd.shape[0]
    nc = lp // CHUNK

    def body(*refs):
        own, ride = _rider_split(rider, refs, 14, 7, 3)
        (z_ref, xs_ref, bc_ref, hx_ref, hb_ref, dt_ref, st_ref, dy_ref, cw_ref, cb_ref, dtb_ref,
         alog_ref, dsk_ref, ng_ref, dz_ref, dcw_ref, dcb_ref, ddtb_ref, dalog_ref, ddsk_ref,
         dng_ref, d_state, d_hx, d_hb) = own
        i = pl.program_id(0)
        c = nc - 1 - i

        @pl.when(i == 0)
        def _():
            d_state[...] = jnp.zeros_like(d_state)
            d_hx[...] = jnp.zeros_like(d_hx)
            d_hb[...] = jnp.zeros_like(d_hb)
            for r in (dcw_ref, dcb_ref, ddtb_ref, dalog_ref, ddsk_ref, dng_ref):
                r[...] = jnp.zeros_like(r)
            if ride is not None:
                rider.start(*ride)

        live = (c > 0).astype(F32)
        fn = functools.partial(_ssd_chunk, _row_mask(c, CHUNK))
        prim = (z_ref[...], xs_ref[...], bc_ref[...], hx_ref[...] * live, hb_ref[...] * live,
                dt_ref[...], st_ref[0],
                *_ssd_param_vals(cw_ref, cb_ref, dtb_ref, alog_ref, dsk_ref, ng_ref))
        _, vjp = jax.vjp(fn, *prim)
        (d_z, d_xs, d_bc, g_hx, g_hb, d_dt, g_st, *d_par) = vjp((dy_ref[...], d_state[...]))
        zeros = jnp.zeros((CHUNK - 8, SSD_D_INNER), F32)
        d_xs = d_xs + jnp.concatenate([zeros, d_hx[...]], axis=0)
        d_bc = d_bc + jnp.concatenate([zeros, d_hb[...]], axis=0)
        dz_ref[:, 0:SSD_D_INNER] = d_z.astype(dz_ref.dtype)
        dz_ref[:, SSD_D_INNER:2 * SSD_D_INNER] = d_xs.astype(dz_ref.dtype)
        dz_ref[:, 2 * SSD_D_INNER:3 * SSD_D_INNER] = d_bc.astype(dz_ref.dtype)
        dz_ref[:, 3 * SSD_D_INNER:] = d_dt.astype(dz_ref.dtype)
        d_state[...] = g_st
        d_hx[...] = g_hx * live
        d_hb[...] = g_hb * live
        for k in range(SSD_CONV):
            dcw_ref[k:k + 1, 0:SSD_D_INNER] += d_par[k]
            dcw_ref[k:k + 1, SSD_D_INNER:2 * SSD_D_INNER] += d_par[SSD_CONV + k]
        dcb_ref[:, 0:SSD_D_INNER] += d_par[8]
        dcb_ref[:, SSD_D_INNER:2 * SSD_D_INNER] += d_par[9]
        ddtb_ref[...] += d_par[10]
        dalog_ref[...] += d_par[11]
        ddsk_ref[...] += d_par[12]
        dng_ref[...] += d_par[13]

        if ride is not None:
            @pl.when(i == nc - 1)
            def _():
                rider.finish(*ride)

    const_specs = [pl.BlockSpec(c.shape, lambda i: (0, 0)) for c in consts]
    r_in, r_out, r_shapes, r_scratch = _rider_args(rider)
    return pl.pallas_call(
        body, name=name,
        out_shape=[jax.ShapeDtypeStruct((lp, SSD_IN_PAD), BF16)]
        + [jax.ShapeDtypeStruct(c.shape, F32) for c in consts] + r_shapes,
        grid=(nc,),
        in_specs=_ssd_in_specs(True, nc)
        + [pl.BlockSpec((1, SSD_GROUPS * SSD_STATE, 256), lambda i: (nc - 1 - i, 0, 0)),
           pl.BlockSpec((CHUNK, SSD_D_INNER), lambda i: (nc - 1 - i, 0))] + const_specs + r_in,
        out_specs=[pl.BlockSpec((CHUNK, SSD_IN_PAD), lambda i: (nc - 1 - i, 0))] + const_specs + r_out,
        scratch_shapes=[pltpu.VMEM((SSD_GROUPS * SSD_STATE, 256), F32),
                        pltpu.VMEM((8, SSD_D_INNER), F32), pltpu.VMEM((8, SSD_D_INNER), F32)] + r_scratch,
        compiler_params=_cparams(dimension_semantics=("arbitrary",)),
    )(zxd, zxd, zxd, zxd, zxd, zxd, states, d_y, *consts, *(rider.operands if rider else ()))


@jax.custom_vjp
def _rot_half(x):
    lane = lax.broadcasted_iota(jnp.int32, x.shape, 1)
    lo = (lane >= MLA_NOPE) & (lane < MLA_NOPE + MLA_ROPE // 2)
    hi = (lane >= MLA_NOPE + MLA_ROPE // 2) & (lane < MLA_QK)
    down = pltpu.roll(x, HEAD_SLOT - MLA_ROPE // 2, 1)
    up = pltpu.roll(x, MLA_ROPE // 2, 1)
    return jnp.where(lo, -down, jnp.where(hi, up, 0.0))


def _rot_half_fwd(x):
    return _rot_half(x), None


def _rot_half_bwd(_, ct):
    return (-_rot_half(ct),)


_rot_half.defvjp(_rot_half_fwd, _rot_half_bwd)


def _head_norm_rope(t, gain, cos, sin):
    n = t * lax.rsqrt(jnp.sum(t * t, axis=-1, keepdims=True) * (1.0 / MLA_QK) + EPS) * gain
    return n * cos + _rot_half(n) * sin


def _qk_prep(q_raw, kn_raw, kpe, cos, sin, qg, kg):
    qs, ks = [], []
    for h in range(MLA_HEADS):
        sl = slice(h * HEAD_SLOT, (h + 1) * HEAD_SLOT)
        qs.append(_head_norm_rope(q_raw[:, sl], qg, cos, sin))
        ks.append(_head_norm_rope(kn_raw[:, sl] + kpe, kg, cos, sin))
    return jnp.concatenate(qs, axis=1), jnp.concatenate(ks, axis=1)


def _lat_norm(kv_lat, q_lat, kvg, qg):
    return _rms(kv_lat, kvg), _rms(q_lat, qg)


_NEG = -1e30
_SCALE = MLA_QK ** -0.5


STRIP = 128
_EXP2_SCALE = _SCALE * math.log2(math.e)


def _strip_mask(kind, blk, c, t):
    if kind is None:
        return None
    kpos = blk * t + c * STRIP + lax.broadcasted_iota(jnp.int32, (1, STRIP), 1)
    if kind == 'keys':
        return kpos >= NPAD
    qpos = blk * t + lax.broadcasted_iota(jnp.int32, (t, 1), 0)
    return (kpos <= qpos) & ((kpos >= NPAD) | (kpos == qpos))


def _attn_fwd(q, k, v, name, rider=None):
    lp = q.shape[0]
    t = tk = _pick(lp, (384, 256, 128))
    nb = lp // t
    hp = HEADS_PER_STEP
    wide = hp * HEAD_SLOT
    heads = [slice(a * HEAD_SLOT, (a + 1) * HEAD_SLOT) for a in range(hp)]

    def body(*refs):
        (q_ref, k_ref, v_ref, o_ref, lse_ref), ride = _rider_split(rider, refs, 3, 2, 0)
        qi = pl.program_id(1)
        if ride is not None:
            @pl.when((pl.program_id(0) == 0) & (qi == 0))
            def _():
                rider.start(*ride)

        def scores(ki):
            rows = pl.ds(pl.multiple_of(ki * tk, tk), tk)
            return tuple(lax.dot_general(q_ref[:, heads[a]], k_ref[rows, heads[a]], _NT,
                                         preferred_element_type=F32) for a in range(hp))

        def update(a, ki, carry, s, mask):
            rows = pl.ds(pl.multiple_of(ki * tk, tk), tk)
            m, acc = carry
            s = jnp.where(mask, s, _NEG)
            m_new = jnp.maximum(m, jnp.max(s, axis=-1, keepdims=True))
            alpha = jnp.exp2((m - m_new) * _EXP2_SCALE)
            p = jnp.concatenate(
                [jnp.exp2((s[:, c:c + STRIP] - m_new) * _EXP2_SCALE).astype(BF16) for c in range(0, tk, STRIP)],
                axis=1)
            acc = alpha * acc + lax.dot_general(p, v_ref[rows, heads[a]], _NN, preferred_element_type=F32)
            return m_new, acc

        init = (jnp.full((t, 1), _NEG, F32), jnp.zeros((t, HEAD_SLOT), F32))
        ones_lane = lax.broadcasted_iota(jnp.int32, (1, HEAD_SLOT), 1) == MLA_V
        key_pos = lax.broadcasted_iota(jnp.int32, (1, tk), 1)
        n_full = (qi * t) // tk

        def before(ki, state):
            carry, s = state
            s_next = scores(ki + 1)
            key_ok = ki * tk + key_pos >= NPAD
            return tuple(update(a, ki, carry[a], s[a], key_ok) for a in range(hp)), s_next

        carry, s = lax.fori_loop(0, n_full, before, ((init,) * hp, scores(0)))
        qpos = qi * t + lax.broadcasted_iota(jnp.int32, (t, tk), 0)
        kpos = n_full * tk + lax.broadcasted_iota(jnp.int32, (t, tk), 1)
        diag = (kpos <= qpos) & ((kpos >= NPAD) | (kpos == qpos))
        carry = tuple(update(a, n_full, carry[a], s[a], diag) for a in range(hp))
        for a in range(hp):
            m, acc = carry[a]
            l = jnp.sum(jnp.where(ones_lane, acc, 0.0), axis=-1, keepdims=True)
            o_ref[:, heads[a]] = jnp.where(ones_lane, 0.0, acc / l * _row_mask(qi, t))
            lse_ref[a] = m * _SCALE + jnp.log(l)

        if ride is not None:
            @pl.when((pl.program_id(0) == MLA_HEADS // hp - 1) & (qi == nb - 1))
            def _():
                rider.finish(*ride)

    qspec = pl.BlockSpec((t, wide), lambda g, i: (i, g))
    kspec = pl.BlockSpec((lp, wide), lambda g, i: (0, g))
    r_in, r_out, r_shapes, r_scratch = _rider_args(rider)
    return pl.pallas_call(
        body, name=name,
        out_shape=[jax.ShapeDtypeStruct((lp, MLA_WIDE), F32),
                   jax.ShapeDtypeStruct((MLA_HEADS, lp, 1), F32)] + r_shapes,
        grid=(MLA_HEADS // hp, nb),
        in_specs=[qspec, kspec, kspec] + r_in,
        out_specs=[qspec, pl.BlockSpec((hp, t, 1), lambda g, i: (g, i, 0))] + r_out,
        scratch_shapes=r_scratch,
        compiler_params=_cparams(dimension_semantics=("arbitrary", "arbitrary")),
    )(q, k, v, *(rider.operands if rider else ()))


def _attn_bwd(q, k, v, do, lse, delta, name, rider=None):
    lp = q.shape[0]
    t = _pick(lp, (384, 256, 128))
    nb = lp // t
    ns = t // STRIP
    hp = HEADS_PER_STEP
    wide = hp * HEAD_SLOT
    heads = [slice(a * HEAD_SLOT, (a + 1) * HEAD_SLOT) for a in range(hp)]
    log2e = math.log2(math.e)

    def body(*refs):
        own, ride = _rider_split(rider, refs, 6, 3, 4)
        (q_ref, k_ref, v_ref, do_ref, lse_ref, delta_ref, dq_ref, dk_ref, dv_ref,
         s_scr, dp_scr, p_scr, ds_scr) = own
        kj = pl.program_id(1)
        if ride is not None:
            @pl.when((pl.program_id(0) == 0) & (kj == 0))
            def _():
                rider.start(*ride)

        @pl.when(kj == 0)
        def _():
            dq_ref[...] = jnp.zeros_like(dq_ref)

        dk_ref[...] = jnp.zeros_like(dk_ref)
        dv_ref[...] = jnp.zeros_like(dv_ref)

        def tile(qi, kind):
            rows = pl.ds(pl.multiple_of(qi * t, t), t)
            for a in range(hp):
                qb, dob = q_ref[rows, heads[a]], do_ref[rows, heads[a]]
                kb, vb = k_ref[:, heads[a]], v_ref[:, heads[a]]
                s_scr[a] = lax.dot_general(qb, kb, _NT, preferred_element_type=F32)
                dp_scr[a] = lax.dot_general(dob, vb, _NT, preferred_element_type=F32)
                lse2 = lse_ref[a, rows, :] * log2e
                delta = delta_ref[a, rows, :]
                for c in range(ns):
                    cs = slice(c * STRIP, (c + 1) * STRIP)
                    pc = jnp.exp2(s_scr[a, :, cs] * _EXP2_SCALE - lse2)
                    pc = jnp.where(_strip_mask(kind, kj, c, t), pc, 0.0)
                    p_scr[a, :, cs] = pc.astype(BF16)
                    ds_scr[a, :, cs] = (pc * (dp_scr[a, :, cs] - delta)).astype(BF16)
                dq_ref[rows, heads[a]] += lax.dot_general(ds_scr[a], kb, _NN,
                                                          preferred_element_type=F32) * _SCALE
                dv_ref[:, heads[a]] += lax.dot_general(p_scr[a], dob, _TN, preferred_element_type=F32)
                dk_ref[:, heads[a]] += lax.dot_general(ds_scr[a], qb, _TN, preferred_element_type=F32)

        tile(kj, 'diag')

        def below(qi, carry):
            tile(qi, 'keys')
            return carry

        lax.fori_loop(kj + 1, nb, below, 0)
        dk_ref[...] = dk_ref[...] * _SCALE

        if ride is not None:
            @pl.when((pl.program_id(0) == MLA_HEADS // hp - 1) & (kj == nb - 1))
            def _():
                rider.finish(*ride)

    whole = pl.BlockSpec((lp, wide), lambda g, j: (0, g))
    kspec = pl.BlockSpec((t, wide), lambda g, j: (j, g))
    stat = pl.BlockSpec((hp, lp, 1), lambda g, j: (g, 0, 0))
    r_in, r_out, r_shapes, r_scratch = _rider_args(rider)
    return pl.pallas_call(
        body, name=name,
        out_shape=[jax.ShapeDtypeStruct((lp, MLA_WIDE), F32)] * 3 + r_shapes,
        grid=(MLA_HEADS // hp, nb),
        in_specs=[whole, kspec, kspec, whole, stat, stat] + r_in,
        out_specs=[whole, kspec, kspec] + r_out,
        scratch_shapes=[pltpu.VMEM((hp, t, t), F32), pltpu.VMEM((hp, t, t), F32),
                        pltpu.VMEM((hp, t, t), BF16), pltpu.VMEM((hp, t, t), BF16)] + r_scratch,
        compiler_params=_cparams(dimension_semantics=("arbitrary", "arbitrary")),
    )(q, k, v, do, lse, delta, *(rider.operands if rider else ()))


def _rope_tables(lp):
    inv = 1.0 / (ROPE_THETA ** (jnp.arange(0, MLA_ROPE, 2, dtype=F32) / MLA_ROPE))
    pos = jnp.maximum(jnp.arange(lp, dtype=jnp.int32) - NPAD, 0).astype(F32)
    ang = pos[:, None] * inv[None, :]
    cos, sin = jnp.cos(ang), jnp.sin(ang)
    z32 = jnp.zeros((lp, HEAD_SLOT - MLA_QK), F32)
    cos_t = jnp.concatenate([jnp.ones((lp, MLA_NOPE), F32), cos, cos, z32], axis=1)
    sin_t = jnp.concatenate([jnp.zeros((lp, MLA_NOPE), F32), sin, sin, z32], axis=1)
    return cos_t, sin_t


def _loss_head(h, target, name):
    lp = h.shape[0]

    def body(h_ref, t_ref, d_ref, loss_ref):
        i = pl.program_id(0)

        @pl.when(i == 0)
        def _():
            d_ref[...] = jnp.zeros_like(d_ref)
            loss_ref[...] = jnp.zeros_like(loss_ref)

        @pl.when(i > 0)
        def _():
            err = h_ref[...] - t_ref[...]
            d_ref[...] = err * (1.0 / D_MODEL)
            loss_ref[...] += jnp.sum(err * err, axis=0, keepdims=True) * (0.5 / D_MODEL)

    return pl.pallas_call(
        body, name=name,
        out_shape=[jax.ShapeDtypeStruct((lp, D_MODEL), F32), jax.ShapeDtypeStruct((1, D_MODEL), F32)],
        grid=(lp // CHUNK,),
        in_specs=[pl.BlockSpec((CHUNK, D_MODEL), lambda i: (i, 0)),
                  pl.BlockSpec((CHUNK, D_MODEL), lambda i: (jnp.maximum(i - 1, 0), 0))],
        out_specs=[pl.BlockSpec((CHUNK, D_MODEL), lambda i: (i, 0)),
                   pl.BlockSpec((1, D_MODEL), lambda i: (0, 0))],
        compiler_params=_cparams(dimension_semantics=("arbitrary",)),
    )(h, target)


def _pad_cols(w, n):
    return jnp.pad(w, [(0, 0)] * (w.ndim - 1) + [(0, n - w.shape[-1])])


def _layer_slab(name, i):
    return i if name.startswith('mlp_') else i // 2


def _prep_matrix(key, raw):
    if key == 'ssd_in':
        return _pad_cols(raw('ssd_w_in'), SSD_IN_PAD).astype(BF16)
    if key == 'mla_in':
        wi = raw('mla_w_in')
        kpe = jnp.pad(wi[:, MLA_Q_RANK + MLA_KV_RANK:], ((0, 0), (MLA_NOPE, HEAD_SLOT - MLA_QK)))
        return jnp.concatenate(
            [wi[:, MLA_Q_RANK:MLA_Q_RANK + MLA_KV_RANK], kpe, wi[:, :MLA_Q_RANK]], axis=1).astype(BF16)
    if key == 'mla_qb':
        qb = raw('mla_w_q_b').reshape(MLA_Q_RANK, MLA_HEADS, MLA_QK)
        return _pad_cols(qb, HEAD_SLOT).reshape(MLA_Q_RANK, MLA_WIDE).astype(BF16)
    if key == 'mla_kvb':
        kvb = raw('mla_w_kv_b').reshape(MLA_KV_RANK, MLA_HEADS, MLA_NOPE + MLA_V)
        kn = _pad_cols(kvb[:, :, :MLA_NOPE], HEAD_SLOT).reshape(MLA_KV_RANK, MLA_WIDE)
        vv = _pad_cols(kvb[:, :, MLA_NOPE:], HEAD_SLOT).reshape(MLA_KV_RANK, MLA_WIDE)
        return jnp.concatenate([kn, vv], axis=1).astype(BF16)
    if key == 'mla_out':
        wo = raw('mla_w_out').reshape(MLA_HEADS, MLA_V, D_MODEL)
        return jnp.pad(wo, ((0, 0), (0, HEAD_SLOT - MLA_V), (0, 0))).reshape(MLA_WIDE, D_MODEL).astype(BF16)
    return raw({'ssd_out': 'ssd_w_out', 'up': 'mlp_w_up', 'down': 'mlp_w_down'}[key]).astype(BF16)


class _Matrices:
    def __init__(self):
        self.p = {k: _Slabs(k, self) for k in ('ssd_in', 'ssd_out', 'mla_in', 'mla_qb', 'mla_kvb',
                                               'mla_out', 'up', 'down')}
        self.made = {}

    def matrix(self, key, slab):
        if (key, slab) not in self.made:
            self.made[(key, slab)] = _prep_matrix(key, lambda n: self.raw(n, slab))
        return self.made[(key, slab)]


class _Slabs:
    def __init__(self, key, owner):
        self.key, self.owner = key, owner

    def __getitem__(self, slab):
        return self.owner.matrix(self.key, slab)


class _ReadyWeights(_Matrices):
    def __init__(self, w):
        super().__init__()
        self.w = w

    def raw(self, name, slab):
        return self.w[name][slab]

    def start(self):
        pass

    def rider(self, host):
        return None

    def deliver(self, host, outs):
        assert not outs


class _KeepGrads:
    def __init__(self):
        self.rounds = {}

    def begin(self, r, grads):
        self.rounds[r] = grads
        return None

    def middle(self, r, recv):
        return None

    def finish(self, r, outs):
        assert not outs

    def result(self):
        names = {n for g in self.rounds.values() for n in g}
        return {n: jnp.concatenate([self.rounds[r][n] for r in sorted(self.rounds, reverse=True)
                                    if n in self.rounds[r]], axis=0) for n in names}


def _pad128(v):
    return _pad_cols(v.reshape(1, -1), 128)


def _sqrelu(u):
    r = jnp.maximum(u, 0.0)
    return r * r


def _local_step(x, target, w, big=None, red=None):
    seq = x.shape[0]
    lp = NPAD + N_META + seq
    big = _ReadyWeights(w) if big is None else big
    p = big.p
    h = jnp.concatenate([jnp.zeros((NPAD, D_MODEL), F32), w['meta_tokens'], x], axis=0)
    cos_t, sin_t = _rope_tables(lp)
    rt = _pick(lp, (384, 256, 128))
    saved = []
    big.start()
    for i in range(4):
        j = i // 2
        s = {'h0': h}
        g_mix = w['ln_mix'][i].reshape(1, -1)
        g_mlp = w['ln_mlp'][i].reshape(1, -1)
        if i == 0:
            hn = _rms_fwd(h, g_mix, f"rms_mix_f{i}")
        s['hn'] = hn
        if i % 2 == 0:
            rid = big.rider(f"ssd_in_f{i}")
            zxd = _mm(hn, p['ssd_in'][j], 'nn', name=f"ssd_in_f{i}", rider=rid)
            if rid is not None:
                zxd, *got = zxd
                big.deliver(f"ssd_in_f{i}", got)
            consts = _ssd_consts(w['ssd_conv_w'][j], w['ssd_conv_b'][j].reshape(1, -1),
                                 _pad128(w['ssd_dt_bias'][j]), _pad128(w['ssd_a_log'][j]),
                                 _pad128(w['ssd_d'][j]), w['ssd_norm'][j].reshape(1, -1))
            yg, states, *got = _ssd_fwd(zxd, consts, f"ssd_core_f{i}", rider=big.rider(f"ssd_core_f{i}"))
            big.deliver(f"ssd_core_f{i}", got)
            s.update(zxd=zxd, consts=consts, yg=yg, states=states)
            h, hn2 = _mm(yg, p['ssd_out'][j], 'nn', name=f"ssd_out_f{i}", epi=lambda r, hv: hv + r,
                         extras=(h,), norm_gain=g_mlp)
        else:
            lat = _mm(hn, p['mla_in'][j], 'nn', name=f"mla_in_f{i}")
            kvg = w['mla_kv_a_norm'][j].reshape(1, -1)
            qag = w['mla_q_a_norm'][j].reshape(1, -1)
            kvn, qn = _row_call(lambda _, a, b, c, d: _lat_norm(a, b, c, d),
                                [(lat, MLA_KV_RANK, 0), (lat, MLA_Q_RANK, 1)], [kvg, qag],
                                [(MLA_KV_RANK, BF16), (MLA_Q_RANK, BF16)], n_rows=lp, tile=rt,
                                name=f"mla_latnorm_f{i}")
            q_raw = _mm(qn, p['mla_qb'][j], 'nn', name=f"mla_qb_f{i}")
            kv_raw = _mm(kvn, p['mla_kvb'][j], 'nn', name=f"mla_kvb_f{i}")
            qg = _pad_cols(w['mla_q_norm'][j].reshape(1, -1), HEAD_SLOT)
            kg = _pad_cols(w['mla_k_norm'][j].reshape(1, -1), HEAD_SLOT)

            def prep_fwd(_, qr, kn, kpe, vv, cs, sn, qgv, kgv):
                qq, kk = _qk_prep(qr, kn, kpe, cs, sn, qgv, kgv)
                ones = lax.broadcasted_iota(jnp.int32, vv.shape, 1) % HEAD_SLOT == MLA_V
                return qq, kk, jnp.where(ones, 1.0, vv)

            q, k, v = _row_call(prep_fwd,
                                [(q_raw, MLA_WIDE, 0), (kv_raw, MLA_WIDE, 0), (lat, HEAD_SLOT, 2),
                                 (kv_raw, MLA_WIDE, 1), (cos_t, HEAD_SLOT, 0), (sin_t, HEAD_SLOT, 0)],
                                [qg, kg], [(MLA_WIDE, BF16)] * 3, n_rows=lp, tile=rt,
                                name=f"mla_qkprep_f{i}")
            o, lse, *got = _attn_fwd(q, k, v, f"mla_attn_f{i}", rider=big.rider(f"mla_attn_f{i}"))
            big.deliver(f"mla_attn_f{i}", got)
            s.update(lat=lat, kvg=kvg, qag=qag, kvn=kvn, qn=qn, q_raw=q_raw, kv_raw=kv_raw, qg=qg, kg=kg,
                     q=q, k=k, v=v, o=o, lse=lse)
            h, hn2 = _mm(o, p['mla_out'][j], 'nn', name=f"mla_out_f{i}", epi=lambda r, hv: hv + r,
                         extras=(h,), norm_gain=g_mlp)
        s['h1'] = h
        u = _mm(hn2, p['up'][i], 'nn', name=f"mlp_up_f{i}", out_dtype=BF16)
        if i < 3:
            h, hn = _mm(u, p['down'][i], 'nn', name=f"mlp_down_f{i}", a_fn=_sqrelu, epi=lambda r, hv: hv + r,
                        extras=(h,), norm_gain=w['ln_mix'][i + 1].reshape(1, -1))
        else:
            h = _mm(u, p['down'][i], 'nn', name=f"mlp_down_f{i}", a_fn=_sqrelu,
                    epi=lambda r, hv: hv + r, extras=(h,))
        s.update(hn2=hn2, u=u, g_mix=g_mix, g_mlp=g_mlp)
        saved.append(s)

    dh, loss_row = _loss_head(h, target, "loss_head")

    large = {n for n, _ in BIG}
    g = {k_: [None] * (4 if k_ in ('ln_mix', 'ln_mlp') else 2)
         for k_ in ALL_NAMES if k_ != 'meta_tokens' and k_ not in large}
    red = _KeepGrads() if red is None else red
    rounds, pending = {}, None

    def round_of(nm, i):
        return next(r for r, spec in enumerate(REDUCE_ROUNDS)
                    for n, l0, l1 in spec if n == nm and l0 <= _layer_slab(nm, i) < l1)

    def slabs_in(nm, r):
        return next((l0, l1) for n, l0, l1 in REDUCE_ROUNDS[r] if n == nm)

    swapping = None

    def dw_into(nm, i, a, b, **kw):
        nonlocal swapping, pending
        r = round_of(nm, i)
        (l0, l1), cur = slabs_in(nm, r), rounds.setdefault(r, {})
        stack = (l1 - l0, _layer_slab(nm, i) - l0, cur.get(nm))
        if swapping is None:
            cur[nm] = _mm(a, b, 'tn', stack=stack, **kw)
        else:
            (r0, rider), swapping = swapping, None
            cur[nm], *recv = _mm(a, b, 'tn', stack=stack, rider=rider, **kw)
            pending = (r0, red.middle(r0, recv))

    def put(nm, i, arr):
        rounds.setdefault(round_of(nm, i), {})[nm] = arr[None]

    def hand_over(r):
        nonlocal pending, swapping
        swap = red.begin(r, rounds.pop(r))
        if swap is None:
            pending = (r, None)
        elif r == 0:
            swapping = (r, swap)
        else:
            pending = (r, red.middle(r, _run_rider(swap, f"rs_swap{r}")))

    def host(fn, *args):
        nonlocal pending
        if pending is None or pending[1] is None:
            return fn(*args)
        (r, rider), pending = pending, None
        outs = fn(*args, rider=rider)
        own = len(outs) - len(rider.out_shapes)
        red.finish(r, outs[own:])
        return outs[:own]

    for i in reversed(range(4)):
        j = i // 2
        s = saved[i]
        dw_into('mlp_w_down', i, s['u'], dh, name=f"mlp_down_dw{i}", a_fn=_sqrelu)
        du = _mm(dh, p['down'][i], 'nt', name=f"mlp_down_dx{i}", out_dtype=BF16,
                 epi=lambda r, uv: r * (2.0 * jnp.maximum(uv, 0.0)), extras=(s['u'],))
        dw_into('mlp_w_up', i, s['hn2'], du, name=f"mlp_up_dw{i}")
        dh, dg = _mm_rms_bwd(du, p['up'][i], s['h1'], dh, s['g_mlp'], f"mlp_up_dx{i}")
        g['ln_mlp'][i] = dg[0]
        if i % 2 == 0:
            dw_into('ssd_w_out', i, s['yg'], dh, name=f"ssd_out_dw{i}")
            d_yg = _mm(dh, p['ssd_out'][j], 'nt', name=f"ssd_out_dx{i}")
            if i == 0:
                hand_over(1)
            d_zxd, dcw, dcb, ddtb, dalog, ddsk, dng = host(_ssd_bwd, s['zxd'], s['states'], d_yg, s['consts'],
                                                           f"ssd_core_b{i}")
            g['ssd_conv_w'][j], g['ssd_conv_b'][j], g['ssd_norm'][j] = dcw, dcb[0], dng[0]
            g['ssd_dt_bias'][j], g['ssd_a_log'][j], g['ssd_d'][j] = (
                ddtb[0, :SSD_HEADS], dalog[0, :SSD_HEADS], ddsk[0, :SSD_HEADS])
            dw_into('ssd_w_in', i, s['hn'], d_zxd, name=f"ssd_in_dw{i}")
            dh, dg = _mm_rms_bwd(d_zxd, p['ssd_in'][j], s['h0'], dh, s['g_mix'], f"ssd_in_dx{i}")
        else:
            wo = _mm(s['o'], dh, 'tn', name=f"mla_out_dw{i}")
            put('mla_w_out', i, wo.reshape(MLA_HEADS, HEAD_SLOT, D_MODEL)[:, :MLA_V].reshape(-1, D_MODEL))
            dob, delta = _mm_attn_do(dh, p['mla_out'][j], s['o'], f"mla_out_dx{i}")
            dq, dk, dv = host(_attn_bwd, s['q'], s['k'], s['v'], dob, s['lse'], delta, f"mla_attn_b{i}")

            def prep_bwd(_, qr, kn, kpe, cs, sn, dqv, dkv, dvv, qgv, kgv):
                _, vjp = jax.vjp(lambda a, b, c, d, e: _qk_prep(a, b, c, cs, sn, d, e), qr, kn, kpe, qgv, kgv)
                d_qr, d_kn, d_kpe, d_qg, d_kg = vjp((dqv, dkv))
                return d_qr, jnp.concatenate([d_kn, dvv], axis=1), d_kpe, d_qg, d_kg

            d_qraw, d_kvraw, d_kpe, d_qg, d_kg = _row_call(
                prep_bwd,
                [(s['q_raw'], MLA_WIDE, 0), (s['kv_raw'], MLA_WIDE, 0), (s['lat'], HEAD_SLOT, 2),
                 (cos_t, HEAD_SLOT, 0), (sin_t, HEAD_SLOT, 0), (dq, MLA_WIDE, 0), (dk, MLA_WIDE, 0),
                 (dv, MLA_WIDE, 0)],
                [s['qg'], s['kg']], [(MLA_WIDE, BF16), (2 * MLA_WIDE, BF16), (HEAD_SLOT, F32)],
                [(1, HEAD_SLOT), (1, HEAD_SLOT)], n_rows=lp, tile=_pick(lp, (128,)), name=f"mla_qkprep_b{i}")
            g['mla_q_norm'][j], g['mla_k_norm'][j] = d_qg[0, :MLA_QK], d_kg[0, :MLA_QK]
            wqb = _mm(s['qn'], d_qraw, 'tn', name=f"mla_qb_dw{i}")
            put('mla_w_q_b', i, wqb.reshape(MLA_Q_RANK, MLA_HEADS, HEAD_SLOT)[:, :, :MLA_QK].reshape(MLA_Q_RANK, -1))
            d_qn = _mm(d_qraw, p['mla_qb'][j], 'nt', name=f"mla_qb_dx{i}")
            wkvb = _mm(s['kvn'], d_kvraw, 'tn', name=f"mla_kvb_dw{i}").reshape(MLA_KV_RANK, 2, MLA_HEADS, HEAD_SLOT)
            put('mla_w_kv_b', i, jnp.concatenate([wkvb[:, 0, :, :MLA_NOPE], wkvb[:, 1, :, :MLA_V]],
                                                 axis=-1).reshape(MLA_KV_RANK, -1))
            d_kvn = _mm(d_kvraw, p['mla_kvb'][j], 'nt', name=f"mla_kvb_dx{i}")

            def lat_bwd(_, kvl, ql, dkvn, dqn, dkpe, kvgv, qagv):
                _, vjp = jax.vjp(_lat_norm, kvl, ql, kvgv, qagv)
                d_kvl, d_ql, d_kvg, d_qag = vjp((dkvn, dqn))
                return jnp.concatenate([d_kvl, dkpe, d_ql], axis=1), d_kvg, d_qag

            d_lat, d_kvg, d_qag = _row_call(
                lat_bwd, [(s['lat'], MLA_KV_RANK, 0), (s['lat'], MLA_Q_RANK, 1), (d_kvn, MLA_KV_RANK, 0),
                          (d_qn, MLA_Q_RANK, 0), (d_kpe, HEAD_SLOT, 0)],
                [s['kvg'], s['qag']], [(LAT_PAD, BF16)], [(1, MLA_KV_RANK), (1, MLA_Q_RANK)],
                n_rows=lp, tile=rt, name=f"mla_latnorm_b{i}")
            g['mla_kv_a_norm'][j], g['mla_q_a_norm'][j] = d_kvg[0], d_qag[0]
            win = _mm(s['hn'], d_lat, 'tn', name=f"mla_in_dw{i}")
            put('mla_w_in', i, jnp.concatenate(
                [win[:, MLA_KV_RANK + HEAD_SLOT:], win[:, :MLA_KV_RANK],
                 win[:, MLA_KV_RANK + MLA_NOPE:MLA_KV_RANK + MLA_QK]], axis=1))
            dh, dg = _mm_rms_bwd(d_lat, p['mla_in'][j], s['h0'], dh, s['g_mix'], f"mla_in_dx{i}")
        g['ln_mix'][i] = dg[0]
        if i == 2:
            hand_over(0)
    hand_over(2)

    if pending[1] is not None:
        red.finish(pending[0], _run_rider(pending[1], "rs_exchange_last"))
    grads = {k_: jnp.stack(v_) for k_, v_ in g.items()}
    grads['meta_tokens'] = dh[NPAD:NPAD + N_META]
    return loss_row, dh[NPAD + N_META:], grads, red


def _all_gather8(shard, name):
    m_per, n = shard.shape

    def body(x_ref, out_ref, send_sems, recv_sems, local_sem):
        x, y, c = lax.axis_index("x"), lax.axis_index("y"), lax.axis_index("c")
        me, sibling = (x, y, c), (x, y, 1 - c)
        chips = [(1 - x, y), (x, 1 - y), (1 - x, 1 - y)]

        def rows(px, py, pc):
            return out_ref.at[pl.ds((4 * px + 2 * py + pc) * m_per, m_per), :]

        def copy(k, block, to, src=None):
            return pltpu.make_async_remote_copy(
                src_ref=rows(*block) if src is None else src, dst_ref=rows(*block),
                send_sem=send_sems.at[k], recv_sem=recv_sems.at[k], device_id=to, device_id_type=MESH)

        mine = pltpu.make_async_copy(x_ref, rows(*me), local_sem)
        mine.start()
        first = [copy(0, me, sibling, src=x_ref)]
        first += [copy(1 + j, me, (*chip, c), src=x_ref) for j, chip in enumerate(chips)]
        for cp in first:
            cp.start()
        passed = [copy(4 + j, (*chip, c), sibling) for j, chip in enumerate(chips)]
        for j, chip in enumerate(chips):
            copy(1 + j, (*chip, c), me).wait_recv()
            passed[j].start()
        copy(0, sibling, me).wait_recv()
        for j, chip in enumerate(chips):
            copy(4 + j, (*chip, 1 - c), me).wait_recv()
        for cp in first + passed:
            cp.wait_send()
        mine.wait()

    return pl.pallas_call(
        body, name=name,
        out_shape=jax.ShapeDtypeStruct((8 * m_per, n), shard.dtype),
        in_specs=[pl.BlockSpec(memory_space=pl.ANY)],
        out_specs=pl.BlockSpec(memory_space=pl.ANY),
        scratch_shapes=[pltpu.SemaphoreType.DMA((7,)), pltpu.SemaphoreType.DMA((7,)), pltpu.SemaphoreType.DMA],
    )(shard)


def _mesh_pos():
    return lax.axis_index("x"), lax.axis_index("y"), lax.axis_index("c")


def _half_rows(pc, h):
    return pl.ds(pl.multiple_of(pc * h, 16), h)


def _whole_view(ref, kind, shard_shape, k, pc):
    _, r, c = shard_shape
    rows = _half_rows(pc, r // 2)
    if kind == 'row':
        return ref.at[:, k, rows, :]
    if kind == 'col':
        return ref.at[:, rows, pl.ds(pl.multiple_of(k * c, 128), c)]
    return ref.at[k, :, rows, :]


def _whole_shape(kind, shard_shape, rows=None):
    l, r, c = shard_shape
    r = r if rows is None else rows
    return {'row': (l, 4, r, c), 'col': (l, r, 4 * c), 'colx': (4, l, r, c)}[kind]


def _gather_rider(shards, kinds):
    n = len(shards)
    shapes = [s.shape for s in shards]

    def plan(ins, outs, sems):
        send_sems, recv_sems, local_sems = sems
        x, y, c = _mesh_pos()
        me, sibling = (x, y, c), (x, y, 1 - c)
        chips = [(1 - x, y), (x, 1 - y), (1 - x, 1 - y)]

        def place(a, px, py, pc):
            return _whole_view(outs[a], kinds[a], shapes[a], 2 * px + py, pc)

        def own(a):
            return ins[a].at[:, _half_rows(c, shapes[a][1] // 2), :]

        def copy(a, k, block, to, src=None):
            return pltpu.make_async_remote_copy(
                src_ref=place(a, *block) if src is None else src, dst_ref=place(a, *block),
                send_sem=send_sems.at[7 * a + k], recv_sem=recv_sems.at[7 * a + k],
                device_id=to, device_id_type=MESH)

        mine = [pltpu.make_async_copy(own(a), place(a, *me), local_sems.at[a]) for a in range(n)]
        first = [copy(a, 1 + j, me, (*chip, c), src=own(a)) for j, chip in enumerate(chips) for a in range(n)]
        first += [copy(a, 0, me, sibling, src=own(a)) for a in range(n)]
        return copy, mine, first, chips, me, sibling, c

    def start(ins, outs, sems):
        _, mine, first, *_ = plan(ins, outs, sems)
        for cp in first + mine:
            cp.start()

    def finish(ins, outs, sems):
        copy, mine, first, chips, me, sibling, c = plan(ins, outs, sems)
        passed = []
        for j, chip in enumerate(chips):
            for a in range(n):
                copy(a, 1 + j, (*chip, c), me).wait_recv()
                passed.append(copy(a, 4 + j, (*chip, c), sibling))
                passed[-1].start()
        for a in range(n):
            copy(a, 0, sibling, me).wait_recv()
        for j, chip in enumerate(chips):
            for a in range(n):
                copy(a, 4 + j, (*chip, 1 - c), me).wait_recv()
        for cp in first + passed:
            cp.wait_send()
        for cp in mine:
            cp.wait()

    return _Rider(
        shards, [jax.ShapeDtypeStruct(_whole_shape(k, s.shape), s.dtype) for k, s in zip(kinds, shards)],
        [pltpu.SemaphoreType.DMA((7 * n,)), pltpu.SemaphoreType.DMA((7 * n,)), pltpu.SemaphoreType.DMA((n,))],
        start, finish)


def _run_rider(rider, name):
    ni, no = len(rider.operands), len(rider.out_shapes)

    def body(*refs):
        ride = (refs[:ni], refs[ni:ni + no], refs[ni + no:])
        rider.start(*ride)
        rider.finish(*ride)

    return pl.pallas_call(
        body, name=name, out_shape=rider.out_shapes,
        in_specs=[pl.BlockSpec(memory_space=pl.ANY)] * ni,
        out_specs=[pl.BlockSpec(memory_space=pl.ANY)] * no,
        scratch_shapes=rider.scratch,
    )(*rider.operands)


def _swap_rider(wholes, kinds, shapes):
    n = len(wholes)

    def plan(ins, outs, sems):
        send_sems, recv_sems = sems
        x, y, c = _mesh_pos()
        cps = []
        for a in range(n):
            rows = _half_rows(1 - c, shapes[a][1] // 2)
            src = ins[a].at[:, rows, :] if kinds[a] == 'col' else ins[a].at[:, :, rows, :]
            cps.append(pltpu.make_async_remote_copy(
                src_ref=src, dst_ref=outs[a], send_sem=send_sems.at[a], recv_sem=recv_sems.at[a],
                device_id=(x, y, 1 - c), device_id_type=MESH))
        return cps

    def start(ins, outs, sems):
        for cp in plan(ins, outs, sems):
            cp.start()

    def finish(ins, outs, sems):
        for cp in plan(ins, outs, sems):
            cp.wait()

    return _Rider(
        wholes, [jax.ShapeDtypeStruct(_whole_shape(k, s, s[1] // 2), w.dtype)
                 for k, s, w in zip(kinds, shapes, wholes)],
        [pltpu.SemaphoreType.DMA((n,)), pltpu.SemaphoreType.DMA((n,))], start, finish)


def _exchange_rider(parts, kinds, shapes):
    n = len(parts)

    def plan(ins, outs, sems):
        send_sems, recv_sems, local_sems = sems
        x, y, c = _mesh_pos()
        kme = 2 * x + y
        chips = [(1 - x, y), (x, 1 - y), (1 - x, 1 - y)]

        def slab(a, k):
            if kinds[a] == 'row':
                return ins[a].at[:, k]
            if kinds[a] == 'col':
                cw = shapes[a][2]
                return ins[a].at[:, :, pl.ds(pl.multiple_of(k * cw, 128), cw)]
            return ins[a].at[k]

        cps = [pltpu.make_async_remote_copy(
            src_ref=slab(a, 2 * px + py), dst_ref=outs[a].at[kme], send_sem=send_sems.at[3 * a + j],
            recv_sem=recv_sems.at[3 * a + j], device_id=(px, py, c), device_id_type=MESH)
            for j, (px, py) in enumerate(chips) for a in range(n)]
        return cps + [pltpu.make_async_copy(slab(a, kme), outs[a].at[kme], local_sems.at[a]) for a in range(n)]

    def start(ins, outs, sems):
        for cp in plan(ins, outs, sems):
            cp.start()

    def finish(ins, outs, sems):
        for cp in plan(ins, outs, sems):
            cp.wait()

    return _Rider(
        parts, [jax.ShapeDtypeStruct((4, s[0], s[1] // 2, s[2]), p.dtype) for s, p in zip(shapes, parts)],
        [pltpu.SemaphoreType.DMA((3 * n,)), pltpu.SemaphoreType.DMA((3 * n,)), pltpu.SemaphoreType.DMA((n,))],
        start, finish)


def _rs_share(shards, slabs, name):
    n = len(shards)

    def body(*refs):
        outs = refs[n:2 * n]
        send_sems, recv_sems = refs[2 * n:]
        x, y, c = _mesh_pos()
        cps = []
        for a in range(n):
            l0, l1 = slabs[a]
            rows = outs[a].at[pl.ds(l0, l1 - l0), _half_rows(c, shards[a].shape[1] // 2), :]
            cps.append(pltpu.make_async_remote_copy(
                src_ref=rows, dst_ref=rows, send_sem=send_sems.at[a], recv_sem=recv_sems.at[a],
                device_id=(x, y, 1 - c), device_id_type=MESH))
        for cp in cps:
            cp.start()
        for cp in cps:
            cp.wait()

    return pl.pallas_call(
        body, name=name,
        out_shape=[jax.ShapeDtypeStruct(s.shape, s.dtype) for s in shards],
        in_specs=[pl.BlockSpec(memory_space=pl.ANY)] * n,
        out_specs=[pl.BlockSpec(memory_space=pl.ANY)] * n,
        input_output_aliases={a: a for a in range(n)},
        scratch_shapes=[pltpu.SemaphoreType.DMA((n,)), pltpu.SemaphoreType.DMA((n,))],
    )(*shards)


def _tile_rows(rows, cols, budget=2 * 1024 * 1024):
    for t in (1024, 512, 256, 128, 64, 32, 16, 8):
        if rows % t == 0 and t * cols * 4 <= budget:
            return t
    return rows


def _add_half(g3, r3, c_idx, name):
    a, h, n = r3.shape
    t = _tile_rows(h, n)
    nt = h // t

    def body(c_ref, g_ref, r_ref, o_ref):
        o_ref[...] = (g_ref[...] + r_ref[...]).astype(o_ref.dtype)

    return pl.pallas_call(
        body, name=name, out_shape=jax.ShapeDtypeStruct((a, h, n), BF16),
        grid_spec=pltpu.PrefetchScalarGridSpec(
            num_scalar_prefetch=1, grid=(a, nt),
            in_specs=[pl.BlockSpec((1, t, n), lambda k, i, c: (k, c[0] * nt + i, 0)),
                      pl.BlockSpec((1, t, n), lambda k, i, c: (k, i, 0))],
            out_specs=pl.BlockSpec((1, t, n), lambda k, i, c: (k, i, 0))),
        compiler_params=_cparams(dimension_semantics=("parallel", "parallel")),
    )(c_idx, g3, r3)


def _sum4(parts, c_idx, name, into):
    _, l, h, n = parts.shape
    n_slabs, l0, buf = into
    t = _tile_rows(h, n, 1024 * 1024)
    nt = h // t
    held = () if buf is None else (buf,)

    def body(c_ref, p_ref, *rest):
        pv = p_ref[...].astype(F32)
        rest[-1][...] = ((pv[0] + pv[1]) + pv[2]) + pv[3]

    return pl.pallas_call(
        body, name=name, out_shape=jax.ShapeDtypeStruct((n_slabs, 2 * h, n), F32),
        grid_spec=pltpu.PrefetchScalarGridSpec(
            num_scalar_prefetch=1, grid=(l, nt),
            in_specs=[pl.BlockSpec((4, 1, t, n), lambda k, i, c: (0, k, i, 0))]
            + [pl.BlockSpec(memory_space=pl.ANY)] * len(held),
            out_specs=pl.BlockSpec((1, t, n), lambda k, i, c: (l0 + k, c[0] * nt + i, 0))),
        input_output_aliases={2: 0} if held else {},
        compiler_params=_cparams(dimension_semantics=("parallel", "parallel")),
    )(c_idx, parts, *held)


def _sum8(parts, name):
    _, m, n = parts.shape

    def body(p_ref, o_ref):
        acc = p_ref[0]
        for d in range(1, 8):
            acc = acc + p_ref[d]
        o_ref[...] = acc

    return pl.pallas_call(body, name=name, out_shape=jax.ShapeDtypeStruct((m, n), F32))(parts)


def _adamw(wp, gp, mp, vp, name):
    r, n = wp.shape
    t = _tile_rows(r, n, 1024 * 1024)

    def body(w_ref, g_ref, m_ref, v_ref, d_ref, mo_ref, vo_ref):
        gv = g_ref[...]
        m2 = ADAM_B1 * m_ref[...] + (1.0 - ADAM_B1) * gv
        v2 = ADAM_B2 * v_ref[...] + (1.0 - ADAM_B2) * (gv * gv)
        m_hat = m2 / (1.0 - ADAM_B1 ** ADAM_STEP)
        v_hat = v2 / (1.0 - ADAM_B2 ** ADAM_STEP)
        d_ref[...] = -ADAM_LR * (m_hat / (jnp.sqrt(v_hat) + ADAM_EPS) + ADAM_WD * w_ref[...])
        mo_ref[...] = m2
        vo_ref[...] = v2

    spec = pl.BlockSpec((t, n), lambda i: (i, 0))
    return pl.pallas_call(
        body, name=name, out_shape=[jax.ShapeDtypeStruct((r, n), F32)] * 3, grid=(r // t,),
        in_specs=[spec] * 4, out_specs=[spec] * 3,
        compiler_params=_cparams(dimension_semantics=("parallel",)),
    )(wp, gp, mp, vp)


BIG = (('ssd_w_in', 'colx'), ('ssd_w_out', 'row'), ('mla_w_in', 'row'), ('mla_w_q_b', 'col'),
       ('mla_w_kv_b', 'col'), ('mla_w_out', 'row'), ('mlp_w_up', 'col'), ('mlp_w_down', 'row'))
SMALL_SHARDED = (('meta_tokens', 1), ('ssd_conv_w', 2), ('mla_q_a_norm', 1), ('mla_kv_a_norm', 1))
SMALL_REPL = ('ln_mix', 'ln_mlp', 'ssd_conv_b', 'ssd_dt_bias', 'ssd_a_log', 'ssd_d', 'ssd_norm',
              'mla_q_norm', 'mla_k_norm')
ALL_NAMES = ('meta_tokens', 'ln_mix', 'ln_mlp', 'ssd_w_in', 'ssd_conv_w', 'ssd_conv_b', 'ssd_dt_bias',
             'ssd_a_log', 'ssd_d', 'ssd_norm', 'ssd_w_out', 'mla_w_in', 'mla_q_a_norm', 'mla_w_q_b',
             'mla_kv_a_norm', 'mla_w_kv_b', 'mla_q_norm', 'mla_k_norm', 'mla_w_out', 'mlp_w_up', 'mlp_w_down')


_MLA_BIG = ('mla_w_in', 'mla_w_q_b', 'mla_w_kv_b', 'mla_w_out')
GATHER_ROUNDS = (
    (('ssd_w_in', 0, 1),),
    (('ssd_w_out', 0, 1), ('mlp_w_up', 0, 1)),
    (('mlp_w_down', 0, 1),) + tuple((n, 0, 1) for n in _MLA_BIG) + (('mlp_w_up', 1, 2), ('mlp_w_down', 1, 2)),
    (('ssd_w_in', 1, 2), ('ssd_w_out', 1, 2)) + tuple((n, 1, 2) for n in _MLA_BIG)
    + (('mlp_w_up', 2, 4), ('mlp_w_down', 2, 4)),
)
GATHER_HOSTS = {'ssd_in_f0': 1, 'ssd_core_f0': 2, 'mla_attn_f1': 3}


REDUCE_ROUNDS = (
    GATHER_ROUNDS[3],
    tuple((n, 0, 1) for n in _MLA_BIG) + (('mlp_w_up', 0, 2), ('mlp_w_down', 0, 2), ('ssd_w_out', 0, 1)),
    (('ssd_w_in', 0, 1),),
)


class _GatheredWeights(_Matrices):
    def __init__(self, shards):
        super().__init__()
        self.shards, self.whole = shards, {}

    def raw(self, name, slab):
        return self.whole[(name, slab)]

    def _round(self, r):
        spec = GATHER_ROUNDS[r]
        return _gather_rider([self.shards[n][l0:l1] for n, l0, l1 in spec], [dict(BIG)[n] for n, _, _ in spec])

    def _take(self, r, outs):
        for (n, l0, l1), o in zip(GATHER_ROUNDS[r], outs):
            kind = dict(BIG)[n]
            for l in range(l0, l1):
                if kind == 'row':
                    m = o[l - l0].reshape(-1, o.shape[-1])
                elif kind == 'col':
                    m = o[l - l0]
                else:
                    m = jnp.concatenate([o[k, l - l0] for k in range(4)], axis=-1)
                self.whole[(n, l)] = m

    def start(self):
        self._take(0, _run_rider(self._round(0), "gather_first"))

    def rider(self, host):
        return self._round(GATHER_HOSTS[host]) if host in GATHER_HOSTS else None

    def deliver(self, host, outs):
        if host in GATHER_HOSTS:
            self._take(GATHER_HOSTS[host], outs)


class _ScatterGrads:
    def __init__(self, shard_shapes, c_idx):
        self.shard_shapes, self.c_idx, self.out = shard_shapes, c_idx, {}

    def begin(self, r, grads):
        spec = REDUCE_ROUNDS[r]
        kinds = [dict(BIG)[n] for n, _, _ in spec]
        shapes = [(l1 - l0,) + tuple(self.shard_shapes[n][1:]) for n, l0, l1 in spec]
        wholes = []
        for (n, _, _), kind, s in zip(spec, kinds, shapes):
            if kind == 'row':
                wholes.append(grads[n].reshape(s[0], 4, s[1], s[2]))
            elif kind == 'col':
                wholes.append(grads[n])
            else:
                wholes.append(jnp.stack([grads[n][..., k * s[2]:(k + 1) * s[2]] for k in range(4)]))
        self.swapping = (kinds, shapes, wholes)
        return _swap_rider(wholes, kinds, shapes)

    def middle(self, r, recv):
        spec = REDUCE_ROUNDS[r]
        kinds, shapes, wholes = self.swapping
        parts = []
        for (n, _, _), kind, s, gw, rc in zip(spec, kinds, shapes, wholes, recv):
            if kind == 'col':
                g3, r3 = gw, rc
            else:
                g3, r3 = gw.reshape(-1, s[1], s[2]), rc.reshape(-1, s[1] // 2, s[2])
            parts.append(_add_half(g3, r3, self.c_idx, f"rs_add{r}_{n}").reshape(rc.shape))
        return _exchange_rider(parts, kinds, shapes)

    def finish(self, r, outs):
        spec = REDUCE_ROUNDS[r]
        for (n, l0, _), part in zip(spec, outs):
            self.out[n] = _sum4(part, self.c_idx, f"rs_sum{r}_{n}",
                                into=(self.shard_shapes[n][0], l0, self.out.get(n)))
        shared = _rs_share([self.out[n] for n, _, _ in spec], [(l0, l1) for _, l0, l1 in spec], f"rs_share{r}")
        self.out.update(zip([n for n, _, _ in spec], shared))


def _pack(arrs, rows_mult):
    flat = jnp.concatenate([a.reshape(-1) for a in arrs])
    per = LANES * rows_mult
    pad = (-flat.shape[0]) % per
    if pad:
        flat = jnp.concatenate([flat, jnp.zeros((pad,), flat.dtype)])
    return flat.reshape(-1, LANES)


def _unpack(pack, shapes):
    flat = pack.reshape(-1)
    out, off = [], 0
    for shp in shapes:
        n = math.prod(shp)
        out.append(flat[off:off + n].reshape(shp))
        off += n
    return out


def _split4(full, axis):
    shp = full.shape
    r = full.reshape(shp[:axis] + (4, shp[axis] // 4) + shp[axis + 1:])
    return jnp.moveaxis(r, axis, 0)


def _join4(parts, axis):
    r = jnp.moveaxis(parts, 0, axis)
    shp = r.shape
    return r.reshape(shp[:axis] + (shp[axis] * shp[axis + 1],) + shp[axis + 2:])


def _gather_params(shards, table, dtype, c, name):
    pack = _pack([shards[n].astype(dtype) for n, _ in table], 16)
    half = pack.shape[0] // 2
    mine = lax.dynamic_slice_in_dim(pack, c * half, half, axis=0)
    full = _all_gather8(mine, name).reshape(4, -1)
    out, off = {}, 0
    for n, ax in table:
        cnt = math.prod(shards[n].shape)
        out[n] = _join4(full[:, off:off + cnt].reshape((4,) + shards[n].shape), ax)
        off += cnt
    return out


def kernel(x, meta_tokens, ln_mix, ln_mlp, ssd_w_in, ssd_conv_w, ssd_conv_b, ssd_dt_bias, ssd_a_log, ssd_d, ssd_norm, ssd_w_out, mla_w_in, mla_q_a_norm, mla_w_q_b, mla_kv_a_norm, mla_w_kv_b, mla_q_norm, mla_k_norm, mla_w_out, mlp_w_up, mlp_w_down, loss_target, m_meta_tokens, m_ln_mix, m_ln_mlp, m_ssd_w_in, m_ssd_conv_w, m_ssd_conv_b, m_ssd_dt_bias, m_ssd_a_log, m_ssd_d, m_ssd_norm, m_ssd_w_out, m_mla_w_in, m_mla_q_a_norm, m_mla_w_q_b, m_mla_kv_a_norm, m_mla_w_kv_b, m_mla_q_norm, m_mla_k_norm, m_mla_w_out, m_mlp_w_up, m_mlp_w_down, v_meta_tokens, v_ln_mix, v_ln_mlp, v_ssd_w_in, v_ssd_conv_w, v_ssd_conv_b, v_ssd_dt_bias, v_ssd_a_log, v_ssd_d, v_ssd_norm, v_ssd_w_out, v_mla_w_in, v_mla_q_a_norm, v_mla_w_q_b, v_mla_kv_a_norm, v_mla_w_kv_b, v_mla_q_norm, v_mla_k_norm, v_mla_w_out, v_mlp_w_up, v_mlp_w_down):
    w_sh = dict(meta_tokens=meta_tokens, ln_mix=ln_mix, ln_mlp=ln_mlp, ssd_w_in=ssd_w_in, ssd_conv_w=ssd_conv_w, ssd_conv_b=ssd_conv_b, ssd_dt_bias=ssd_dt_bias, ssd_a_log=ssd_a_log, ssd_d=ssd_d, ssd_norm=ssd_norm, ssd_w_out=ssd_w_out, mla_w_in=mla_w_in, mla_q_a_norm=mla_q_a_norm, mla_w_q_b=mla_w_q_b, mla_kv_a_norm=mla_kv_a_norm, mla_w_kv_b=mla_w_kv_b, mla_q_norm=mla_q_norm, mla_k_norm=mla_k_norm, mla_w_out=mla_w_out, mlp_w_up=mlp_w_up, mlp_w_down=mlp_w_down)
    m_sh = dict(meta_tokens=m_meta_tokens, ln_mix=m_ln_mix, ln_mlp=m_ln_mlp, ssd_w_in=m_ssd_w_in, ssd_conv_w=m_ssd_conv_w, ssd_conv_b=m_ssd_conv_b, ssd_dt_bias=m_ssd_dt_bias, ssd_a_log=m_ssd_a_log, ssd_d=m_ssd_d, ssd_norm=m_ssd_norm, ssd_w_out=m_ssd_w_out, mla_w_in=m_mla_w_in, mla_q_a_norm=m_mla_q_a_norm, mla_w_q_b=m_mla_w_q_b, mla_kv_a_norm=m_mla_kv_a_norm, mla_w_kv_b=m_mla_w_kv_b, mla_q_norm=m_mla_q_norm, mla_k_norm=m_mla_k_norm, mla_w_out=m_mla_w_out, mlp_w_up=m_mlp_w_up, mlp_w_down=m_mlp_w_down)
    v_sh = dict(meta_tokens=v_meta_tokens, ln_mix=v_ln_mix, ln_mlp=v_ln_mlp, ssd_w_in=v_ssd_w_in, ssd_conv_w=v_ssd_conv_w, ssd_conv_b=v_ssd_conv_b, ssd_dt_bias=v_ssd_dt_bias, ssd_a_log=v_ssd_a_log, ssd_d=v_ssd_d, ssd_norm=v_ssd_norm, ssd_w_out=v_ssd_w_out, mla_w_in=v_mla_w_in, mla_q_a_norm=v_mla_q_a_norm, mla_w_q_b=v_mla_w_q_b, mla_kv_a_norm=v_mla_kv_a_norm, mla_w_kv_b=v_mla_w_kv_b, mla_q_norm=v_mla_q_norm, mla_k_norm=v_mla_k_norm, mla_w_out=v_mla_w_out, mlp_w_up=v_mlp_w_up, mlp_w_down=v_mlp_w_down)

    cx, cy, cc = lax.axis_index("x"), lax.axis_index("y"), lax.axis_index("c")
    chip = 2 * cx + cy

    c_idx = cc.reshape(1).astype(jnp.int32)
    big_names = [n for n, _ in BIG]
    shapes = [w_sh[n].shape for n in big_names]

    w = {n: w_sh[n] for n in SMALL_REPL}
    w.update(_gather_params(w_sh, SMALL_SHARDED, F32, cc, "gather_small"))
    big = _GatheredWeights({n: w_sh[n].astype(BF16) for n in big_names})
    red = _ScatterGrads({n: w_sh[n].shape for n in big_names}, c_idx)

    loss_row, grad_x, grads, red = _local_step(x[0], loss_target[0], w, big, red)
    loss = lax.psum(jnp.sum(loss_row), ("x", "y", "c"))
    g_sh = dict(red.out)

    small_names = tuple(n for n, _ in SMALL_SHARDED) + SMALL_REPL
    sp = _pack([grads[n] for n in small_names], 8)
    srows = sp.shape[0]
    s_all = _sum8(_all_gather8(sp, "ar_small_gather").reshape(8, srows, LANES), "ar_small_sum")
    s_full = dict(zip(small_names, _unpack(s_all, [grads[n].shape for n in small_names])))
    for n, ax in SMALL_SHARDED:
        g_sh[n] = lax.dynamic_index_in_dim(_split4(s_full[n], ax), chip, axis=0, keepdims=False)
    for n in SMALL_REPL:
        g_sh[n] = s_full[n]

    delta, new_m, new_v = {}, {}, {}
    for n, s in zip(big_names, shapes):
        res = _adamw(*[t[n].reshape(-1, s[2]) for t in (w_sh, g_sh, m_sh, v_sh)], f"adamw_{n}")
        delta[n], new_m[n], new_v[n] = [r.reshape(s) for r in res]
    d_s, m_s, v_s = _adamw(*[_pack([t[n] for n in small_names], 8) for t in (w_sh, g_sh, m_sh, v_sh)],
                           "adamw_small")
    for dst, ps in ((delta, d_s), (new_m, m_s), (new_v, v_s)):
        dst.update(zip(small_names, _unpack(ps, [w_sh[n].shape for n in small_names])))

    return (loss, grad_x[None], *[g_sh[n] for n in ALL_NAMES], *[delta[n] for n in ALL_NAMES],
            *[new_m[n] for n in ALL_NAMES], *[new_v[n] for n in ALL_NAMES])
```

```python
import functools
import math

import jax
import jax.numpy as jnp
from jax import lax
from jax.experimental import pallas as pl
from jax.experimental.pallas import tpu as pltpu

F32 = jnp.float32
BF16 = jnp.bfloat16
MESH = pl.DeviceIdType.MESH
_NN = (((1,), (0,)), ((), ()))
_NT = (((1,), (1,)), ((), ()))
_TN = (((0,), (0,)), ((), ()))

D_MODEL = 1024
N_META = 16
EPS = 1e-6
SSD_D_INNER = 2048
SSD_HEADS = 32
SSD_HEAD_DIM = 64
SSD_GROUPS = 8
SSD_HPG = 4
SSD_STATE = 128
SSD_CONV = 4
CHUNK = 128
SSD_IN_DIM = 6176
SSD_IN_PAD = 6272
MLA_HEADS = 16
MLA_NOPE = 64
MLA_ROPE = 32
MLA_V = 64
MLA_QK = 96
MLA_Q_RANK = 384
MLA_KV_RANK = 256
HEAD_SLOT = 128
MLA_WIDE = MLA_HEADS * HEAD_SLOT
HEADS_PER_STEP = 2
LAT_PAD = 768
ROPE_THETA = 10000.0
D_FF = 4096
NPAD = CHUNK - N_META
ADAM_LR, ADAM_B1, ADAM_B2, ADAM_EPS, ADAM_WD, ADAM_STEP = 0.001, 0.9, 0.999, 1e-08, 0.01, 10
LANES = 1024
V7X_VMEM_BYTES = 64 * 1024 * 1024
VMEM_LIMIT = V7X_VMEM_BYTES * 7 // 8


def _pick(n, cands):
    for c in cands:
        if n % c == 0:
            return c
    return n


def _cparams(**kw):
    return pltpu.CompilerParams(vmem_limit_bytes=VMEM_LIMIT, **kw)


def _slab_spec(spec, slab):
    return pl.BlockSpec((None,) + tuple(spec.block_shape), lambda *ids: (slab,) + tuple(spec.index_map(*ids)))


def _mm(a, b, dims, *, name, out_dtype=F32, a_fn=None, epi=None, extras=(), stack=None, norm_gain=None,
        rider=None):
    b, b_slab = b if isinstance(b, tuple) else (b, None)
    b_shape = b.shape if b_slab is None else b.shape[1:]
    if dims == 'nn':
        (M, K), (K2, N) = a.shape, b_shape
    elif dims == 'nt':
        (M, K), (N, K2) = a.shape, b_shape
    else:
        (K, M), (K2, N) = a.shape, b_shape
    assert K == K2, (a.shape, b.shape, dims)
    if dims == 'tn':
        tm = _pick(M, (1024, 768, 512, 384, 256, 128))
        tn = _pick(N, (1024, 896, 768, 512, 384, 256, 128))
        tk = _pick(K, (1408, 1024, 512, 384, 256, 128))
    else:
        tm = _pick(M, (704, 512, 384, 256, 128) if norm_gain is not None else (1408, 1024, 512, 384, 256, 128))
        tn = _pick(N, (1024, 896, 768, 512, 384, 256, 128))
        tk = _pick(K, (1024, 896, 768, 512, 384, 256, 128))
    nk = K // tk
    if dims == 'nn':
        a_spec = pl.BlockSpec((tm, tk), lambda i, j, k: (i, k))
        b_spec = pl.BlockSpec((tk, tn), lambda i, j, k: (k, j))
        dn = (((1,), (0,)), ((), ()))
    elif dims == 'nt':
        a_spec = pl.BlockSpec((tm, tk), lambda i, j, k: (i, k))
        b_spec = pl.BlockSpec((tn, tk), lambda i, j, k: (j, k))
        dn = (((1,), (1,)), ((), ()))
    else:
        a_spec = pl.BlockSpec((tk, tm), lambda i, j, k: (k, i))
        b_spec = pl.BlockSpec((tk, tn), lambda i, j, k: (k, j))
        dn = (((0,), (0,)), ((), ()))
    if b_slab is not None:
        b_spec = _slab_spec(b_spec, b_slab)
    o_spec = pl.BlockSpec((tm, tn), lambda i, j, k: (i, j))
    n_ex = len(extras)
    out_shape = jax.ShapeDtypeStruct((M, N), out_dtype)
    out_spec, held, aliases = o_spec, (), {}
    if stack is not None:
        n_slabs, slab, buf = stack
        out_shape = jax.ShapeDtypeStruct((n_slabs, M, N), out_dtype)
        out_spec = pl.BlockSpec((None, tm, tn), lambda i, j, k: (slab, i, j))
        if buf is not None:
            held, aliases = (buf,), {2 + n_ex: 0}

    gains = ()
    if norm_gain is not None:
        assert tn == N and stack is None, "the rms epilogue needs whole rows"
        gains = (norm_gain,)
        out_shape = [out_shape, jax.ShapeDtypeStruct((M, N), BF16)]
        out_spec = [out_spec, o_spec]

    n_own_in = 2 + n_ex + len(gains) + len(held)
    steps = (M // tm, N // tn, nk)

    def body(*refs):
        (a_ref, b_ref, *rest), ride = _rider_split(rider, refs, n_own_in, 1 + len(gains), 1)
        ex_refs, rest = rest[:n_ex], rest[n_ex:]
        g_refs, rest = rest[:len(gains)], rest[len(gains) + len(held):]
        o_ref, acc = rest[0], rest[-1]
        k = pl.program_id(2)
        if ride is not None:
            @pl.when((pl.program_id(0) == 0) & (pl.program_id(1) == 0) & (k == 0))
            def _():
                rider.start(*ride)

        @pl.when(k == 0)
        def _():
            acc[...] = jnp.zeros_like(acc)

        av = a_ref[...]
        if a_fn is not None:
            av = a_fn(av)
        acc[...] += lax.dot_general(av.astype(BF16), b_ref[...].astype(BF16), dn,
                                    preferred_element_type=F32)

        @pl.when(k == nk - 1)
        def _():
            r = acc[...]
            if epi is not None:
                r = epi(r, *[e[...] for e in ex_refs])
            o_ref[...] = r.astype(out_dtype)
            if gains:
                rest[1][...] = _rms(r, g_refs[0][...]).astype(BF16)

        if ride is not None:
            @pl.when((pl.program_id(0) == steps[0] - 1) & (pl.program_id(1) == steps[1] - 1) & (k == nk - 1))
            def _():
                rider.finish(*ride)

    r_in, r_out, r_shapes, r_scratch = _rider_args(rider)
    own_shapes = out_shape if isinstance(out_shape, list) else [out_shape]
    own_specs = out_spec if isinstance(out_spec, list) else [out_spec]
    res = pl.pallas_call(
        body, name=name,
        out_shape=own_shapes + r_shapes,
        grid=steps,
        in_specs=[a_spec, b_spec] + [o_spec] * n_ex
        + [pl.BlockSpec((1, tn), lambda i, j, k: (0, j))] * len(gains)
        + [pl.BlockSpec(memory_space=pl.ANY)] * len(held) + r_in,
        out_specs=own_specs + r_out,
        input_output_aliases=aliases,
        scratch_shapes=[pltpu.VMEM((tm, tn), F32)] + r_scratch,
        compiler_params=_cparams(dimension_semantics=("arbitrary", "arbitrary", "arbitrary")),
    )(a, b, *extras, *gains, *held, *(rider.operands if rider else ()))
    return res[0] if len(res) == 1 else res


def _mm_rms_bwd(cot, w_t, h, d_res, gain, name):
    w_t, w_slab = w_t if isinstance(w_t, tuple) else (w_t, None)
    (M, K), N = cot.shape, w_t.shape[-2]
    tm = _pick(M, (704, 512, 384, 256, 128))
    tk = _pick(K, (1024, 896, 768, 512, 384, 256, 128))
    nk = K // tk
    w_spec = pl.BlockSpec((N, tk), lambda i, k: (0, k))
    if w_slab is not None:
        w_spec = _slab_spec(w_spec, w_slab)

    def body(a_ref, b_ref, h_ref, r_ref, g_ref, dh_ref, dg_ref, acc):
        i, k = pl.program_id(0), pl.program_id(1)

        @pl.when(k == 0)
        def _():
            acc[...] = jnp.zeros_like(acc)

        acc[...] += lax.dot_general(a_ref[...].astype(BF16), b_ref[...].astype(BF16), _NT,
                                    preferred_element_type=F32)

        @pl.when(k == nk - 1)
        def _():
            _, vjp = jax.vjp(_rms, h_ref[...], g_ref[...])
            dh, dg = vjp(acc[...])
            dh_ref[...] = (r_ref[...] + dh) * _row_mask(i, tm)

            @pl.when(i == 0)
            def _():
                dg_ref[...] = dg

            @pl.when(i > 0)
            def _():
                dg_ref[...] += dg

    rows = pl.BlockSpec((tm, N), lambda i, k: (i, 0))
    vec = pl.BlockSpec((1, N), lambda i, k: (0, 0))
    return pl.pallas_call(
        body, name=name,
        out_shape=[jax.ShapeDtypeStruct((M, N), F32), jax.ShapeDtypeStruct((1, N), F32)],
        grid=(M // tm, nk),
        in_specs=[pl.BlockSpec((tm, tk), lambda i, k: (i, k)), w_spec, rows, rows, vec],
        out_specs=[rows, vec],
        scratch_shapes=[pltpu.VMEM((tm, N), F32)],
        compiler_params=_cparams(dimension_semantics=("arbitrary", "arbitrary")),
    )(cot, w_t, h, d_res, gain)


def _mm_attn_do(dh, w_out_t, o, name):
    (M, K), (N, _) = dh.shape, w_out_t.shape
    tm = _pick(M, (704, 512, 384, 256, 128))
    tn = 8 * HEAD_SLOT

    def body(a_ref, b_ref, o_ref, dob_ref, delta_ref):
        do = lax.dot_general(a_ref[...].astype(BF16), b_ref[...], _NT, preferred_element_type=F32)
        dob_ref[...] = do.astype(BF16)
        for hh in range(tn // HEAD_SLOT):
            sl = slice(hh * HEAD_SLOT, (hh + 1) * HEAD_SLOT)
            delta_ref[hh] = jnp.sum(do[:, sl] * o_ref[:, sl], axis=-1, keepdims=True)

    tile = pl.BlockSpec((tm, tn), lambda i, j: (i, j))
    return pl.pallas_call(
        body, name=name,
        out_shape=[jax.ShapeDtypeStruct((M, N), BF16), jax.ShapeDtypeStruct((N // HEAD_SLOT, M, 1), F32)],
        grid=(M // tm, N // tn),
        in_specs=[pl.BlockSpec((tm, K), lambda i, j: (i, 0)), pl.BlockSpec((tn, K), lambda i, j: (j, 0)), tile],
        out_specs=[tile, pl.BlockSpec((tn // HEAD_SLOT, tm, 1), lambda i, j: (j, i, 0))],
        compiler_params=_cparams(dimension_semantics=("parallel", "parallel")),
    )(dh, w_out_t, o)


def _row_call(fn, rows, consts, out_rows, out_accs=(), *, n_rows, tile, name):
    n_r, n_c, n_o, n_a = len(rows), len(consts), len(out_rows), len(out_accs)
    steps = n_rows // tile

    def body(*refs):
        r_refs = refs[:n_r]
        c_refs = refs[n_r:n_r + n_c]
        o_refs = refs[n_r + n_c:n_r + n_c + n_o]
        a_refs = refs[n_r + n_c + n_o:]
        i = pl.program_id(0)
        res = fn(i, *[r[...] for r in r_refs], *[c[...] for c in c_refs])
        for o_ref, val in zip(o_refs, res[:n_o]):
            o_ref[...] = val.astype(o_ref.dtype)

        @pl.when(i == 0)
        def _():
            for a_ref in a_refs:
                a_ref[...] = jnp.zeros_like(a_ref)

        for a_ref, val in zip(a_refs, res[n_o:]):
            a_ref[...] += val

    in_specs = [pl.BlockSpec((tile, w), functools.partial(lambda i, cb: (i, cb), cb=cb))
                for (_, w, cb) in rows]
    in_specs += [pl.BlockSpec(c.shape, lambda i: (0, 0)) for c in consts]
    out_specs = [pl.BlockSpec((tile, c), lambda i: (i, 0)) for (c, _) in out_rows]
    out_specs += [pl.BlockSpec(s, lambda i: (0, 0)) for s in out_accs]
    out_shape = [jax.ShapeDtypeStruct((n_rows, c), dt) for (c, dt) in out_rows]
    out_shape += [jax.ShapeDtypeStruct(s, F32) for s in out_accs]
    return pl.pallas_call(
        body, name=name, out_shape=out_shape, grid=(steps,),
        in_specs=in_specs, out_specs=out_specs,
        compiler_params=_cparams(dimension_semantics=("arbitrary",)),
    )(*[r[0] for r in rows], *consts)


def _row_mask(i, tile):
    r = i * tile + lax.broadcasted_iota(jnp.int32, (tile, 1), 0)
    return (r >= NPAD).astype(F32)


def _rms(x, g):
    return x * lax.rsqrt(jnp.mean(x * x, axis=-1, keepdims=True) + EPS) * g


def _silu(x):
    return x * (0.5 * jnp.tanh(0.5 * x) + 0.5)


def _softplus(x):
    return jnp.maximum(x, 0.0) + jnp.log(1.0 + jnp.exp(-jnp.abs(x)))


def _rms_fwd(h, g, name):
    lp = h.shape[0]
    return _row_call(lambda i, hv, gv: (_rms(hv, gv),), [(h, D_MODEL, 0)], [g],
                     [(D_MODEL, BF16)], n_rows=lp, tile=_pick(lp, (384, 256, 128)), name=name)[0]


@functools.partial(jax.custom_vjp, nondiff_argnums=(1,))
def _roll_rows(x, s):
    return pltpu.roll(x, s, 0)


def _roll_rows_fwd(x, s):
    return pltpu.roll(x, s, 0), None


def _roll_rows_bwd(s, _, ct):
    return (pltpu.roll(ct, (ct.shape[0] - s) % ct.shape[0], 0),)


_roll_rows.defvjp(_roll_rows_fwd, _roll_rows_bwd)


def _conv_silu(cur, halo, w_rows, b):
    full = jnp.concatenate([halo, cur], axis=0)
    acc = cur * w_rows[SSD_CONV - 1] + b
    for k in range(SSD_CONV - 1):
        acc = acc + _roll_rows(full, SSD_CONV - 1 - k)[8:] * w_rows[k]
    return _silu(acc)


def _split3(v):
    hi = v.astype(BF16)
    r1 = v - hi.astype(F32)
    mid = r1.astype(BF16)
    lo = (r1 - mid.astype(F32)).astype(BF16)
    return hi, mid, lo


def _select_right(v, sel, dn):
    return sum(lax.dot_general(p, sel, dn, preferred_element_type=F32) for p in _split3(v))


@jax.custom_vjp
def _expand_heads(v, e_mat):
    return _select_right(v, e_mat, _NN)


def _expand_heads_fwd(v, e_mat):
    return _select_right(v, e_mat, _NN), e_mat


def _expand_heads_bwd(e_mat, ct):
    return _select_right(ct, e_mat, _NT), jnp.zeros_like(e_mat)


_expand_heads.defvjp(_expand_heads_fwd, _expand_heads_bwd)


@jax.custom_vjp
def _cumsum_rows(a, tri):
    return sum(lax.dot_general(tri, p, _NN, preferred_element_type=F32) for p in _split3(a))


def _cumsum_rows_fwd(a, tri):
    return _cumsum_rows(a, tri), tri


def _cumsum_rows_bwd(tri, ct):
    return (sum(lax.dot_general(tri, p, _TN, preferred_element_type=F32) for p in _split3(ct)),
            jnp.zeros_like(tri))


_cumsum_rows.defvjp(_cumsum_rows_fwd, _cumsum_rows_bwd)


def _ssd_chunk(mask, z, xs_pre, bc_pre, halo_x, halo_bc, dt_pre, st, cwx0, cwx1, cwx2, cwx3,
               cwb0, cwb1, cwb2, cwb3, cb_x, cb_bc, dtb, alog, dsk, ng):
    L = CHUNK
    lane_h = lax.broadcasted_iota(jnp.int32, (1, 128), 1)
    head_ok = (lane_h < SSD_HEADS).astype(F32)
    e_mat = (lax.broadcasted_iota(jnp.int32, (128, SSD_D_INNER), 1) // SSD_HEAD_DIM
             == lax.broadcasted_iota(jnp.int32, (128, SSD_D_INNER), 0)).astype(BF16)
    ri = lax.broadcasted_iota(jnp.int32, (L, L), 0)
    ci = lax.broadcasted_iota(jnp.int32, (L, L), 1)
    causal = ri >= ci

    xs = _conv_silu(xs_pre, halo_x, (cwx0, cwx1, cwx2, cwx3), cb_x) * mask
    bc = _conv_silu(bc_pre, halo_bc, (cwb0, cwb1, cwb2, cwb3), cb_bc) * mask
    dt = _softplus(dt_pre + dtb) * mask * head_ok
    a_dt = dt * (-jnp.exp(alog))
    a_cs = _cumsum_rows(a_dt, causal.astype(BF16))
    a_cs_t = a_cs.T
    row8 = lax.broadcasted_iota(jnp.int32, (8, 128), 0)
    last8 = jnp.where(row8 == 0, jnp.sum(a_dt, axis=0, keepdims=True), 0.0)
    dsk8 = jnp.where(row8 == 0, dsk, 0.0)
    wide = _expand_heads(jnp.concatenate([dt, a_cs, last8, dsk8], axis=0), e_mat)
    dt_e, acs_e = wide[0:L], wide[L:2 * L]
    last_e = jnp.sum(wide[2 * L:2 * L + 8], axis=0, keepdims=True)
    d_e = jnp.sum(wide[2 * L + 8:2 * L + 16], axis=0, keepdims=True)
    xdt = xs * dt_e
    dte_e = jnp.exp(last_e - acs_e)
    dfs_e = jnp.exp(acs_e)
    cd_e = jnp.exp(last_e)
    sub_h = lax.broadcasted_iota(jnp.int32, (128, L), 0)
    lane_hl = lax.broadcasted_iota(jnp.int32, (L, 128), 1)
    lane_g = lax.broadcasted_iota(jnp.int32, (1, SSD_HPG * SSD_HEAD_DIM), 1) // SSD_HEAD_DIM

    ys, new_st = [], []
    for g in range(SSD_GROUPS):
        b_g = bc[:, g * 128:(g + 1) * 128].astype(BF16)
        c_g = bc[:, 1024 + g * 128:1024 + (g + 1) * 128].astype(BF16)
        gs = slice(g * 256, (g + 1) * 256)
        xdt_g = xdt[:, gs]
        cb = lax.dot_general(c_g, b_g, (((1,), (1,)), ((), ())), preferred_element_type=F32)
        st_g = st[g * 128:(g + 1) * 128, :]
        y_g = lax.dot_general(c_g, st_g.astype(BF16), (((1,), (0,)), ((), ())),
                              preferred_element_type=F32) * dfs_e[:, gs]
        for j in range(SSD_HPG):
            h = g * SSD_HPG + j
            col = jnp.sum(jnp.where(lane_hl == h, a_cs, 0.0), axis=1, keepdims=True)
            row = jnp.sum(jnp.where(sub_h == h, a_cs_t, 0.0), axis=0, keepdims=True)
            dec = jnp.where(causal, jnp.exp(jnp.where(causal, col - row, 0.0)), 0.0)
            m_h = (cb * dec).astype(BF16)
            x_h = jnp.where(lane_g == j, xdt_g, 0.0).astype(BF16)
            y_g = y_g + lax.dot_general(m_h, x_h, (((1,), (0,)), ((), ())),
                                        preferred_element_type=F32)
        s_new = lax.dot_general(b_g, (xdt_g * dte_e[:, gs]).astype(BF16), (((0,), (0,)), ((), ())),
                                preferred_element_type=F32)
        new_st.append(st_g * cd_e[:, gs] + s_new)
        ys.append(y_g)
    y = jnp.concatenate(ys, axis=1) + xs * d_e
    gg = y * _silu(z)
    outs = []
    for g in range(SSD_GROUPS):
        sl = gg[:, g * 256:(g + 1) * 256]
        outs.append(sl * lax.rsqrt(jnp.mean(sl * sl, axis=-1, keepdims=True) + EPS))
    out = jnp.concatenate(outs, axis=1) * ng
    return out, jnp.concatenate(new_st, axis=0)


def _ssd_consts(conv_w, conv_b, dtb, alog, dsk, ng):
    return [conv_w, conv_b, dtb, alog, dsk, ng]


def _ssd_param_vals(cw_ref, cb_ref, dtb_ref, alog_ref, dsk_ref, ng_ref):
    cwx = [cw_ref[k:k + 1, 0:SSD_D_INNER] for k in range(SSD_CONV)]
    cwb = [cw_ref[k:k + 1, SSD_D_INNER:2 * SSD_D_INNER] for k in range(SSD_CONV)]
    return (*cwx, *cwb, cb_ref[:, 0:SSD_D_INNER], cb_ref[:, SSD_D_INNER:2 * SSD_D_INNER],
            dtb_ref[...], alog_ref[...], dsk_ref[...], ng_ref[...])


def _ssd_in_specs(rev, nc):
    def cidx(i):
        return (nc - 1 - i) if rev else i

    def halo(cb):
        return pl.BlockSpec((8, SSD_D_INNER), lambda i: (jnp.maximum(16 * cidx(i) - 1, 0), cb))

    return [
        pl.BlockSpec((CHUNK, SSD_D_INNER), lambda i: (cidx(i), 0)),
        pl.BlockSpec((CHUNK, SSD_D_INNER), lambda i: (cidx(i), 1)),
        pl.BlockSpec((CHUNK, SSD_D_INNER), lambda i: (cidx(i), 2)),
        halo(1), halo(2),
        pl.BlockSpec((CHUNK, 128), lambda i: (cidx(i), 48)),
    ]


class _Rider:
    def __init__(self, operands, out_shapes, scratch, start, finish):
        self.operands, self.out_shapes, self.scratch = list(operands), list(out_shapes), list(scratch)
        self.start, self.finish = start, finish


def _rider_split(rider, refs, n_in, n_out, n_scratch):
    if rider is None:
        return refs, None
    ni, no = len(rider.operands), len(rider.out_shapes)
    own = refs[:n_in] + refs[n_in + ni:n_in + ni + n_out] + refs[n_in + ni + n_out + no:n_in + ni + n_out + no + n_scratch]
    mine = (refs[n_in:n_in + ni], refs[n_in + ni + n_out:n_in + ni + n_out + no],
            refs[n_in + ni + n_out + no + n_scratch:])
    return own, mine


def _rider_args(rider):
    if rider is None:
        return [], [], [], []
    hbm = pl.BlockSpec(memory_space=pl.ANY)
    return ([hbm] * len(rider.operands), [hbm] * len(rider.out_shapes), rider.out_shapes, rider.scratch)


def _ssd_fwd(zxd, consts, name, rider=None):
    lp = zxd.shape[0]
    nc = lp // CHUNK

    def body(*refs):
        own, ride = _rider_split(rider, refs, 12, 2, 1)
        (z_ref, xs_ref, bc_ref, hx_ref, hb_ref, dt_ref, cw_ref, cb_ref, dtb_ref, alog_ref,
         dsk_ref, ng_ref, y_ref, st_ref, state) = own
        c = pl.program_id(0)

        @pl.when(c == 0)
        def _():
            state[...] = jnp.zeros_like(state)
            if ride is not None:
                rider.start(*ride)

        live = (c > 0).astype(F32)
        st_ref[0] = state[...]
        out, st_new = _ssd_chunk(
            _row_mask(c, CHUNK), z_ref[...], xs_ref[...], bc_ref[...], hx_ref[...] * live,
            hb_ref[...] * live, dt_ref[...], state[...],
            *_ssd_param_vals(cw_ref, cb_ref, dtb_ref, alog_ref, dsk_ref, ng_ref))
        y_ref[...] = out.astype(y_ref.dtype)
        state[...] = st_new

        if ride is not None:
            @pl.when(c == nc - 1)
            def _():
                rider.finish(*ride)

    r_in, r_out, r_shapes, r_scratch = _rider_args(rider)
    return pl.pallas_call(
        body, name=name,
        out_shape=[jax.ShapeDtypeStruct((lp, SSD_D_INNER), BF16),
                   jax.ShapeDtypeStruct((nc, SSD_GROUPS * SSD_STATE, 256), F32)] + r_shapes,
        grid=(nc,),
        in_specs=_ssd_in_specs(False, nc) + [pl.BlockSpec(c.shape, lambda i: (0, 0)) for c in consts] + r_in,
        out_specs=[pl.BlockSpec((CHUNK, SSD_D_INNER), lambda i: (i, 0)),
                   pl.BlockSpec((1, SSD_GROUPS * SSD_STATE, 256), lambda i: (i, 0, 0))] + r_out,
        scratch_shapes=[pltpu.VMEM((SSD_GROUPS * SSD_STATE, 256), F32)] + r_scratch,
        compiler_params=_cparams(dimension_semantics=("arbitrary",)),
    )(zxd, zxd, zxd, zxd, zxd, zxd, *consts, *(rider.operands if rider else ()))


def _ssd_bwd(zxd, states, d_y, consts, name, rider=None):
    lp = zxd.shape[0]
    nc = lp // CHUNK

    def body(*refs):
        own, ride = _rider_split(rider, refs, 14, 7, 3)
        (z_ref, xs_ref, bc_ref, hx_ref, hb_ref, dt_ref, st_ref, dy_ref, cw_ref, cb_ref, dtb_ref,
         alog_ref, dsk_ref, ng_ref, dz_ref, dcw_ref, dcb_ref, ddtb_ref, dalog_ref, ddsk_ref,
         dng_ref, d_state, d_hx, d_hb) = own
        i = pl.program_id(0)
        c = nc - 1 - i

        @pl.when(i == 0)
        def _():
            d_state[...] = jnp.zeros_like(d_state)
            d_hx[...] = jnp.zeros_like(d_hx)
            d_hb[...] = jnp.zeros_like(d_hb)
            for r in (dcw_ref, dcb_ref, ddtb_ref, dalog_ref, ddsk_ref, dng_ref):
                r[...] = jnp.zeros_like(r)
            if ride is not None:
                rider.start(*ride)

        live = (c > 0).astype(F32)
        fn = functools.partial(_ssd_chunk, _row_mask(c, CHUNK))
        prim = (z_ref[...], xs_ref[...], bc_ref[...], hx_ref[...] * live, hb_ref[...] * live,
                dt_ref[...], st_ref[0],
                *_ssd_param_vals(cw_ref, cb_ref, dtb_ref, alog_ref, dsk_ref, ng_ref))
        _, vjp = jax.vjp(fn, *prim)
        (d_z, d_xs, d_bc, g_hx, g_hb, d_dt, g_st, *d_par) = vjp((dy_ref[...], d_state[...]))
        zeros = jnp.zeros((CHUNK - 8, SSD_D_INNER), F32)
        d_xs = d_xs + jnp.concatenate([zeros, d_hx[...]], axis=0)
        d_bc = d_bc + jnp.concatenate([zeros, d_hb[...]], axis=0)
        dz_ref[:, 0:SSD_D_INNER] = d_z.astype(dz_ref.dtype)
        dz_ref[:, SSD_D_INNER:2 * SSD_D_INNER] = d_xs.astype(dz_ref.dtype)
        dz_ref[:, 2 * SSD_D_INNER:3 * SSD_D_INNER] = d_bc.astype(dz_ref.dtype)
        dz_ref[:, 3 * SSD_D_INNER:] = d_dt.astype(dz_ref.dtype)
        d_state[...] = g_st
        d_hx[...] = g_hx * live
        d_hb[...] = g_hb * live
        for k in range(SSD_CONV):
            dcw_ref[k:k + 1, 0:SSD_D_INNER] += d_par[k]
            dcw_ref[k:k + 1, SSD_D_INNER:2 * SSD_D_INNER] += d_par[SSD_CONV + k]
        dcb_ref[:, 0:SSD_D_INNER] += d_par[8]
        dcb_ref[:, SSD_D_INNER:2 * SSD_D_INNER] += d_par[9]
        ddtb_ref[...] += d_par[10]
        dalog_ref[...] += d_par[11]
        ddsk_ref[...] += d_par[12]
        dng_ref[...] += d_par[13]

        if ride is not None:
            @pl.when(i == nc - 1)
            def _():
                rider.finish(*ride)

    const_specs = [pl.BlockSpec(c.shape, lambda i: (0, 0)) for c in consts]
    r_in, r_out, r_shapes, r_scratch = _rider_args(rider)
    return pl.pallas_call(
        body, name=name,
        out_shape=[jax.ShapeDtypeStruct((lp, SSD_IN_PAD), BF16)]
        + [jax.ShapeDtypeStruct(c.shape, F32) for c in consts] + r_shapes,
        grid=(nc,),
        in_specs=_ssd_in_specs(True, nc)
        + [pl.BlockSpec((1, SSD_GROUPS * SSD_STATE, 256), lambda i: (nc - 1 - i, 0, 0)),
           pl.BlockSpec((CHUNK, SSD_D_INNER), lambda i: (nc - 1 - i, 0))] + const_specs + r_in,
        out_specs=[pl.BlockSpec((CHUNK, SSD_IN_PAD), lambda i: (nc - 1 - i, 0))] + const_specs + r_out,
        scratch_shapes=[pltpu.VMEM((SSD_GROUPS * SSD_STATE, 256), F32),
                        pltpu.VMEM((8, SSD_D_INNER), F32), pltpu.VMEM((8, SSD_D_INNER), F32)] + r_scratch,
        compiler_params=_cparams(dimension_semantics=("arbitrary",)),
    )(zxd, zxd, zxd, zxd, zxd, zxd, states, d_y, *consts, *(rider.operands if rider else ()))


@jax.custom_vjp
def _rot_half(x):
    lane = lax.broadcasted_iota(jnp.int32, x.shape, 1)
    lo = (lane >= MLA_NOPE) & (lane < MLA_NOPE + MLA_ROPE // 2)
    hi = (lane >= MLA_NOPE + MLA_ROPE // 2) & (lane < MLA_QK)
    down = pltpu.roll(x, HEAD_SLOT - MLA_ROPE // 2, 1)
    up = pltpu.roll(x, MLA_ROPE // 2, 1)
    return jnp.where(lo, -down, jnp.where(hi, up, 0.0))


def _rot_half_fwd(x):
    return _rot_half(x), None


def _rot_half_bwd(_, ct):
    return (-_rot_half(ct),)


_rot_half.defvjp(_rot_half_fwd, _rot_half_bwd)


def _head_norm_rope(t, gain, cos, sin):
    n = t * lax.rsqrt(jnp.sum(t * t, axis=-1, keepdims=True) * (1.0 / MLA_QK) + EPS) * gain
    return n * cos + _rot_half(n) * sin


def _qk_prep(q_raw, kn_raw, kpe, cos, sin, qg, kg):
    qs, ks = [], []
    for h in range(MLA_HEADS):
        sl = slice(h * HEAD_SLOT, (h + 1) * HEAD_SLOT)
        qs.append(_head_norm_rope(q_raw[:, sl], qg, cos, sin))
        ks.append(_head_norm_rope(kn_raw[:, sl] + kpe, kg, cos, sin))
    return jnp.concatenate(qs, axis=1), jnp.concatenate(ks, axis=1)


def _lat_norm(kv_lat, q_lat, kvg, qg):
    return _rms(kv_lat, kvg), _rms(q_lat, qg)


_NEG = -1e30
_SCALE = MLA_QK ** -0.5


STRIP = 128
_EXP2_SCALE = _SCALE * math.log2(math.e)


def _strip_mask(kind, blk, c, t):
    if kind is None:
        return None
    kpos = blk * t + c * STRIP + lax.broadcasted_iota(jnp.int32, (1, STRIP), 1)
    if kind == 'keys':
        return kpos >= NPAD
    qpos = blk * t + lax.broadcasted_iota(jnp.int32, (t, 1), 0)
    return (kpos <= qpos) & ((kpos >= NPAD) | (kpos == qpos))


def _attn_fwd(q, k, v, name, rider=None):
    lp = q.shape[0]
    t = tk = _pick(lp, (384, 256, 128))
    nb = lp // t
    hp = HEADS_PER_STEP
    wide = hp * HEAD_SLOT
    heads = [slice(a * HEAD_SLOT, (a + 1) * HEAD_SLOT) for a in range(hp)]

    def body(*refs):
        (q_ref, k_ref, v_ref, o_ref, lse_ref), ride = _rider_split(rider, refs, 3, 2, 0)
        qi = pl.program_id(1)
        if ride is not None:
            @pl.when((pl.program_id(0) == 0) & (qi == 0))
            def _():
                rider.start(*ride)

        def scores(ki):
            rows = pl.ds(pl.multiple_of(ki * tk, tk), tk)
            return tuple(lax.dot_general(q_ref[:, heads[a]], k_ref[rows, heads[a]], _NT,
                                         preferred_element_type=F32) for a in range(hp))

        def update(a, ki, carry, s, mask):
            rows = pl.ds(pl.multiple_of(ki * tk, tk), tk)
            m, acc = carry
            s = jnp.where(mask, s, _NEG)
            m_new = jnp.maximum(m, jnp.max(s, axis=-1, keepdims=True))
            alpha = jnp.exp2((m - m_new) * _EXP2_SCALE)
            p = jnp.concatenate(
                [jnp.exp2((s[:, c:c + STRIP] - m_new) * _EXP2_SCALE).astype(BF16) for c in range(0, tk, STRIP)],
                axis=1)
            acc = alpha * acc + lax.dot_general(p, v_ref[rows, heads[a]], _NN, preferred_element_type=F32)
            return m_new, acc

        init = (jnp.full((t, 1), _NEG, F32), jnp.zeros((t, HEAD_SLOT), F32))
        ones_lane = lax.broadcasted_iota(jnp.int32, (1, HEAD_SLOT), 1) == MLA_V
        key_pos = lax.broadcasted_iota(jnp.int32, (1, tk), 1)
        n_full = (qi * t) // tk

        def before(ki, state):
            carry, s = state
            s_next = scores(ki + 1)
            key_ok = ki * tk + key_pos >= NPAD
            return tuple(update(a, ki, carry[a], s[a], key_ok) for a in range(hp)), s_next

        carry, s = lax.fori_loop(0, n_full, before, ((init,) * hp, scores(0)))
        qpos = qi * t + lax.broadcasted_iota(jnp.int32, (t, tk), 0)
        kpos = n_full * tk + lax.broadcasted_iota(jnp.int32, (t, tk), 1)
        diag = (kpos <= qpos) & ((kpos >= NPAD) | (kpos == qpos))
        carry = tuple(update(a, n_full, carry[a], s[a], diag) for a in range(hp))
        for a in range(hp):
            m, acc = carry[a]
            l = jnp.sum(jnp.where(ones_lane, acc, 0.0), axis=-1, keepdims=True)
            o_ref[:, heads[a]] = jnp.where(ones_lane, 0.0, acc / l * _row_mask(qi, t))
            lse_ref[a] = m * _SCALE + jnp.log(l)

        if ride is not None:
            @pl.when((pl.program_id(0) == MLA_HEADS // hp - 1) & (qi == nb - 1))
            def _():
                rider.finish(*ride)

    qspec = pl.BlockSpec((t, wide), lambda g, i: (i, g))
    kspec = pl.BlockSpec((lp, wide), lambda g, i: (0, g))
    r_in, r_out, r_shapes, r_scratch = _rider_args(rider)
    return pl.pallas_call(
        body, name=name,
        out_shape=[jax.ShapeDtypeStruct((lp, MLA_WIDE), F32),
                   jax.ShapeDtypeStruct((MLA_HEADS, lp, 1), F32)] + r_shapes,
        grid=(MLA_HEADS // hp, nb),
        in_specs=[qspec, kspec, kspec] + r_in,
        out_specs=[qspec, pl.BlockSpec((hp, t, 1), lambda g, i: (g, i, 0))] + r_out,
        scratch_shapes=r_scratch,
        compiler_params=_cparams(dimension_semantics=("arbitrary", "arbitrary")),
    )(q, k, v, *(rider.operands if rider else ()))


def _attn_bwd(q, k, v, do, lse, delta, name, rider=None):
    lp = q.shape[0]
    t = _pick(lp, (384, 256, 128))
    nb = lp // t
    ns = t // STRIP
    hp = HEADS_PER_STEP
    wide = hp * HEAD_SLOT
    heads = [slice(a * HEAD_SLOT, (a + 1) * HEAD_SLOT) for a in range(hp)]
    log2e = math.log2(math.e)

    def body(*refs):
        own, ride = _rider_split(rider, refs, 6, 3, 4)
        (q_ref, k_ref, v_ref, do_ref, lse_ref, delta_ref, dq_ref, dk_ref, dv_ref,
         s_scr, dp_scr, p_scr, ds_scr) = own
        kj = pl.program_id(1)
        if ride is not None:
            @pl.when((pl.program_id(0) == 0) & (kj == 0))
            def _():
                rider.start(*ride)

        @pl.when(kj == 0)
        def _():
            dq_ref[...] = jnp.zeros_like(dq_ref)

        dk_ref[...] = jnp.zeros_like(dk_ref)
        dv_ref[...] = jnp.zeros_like(dv_ref)

        def tile(qi, kind):
            rows = pl.ds(pl.multiple_of(qi * t, t), t)
            for a in range(hp):
                qb, dob = q_ref[rows, heads[a]], do_ref[rows, heads[a]]
                kb, vb = k_ref[:, heads[a]], v_ref[:, heads[a]]
                s_scr[a] = lax.dot_general(qb, kb, _NT, preferred_element_type=F32)
                dp_scr[a] = lax.dot_general(dob, vb, _NT, preferred_element_type=F32)
                lse2 = lse_ref[a, rows, :] * log2e
                delta = delta_ref[a, rows, :]
                for c in range(ns):
                    cs = slice(c * STRIP, (c + 1) * STRIP)
                    pc = jnp.exp2(s_scr[a, :, cs] * _EXP2_SCALE - lse2)
                    pc = jnp.where(_strip_mask(kind, kj, c, t), pc, 0.0)
                    p_scr[a, :, cs] = pc.astype(BF16)
                    ds_scr[a, :, cs] = (pc * (dp_scr[a, :, cs] - delta)).astype(BF16)
                dq_ref[rows, heads[a]] += lax.dot_general(ds_scr[a], kb, _NN,
                                                          preferred_element_type=F32) * _SCALE
                dv_ref[:, heads[a]] += lax.dot_general(p_scr[a], dob, _TN, preferred_element_type=F32)
                dk_ref[:, heads[a]] += lax.dot_general(ds_scr[a], qb, _TN, preferred_element_type=F32)

        tile(kj, 'diag')

        def below(qi, carry):
            tile(qi, 'keys')
            return carry

        lax.fori_loop(kj + 1, nb, below, 0)
        dk_ref[...] = dk_ref[...] * _SCALE

        if ride is not None:
            @pl.when((pl.program_id(0) == MLA_HEADS // hp - 1) & (kj == nb - 1))
            def _():
                rider.finish(*ride)

    whole = pl.BlockSpec((lp, wide), lambda g, j: (0, g))
    kspec = pl.BlockSpec((t, wide), lambda g, j: (j, g))
    stat = pl.BlockSpec((hp, lp, 1), lambda g, j: (g, 0, 0))
    r_in, r_out, r_shapes, r_scratch = _rider_args(rider)
    return pl.pallas_call(
        body, name=name,
        out_shape=[jax.ShapeDtypeStruct((lp, MLA_WIDE), F32)] * 3 + r_shapes,
        grid=(MLA_HEADS // hp, nb),
        in_specs=[whole, kspec, kspec, whole, stat, stat] + r_in,
        out_specs=[whole, kspec, kspec] + r_out,
        scratch_shapes=[pltpu.VMEM((hp, t, t), F32), pltpu.VMEM((hp, t, t), F32),
                        pltpu.VMEM((hp, t, t), BF16), pltpu.VMEM((hp, t, t), BF16)] + r_scratch,
        compiler_params=_cparams(dimension_semantics=("arbitrary", "arbitrary")),
    )(q, k, v, do, lse, delta, *(rider.operands if rider else ()))


def _rope_tables(lp):
    inv = 1.0 / (ROPE_THETA ** (jnp.arange(0, MLA_ROPE, 2, dtype=F32) / MLA_ROPE))
    pos = jnp.maximum(jnp.arange(lp, dtype=jnp.int32) - NPAD, 0).astype(F32)
    ang = pos[:, None] * inv[None, :]
    cos, sin = jnp.cos(ang), jnp.sin(ang)
    z32 = jnp.zeros((lp, HEAD_SLOT - MLA_QK), F32)
    cos_t = jnp.concatenate([jnp.ones((lp, MLA_NOPE), F32), cos, cos, z32], axis=1)
    sin_t = jnp.concatenate([jnp.zeros((lp, MLA_NOPE), F32), sin, sin, z32], axis=1)
    return cos_t, sin_t


def _loss_head(h, target, name):
    lp = h.shape[0]

    def body(h_ref, t_ref, d_ref, loss_ref):
        i = pl.program_id(0)

        @pl.when(i == 0)
        def _():
            d_ref[...] = jnp.zeros_like(d_ref)
            loss_ref[...] = jnp.zeros_like(loss_ref)

        @pl.when(i > 0)
        def _():
            err = h_ref[...] - t_ref[...]
            d_ref[...] = err * (1.0 / D_MODEL)
            loss_ref[...] += jnp.sum(err * err, axis=0, keepdims=True) * (0.5 / D_MODEL)

    return pl.pallas_call(
        body, name=name,
        out_shape=[jax.ShapeDtypeStruct((lp, D_MODEL), F32), jax.ShapeDtypeStruct((1, D_MODEL), F32)],
        grid=(lp // CHUNK,),
        in_specs=[pl.BlockSpec((CHUNK, D_MODEL), lambda i: (i, 0)),
                  pl.BlockSpec((CHUNK, D_MODEL), lambda i: (jnp.maximum(i - 1, 0), 0))],
        out_specs=[pl.BlockSpec((CHUNK, D_MODEL), lambda i: (i, 0)),
                   pl.BlockSpec((1, D_MODEL), lambda i: (0, 0))],
        compiler_params=_cparams(dimension_semantics=("arbitrary",)),
    )(h, target)


def _pad_cols(w, n):
    return jnp.pad(w, [(0, 0)] * (w.ndim - 1) + [(0, n - w.shape[-1])])


def _layer_slab(name, i):
    return i if name.startswith('mlp_') else i // 2


def _prep_matrix(key, raw_any):
    def raw(n):
        r = raw_any(n)
        return r[0][r[1]] if isinstance(r, tuple) else r

    if key in ('ssd_out', 'up', 'down'):
        r = raw_any({'ssd_out': 'ssd_w_out', 'up': 'mlp_w_up', 'down': 'mlp_w_down'}[key])
        return (r[0].astype(BF16), r[1]) if isinstance(r, tuple) else r.astype(BF16)
    if key == 'ssd_in':
        return _pad_cols(raw('ssd_w_in'), SSD_IN_PAD).astype(BF16)
    if key == 'mla_in':
        wi = raw('mla_w_in')
        kpe = jnp.pad(wi[:, MLA_Q_RANK + MLA_KV_RANK:], ((0, 0), (MLA_NOPE, HEAD_SLOT - MLA_QK)))
        return jnp.concatenate(
            [wi[:, MLA_Q_RANK:MLA_Q_RANK + MLA_KV_RANK], kpe, wi[:, :MLA_Q_RANK]], axis=1).astype(BF16)
    if key == 'mla_qb':
        qb = raw('mla_w_q_b').reshape(MLA_Q_RANK, MLA_HEADS, MLA_QK)
        return _pad_cols(qb, HEAD_SLOT).reshape(MLA_Q_RANK, MLA_WIDE).astype(BF16)
    if key == 'mla_kvb':
        kvb = raw('mla_w_kv_b').reshape(MLA_KV_RANK, MLA_HEADS, MLA_NOPE + MLA_V)
        kn = _pad_cols(kvb[:, :, :MLA_NOPE], HEAD_SLOT).reshape(MLA_KV_RANK, MLA_WIDE)
        vv = _pad_cols(kvb[:, :, MLA_NOPE:], HEAD_SLOT).reshape(MLA_KV_RANK, MLA_WIDE)
        return jnp.concatenate([kn, vv], axis=1).astype(BF16)
    assert key == 'mla_out'
    wo = raw('mla_w_out').reshape(MLA_HEADS, MLA_V, D_MODEL)
    return jnp.pad(wo, ((0, 0), (0, HEAD_SLOT - MLA_V), (0, 0))).reshape(MLA_WIDE, D_MODEL).astype(BF16)


class _Matrices:
    def __init__(self):
        self.p = {k: _Slabs(k, self) for k in ('ssd_in', 'ssd_out', 'mla_in', 'mla_qb', 'mla_kvb',
                                               'mla_out', 'up', 'down')}
        self.made = {}

    def matrix(self, key, slab):
        if (key, slab) not in self.made:
            self.made[(key, slab)] = _prep_matrix(key, lambda n: self.raw(n, slab))
        return self.made[(key, slab)]


class _Slabs:
    def __init__(self, key, owner):
        self.key, self.owner = key, owner

    def __getitem__(self, slab):
        return self.owner.matrix(self.key, slab)


class _ReadyWeights(_Matrices):
    def __init__(self, w):
        super().__init__()
        self.w = w

    def raw(self, name, slab):
        return self.w[name][slab]

    def start(self):
        pass

    def rider(self, host):
        return None

    def deliver(self, host, outs):
        assert not outs


class _KeepGrads:
    def __init__(self):
        self.rounds = {}

    def begin(self, r, grads):
        self.rounds[r] = grads
        return None

    def middle(self, r, recv):
        return None

    def finish(self, r, outs):
        assert not outs

    def result(self):
        names = {n for g in self.rounds.values() for n in g}
        return {n: jnp.concatenate([self.rounds[r][n] for r in sorted(self.rounds, reverse=True)
                                    if n in self.rounds[r]], axis=0) for n in names}


def _pad128(v):
    return _pad_cols(v.reshape(1, -1), 128)


def _sqrelu(u):
    r = jnp.maximum(u, 0.0)
    return r * r


def _local_step(x, target, w, big=None, red=None):
    seq = x.shape[0]
    lp = NPAD + N_META + seq
    big = _ReadyWeights(w) if big is None else big
    p = big.p
    h = jnp.concatenate([jnp.zeros((NPAD, D_MODEL), F32), w['meta_tokens'], x], axis=0)
    cos_t, sin_t = _rope_tables(lp)
    rt = _pick(lp, (384, 256, 128))
    saved = []
    big.start()
    for i in range(4):
        j = i // 2
        s = {'h0': h}
        g_mix = w['ln_mix'][i].reshape(1, -1)
        g_mlp = w['ln_mlp'][i].reshape(1, -1)
        if i == 0:
            hn = _rms_fwd(h, g_mix, f"rms_mix_f{i}")
        s['hn'] = hn
        if i % 2 == 0:
            rid = big.rider(f"ssd_in_f{i}")
            zxd = _mm(hn, p['ssd_in'][j], 'nn', name=f"ssd_in_f{i}", rider=rid)
            if rid is not None:
                zxd, *got = zxd
                big.deliver(f"ssd_in_f{i}", got)
            consts = _ssd_consts(w['ssd_conv_w'][j], w['ssd_conv_b'][j].reshape(1, -1),
                                 _pad128(w['ssd_dt_bias'][j]), _pad128(w['ssd_a_log'][j]),
                                 _pad128(w['ssd_d'][j]), w['ssd_norm'][j].reshape(1, -1))
            yg, states, *got = _ssd_fwd(zxd, consts, f"ssd_core_f{i}", rider=big.rider(f"ssd_core_f{i}"))
            big.deliver(f"ssd_core_f{i}", got)
            s.update(zxd=zxd, consts=consts, yg=yg, states=states)
            h, hn2 = _mm(yg, p['ssd_out'][j], 'nn', name=f"ssd_out_f{i}", epi=lambda r, hv: hv + r,
                         extras=(h,), norm_gain=g_mlp)
        else:
            lat = _mm(hn, p['mla_in'][j], 'nn', name=f"mla_in_f{i}")
            kvg = w['mla_kv_a_norm'][j].reshape(1, -1)
            qag = w['mla_q_a_norm'][j].reshape(1, -1)
            kvn, qn = _row_call(lambda _, a, b, c, d: _lat_norm(a, b, c, d),
                                [(lat, MLA_KV_RANK, 0), (lat, MLA_Q_RANK, 1)], [kvg, qag],
                                [(MLA_KV_RANK, BF16), (MLA_Q_RANK, BF16)], n_rows=lp, tile=rt,
                                name=f"mla_latnorm_f{i}")
            q_raw = _mm(qn, p['mla_qb'][j], 'nn', name=f"mla_qb_f{i}")
            kv_raw = _mm(kvn, p['mla_kvb'][j], 'nn', name=f"mla_kvb_f{i}")
            qg = _pad_cols(w['mla_q_norm'][j].reshape(1, -1), HEAD_SLOT)
            kg = _pad_cols(w['mla_k_norm'][j].reshape(1, -1), HEAD_SLOT)

            def prep_fwd(_, qr, kn, kpe, vv, cs, sn, qgv, kgv):
                qq, kk = _qk_prep(qr, kn, kpe, cs, sn, qgv, kgv)
                ones = lax.broadcasted_iota(jnp.int32, vv.shape, 1) % HEAD_SLOT == MLA_V
                return qq, kk, jnp.where(ones, 1.0, vv)

            q, k, v = _row_call(prep_fwd,
                                [(q_raw, MLA_WIDE, 0), (kv_raw, MLA_WIDE, 0), (lat, HEAD_SLOT, 2),
                                 (kv_raw, MLA_WIDE, 1), (cos_t, HEAD_SLOT, 0), (sin_t, HEAD_SLOT, 0)],
                                [qg, kg], [(MLA_WIDE, BF16)] * 3, n_rows=lp, tile=rt,
                                name=f"mla_qkprep_f{i}")
            o, lse, *got = _attn_fwd(q, k, v, f"mla_attn_f{i}", rider=big.rider(f"mla_attn_f{i}"))
            big.deliver(f"mla_attn_f{i}", got)
            s.update(lat=lat, kvg=kvg, qag=qag, kvn=kvn, qn=qn, q_raw=q_raw, kv_raw=kv_raw, qg=qg, kg=kg,
                     q=q, k=k, v=v, o=o, lse=lse)
            h, hn2 = _mm(o, p['mla_out'][j], 'nn', name=f"mla_out_f{i}", epi=lambda r, hv: hv + r,
                         extras=(h,), norm_gain=g_mlp)
        s['h1'] = h
        u = _mm(hn2, p['up'][i], 'nn', name=f"mlp_up_f{i}", out_dtype=BF16)
        if i < 3:
            h, hn = _mm(u, p['down'][i], 'nn', name=f"mlp_down_f{i}", a_fn=_sqrelu, epi=lambda r, hv: hv + r,
                        extras=(h,), norm_gain=w['ln_mix'][i + 1].reshape(1, -1))
        else:
            h = _mm(u, p['down'][i], 'nn', name=f"mlp_down_f{i}", a_fn=_sqrelu,
                    epi=lambda r, hv: hv + r, extras=(h,))
        s.update(hn2=hn2, u=u, g_mix=g_mix, g_mlp=g_mlp)
        saved.append(s)

    dh, loss_row = _loss_head(h, target, "loss_head")

    large = {n for n, _ in BIG}
    g = {k_: [None] * (4 if k_ in ('ln_mix', 'ln_mlp') else 2)
         for k_ in ALL_NAMES if k_ != 'meta_tokens' and k_ not in large}
    red = _KeepGrads() if red is None else red
    rounds, pending = {}, None

    def round_of(nm, i):
        return next(r for r, spec in enumerate(REDUCE_ROUNDS)
                    for n, l0, l1 in spec if n == nm and l0 <= _layer_slab(nm, i) < l1)

    def slabs_in(nm, r):
        return next((l0, l1) for n, l0, l1 in REDUCE_ROUNDS[r] if n == nm)

    swapping = None

    def dw_into(nm, i, a, b, **kw):
        nonlocal swapping, pending
        r = round_of(nm, i)
        (l0, l1), cur = slabs_in(nm, r), rounds.setdefault(r, {})
        stack = (l1 - l0, _layer_slab(nm, i) - l0, cur.get(nm))
        if swapping is None:
            cur[nm] = _mm(a, b, 'tn', stack=stack, **kw)
        else:
            (r0, rider), swapping = swapping, None
            cur[nm], *recv = _mm(a, b, 'tn', stack=stack, rider=rider, **kw)
            pending = (r0, red.middle(r0, recv))

    def put(nm, i, arr):
        rounds.setdefault(round_of(nm, i), {})[nm] = arr[None]

    def hand_over(r):
        nonlocal pending, swapping
        swap = red.begin(r, rounds.pop(r))
        if swap is None:
            pending = (r, None)
        elif r == 0:
            swapping = (r, swap)
        else:
            pending = (r, red.middle(r, _run_rider(swap, f"rs_swap{r}")))

    def host(fn, *args):
        nonlocal pending
        if pending is None or pending[1] is None:
            return fn(*args)
        (r, rider), pending = pending, None
        outs = fn(*args, rider=rider)
        own = len(outs) - len(rider.out_shapes)
        red.finish(r, outs[own:])
        return outs[:own]

    for i in reversed(range(4)):
        j = i // 2
        s = saved[i]
        dw_into('mlp_w_down', i, s['u'], dh, name=f"mlp_down_dw{i}", a_fn=_sqrelu)
        du = _mm(dh, p['down'][i], 'nt', name=f"mlp_down_dx{i}", out_dtype=BF16,
                 epi=lambda r, uv: r * (2.0 * jnp.maximum(uv, 0.0)), extras=(s['u'],))
        dw_into('mlp_w_up', i, s['hn2'], du, name=f"mlp_up_dw{i}")
        dh, dg = _mm_rms_bwd(du, p['up'][i], s['h1'], dh, s['g_mlp'], f"mlp_up_dx{i}")
        g['ln_mlp'][i] = dg[0]
        if i % 2 == 0:
            dw_into('ssd_w_out', i, s['yg'], dh, name=f"ssd_out_dw{i}")
            d_yg = _mm(dh, p['ssd_out'][j], 'nt', name=f"ssd_out_dx{i}")
            if i == 0:
                hand_over(1)
            d_zxd, dcw, dcb, ddtb, dalog, ddsk, dng = host(_ssd_bwd, s['zxd'], s['states'], d_yg, s['consts'],
                                                           f"ssd_core_b{i}")
            g['ssd_conv_w'][j], g['ssd_conv_b'][j], g['ssd_norm'][j] = dcw, dcb[0], dng[0]
            g['ssd_dt_bias'][j], g['ssd_a_log'][j], g['ssd_d'][j] = (
                ddtb[0, :SSD_HEADS], dalog[0, :SSD_HEADS], ddsk[0, :SSD_HEADS])
            dw_into('ssd_w_in', i, s['hn'], d_zxd, name=f"ssd_in_dw{i}")
            dh, dg = _mm_rms_bwd(d_zxd, p['ssd_in'][j], s['h0'], dh, s['g_mix'], f"ssd_in_dx{i}")
        else:
            wo = _mm(s['o'], dh, 'tn', name=f"mla_out_dw{i}")
            put('mla_w_out', i, wo.reshape(MLA_HEADS, HEAD_SLOT, D_MODEL)[:, :MLA_V].reshape(-1, D_MODEL))
            dob, delta = _mm_attn_do(dh, p['mla_out'][j], s['o'], f"mla_out_dx{i}")
            dq, dk, dv = host(_attn_bwd, s['q'], s['k'], s['v'], dob, s['lse'], delta, f"mla_attn_b{i}")

            def prep_bwd(_, qr, kn, kpe, cs, sn, dqv, dkv, dvv, qgv, kgv):
                _, vjp = jax.vjp(lambda a, b, c, d, e: _qk_prep(a, b, c, cs, sn, d, e), qr, kn, kpe, qgv, kgv)
                d_qr, d_kn, d_kpe, d_qg, d_kg = vjp((dqv, dkv))
                return d_qr, jnp.concatenate([d_kn, dvv], axis=1), d_kpe, d_qg, d_kg

            d_qraw, d_kvraw, d_kpe, d_qg, d_kg = _row_call(
                prep_bwd,
                [(s['q_raw'], MLA_WIDE, 0), (s['kv_raw'], MLA_WIDE, 0), (s['lat'], HEAD_SLOT, 2),
                 (cos_t, HEAD_SLOT, 0), (sin_t, HEAD_SLOT, 0), (dq, MLA_WIDE, 0), (dk, MLA_WIDE, 0),
                 (dv, MLA_WIDE, 0)],
                [s['qg'], s['kg']], [(MLA_WIDE, BF16), (2 * MLA_WIDE, BF16), (HEAD_SLOT, F32)],
                [(1, HEAD_SLOT), (1, HEAD_SLOT)], n_rows=lp, tile=_pick(lp, (128,)), name=f"mla_qkprep_b{i}")
            g['mla_q_norm'][j], g['mla_k_norm'][j] = d_qg[0, :MLA_QK], d_kg[0, :MLA_QK]
            wqb = _mm(s['qn'], d_qraw, 'tn', name=f"mla_qb_dw{i}")
            put('mla_w_q_b', i, wqb.reshape(MLA_Q_RANK, MLA_HEADS, HEAD_SLOT)[:, :, :MLA_QK].reshape(MLA_Q_RANK, -1))
            d_qn = _mm(d_qraw, p['mla_qb'][j], 'nt', name=f"mla_qb_dx{i}")
            wkvb = _mm(s['kvn'], d_kvraw, 'tn', name=f"mla_kvb_dw{i}").reshape(MLA_KV_RANK, 2, MLA_HEADS, HEAD_SLOT)
            put('mla_w_kv_b', i, jnp.concatenate([wkvb[:, 0, :, :MLA_NOPE], wkvb[:, 1, :, :MLA_V]],
                                                 axis=-1).reshape(MLA_KV_RANK, -1))
            d_kvn = _mm(d_kvraw, p['mla_kvb'][j], 'nt', name=f"mla_kvb_dx{i}")

            def lat_bwd(_, kvl, ql, dkvn, dqn, dkpe, kvgv, qagv):
                _, vjp = jax.vjp(_lat_norm, kvl, ql, kvgv, qagv)
                d_kvl, d_ql, d_kvg, d_qag = vjp((dkvn, dqn))
                return jnp.concatenate([d_kvl, dkpe, d_ql], axis=1), d_kvg, d_qag

            d_lat, d_kvg, d_qag = _row_call(
                lat_bwd, [(s['lat'], MLA_KV_RANK, 0), (s['lat'], MLA_Q_RANK, 1), (d_kvn, MLA_KV_RANK, 0),
                          (d_qn, MLA_Q_RANK, 0), (d_kpe, HEAD_SLOT, 0)],
                [s['kvg'], s['qag']], [(LAT_PAD, BF16)], [(1, MLA_KV_RANK), (1, MLA_Q_RANK)],
                n_rows=lp, tile=rt, name=f"mla_latnorm_b{i}")
            g['mla_kv_a_norm'][j], g['mla_q_a_norm'][j] = d_kvg[0], d_qag[0]
            win = _mm(s['hn'], d_lat, 'tn', name=f"mla_in_dw{i}")
            put('mla_w_in', i, jnp.concatenate(
                [win[:, MLA_KV_RANK + HEAD_SLOT:], win[:, :MLA_KV_RANK],
                 win[:, MLA_KV_RANK + MLA_NOPE:MLA_KV_RANK + MLA_QK]], axis=1))
            dh, dg = _mm_rms_bwd(d_lat, p['mla_in'][j], s['h0'], dh, s['g_mix'], f"mla_in_dx{i}")
        g['ln_mix'][i] = dg[0]
        if i == 2:
            hand_over(0)
    hand_over(2)

    if pending[1] is not None:
        red.finish(pending[0], _run_rider(pending[1], "rs_exchange_last"))
    grads = {k_: jnp.stack(v_) for k_, v_ in g.items()}
    grads['meta_tokens'] = dh[NPAD:NPAD + N_META]
    return loss_row, dh[NPAD + N_META:], grads, red


def _all_gather8(shard, name):
    m_per, n = shard.shape

    def body(x_ref, out_ref, send_sems, recv_sems, local_sem):
        x, y, c = lax.axis_index("x"), lax.axis_index("y"), lax.axis_index("c")
        me, sibling = (x, y, c), (x, y, 1 - c)
        chips = [(1 - x, y), (x, 1 - y), (1 - x, 1 - y)]

        def rows(px, py, pc):
            return out_ref.at[pl.ds((4 * px + 2 * py + pc) * m_per, m_per), :]

        def copy(k, block, to, src=None):
            return pltpu.make_async_remote_copy(
                src_ref=rows(*block) if src is None else src, dst_ref=rows(*block),
                send_sem=send_sems.at[k], recv_sem=recv_sems.at[k], device_id=to, device_id_type=MESH)

        mine = pltpu.make_async_copy(x_ref, rows(*me), local_sem)
        mine.start()
        first = [copy(0, me, sibling, src=x_ref)]
        first += [copy(1 + j, me, (*chip, c), src=x_ref) for j, chip in enumerate(chips)]
        for cp in first:
            cp.start()
        passed = [copy(4 + j, (*chip, c), sibling) for j, chip in enumerate(chips)]
        for j, chip in enumerate(chips):
            copy(1 + j, (*chip, c), me).wait_recv()
            passed[j].start()
        copy(0, sibling, me).wait_recv()
        for j, chip in enumerate(chips):
            copy(4 + j, (*chip, 1 - c), me).wait_recv()
        for cp in first + passed:
            cp.wait_send()
        mine.wait()

    return pl.pallas_call(
        body, name=name,
        out_shape=jax.ShapeDtypeStruct((8 * m_per, n), shard.dtype),
        in_specs=[pl.BlockSpec(memory_space=pl.ANY)],
        out_specs=pl.BlockSpec(memory_space=pl.ANY),
        scratch_shapes=[pltpu.SemaphoreType.DMA((7,)), pltpu.SemaphoreType.DMA((7,)), pltpu.SemaphoreType.DMA],
    )(shard)


def _mesh_pos():
    return lax.axis_index("x"), lax.axis_index("y"), lax.axis_index("c")


def _half_rows(pc, h):
    return pl.ds(pl.multiple_of(pc * h, 16), h)


def _whole_view(ref, kind, shard_shape, k, pc):
    _, r, c = shard_shape
    rows = _half_rows(pc, r // 2)
    if kind == 'row':
        return ref.at[:, k, rows, :]
    if kind == 'col':
        return ref.at[:, rows, pl.ds(pl.multiple_of(k * c, 128), c)]
    return ref.at[k, :, rows, :]


def _whole_shape(kind, shard_shape, rows=None):
    l, r, c = shard_shape
    r = r if rows is None else rows
    return {'row': (l, 4, r, c), 'col': (l, r, 4 * c), 'colx': (4, l, r, c)}[kind]


def _gather_rider(shards, kinds):
    n = len(shards)
    shapes = [s.shape for s in shards]

    def plan(ins, outs, sems):
        send_sems, recv_sems, local_sems = sems
        x, y, c = _mesh_pos()
        me, sibling = (x, y, c), (x, y, 1 - c)
        chips = [(1 - x, y), (x, 1 - y), (1 - x, 1 - y)]

        def place(a, px, py, pc):
            return _whole_view(outs[a], kinds[a], shapes[a], 2 * px + py, pc)

        def own(a):
            return ins[a].at[:, _half_rows(c, shapes[a][1] // 2), :]

        def copy(a, k, block, to, src=None):
            return pltpu.make_async_remote_copy(
                src_ref=place(a, *block) if src is None else src, dst_ref=place(a, *block),
                send_sem=send_sems.at[7 * a + k], recv_sem=recv_sems.at[7 * a + k],
                device_id=to, device_id_type=MESH)

        mine = [pltpu.make_async_copy(own(a), place(a, *me), local_sems.at[a]) for a in range(n)]
        first = [copy(a, 1 + j, me, (*chip, c), src=own(a)) for j, chip in enumerate(chips) for a in range(n)]
        first += [copy(a, 0, me, sibling, src=own(a)) for a in range(n)]
        return copy, mine, first, chips, me, sibling, c

    def start(ins, outs, sems):
        _, mine, first, *_ = plan(ins, outs, sems)
        for cp in first + mine:
            cp.start()

    def finish(ins, outs, sems):
        copy, mine, first, chips, me, sibling, c = plan(ins, outs, sems)
        passed = []
        for j, chip in enumerate(chips):
            for a in range(n):
                copy(a, 1 + j, (*chip, c), me).wait_recv()
                passed.append(copy(a, 4 + j, (*chip, c), sibling))
                passed[-1].start()
        for a in range(n):
            copy(a, 0, sibling, me).wait_recv()
        for j, chip in enumerate(chips):
            for a in range(n):
                copy(a, 4 + j, (*chip, 1 - c), me).wait_recv()
        for cp in first + passed:
            cp.wait_send()
        for cp in mine:
            cp.wait()

    return _Rider(
        shards, [jax.ShapeDtypeStruct(_whole_shape(k, s.shape), s.dtype) for k, s in zip(kinds, shards)],
        [pltpu.SemaphoreType.DMA((7 * n,)), pltpu.SemaphoreType.DMA((7 * n,)), pltpu.SemaphoreType.DMA((n,))],
        start, finish)


def _run_rider(rider, name):
    ni, no = len(rider.operands), len(rider.out_shapes)

    def body(*refs):
        ride = (refs[:ni], refs[ni:ni + no], refs[ni + no:])
        rider.start(*ride)
        rider.finish(*ride)

    return pl.pallas_call(
        body, name=name, out_shape=rider.out_shapes,
        in_specs=[pl.BlockSpec(memory_space=pl.ANY)] * ni,
        out_specs=[pl.BlockSpec(memory_space=pl.ANY)] * no,
        scratch_shapes=rider.scratch,
    )(*rider.operands)


def _swap_rider(wholes, kinds, shapes):
    n = len(wholes)

    def plan(ins, outs, sems):
        send_sems, recv_sems = sems
        x, y, c = _mesh_pos()
        cps = []
        for a in range(n):
            rows = _half_rows(1 - c, shapes[a][1] // 2)
            src = ins[a].at[:, rows, :] if kinds[a] == 'col' else ins[a].at[:, :, rows, :]
            cps.append(pltpu.make_async_remote_copy(
                src_ref=src, dst_ref=outs[a], send_sem=send_sems.at[a], recv_sem=recv_sems.at[a],
                device_id=(x, y, 1 - c), device_id_type=MESH))
        return cps

    def start(ins, outs, sems):
        for cp in plan(ins, outs, sems):
            cp.start()

    def finish(ins, outs, sems):
        for cp in plan(ins, outs, sems):
            cp.wait()

    return _Rider(
        wholes, [jax.ShapeDtypeStruct(_whole_shape(k, s, s[1] // 2), w.dtype)
                 for k, s, w in zip(kinds, shapes, wholes)],
        [pltpu.SemaphoreType.DMA((n,)), pltpu.SemaphoreType.DMA((n,))], start, finish)


def _exchange_rider(parts, kinds, shapes):
    n = len(parts)

    def plan(ins, outs, sems):
        send_sems, recv_sems, local_sems = sems
        x, y, c = _mesh_pos()
        kme = 2 * x + y
        chips = [(1 - x, y), (x, 1 - y), (1 - x, 1 - y)]

        def slab(a, k):
            if kinds[a] == 'row':
                return ins[a].at[:, k]
            if kinds[a] == 'col':
                cw = shapes[a][2]
                return ins[a].at[:, :, pl.ds(pl.multiple_of(k * cw, 128), cw)]
            return ins[a].at[k]

        cps = [pltpu.make_async_remote_copy(
            src_ref=slab(a, 2 * px + py), dst_ref=outs[a].at[kme], send_sem=send_sems.at[3 * a + j],
            recv_sem=recv_sems.at[3 * a + j], device_id=(px, py, c), device_id_type=MESH)
            for j, (px, py) in enumerate(chips) for a in range(n)]
        return cps + [pltpu.make_async_copy(slab(a, kme), outs[a].at[kme], local_sems.at[a]) for a in range(n)]

    def start(ins, outs, sems):
        for cp in plan(ins, outs, sems):
            cp.start()

    def finish(ins, outs, sems):
        for cp in plan(ins, outs, sems):
            cp.wait()

    return _Rider(
        parts, [jax.ShapeDtypeStruct((4, s[0], s[1] // 2, s[2]), p.dtype) for s, p in zip(shapes, parts)],
        [pltpu.SemaphoreType.DMA((3 * n,)), pltpu.SemaphoreType.DMA((3 * n,)), pltpu.SemaphoreType.DMA((n,))],
        start, finish)


def _rs_share(shards, slabs, name):
    n = len(shards)

    def body(*refs):
        outs = refs[n:2 * n]
        send_sems, recv_sems = refs[2 * n:]
        x, y, c = _mesh_pos()
        cps = []
        for a in range(n):
            l0, l1 = slabs[a]
            rows = outs[a].at[pl.ds(l0, l1 - l0), _half_rows(c, shards[a].shape[1] // 2), :]
            cps.append(pltpu.make_async_remote_copy(
                src_ref=rows, dst_ref=rows, send_sem=send_sems.at[a], recv_sem=recv_sems.at[a],
                device_id=(x, y, 1 - c), device_id_type=MESH))
        for cp in cps:
            cp.start()
        for cp in cps:
            cp.wait()

    return pl.pallas_call(
        body, name=name,
        out_shape=[jax.ShapeDtypeStruct(s.shape, s.dtype) for s in shards],
        in_specs=[pl.BlockSpec(memory_space=pl.ANY)] * n,
        out_specs=[pl.BlockSpec(memory_space=pl.ANY)] * n,
        input_output_aliases={a: a for a in range(n)},
        scratch_shapes=[pltpu.SemaphoreType.DMA((n,)), pltpu.SemaphoreType.DMA((n,))],
    )(*shards)


def _tile_rows(rows, cols, budget=2 * 1024 * 1024):
    for t in (1024, 512, 256, 128, 64, 32, 16, 8):
        if rows % t == 0 and t * cols * 4 <= budget:
            return t
    return rows


def _add_half(g3, r3, c_idx, name):
    a, h, n = r3.shape
    t = _tile_rows(h, n)
    nt = h // t

    def body(c_ref, g_ref, r_ref, o_ref):
        o_ref[...] = (g_ref[...] + r_ref[...]).astype(o_ref.dtype)

    return pl.pallas_call(
        body, name=name, out_shape=jax.ShapeDtypeStruct((a, h, n), BF16),
        grid_spec=pltpu.PrefetchScalarGridSpec(
            num_scalar_prefetch=1, grid=(a, nt),
            in_specs=[pl.BlockSpec((1, t, n), lambda k, i, c: (k, c[0] * nt + i, 0)),
                      pl.BlockSpec((1, t, n), lambda k, i, c: (k, i, 0))],
            out_specs=pl.BlockSpec((1, t, n), lambda k, i, c: (k, i, 0))),
        compiler_params=_cparams(dimension_semantics=("parallel", "parallel")),
    )(c_idx, g3, r3)


def _sum4(parts, c_idx, name, into):
    _, l, h, n = parts.shape
    n_slabs, l0, buf = into
    t = _tile_rows(h, n, 1024 * 1024)
    nt = h // t
    held = () if buf is None else (buf,)

    def body(c_ref, p_ref, *rest):
        pv = p_ref[...].astype(F32)
        rest[-1][...] = ((pv[0] + pv[1]) + pv[2]) + pv[3]

    return pl.pallas_call(
        body, name=name, out_shape=jax.ShapeDtypeStruct((n_slabs, 2 * h, n), F32),
        grid_spec=pltpu.PrefetchScalarGridSpec(
            num_scalar_prefetch=1, grid=(l, nt),
            in_specs=[pl.BlockSpec((4, 1, t, n), lambda k, i, c: (0, k, i, 0))]
            + [pl.BlockSpec(memory_space=pl.ANY)] * len(held),
            out_specs=pl.BlockSpec((1, t, n), lambda k, i, c: (l0 + k, c[0] * nt + i, 0))),
        input_output_aliases={2: 0} if held else {},
        compiler_params=_cparams(dimension_semantics=("parallel", "parallel")),
    )(c_idx, parts, *held)


def _sum8(parts, name):
    _, m, n = parts.shape

    def body(p_ref, o_ref):
        acc = p_ref[0]
        for d in range(1, 8):
            acc = acc + p_ref[d]
        o_ref[...] = acc

    return pl.pallas_call(body, name=name, out_shape=jax.ShapeDtypeStruct((m, n), F32))(parts)


def _adamw(wp, gp, mp, vp, name):
    r, n = wp.shape
    t = _tile_rows(r, n, 1024 * 1024)

    def body(w_ref, g_ref, m_ref, v_ref, d_ref, mo_ref, vo_ref):
        gv = g_ref[...]
        m2 = ADAM_B1 * m_ref[...] + (1.0 - ADAM_B1) * gv
        v2 = ADAM_B2 * v_ref[...] + (1.0 - ADAM_B2) * (gv * gv)
        m_hat = m2 / (1.0 - ADAM_B1 ** ADAM_STEP)
        v_hat = v2 / (1.0 - ADAM_B2 ** ADAM_STEP)
        d_ref[...] = -ADAM_LR * (m_hat / (jnp.sqrt(v_hat) + ADAM_EPS) + ADAM_WD * w_ref[...])
        mo_ref[...] = m2
        vo_ref[...] = v2

    spec = pl.BlockSpec((t, n), lambda i: (i, 0))
    return pl.pallas_call(
        body, name=name, out_shape=[jax.ShapeDtypeStruct((r, n), F32)] * 3, grid=(r // t,),
        in_specs=[spec] * 4, out_specs=[spec] * 3,
        compiler_params=_cparams(dimension_semantics=("parallel",)),
    )(wp, gp, mp, vp)


BIG = (('ssd_w_in', 'colx'), ('ssd_w_out', 'row'), ('mla_w_in', 'row'), ('mla_w_q_b', 'col'),
       ('mla_w_kv_b', 'col'), ('mla_w_out', 'row'), ('mlp_w_up', 'col'), ('mlp_w_down', 'row'))
SMALL_SHARDED = (('meta_tokens', 1), ('ssd_conv_w', 2), ('mla_q_a_norm', 1), ('mla_kv_a_norm', 1))
SMALL_REPL = ('ln_mix', 'ln_mlp', 'ssd_conv_b', 'ssd_dt_bias', 'ssd_a_log', 'ssd_d', 'ssd_norm',
              'mla_q_norm', 'mla_k_norm')
ALL_NAMES = ('meta_tokens', 'ln_mix', 'ln_mlp', 'ssd_w_in', 'ssd_conv_w', 'ssd_conv_b', 'ssd_dt_bias',
             'ssd_a_log', 'ssd_d', 'ssd_norm', 'ssd_w_out', 'mla_w_in', 'mla_q_a_norm', 'mla_w_q_b',
             'mla_kv_a_norm', 'mla_w_kv_b', 'mla_q_norm', 'mla_k_norm', 'mla_w_out', 'mlp_w_up', 'mlp_w_down')


_MLA_BIG = ('mla_w_in', 'mla_w_q_b', 'mla_w_kv_b', 'mla_w_out')
GATHER_ROUNDS = (
    (('ssd_w_in', 0, 1),),
    (('ssd_w_out', 0, 1), ('mlp_w_up', 0, 1)),
    (('mlp_w_down', 0, 1),) + tuple((n, 0, 1) for n in _MLA_BIG) + (('mlp_w_up', 1, 2), ('mlp_w_down', 1, 2)),
    (('ssd_w_in', 1, 2), ('ssd_w_out', 1, 2)) + tuple((n, 1, 2) for n in _MLA_BIG)
    + (('mlp_w_up', 2, 4), ('mlp_w_down', 2, 4)),
)
GATHER_HOSTS = {'ssd_in_f0': 1, 'ssd_core_f0': 2, 'mla_attn_f1': 3}


REDUCE_ROUNDS = (
    GATHER_ROUNDS[3],
    tuple((n, 0, 1) for n in _MLA_BIG) + (('mlp_w_up', 0, 2), ('mlp_w_down', 0, 2), ('ssd_w_out', 0, 1)),
    (('ssd_w_in', 0, 1),),
)


class _GatheredWeights(_Matrices):
    def __init__(self, shards):
        super().__init__()
        self.shards, self.whole = shards, {}

    def raw(self, name, slab):
        return self.whole[(name, slab)]

    def _round(self, r):
        spec = GATHER_ROUNDS[r]
        return _gather_rider([self.shards[n][l0:l1].astype(BF16) for n, l0, l1 in spec],
                             [dict(BIG)[n] for n, _, _ in spec])

    def _take(self, r, outs):
        for (n, l0, l1), o in zip(GATHER_ROUNDS[r], outs):
            kind = dict(BIG)[n]
            if kind == 'row':
                o = o.reshape(o.shape[0], -1, o.shape[-1])
            for l in range(l0, l1):
                if kind == 'colx':
                    self.whole[(n, l)] = jnp.concatenate([o[k, l - l0] for k in range(4)], axis=-1)
                else:
                    self.whole[(n, l)] = (o, l - l0)

    def start(self):
        self._take(0, _run_rider(self._round(0), "gather_first"))

    def rider(self, host):
        return self._round(GATHER_HOSTS[host]) if host in GATHER_HOSTS else None

    def deliver(self, host, outs):
        if host in GATHER_HOSTS:
            self._take(GATHER_HOSTS[host], outs)


class _ScatterGrads:
    def __init__(self, shard_shapes, c_idx):
        self.shard_shapes, self.c_idx, self.out = shard_shapes, c_idx, {}

    def begin(self, r, grads):
        spec = REDUCE_ROUNDS[r]
        kinds = [dict(BIG)[n] for n, _, _ in spec]
        shapes = [(l1 - l0,) + tuple(self.shard_shapes[n][1:]) for n, l0, l1 in spec]
        wholes = []
        for (n, _, _), kind, s in zip(spec, kinds, shapes):
            if kind == 'row':
                wholes.append(grads[n].reshape(s[0], 4, s[1], s[2]))
            elif kind == 'col':
                wholes.append(grads[n])
            else:
                wholes.append(jnp.stack([grads[n][..., k * s[2]:(k + 1) * s[2]] for k in range(4)]))
        self.swapping = (kinds, shapes, wholes)
        return _swap_rider(wholes, kinds, shapes)

    def middle(self, r, recv):
        spec = REDUCE_ROUNDS[r]
        kinds, shapes, wholes = self.swapping
        parts = []
        for (n, _, _), kind, s, gw, rc in zip(spec, kinds, shapes, wholes, recv):
            if kind == 'col':
                g3, r3 = gw, rc
            else:
                g3, r3 = gw.reshape(-1, s[1], s[2]), rc.reshape(-1, s[1] // 2, s[2])
            parts.append(_add_half(g3, r3, self.c_idx, f"rs_add{r}_{n}").reshape(rc.shape))
        return _exchange_rider(parts, kinds, shapes)

    def finish(self, r, outs):
        spec = REDUCE_ROUNDS[r]
        for (n, l0, _), part in zip(spec, outs):
            self.out[n] = _sum4(part, self.c_idx, f"rs_sum{r}_{n}",
                                into=(self.shard_shapes[n][0], l0, self.out.get(n)))
        shared = _rs_share([self.out[n] for n, _, _ in spec], [(l0, l1) for _, l0, l1 in spec], f"rs_share{r}")
        self.out.update(zip([n for n, _, _ in spec], shared))


def _pack(arrs, rows_mult):
    flat = jnp.concatenate([a.reshape(-1) for a in arrs])
    per = LANES * rows_mult
    pad = (-flat.shape[0]) % per
    if pad:
        flat = jnp.concatenate([flat, jnp.zeros((pad,), flat.dtype)])
    return flat.reshape(-1, LANES)


def _unpack(pack, shapes):
    flat = pack.reshape(-1)
    out, off = [], 0
    for shp in shapes:
        n = math.prod(shp)
        out.append(flat[off:off + n].reshape(shp))
        off += n
    return out


def _split4(full, axis):
    shp = full.shape
    r = full.reshape(shp[:axis] + (4, shp[axis] // 4) + shp[axis + 1:])
    return jnp.moveaxis(r, axis, 0)


def _join4(parts, axis):
    r = jnp.moveaxis(parts, 0, axis)
    shp = r.shape
    return r.reshape(shp[:axis] + (shp[axis] * shp[axis + 1],) + shp[axis + 2:])


def _gather_params(shards, table, dtype, c, name):
    pack = _pack([shards[n].astype(dtype) for n, _ in table], 16)
    half = pack.shape[0] // 2
    mine = lax.dynamic_slice_in_dim(pack, c * half, half, axis=0)
    full = _all_gather8(mine, name).reshape(4, -1)
    out, off = {}, 0
    for n, ax in table:
        cnt = math.prod(shards[n].shape)
        out[n] = _join4(full[:, off:off + cnt].reshape((4,) + shards[n].shape), ax)
        off += cnt
    return out


def kernel(x, meta_tokens, ln_mix, ln_mlp, ssd_w_in, ssd_conv_w, ssd_conv_b, ssd_dt_bias, ssd_a_log, ssd_d, ssd_norm, ssd_w_out, mla_w_in, mla_q_a_norm, mla_w_q_b, mla_kv_a_norm, mla_w_kv_b, mla_q_norm, mla_k_norm, mla_w_out, mlp_w_up, mlp_w_down, loss_target, m_meta_tokens, m_ln_mix, m_ln_mlp, m_ssd_w_in, m_ssd_conv_w, m_ssd_conv_b, m_ssd_dt_bias, m_ssd_a_log, m_ssd_d, m_ssd_norm, m_ssd_w_out, m_mla_w_in, m_mla_q_a_norm, m_mla_w_q_b, m_mla_kv_a_norm, m_mla_w_kv_b, m_mla_q_norm, m_mla_k_norm, m_mla_w_out, m_mlp_w_up, m_mlp_w_down, v_meta_tokens, v_ln_mix, v_ln_mlp, v_ssd_w_in, v_ssd_conv_w, v_ssd_conv_b, v_ssd_dt_bias, v_ssd_a_log, v_ssd_d, v_ssd_norm, v_ssd_w_out, v_mla_w_in, v_mla_q_a_norm, v_mla_w_q_b, v_mla_kv_a_norm, v_mla_w_kv_b, v_mla_q_norm, v_mla_k_norm, v_mla_w_out, v_mlp_w_up, v_mlp_w_down):
    w_sh = dict(meta_tokens=meta_tokens, ln_mix=ln_mix, ln_mlp=ln_mlp, ssd_w_in=ssd_w_in, ssd_conv_w=ssd_conv_w, ssd_conv_b=ssd_conv_b, ssd_dt_bias=ssd_dt_bias, ssd_a_log=ssd_a_log, ssd_d=ssd_d, ssd_norm=ssd_norm, ssd_w_out=ssd_w_out, mla_w_in=mla_w_in, mla_q_a_norm=mla_q_a_norm, mla_w_q_b=mla_w_q_b, mla_kv_a_norm=mla_kv_a_norm, mla_w_kv_b=mla_w_kv_b, mla_q_norm=mla_q_norm, mla_k_norm=mla_k_norm, mla_w_out=mla_w_out, mlp_w_up=mlp_w_up, mlp_w_down=mlp_w_down)
    m_sh = dict(meta_tokens=m_meta_tokens, ln_mix=m_ln_mix, ln_mlp=m_ln_mlp, ssd_w_in=m_ssd_w_in, ssd_conv_w=m_ssd_conv_w, ssd_conv_b=m_ssd_conv_b, ssd_dt_bias=m_ssd_dt_bias, ssd_a_log=m_ssd_a_log, ssd_d=m_ssd_d, ssd_norm=m_ssd_norm, ssd_w_out=m_ssd_w_out, mla_w_in=m_mla_w_in, mla_q_a_norm=m_mla_q_a_norm, mla_w_q_b=m_mla_w_q_b, mla_kv_a_norm=m_mla_kv_a_norm, mla_w_kv_b=m_mla_w_kv_b, mla_q_norm=m_mla_q_norm, mla_k_norm=m_mla_k_norm, mla_w_out=m_mla_w_out, mlp_w_up=m_mlp_w_up, mlp_w_down=m_mlp_w_down)
    v_sh = dict(meta_tokens=v_meta_tokens, ln_mix=v_ln_mix, ln_mlp=v_ln_mlp, ssd_w_in=v_ssd_w_in, ssd_conv_w=v_ssd_conv_w, ssd_conv_b=v_ssd_conv_b, ssd_dt_bias=v_ssd_dt_bias, ssd_a_log=v_ssd_a_log, ssd_d=v_ssd_d, ssd_norm=v_ssd_norm, ssd_w_out=v_ssd_w_out, mla_w_in=v_mla_w_in, mla_q_a_norm=v_mla_q_a_norm, mla_w_q_b=v_mla_w_q_b, mla_kv_a_norm=v_mla_kv_a_norm, mla_w_kv_b=v_mla_w_kv_b, mla_q_norm=v_mla_q_norm, mla_k_norm=v_mla_k_norm, mla_w_out=v_mla_w_out, mlp_w_up=v_mlp_w_up, mlp_w_down=v_mlp_w_down)

    cx, cy, cc = lax.axis_index("x"), lax.axis_index("y"), lax.axis_index("c")
    chip = 2 * cx + cy

    c_idx = cc.reshape(1).astype(jnp.int32)
    big_names = [n for n, _ in BIG]
    shapes = [w_sh[n].shape for n in big_names]

    w = {n: w_sh[n] for n in SMALL_REPL}
    w.update(_gather_params(w_sh, SMALL_SHARDED, F32, cc, "gather_small"))
    big = _GatheredWeights({n: w_sh[n] for n in big_names})
    red = _ScatterGrads({n: w_sh[n].shape for n in big_names}, c_idx)

    loss_row, grad_x, grads, red = _local_step(x[0], loss_target[0], w, big, red)
    loss = lax.psum(jnp.sum(loss_row), ("x", "y", "c"))
    g_sh = dict(red.out)

    small_names = tuple(n for n, _ in SMALL_SHARDED) + SMALL_REPL
    sp = _pack([grads[n] for n in small_names], 8)
    srows = sp.shape[0]
    s_all = _sum8(_all_gather8(sp, "ar_small_gather").reshape(8, srows, LANES), "ar_small_sum")
    s_full = dict(zip(small_names, _unpack(s_all, [grads[n].shape for n in small_names])))
    for n, ax in SMALL_SHARDED:
        g_sh[n] = lax.dynamic_index_in_dim(_split4(s_full[n], ax), chip, axis=0, keepdims=False)
    for n in SMALL_REPL:
        g_sh[n] = s_full[n]

    delta, new_m, new_v = {}, {}, {}
    for n, s in zip(big_names, shapes):
        res = _adamw(*[t[n].reshape(-1, s[2]) for t in (w_sh, g_sh, m_sh, v_sh)], f"adamw_{n}")
        delta[n], new_m[n], new_v[n] = [r.reshape(s) for r in res]
    d_s, m_s, v_s = _adamw(*[_pack([t[n] for n in small_names], 8) for t in (w_sh, g_sh, m_sh, v_sh)],
                           "adamw_small")
    for dst, ps in ((delta, d_s), (new_m, m_s), (new_v, v_s)):
        dst.update(zip(small_names, _unpack(ps, [w_sh[n].shape for n in small_names])))

    return (loss, grad_x[None], *[g_sh[n] for n in ALL_NAMES], *[delta[n] for n in ALL_NAMES],
            *[new_m[n] for n in ALL_NAMES], *[new_v[n] for n in ALL_NAMES])
```

```python
import functools
import math

import jax
import jax.numpy as jnp
from jax import lax
from jax.experimental import pallas as pl
from jax.experimental.pallas import tpu as pltpu

F32 = jnp.float32
BF16 = jnp.bfloat16
MESH = pl.DeviceIdType.MESH
_NN = (((1,), (0,)), ((), ()))
_NT = (((1,), (1,)), ((), ()))
_TN = (((0,), (0,)), ((), ()))

D_MODEL = 1024
N_META = 16
EPS = 1e-6
SSD_D_INNER = 2048
SSD_HEADS = 32
SSD_HEAD_DIM = 64
SSD_GROUPS = 8
SSD_HPG = 4
SSD_STATE = 128
SSD_CONV = 4
CHUNK = 128
SSD_IN_DIM = 6176
SSD_IN_PAD = 6272
MLA_HEADS = 16
MLA_NOPE = 64
MLA_ROPE = 32
MLA_V = 64
MLA_QK = 96
MLA_Q_RANK = 384
MLA_KV_RANK = 256
HEAD_SLOT = 128
MLA_WIDE = MLA_HEADS * HEAD_SLOT
HEADS_PER_STEP = 2
LAT_PAD = 768
ROPE_THETA = 10000.0
D_FF = 4096
NPAD = CHUNK - N_META
ADAM_LR, ADAM_B1, ADAM_B2, ADAM_EPS, ADAM_WD, ADAM_STEP = 0.001, 0.9, 0.999, 1e-08, 0.01, 10
LANES = 1024
V7X_VMEM_BYTES = 64 * 1024 * 1024
VMEM_LIMIT = V7X_VMEM_BYTES * 7 // 8


def _pick(n, cands):
    for c in cands:
        if n % c == 0:
            return c
    return n


def _cparams(**kw):
    return pltpu.CompilerParams(vmem_limit_bytes=VMEM_LIMIT, **kw)


def _slab_spec(spec, slab):
    return pl.BlockSpec((None,) + tuple(spec.block_shape), lambda *ids: (slab,) + tuple(spec.index_map(*ids)))


def _mm(a, b, dims, *, name, out_dtype=F32, a_fn=None, epi=None, extras=(), stack=None, norm_gain=None,
        rider=None):
    b, b_slab = b if isinstance(b, tuple) else (b, None)
    b_shape = b.shape if b_slab is None else b.shape[1:]
    if dims == 'nn':
        (M, K), (K2, N) = a.shape, b_shape
    elif dims == 'nt':
        (M, K), (N, K2) = a.shape, b_shape
    else:
        (K, M), (K2, N) = a.shape, b_shape
    assert K == K2, (a.shape, b.shape, dims)
    if dims == 'tn':
        tm = _pick(M, (1024, 768, 512, 384, 256, 128))
        tn = _pick(N, (1024, 896, 768, 512, 384, 256, 128))
        tk = _pick(K, (1408, 1024, 512, 384, 256, 128))
    else:
        tm = _pick(M, (704, 512, 384, 256, 128) if norm_gain is not None else (1408, 1024, 512, 384, 256, 128))
        tn = _pick(N, (1024, 896, 768, 512, 384, 256, 128))
        tk = _pick(K, (1024, 896, 768, 512, 384, 256, 128))
    nk = K // tk
    if dims == 'nn':
        a_spec = pl.BlockSpec((tm, tk), lambda i, j, k: (i, k))
        b_spec = pl.BlockSpec((tk, tn), lambda i, j, k: (k, j))
        dn = (((1,), (0,)), ((), ()))
    elif dims == 'nt':
        a_spec = pl.BlockSpec((tm, tk), lambda i, j, k: (i, k))
        b_spec = pl.BlockSpec((tn, tk), lambda i, j, k: (j, k))
        dn = (((1,), (1,)), ((), ()))
    else:
        a_spec = pl.BlockSpec((tk, tm), lambda i, j, k: (k, i))
        b_spec = pl.BlockSpec((tk, tn), lambda i, j, k: (k, j))
        dn = (((0,), (0,)), ((), ()))
    if b_slab is not None:
        b_spec = _slab_spec(b_spec, b_slab)
    o_spec = pl.BlockSpec((tm, tn), lambda i, j, k: (i, j))
    n_ex = len(extras)
    out_shape = jax.ShapeDtypeStruct((M, N), out_dtype)
    out_spec, held, aliases = o_spec, (), {}
    if stack is not None:
        n_slabs, slab, buf = stack
        out_shape = jax.ShapeDtypeStruct((n_slabs, M, N), out_dtype)
        out_spec = pl.BlockSpec((None, tm, tn), lambda i, j, k: (slab, i, j))
        if buf is not None:
            held, aliases = (buf,), {2 + n_ex: 0}

    gains = ()
    if norm_gain is not None:
        assert tn == N and stack is None, "the rms epilogue needs whole rows"
        gains = (norm_gain,)
        out_shape = [out_shape, jax.ShapeDtypeStruct((M, N), BF16)]
        out_spec = [out_spec, o_spec]

    n_own_in = 2 + n_ex + len(gains) + len(held)
    steps = (M // tm, N // tn, nk)

    def body(*refs):
        (a_ref, b_ref, *rest), ride = _rider_split(rider, refs, n_own_in, 1 + len(gains), 1)
        ex_refs, rest = rest[:n_ex], rest[n_ex:]
        g_refs, rest = rest[:len(gains)], rest[len(gains) + len(held):]
        o_ref, acc = rest[0], rest[-1]
        k = pl.program_id(2)
        if ride is not None:
            @pl.when((pl.program_id(0) == 0) & (pl.program_id(1) == 0) & (k == 0))
            def _():
                rider.start(*ride)

        @pl.when(k == 0)
        def _():
            acc[...] = jnp.zeros_like(acc)

        av = a_ref[...]
        if a_fn is not None:
            av = a_fn(av)
        acc[...] += lax.dot_general(av.astype(BF16), b_ref[...].astype(BF16), dn,
                                    preferred_element_type=F32)

        @pl.when(k == nk - 1)
        def _():
            r = acc[...]
            if epi is not None:
                r = epi(r, *[e[...] for e in ex_refs])
            o_ref[...] = r.astype(out_dtype)
            if gains:
                rest[1][...] = _rms(r, g_refs[0][...]).astype(BF16)

        if ride is not None:
            @pl.when((pl.program_id(0) == steps[0] - 1) & (pl.program_id(1) == steps[1] - 1) & (k == nk - 1))
            def _():
                rider.finish(*ride)

    r_in, r_out, r_shapes, r_scratch = _rider_args(rider)
    own_shapes = out_shape if isinstance(out_shape, list) else [out_shape]
    own_specs = out_spec if isinstance(out_spec, list) else [out_spec]
    res = pl.pallas_call(
        body, name=name,
        out_shape=own_shapes + r_shapes,
        grid=steps,
        in_specs=[a_spec, b_spec] + [o_spec] * n_ex
        + [pl.BlockSpec((1, tn), lambda i, j, k: (0, j))] * len(gains)
        + [pl.BlockSpec(memory_space=pl.ANY)] * len(held) + r_in,
        out_specs=own_specs + r_out,
        input_output_aliases=aliases,
        scratch_shapes=[pltpu.VMEM((tm, tn), F32)] + r_scratch,
        compiler_params=_cparams(dimension_semantics=("arbitrary", "arbitrary", "arbitrary")),
    )(a, b, *extras, *gains, *held, *(rider.operands if rider else ()))
    return res[0] if len(res) == 1 else res


def _mm_rms_bwd(cot, w_t, h, d_res, gain, name):
    w_t, w_slab = w_t if isinstance(w_t, tuple) else (w_t, None)
    (M, K), N = cot.shape, w_t.shape[-2]
    tm = _pick(M, (704, 512, 384, 256, 128))
    tk = _pick(K, (1024, 896, 768, 512, 384, 256, 128))
    nk = K // tk
    w_spec = pl.BlockSpec((N, tk), lambda i, k: (0, k))
    if w_slab is not None:
        w_spec = _slab_spec(w_spec, w_slab)

    def body(a_ref, b_ref, h_ref, r_ref, g_ref, dh_ref, dg_ref, acc):
        i, k = pl.program_id(0), pl.program_id(1)

        @pl.when(k == 0)
        def _():
            acc[...] = jnp.zeros_like(acc)

        acc[...] += lax.dot_general(a_ref[...].astype(BF16), b_ref[...].astype(BF16), _NT,
                                    preferred_element_type=F32)

        @pl.when(k == nk - 1)
        def _():
            _, vjp = jax.vjp(_rms, h_ref[...], g_ref[...])
            dh, dg = vjp(acc[...])
            dh_ref[...] = (r_ref[...] + dh) * _row_mask(i, tm)

            @pl.when(i == 0)
            def _():
                dg_ref[...] = dg

            @pl.when(i > 0)
            def _():
                dg_ref[...] += dg

    rows = pl.BlockSpec((tm, N), lambda i, k: (i, 0))
    vec = pl.BlockSpec((1, N), lambda i, k: (0, 0))
    return pl.pallas_call(
        body, name=name,
        out_shape=[jax.ShapeDtypeStruct((M, N), F32), jax.ShapeDtypeStruct((1, N), F32)],
        grid=(M // tm, nk),
        in_specs=[pl.BlockSpec((tm, tk), lambda i, k: (i, k)), w_spec, rows, rows, vec],
        out_specs=[rows, vec],
        scratch_shapes=[pltpu.VMEM((tm, N), F32)],
        compiler_params=_cparams(dimension_semantics=("arbitrary", "arbitrary")),
    )(cot, w_t, h, d_res, gain)


def _mm_attn_do(dh, w_out_t, o, name):
    (M, K), (N, _) = dh.shape, w_out_t.shape
    tm = _pick(M, (704, 512, 384, 256, 128))
    tn = 8 * HEAD_SLOT

    def body(a_ref, b_ref, o_ref, dob_ref, delta_ref):
        do = lax.dot_general(a_ref[...].astype(BF16), b_ref[...], _NT, preferred_element_type=F32)
        dob_ref[...] = do.astype(BF16)
        for hh in range(tn // HEAD_SLOT):
            sl = slice(hh * HEAD_SLOT, (hh + 1) * HEAD_SLOT)
            delta_ref[hh] = jnp.sum(do[:, sl] * o_ref[:, sl], axis=-1, keepdims=True)

    tile = pl.BlockSpec((tm, tn), lambda i, j: (i, j))
    return pl.pallas_call(
        body, name=name,
        out_shape=[jax.ShapeDtypeStruct((M, N), BF16), jax.ShapeDtypeStruct((N // HEAD_SLOT, M, 1), F32)],
        grid=(M // tm, N // tn),
        in_specs=[pl.BlockSpec((tm, K), lambda i, j: (i, 0)), pl.BlockSpec((tn, K), lambda i, j: (j, 0)), tile],
        out_specs=[tile, pl.BlockSpec((tn // HEAD_SLOT, tm, 1), lambda i, j: (j, i, 0))],
        compiler_params=_cparams(dimension_semantics=("parallel", "parallel")),
    )(dh, w_out_t, o)


def _row_call(fn, rows, consts, out_rows, out_accs=(), *, n_rows, tile, name):
    n_r, n_c, n_o, n_a = len(rows), len(consts), len(out_rows), len(out_accs)
    steps = n_rows // tile

    def body(*refs):
        r_refs = refs[:n_r]
        c_refs = refs[n_r:n_r + n_c]
        o_refs = refs[n_r + n_c:n_r + n_c + n_o]
        a_refs = refs[n_r + n_c + n_o:]
        i = pl.program_id(0)
        res = fn(i, *[r[...] for r in r_refs], *[c[...] for c in c_refs])
        for o_ref, val in zip(o_refs, res[:n_o]):
            o_ref[...] = val.astype(o_ref.dtype)

        @pl.when(i == 0)
        def _():
            for a_ref in a_refs:
                a_ref[...] = jnp.zeros_like(a_ref)

        for a_ref, val in zip(a_refs, res[n_o:]):
            a_ref[...] += val

    in_specs = [pl.BlockSpec((tile, w), functools.partial(lambda i, cb: (i, cb), cb=cb))
                for (_, w, cb) in rows]
    in_specs += [pl.BlockSpec(c.shape, lambda i: (0, 0)) for c in consts]
    out_specs = [pl.BlockSpec((tile, c), lambda i: (i, 0)) for (c, _) in out_rows]
    out_specs += [pl.BlockSpec(s, lambda i: (0, 0)) for s in out_accs]
    out_shape = [jax.ShapeDtypeStruct((n_rows, c), dt) for (c, dt) in out_rows]
    out_shape += [jax.ShapeDtypeStruct(s, F32) for s in out_accs]
    return pl.pallas_call(
        body, name=name, out_shape=out_shape, grid=(steps,),
        in_specs=in_specs, out_specs=out_specs,
        compiler_params=_cparams(dimension_semantics=("arbitrary",)),
    )(*[r[0] for r in rows], *consts)


def _row_mask(i, tile):
    r = i * tile + lax.broadcasted_iota(jnp.int32, (tile, 1), 0)
    return (r >= NPAD).astype(F32)


def _rms(x, g):
    return x * lax.rsqrt(jnp.mean(x * x, axis=-1, keepdims=True) + EPS) * g


def _silu(x):
    return x * (0.5 * jnp.tanh(0.5 * x) + 0.5)


def _softplus(x):
    return jnp.maximum(x, 0.0) + jnp.log(1.0 + jnp.exp(-jnp.abs(x)))


def _rms_fwd(h, g, name):
    lp = h.shape[0]
    return _row_call(lambda i, hv, gv: (_rms(hv, gv),), [(h, D_MODEL, 0)], [g],
                     [(D_MODEL, BF16)], n_rows=lp, tile=_pick(lp, (384, 256, 128)), name=name)[0]


@functools.partial(jax.custom_vjp, nondiff_argnums=(1,))
def _roll_rows(x, s):
    return pltpu.roll(x, s, 0)


def _roll_rows_fwd(x, s):
    return pltpu.roll(x, s, 0), None


def _roll_rows_bwd(s, _, ct):
    return (pltpu.roll(ct, (ct.shape[0] - s) % ct.shape[0], 0),)


_roll_rows.defvjp(_roll_rows_fwd, _roll_rows_bwd)


def _conv_silu(cur, halo, w_rows, b):
    full = jnp.concatenate([halo, cur], axis=0)
    acc = cur * w_rows[SSD_CONV - 1] + b
    for k in range(SSD_CONV - 1):
        acc = acc + _roll_rows(full, SSD_CONV - 1 - k)[8:] * w_rows[k]
    return _silu(acc)


def _split3(v):
    hi = v.astype(BF16)
    r1 = v - hi.astype(F32)
    mid = r1.astype(BF16)
    lo = (r1 - mid.astype(F32)).astype(BF16)
    return hi, mid, lo


def _select_right(v, sel, dn):
    return sum(lax.dot_general(p, sel, dn, preferred_element_type=F32) for p in _split3(v))


@jax.custom_vjp
def _expand_heads(v, e_mat):
    return _select_right(v, e_mat, _NN)


def _expand_heads_fwd(v, e_mat):
    return _select_right(v, e_mat, _NN), e_mat


def _expand_heads_bwd(e_mat, ct):
    return _select_right(ct, e_mat, _NT), jnp.zeros_like(e_mat)


_expand_heads.defvjp(_expand_heads_fwd, _expand_heads_bwd)


@jax.custom_vjp
def _cumsum_rows(a, tri):
    return sum(lax.dot_general(tri, p, _NN, preferred_element_type=F32) for p in _split3(a))


def _cumsum_rows_fwd(a, tri):
    return _cumsum_rows(a, tri), tri


def _cumsum_rows_bwd(tri, ct):
    return (sum(lax.dot_general(tri, p, _TN, preferred_element_type=F32) for p in _split3(ct)),
            jnp.zeros_like(tri))


_cumsum_rows.defvjp(_cumsum_rows_fwd, _cumsum_rows_bwd)


def _ssd_chunk(mask, z, xs_pre, bc_pre, halo_x, halo_bc, dt_pre, st, cwx0, cwx1, cwx2, cwx3,
               cwb0, cwb1, cwb2, cwb3, cb_x, cb_bc, dtb, alog, dsk, ng):
    L = CHUNK
    lane_h = lax.broadcasted_iota(jnp.int32, (1, 128), 1)
    head_ok = (lane_h < SSD_HEADS).astype(F32)
    e_mat = (lax.broadcasted_iota(jnp.int32, (128, SSD_D_INNER), 1) // SSD_HEAD_DIM
             == lax.broadcasted_iota(jnp.int32, (128, SSD_D_INNER), 0)).astype(BF16)
    ri = lax.broadcasted_iota(jnp.int32, (L, L), 0)
    ci = lax.broadcasted_iota(jnp.int32, (L, L), 1)
    causal = ri >= ci

    xs = _conv_silu(xs_pre, halo_x, (cwx0, cwx1, cwx2, cwx3), cb_x) * mask
    bc = _conv_silu(bc_pre, halo_bc, (cwb0, cwb1, cwb2, cwb3), cb_bc) * mask
    dt = _softplus(dt_pre + dtb) * mask * head_ok
    a_dt = dt * (-jnp.exp(alog))
    a_cs = _cumsum_rows(a_dt, causal.astype(BF16))
    a_cs_t = a_cs.T
    row8 = lax.broadcasted_iota(jnp.int32, (8, 128), 0)
    last8 = jnp.where(row8 == 0, jnp.sum(a_dt, axis=0, keepdims=True), 0.0)
    dsk8 = jnp.where(row8 == 0, dsk, 0.0)
    wide = _expand_heads(jnp.concatenate([dt, a_cs, last8, dsk8], axis=0), e_mat)
    dt_e, acs_e = wide[0:L], wide[L:2 * L]
    last_e = jnp.sum(wide[2 * L:2 * L + 8], axis=0, keepdims=True)
    d_e = jnp.sum(wide[2 * L + 8:2 * L + 16], axis=0, keepdims=True)
    xdt = xs * dt_e
    dte_e = jnp.exp(last_e - acs_e)
    dfs_e = jnp.exp(acs_e)
    cd_e = jnp.exp(last_e)
    sub_h = lax.broadcasted_iota(jnp.int32, (128, L), 0)
    lane_hl = lax.broadcasted_iota(jnp.int32, (L, 128), 1)
    lane_g = lax.broadcasted_iota(jnp.int32, (1, SSD_HPG * SSD_HEAD_DIM), 1) // SSD_HEAD_DIM

    ys, new_st = [], []
    for g in range(SSD_GROUPS):
        b_g = bc[:, g * 128:(g + 1) * 128].astype(BF16)
        c_g = bc[:, 1024 + g * 128:1024 + (g + 1) * 128].astype(BF16)
        gs = slice(g * 256, (g + 1) * 256)
        xdt_g = xdt[:, gs]
        cb = lax.dot_general(c_g, b_g, (((1,), (1,)), ((), ())), preferred_element_type=F32)
        st_g = st[g * 128:(g + 1) * 128, :]
        y_g = lax.dot_general(c_g, st_g.astype(BF16), (((1,), (0,)), ((), ())),
                              preferred_element_type=F32) * dfs_e[:, gs]
        for j in range(SSD_HPG):
            h = g * SSD_HPG + j
            col = jnp.sum(jnp.where(lane_hl == h, a_cs, 0.0), axis=1, keepdims=True)
            row = jnp.sum(jnp.where(sub_h == h, a_cs_t, 0.0), axis=0, keepdims=True)
            dec = jnp.where(causal, jnp.exp(jnp.where(causal, col - row, 0.0)), 0.0)
            m_h = (cb * dec).astype(BF16)
            x_h = jnp.where(lane_g == j, xdt_g, 0.0).astype(BF16)
            y_g = y_g + lax.dot_general(m_h, x_h, (((1,), (0,)), ((), ())),
                                        preferred_element_type=F32)
        s_new = lax.dot_general(b_g, (xdt_g * dte_e[:, gs]).astype(BF16), (((0,), (0,)), ((), ())),
                                preferred_element_type=F32)
        new_st.append(st_g * cd_e[:, gs] + s_new)
        ys.append(y_g)
    y = jnp.concatenate(ys, axis=1) + xs * d_e
    gg = y * _silu(z)
    outs = []
    for g in range(SSD_GROUPS):
        sl = gg[:, g * 256:(g + 1) * 256]
        outs.append(sl * lax.rsqrt(jnp.mean(sl * sl, axis=-1, keepdims=True) + EPS))
    out = jnp.concatenate(outs, axis=1) * ng
    return out, jnp.concatenate(new_st, axis=0)


def _ssd_consts(conv_w, conv_b, dtb, alog, dsk, ng):
    return [conv_w, conv_b, dtb, alog, dsk, ng]


def _ssd_param_vals(cw_ref, cb_ref, dtb_ref, alog_ref, dsk_ref, ng_ref):
    cwx = [cw_ref[k:k + 1, 0:SSD_D_INNER] for k in range(SSD_CONV)]
    cwb = [cw_ref[k:k + 1, SSD_D_INNER:2 * SSD_D_INNER] for k in range(SSD_CONV)]
    return (*cwx, *cwb, cb_ref[:, 0:SSD_D_INNER], cb_ref[:, SSD_D_INNER:2 * SSD_D_INNER],
            dtb_ref[...], alog_ref[...], dsk_ref[...], ng_ref[...])


def _ssd_in_specs(rev, nc):
    def cidx(i):
        return (nc - 1 - i) if rev else i

    def halo(cb):
        return pl.BlockSpec((8, SSD_D_INNER), lambda i: (jnp.maximum(16 * cidx(i) - 1, 0), cb))

    return [
        pl.BlockSpec((CHUNK, SSD_D_INNER), lambda i: (cidx(i), 0)),
        pl.BlockSpec((CHUNK, SSD_D_INNER), lambda i: (cidx(i), 1)),
        pl.BlockSpec((CHUNK, SSD_D_INNER), lambda i: (cidx(i), 2)),
        halo(1), halo(2),
        pl.BlockSpec((CHUNK, 128), lambda i: (cidx(i), 48)),
    ]


class _Rider:
    def __init__(self, operands, out_shapes, scratch, start, finish):
        self.operands, self.out_shapes, self.scratch = list(operands), list(out_shapes), list(scratch)
        self.start, self.finish = start, finish


def _rider_split(rider, refs, n_in, n_out, n_scratch):
    if rider is None:
        return refs, None
    ni, no = len(rider.operands), len(rider.out_shapes)
    own = refs[:n_in] + refs[n_in + ni:n_in + ni + n_out] + refs[n_in + ni + n_out + no:n_in + ni + n_out + no + n_scratch]
    mine = (refs[n_in:n_in + ni], refs[n_in + ni + n_out:n_in + ni + n_out + no],
            refs[n_in + ni + n_out + no + n_scratch:])
    return own, mine


def _rider_args(rider):
    if rider is None:
        return [], [], [], []
    hbm = pl.BlockSpec(memory_space=pl.ANY)
    return ([hbm] * len(rider.operands), [hbm] * len(rider.out_shapes), rider.out_shapes, rider.scratch)


def _ssd_fwd(zxd, consts, name, rider=None):
    lp = zxd.shape[0]
    nc = lp // CHUNK

    def body(*refs):
        own, ride = _rider_split(rider, refs, 12, 2, 1)
        (z_ref, xs_ref, bc_ref, hx_ref, hb_ref, dt_ref, cw_ref, cb_ref, dtb_ref, alog_ref,
         dsk_ref, ng_ref, y_ref, st_ref, state) = own
        c = pl.program_id(0)

        @pl.when(c == 0)
        def _():
            state[...] = jnp.zeros_like(state)
            if ride is not None:
                rider.start(*ride)

        live = (c > 0).astype(F32)
        st_ref[0] = state[...]
        out, st_new = _ssd_chunk(
            _row_mask(c, CHUNK), z_ref[...], xs_ref[...], bc_ref[...], hx_ref[...] * live,
            hb_ref[...] * live, dt_ref[...], state[...],
            *_ssd_param_vals(cw_ref, cb_ref, dtb_ref, alog_ref, dsk_ref, ng_ref))
        y_ref[...] = out.astype(y_ref.dtype)
        state[...] = st_new

        if ride is not None:
            @pl.when(c == nc - 1)
            def _():
                rider.finish(*ride)

    r_in, r_out, r_shapes, r_scratch = _rider_args(rider)
    return pl.pallas_call(
        body, name=name,
        out_shape=[jax.ShapeDtypeStruct((lp, SSD_D_INNER), BF16),
                   jax.ShapeDtypeStruct((nc, SSD_GROUPS * SSD_STATE, 256), F32)] + r_shapes,
        grid=(nc,),
        in_specs=_ssd_in_specs(False, nc) + [pl.BlockSpec(c.shape, lambda i: (0, 0)) for c in consts] + r_in,
        out_specs=[pl.BlockSpec((CHUNK, SSD_D_INNER), lambda i: (i, 0)),
                   pl.BlockSpec((1, SSD_GROUPS * SSD_STATE, 256), lambda i: (i, 0, 0))] + r_out,
        scratch_shapes=[pltpu.VMEM((SSD_GROUPS * SSD_STATE, 256), F32)] + r_scratch,
        compiler_params=_cparams(dimension_semantics=("arbitrary",)),
    )(zxd, zxd, zxd, zxd, zxd, zxd, *consts, *(rider.operands if rider else ()))


def _ssd_bwd(zxd, states, d_y, consts, name, rider=None):
    lp = zxd.shape[0]
    nc = lp // CHUNK

    def body(*refs):
        own, ride = _rider_split(rider, refs, 14, 7, 3)
        (z_ref, xs_ref, bc_ref, hx_ref, hb_ref, dt_ref, st_ref, dy_ref, cw_ref, cb_ref, dtb_ref,
         alog_ref, dsk_ref, ng_ref, dz_ref, dcw_ref, dcb_ref, ddtb_ref, dalog_ref, ddsk_ref,
         dng_ref, d_state, d_hx, d_hb) = own
        i = pl.program_id(0)
        c = nc - 1 - i

        @pl.when(i == 0)
        def _():
            d_state[...] = jnp.zeros_like(d_state)
            d_hx[...] = jnp.zeros_like(d_hx)
            d_hb[...] = jnp.zeros_like(d_hb)
            for r in (dcw_ref, dcb_ref, ddtb_ref, dalog_ref, ddsk_ref, dng_ref):
                r[...] = jnp.zeros_like(r)
            if ride is not None:
                rider.start(*ride)

        live = (c > 0).astype(F32)
        fn = functools.partial(_ssd_chunk, _row_mask(c, CHUNK))
        prim = (z_ref[...], xs_ref[...], bc_ref[...], hx_ref[...] * live, hb_ref[...] * live,
                dt_ref[...], st_ref[0],
                *_ssd_param_vals(cw_ref, cb_ref, dtb_ref, alog_ref, dsk_ref, ng_ref))
        _, vjp = jax.vjp(fn, *prim)
        (d_z, d_xs, d_bc, g_hx, g_hb, d_dt, g_st, *d_par) = vjp((dy_ref[...], d_state[...]))
        zeros = jnp.zeros((CHUNK - 8, SSD_D_INNER), F32)
        d_xs = d_xs + jnp.concatenate([zeros, d_hx[...]], axis=0)
        d_bc = d_bc + jnp.concatenate([zeros, d_hb[...]], axis=0)
        dz_ref[:, 0:SSD_D_INNER] = d_z.astype(dz_ref.dtype)
        dz_ref[:, SSD_D_INNER:2 * SSD_D_INNER] = d_xs.astype(dz_ref.dtype)
        dz_ref[:, 2 * SSD_D_INNER:3 * SSD_D_INNER] = d_bc.astype(dz_ref.dtype)
        dz_ref[:, 3 * SSD_D_INNER:] = d_dt.astype(dz_ref.dtype)
        d_state[...] = g_st
        d_hx[...] = g_hx * live
        d_hb[...] = g_hb * live
        for k in range(SSD_CONV):
            dcw_ref[k:k + 1, 0:SSD_D_INNER] += d_par[k]
            dcw_ref[k:k + 1, SSD_D_INNER:2 * SSD_D_INNER] += d_par[SSD_CONV + k]
        dcb_ref[:, 0:SSD_D_INNER] += d_par[8]
        dcb_ref[:, SSD_D_INNER:2 * SSD_D_INNER] += d_par[9]
        ddtb_ref[...] += d_par[10]
        dalog_ref[...] += d_par[11]
        ddsk_ref[...] += d_par[12]
        dng_ref[...] += d_par[13]

        if ride is not None:
            @pl.when(i == nc - 1)
            def _():
                rider.finish(*ride)

    const_specs = [pl.BlockSpec(c.shape, lambda i: (0, 0)) for c in consts]
    r_in, r_out, r_shapes, r_scratch = _rider_args(rider)
    return pl.pallas_call(
        body, name=name,
        out_shape=[jax.ShapeDtypeStruct((lp, SSD_IN_PAD), BF16)]
        + [jax.ShapeDtypeStruct(c.shape, F32) for c in consts] + r_shapes,
        grid=(nc,),
        in_specs=_ssd_in_specs(True, nc)
        + [pl.BlockSpec((1, SSD_GROUPS * SSD_STATE, 256), lambda i: (nc - 1 - i, 0, 0)),
           pl.BlockSpec((CHUNK, SSD_D_INNER), lambda i: (nc - 1 - i, 0))] + const_specs + r_in,
        out_specs=[pl.BlockSpec((CHUNK, SSD_IN_PAD), lambda i: (nc - 1 - i, 0))] + const_specs + r_out,
        scratch_shapes=[pltpu.VMEM((SSD_GROUPS * SSD_STATE, 256), F32),
                        pltpu.VMEM((8, SSD_D_INNER), F32), pltpu.VMEM((8, SSD_D_INNER), F32)] + r_scratch,
        compiler_params=_cparams(dimension_semantics=("arbitrary",)),
    )(zxd, zxd, zxd, zxd, zxd, zxd, states, d_y, *consts, *(rider.operands if rider else ()))


@jax.custom_vjp
def _rot_half(x):
    lane = lax.broadcasted_iota(jnp.int32, x.shape, 1)
    lo = (lane >= MLA_NOPE) & (lane < MLA_NOPE + MLA_ROPE // 2)
    hi = (lane >= MLA_NOPE + MLA_ROPE // 2) & (lane < MLA_QK)
    down = pltpu.roll(x, HEAD_SLOT - MLA_ROPE // 2, 1)
    up = pltpu.roll(x, MLA_ROPE // 2, 1)
    return jnp.where(lo, -down, jnp.where(hi, up, 0.0))


def _rot_half_fwd(x):
    return _rot_half(x), None


def _rot_half_bwd(_, ct):
    return (-_rot_half(ct),)


_rot_half.defvjp(_rot_half_fwd, _rot_half_bwd)


def _head_norm_rope(t, gain, cos, sin):
    n = t * lax.rsqrt(jnp.sum(t * t, axis=-1, keepdims=True) * (1.0 / MLA_QK) + EPS) * gain
    return n * cos + _rot_half(n) * sin


def _qk_prep(q_raw, kn_raw, kpe, cos, sin, qg, kg):
    qs, ks = [], []
    for h in range(MLA_HEADS):
        sl = slice(h * HEAD_SLOT, (h + 1) * HEAD_SLOT)
        qs.append(_head_norm_rope(q_raw[:, sl], qg, cos, sin))
        ks.append(_head_norm_rope(kn_raw[:, sl] + kpe, kg, cos, sin))
    return jnp.concatenate(qs, axis=1), jnp.concatenate(ks, axis=1)


def _lat_norm(kv_lat, q_lat, kvg, qg):
    return _rms(kv_lat, kvg), _rms(q_lat, qg)


_NEG = -1e30
_SCALE = MLA_QK ** -0.5


STRIP = 128
_EXP2_SCALE = _SCALE * math.log2(math.e)


def _strip_mask(kind, blk, c, t):
    if kind is None:
        return None
    kpos = blk * t + c * STRIP + lax.broadcasted_iota(jnp.int32, (1, STRIP), 1)
    if kind == 'keys':
        return kpos >= NPAD
    qpos = blk * t + lax.broadcasted_iota(jnp.int32, (t, 1), 0)
    return (kpos <= qpos) & ((kpos >= NPAD) | (kpos == qpos))


def _attn_fwd(q, k, v, name, rider=None):
    lp = q.shape[0]
    t = tk = _pick(lp, (384, 256, 128))
    nb = lp // t
    hp = HEADS_PER_STEP
    wide = hp * HEAD_SLOT
    heads = [slice(a * HEAD_SLOT, (a + 1) * HEAD_SLOT) for a in range(hp)]

    def body(*refs):
        (q_ref, k_ref, v_ref, o_ref, lse_ref), ride = _rider_split(rider, refs, 3, 2, 0)
        qi = pl.program_id(1)
        if ride is not None:
            @pl.when((pl.program_id(0) == 0) & (qi == 0))
            def _():
                rider.start(*ride)

        def scores(ki):
            rows = pl.ds(pl.multiple_of(ki * tk, tk), tk)
            return tuple(lax.dot_general(q_ref[:, heads[a]], k_ref[rows, heads[a]], _NT,
                                         preferred_element_type=F32) for a in range(hp))

        def update(a, ki, carry, s, mask):
            rows = pl.ds(pl.multiple_of(ki * tk, tk), tk)
            m, acc = carry
            s = jnp.where(mask, s, _NEG)
            m_new = jnp.maximum(m, jnp.max(s, axis=-1, keepdims=True))
            alpha = jnp.exp2((m - m_new) * _EXP2_SCALE)
            p = jnp.concatenate(
                [jnp.exp2((s[:, c:c + STRIP] - m_new) * _EXP2_SCALE).astype(BF16) for c in range(0, tk, STRIP)],
                axis=1)
            acc = alpha * acc + lax.dot_general(p, v_ref[rows, heads[a]], _NN, preferred_element_type=F32)
            return m_new, acc

        init = (jnp.full((t, 1), _NEG, F32), jnp.zeros((t, HEAD_SLOT), F32))
        ones_lane = lax.broadcasted_iota(jnp.int32, (1, HEAD_SLOT), 1) == MLA_V
        key_pos = lax.broadcasted_iota(jnp.int32, (1, tk), 1)
        n_full = (qi * t) // tk

        def before(ki, state):
            carry, s = state
            s_next = scores(ki + 1)
            key_ok = ki * tk + key_pos >= NPAD
            return tuple(update(a, ki, carry[a], s[a], key_ok) for a in range(hp)), s_next

        carry, s = lax.fori_loop(0, n_full, before, ((init,) * hp, scores(0)))
        qpos = qi * t + lax.broadcasted_iota(jnp.int32, (t, tk), 0)
        kpos = n_full * tk + lax.broadcasted_iota(jnp.int32, (t, tk), 1)
        diag = (kpos <= qpos) & ((kpos >= NPAD) | (kpos == qpos))
        carry = tuple(update(a, n_full, carry[a], s[a], diag) for a in range(hp))
        for a in range(hp):
            m, acc = carry[a]
            l = jnp.sum(jnp.where(ones_lane, acc, 0.0), axis=-1, keepdims=True)
            o_ref[:, heads[a]] = jnp.where(ones_lane, 0.0, acc / l * _row_mask(qi, t))
            lse_ref[a] = m * _SCALE + jnp.log(l)

        if ride is not None:
            @pl.when((pl.program_id(0) == MLA_HEADS // hp - 1) & (qi == nb - 1))
            def _():
                rider.finish(*ride)

    qspec = pl.BlockSpec((t, wide), lambda g, i: (i, g))
    kspec = pl.BlockSpec((lp, wide), lambda g, i: (0, g))
    r_in, r_out, r_shapes, r_scratch = _rider_args(rider)
    return pl.pallas_call(
        body, name=name,
        out_shape=[jax.ShapeDtypeStruct((lp, MLA_WIDE), F32),
                   jax.ShapeDtypeStruct((MLA_HEADS, lp, 1), F32)] + r_shapes,
        grid=(MLA_HEADS // hp, nb),
        in_specs=[qspec, kspec, kspec] + r_in,
        out_specs=[qspec, pl.BlockSpec((hp, t, 1), lambda g, i: (g, i, 0))] + r_out,
        scratch_shapes=r_scratch,
        compiler_params=_cparams(dimension_semantics=("arbitrary", "arbitrary")),
    )(q, k, v, *(rider.operands if rider else ()))


def _attn_bwd(q, k, v, do, lse, delta, name, rider=None):
    lp = q.shape[0]
    t = _pick(lp, (384, 256, 128))
    nb = lp // t
    ns = t // STRIP
    hp = HEADS_PER_STEP
    wide = hp * HEAD_SLOT
    heads = [slice(a * HEAD_SLOT, (a + 1) * HEAD_SLOT) for a in range(hp)]
    log2e = math.log2(math.e)

    def body(*refs):
        own, ride = _rider_split(rider, refs, 6, 3, 4)
        (q_ref, k_ref, v_ref, do_ref, lse_ref, delta_ref, dq_ref, dk_ref, dv_ref,
         s_scr, dp_scr, p_scr, ds_scr) = own
        kj = pl.program_id(1)
        if ride is not None:
            @pl.when((pl.program_id(0) == 0) & (kj == 0))
            def _():
                rider.start(*ride)

        @pl.when(kj == 0)
        def _():
            dq_ref[...] = jnp.zeros_like(dq_ref)

        dk_ref[...] = jnp.zeros_like(dk_ref)
        dv_ref[...] = jnp.zeros_like(dv_ref)

        def tile(qi, kind):
            rows = pl.ds(pl.multiple_of(qi * t, t), t)
            for a in range(hp):
                qb, dob = q_ref[rows, heads[a]], do_ref[rows, heads[a]]
                kb, vb = k_ref[:, heads[a]], v_ref[:, heads[a]]
                s_scr[a] = lax.dot_general(qb, kb, _NT, preferred_element_type=F32)
                dp_scr[a] = lax.dot_general(dob, vb, _NT, preferred_element_type=F32)
                lse2 = lse_ref[a, rows, :] * log2e
                delta = delta_ref[a, rows, :]
                for c in range(ns):
                    cs = slice(c * STRIP, (c + 1) * STRIP)
                    pc = jnp.exp2(s_scr[a, :, cs] * _EXP2_SCALE - lse2)
                    mask = _strip_mask(kind, kj, c, t)
                    if mask is not None:
                        pc = jnp.where(mask, pc, 0.0)
                    p_scr[a, :, cs] = pc.astype(BF16)
                    ds_scr[a, :, cs] = (pc * (dp_scr[a, :, cs] - delta)).astype(BF16)
                dq_ref[rows, heads[a]] += lax.dot_general(ds_scr[a], kb, _NN,
                                                          preferred_element_type=F32) * _SCALE
                dv_ref[:, heads[a]] += lax.dot_general(p_scr[a], dob, _TN, preferred_element_type=F32)
                dk_ref[:, heads[a]] += lax.dot_general(ds_scr[a], qb, _TN, preferred_element_type=F32)

        tile(kj, 'diag')

        def below(kind):
            def body(qi, carry):
                tile(qi, kind)
                return carry
            return body

        @pl.when(kj == 0)
        def _():
            lax.fori_loop(kj + 1, nb, below('keys'), 0)

        @pl.when(kj > 0)
        def _():
            lax.fori_loop(kj + 1, nb, below(None), 0)

        dk_ref[...] = dk_ref[...] * _SCALE

        if ride is not None:
            @pl.when((pl.program_id(0) == MLA_HEADS // hp - 1) & (kj == nb - 1))
            def _():
                rider.finish(*ride)

    whole = pl.BlockSpec((lp, wide), lambda g, j: (0, g))
    kspec = pl.BlockSpec((t, wide), lambda g, j: (j, g))
    stat = pl.BlockSpec((hp, lp, 1), lambda g, j: (g, 0, 0))
    r_in, r_out, r_shapes, r_scratch = _rider_args(rider)
    return pl.pallas_call(
        body, name=name,
        out_shape=[jax.ShapeDtypeStruct((lp, MLA_WIDE), F32)] * 3 + r_shapes,
        grid=(MLA_HEADS // hp, nb),
        in_specs=[whole, kspec, kspec, whole, stat, stat] + r_in,
        out_specs=[whole, kspec, kspec] + r_out,
        scratch_shapes=[pltpu.VMEM((hp, t, t), F32), pltpu.VMEM((hp, t, t), F32),
                        pltpu.VMEM((hp, t, t), BF16), pltpu.VMEM((hp, t, t), BF16)] + r_scratch,
        compiler_params=_cparams(dimension_semantics=("arbitrary", "arbitrary")),
    )(q, k, v, do, lse, delta, *(rider.operands if rider else ()))


def _rope_tables(lp):
    inv = 1.0 / (ROPE_THETA ** (jnp.arange(0, MLA_ROPE, 2, dtype=F32) / MLA_ROPE))
    pos = jnp.maximum(jnp.arange(lp, dtype=jnp.int32) - NPAD, 0).astype(F32)
    ang = pos[:, None] * inv[None, :]
    cos, sin = jnp.cos(ang), jnp.sin(ang)
    z32 = jnp.zeros((lp, HEAD_SLOT - MLA_QK), F32)
    cos_t = jnp.concatenate([jnp.ones((lp, MLA_NOPE), F32), cos, cos, z32], axis=1)
    sin_t = jnp.concatenate([jnp.zeros((lp, MLA_NOPE), F32), sin, sin, z32], axis=1)
    return cos_t, sin_t


def _loss_head(h, target, name):
    lp = h.shape[0]

    def body(h_ref, t_ref, d_ref, loss_ref):
        i = pl.program_id(0)

        @pl.when(i == 0)
        def _():
            d_ref[...] = jnp.zeros_like(d_ref)
            loss_ref[...] = jnp.zeros_like(loss_ref)

        @pl.when(i > 0)
        def _():
            err = h_ref[...] - t_ref[...]
            d_ref[...] = err * (1.0 / D_MODEL)
            loss_ref[...] += jnp.sum(err * err, axis=0, keepdims=True) * (0.5 / D_MODEL)

    return pl.pallas_call(
        body, name=name,
        out_shape=[jax.ShapeDtypeStruct((lp, D_MODEL), F32), jax.ShapeDtypeStruct((1, D_MODEL), F32)],
        grid=(lp // CHUNK,),
        in_specs=[pl.BlockSpec((CHUNK, D_MODEL), lambda i: (i, 0)),
                  pl.BlockSpec((CHUNK, D_MODEL), lambda i: (jnp.maximum(i - 1, 0), 0))],
        out_specs=[pl.BlockSpec((CHUNK, D_MODEL), lambda i: (i, 0)),
                   pl.BlockSpec((1, D_MODEL), lambda i: (0, 0))],
        compiler_params=_cparams(dimension_semantics=("arbitrary",)),
    )(h, target)


def _pad_cols(w, n):
    return jnp.pad(w, [(0, 0)] * (w.ndim - 1) + [(0, n - w.shape[-1])])


def _layer_slab(name, i):
    return i if name.startswith('mlp_') else i // 2


def _prep_matrix(key, raw_any):
    def raw(n):
        r = raw_any(n)
        return r[0][r[1]] if isinstance(r, tuple) else r

    if key in ('ssd_out', 'up', 'down'):
        r = raw_any({'ssd_out': 'ssd_w_out', 'up': 'mlp_w_up', 'down': 'mlp_w_down'}[key])
        return (r[0].astype(BF16), r[1]) if isinstance(r, tuple) else r.astype(BF16)
    if key == 'ssd_in':
        return _pad_cols(raw('ssd_w_in'), SSD_IN_PAD).astype(BF16)
    if key == 'mla_in':
        wi = raw('mla_w_in')
        kpe = jnp.pad(wi[:, MLA_Q_RANK + MLA_KV_RANK:], ((0, 0), (MLA_NOPE, HEAD_SLOT - MLA_QK)))
        return jnp.concatenate(
            [wi[:, MLA_Q_RANK:MLA_Q_RANK + MLA_KV_RANK], kpe, wi[:, :MLA_Q_RANK]], axis=1).astype(BF16)
    if key == 'mla_qb':
        qb = raw('mla_w_q_b').reshape(MLA_Q_RANK, MLA_HEADS, MLA_QK)
        return _pad_cols(qb, HEAD_SLOT).reshape(MLA_Q_RANK, MLA_WIDE).astype(BF16)
    if key == 'mla_kvb':
        kvb = raw('mla_w_kv_b').reshape(MLA_KV_RANK, MLA_HEADS, MLA_NOPE + MLA_V)
        kn = _pad_cols(kvb[:, :, :MLA_NOPE], HEAD_SLOT).reshape(MLA_KV_RANK, MLA_WIDE)
        vv = _pad_cols(kvb[:, :, MLA_NOPE:], HEAD_SLOT).reshape(MLA_KV_RANK, MLA_WIDE)
        return jnp.concatenate([kn, vv], axis=1).astype(BF16)
    assert key == 'mla_out'
    wo = raw('mla_w_out').reshape(MLA_HEADS, MLA_V, D_MODEL)
    return jnp.pad(wo, ((0, 0), (0, HEAD_SLOT - MLA_V), (0, 0))).reshape(MLA_WIDE, D_MODEL).astype(BF16)


class _Matrices:
    def __init__(self):
        self.p = {k: _Slabs(k, self) for k in ('ssd_in', 'ssd_out', 'mla_in', 'mla_qb', 'mla_kvb',
                                               'mla_out', 'up', 'down')}
        self.made = {}

    def matrix(self, key, slab):
        if (key, slab) not in self.made:
            self.made[(key, slab)] = _prep_matrix(key, lambda n: self.raw(n, slab))
        return self.made[(key, slab)]


class _Slabs:
    def __init__(self, key, owner):
        self.key, self.owner = key, owner

    def __getitem__(self, slab):
        return self.owner.matrix(self.key, slab)


class _ReadyWeights(_Matrices):
    def __init__(self, w):
        super().__init__()
        self.w = w

    def raw(self, name, slab):
        return self.w[name][slab]

    def start(self):
        pass

    def rider(self, host):
        return None

    def deliver(self, host, outs):
        assert not outs


class _KeepGrads:
    def __init__(self):
        self.rounds = {}

    def begin(self, r, grads):
        self.rounds[r] = grads
        return None

    def middle(self, r, recv):
        return None

    def finish(self, r, outs):
        assert not outs

    def result(self):
        names = {n for g in self.rounds.values() for n in g}
        return {n: jnp.concatenate([self.rounds[r][n] for r in sorted(self.rounds, reverse=True)
                                    if n in self.rounds[r]], axis=0) for n in names}


def _pad128(v):
    return _pad_cols(v.reshape(1, -1), 128)


def _sqrelu(u):
    r = jnp.maximum(u, 0.0)
    return r * r


def _local_step(x, target, w, big=None, red=None):
    seq = x.shape[0]
    lp = NPAD + N_META + seq
    big = _ReadyWeights(w) if big is None else big
    p = big.p
    h = jnp.concatenate([jnp.zeros((NPAD, D_MODEL), F32), w['meta_tokens'], x], axis=0)
    cos_t, sin_t = _rope_tables(lp)
    rt = _pick(lp, (384, 256, 128))
    saved = []
    big.start()
    for i in range(4):
        j = i // 2
        s = {'h0': h}
        g_mix = w['ln_mix'][i].reshape(1, -1)
        g_mlp = w['ln_mlp'][i].reshape(1, -1)
        if i == 0:
            hn = _rms_fwd(h, g_mix, f"rms_mix_f{i}")
        s['hn'] = hn
        if i % 2 == 0:
            rid = big.rider(f"ssd_in_f{i}")
            zxd = _mm(hn, p['ssd_in'][j], 'nn', name=f"ssd_in_f{i}", rider=rid)
            if rid is not None:
                zxd, *got = zxd
                big.deliver(f"ssd_in_f{i}", got)
            consts = _ssd_consts(w['ssd_conv_w'][j], w['ssd_conv_b'][j].reshape(1, -1),
                                 _pad128(w['ssd_dt_bias'][j]), _pad128(w['ssd_a_log'][j]),
                                 _pad128(w['ssd_d'][j]), w['ssd_norm'][j].reshape(1, -1))
            yg, states, *got = _ssd_fwd(zxd, consts, f"ssd_core_f{i}", rider=big.rider(f"ssd_core_f{i}"))
            big.deliver(f"ssd_core_f{i}", got)
            s.update(zxd=zxd, consts=consts, yg=yg, states=states)
            h, hn2 = _mm(yg, p['ssd_out'][j], 'nn', name=f"ssd_out_f{i}", epi=lambda r, hv: hv + r,
                         extras=(h,), norm_gain=g_mlp)
        else:
            lat = _mm(hn, p['mla_in'][j], 'nn', name=f"mla_in_f{i}")
            kvg = w['mla_kv_a_norm'][j].reshape(1, -1)
            qag = w['mla_q_a_norm'][j].reshape(1, -1)
            kvn, qn = _row_call(lambda _, a, b, c, d: _lat_norm(a, b, c, d),
                                [(lat, MLA_KV_RANK, 0), (lat, MLA_Q_RANK, 1)], [kvg, qag],
                                [(MLA_KV_RANK, BF16), (MLA_Q_RANK, BF16)], n_rows=lp, tile=rt,
                                name=f"mla_latnorm_f{i}")
            q_raw = _mm(qn, p['mla_qb'][j], 'nn', name=f"mla_qb_f{i}")
            kv_raw = _mm(kvn, p['mla_kvb'][j], 'nn', name=f"mla_kvb_f{i}")
            qg = _pad_cols(w['mla_q_norm'][j].reshape(1, -1), HEAD_SLOT)
            kg = _pad_cols(w['mla_k_norm'][j].reshape(1, -1), HEAD_SLOT)

            def prep_fwd(_, qr, kn, kpe, vv, cs, sn, qgv, kgv):
                qq, kk = _qk_prep(qr, kn, kpe, cs, sn, qgv, kgv)
                ones = lax.broadcasted_iota(jnp.int32, vv.shape, 1) % HEAD_SLOT == MLA_V
                return qq, kk, jnp.where(ones, 1.0, vv)

            q, k, v = _row_call(prep_fwd,
                                [(q_raw, MLA_WIDE, 0), (kv_raw, MLA_WIDE, 0), (lat, HEAD_SLOT, 2),
                                 (kv_raw, MLA_WIDE, 1), (cos_t, HEAD_SLOT, 0), (sin_t, HEAD_SLOT, 0)],
                                [qg, kg], [(MLA_WIDE, BF16)] * 3, n_rows=lp, tile=rt,
                                name=f"mla_qkprep_f{i}")
            o, lse, *got = _attn_fwd(q, k, v, f"mla_attn_f{i}", rider=big.rider(f"mla_attn_f{i}"))
            big.deliver(f"mla_attn_f{i}", got)
            s.update(lat=lat, kvg=kvg, qag=qag, kvn=kvn, qn=qn, q_raw=q_raw, kv_raw=kv_raw, qg=qg, kg=kg,
                     q=q, k=k, v=v, o=o, lse=lse)
            h, hn2 = _mm(o, p['mla_out'][j], 'nn', name=f"mla_out_f{i}", epi=lambda r, hv: hv + r,
                         extras=(h,), norm_gain=g_mlp)
        s['h1'] = h
        u = _mm(hn2, p['up'][i], 'nn', name=f"mlp_up_f{i}", out_dtype=BF16)
        if i < 3:
            h, hn = _mm(u, p['down'][i], 'nn', name=f"mlp_down_f{i}", a_fn=_sqrelu, epi=lambda r, hv: hv + r,
                        extras=(h,), norm_gain=w['ln_mix'][i + 1].reshape(1, -1))
        else:
            h = _mm(u, p['down'][i], 'nn', name=f"mlp_down_f{i}", a_fn=_sqrelu,
                    epi=lambda r, hv: hv + r, extras=(h,))
        s.update(hn2=hn2, u=u, g_mix=g_mix, g_mlp=g_mlp)
        saved.append(s)

    dh, loss_row = _loss_head(h, target, "loss_head")

    large = {n for n, _ in BIG}
    g = {k_: [None] * (4 if k_ in ('ln_mix', 'ln_mlp') else 2)
         for k_ in ALL_NAMES if k_ != 'meta_tokens' and k_ not in large}
    red = _KeepGrads() if red is None else red
    rounds, pending = {}, None

    def round_of(nm, i):
        return next(r for r, spec in enumerate(REDUCE_ROUNDS)
                    for n, l0, l1 in spec if n == nm and l0 <= _layer_slab(nm, i) < l1)

    def slabs_in(nm, r):
        return next((l0, l1) for n, l0, l1 in REDUCE_ROUNDS[r] if n == nm)

    swapping = None

    def dw_into(nm, i, a, b, **kw):
        nonlocal swapping, pending
        r = round_of(nm, i)
        (l0, l1), cur = slabs_in(nm, r), rounds.setdefault(r, {})
        stack = (l1 - l0, _layer_slab(nm, i) - l0, cur.get(nm))
        if swapping is None:
            cur[nm] = _mm(a, b, 'tn', stack=stack, **kw)
        else:
            (r0, rider), swapping = swapping, None
            cur[nm], *recv = _mm(a, b, 'tn', stack=stack, rider=rider, **kw)
            pending = (r0, red.middle(r0, recv))

    def put(nm, i, arr):
        rounds.setdefault(round_of(nm, i), {})[nm] = arr[None]

    def hand_over(r):
        nonlocal pending, swapping
        swap = red.begin(r, rounds.pop(r))
        if swap is None:
            pending = (r, None)
        elif r == 0:
            swapping = (r, swap)
        else:
            pending = (r, red.middle(r, _run_rider(swap, f"rs_swap{r}")))

    def host(fn, *args):
        nonlocal pending
        if pending is None or pending[1] is None:
            return fn(*args)
        (r, rider), pending = pending, None
        outs = fn(*args, rider=rider)
        own = len(outs) - len(rider.out_shapes)
        red.finish(r, outs[own:])
        return outs[:own]

    for i in reversed(range(4)):
        j = i // 2
        s = saved[i]
        dw_into('mlp_w_down', i, s['u'], dh, name=f"mlp_down_dw{i}", a_fn=_sqrelu)
        du = _mm(dh, p['down'][i], 'nt', name=f"mlp_down_dx{i}", out_dtype=BF16,
                 epi=lambda r, uv: r * (2.0 * jnp.maximum(uv, 0.0)), extras=(s['u'],))
        dw_into('mlp_w_up', i, s['hn2'], du, name=f"mlp_up_dw{i}")
        dh, dg = _mm_rms_bwd(du, p['up'][i], s['h1'], dh, s['g_mlp'], f"mlp_up_dx{i}")
        g['ln_mlp'][i] = dg[0]
        if i % 2 == 0:
            dw_into('ssd_w_out', i, s['yg'], dh, name=f"ssd_out_dw{i}")
            d_yg = _mm(dh, p['ssd_out'][j], 'nt', name=f"ssd_out_dx{i}")
            if i == 0:
                hand_over(1)
            d_zxd, dcw, dcb, ddtb, dalog, ddsk, dng = host(_ssd_bwd, s['zxd'], s['states'], d_yg, s['consts'],
                                                           f"ssd_core_b{i}")
            g['ssd_conv_w'][j], g['ssd_conv_b'][j], g['ssd_norm'][j] = dcw, dcb[0], dng[0]
            g['ssd_dt_bias'][j], g['ssd_a_log'][j], g['ssd_d'][j] = (
                ddtb[0, :SSD_HEADS], dalog[0, :SSD_HEADS], ddsk[0, :SSD_HEADS])
            dw_into('ssd_w_in', i, s['hn'], d_zxd, name=f"ssd_in_dw{i}")
            dh, dg = _mm_rms_bwd(d_zxd, p['ssd_in'][j], s['h0'], dh, s['g_mix'], f"ssd_in_dx{i}")
        else:
            wo = _mm(s['o'], dh, 'tn', name=f"mla_out_dw{i}")
            put('mla_w_out', i, wo.reshape(MLA_HEADS, HEAD_SLOT, D_MODEL)[:, :MLA_V].reshape(-1, D_MODEL))
            dob, delta = _mm_attn_do(dh, p['mla_out'][j], s['o'], f"mla_out_dx{i}")
            dq, dk, dv = host(_attn_bwd, s['q'], s['k'], s['v'], dob, s['lse'], delta, f"mla_attn_b{i}")

            def prep_bwd(_, qr, kn, kpe, cs, sn, dqv, dkv, dvv, qgv, kgv):
                _, vjp = jax.vjp(lambda a, b, c, d, e: _qk_prep(a, b, c, cs, sn, d, e), qr, kn, kpe, qgv, kgv)
                d_qr, d_kn, d_kpe, d_qg, d_kg = vjp((dqv, dkv))
                return d_qr, jnp.concatenate([d_kn, dvv], axis=1), d_kpe, d_qg, d_kg

            d_qraw, d_kvraw, d_kpe, d_qg, d_kg = _row_call(
                prep_bwd,
                [(s['q_raw'], MLA_WIDE, 0), (s['kv_raw'], MLA_WIDE, 0), (s['lat'], HEAD_SLOT, 2),
                 (cos_t, HEAD_SLOT, 0), (sin_t, HEAD_SLOT, 0), (dq, MLA_WIDE, 0), (dk, MLA_WIDE, 0),
                 (dv, MLA_WIDE, 0)],
                [s['qg'], s['kg']], [(MLA_WIDE, BF16), (2 * MLA_WIDE, BF16), (HEAD_SLOT, F32)],
                [(1, HEAD_SLOT), (1, HEAD_SLOT)], n_rows=lp, tile=_pick(lp, (128,)), name=f"mla_qkprep_b{i}")
            g['mla_q_norm'][j], g['mla_k_norm'][j] = d_qg[0, :MLA_QK], d_kg[0, :MLA_QK]
            wqb = _mm(s['qn'], d_qraw, 'tn', name=f"mla_qb_dw{i}")
            put('mla_w_q_b', i, wqb.reshape(MLA_Q_RANK, MLA_HEADS, HEAD_SLOT)[:, :, :MLA_QK].reshape(MLA_Q_RANK, -1))
            d_qn = _mm(d_qraw, p['mla_qb'][j], 'nt', name=f"mla_qb_dx{i}")
            wkvb = _mm(s['kvn'], d_kvraw, 'tn', name=f"mla_kvb_dw{i}").reshape(MLA_KV_RANK, 2, MLA_HEADS, HEAD_SLOT)
            put('mla_w_kv_b', i, jnp.concatenate([wkvb[:, 0, :, :MLA_NOPE], wkvb[:, 1, :, :MLA_V]],
                                                 axis=-1).reshape(MLA_KV_RANK, -1))
            d_kvn = _mm(d_kvraw, p['mla_kvb'][j], 'nt', name=f"mla_kvb_dx{i}")

            def lat_bwd(_, kvl, ql, dkvn, dqn, dkpe, kvgv, qagv):
                _, vjp = jax.vjp(_lat_norm, kvl, ql, kvgv, qagv)
                d_kvl, d_ql, d_kvg, d_qag = vjp((dkvn, dqn))
                return jnp.concatenate([d_kvl, dkpe, d_ql], axis=1), d_kvg, d_qag

            d_lat, d_kvg, d_qag = _row_call(
                lat_bwd, [(s['lat'], MLA_KV_RANK, 0), (s['lat'], MLA_Q_RANK, 1), (d_kvn, MLA_KV_RANK, 0),
                          (d_qn, MLA_Q_RANK, 0), (d_kpe, HEAD_SLOT, 0)],
                [s['kvg'], s['qag']], [(LAT_PAD, BF16)], [(1, MLA_KV_RANK), (1, MLA_Q_RANK)],
                n_rows=lp, tile=rt, name=f"mla_latnorm_b{i}")
            g['mla_kv_a_norm'][j], g['mla_q_a_norm'][j] = d_kvg[0], d_qag[0]
            win = _mm(s['hn'], d_lat, 'tn', name=f"mla_in_dw{i}")
            put('mla_w_in', i, jnp.concatenate(
                [win[:, MLA_KV_RANK + HEAD_SLOT:], win[:, :MLA_KV_RANK],
                 win[:, MLA_KV_RANK + MLA_NOPE:MLA_KV_RANK + MLA_QK]], axis=1))
            dh, dg = _mm_rms_bwd(d_lat, p['mla_in'][j], s['h0'], dh, s['g_mix'], f"mla_in_dx{i}")
        g['ln_mix'][i] = dg[0]
        if i == 2:
            hand_over(0)
    hand_over(2)

    if pending[1] is not None:
        red.finish(pending[0], _run_rider(pending[1], "rs_exchange_last"))
    grads = {k_: jnp.stack(v_) for k_, v_ in g.items()}
    grads['meta_tokens'] = dh[NPAD:NPAD + N_META]
    return loss_row, dh[NPAD + N_META:], grads, red


def _all_gather8(shard, name):
    m_per, n = shard.shape

    def body(x_ref, out_ref, send_sems, recv_sems, local_sem):
        x, y, c = lax.axis_index("x"), lax.axis_index("y"), lax.axis_index("c")
        me, sibling = (x, y, c), (x, y, 1 - c)
        chips = [(1 - x, y), (x, 1 - y), (1 - x, 1 - y)]

        def rows(px, py, pc):
            return out_ref.at[pl.ds((4 * px + 2 * py + pc) * m_per, m_per), :]

        def copy(k, block, to, src=None):
            return pltpu.make_async_remote_copy(
                src_ref=rows(*block) if src is None else src, dst_ref=rows(*block),
                send_sem=send_sems.at[k], recv_sem=recv_sems.at[k], device_id=to, device_id_type=MESH)

        mine = pltpu.make_async_copy(x_ref, rows(*me), local_sem)
        mine.start()
        first = [copy(0, me, sibling, src=x_ref)]
        first += [copy(1 + j, me, (*chip, c), src=x_ref) for j, chip in enumerate(chips)]
        for cp in first:
            cp.start()
        passed = [copy(4 + j, (*chip, c), sibling) for j, chip in enumerate(chips)]
        for j, chip in enumerate(chips):
            copy(1 + j, (*chip, c), me).wait_recv()
            passed[j].start()
        copy(0, sibling, me).wait_recv()
        for j, chip in enumerate(chips):
            copy(4 + j, (*chip, 1 - c), me).wait_recv()
        for cp in first + passed:
            cp.wait_send()
        mine.wait()

    return pl.pallas_call(
        body, name=name,
        out_shape=jax.ShapeDtypeStruct((8 * m_per, n), shard.dtype),
        in_specs=[pl.BlockSpec(memory_space=pl.ANY)],
        out_specs=pl.BlockSpec(memory_space=pl.ANY),
        scratch_shapes=[pltpu.SemaphoreType.DMA((7,)), pltpu.SemaphoreType.DMA((7,)), pltpu.SemaphoreType.DMA],
    )(shard)


def _mesh_pos():
    return lax.axis_index("x"), lax.axis_index("y"), lax.axis_index("c")


def _half_rows(pc, h):
    return pl.ds(pl.multiple_of(pc * h, 16), h)


def _whole_view(ref, kind, shard_shape, k, pc):
    _, r, c = shard_shape
    rows = _half_rows(pc, r // 2)
    if kind == 'row':
        return ref.at[:, k, rows, :]
    if kind == 'col':
        return ref.at[:, rows, pl.ds(pl.multiple_of(k * c, 128), c)]
    return ref.at[k, :, rows, :]


def _whole_shape(kind, shard_shape, rows=None):
    l, r, c = shard_shape
    r = r if rows is None else rows
    return {'row': (l, 4, r, c), 'col': (l, r, 4 * c), 'colx': (4, l, r, c)}[kind]


def _gather_rider(shards, kinds):
    n = len(shards)
    shapes = [s.shape for s in shards]

    def plan(ins, outs, sems):
        send_sems, recv_sems, local_sems = sems
        x, y, c = _mesh_pos()
        me, sibling = (x, y, c), (x, y, 1 - c)
        chips = [(1 - x, y), (x, 1 - y), (1 - x, 1 - y)]

        def place(a, px, py, pc):
            return _whole_view(outs[a], kinds[a], shapes[a], 2 * px + py, pc)

        def own(a):
            return ins[a].at[:, _half_rows(c, shapes[a][1] // 2), :]

        def copy(a, k, block, to, src=None):
            return pltpu.make_async_remote_copy(
                src_ref=place(a, *block) if src is None else src, dst_ref=place(a, *block),
                send_sem=send_sems.at[7 * a + k], recv_sem=recv_sems.at[7 * a + k],
                device_id=to, device_id_type=MESH)

        mine = [pltpu.make_async_copy(own(a), place(a, *me), local_sems.at[a]) for a in range(n)]
        first = [copy(a, 1 + j, me, (*chip, c), src=own(a)) for j, chip in enumerate(chips) for a in range(n)]
        first += [copy(a, 0, me, sibling, src=own(a)) for a in range(n)]
        return copy, mine, first, chips, me, sibling, c

    def start(ins, outs, sems):
        _, mine, first, *_ = plan(ins, outs, sems)
        for cp in first + mine:
            cp.start()

    def finish(ins, outs, sems):
        copy, mine, first, chips, me, sibling, c = plan(ins, outs, sems)
        passed = []
        for j, chip in enumerate(chips):
            for a in range(n):
                copy(a, 1 + j, (*chip, c), me).wait_recv()
                passed.append(copy(a, 4 + j, (*chip, c), sibling))
                passed[-1].start()
        for a in range(n):
            copy(a, 0, sibling, me).wait_recv()
        for j, chip in enumerate(chips):
            for a in range(n):
                copy(a, 4 + j, (*chip, 1 - c), me).wait_recv()
        for cp in first + passed:
            cp.wait_send()
        for cp in mine:
            cp.wait()

    return _Rider(
        shards, [jax.ShapeDtypeStruct(_whole_shape(k, s.shape), s.dtype) for k, s in zip(kinds, shards)],
        [pltpu.SemaphoreType.DMA((7 * n,)), pltpu.SemaphoreType.DMA((7 * n,)), pltpu.SemaphoreType.DMA((n,))],
        start, finish)


def _run_rider(rider, name):
    ni, no = len(rider.operands), len(rider.out_shapes)

    def body(*refs):
        ride = (refs[:ni], refs[ni:ni + no], refs[ni + no:])
        rider.start(*ride)
        rider.finish(*ride)

    return pl.pallas_call(
        body, name=name, out_shape=rider.out_shapes,
        in_specs=[pl.BlockSpec(memory_space=pl.ANY)] * ni,
        out_specs=[pl.BlockSpec(memory_space=pl.ANY)] * no,
        scratch_shapes=rider.scratch,
    )(*rider.operands)


def _swap_rider(wholes, kinds, shapes):
    n = len(wholes)

    def plan(ins, outs, sems):
        send_sems, recv_sems = sems
        x, y, c = _mesh_pos()
        cps = []
        for a in range(n):
            rows = _half_rows(1 - c, shapes[a][1] // 2)
            src = ins[a].at[:, rows, :] if kinds[a] == 'col' else ins[a].at[:, :, rows, :]
            cps.append(pltpu.make_async_remote_copy(
                src_ref=src, dst_ref=outs[a], send_sem=send_sems.at[a], recv_sem=recv_sems.at[a],
                device_id=(x, y, 1 - c), device_id_type=MESH))
        return cps

    def start(ins, outs, sems):
        for cp in plan(ins, outs, sems):
            cp.start()

    def finish(ins, outs, sems):
        for cp in plan(ins, outs, sems):
            cp.wait()

    return _Rider(
        wholes, [jax.ShapeDtypeStruct(_whole_shape(k, s, s[1] // 2), w.dtype)
                 for k, s, w in zip(kinds, shapes, wholes)],
        [pltpu.SemaphoreType.DMA((n,)), pltpu.SemaphoreType.DMA((n,))], start, finish)


def _exchange_rider(parts, kinds, shapes):
    n = len(parts)

    def plan(ins, outs, sems):
        send_sems, recv_sems, local_sems = sems
        x, y, c = _mesh_pos()
        kme = 2 * x + y
        chips = [(1 - x, y), (x, 1 - y), (1 - x, 1 - y)]

        def slab(a, k):
            if kinds[a] == 'row':
                return ins[a].at[:, k]
            if kinds[a] == 'col':
                cw = shapes[a][2]
                return ins[a].at[:, :, pl.ds(pl.multiple_of(k * cw, 128), cw)]
            return ins[a].at[k]

        cps = [pltpu.make_async_remote_copy(
            src_ref=slab(a, 2 * px + py), dst_ref=outs[a].at[kme], send_sem=send_sems.at[3 * a + j],
            recv_sem=recv_sems.at[3 * a + j], device_id=(px, py, c), device_id_type=MESH)
            for j, (px, py) in enumerate(chips) for a in range(n)]
        return cps + [pltpu.make_async_copy(slab(a, kme), outs[a].at[kme], local_sems.at[a]) for a in range(n)]

    def start(ins, outs, sems):
        for cp in plan(ins, outs, sems):
            cp.start()

    def finish(ins, outs, sems):
        for cp in plan(ins, outs, sems):
            cp.wait()

    return _Rider(
        parts, [jax.ShapeDtypeStruct((4, s[0], s[1] // 2, s[2]), p.dtype) for s, p in zip(shapes, parts)],
        [pltpu.SemaphoreType.DMA((3 * n,)), pltpu.SemaphoreType.DMA((3 * n,)), pltpu.SemaphoreType.DMA((n,))],
        start, finish)


def _rs_share(shards, slabs, name):
    n = len(shards)

    def body(*refs):
        outs = refs[n:2 * n]
        send_sems, recv_sems = refs[2 * n:]
        x, y, c = _mesh_pos()
        cps = []
        for a in range(n):
            l0, l1 = slabs[a]
            rows = outs[a].at[pl.ds(l0, l1 - l0), _half_rows(c, shards[a].shape[1] // 2), :]
            cps.append(pltpu.make_async_remote_copy(
                src_ref=rows, dst_ref=rows, send_sem=send_sems.at[a], recv_sem=recv_sems.at[a],
                device_id=(x, y, 1 - c), device_id_type=MESH))
        for cp in cps:
            cp.start()
        for cp in cps:
            cp.wait()

    return pl.pallas_call(
        body, name=name,
        out_shape=[jax.ShapeDtypeStruct(s.shape, s.dtype) for s in shards],
        in_specs=[pl.BlockSpec(memory_space=pl.ANY)] * n,
        out_specs=[pl.BlockSpec(memory_space=pl.ANY)] * n,
        input_output_aliases={a: a for a in range(n)},
        scratch_shapes=[pltpu.SemaphoreType.DMA((n,)), pltpu.SemaphoreType.DMA((n,))],
    )(*shards)


def _tile_rows(rows, cols, budget=2 * 1024 * 1024):
    for t in (1024, 512, 256, 128, 64, 32, 16, 8):
        if rows % t == 0 and t * cols * 4 <= budget:
            return t
    return rows


def _add_half(g3, r3, c_idx, name):
    a, h, n = r3.shape
    t = _tile_rows(h, n)
    nt = h // t

    def body(c_ref, g_ref, r_ref, o_ref):
        o_ref[...] = (g_ref[...] + r_ref[...]).astype(o_ref.dtype)

    return pl.pallas_call(
        body, name=name, out_shape=jax.ShapeDtypeStruct((a, h, n), BF16),
        grid_spec=pltpu.PrefetchScalarGridSpec(
            num_scalar_prefetch=1, grid=(a, nt),
            in_specs=[pl.BlockSpec((1, t, n), lambda k, i, c: (k, c[0] * nt + i, 0)),
                      pl.BlockSpec((1, t, n), lambda k, i, c: (k, i, 0))],
            out_specs=pl.BlockSpec((1, t, n), lambda k, i, c: (k, i, 0))),
        compiler_params=_cparams(dimension_semantics=("parallel", "parallel")),
    )(c_idx, g3, r3)


def _sum4(parts, c_idx, name, into):
    _, l, h, n = parts.shape
    n_slabs, l0, buf = into
    t = _tile_rows(h, n, 1024 * 1024)
    nt = h // t
    held = () if buf is None else (buf,)

    def body(c_ref, p_ref, *rest):
        pv = p_ref[...].astype(F32)
        rest[-1][...] = ((pv[0] + pv[1]) + pv[2]) + pv[3]

    return pl.pallas_call(
        body, name=name, out_shape=jax.ShapeDtypeStruct((n_slabs, 2 * h, n), F32),
        grid_spec=pltpu.PrefetchScalarGridSpec(
            num_scalar_prefetch=1, grid=(l, nt),
            in_specs=[pl.BlockSpec((4, 1, t, n), lambda k, i, c: (0, k, i, 0))]
            + [pl.BlockSpec(memory_space=pl.ANY)] * len(held),
            out_specs=pl.BlockSpec((1, t, n), lambda k, i, c: (l0 + k, c[0] * nt + i, 0))),
        input_output_aliases={2: 0} if held else {},
        compiler_params=_cparams(dimension_semantics=("parallel", "parallel")),
    )(c_idx, parts, *held)


def _sum8(parts, name):
    _, m, n = parts.shape

    def body(p_ref, o_ref):
        acc = p_ref[0]
        for d in range(1, 8):
            acc = acc + p_ref[d]
        o_ref[...] = acc

    return pl.pallas_call(body, name=name, out_shape=jax.ShapeDtypeStruct((m, n), F32))(parts)


def _adamw(wp, gp, mp, vp, name):
    r, n = wp.shape
    t = _tile_rows(r, n, 1024 * 1024)

    def body(w_ref, g_ref, m_ref, v_ref, d_ref, mo_ref, vo_ref):
        gv = g_ref[...]
        m2 = ADAM_B1 * m_ref[...] + (1.0 - ADAM_B1) * gv
        v2 = ADAM_B2 * v_ref[...] + (1.0 - ADAM_B2) * (gv * gv)
        m_hat = m2 / (1.0 - ADAM_B1 ** ADAM_STEP)
        v_hat = v2 / (1.0 - ADAM_B2 ** ADAM_STEP)
        d_ref[...] = -ADAM_LR * (m_hat / (jnp.sqrt(v_hat) + ADAM_EPS) + ADAM_WD * w_ref[...])
        mo_ref[...] = m2
        vo_ref[...] = v2

    spec = pl.BlockSpec((t, n), lambda i: (i, 0))
    return pl.pallas_call(
        body, name=name, out_shape=[jax.ShapeDtypeStruct((r, n), F32)] * 3, grid=(r // t,),
        in_specs=[spec] * 4, out_specs=[spec] * 3,
        compiler_params=_cparams(dimension_semantics=("parallel",)),
    )(wp, gp, mp, vp)


BIG = (('ssd_w_in', 'colx'), ('ssd_w_out', 'row'), ('mla_w_in', 'row'), ('mla_w_q_b', 'col'),
       ('mla_w_kv_b', 'col'), ('mla_w_out', 'row'), ('mlp_w_up', 'col'), ('mlp_w_down', 'row'))
SMALL_SHARDED = (('meta_tokens', 1), ('ssd_conv_w', 2), ('mla_q_a_norm', 1), ('mla_kv_a_norm', 1))
SMALL_REPL = ('ln_mix', 'ln_mlp', 'ssd_conv_b', 'ssd_dt_bias', 'ssd_a_log', 'ssd_d', 'ssd_norm',
              'mla_q_norm', 'mla_k_norm')
ALL_NAMES = ('meta_tokens', 'ln_mix', 'ln_mlp', 'ssd_w_in', 'ssd_conv_w', 'ssd_conv_b', 'ssd_dt_bias',
             'ssd_a_log', 'ssd_d', 'ssd_norm', 'ssd_w_out', 'mla_w_in', 'mla_q_a_norm', 'mla_w_q_b',
             'mla_kv_a_norm', 'mla_w_kv_b', 'mla_q_norm', 'mla_k_norm', 'mla_w_out', 'mlp_w_up', 'mlp_w_down')


_MLA_BIG = ('mla_w_in', 'mla_w_q_b', 'mla_w_kv_b', 'mla_w_out')
GATHER_ROUNDS = (
    (('ssd_w_in', 0, 1),),
    (('ssd_w_out', 0, 1), ('mlp_w_up', 0, 1)),
    (('mlp_w_down', 0, 1),) + tuple((n, 0, 1) for n in _MLA_BIG) + (('mlp_w_up', 1, 2), ('mlp_w_down', 1, 2)),
    (('ssd_w_in', 1, 2), ('ssd_w_out', 1, 2)) + tuple((n, 1, 2) for n in _MLA_BIG)
    + (('mlp_w_up', 2, 4), ('mlp_w_down', 2, 4)),
)
GATHER_HOSTS = {'ssd_in_f0': 1, 'ssd_core_f0': 2, 'mla_attn_f1': 3}


REDUCE_ROUNDS = (
    GATHER_ROUNDS[3],
    tuple((n, 0, 1) for n in _MLA_BIG) + (('mlp_w_up', 0, 2), ('mlp_w_down', 0, 2), ('ssd_w_out', 0, 1)),
    (('ssd_w_in', 0, 1),),
)


class _GatheredWeights(_Matrices):
    def __init__(self, shards):
        super().__init__()
        self.shards, self.whole = shards, {}

    def raw(self, name, slab):
        return self.whole[(name, slab)]

    def _round(self, r):
        spec = GATHER_ROUNDS[r]
        return _gather_rider([self.shards[n][l0:l1].astype(BF16) for n, l0, l1 in spec],
                             [dict(BIG)[n] for n, _, _ in spec])

    def _take(self, r, outs):
        for (n, l0, l1), o in zip(GATHER_ROUNDS[r], outs):
            kind = dict(BIG)[n]
            if kind == 'row':
                o = o.reshape(o.shape[0], -1, o.shape[-1])
            for l in range(l0, l1):
                if kind == 'colx':
                    self.whole[(n, l)] = jnp.concatenate([o[k, l - l0] for k in range(4)], axis=-1)
                else:
                    self.whole[(n, l)] = (o, l - l0)

    def start(self):
        self._take(0, _run_rider(self._round(0), "gather_first"))

    def rider(self, host):
        return self._round(GATHER_HOSTS[host]) if host in GATHER_HOSTS else None

    def deliver(self, host, outs):
        if host in GATHER_HOSTS:
            self._take(GATHER_HOSTS[host], outs)


class _ScatterGrads:
    def __init__(self, shard_shapes, c_idx):
        self.shard_shapes, self.c_idx, self.out = shard_shapes, c_idx, {}

    def begin(self, r, grads):
        spec = REDUCE_ROUNDS[r]
        kinds = [dict(BIG)[n] for n, _, _ in spec]
        shapes = [(l1 - l0,) + tuple(self.shard_shapes[n][1:]) for n, l0, l1 in spec]
        wholes = []
        for (n, _, _), kind, s in zip(spec, kinds, shapes):
            if kind == 'row':
                wholes.append(grads[n].reshape(s[0], 4, s[1], s[2]))
            elif kind == 'col':
                wholes.append(grads[n])
            else:
                wholes.append(jnp.stack([grads[n][..., k * s[2]:(k + 1) * s[2]] for k in range(4)]))
        self.swapping = (kinds, shapes, wholes)
        return _swap_rider(wholes, kinds, shapes)

    def middle(self, r, recv):
        spec = REDUCE_ROUNDS[r]
        kinds, shapes, wholes = self.swapping
        parts = []
        for (n, _, _), kind, s, gw, rc in zip(spec, kinds, shapes, wholes, recv):
            if kind == 'col':
                g3, r3 = gw, rc
            else:
                g3, r3 = gw.reshape(-1, s[1], s[2]), rc.reshape(-1, s[1] // 2, s[2])
            parts.append(_add_half(g3, r3, self.c_idx, f"rs_add{r}_{n}").reshape(rc.shape))
        return _exchange_rider(parts, kinds, shapes)

    def finish(self, r, outs):
        spec = REDUCE_ROUNDS[r]
        for (n, l0, _), part in zip(spec, outs):
            self.out[n] = _sum4(part, self.c_idx, f"rs_sum{r}_{n}",
                                into=(self.shard_shapes[n][0], l0, self.out.get(n)))
        shared = _rs_share([self.out[n] for n, _, _ in spec], [(l0, l1) for _, l0, l1 in spec], f"rs_share{r}")
        self.out.update(zip([n for n, _, _ in spec], shared))


def _pack(arrs, rows_mult):
    flat = jnp.concatenate([a.reshape(-1) for a in arrs])
    per = LANES * rows_mult
    pad = (-flat.shape[0]) % per
    if pad:
        flat = jnp.concatenate([flat, jnp.zeros((pad,), flat.dtype)])
    return flat.reshape(-1, LANES)


def _unpack(pack, shapes):
    flat = pack.reshape(-1)
    out, off = [], 0
    for shp in shapes:
        n = math.prod(shp)
        out.append(flat[off:off + n].reshape(shp))
        off += n
    return out


def _split4(full, axis):
    shp = full.shape
    r = full.reshape(shp[:axis] + (4, shp[axis] // 4) + shp[axis + 1:])
    return jnp.moveaxis(r, axis, 0)


def _join4(parts, axis):
    r = jnp.moveaxis(parts, 0, axis)
    shp = r.shape
    return r.reshape(shp[:axis] + (shp[axis] * shp[axis + 1],) + shp[axis + 2:])


def _gather_params(shards, table, dtype, c, name):
    pack = _pack([shards[n].astype(dtype) for n, _ in table], 16)
    half = pack.shape[0] // 2
    mine = lax.dynamic_slice_in_dim(pack, c * half, half, axis=0)
    full = _all_gather8(mine, name).reshape(4, -1)
    out, off = {}, 0
    for n, ax in table:
        cnt = math.prod(shards[n].shape)
        out[n] = _join4(full[:, off:off + cnt].reshape((4,) + shards[n].shape), ax)
        off += cnt
    return out


def kernel(x, meta_tokens, ln_mix, ln_mlp, ssd_w_in, ssd_conv_w, ssd_conv_b, ssd_dt_bias, ssd_a_log, ssd_d, ssd_norm, ssd_w_out, mla_w_in, mla_q_a_norm, mla_w_q_b, mla_kv_a_norm, mla_w_kv_b, mla_q_norm, mla_k_norm, mla_w_out, mlp_w_up, mlp_w_down, loss_target, m_meta_tokens, m_ln_mix, m_ln_mlp, m_ssd_w_in, m_ssd_conv_w, m_ssd_conv_b, m_ssd_dt_bias, m_ssd_a_log, m_ssd_d, m_ssd_norm, m_ssd_w_out, m_mla_w_in, m_mla_q_a_norm, m_mla_w_q_b, m_mla_kv_a_norm, m_mla_w_kv_b, m_mla_q_norm, m_mla_k_norm, m_mla_w_out, m_mlp_w_up, m_mlp_w_down, v_meta_tokens, v_ln_mix, v_ln_mlp, v_ssd_w_in, v_ssd_conv_w, v_ssd_conv_b, v_ssd_dt_bias, v_ssd_a_log, v_ssd_d, v_ssd_norm, v_ssd_w_out, v_mla_w_in, v_mla_q_a_norm, v_mla_w_q_b, v_mla_kv_a_norm, v_mla_w_kv_b, v_mla_q_norm, v_mla_k_norm, v_mla_w_out, v_mlp_w_up, v_mlp_w_down):
    w_sh = dict(meta_tokens=meta_tokens, ln_mix=ln_mix, ln_mlp=ln_mlp, ssd_w_in=ssd_w_in, ssd_conv_w=ssd_conv_w, ssd_conv_b=ssd_conv_b, ssd_dt_bias=ssd_dt_bias, ssd_a_log=ssd_a_log, ssd_d=ssd_d, ssd_norm=ssd_norm, ssd_w_out=ssd_w_out, mla_w_in=mla_w_in, mla_q_a_norm=mla_q_a_norm, mla_w_q_b=mla_w_q_b, mla_kv_a_norm=mla_kv_a_norm, mla_w_kv_b=mla_w_kv_b, mla_q_norm=mla_q_norm, mla_k_norm=mla_k_norm, mla_w_out=mla_w_out, mlp_w_up=mlp_w_up, mlp_w_down=mlp_w_down)
    m_sh = dict(meta_tokens=m_meta_tokens, ln_mix=m_ln_mix, ln_mlp=m_ln_mlp, ssd_w_in=m_ssd_w_in, ssd_conv_w=m_ssd_conv_w, ssd_conv_b=m_ssd_conv_b, ssd_dt_bias=m_ssd_dt_bias, ssd_a_log=m_ssd_a_log, ssd_d=m_ssd_d, ssd_norm=m_ssd_norm, ssd_w_out=m_ssd_w_out, mla_w_in=m_mla_w_in, mla_q_a_norm=m_mla_q_a_norm, mla_w_q_b=m_mla_w_q_b, mla_kv_a_norm=m_mla_kv_a_norm, mla_w_kv_b=m_mla_w_kv_b, mla_q_norm=m_mla_q_norm, mla_k_norm=m_mla_k_norm, mla_w_out=m_mla_w_out, mlp_w_up=m_mlp_w_up, mlp_w_down=m_mlp_w_down)
    v_sh = dict(meta_tokens=v_meta_tokens, ln_mix=v_ln_mix, ln_mlp=v_ln_mlp, ssd_w_in=v_ssd_w_in, ssd_conv_w=v_ssd_conv_w, ssd_conv_b=v_ssd_conv_b, ssd_dt_bias=v_ssd_dt_bias, ssd_a_log=v_ssd_a_log, ssd_d=v_ssd_d, ssd_norm=v_ssd_norm, ssd_w_out=v_ssd_w_out, mla_w_in=v_mla_w_in, mla_q_a_norm=v_mla_q_a_norm, mla_w_q_b=v_mla_w_q_b, mla_kv_a_norm=v_mla_kv_a_norm, mla_w_kv_b=v_mla_w_kv_b, mla_q_norm=v_mla_q_norm, mla_k_norm=v_mla_k_norm, mla_w_out=v_mla_w_out, mlp_w_up=v_mlp_w_up, mlp_w_down=v_mlp_w_down)

    cx, cy, cc = lax.axis_index("x"), lax.axis_index("y"), lax.axis_index("c")
    chip = 2 * cx + cy

    c_idx = cc.reshape(1).astype(jnp.int32)
    big_names = [n for n, _ in BIG]
    shapes = [w_sh[n].shape for n in big_names]

    w = {n: w_sh[n] for n in SMALL_REPL}
    w.update(_gather_params(w_sh, SMALL_SHARDED, F32, cc, "gather_small"))
    big = _GatheredWeights({n: w_sh[n] for n in big_names})
    red = _ScatterGrads({n: w_sh[n].shape for n in big_names}, c_idx)

    loss_row, grad_x, grads, red = _local_step(x[0], loss_target[0], w, big, red)
    loss = lax.psum(jnp.sum(loss_row), ("x", "y", "c"))
    g_sh = dict(red.out)

    small_names = tuple(n for n, _ in SMALL_SHARDED) + SMALL_REPL
    sp = _pack([grads[n] for n in small_names], 8)
    srows = sp.shape[0]
    s_all = _sum8(_all_gather8(sp, "ar_small_gather").reshape(8, srows, LANES), "ar_small_sum")
    s_full = dict(zip(small_names, _unpack(s_all, [grads[n].shape for n in small_names])))
    for n, ax in SMALL_SHARDED:
        g_sh[n] = lax.dynamic_index_in_dim(_split4(s_full[n], ax), chip, axis=0, keepdims=False)
    for n in SMALL_REPL:
        g_sh[n] = s_full[n]

    delta, new_m, new_v = {}, {}, {}
    for n, s in zip(big_names, shapes):
        res = _adamw(*[t[n].reshape(-1, s[2]) for t in (w_sh, g_sh, m_sh, v_sh)], f"adamw_{n}")
        delta[n], new_m[n], new_v[n] = [r.reshape(s) for r in res]
    d_s, m_s, v_s = _adamw(*[_pack([t[n] for n in small_names], 8) for t in (w_sh, g_sh, m_sh, v_sh)],
                           "adamw_small")
    for dst, ps in ((delta, d_s), (new_m, m_s), (new_v, v_s)):
        dst.update(zip(small_names, _unpack(ps, [w_sh[n].shape for n in small_names])))

    return (loss, grad_x[None], *[g_sh[n] for n in ALL_NAMES], *[delta[n] for n in ALL_NAMES],
            *[new_m[n] for n in ALL_NAMES], *[new_v[n] for n in ALL_NAMES])
```

```python
import functools
import math

import jax
import jax.numpy as jnp
from jax import lax
from jax.experimental import pallas as pl
from jax.experimental.pallas import tpu as pltpu

F32 = jnp.float32
BF16 = jnp.bfloat16
MESH = pl.DeviceIdType.MESH
_NN = (((1,), (0,)), ((), ()))
_NT = (((1,), (1,)), ((), ()))
_TN = (((0,), (0,)), ((), ()))

D_MODEL = 1024
N_META = 16
EPS = 1e-6
SSD_D_INNER = 2048
SSD_HEADS = 32
SSD_HEAD_DIM = 64
SSD_GROUPS = 8
SSD_HPG = 4
SSD_STATE = 128
SSD_CONV = 4
CHUNK = 128
SSD_IN_DIM = 6176
SSD_IN_PAD = 6272
MLA_HEADS = 16
MLA_NOPE = 64
MLA_ROPE = 32
MLA_V = 64
MLA_QK = 96
MLA_Q_RANK = 384
MLA_KV_RANK = 256
HEAD_SLOT = 128
MLA_WIDE = MLA_HEADS * HEAD_SLOT
HEADS_PER_STEP = 2
LAT_PAD = 768
ROPE_THETA = 10000.0
D_FF = 4096
NPAD = CHUNK - N_META
ADAM_LR, ADAM_B1, ADAM_B2, ADAM_EPS, ADAM_WD, ADAM_STEP = 0.001, 0.9, 0.999, 1e-08, 0.01, 10
LANES = 1024
V7X_VMEM_BYTES = 64 * 1024 * 1024
VMEM_LIMIT = V7X_VMEM_BYTES * 7 // 8


def _pick(n, cands):
    for c in cands:
        if n % c == 0:
            return c
    return n


def _cparams(**kw):
    return pltpu.CompilerParams(vmem_limit_bytes=VMEM_LIMIT, **kw)


def _slab_spec(spec, slab):
    return pl.BlockSpec((None,) + tuple(spec.block_shape), lambda *ids: (slab,) + tuple(spec.index_map(*ids)))


def _mm(a, b, dims, *, name, out_dtype=F32, a_fn=None, epi=None, extras=(), stack=None, norm_gain=None,
        rider=None):
    b, b_slab = b if isinstance(b, tuple) else (b, None)
    b_shape = b.shape if b_slab is None else b.shape[1:]
    if dims == 'nn':
        (M, K), (K2, N) = a.shape, b_shape
    elif dims == 'nt':
        (M, K), (N, K2) = a.shape, b_shape
    else:
        (K, M), (K2, N) = a.shape, b_shape
    assert K == K2, (a.shape, b.shape, dims)
    if dims == 'tn':
        tm = _pick(M, (1024, 768, 512, 384, 256, 128))
        tn = _pick(N, (1024, 896, 768, 512, 384, 256, 128))
        tk = _pick(K, (1408, 1024, 512, 384, 256, 128))
    else:
        tm = _pick(M, (704, 512, 384, 256, 128) if norm_gain is not None else (1408, 1024, 512, 384, 256, 128))
        tn = _pick(N, (1024, 896, 768, 512, 384, 256, 128))
        tk = _pick(K, (1024, 896, 768, 512, 384, 256, 128))
    nk = K // tk
    if dims == 'nn':
        a_spec = pl.BlockSpec((tm, tk), lambda i, j, k: (i, k))
        b_spec = pl.BlockSpec((tk, tn), lambda i, j, k: (k, j))
        dn = (((1,), (0,)), ((), ()))
    elif dims == 'nt':
        a_spec = pl.BlockSpec((tm, tk), lambda i, j, k: (i, k))
        b_spec = pl.BlockSpec((tn, tk), lambda i, j, k: (j, k))
        dn = (((1,), (1,)), ((), ()))
    else:
        a_spec = pl.BlockSpec((tk, tm), lambda i, j, k: (k, i))
        b_spec = pl.BlockSpec((tk, tn), lambda i, j, k: (k, j))
        dn = (((0,), (0,)), ((), ()))
    if b_slab is not None:
        b_spec = _slab_spec(b_spec, b_slab)
    o_spec = pl.BlockSpec((tm, tn), lambda i, j, k: (i, j))
    n_ex = len(extras)
    out_shape = jax.ShapeDtypeStruct((M, N), out_dtype)
    out_spec, held, aliases = o_spec, (), {}
    if stack is not None:
        n_slabs, slab, buf = stack
        out_shape = jax.ShapeDtypeStruct((n_slabs, M, N), out_dtype)
        out_spec = pl.BlockSpec((None, tm, tn), lambda i, j, k: (slab, i, j))
        if buf is not None:
            held, aliases = (buf,), {2 + n_ex: 0}

    gains = ()
    if norm_gain is not None:
        assert tn == N and stack is None, "the rms epilogue needs whole rows"
        gains = (norm_gain,)
        out_shape = [out_shape, jax.ShapeDtypeStruct((M, N), BF16)]
        out_spec = [out_spec, o_spec]

    n_own_in = 2 + n_ex + len(gains) + len(held)
    steps = (M // tm, N // tn, nk)

    def body(*refs):
        (a_ref, b_ref, *rest), ride = _rider_split(rider, refs, n_own_in, 1 + len(gains), 1)
        ex_refs, rest = rest[:n_ex], rest[n_ex:]
        g_refs, rest = rest[:len(gains)], rest[len(gains) + len(held):]
        o_ref, acc = rest[0], rest[-1]
        k = pl.program_id(2)
        if ride is not None:
            @pl.when((pl.program_id(0) == 0) & (pl.program_id(1) == 0) & (k == 0))
            def _():
                rider.start(*ride)

        @pl.when(k == 0)
        def _():
            acc[...] = jnp.zeros_like(acc)

        av = a_ref[...]
        if a_fn is not None:
            av = a_fn(av)
        acc[...] += lax.dot_general(av.astype(BF16), b_ref[...].astype(BF16), dn,
                                    preferred_element_type=F32)

        @pl.when(k == nk - 1)
        def _():
            r = acc[...]
            if epi is not None:
                r = epi(r, *[e[...] for e in ex_refs])
            o_ref[...] = r.astype(out_dtype)
            if gains:
                rest[1][...] = _rms(r, g_refs[0][...]).astype(BF16)

        if ride is not None:
            @pl.when((pl.program_id(0) == steps[0] - 1) & (pl.program_id(1) == steps[1] - 1) & (k == nk - 1))
            def _():
                rider.finish(*ride)

    r_in, r_out, r_shapes, r_scratch = _rider_args(rider)
    own_shapes = out_shape if isinstance(out_shape, list) else [out_shape]
    own_specs = out_spec if isinstance(out_spec, list) else [out_spec]
    res = pl.pallas_call(
        body, name=name,
        out_shape=own_shapes + r_shapes,
        grid=steps,
        in_specs=[a_spec, b_spec] + [o_spec] * n_ex
        + [pl.BlockSpec((1, tn), lambda i, j, k: (0, j))] * len(gains)
        + [pl.BlockSpec(memory_space=pl.ANY)] * len(held) + r_in,
        out_specs=own_specs + r_out,
        input_output_aliases=aliases,
        scratch_shapes=[pltpu.VMEM((tm, tn), F32)] + r_scratch,
        compiler_params=_cparams(dimension_semantics=("arbitrary", "arbitrary", "arbitrary")),
    )(a, b, *extras, *gains, *held, *(rider.operands if rider else ()))
    return res[0] if len(res) == 1 else res


def _mm_rms_bwd(cot, w_t, h, d_res, gain, name):
    w_t, w_slab = w_t if isinstance(w_t, tuple) else (w_t, None)
    (M, K), N = cot.shape, w_t.shape[-2]
    tm = _pick(M, (704, 512, 384, 256, 128))
    tk = _pick(K, (1024, 896, 768, 512, 384, 256, 128))
    nk = K // tk
    w_spec = pl.BlockSpec((N, tk), lambda i, k: (0, k))
    if w_slab is not None:
        w_spec = _slab_spec(w_spec, w_slab)

    def body(a_ref, b_ref, h_ref, r_ref, g_ref, dh_ref, dg_ref, acc):
        i, k = pl.program_id(0), pl.program_id(1)

        @pl.when(k == 0)
        def _():
            acc[...] = jnp.zeros_like(acc)

        acc[...] += lax.dot_general(a_ref[...].astype(BF16), b_ref[...].astype(BF16), _NT,
                                    preferred_element_type=F32)

        @pl.when(k == nk - 1)
        def _():
            _, vjp = jax.vjp(_rms, h_ref[...], g_ref[...])
            dh, dg = vjp(acc[...])
            dh_ref[...] = (r_ref[...] + dh) * _row_mask(i, tm)

            @pl.when(i == 0)
            def _():
                dg_ref[...] = dg

            @pl.when(i > 0)
            def _():
                dg_ref[...] += dg

    rows = pl.BlockSpec((tm, N), lambda i, k: (i, 0))
    vec = pl.BlockSpec((1, N), lambda i, k: (0, 0))
    return pl.pallas_call(
        body, name=name,
        out_shape=[jax.ShapeDtypeStruct((M, N), F32), jax.ShapeDtypeStruct((1, N), F32)],
        grid=(M // tm, nk),
        in_specs=[pl.BlockSpec((tm, tk), lambda i, k: (i, k)), w_spec, rows, rows, vec],
        out_specs=[rows, vec],
        scratch_shapes=[pltpu.VMEM((tm, N), F32)],
        compiler_params=_cparams(dimension_semantics=("arbitrary", "arbitrary")),
    )(cot, w_t, h, d_res, gain)


def _mm_attn_do(dh, w_out_t, o, name):
    (M, K), (N, _) = dh.shape, w_out_t.shape
    tm = _pick(M, (704, 512, 384, 256, 128))
    tn = 8 * HEAD_SLOT

    def body(a_ref, b_ref, o_ref, dob_ref, delta_ref):
        do = lax.dot_general(a_ref[...].astype(BF16), b_ref[...], _NT, preferred_element_type=F32)
        dob_ref[...] = do.astype(BF16)
        for hh in range(tn // HEAD_SLOT):
            sl = slice(hh * HEAD_SLOT, (hh + 1) * HEAD_SLOT)
            delta_ref[hh] = jnp.sum(do[:, sl] * o_ref[:, sl], axis=-1, keepdims=True)

    tile = pl.BlockSpec((tm, tn), lambda i, j: (i, j))
    return pl.pallas_call(
        body, name=name,
        out_shape=[jax.ShapeDtypeStruct((M, N), BF16), jax.ShapeDtypeStruct((N // HEAD_SLOT, M, 1), F32)],
        grid=(M // tm, N // tn),
        in_specs=[pl.BlockSpec((tm, K), lambda i, j: (i, 0)), pl.BlockSpec((tn, K), lambda i, j: (j, 0)), tile],
        out_specs=[tile, pl.BlockSpec((tn // HEAD_SLOT, tm, 1), lambda i, j: (j, i, 0))],
        compiler_params=_cparams(dimension_semantics=("parallel", "parallel")),
    )(dh, w_out_t, o)


def _row_call(fn, rows, consts, out_rows, out_accs=(), *, n_rows, tile, name):
    n_r, n_c, n_o, n_a = len(rows), len(consts), len(out_rows), len(out_accs)
    steps = n_rows // tile

    def body(*refs):
        r_refs = refs[:n_r]
        c_refs = refs[n_r:n_r + n_c]
        o_refs = refs[n_r + n_c:n_r + n_c + n_o]
        a_refs = refs[n_r + n_c + n_o:]
        i = pl.program_id(0)
        res = fn(i, *[r[...] for r in r_refs], *[c[...] for c in c_refs])
        for o_ref, val in zip(o_refs, res[:n_o]):
            o_ref[...] = val.astype(o_ref.dtype)

        @pl.when(i == 0)
        def _():
            for a_ref in a_refs:
                a_ref[...] = jnp.zeros_like(a_ref)

        for a_ref, val in zip(a_refs, res[n_o:]):
            a_ref[...] += val

    in_specs = [pl.BlockSpec((tile, w), functools.partial(lambda i, cb: (i, cb), cb=cb))
                for (_, w, cb) in rows]
    in_specs += [pl.BlockSpec(c.shape, lambda i: (0, 0)) for c in consts]
    out_specs = [pl.BlockSpec((tile, c), lambda i: (i, 0)) for (c, _) in out_rows]
    out_specs += [pl.BlockSpec(s, lambda i: (0, 0)) for s in out_accs]
    out_shape = [jax.ShapeDtypeStruct((n_rows, c), dt) for (c, dt) in out_rows]
    out_shape += [jax.ShapeDtypeStruct(s, F32) for s in out_accs]
    return pl.pallas_call(
        body, name=name, out_shape=out_shape, grid=(steps,),
        in_specs=in_specs, out_specs=out_specs,
        compiler_params=_cparams(dimension_semantics=("arbitrary",)),
    )(*[r[0] for r in rows], *consts)


def _row_mask(i, tile):
    r = i * tile + lax.broadcasted_iota(jnp.int32, (tile, 1), 0)
    return (r >= NPAD).astype(F32)


def _rms(x, g):
    return x * lax.rsqrt(jnp.mean(x * x, axis=-1, keepdims=True) + EPS) * g


def _silu(x):
    return x * (0.5 * jnp.tanh(0.5 * x) + 0.5)


def _softplus(x):
    return jnp.maximum(x, 0.0) + jnp.log(1.0 + jnp.exp(-jnp.abs(x)))


def _rms_fwd(h, g, name):
    lp = h.shape[0]
    return _row_call(lambda i, hv, gv: (_rms(hv, gv),), [(h, D_MODEL, 0)], [g],
                     [(D_MODEL, BF16)], n_rows=lp, tile=_pick(lp, (384, 256, 128)), name=name)[0]


@functools.partial(jax.custom_vjp, nondiff_argnums=(1,))
def _roll_rows(x, s):
    return pltpu.roll(x, s, 0)


def _roll_rows_fwd(x, s):
    return pltpu.roll(x, s, 0), None


def _roll_rows_bwd(s, _, ct):
    return (pltpu.roll(ct, (ct.shape[0] - s) % ct.shape[0], 0),)


_roll_rows.defvjp(_roll_rows_fwd, _roll_rows_bwd)


def _conv_silu(cur, halo, w_rows, b):
    full = jnp.concatenate([halo, cur], axis=0)
    acc = cur * w_rows[SSD_CONV - 1] + b
    for k in range(SSD_CONV - 1):
        acc = acc + _roll_rows(full, SSD_CONV - 1 - k)[8:] * w_rows[k]
    return _silu(acc)


def _split3(v):
    hi = v.astype(BF16)
    r1 = v - hi.astype(F32)
    mid = r1.astype(BF16)
    lo = (r1 - mid.astype(F32)).astype(BF16)
    return hi, mid, lo


def _select_right(v, sel, dn):
    return sum(lax.dot_general(p, sel, dn, preferred_element_type=F32) for p in _split3(v))


@jax.custom_vjp
def _expand_heads(v, e_mat):
    return _select_right(v, e_mat, _NN)


def _expand_heads_fwd(v, e_mat):
    return _select_right(v, e_mat, _NN), e_mat


def _expand_heads_bwd(e_mat, ct):
    return _select_right(ct, e_mat, _NT), jnp.zeros_like(e_mat)


_expand_heads.defvjp(_expand_heads_fwd, _expand_heads_bwd)


@jax.custom_vjp
def _cumsum_rows(a, tri):
    return sum(lax.dot_general(tri, p, _NN, preferred_element_type=F32) for p in _split3(a))


def _cumsum_rows_fwd(a, tri):
    return _cumsum_rows(a, tri), tri


def _cumsum_rows_bwd(tri, ct):
    return (sum(lax.dot_general(tri, p, _TN, preferred_element_type=F32) for p in _split3(ct)),
            jnp.zeros_like(tri))


_cumsum_rows.defvjp(_cumsum_rows_fwd, _cumsum_rows_bwd)


def _ssd_chunk(mask, z, xs_pre, bc_pre, halo_x, halo_bc, dt_pre, st, cwx0, cwx1, cwx2, cwx3,
               cwb0, cwb1, cwb2, cwb3, cb_x, cb_bc, dtb, alog, dsk, ng):
    L = CHUNK
    lane_h = lax.broadcasted_iota(jnp.int32, (1, 128), 1)
    head_ok = (lane_h < SSD_HEADS).astype(F32)
    e_mat = (lax.broadcasted_iota(jnp.int32, (128, SSD_D_INNER), 1) // SSD_HEAD_DIM
             == lax.broadcasted_iota(jnp.int32, (128, SSD_D_INNER), 0)).astype(BF16)
    ri = lax.broadcasted_iota(jnp.int32, (L, L), 0)
    ci = lax.broadcasted_iota(jnp.int32, (L, L), 1)
    causal = ri >= ci

    xs = _conv_silu(xs_pre, halo_x, (cwx0, cwx1, cwx2, cwx3), cb_x) * mask
    bc = _conv_silu(bc_pre, halo_bc, (cwb0, cwb1, cwb2, cwb3), cb_bc) * mask
    dt = _softplus(dt_pre + dtb) * mask * head_ok
    a_dt = dt * (-jnp.exp(alog))
    a_cs = _cumsum_rows(a_dt, causal.astype(BF16))
    a_cs_t = a_cs.T
    row8 = lax.broadcasted_iota(jnp.int32, (8, 128), 0)
    last8 = jnp.where(row8 == 0, jnp.sum(a_dt, axis=0, keepdims=True), 0.0)
    dsk8 = jnp.where(row8 == 0, dsk, 0.0)
    wide = _expand_heads(jnp.concatenate([dt, a_cs, last8, dsk8], axis=0), e_mat)
    dt_e, acs_e = wide[0:L], wide[L:2 * L]
    last_e = jnp.sum(wide[2 * L:2 * L + 8], axis=0, keepdims=True)
    d_e = jnp.sum(wide[2 * L + 8:2 * L + 16], axis=0, keepdims=True)
    xdt = xs * dt_e
    dte_e = jnp.exp(last_e - acs_e)
    dfs_e = jnp.exp(acs_e)
    cd_e = jnp.exp(last_e)
    sub_h = lax.broadcasted_iota(jnp.int32, (128, L), 0)
    lane_hl = lax.broadcasted_iota(jnp.int32, (L, 128), 1)
    lane_g = lax.broadcasted_iota(jnp.int32, (1, SSD_HPG * SSD_HEAD_DIM), 1) // SSD_HEAD_DIM

    ys, new_st = [], []
    for g in range(SSD_GROUPS):
        b_g = bc[:, g * 128:(g + 1) * 128].astype(BF16)
        c_g = bc[:, 1024 + g * 128:1024 + (g + 1) * 128].astype(BF16)
        gs = slice(g * 256, (g + 1) * 256)
        xdt_g = xdt[:, gs]
        cb = lax.dot_general(c_g, b_g, (((1,), (1,)), ((), ())), preferred_element_type=F32)
        st_g = st[g * 128:(g + 1) * 128, :]
        y_g = lax.dot_general(c_g, st_g.astype(BF16), (((1,), (0,)), ((), ())),
                              preferred_element_type=F32) * dfs_e[:, gs]
        for j in range(SSD_HPG):
            h = g * SSD_HPG + j
            col = jnp.sum(jnp.where(lane_hl == h, a_cs, 0.0), axis=1, keepdims=True)
            row = jnp.sum(jnp.where(sub_h == h, a_cs_t, 0.0), axis=0, keepdims=True)
            dec = jnp.where(causal, jnp.exp(jnp.where(causal, col - row, 0.0)), 0.0)
            m_h = (cb * dec).astype(BF16)
            x_h = jnp.where(lane_g == j, xdt_g, 0.0).astype(BF16)
            y_g = y_g + lax.dot_general(m_h, x_h, (((1,), (0,)), ((), ())),
                                        preferred_element_type=F32)
        s_new = lax.dot_general(b_g, (xdt_g * dte_e[:, gs]).astype(BF16), (((0,), (0,)), ((), ())),
                                preferred_element_type=F32)
        new_st.append(st_g * cd_e[:, gs] + s_new)
        ys.append(y_g)
    y = jnp.concatenate(ys, axis=1) + xs * d_e
    gg = y * _silu(z)
    outs = []
    for g in range(SSD_GROUPS):
        sl = gg[:, g * 256:(g + 1) * 256]
        outs.append(sl * lax.rsqrt(jnp.mean(sl * sl, axis=-1, keepdims=True) + EPS))
    out = jnp.concatenate(outs, axis=1) * ng
    return out, jnp.concatenate(new_st, axis=0)


def _ssd_consts(conv_w, conv_b, dtb, alog, dsk, ng):
    return [conv_w, conv_b, dtb, alog, dsk, ng]


def _ssd_param_vals(cw_ref, cb_ref, dtb_ref, alog_ref, dsk_ref, ng_ref):
    cwx = [cw_ref[k:k + 1, 0:SSD_D_INNER] for k in range(SSD_CONV)]
    cwb = [cw_ref[k:k + 1, SSD_D_INNER:2 * SSD_D_INNER] for k in range(SSD_CONV)]
    return (*cwx, *cwb, cb_ref[:, 0:SSD_D_INNER], cb_ref[:, SSD_D_INNER:2 * SSD_D_INNER],
            dtb_ref[...], alog_ref[...], dsk_ref[...], ng_ref[...])


def _ssd_in_specs(rev, nc):
    def cidx(i):
        return (nc - 1 - i) if rev else i

    def halo(cb):
        return pl.BlockSpec((8, SSD_D_INNER), lambda i: (jnp.maximum(16 * cidx(i) - 1, 0), cb))

    return [
        pl.BlockSpec((CHUNK, SSD_D_INNER), lambda i: (cidx(i), 0)),
        pl.BlockSpec((CHUNK, SSD_D_INNER), lambda i: (cidx(i), 1)),
        pl.BlockSpec((CHUNK, SSD_D_INNER), lambda i: (cidx(i), 2)),
        halo(1), halo(2),
        pl.BlockSpec((CHUNK, 128), lambda i: (cidx(i), 48)),
    ]


class _Rider:
    def __init__(self, operands, out_shapes, scratch, start, finish):
        self.operands, self.out_shapes, self.scratch = list(operands), list(out_shapes), list(scratch)
        self.start, self.finish = start, finish


def _rider_split(rider, refs, n_in, n_out, n_scratch):
    if rider is None:
        return refs, None
    ni, no = len(rider.operands), len(rider.out_shapes)
    own = refs[:n_in] + refs[n_in + ni:n_in + ni + n_out] + refs[n_in + ni + n_out + no:n_in + ni + n_out + no + n_scratch]
    mine = (refs[n_in:n_in + ni], refs[n_in + ni + n_out:n_in + ni + n_out + no],
            refs[n_in + ni + n_out + no + n_scratch:])
    return own, mine


def _rider_args(rider):
    if rider is None:
        return [], [], [], []
    hbm = pl.BlockSpec(memory_space=pl.ANY)
    return ([hbm] * len(rider.operands), [hbm] * len(rider.out_shapes), rider.out_shapes, rider.scratch)


def _ssd_fwd(zxd, consts, name, rider=None):
    lp = zxd.shape[0]
    nc = lp // CHUNK

    def body(*refs):
        own, ride = _rider_split(rider, refs, 12, 2, 1)
        (z_ref, xs_ref, bc_ref, hx_ref, hb_ref, dt_ref, cw_ref, cb_ref, dtb_ref, alog_ref,
         dsk_ref, ng_ref, y_ref, st_ref, state) = own
        c = pl.program_id(0)

        @pl.when(c == 0)
        def _():
            state[...] = jnp.zeros_like(state)
            if ride is not None:
                rider.start(*ride)

        live = (c > 0).astype(F32)
        st_ref[0] = state[...]
        out, st_new = _ssd_chunk(
            _row_mask(c, CHUNK), z_ref[...], xs_ref[...], bc_ref[...], hx_ref[...] * live,
            hb_ref[...] * live, dt_ref[...], state[...],
            *_ssd_param_vals(cw_ref, cb_ref, dtb_ref, alog_ref, dsk_ref, ng_ref))
        y_ref[...] = out.astype(y_ref.dtype)
        state[...] = st_new

        if ride is not None:
            @pl.when(c == nc - 1)
            def _():
                rider.finish(*ride)

    r_in, r_out, r_shapes, r_scratch = _rider_args(rider)
    return pl.pallas_call(
        body, name=name,
        out_shape=[jax.ShapeDtypeStruct((lp, SSD_D_INNER), BF16),
                   jax.ShapeDtypeStruct((nc, SSD_GROUPS * SSD_STATE, 256), F32)] + r_shapes,
        grid=(nc,),
        in_specs=_ssd_in_specs(False, nc) + [pl.BlockSpec(c.shape, lambda i: (0, 0)) for c in consts] + r_in,
        out_specs=[pl.BlockSpec((CHUNK, SSD_D_INNER), lambda i: (i, 0)),
                   pl.BlockSpec((1, SSD_GROUPS * SSD_STATE, 256), lambda i: (i, 0, 0))] + r_out,
        scratch_shapes=[pltpu.VMEM((SSD_GROUPS * SSD_STATE, 256), F32)] + r_scratch,
        compiler_params=_cparams(dimension_semantics=("arbitrary",)),
    )(zxd, zxd, zxd, zxd, zxd, zxd, *consts, *(rider.operands if rider else ()))


def _ssd_bwd(zxd, states, d_y, consts, name, rider=None):
    lp = zxd.shape[0]
    nc = lp // CHUNK

    def body(*refs):
        own, ride = _rider_split(rider, refs, 14, 7, 3)
        (z_ref, xs_ref, bc_ref, hx_ref, hb_ref, dt_ref, st_ref, dy_ref, cw_ref, cb_ref, dtb_ref,
         alog_ref, dsk_ref, ng_ref, dz_ref, dcw_ref, dcb_ref, ddtb_ref, dalog_ref, ddsk_ref,
         dng_ref, d_state, d_hx, d_hb) = own
        i = pl.program_id(0)
        c = nc - 1 - i

        @pl.when(i == 0)
        def _():
            d_state[...] = jnp.zeros_like(d_state)
            d_hx[...] = jnp.zeros_like(d_hx)
            d_hb[...] = jnp.zeros_like(d_hb)
            for r in (dcw_ref, dcb_ref, ddtb_ref, dalog_ref, ddsk_ref, dng_ref):
                r[...] = jnp.zeros_like(r)
            if ride is not None:
                rider.start(*ride)

        live = (c > 0).astype(F32)
        fn = functools.partial(_ssd_chunk, _row_mask(c, CHUNK))
        prim = (z_ref[...], xs_ref[...], bc_ref[...], hx_ref[...] * live, hb_ref[...] * live,
                dt_ref[...], st_ref[0],
                *_ssd_param_vals(cw_ref, cb_ref, dtb_ref, alog_ref, dsk_ref, ng_ref))
        _, vjp = jax.vjp(fn, *prim)
        (d_z, d_xs, d_bc, g_hx, g_hb, d_dt, g_st, *d_par) = vjp((dy_ref[...], d_state[...]))
        zeros = jnp.zeros((CHUNK - 8, SSD_D_INNER), F32)
        d_xs = d_xs + jnp.concatenate([zeros, d_hx[...]], axis=0)
        d_bc = d_bc + jnp.concatenate([zeros, d_hb[...]], axis=0)
        dz_ref[:, 0:SSD_D_INNER] = d_z.astype(dz_ref.dtype)
        dz_ref[:, SSD_D_INNER:2 * SSD_D_INNER] = d_xs.astype(dz_ref.dtype)
        dz_ref[:, 2 * SSD_D_INNER:3 * SSD_D_INNER] = d_bc.astype(dz_ref.dtype)
        dz_ref[:, 3 * SSD_D_INNER:] = d_dt.astype(dz_ref.dtype)
        d_state[...] = g_st
        d_hx[...] = g_hx * live
        d_hb[...] = g_hb * live
        for k in range(SSD_CONV):
            dcw_ref[k:k + 1, 0:SSD_D_INNER] += d_par[k]
            dcw_ref[k:k + 1, SSD_D_INNER:2 * SSD_D_INNER] += d_par[SSD_CONV + k]
        dcb_ref[:, 0:SSD_D_INNER] += d_par[8]
        dcb_ref[:, SSD_D_INNER:2 * SSD_D_INNER] += d_par[9]
        ddtb_ref[...] += d_par[10]
        dalog_ref[...] += d_par[11]
        ddsk_ref[...] += d_par[12]
        dng_ref[...] += d_par[13]

        if ride is not None:
            @pl.when(i == nc - 1)
            def _():
                rider.finish(*ride)

    const_specs = [pl.BlockSpec(c.shape, lambda i: (0, 0)) for c in consts]
    r_in, r_out, r_shapes, r_scratch = _rider_args(rider)
    return pl.pallas_call(
        body, name=name,
        out_shape=[jax.ShapeDtypeStruct((lp, SSD_IN_PAD), BF16)]
        + [jax.ShapeDtypeStruct(c.shape, F32) for c in consts] + r_shapes,
        grid=(nc,),
        in_specs=_ssd_in_specs(True, nc)
        + [pl.BlockSpec((1, SSD_GROUPS * SSD_STATE, 256), lambda i: (nc - 1 - i, 0, 0)),
           pl.BlockSpec((CHUNK, SSD_D_INNER), lambda i: (nc - 1 - i, 0))] + const_specs + r_in,
        out_specs=[pl.BlockSpec((CHUNK, SSD_IN_PAD), lambda i: (nc - 1 - i, 0))] + const_specs + r_out,
        scratch_shapes=[pltpu.VMEM((SSD_GROUPS * SSD_STATE, 256), F32),
                        pltpu.VMEM((8, SSD_D_INNER), F32), pltpu.VMEM((8, SSD_D_INNER), F32)] + r_scratch,
        compiler_params=_cparams(dimension_semantics=("arbitrary",)),
    )(zxd, zxd, zxd, zxd, zxd, zxd, states, d_y, *consts, *(rider.operands if rider else ()))


@jax.custom_vjp
def _rot_half(x):
    lane = lax.broadcasted_iota(jnp.int32, x.shape, 1)
    lo = (lane >= MLA_NOPE) & (lane < MLA_NOPE + MLA_ROPE // 2)
    hi = (lane >= MLA_NOPE + MLA_ROPE // 2) & (lane < MLA_QK)
    down = pltpu.roll(x, HEAD_SLOT - MLA_ROPE // 2, 1)
    up = pltpu.roll(x, MLA_ROPE // 2, 1)
    return jnp.where(lo, -down, jnp.where(hi, up, 0.0))


def _rot_half_fwd(x):
    return _rot_half(x), None


def _rot_half_bwd(_, ct):
    return (-_rot_half(ct),)


_rot_half.defvjp(_rot_half_fwd, _rot_half_bwd)


def _head_norm_rope(t, gain, cos, sin):
    n = t * lax.rsqrt(jnp.sum(t * t, axis=-1, keepdims=True) * (1.0 / MLA_QK) + EPS) * gain
    return n * cos + _rot_half(n) * sin


def _qk_prep(q_raw, kn_raw, kpe, cos, sin, qg, kg):
    qs, ks = [], []
    for h in range(MLA_HEADS):
        sl = slice(h * HEAD_SLOT, (h + 1) * HEAD_SLOT)
        qs.append(_head_norm_rope(q_raw[:, sl], qg, cos, sin))
        ks.append(_head_norm_rope(kn_raw[:, sl] + kpe, kg, cos, sin))
    return jnp.concatenate(qs, axis=1), jnp.concatenate(ks, axis=1)


def _lat_norm(kv_lat, q_lat, kvg, qg):
    return _rms(kv_lat, kvg), _rms(q_lat, qg)


_NEG = -1e30
_SCALE = MLA_QK ** -0.5


STRIP = 128
_EXP2_SCALE = _SCALE * math.log2(math.e)


def _strip_mask(kind, blk, c, t):
    if kind is None:
        return None
    kpos = blk * t + c * STRIP + lax.broadcasted_iota(jnp.int32, (1, STRIP), 1)
    if kind == 'keys':
        return kpos >= NPAD
    qpos = blk * t + lax.broadcasted_iota(jnp.int32, (t, 1), 0)
    return (kpos <= qpos) & ((kpos >= NPAD) | (kpos == qpos))


def _attn_fwd(q, k, v, name, rider=None):
    lp = q.shape[0]
    t = tk = _pick(lp, (384, 256, 128))
    nb = lp // t
    hp = HEADS_PER_STEP
    wide = hp * HEAD_SLOT
    heads = [slice(a * HEAD_SLOT, (a + 1) * HEAD_SLOT) for a in range(hp)]

    def body(*refs):
        (q_ref, k_ref, v_ref, o_ref, lse_ref), ride = _rider_split(rider, refs, 3, 2, 0)
        qi = pl.program_id(1)
        if ride is not None:
            @pl.when((pl.program_id(0) == 0) & (qi == 0))
            def _():
                rider.start(*ride)

        def scores(ki):
            rows = pl.ds(pl.multiple_of(ki * tk, tk), tk)
            return tuple(lax.dot_general(q_ref[:, heads[a]], k_ref[rows, heads[a]], _NT,
                                         preferred_element_type=F32) for a in range(hp))

        def update(a, ki, carry, s, mask):
            rows = pl.ds(pl.multiple_of(ki * tk, tk), tk)
            m, acc = carry
            m_parts, p_parts = [], []
            for r0 in range(0, t, t // 2):
                rh = slice(r0, r0 + t // 2)
                sh = jnp.where(mask if mask.shape[0] == 1 else mask[rh], s[rh], _NEG)
                mh = jnp.maximum(m[rh], jnp.max(sh, axis=-1, keepdims=True))
                m_parts.append(mh)
                p_parts.append(jnp.concatenate(
                    [jnp.exp2((sh[:, c:c + STRIP] - mh) * _EXP2_SCALE).astype(BF16) for c in range(0, tk, STRIP)],
                    axis=1))
            m_new = jnp.concatenate(m_parts, axis=0)
            p = jnp.concatenate(p_parts, axis=0)
            alpha = jnp.exp2((m - m_new) * _EXP2_SCALE)
            acc = alpha * acc + lax.dot_general(p, v_ref[rows, heads[a]], _NN, preferred_element_type=F32)
            return m_new, acc

        init = (jnp.full((t, 1), _NEG, F32), jnp.zeros((t, HEAD_SLOT), F32))
        ones_lane = lax.broadcasted_iota(jnp.int32, (1, HEAD_SLOT), 1) == MLA_V
        key_pos = lax.broadcasted_iota(jnp.int32, (1, tk), 1)
        n_full = (qi * t) // tk

        def before(ki, state):
            carry, s = state
            s_next = scores(ki + 1)
            key_ok = ki * tk + key_pos >= NPAD
            return tuple(update(a, ki, carry[a], s[a], key_ok) for a in range(hp)), s_next

        carry, s = lax.fori_loop(0, n_full, before, ((init,) * hp, scores(0)))
        qpos = qi * t + lax.broadcasted_iota(jnp.int32, (t, tk), 0)
        kpos = n_full * tk + lax.broadcasted_iota(jnp.int32, (t, tk), 1)
        diag = (kpos <= qpos) & ((kpos >= NPAD) | (kpos == qpos))
        carry = tuple(update(a, n_full, carry[a], s[a], diag) for a in range(hp))
        for a in range(hp):
            m, acc = carry[a]
            l = jnp.sum(jnp.where(ones_lane, acc, 0.0), axis=-1, keepdims=True)
            o_ref[:, heads[a]] = jnp.where(ones_lane, 0.0, acc / l * _row_mask(qi, t))
            lse_ref[a] = m * _SCALE + jnp.log(l)

        if ride is not None:
            @pl.when((pl.program_id(0) == MLA_HEADS // hp - 1) & (qi == nb - 1))
            def _():
                rider.finish(*ride)

    qspec = pl.BlockSpec((t, wide), lambda g, i: (i, g))
    kspec = pl.BlockSpec((lp, wide), lambda g, i: (0, g))
    r_in, r_out, r_shapes, r_scratch = _rider_args(rider)
    return pl.pallas_call(
        body, name=name,
        out_shape=[jax.ShapeDtypeStruct((lp, MLA_WIDE), F32),
                   jax.ShapeDtypeStruct((MLA_HEADS, lp, 1), F32)] + r_shapes,
        grid=(MLA_HEADS // hp, nb),
        in_specs=[qspec, kspec, kspec] + r_in,
        out_specs=[qspec, pl.BlockSpec((hp, t, 1), lambda g, i: (g, i, 0))] + r_out,
        scratch_shapes=r_scratch,
        compiler_params=_cparams(dimension_semantics=("arbitrary", "arbitrary")),
    )(q, k, v, *(rider.operands if rider else ()))


def _attn_bwd(q, k, v, do, lse, delta, name, rider=None):
    lp = q.shape[0]
    t = _pick(lp, (384, 256, 128))
    nb = lp // t
    ns = t // STRIP
    hp = HEADS_PER_STEP
    wide = hp * HEAD_SLOT
    heads = [slice(a * HEAD_SLOT, (a + 1) * HEAD_SLOT) for a in range(hp)]
    log2e = math.log2(math.e)

    def body(*refs):
        own, ride = _rider_split(rider, refs, 6, 3, 4)
        (q_ref, k_ref, v_ref, do_ref, lse_ref, delta_ref, dq_ref, dk_ref, dv_ref,
         s_scr, dp_scr, p_scr, ds_scr) = own
        kj = pl.program_id(1)
        if ride is not None:
            @pl.when((pl.program_id(0) == 0) & (kj == 0))
            def _():
                rider.start(*ride)

        @pl.when(kj == 0)
        def _():
            dq_ref[...] = jnp.zeros_like(dq_ref)

        dk_ref[...] = jnp.zeros_like(dk_ref)
        dv_ref[...] = jnp.zeros_like(dv_ref)

        def tile(qi, kind):
            rows = pl.ds(pl.multiple_of(qi * t, t), t)
            for a in range(hp):
                qb, dob = q_ref[rows, heads[a]], do_ref[rows, heads[a]]
                kb, vb = k_ref[:, heads[a]], v_ref[:, heads[a]]
                s_scr[a] = lax.dot_general(qb, kb, _NT, preferred_element_type=F32)
                dp_scr[a] = lax.dot_general(dob, vb, _NT, preferred_element_type=F32)
                lse2 = lse_ref[a, rows, :] * log2e
                delta = delta_ref[a, rows, :]
                for c in range(ns):
                    cs = slice(c * STRIP, (c + 1) * STRIP)
                    pc = jnp.exp2(s_scr[a, :, cs] * _EXP2_SCALE - lse2)
                    mask = _strip_mask(kind, kj, c, t)
                    if mask is not None:
                        pc = jnp.where(mask, pc, 0.0)
                    p_scr[a, :, cs] = pc.astype(BF16)
                    ds_scr[a, :, cs] = (pc * (dp_scr[a, :, cs] - delta)).astype(BF16)
                dq_ref[rows, heads[a]] += lax.dot_general(ds_scr[a], kb, _NN,
                                                          preferred_element_type=F32) * _SCALE
                dv_ref[:, heads[a]] += lax.dot_general(p_scr[a], dob, _TN, preferred_element_type=F32)
                dk_ref[:, heads[a]] += lax.dot_general(ds_scr[a], qb, _TN, preferred_element_type=F32)

        tile(kj, 'diag')

        def below(kind):
            def body(qi, carry):
                tile(qi, kind)
                return carry
            return body

        @pl.when(kj == 0)
        def _():
            lax.fori_loop(kj + 1, nb, below('keys'), 0)

        @pl.when(kj > 0)
        def _():
            lax.fori_loop(kj + 1, nb, below(None), 0)

        dk_ref[...] = dk_ref[...] * _SCALE

        if ride is not None:
            @pl.when((pl.program_id(0) == MLA_HEADS // hp - 1) & (kj == nb - 1))
            def _():
                rider.finish(*ride)

    whole = pl.BlockSpec((lp, wide), lambda g, j: (0, g))
    kspec = pl.BlockSpec((t, wide), lambda g, j: (j, g))
    stat = pl.BlockSpec((hp, lp, 1), lambda g, j: (g, 0, 0))
    r_in, r_out, r_shapes, r_scratch = _rider_args(rider)
    return pl.pallas_call(
        body, name=name,
        out_shape=[jax.ShapeDtypeStruct((lp, MLA_WIDE), F32)] * 3 + r_shapes,
        grid=(MLA_HEADS // hp, nb),
        in_specs=[whole, kspec, kspec, whole, stat, stat] + r_in,
        out_specs=[whole, kspec, kspec] + r_out,
        scratch_shapes=[pltpu.VMEM((hp, t, t), F32), pltpu.VMEM((hp, t, t), F32),
                        pltpu.VMEM((hp, t, t), BF16), pltpu.VMEM((hp, t, t), BF16)] + r_scratch,
        compiler_params=_cparams(dimension_semantics=("arbitrary", "arbitrary")),
    )(q, k, v, do, lse, delta, *(rider.operands if rider else ()))


def _rope_tables(lp):
    inv = 1.0 / (ROPE_THETA ** (jnp.arange(0, MLA_ROPE, 2, dtype=F32) / MLA_ROPE))
    pos = jnp.maximum(jnp.arange(lp, dtype=jnp.int32) - NPAD, 0).astype(F32)
    ang = pos[:, None] * inv[None, :]
    cos, sin = jnp.cos(ang), jnp.sin(ang)
    z32 = jnp.zeros((lp, HEAD_SLOT - MLA_QK), F32)
    cos_t = jnp.concatenate([jnp.ones((lp, MLA_NOPE), F32), cos, cos, z32], axis=1)
    sin_t = jnp.concatenate([jnp.zeros((lp, MLA_NOPE), F32), sin, sin, z32], axis=1)
    return cos_t, sin_t


def _loss_head(h, target, name):
    lp = h.shape[0]

    def body(h_ref, t_ref, d_ref, loss_ref):
        i = pl.program_id(0)

        @pl.when(i == 0)
        def _():
            d_ref[...] = jnp.zeros_like(d_ref)
            loss_ref[...] = jnp.zeros_like(loss_ref)

        @pl.when(i > 0)
        def _():
            err = h_ref[...] - t_ref[...]
            d_ref[...] = err * (1.0 / D_MODEL)
            loss_ref[...] += jnp.sum(err * err, axis=0, keepdims=True) * (0.5 / D_MODEL)

    return pl.pallas_call(
        body, name=name,
        out_shape=[jax.ShapeDtypeStruct((lp, D_MODEL), F32), jax.ShapeDtypeStruct((1, D_MODEL), F32)],
        grid=(lp // CHUNK,),
        in_specs=[pl.BlockSpec((CHUNK, D_MODEL), lambda i: (i, 0)),
                  pl.BlockSpec((CHUNK, D_MODEL), lambda i: (jnp.maximum(i - 1, 0), 0))],
        out_specs=[pl.BlockSpec((CHUNK, D_MODEL), lambda i: (i, 0)),
                   pl.BlockSpec((1, D_MODEL), lambda i: (0, 0))],
        compiler_params=_cparams(dimension_semantics=("arbitrary",)),
    )(h, target)


def _pad_cols(w, n):
    return jnp.pad(w, [(0, 0)] * (w.ndim - 1) + [(0, n - w.shape[-1])])


def _layer_slab(name, i):
    return i if name.startswith('mlp_') else i // 2


def _prep_matrix(key, raw_any):
    def raw(n):
        r = raw_any(n)
        return r[0][r[1]] if isinstance(r, tuple) else r

    if key in ('ssd_out', 'up', 'down'):
        r = raw_any({'ssd_out': 'ssd_w_out', 'up': 'mlp_w_up', 'down': 'mlp_w_down'}[key])
        return (r[0].astype(BF16), r[1]) if isinstance(r, tuple) else r.astype(BF16)
    if key == 'ssd_in':
        return _pad_cols(raw('ssd_w_in'), SSD_IN_PAD).astype(BF16)
    if key == 'mla_in':
        wi = raw('mla_w_in')
        kpe = jnp.pad(wi[:, MLA_Q_RANK + MLA_KV_RANK:], ((0, 0), (MLA_NOPE, HEAD_SLOT - MLA_QK)))
        return jnp.concatenate(
            [wi[:, MLA_Q_RANK:MLA_Q_RANK + MLA_KV_RANK], kpe, wi[:, :MLA_Q_RANK]], axis=1).astype(BF16)
    if key == 'mla_qb':
        qb = raw('mla_w_q_b').reshape(MLA_Q_RANK, MLA_HEADS, MLA_QK)
        return _pad_cols(qb, HEAD_SLOT).reshape(MLA_Q_RANK, MLA_WIDE).astype(BF16)
    if key == 'mla_kvb':
        kvb = raw('mla_w_kv_b').reshape(MLA_KV_RANK, MLA_HEADS, MLA_NOPE + MLA_V)
        kn = _pad_cols(kvb[:, :, :MLA_NOPE], HEAD_SLOT).reshape(MLA_KV_RANK, MLA_WIDE)
        vv = _pad_cols(kvb[:, :, MLA_NOPE:], HEAD_SLOT).reshape(MLA_KV_RANK, MLA_WIDE)
        return jnp.concatenate([kn, vv], axis=1).astype(BF16)
    assert key == 'mla_out'
    wo = raw('mla_w_out').reshape(MLA_HEADS, MLA_V, D_MODEL)
    return jnp.pad(wo, ((0, 0), (0, HEAD_SLOT - MLA_V), (0, 0))).reshape(MLA_WIDE, D_MODEL).astype(BF16)


class _Matrices:
    def __init__(self):
        self.p = {k: _Slabs(k, self) for k in ('ssd_in', 'ssd_out', 'mla_in', 'mla_qb', 'mla_kvb',
                                               'mla_out', 'up', 'down')}
        self.made = {}

    def matrix(self, key, slab):
        if (key, slab) not in self.made:
            self.made[(key, slab)] = _prep_matrix(key, lambda n: self.raw(n, slab))
        return self.made[(key, slab)]


class _Slabs:
    def __init__(self, key, owner):
        self.key, self.owner = key, owner

    def __getitem__(self, slab):
        return self.owner.matrix(self.key, slab)


class _ReadyWeights(_Matrices):
    def __init__(self, w):
        super().__init__()
        self.w = w

    def raw(self, name, slab):
        return self.w[name][slab]

    def start(self):
        pass

    def rider(self, host):
        return None

    def deliver(self, host, outs):
        assert not outs


class _KeepGrads:
    def __init__(self):
        self.rounds = {}

    def begin(self, r, grads):
        self.rounds[r] = grads
        return None

    def middle(self, r, recv):
        return None

    def finish(self, r, outs):
        assert not outs

    def result(self):
        names = {n for g in self.rounds.values() for n in g}
        return {n: jnp.concatenate([self.rounds[r][n] for r in sorted(self.rounds, reverse=True)
                                    if n in self.rounds[r]], axis=0) for n in names}


def _pad128(v):
    return _pad_cols(v.reshape(1, -1), 128)


def _sqrelu(u):
    r = jnp.maximum(u, 0.0)
    return r * r


def _local_step(x, target, w, big=None, red=None):
    seq = x.shape[0]
    lp = NPAD + N_META + seq
    big = _ReadyWeights(w) if big is None else big
    p = big.p
    h = jnp.concatenate([jnp.zeros((NPAD, D_MODEL), F32), w['meta_tokens'], x], axis=0)
    cos_t, sin_t = _rope_tables(lp)
    rt = _pick(lp, (384, 256, 128))
    saved = []
    big.start()
    for i in range(4):
        j = i // 2
        s = {'h0': h}
        g_mix = w['ln_mix'][i].reshape(1, -1)
        g_mlp = w['ln_mlp'][i].reshape(1, -1)
        if i == 0:
            hn = _rms_fwd(h, g_mix, f"rms_mix_f{i}")
        s['hn'] = hn
        if i % 2 == 0:
            rid = big.rider(f"ssd_in_f{i}")
            zxd = _mm(hn, p['ssd_in'][j], 'nn', name=f"ssd_in_f{i}", rider=rid)
            if rid is not None:
                zxd, *got = zxd
                big.deliver(f"ssd_in_f{i}", got)
            consts = _ssd_consts(w['ssd_conv_w'][j], w['ssd_conv_b'][j].reshape(1, -1),
                                 _pad128(w['ssd_dt_bias'][j]), _pad128(w['ssd_a_log'][j]),
                                 _pad128(w['ssd_d'][j]), w['ssd_norm'][j].reshape(1, -1))
            yg, states, *got = _ssd_fwd(zxd, consts, f"ssd_core_f{i}", rider=big.rider(f"ssd_core_f{i}"))
            big.deliver(f"ssd_core_f{i}", got)
            s.update(zxd=zxd, consts=consts, yg=yg, states=states)
            h, hn2 = _mm(yg, p['ssd_out'][j], 'nn', name=f"ssd_out_f{i}", epi=lambda r, hv: hv + r,
                         extras=(h,), norm_gain=g_mlp)
        else:
            lat = _mm(hn, p['mla_in'][j], 'nn', name=f"mla_in_f{i}")
            kvg = w['mla_kv_a_norm'][j].reshape(1, -1)
            qag = w['mla_q_a_norm'][j].reshape(1, -1)
            kvn, qn = _row_call(lambda _, a, b, c, d: _lat_norm(a, b, c, d),
                                [(lat, MLA_KV_RANK, 0), (lat, MLA_Q_RANK, 1)], [kvg, qag],
                                [(MLA_KV_RANK, BF16), (MLA_Q_RANK, BF16)], n_rows=lp, tile=rt,
                                name=f"mla_latnorm_f{i}")
            q_raw = _mm(qn, p['mla_qb'][j], 'nn', name=f"mla_qb_f{i}")
            kv_raw = _mm(kvn, p['mla_kvb'][j], 'nn', name=f"mla_kvb_f{i}")
            qg = _pad_cols(w['mla_q_norm'][j].reshape(1, -1), HEAD_SLOT)
            kg = _pad_cols(w['mla_k_norm'][j].reshape(1, -1), HEAD_SLOT)

            def prep_fwd(_, qr, kn, kpe, vv, cs, sn, qgv, kgv):
                qq, kk = _qk_prep(qr, kn, kpe, cs, sn, qgv, kgv)
                ones = lax.broadcasted_iota(jnp.int32, vv.shape, 1) % HEAD_SLOT == MLA_V
                return qq, kk, jnp.where(ones, 1.0, vv)

            q, k, v = _row_call(prep_fwd,
                                [(q_raw, MLA_WIDE, 0), (kv_raw, MLA_WIDE, 0), (lat, HEAD_SLOT, 2),
                                 (kv_raw, MLA_WIDE, 1), (cos_t, HEAD_SLOT, 0), (sin_t, HEAD_SLOT, 0)],
                                [qg, kg], [(MLA_WIDE, BF16)] * 3, n_rows=lp, tile=rt,
                                name=f"mla_qkprep_f{i}")
            o, lse, *got = _attn_fwd(q, k, v, f"mla_attn_f{i}", rider=big.rider(f"mla_attn_f{i}"))
            big.deliver(f"mla_attn_f{i}", got)
            s.update(lat=lat, kvg=kvg, qag=qag, kvn=kvn, qn=qn, q_raw=q_raw, kv_raw=kv_raw, qg=qg, kg=kg,
                     q=q, k=k, v=v, o=o, lse=lse)
            h, hn2 = _mm(o, p['mla_out'][j], 'nn', name=f"mla_out_f{i}", epi=lambda r, hv: hv + r,
                         extras=(h,), norm_gain=g_mlp)
        s['h1'] = h
        u = _mm(hn2, p['up'][i], 'nn', name=f"mlp_up_f{i}", out_dtype=BF16)
        if i < 3:
            h, hn = _mm(u, p['down'][i], 'nn', name=f"mlp_down_f{i}", a_fn=_sqrelu, epi=lambda r, hv: hv + r,
                        extras=(h,), norm_gain=w['ln_mix'][i + 1].reshape(1, -1))
        else:
            h = _mm(u, p['down'][i], 'nn', name=f"mlp_down_f{i}", a_fn=_sqrelu,
                    epi=lambda r, hv: hv + r, extras=(h,))
        s.update(hn2=hn2, u=u, g_mix=g_mix, g_mlp=g_mlp)
        saved.append(s)

    dh, loss_row = _loss_head(h, target, "loss_head")

    large = {n for n, _ in BIG}
    g = {k_: [None] * (4 if k_ in ('ln_mix', 'ln_mlp') else 2)
         for k_ in ALL_NAMES if k_ != 'meta_tokens' and k_ not in large}
    red = _KeepGrads() if red is None else red
    rounds, pending = {}, None

    def round_of(nm, i):
        return next(r for r, spec in enumerate(REDUCE_ROUNDS)
                    for n, l0, l1 in spec if n == nm and l0 <= _layer_slab(nm, i) < l1)

    def slabs_in(nm, r):
        return next((l0, l1) for n, l0, l1 in REDUCE_ROUNDS[r] if n == nm)

    swapping = None

    def dw_into(nm, i, a, b, **kw):
        nonlocal swapping, pending
        r = round_of(nm, i)
        (l0, l1), cur = slabs_in(nm, r), rounds.setdefault(r, {})
        stack = (l1 - l0, _layer_slab(nm, i) - l0, cur.get(nm))
        if swapping is None:
            cur[nm] = _mm(a, b, 'tn', stack=stack, **kw)
        else:
            (r0, rider), swapping = swapping, None
            cur[nm], *recv = _mm(a, b, 'tn', stack=stack, rider=rider, **kw)
            pending = (r0, red.middle(r0, recv))

    def put(nm, i, arr):
        rounds.setdefault(round_of(nm, i), {})[nm] = arr[None]

    def hand_over(r):
        nonlocal pending, swapping
        swap = red.begin(r, rounds.pop(r))
        if swap is None:
            pending = (r, None)
        elif r == 0:
            swapping = (r, swap)
        else:
            pending = (r, red.middle(r, _run_rider(swap, f"rs_swap{r}")))

    def host(fn, *args):
        nonlocal pending
        if pending is None or pending[1] is None:
            return fn(*args)
        (r, rider), pending = pending, None
        outs = fn(*args, rider=rider)
        own = len(outs) - len(rider.out_shapes)
        red.finish(r, outs[own:])
        return outs[:own]

    for i in reversed(range(4)):
        j = i // 2
        s = saved[i]
        dw_into('mlp_w_down', i, s['u'], dh, name=f"mlp_down_dw{i}", a_fn=_sqrelu)
        du = _mm(dh, p['down'][i], 'nt', name=f"mlp_down_dx{i}", out_dtype=BF16,
                 epi=lambda r, uv: r * (2.0 * jnp.maximum(uv, 0.0)), extras=(s['u'],))
        dw_into('mlp_w_up', i, s['hn2'], du, name=f"mlp_up_dw{i}")
        dh, dg = _mm_rms_bwd(du, p['up'][i], s['h1'], dh, s['g_mlp'], f"mlp_up_dx{i}")
        g['ln_mlp'][i] = dg[0]
        if i % 2 == 0:
            dw_into('ssd_w_out', i, s['yg'], dh, name=f"ssd_out_dw{i}")
            d_yg = _mm(dh, p['ssd_out'][j], 'nt', name=f"ssd_out_dx{i}")
            if i == 0:
                hand_over(1)
            d_zxd, dcw, dcb, ddtb, dalog, ddsk, dng = host(_ssd_bwd, s['zxd'], s['states'], d_yg, s['consts'],
                                                           f"ssd_core_b{i}")
            g['ssd_conv_w'][j], g['ssd_conv_b'][j], g['ssd_norm'][j] = dcw, dcb[0], dng[0]
            g['ssd_dt_bias'][j], g['ssd_a_log'][j], g['ssd_d'][j] = (
                ddtb[0, :SSD_HEADS], dalog[0, :SSD_HEADS], ddsk[0, :SSD_HEADS])
            dw_into('ssd_w_in', i, s['hn'], d_zxd, name=f"ssd_in_dw{i}")
            dh, dg = _mm_rms_bwd(d_zxd, p['ssd_in'][j], s['h0'], dh, s['g_mix'], f"ssd_in_dx{i}")
        else:
            wo = _mm(s['o'], dh, 'tn', name=f"mla_out_dw{i}")
            put('mla_w_out', i, wo.reshape(MLA_HEADS, HEAD_SLOT, D_MODEL)[:, :MLA_V].reshape(-1, D_MODEL))
            dob, delta = _mm_attn_do(dh, p['mla_out'][j], s['o'], f"mla_out_dx{i}")
            dq, dk, dv = host(_attn_bwd, s['q'], s['k'], s['v'], dob, s['lse'], delta, f"mla_attn_b{i}")

            def prep_bwd(_, qr, kn, kpe, cs, sn, dqv, dkv, dvv, qgv, kgv):
                _, vjp = jax.vjp(lambda a, b, c, d, e: _qk_prep(a, b, c, cs, sn, d, e), qr, kn, kpe, qgv, kgv)
                d_qr, d_kn, d_kpe, d_qg, d_kg = vjp((dqv, dkv))
                return d_qr, jnp.concatenate([d_kn, dvv], axis=1), d_kpe, d_qg, d_kg

            d_qraw, d_kvraw, d_kpe, d_qg, d_kg = _row_call(
                prep_bwd,
                [(s['q_raw'], MLA_WIDE, 0), (s['kv_raw'], MLA_WIDE, 0), (s['lat'], HEAD_SLOT, 2),
                 (cos_t, HEAD_SLOT, 0), (sin_t, HEAD_SLOT, 0), (dq, MLA_WIDE, 0), (dk, MLA_WIDE, 0),
                 (dv, MLA_WIDE, 0)],
                [s['qg'], s['kg']], [(MLA_WIDE, BF16), (2 * MLA_WIDE, BF16), (HEAD_SLOT, F32)],
                [(1, HEAD_SLOT), (1, HEAD_SLOT)], n_rows=lp, tile=_pick(lp, (128,)), name=f"mla_qkprep_b{i}")
            g['mla_q_norm'][j], g['mla_k_norm'][j] = d_qg[0, :MLA_QK], d_kg[0, :MLA_QK]
            wqb = _mm(s['qn'], d_qraw, 'tn', name=f"mla_qb_dw{i}")
            put('mla_w_q_b', i, wqb.reshape(MLA_Q_RANK, MLA_HEADS, HEAD_SLOT)[:, :, :MLA_QK].reshape(MLA_Q_RANK, -1))
            d_qn = _mm(d_qraw, p['mla_qb'][j], 'nt', name=f"mla_qb_dx{i}")
            wkvb = _mm(s['kvn'], d_kvraw, 'tn', name=f"mla_kvb_dw{i}").reshape(MLA_KV_RANK, 2, MLA_HEADS, HEAD_SLOT)
            put('mla_w_kv_b', i, jnp.concatenate([wkvb[:, 0, :, :MLA_NOPE], wkvb[:, 1, :, :MLA_V]],
                                                 axis=-1).reshape(MLA_KV_RANK, -1))
            d_kvn = _mm(d_kvraw, p['mla_kvb'][j], 'nt', name=f"mla_kvb_dx{i}")

            def lat_bwd(_, kvl, ql, dkvn, dqn, dkpe, kvgv, qagv):
                _, vjp = jax.vjp(_lat_norm, kvl, ql, kvgv, qagv)
                d_kvl, d_ql, d_kvg, d_qag = vjp((dkvn, dqn))
                return jnp.concatenate([d_kvl, dkpe, d_ql], axis=1), d_kvg, d_qag

            d_lat, d_kvg, d_qag = _row_call(
                lat_bwd, [(s['lat'], MLA_KV_RANK, 0), (s['lat'], MLA_Q_RANK, 1), (d_kvn, MLA_KV_RANK, 0),
                          (d_qn, MLA_Q_RANK, 0), (d_kpe, HEAD_SLOT, 0)],
                [s['kvg'], s['qag']], [(LAT_PAD, BF16)], [(1, MLA_KV_RANK), (1, MLA_Q_RANK)],
                n_rows=lp, tile=rt, name=f"mla_latnorm_b{i}")
            g['mla_kv_a_norm'][j], g['mla_q_a_norm'][j] = d_kvg[0], d_qag[0]
            win = _mm(s['hn'], d_lat, 'tn', name=f"mla_in_dw{i}")
            put('mla_w_in', i, jnp.concatenate(
                [win[:, MLA_KV_RANK + HEAD_SLOT:], win[:, :MLA_KV_RANK],
                 win[:, MLA_KV_RANK + MLA_NOPE:MLA_KV_RANK + MLA_QK]], axis=1))
            dh, dg = _mm_rms_bwd(d_lat, p['mla_in'][j], s['h0'], dh, s['g_mix'], f"mla_in_dx{i}")
        g['ln_mix'][i] = dg[0]
        if i == 2:
            hand_over(0)
    hand_over(2)

    if pending[1] is not None:
        red.finish(pending[0], _run_rider(pending[1], "rs_exchange_last"))
    grads = {k_: jnp.stack(v_) for k_, v_ in g.items()}
    grads['meta_tokens'] = dh[NPAD:NPAD + N_META]
    return loss_row, dh[NPAD + N_META:], grads, red


def _all_gather8(shard, name):
    m_per, n = shard.shape

    def body(x_ref, out_ref, send_sems, recv_sems, local_sem):
        x, y, c = lax.axis_index("x"), lax.axis_index("y"), lax.axis_index("c")
        me, sibling = (x, y, c), (x, y, 1 - c)
        chips = [(1 - x, y), (x, 1 - y), (1 - x, 1 - y)]

        def rows(px, py, pc):
            return out_ref.at[pl.ds((4 * px + 2 * py + pc) * m_per, m_per), :]

        def copy(k, block, to, src=None):
            return pltpu.make_async_remote_copy(
                src_ref=rows(*block) if src is None else src, dst_ref=rows(*block),
                send_sem=send_sems.at[k], recv_sem=recv_sems.at[k], device_id=to, device_id_type=MESH)

        mine = pltpu.make_async_copy(x_ref, rows(*me), local_sem)
        mine.start()
        first = [copy(0, me, sibling, src=x_ref)]
        first += [copy(1 + j, me, (*chip, c), src=x_ref) for j, chip in enumerate(chips)]
        for cp in first:
            cp.start()
        passed = [copy(4 + j, (*chip, c), sibling) for j, chip in enumerate(chips)]
        for j, chip in enumerate(chips):
            copy(1 + j, (*chip, c), me).wait_recv()
            passed[j].start()
        copy(0, sibling, me).wait_recv()
        for j, chip in enumerate(chips):
            copy(4 + j, (*chip, 1 - c), me).wait_recv()
        for cp in first + passed:
            cp.wait_send()
        mine.wait()

    return pl.pallas_call(
        body, name=name,
        out_shape=jax.ShapeDtypeStruct((8 * m_per, n), shard.dtype),
        in_specs=[pl.BlockSpec(memory_space=pl.ANY)],
        out_specs=pl.BlockSpec(memory_space=pl.ANY),
        scratch_shapes=[pltpu.SemaphoreType.DMA((7,)), pltpu.SemaphoreType.DMA((7,)), pltpu.SemaphoreType.DMA],
    )(shard)


def _mesh_pos():
    return lax.axis_index("x"), lax.axis_index("y"), lax.axis_index("c")


def _half_rows(pc, h):
    return pl.ds(pl.multiple_of(pc * h, 16), h)


def _whole_view(ref, kind, shard_shape, k, pc):
    _, r, c = shard_shape
    rows = _half_rows(pc, r // 2)
    if kind == 'row':
        return ref.at[:, k, rows, :]
    if kind == 'col':
        return ref.at[:, rows, pl.ds(pl.multiple_of(k * c, 128), c)]
    return ref.at[k, :, rows, :]


def _whole_shape(kind, shard_shape, rows=None):
    l, r, c = shard_shape
    r = r if rows is None else rows
    return {'row': (l, 4, r, c), 'col': (l, r, 4 * c), 'colx': (4, l, r, c)}[kind]


def _gather_rider(shards, kinds):
    n = len(shards)
    shapes = [s.shape for s in shards]

    def plan(ins, outs, sems):
        send_sems, recv_sems, local_sems = sems
        x, y, c = _mesh_pos()
        me, sibling = (x, y, c), (x, y, 1 - c)
        chips = [(1 - x, y), (x, 1 - y), (1 - x, 1 - y)]

        def place(a, px, py, pc):
            return _whole_view(outs[a], kinds[a], shapes[a], 2 * px + py, pc)

        def own(a):
            return ins[a].at[:, _half_rows(c, shapes[a][1] // 2), :]

        def copy(a, k, block, to, src=None):
            return pltpu.make_async_remote_copy(
                src_ref=place(a, *block) if src is None else src, dst_ref=place(a, *block),
                send_sem=send_sems.at[7 * a + k], recv_sem=recv_sems.at[7 * a + k],
                device_id=to, device_id_type=MESH)

        mine = [pltpu.make_async_copy(own(a), place(a, *me), local_sems.at[a]) for a in range(n)]
        first = [copy(a, 1 + j, me, (*chip, c), src=own(a)) for j, chip in enumerate(chips) for a in range(n)]
        first += [copy(a, 0, me, sibling, src=own(a)) for a in range(n)]
        return copy, mine, first, chips, me, sibling, c

    def start(ins, outs, sems):
        _, mine, first, *_ = plan(ins, outs, sems)
        for cp in first + mine:
            cp.start()

    def finish(ins, outs, sems):
        copy, mine, first, chips, me, sibling, c = plan(ins, outs, sems)
        passed = []
        for j, chip in enumerate(chips):
            for a in range(n):
                copy(a, 1 + j, (*chip, c), me).wait_recv()
                passed.append(copy(a, 4 + j, (*chip, c), sibling))
                passed[-1].start()
        for a in range(n):
            copy(a, 0, sibling, me).wait_recv()
        for j, chip in enumerate(chips):
            for a in range(n):
                copy(a, 4 + j, (*chip, 1 - c), me).wait_recv()
        for cp in first + passed:
            cp.wait_send()
        for cp in mine:
            cp.wait()

    return _Rider(
        shards, [jax.ShapeDtypeStruct(_whole_shape(k, s.shape), s.dtype) for k, s in zip(kinds, shards)],
        [pltpu.SemaphoreType.DMA((7 * n,)), pltpu.SemaphoreType.DMA((7 * n,)), pltpu.SemaphoreType.DMA((n,))],
        start, finish)


def _run_rider(rider, name):
    ni, no = len(rider.operands), len(rider.out_shapes)

    def body(*refs):
        ride = (refs[:ni], refs[ni:ni + no], refs[ni + no:])
        rider.start(*ride)
        rider.finish(*ride)

    return pl.pallas_call(
        body, name=name, out_shape=rider.out_shapes,
        in_specs=[pl.BlockSpec(memory_space=pl.ANY)] * ni,
        out_specs=[pl.BlockSpec(memory_space=pl.ANY)] * no,
        scratch_shapes=rider.scratch,
    )(*rider.operands)


def _swap_rider(wholes, kinds, shapes):
    n = len(wholes)

    def plan(ins, outs, sems):
        send_sems, recv_sems = sems
        x, y, c = _mesh_pos()
        cps = []
        for a in range(n):
            rows = _half_rows(1 - c, shapes[a][1] // 2)
            src = ins[a].at[:, rows, :] if kinds[a] == 'col' else ins[a].at[:, :, rows, :]
            cps.append(pltpu.make_async_remote_copy(
                src_ref=src, dst_ref=outs[a], send_sem=send_sems.at[a], recv_sem=recv_sems.at[a],
                device_id=(x, y, 1 - c), device_id_type=MESH))
        return cps

    def start(ins, outs, sems):
        for cp in plan(ins, outs, sems):
            cp.start()

    def finish(ins, outs, sems):
        for cp in plan(ins, outs, sems):
            cp.wait()

    return _Rider(
        wholes, [jax.ShapeDtypeStruct(_whole_shape(k, s, s[1] // 2), w.dtype)
                 for k, s, w in zip(kinds, shapes, wholes)],
        [pltpu.SemaphoreType.DMA((n,)), pltpu.SemaphoreType.DMA((n,))], start, finish)


def _exchange_rider(parts, kinds, shapes):
    n = len(parts)

    def plan(ins, outs, sems):
        send_sems, recv_sems, local_sems = sems
        x, y, c = _mesh_pos()
        kme = 2 * x + y
        chips = [(1 - x, y), (x, 1 - y), (1 - x, 1 - y)]

        def slab(a, k):
            if kinds[a] == 'row':
                return ins[a].at[:, k]
            if kinds[a] == 'col':
                cw = shapes[a][2]
                return ins[a].at[:, :, pl.ds(pl.multiple_of(k * cw, 128), cw)]
            return ins[a].at[k]

        cps = [pltpu.make_async_remote_copy(
            src_ref=slab(a, 2 * px + py), dst_ref=outs[a].at[kme], send_sem=send_sems.at[3 * a + j],
            recv_sem=recv_sems.at[3 * a + j], device_id=(px, py, c), device_id_type=MESH)
            for j, (px, py) in enumerate(chips) for a in range(n)]
        return cps + [pltpu.make_async_copy(slab(a, kme), outs[a].at[kme], local_sems.at[a]) for a in range(n)]

    def start(ins, outs, sems):
        for cp in plan(ins, outs, sems):
            cp.start()

    def finish(ins, outs, sems):
        for cp in plan(ins, outs, sems):
            cp.wait()

    return _Rider(
        parts, [jax.ShapeDtypeStruct((4, s[0], s[1] // 2, s[2]), p.dtype) for s, p in zip(shapes, parts)],
        [pltpu.SemaphoreType.DMA((3 * n,)), pltpu.SemaphoreType.DMA((3 * n,)), pltpu.SemaphoreType.DMA((n,))],
        start, finish)


def _rs_share(shards, slabs, name):
    n = len(shards)

    def body(*refs):
        outs = refs[n:2 * n]
        send_sems, recv_sems = refs[2 * n:]
        x, y, c = _mesh_pos()
        cps = []
        for a in range(n):
            l0, l1 = slabs[a]
            rows = outs[a].at[pl.ds(l0, l1 - l0), _half_rows(c, shards[a].shape[1] // 2), :]
            cps.append(pltpu.make_async_remote_copy(
                src_ref=rows, dst_ref=rows, send_sem=send_sems.at[a], recv_sem=recv_sems.at[a],
                device_id=(x, y, 1 - c), device_id_type=MESH))
        for cp in cps:
            cp.start()
        for cp in cps:
            cp.wait()

    return pl.pallas_call(
        body, name=name,
        out_shape=[jax.ShapeDtypeStruct(s.shape, s.dtype) for s in shards],
        in_specs=[pl.BlockSpec(memory_space=pl.ANY)] * n,
        out_specs=[pl.BlockSpec(memory_space=pl.ANY)] * n,
        input_output_aliases={a: a for a in range(n)},
        scratch_shapes=[pltpu.SemaphoreType.DMA((n,)), pltpu.SemaphoreType.DMA((n,))],
    )(*shards)


def _tile_rows(rows, cols, budget=2 * 1024 * 1024):
    for t in (1024, 512, 256, 128, 64, 32, 16, 8):
        if rows % t == 0 and t * cols * 4 <= budget:
            return t
    return rows


def _add_half(g3, r3, c_idx, name):
    a, h, n = r3.shape
    t = _tile_rows(h, n)
    nt = h // t

    def body(c_ref, g_ref, r_ref, o_ref):
        o_ref[...] = (g_ref[...] + r_ref[...]).astype(o_ref.dtype)

    return pl.pallas_call(
        body, name=name, out_shape=jax.ShapeDtypeStruct((a, h, n), BF16),
        grid_spec=pltpu.PrefetchScalarGridSpec(
            num_scalar_prefetch=1, grid=(a, nt),
            in_specs=[pl.BlockSpec((1, t, n), lambda k, i, c: (k, c[0] * nt + i, 0)),
                      pl.BlockSpec((1, t, n), lambda k, i, c: (k, i, 0))],
            out_specs=pl.BlockSpec((1, t, n), lambda k, i, c: (k, i, 0))),
        compiler_params=_cparams(dimension_semantics=("parallel", "parallel")),
    )(c_idx, g3, r3)


def _sum4(parts, c_idx, name, into):
    _, l, h, n = parts.shape
    n_slabs, l0, buf = into
    t = _tile_rows(h, n, 1024 * 1024)
    nt = h // t
    held = () if buf is None else (buf,)

    def body(c_ref, p_ref, *rest):
        pv = p_ref[...].astype(F32)
        rest[-1][...] = ((pv[0] + pv[1]) + pv[2]) + pv[3]

    return pl.pallas_call(
        body, name=name, out_shape=jax.ShapeDtypeStruct((n_slabs, 2 * h, n), F32),
        grid_spec=pltpu.PrefetchScalarGridSpec(
            num_scalar_prefetch=1, grid=(l, nt),
            in_specs=[pl.BlockSpec((4, 1, t, n), lambda k, i, c: (0, k, i, 0))]
            + [pl.BlockSpec(memory_space=pl.ANY)] * len(held),
            out_specs=pl.BlockSpec((1, t, n), lambda k, i, c: (l0 + k, c[0] * nt + i, 0))),
        input_output_aliases={2: 0} if held else {},
        compiler_params=_cparams(dimension_semantics=("parallel", "parallel")),
    )(c_idx, parts, *held)


def _sum8(parts, name):
    _, m, n = parts.shape

    def body(p_ref, o_ref):
        acc = p_ref[0]
        for d in range(1, 8):
            acc = acc + p_ref[d]
        o_ref[...] = acc

    return pl.pallas_call(body, name=name, out_shape=jax.ShapeDtypeStruct((m, n), F32))(parts)


def _adamw(wp, gp, mp, vp, name):
    r, n = wp.shape
    t = _tile_rows(r, n, 1024 * 1024)

    def body(w_ref, g_ref, m_ref, v_ref, d_ref, mo_ref, vo_ref):
        gv = g_ref[...]
        m2 = ADAM_B1 * m_ref[...] + (1.0 - ADAM_B1) * gv
        v2 = ADAM_B2 * v_ref[...] + (1.0 - ADAM_B2) * (gv * gv)
        m_hat = m2 / (1.0 - ADAM_B1 ** ADAM_STEP)
        v_hat = v2 / (1.0 - ADAM_B2 ** ADAM_STEP)
        d_ref[...] = -ADAM_LR * (m_hat / (jnp.sqrt(v_hat) + ADAM_EPS) + ADAM_WD * w_ref[...])
        mo_ref[...] = m2
        vo_ref[...] = v2

    spec = pl.BlockSpec((t, n), lambda i: (i, 0))
    return pl.pallas_call(
        body, name=name, out_shape=[jax.ShapeDtypeStruct((r, n), F32)] * 3, grid=(r // t,),
        in_specs=[spec] * 4, out_specs=[spec] * 3,
        compiler_params=_cparams(dimension_semantics=("parallel",)),
    )(wp, gp, mp, vp)


BIG = (('ssd_w_in', 'colx'), ('ssd_w_out', 'row'), ('mla_w_in', 'row'), ('mla_w_q_b', 'col'),
       ('mla_w_kv_b', 'col'), ('mla_w_out', 'row'), ('mlp_w_up', 'col'), ('mlp_w_down', 'row'))
SMALL_SHARDED = (('meta_tokens', 1), ('ssd_conv_w', 2), ('mla_q_a_norm', 1), ('mla_kv_a_norm', 1))
SMALL_REPL = ('ln_mix', 'ln_mlp', 'ssd_conv_b', 'ssd_dt_bias', 'ssd_a_log', 'ssd_d', 'ssd_norm',
              'mla_q_norm', 'mla_k_norm')
ALL_NAMES = ('meta_tokens', 'ln_mix', 'ln_mlp', 'ssd_w_in', 'ssd_conv_w', 'ssd_conv_b', 'ssd_dt_bias',
             'ssd_a_log', 'ssd_d', 'ssd_norm', 'ssd_w_out', 'mla_w_in', 'mla_q_a_norm', 'mla_w_q_b',
             'mla_kv_a_norm', 'mla_w_kv_b', 'mla_q_norm', 'mla_k_norm', 'mla_w_out', 'mlp_w_up', 'mlp_w_down')


_MLA_BIG = ('mla_w_in', 'mla_w_q_b', 'mla_w_kv_b', 'mla_w_out')
GATHER_ROUNDS = (
    (('ssd_w_in', 0, 1),),
    (('ssd_w_out', 0, 1), ('mlp_w_up', 0, 1)),
    (('mlp_w_down', 0, 1),) + tuple((n, 0, 1) for n in _MLA_BIG) + (('mlp_w_up', 1, 2), ('mlp_w_down', 1, 2)),
    (('ssd_w_in', 1, 2), ('ssd_w_out', 1, 2)) + tuple((n, 1, 2) for n in _MLA_BIG)
    + (('mlp_w_up', 2, 4), ('mlp_w_down', 2, 4)),
)
GATHER_HOSTS = {'ssd_in_f0': 1, 'ssd_core_f0': 2, 'mla_attn_f1': 3}


REDUCE_ROUNDS = (
    GATHER_ROUNDS[3],
    tuple((n, 0, 1) for n in _MLA_BIG) + (('mlp_w_up', 0, 2), ('mlp_w_down', 0, 2), ('ssd_w_out', 0, 1)),
    (('ssd_w_in', 0, 1),),
)


class _GatheredWeights(_Matrices):
    def __init__(self, shards):
        super().__init__()
        self.shards, self.whole = shards, {}

    def raw(self, name, slab):
        return self.whole[(name, slab)]

    def _round(self, r):
        spec = GATHER_ROUNDS[r]
        return _gather_rider([self.shards[n][l0:l1].astype(BF16) for n, l0, l1 in spec],
                             [dict(BIG)[n] for n, _, _ in spec])

    def _take(self, r, outs):
        for (n, l0, l1), o in zip(GATHER_ROUNDS[r], outs):
            kind = dict(BIG)[n]
            if kind == 'row':
                o = o.reshape(o.shape[0], -1, o.shape[-1])
            for l in range(l0, l1):
                if kind == 'colx':
                    self.whole[(n, l)] = jnp.concatenate([o[k, l - l0] for k in range(4)], axis=-1)
                else:
                    self.whole[(n, l)] = (o, l - l0)

    def start(self):
        self._take(0, _run_rider(self._round(0), "gather_first"))

    def rider(self, host):
        return self._round(GATHER_HOSTS[host]) if host in GATHER_HOSTS else None

    def deliver(self, host, outs):
        if host in GATHER_HOSTS:
            self._take(GATHER_HOSTS[host], outs)


class _ScatterGrads:
    def __init__(self, shard_shapes, c_idx):
        self.shard_shapes, self.c_idx, self.out = shard_shapes, c_idx, {}

    def begin(self, r, grads):
        spec = REDUCE_ROUNDS[r]
        kinds = [dict(BIG)[n] for n, _, _ in spec]
        shapes = [(l1 - l0,) + tuple(self.shard_shapes[n][1:]) for n, l0, l1 in spec]
        wholes = []
        for (n, _, _), kind, s in zip(spec, kinds, shapes):
            if kind == 'row':
                wholes.append(grads[n].reshape(s[0], 4, s[1], s[2]))
            elif kind == 'col':
                wholes.append(grads[n])
            else:
                wholes.append(jnp.stack([grads[n][..., k * s[2]:(k + 1) * s[2]] for k in range(4)]))
        self.swapping = (kinds, shapes, wholes)
        return _swap_rider(wholes, kinds, shapes)

    def middle(self, r, recv):
        spec = REDUCE_ROUNDS[r]
        kinds, shapes, wholes = self.swapping
        parts = []
        for (n, _, _), kind, s, gw, rc in zip(spec, kinds, shapes, wholes, recv):
            if kind == 'col':
                g3, r3 = gw, rc
            else:
                g3, r3 = gw.reshape(-1, s[1], s[2]), rc.reshape(-1, s[1] // 2, s[2])
            parts.append(_add_half(g3, r3, self.c_idx, f"rs_add{r}_{n}").reshape(rc.shape))
        return _exchange_rider(parts, kinds, shapes)

    def finish(self, r, outs):
        spec = REDUCE_ROUNDS[r]
        for (n, l0, _), part in zip(spec, outs):
            self.out[n] = _sum4(part, self.c_idx, f"rs_sum{r}_{n}",
                                into=(self.shard_shapes[n][0], l0, self.out.get(n)))
        shared = _rs_share([self.out[n] for n, _, _ in spec], [(l0, l1) for _, l0, l1 in spec], f"rs_share{r}")
        self.out.update(zip([n for n, _, _ in spec], shared))


def _pack(arrs, rows_mult):
    flat = jnp.concatenate([a.reshape(-1) for a in arrs])
    per = LANES * rows_mult
    pad = (-flat.shape[0]) % per
    if pad:
        flat = jnp.concatenate([flat, jnp.zeros((pad,), flat.dtype)])
    return flat.reshape(-1, LANES)


def _unpack(pack, shapes):
    flat = pack.reshape(-1)
    out, off = [], 0
    for shp in shapes:
        n = math.prod(shp)
        out.append(flat[off:off + n].reshape(shp))
        off += n
    return out


def _split4(full, axis):
    shp = full.shape
    r = full.reshape(shp[:axis] + (4, shp[axis] // 4) + shp[axis + 1:])
    return jnp.moveaxis(r, axis, 0)


def _join4(parts, axis):
    r = jnp.moveaxis(parts, 0, axis)
    shp = r.shape
    return r.reshape(shp[:axis] + (shp[axis] * shp[axis + 1],) + shp[axis + 2:])


def _gather_params(shards, table, dtype, c, name):
    pack = _pack([shards[n].astype(dtype) for n, _ in table], 16)
    half = pack.shape[0] // 2
    mine = lax.dynamic_slice_in_dim(pack, c * half, half, axis=0)
    full = _all_gather8(mine, name).reshape(4, -1)
    out, off = {}, 0
    for n, ax in table:
        cnt = math.prod(shards[n].shape)
        out[n] = _join4(full[:, off:off + cnt].reshape((4,) + shards[n].shape), ax)
        off += cnt
    return out


def kernel(x, meta_tokens, ln_mix, ln_mlp, ssd_w_in, ssd_conv_w, ssd_conv_b, ssd_dt_bias, ssd_a_log, ssd_d, ssd_norm, ssd_w_out, mla_w_in, mla_q_a_norm, mla_w_q_b, mla_kv_a_norm, mla_w_kv_b, mla_q_norm, mla_k_norm, mla_w_out, mlp_w_up, mlp_w_down, loss_target, m_meta_tokens, m_ln_mix, m_ln_mlp, m_ssd_w_in, m_ssd_conv_w, m_ssd_conv_b, m_ssd_dt_bias, m_ssd_a_log, m_ssd_d, m_ssd_norm, m_ssd_w_out, m_mla_w_in, m_mla_q_a_norm, m_mla_w_q_b, m_mla_kv_a_norm, m_mla_w_kv_b, m_mla_q_norm, m_mla_k_norm, m_mla_w_out, m_mlp_w_up, m_mlp_w_down, v_meta_tokens, v_ln_mix, v_ln_mlp, v_ssd_w_in, v_ssd_conv_w, v_ssd_conv_b, v_ssd_dt_bias, v_ssd_a_log, v_ssd_d, v_ssd_norm, v_ssd_w_out, v_mla_w_in, v_mla_q_a_norm, v_mla_w_q_b, v_mla_kv_a_norm, v_mla_w_kv_b, v_mla_q_norm, v_mla_k_norm, v_mla_w_out, v_mlp_w_up, v_mlp_w_down):
    w_sh = dict(meta_tokens=meta_tokens, ln_mix=ln_mix, ln_mlp=ln_mlp, ssd_w_in=ssd_w_in, ssd_conv_w=ssd_conv_w, ssd_conv_b=ssd_conv_b, ssd_dt_bias=ssd_dt_bias, ssd_a_log=ssd_a_log, ssd_d=ssd_d, ssd_norm=ssd_norm, ssd_w_out=ssd_w_out, mla_w_in=mla_w_in, mla_q_a_norm=mla_q_a_norm, mla_w_q_b=mla_w_q_b, mla_kv_a_norm=mla_kv_a_norm, mla_w_kv_b=mla_w_kv_b, mla_q_norm=mla_q_norm, mla_k_norm=mla_k_norm, mla_w_out=mla_w_out, mlp_w_up=mlp_w_up, mlp_w_down=mlp_w_down)
    m_sh = dict(meta_tokens=m_meta_tokens, ln_mix=m_ln_mix, ln_mlp=m_ln_mlp, ssd_w_in=m_ssd_w_in, ssd_conv_w=m_ssd_conv_w, ssd_conv_b=m_ssd_conv_b, ssd_dt_bias=m_ssd_dt_bias, ssd_a_log=m_ssd_a_log, ssd_d=m_ssd_d, ssd_norm=m_ssd_norm, ssd_w_out=m_ssd_w_out, mla_w_in=m_mla_w_in, mla_q_a_norm=m_mla_q_a_norm, mla_w_q_b=m_mla_w_q_b, mla_kv_a_norm=m_mla_kv_a_norm, mla_w_kv_b=m_mla_w_kv_b, mla_q_norm=m_mla_q_norm, mla_k_norm=m_mla_k_norm, mla_w_out=m_mla_w_out, mlp_w_up=m_mlp_w_up, mlp_w_down=m_mlp_w_down)
    v_sh = dict(meta_tokens=v_meta_tokens, ln_mix=v_ln_mix, ln_mlp=v_ln_mlp, ssd_w_in=v_ssd_w_in, ssd_conv_w=v_ssd_conv_w, ssd_conv_b=v_ssd_conv_b, ssd_dt_bias=v_ssd_dt_bias, ssd_a_log=v_ssd_a_log, ssd_d=v_ssd_d, ssd_norm=v_ssd_norm, ssd_w_out=v_ssd_w_out, mla_w_in=v_mla_w_in, mla_q_a_norm=v_mla_q_a_norm, mla_w_q_b=v_mla_w_q_b, mla_kv_a_norm=v_mla_kv_a_norm, mla_w_kv_b=v_mla_w_kv_b, mla_q_norm=v_mla_q_norm, mla_k_norm=v_mla_k_norm, mla_w_out=v_mla_w_out, mlp_w_up=v_mlp_w_up, mlp_w_down=v_mlp_w_down)

    cx, cy, cc = lax.axis_index("x"), lax.axis_index("y"), lax.axis_index("c")
    chip = 2 * cx + cy

    c_idx = cc.reshape(1).astype(jnp.int32)
    big_names = [n for n, _ in BIG]
    shapes = [w_sh[n].shape for n in big_names]

    w = {n: w_sh[n] for n in SMALL_REPL}
    w.update(_gather_params(w_sh, SMALL_SHARDED, F32, cc, "gather_small"))
    big = _GatheredWeights({n: w_sh[n] for n in big_names})
    red = _ScatterGrads({n: w_sh[n].shape for n in big_names}, c_idx)

    loss_row, grad_x, grads, red = _local_step(x[0], loss_target[0], w, big, red)
    loss = lax.psum(jnp.sum(loss_row), ("x", "y", "c"))
    g_sh = dict(red.out)

    small_names = tuple(n for n, _ in SMALL_SHARDED) + SMALL_REPL
    sp = _pack([grads[n] for n in small_names], 8)
    srows = sp.shape[0]
    s_all = _sum8(_all_gather8(sp, "ar_small_gather").reshape(8, srows, LANES), "ar_small_sum")
    s_full = dict(zip(small_names, _unpack(s_all, [grads[n].shape for n in small_names])))
    for n, ax in SMALL_SHARDED:
        g_sh[n] = lax.dynamic_index_in_dim(_split4(s_full[n], ax), chip, axis=0, keepdims=False)
    for n in SMALL_REPL:
        g_sh[n] = s_full[n]

    delta, new_m, new_v = {}, {}, {}
    for n, s in zip(big_names, shapes):
        res = _adamw(*[t[n].reshape(-1, s[2]) for t in (w_sh, g_sh, m_sh, v_sh)], f"adamw_{n}")
        delta[n], new_m[n], new_v[n] = [r.reshape(s) for r in res]
    d_s, m_s, v_s = _adamw(*[_pack([t[n] for n in small_names], 8) for t in (w_sh, g_sh, m_sh, v_sh)],
                           "adamw_small")
    for dst, ps in ((delta, d_s), (new_m, m_s), (new_v, v_s)):
        dst.update(zip(small_names, _unpack(ps, [w_sh[n].shape for n in small_names])))

    return (loss, grad_x[None], *[g_sh[n] for n in ALL_NAMES], *[delta[n] for n in ALL_NAMES],
            *[new_m[n] for n in ALL_NAMES], *[new_v[n] for n in ALL_NAMES])
```
